```python
import math
import jax, jax.numpy as jnp
from jax import lax
import numpy as np

D_MODEL = 2048
BATCH = 8
SEQ = 4096
DEPTH = 2

CHUNK = 64
Q_BLOCK = 128
MLA_HEADS = 8
MLA_Q_LORA = 512
MLA_KV_LORA = 256
MLA_NOPE_DIM = 128
MLA_ROPE_DIM = 64
MLA_V_DIM = 128
RET_HEADS = 4
RET_QK_DIM = 256
RET_V_DIM = 256
MIX_WIDTH = MLA_HEADS * MLA_V_DIM + RET_HEADS * RET_V_DIM
D_FF = -(-(8 * D_MODEL) // (3 * 256)) * 256
ROPE_THETA = 10000.0
LN_EPS = 1e-5
RMS_EPS = 1e-6
GN_EPS = 1e-5
ALPHA = (2 * DEPTH) ** 0.25
BETA = (8 * DEPTH) ** -0.25
IN_SIZES = (MLA_Q_LORA, MLA_KV_LORA, MLA_ROPE_DIM,
            RET_HEADS * RET_QK_DIM, RET_HEADS * RET_QK_DIM,
            RET_HEADS * RET_V_DIM, RET_HEADS * RET_V_DIM)
D_IN = sum(IN_SIZES)

kernel_name = "hybrid_mla_retention_deepnorm"


def layer_norm(x, g, b):
    xf = x.astype(jnp.float32)
    mu = xf.mean(-1, keepdims=True)
    var = jnp.square(xf - mu).mean(-1, keepdims=True)
    return ((xf - mu) * lax.rsqrt(var + LN_EPS) * g + b).astype(x.dtype)


def rms_norm(x, g):
    xf = x.astype(jnp.float32)
    return (xf * lax.rsqrt(jnp.square(xf).mean(-1, keepdims=True) + RMS_EPS) * g).astype(x.dtype)


def rope_tables(positions, dim):
    inv_freq = ROPE_THETA ** (-jnp.arange(0, dim, 2, dtype=jnp.float32) / dim)
    ang = positions.astype(jnp.float32)[..., None] * inv_freq
    return jnp.cos(ang), jnp.sin(ang)


def apply_rope(t, cos, sin):
    tf = t.astype(jnp.float32)
    half = t.shape[-1] // 2
    t1, t2 = tf[..., :half], tf[..., half:]
    c, s = cos[:, :, None, :], sin[:, :, None, :]
    return jnp.concatenate([t1 * c - t2 * s, t2 * c + t1 * s], axis=-1).astype(t.dtype)


def split_columns(h):
    parts, start = [], 0
    for size in IN_SIZES:
        parts.append(h[..., start:start + size])
        start += size
    return parts


def mla_group(c_q, c_kv, k_rope, cos, sin, q_norm_g, kv_norm_g, w_uq, w_ukv):
    B, S, _ = c_q.shape
    H = MLA_HEADS
    q = (rms_norm(c_q, q_norm_g) @ w_uq).reshape(B, S, H, MLA_NOPE_DIM + MLA_ROPE_DIM)
    q_nope = q[..., :MLA_NOPE_DIM]
    q_rope = apply_rope(q[..., MLA_NOPE_DIM:], cos, sin)
    kv = (rms_norm(c_kv, kv_norm_g) @ w_ukv).reshape(B, S, H, MLA_NOPE_DIM + MLA_V_DIM)
    k_nope, v = kv[..., :MLA_NOPE_DIM], kv[..., MLA_NOPE_DIM:]
    k_r = apply_rope(k_rope[:, :, None, :], cos, sin)[:, :, 0, :]
    scale = (MLA_NOPE_DIM + MLA_ROPE_DIM) ** -0.5
    chunk_id = jnp.arange(S) // CHUNK
    neg = jnp.finfo(jnp.float32).min
    outs = []
    for blk in range(S // Q_BLOCK):
        q0 = blk * Q_BLOCK
        kend = q0 + Q_BLOCK
        s = (jnp.einsum('bqhd,bkhd->bhqk', q_nope[:, q0:kend], k_nope[:, :kend])
             + jnp.einsum('bqhr,bkr->bhqk', q_rope[:, q0:kend], k_r[:, :kend]))
        s = s.astype(jnp.float32) * scale
        mask = chunk_id[q0:kend, None] >= chunk_id[None, :kend]
        s = jnp.where(mask[None, None], s, neg)
        p = jax.nn.softmax(s, axis=-1).astype(v.dtype)
        outs.append(jnp.einsum('bhqk,bkhd->bqhd', p, v[:, :kend]))
    o = jnp.concatenate(outs, axis=1)
    return o.reshape(B, S, H * MLA_V_DIM)


def retention_group(rq, rk, rv, rg, cos, sin, gn_g, gn_b):
    B, S, _ = rq.shape
    H, DK, DV, L = RET_HEADS, RET_QK_DIM, RET_V_DIM, CHUNK
    NC = S // L
    f32 = jnp.float32
    q = apply_rope(rq.reshape(B, S, H, DK), cos, sin).astype(f32) * (DK ** -0.5)
    k = apply_rope(rk.reshape(B, S, H, DK), cos, sin).astype(f32)
    v = rv.reshape(B, S, H, DV).astype(f32)
    q = q.reshape(B, NC, L, H, DK)
    k = k.reshape(B, NC, L, H, DK)
    v = v.reshape(B, NC, L, H, DV)
    log_gamma = jnp.log1p(-jnp.exp2(-5.0 - jnp.arange(H, dtype=f32)))
    idx = jnp.arange(L, dtype=f32)
    intra_decay = jnp.exp(log_gamma[:, None, None] * jnp.abs(idx[:, None] - idx[None, :]))
    scores = jnp.einsum('bcnhd,bcmhd->bchnm', q, k) * intra_decay[None, None]
    o_intra = jnp.einsum('bchnm,bcmhe->bcnhe', scores, v)
    q_decay = jnp.exp(log_gamma[:, None] * (idx + 1.0))[None]
    k_decay = jnp.exp(log_gamma[:, None] * (L - 1.0 - idx))
    chunk_decay = jnp.exp(log_gamma * L)
    q_decay = q_decay[0]

    def step(state, inp):
        qc, kc, vc = inp
        o_inter = jnp.einsum('bnhd,hn,bhde->bnhe', qc, q_decay, state)
        state = (state * chunk_decay[None, :, None, None]
                 + jnp.einsum('bmhd,hm,bmhe->bhde', kc, k_decay, vc))
        return state, o_inter

    state0 = jnp.zeros((B, H, DK, DV), f32)
    xs = (q.transpose(1, 0, 2, 3, 4), k.transpose(1, 0, 2, 3, 4), v.transpose(1, 0, 2, 3, 4))
    _, o_inter = lax.scan(step, state0, xs)
    o = (o_intra + o_inter.transpose(1, 0, 2, 3, 4)).reshape(B, S, H, DV)
    mu = o.mean(-1, keepdims=True)
    var = jnp.square(o - mu).mean(-1, keepdims=True)
    o = ((o - mu) * lax.rsqrt(var + GN_EPS)).reshape(B, S, H * DV) * gn_g + gn_b
    o = jax.nn.silu(rg.astype(f32)) * o
    return o.astype(rq.dtype)


def _fwd_setup_inputs(seed: int = 0) -> dict:
    key = jax.random.key(seed)
    ks = list(jax.random.split(key, 24))
    f32 = jnp.float32

    def nrm(k, shape, scale):
        return jax.random.normal(k, shape, f32) * scale

    x = jax.random.normal(ks[0], (BATCH, SEQ, D_MODEL), f32)
    start = jax.random.randint(ks[1], (BATCH, 1), 0, 4096, dtype=jnp.int32)
    positions = (start + jnp.arange(SEQ, dtype=jnp.int32)[None, :]).astype(jnp.int32)
    return {
        "x": x,
        "positions": positions,
        "ln_in_g": 1.0 + nrm(ks[2], (D_MODEL,), 0.02),
        "ln_in_b": nrm(ks[3], (D_MODEL,), 0.02),
        "w_in": nrm(ks[4], (DEPTH, D_MODEL, D_IN), D_MODEL ** -0.5),
        "q_norm_g": 1.0 + nrm(ks[5], (DEPTH, MLA_Q_LORA), 0.02),
        "kv_norm_g": 1.0 + nrm(ks[6], (DEPTH, MLA_KV_LORA), 0.02),
        "w_uq": nrm(ks[7], (DEPTH, MLA_Q_LORA, MLA_HEADS * (MLA_NOPE_DIM + MLA_ROPE_DIM)), MLA_Q_LORA ** -0.5),
        "w_ukv": nrm(ks[8], (DEPTH, MLA_KV_LORA, MLA_HEADS * (MLA_NOPE_DIM + MLA_V_DIM)), MLA_KV_LORA ** -0.5),
        "ret_gn_g": 1.0 + nrm(ks[9], (DEPTH, RET_HEADS * RET_V_DIM), 0.02),
        "ret_gn_b": nrm(ks[10], (DEPTH, RET_HEADS * RET_V_DIM), 0.02),
        "w_out": nrm(ks[11], (DEPTH, MIX_WIDTH, D_MODEL), (MIX_WIDTH ** -0.5) * BETA),
        "ln1_g": 1.0 + nrm(ks[12], (DEPTH, D_MODEL), 0.02),
        "ln1_b": nrm(ks[13], (DEPTH, D_MODEL), 0.02),
        "w_gate": nrm(ks[14], (DEPTH, D_MODEL, D_FF), D_MODEL ** -0.5),
        "w_up": nrm(ks[15], (DEPTH, D_MODEL, D_FF), D_MODEL ** -0.5),
        "w_down": nrm(ks[16], (DEPTH, D_FF, D_MODEL), (D_FF ** -0.5) * BETA),
        "ln2_g": 1.0 + nrm(ks[17], (DEPTH, D_MODEL), 0.02),
        "ln2_b": nrm(ks[18], (DEPTH, D_MODEL), 0.02),
    }


def _fwd_reference(x, positions, ln_in_g, ln_in_b, w_in, q_norm_g, kv_norm_g, w_uq, w_ukv,
              ret_gn_g, ret_gn_b, w_out, ln1_g, ln1_b, w_gate, w_up, w_down, ln2_g, ln2_b):
    cos_m, sin_m = rope_tables(positions, MLA_ROPE_DIM)
    cos_r, sin_r = rope_tables(positions, RET_QK_DIM)
    x = layer_norm(x, ln_in_g, ln_in_b)
    for l in range(DEPTH):
        h = x @ w_in[l]
        c_q, c_kv, k_rope, rq, rk, rv, rg = split_columns(h)
        a = mla_group(c_q, c_kv, k_rope, cos_m, sin_m, q_norm_g[l], kv_norm_g[l], w_uq[l], w_ukv[l])
        r = retention_group(rq, rk, rv, rg, cos_r, sin_r, ret_gn_g[l], ret_gn_b[l])
        mix = jnp.concatenate([a, r], axis=-1) @ w_out[l]
        x = layer_norm(ALPHA * x + mix, ln1_g[l], ln1_b[l])
        f = (jax.nn.silu(x @ w_gate[l]) * (x @ w_up[l])) @ w_down[l]
        x = layer_norm(ALPHA * x + f, ln2_g[l], ln2_b[l])
    return x


import jax as _jax
import jax.numpy as _jnp

TWIN_FORMAT = 'train_step'
FWD_PARAMS = ['x', 'positions', 'ln_in_g', 'ln_in_b', 'w_in', 'q_norm_g', 'kv_norm_g', 'w_uq', 'w_ukv', 'ret_gn_g', 'ret_gn_b', 'w_out', 'ln1_g', 'ln1_b', 'w_gate', 'w_up', 'w_down', 'ln2_g', 'ln2_b']
TWIN_WEIGHTS = ['ln_in_g', 'ln_in_b', 'w_in', 'q_norm_g', 'kv_norm_g', 'w_uq', 'w_ukv', 'ret_gn_g', 'ret_gn_b', 'w_out', 'ln1_g', 'ln1_b', 'w_gate', 'w_up', 'w_down', 'ln2_g', 'ln2_b']
TWIN_DIFF_INPUT = 'x'
TWIN_INPUTS = ['x', 'positions', 'ln_in_g', 'ln_in_b', 'w_in', 'q_norm_g', 'kv_norm_g', 'w_uq', 'w_ukv', 'ret_gn_g', 'ret_gn_b', 'w_out', 'ln1_g', 'ln1_b', 'w_gate', 'w_up', 'w_down', 'ln2_g', 'ln2_b', 'loss_target', 'm_ln_in_g', 'm_ln_in_b', 'm_w_in', 'm_q_norm_g', 'm_kv_norm_g', 'm_w_uq', 'm_w_ukv', 'm_ret_gn_g', 'm_ret_gn_b', 'm_w_out', 'm_ln1_g', 'm_ln1_b', 'm_w_gate', 'm_w_up', 'm_w_down', 'm_ln2_g', 'm_ln2_b', 'v_ln_in_g', 'v_ln_in_b', 'v_w_in', 'v_q_norm_g', 'v_kv_norm_g', 'v_w_uq', 'v_w_ukv', 'v_ret_gn_g', 'v_ret_gn_b', 'v_w_out', 'v_ln1_g', 'v_ln1_b', 'v_w_gate', 'v_w_up', 'v_w_down', 'v_ln2_g', 'v_ln2_b']
TWIN_OUTPUTS = ['loss', 'grad_x', 'grad_ln_in_g', 'grad_ln_in_b', 'grad_w_in', 'grad_q_norm_g', 'grad_kv_norm_g', 'grad_w_uq', 'grad_w_ukv', 'grad_ret_gn_g', 'grad_ret_gn_b', 'grad_w_out', 'grad_ln1_g', 'grad_ln1_b', 'grad_w_gate', 'grad_w_up', 'grad_w_down', 'grad_ln2_g', 'grad_ln2_b', 'delta_ln_in_g', 'delta_ln_in_b', 'delta_w_in', 'delta_q_norm_g', 'delta_kv_norm_g', 'delta_w_uq', 'delta_w_ukv', 'delta_ret_gn_g', 'delta_ret_gn_b', 'delta_w_out', 'delta_ln1_g', 'delta_ln1_b', 'delta_w_gate', 'delta_w_up', 'delta_w_down', 'delta_ln2_g', 'delta_ln2_b', 'new_m_ln_in_g', 'new_m_ln_in_b', 'new_m_w_in', 'new_m_q_norm_g', 'new_m_kv_norm_g', 'new_m_w_uq', 'new_m_w_ukv', 'new_m_ret_gn_g', 'new_m_ret_gn_b', 'new_m_w_out', 'new_m_ln1_g', 'new_m_ln1_b', 'new_m_w_gate', 'new_m_w_up', 'new_m_w_down', 'new_m_ln2_g', 'new_m_ln2_b', 'new_v_ln_in_g', 'new_v_ln_in_b', 'new_v_w_in', 'new_v_q_norm_g', 'new_v_kv_norm_g', 'new_v_w_uq', 'new_v_w_ukv', 'new_v_ret_gn_g', 'new_v_ret_gn_b', 'new_v_w_out', 'new_v_ln1_g', 'new_v_ln1_b', 'new_v_w_gate', 'new_v_w_up', 'new_v_w_down', 'new_v_ln2_g', 'new_v_ln2_b']
TWIN_LEAF_KINDS = {'loss': 'loss', 'grad_x': 'grad_x', 'grad_ln_in_g': 'grad_w', 'grad_ln_in_b': 'grad_w', 'grad_w_in': 'grad_w', 'grad_q_norm_g': 'grad_w', 'grad_kv_norm_g': 'grad_w', 'grad_w_uq': 'grad_w', 'grad_w_ukv': 'grad_w', 'grad_ret_gn_g': 'grad_w', 'grad_ret_gn_b': 'grad_w', 'grad_w_out': 'grad_w', 'grad_ln1_g': 'grad_w', 'grad_ln1_b': 'grad_w', 'grad_w_gate': 'grad_w', 'grad_w_up': 'grad_w', 'grad_w_down': 'grad_w', 'grad_ln2_g': 'grad_w', 'grad_ln2_b': 'grad_w', 'delta_ln_in_g': 'delta_w', 'delta_ln_in_b': 'delta_w', 'delta_w_in': 'delta_w', 'delta_q_norm_g': 'delta_w', 'delta_kv_norm_g': 'delta_w', 'delta_w_uq': 'delta_w', 'delta_w_ukv': 'delta_w', 'delta_ret_gn_g': 'delta_w', 'delta_ret_gn_b': 'delta_w', 'delta_w_out': 'delta_w', 'delta_ln1_g': 'delta_w', 'delta_ln1_b': 'delta_w', 'delta_w_gate': 'delta_w', 'delta_w_up': 'delta_w', 'delta_w_down': 'delta_w', 'delta_ln2_g': 'delta_w', 'delta_ln2_b': 'delta_w', 'new_m_ln_in_g': 'new_m', 'new_m_ln_in_b': 'new_m', 'new_m_w_in': 'new_m', 'new_m_q_norm_g': 'new_m', 'new_m_kv_norm_g': 'new_m', 'new_m_w_uq': 'new_m', 'new_m_w_ukv': 'new_m', 'new_m_ret_gn_g': 'new_m', 'new_m_ret_gn_b': 'new_m', 'new_m_w_out': 'new_m', 'new_m_ln1_g': 'new_m', 'new_m_ln1_b': 'new_m', 'new_m_w_gate': 'new_m', 'new_m_w_up': 'new_m', 'new_m_w_down': 'new_m', 'new_m_ln2_g': 'new_m', 'new_m_ln2_b': 'new_m', 'new_v_ln_in_g': 'new_v', 'new_v_ln_in_b': 'new_v', 'new_v_w_in': 'new_v', 'new_v_q_norm_g': 'new_v', 'new_v_kv_norm_g': 'new_v', 'new_v_w_uq': 'new_v', 'new_v_w_ukv': 'new_v', 'new_v_ret_gn_g': 'new_v', 'new_v_ret_gn_b': 'new_v', 'new_v_w_out': 'new_v', 'new_v_ln1_g': 'new_v', 'new_v_ln1_b': 'new_v', 'new_v_w_gate': 'new_v', 'new_v_w_up': 'new_v', 'new_v_w_down': 'new_v', 'new_v_ln2_g': 'new_v', 'new_v_ln2_b': 'new_v'}


def _forward(args):
    return _fwd_reference(*[args[k] for k in FWD_PARAMS])


def _output_shape():
    def fwd():
        inp = _fwd_setup_inputs(0)
        return _fwd_reference(*[inp[k] for k in FWD_PARAMS])
    out = _jax.eval_shape(fwd)
    return out.shape, out.dtype

N_MICROBATCH = 1
ADAM_LR = 0.001
ADAM_B1 = 0.9
ADAM_B2 = 0.999
ADAM_EPS = 1e-08
ADAM_WD = 0.01
ADAM_STEP = 10
PER_EXAMPLE_BATCH_AXIS = {'x': 0, 'positions': 0, 'loss_target': 0}
SHARED_INPUTS = []
_WEIGHT_DTYPES = {'ln_in_g': _jnp.float32, 'ln_in_b': _jnp.float32, 'w_in': _jnp.float32, 'q_norm_g': _jnp.float32, 'kv_norm_g': _jnp.float32, 'w_uq': _jnp.float32, 'w_ukv': _jnp.float32, 'ret_gn_g': _jnp.float32, 'ret_gn_b': _jnp.float32, 'w_out': _jnp.float32, 'ln1_g': _jnp.float32, 'ln1_b': _jnp.float32, 'w_gate': _jnp.float32, 'w_up': _jnp.float32, 'w_down': _jnp.float32, 'ln2_g': _jnp.float32, 'ln2_b': _jnp.float32}
MOMENT_SCALE = {'ln_in_g': 5.199655e-01, 'ln_in_b': 2.889393e-01, 'w_in': 1.828408e-02, 'q_norm_g': 6.455948e-03, 'kv_norm_g': 1.445740e-02, 'w_uq': 3.885138e-03, 'w_ukv': 4.643539e-03, 'ret_gn_g': 1.963290e-02, 'ret_gn_b': 2.639912e-02, 'w_out': 2.831799e-02, 'ln1_g': 5.511517e-01, 'ln1_b': 2.840230e-01, 'w_gate': 1.169803e-02, 'w_up': 1.134056e-02, 'w_down': 3.759385e-02, 'ln2_g': 1.133381e+01, 'ln2_b': 4.948030e-01}


def _to_microbatches(a, axis):
    t = _jnp.moveaxis(a, axis, 0)
    t = t.reshape((N_MICROBATCH, t.shape[0] // N_MICROBATCH) + t.shape[1:])
    return _jnp.moveaxis(t, 1, axis + 1)


def setup_inputs(seed: int = 0) -> dict:
    inp = _fwd_setup_inputs(seed)
    key = _jax.random.fold_in(_jax.random.key(seed), 7919)
    shape, _ = _output_shape()
    out = dict(inp)
    out["loss_target"] = _jax.random.normal(_jax.random.fold_in(key, 0), shape, _jnp.float32)
    for i, name in enumerate(TWIN_WEIGHTS):
        w = inp[name].astype(_jnp.float32)
        if MOMENT_SCALE is None:
            s = _jnp.sqrt(_jnp.mean(_jnp.square(w)) + 1e-30)
        else:
            s = MOMENT_SCALE[name]
        km, kv = _jax.random.split(_jax.random.fold_in(key, i + 1))
        out[name] = w
        out["m_" + name] = s * _jax.random.normal(km, w.shape, _jnp.float32)
        out["v_" + name] = (s * s) * _jax.random.uniform(kv, w.shape, _jnp.float32, 0.5, 1.5)
    if N_MICROBATCH > 1:
        for name, axis in PER_EXAMPLE_BATCH_AXIS.items():
            out[name] = _to_microbatches(out[name], axis)
    return {'x': out['x'], 'positions': out['positions'], 'ln_in_g': out['ln_in_g'], 'ln_in_b': out['ln_in_b'], 'w_in': out['w_in'], 'q_norm_g': out['q_norm_g'], 'kv_norm_g': out['kv_norm_g'], 'w_uq': out['w_uq'], 'w_ukv': out['w_ukv'], 'ret_gn_g': out['ret_gn_g'], 'ret_gn_b': out['ret_gn_b'], 'w_out': out['w_out'], 'ln1_g': out['ln1_g'], 'ln1_b': out['ln1_b'], 'w_gate': out['w_gate'], 'w_up': out['w_up'], 'w_down': out['w_down'], 'ln2_g': out['ln2_g'], 'ln2_b': out['ln2_b'], 'loss_target': out['loss_target'], 'm_ln_in_g': out['m_ln_in_g'], 'm_ln_in_b': out['m_ln_in_b'], 'm_w_in': out['m_w_in'], 'm_q_norm_g': out['m_q_norm_g'], 'm_kv_norm_g': out['m_kv_norm_g'], 'm_w_uq': out['m_w_uq'], 'm_w_ukv': out['m_w_ukv'], 'm_ret_gn_g': out['m_ret_gn_g'], 'm_ret_gn_b': out['m_ret_gn_b'], 'm_w_out': out['m_w_out'], 'm_ln1_g': out['m_ln1_g'], 'm_ln1_b': out['m_ln1_b'], 'm_w_gate': out['m_w_gate'], 'm_w_up': out['m_w_up'], 'm_w_down': out['m_w_down'], 'm_ln2_g': out['m_ln2_g'], 'm_ln2_b': out['m_ln2_b'], 'v_ln_in_g': out['v_ln_in_g'], 'v_ln_in_b': out['v_ln_in_b'], 'v_w_in': out['v_w_in'], 'v_q_norm_g': out['v_q_norm_g'], 'v_kv_norm_g': out['v_kv_norm_g'], 'v_w_uq': out['v_w_uq'], 'v_w_ukv': out['v_w_ukv'], 'v_ret_gn_g': out['v_ret_gn_g'], 'v_ret_gn_b': out['v_ret_gn_b'], 'v_w_out': out['v_w_out'], 'v_ln1_g': out['v_ln1_g'], 'v_ln1_b': out['v_ln1_b'], 'v_w_gate': out['v_w_gate'], 'v_w_up': out['v_w_up'], 'v_w_down': out['v_w_down'], 'v_ln2_g': out['v_ln2_g'], 'v_ln2_b': out['v_ln2_b']}


def _loss(weights, diff, rest, loss_target):
    with _jax.named_scope("forward"):
        args = {**rest, TWIN_DIFF_INPUT: diff, **{k: w.astype(_WEIGHT_DTYPES[k]) for k, w in weights.items()}}
        y = _forward(args)
    with _jax.named_scope("loss_head"):
        err = _jnp.square(y.astype(_jnp.float32) - loss_target)
        return 0.5 * _jnp.sum(_jnp.mean(err, axis=-1)) if err.ndim else 0.5 * err


def _adamw(w, g, m, v):
    m = ADAM_B1 * m + (1.0 - ADAM_B1) * g
    v = ADAM_B2 * v + (1.0 - ADAM_B2) * _jnp.square(g)
    m_hat = m / (1.0 - ADAM_B1 ** ADAM_STEP)
    v_hat = v / (1.0 - ADAM_B2 ** ADAM_STEP)
    delta = -ADAM_LR * (m_hat / (_jnp.sqrt(v_hat) + ADAM_EPS) + ADAM_WD * w)
    return delta, m, v


def reference(x, positions, ln_in_g, ln_in_b, w_in, q_norm_g, kv_norm_g, w_uq, w_ukv, ret_gn_g, ret_gn_b, w_out, ln1_g, ln1_b, w_gate, w_up, w_down, ln2_g, ln2_b, loss_target, m_ln_in_g, m_ln_in_b, m_w_in, m_q_norm_g, m_kv_norm_g, m_w_uq, m_w_ukv, m_ret_gn_g, m_ret_gn_b, m_w_out, m_ln1_g, m_ln1_b, m_w_gate, m_w_up, m_w_down, m_ln2_g, m_ln2_b, v_ln_in_g, v_ln_in_b, v_w_in, v_q_norm_g, v_kv_norm_g, v_w_uq, v_w_ukv, v_ret_gn_g, v_ret_gn_b, v_w_out, v_ln1_g, v_ln1_b, v_w_gate, v_w_up, v_w_down, v_ln2_g, v_ln2_b):
    given = dict(x=x, positions=positions, ln_in_g=ln_in_g, ln_in_b=ln_in_b, w_in=w_in, q_norm_g=q_norm_g, kv_norm_g=kv_norm_g, w_uq=w_uq, w_ukv=w_ukv, ret_gn_g=ret_gn_g, ret_gn_b=ret_gn_b, w_out=w_out, ln1_g=ln1_g, ln1_b=ln1_b, w_gate=w_gate, w_up=w_up, w_down=w_down, ln2_g=ln2_g, ln2_b=ln2_b, loss_target=loss_target, m_ln_in_g=m_ln_in_g, m_ln_in_b=m_ln_in_b, m_w_in=m_w_in, m_q_norm_g=m_q_norm_g, m_kv_norm_g=m_kv_norm_g, m_w_uq=m_w_uq, m_w_ukv=m_w_ukv, m_ret_gn_g=m_ret_gn_g, m_ret_gn_b=m_ret_gn_b, m_w_out=m_w_out, m_ln1_g=m_ln1_g, m_ln1_b=m_ln1_b, m_w_gate=m_w_gate, m_w_up=m_w_up, m_w_down=m_w_down, m_ln2_g=m_ln2_g, m_ln2_b=m_ln2_b, v_ln_in_g=v_ln_in_g, v_ln_in_b=v_ln_in_b, v_w_in=v_w_in, v_q_norm_g=v_q_norm_g, v_kv_norm_g=v_kv_norm_g, v_w_uq=v_w_uq, v_w_ukv=v_w_ukv, v_ret_gn_g=v_ret_gn_g, v_ret_gn_b=v_ret_gn_b, v_w_out=v_w_out, v_ln1_g=v_ln1_g, v_ln1_b=v_ln1_b, v_w_gate=v_w_gate, v_w_up=v_w_up, v_w_down=v_w_down, v_ln2_g=v_ln2_g, v_ln2_b=v_ln2_b)
    weights = {n: given[n] for n in TWIN_WEIGHTS}
    shared = {n: given[n] for n in SHARED_INPUTS}
    per_example = {n: given[n] for n in ['x', 'positions']}
    grad_fn = _jax.value_and_grad(_loss, argnums=(0, 1))

    def one_microbatch(ex, loss_target):
        ex = dict(ex)
        diff = ex.pop(TWIN_DIFF_INPUT)
        return grad_fn(weights, diff, {**shared, **ex}, loss_target)

    if N_MICROBATCH == 1:
        loss, (grad_w, grad_x) = one_microbatch(per_example, given["loss_target"])
    else:
        def body(carry, xs):
            loss_sum, grad_sum = carry
            l_k, (gw_k, gx_k) = one_microbatch(xs[0], xs[1])
            with _jax.named_scope("update"):
                return (loss_sum + l_k, _jax.tree.map(_jnp.add, grad_sum, gw_k)), gx_k

        init = (_jnp.zeros((), _jnp.float32), _jax.tree.map(_jnp.zeros_like, weights))
        (loss, grad_w), grad_x = _jax.lax.scan(body, init, (per_example, given["loss_target"]))
    with _jax.named_scope("update"):
        delta_w, new_m, new_v = {}, {}, {}
        for n in TWIN_WEIGHTS:
            delta_w[n], new_m[n], new_v[n] = _adamw(weights[n], grad_w[n], given["m_" + n], given["v_" + n])
    return (loss, grad_x, *[grad_w[n] for n in TWIN_WEIGHTS], *[delta_w[n] for n in TWIN_WEIGHTS],
            *[new_m[n] for n in TWIN_WEIGHTS], *[new_v[n] for n in TWIN_WEIGHTS])
```

```python
import functools

import jax
import jax.numpy as jnp
from jax import lax
from jax.experimental import pallas as pl
from jax.experimental.pallas import tpu as pltpu

F32 = jnp.float32
BF16 = jnp.bfloat16

D_MODEL = 2048
DEPTH = 2
CHUNK = 64
MLA_HEADS = 8
Q_LORA = 512
KV_LORA = 256
NOPE = 128
ROPE = 64
VDIM = 128
RET_HEADS = 4
RET_DK = 256
RET_DV = 256
D_FF = 5632
D_IN = 4928
ROPE_THETA = 10000.0
LN_EPS = 1e-5
RMS_EPS = 1e-6
GN_EPS = 1e-5
ALPHA = (2 * DEPTH) ** 0.25
MLA_SCALE = (NOPE + ROPE) ** -0.5
RET_SCALE = RET_DK ** -0.5
ADAM_LR = 0.001
ADAM_B1 = 0.9
ADAM_B2 = 0.999
ADAM_EPS = 1e-08
ADAM_WD = 0.01
ADAM_STEP = 10

LANES = 128
HEAD_PAD = 256
MLA_IN = 1024
MLA_IN_USED = Q_LORA + KV_LORA + ROPE
D_IN_PAD = MLA_IN + 4 * 1024
ATT_BLOCK = 256
NEG = -1e30
VMEM_LIMIT = 56 * 1024 * 1024

N_CHIPS = 4
FLAT_W = 1024
FLAT_ROWS = 12288
BIG = ("w_in", "w_uq", "w_ukv", "w_out", "w_gate", "w_up", "w_down")
BIG_SHARD = {"w_in": (2048, 1232), "w_uq": (512, 384), "w_ukv": (256, 512), "w_out": (512, 2048),
             "w_gate": (2048, 1408), "w_up": (2048, 1408), "w_down": (1408, 2048)}
SMALL = ("ln_in_g", "ln_in_b", "q_norm_g", "kv_norm_g", "ret_gn_g", "ret_gn_b", "ln1_g", "ln1_b", "ln2_g", "ln2_b")
WEIGHTS = ("ln_in_g", "ln_in_b", "w_in", "q_norm_g", "kv_norm_g", "w_uq", "w_ukv", "ret_gn_g", "ret_gn_b", "w_out",
           "ln1_g", "ln1_b", "w_gate", "w_up", "w_down", "ln2_g", "ln2_b")
SMALL_ROWS = 32

MESH = pl.DeviceIdType.MESH


def _pick(dim, cands):
    for c in cands:
        if dim % c == 0:
            return c
    return dim


def _call(body, name, out_shape, grid, in_specs, out_specs, scratch=(), sem=None):
    return pl.pallas_call(
        body, name=name, out_shape=out_shape, grid=grid, in_specs=in_specs, out_specs=out_specs,
        scratch_shapes=list(scratch),
        compiler_params=pltpu.CompilerParams(dimension_semantics=sem, vmem_limit_bytes=VMEM_LIMIT))


def _rows(tm, w, col=0):
    return pl.BlockSpec((tm, w), lambda i: (i, col))


def _whole(shape):
    return pl.BlockSpec(shape, lambda i: (0,) * len(shape))


def _sds(shape, dtype):
    return jax.ShapeDtypeStruct(shape, dtype)


def _matmul(a, b, name, ta=False, tb=False, out_dtype=F32):
    (K, M) = a.shape if ta else a.shape[::-1]
    (N, Kb) = b.shape if tb else b.shape[::-1]
    assert K == Kb, (a.shape, b.shape, ta, tb)
    tm = _pick(M, (512, 256, 128))
    tn = _pick(N, (1024, 512, 256, 128))
    tk = _pick(K, (2048, 1408, 1280, 1024, 512, 256))
    nk = K // tk
    dn = (((0 if ta else 1,), (1 if tb else 0,)), ((), ()))

    def body(a_ref, b_ref, o_ref, acc_ref):
        k = pl.program_id(2)
        p = lax.dot_general(a_ref[...].astype(BF16), b_ref[...].astype(BF16), dn, preferred_element_type=F32)
        if nk == 1:
            o_ref[...] = p.astype(out_dtype)
        else:
            @pl.when(k == 0)
            def _():
                acc_ref[...] = p

            @pl.when(jnp.logical_and(k > 0, k < nk - 1))
            def _():
                acc_ref[...] += p

            @pl.when(k == nk - 1)
            def _():
                o_ref[...] = (acc_ref[...] + p).astype(out_dtype)

    a_spec = pl.BlockSpec((tk, tm), lambda i, j, k: (k, i)) if ta else pl.BlockSpec((tm, tk), lambda i, j, k: (i, k))
    b_spec = pl.BlockSpec((tn, tk), lambda i, j, k: (j, k)) if tb else pl.BlockSpec((tk, tn), lambda i, j, k: (k, j))
    return _call(body, name, _sds((M, N), out_dtype), (M // tm, N // tn, nk), [a_spec, b_spec],
                 pl.BlockSpec((tm, tn), lambda i, j, k: (i, j)), scratch=[pltpu.VMEM((tm, tn), F32)],
                 sem=("parallel", "parallel", "arbitrary"))(a, b)


def _sigmoid(x):
    return 1.0 / (1.0 + jnp.exp(-x))


def _rope_group(r, c, sa, sb):
    return r * c + pltpu.roll(r, 32, 1) * sa + pltpu.roll(r, 96, 1) * sb


def _ln_fwd(xs, coefs, g, b, name, want_z):
    S, D = xs[0].shape
    tm = 256
    n = len(xs)

    def body(*refs):
        x_refs, g_ref, b_ref, outs = refs[:n], refs[n], refs[n + 1], refs[n + 2:]
        z = None
        for cf, r in zip(coefs, x_refs):
            t = r[...] if cf == 1.0 else cf * r[...]
            z = t if z is None else z + t
        mu = jnp.mean(z, axis=-1, keepdims=True)
        zc = z - mu
        var = jnp.mean(zc * zc, axis=-1, keepdims=True)
        y = zc * lax.rsqrt(var + LN_EPS) * g_ref[...] + b_ref[...]
        if want_z:
            outs[0][...] = z
        outs[-2][...] = y
        outs[-1][...] = y.astype(BF16)

    out_shape = [_sds((S, D), F32)] * (2 if want_z else 1) + [_sds((S, D), BF16)]
    return _call(body, name, out_shape, (S // tm,), [_rows(tm, D)] * n + [_whole((1, D))] * 2,
                 [_rows(tm, D)] * len(out_shape), sem=("parallel",))(*xs, g, b)


def _ln_bwd(dys, coefs, z, g, name):
    S, D = z.shape
    tm = 256
    n = len(dys)

    def body(*refs):
        dy_refs, z_ref, g_ref = refs[:n], refs[n], refs[n + 1]
        dz_ref, dzb_ref, dg_ref, db_ref = refs[n + 2:]
        dy = None
        for cf, r in zip(coefs, dy_refs):
            t = r[...] if cf == 1.0 else cf * r[...]
            dy = t if dy is None else dy + t
        zv = z_ref[...]
        mu = jnp.mean(zv, axis=-1, keepdims=True)
        zc = zv - mu
        var = jnp.mean(zc * zc, axis=-1, keepdims=True)
        rstd = lax.rsqrt(var + LN_EPS)
        xh = zc * rstd
        dyg = dy * g_ref[...]
        dz = rstd * (dyg - jnp.mean(dyg, axis=-1, keepdims=True) - xh * jnp.mean(dyg * xh, axis=-1, keepdims=True))
        dz_ref[...] = dz
        dzb_ref[...] = dz.astype(BF16)

        @pl.when(pl.program_id(0) == 0)
        def _():
            dg_ref[...] = jnp.zeros_like(dg_ref)
            db_ref[...] = jnp.zeros_like(db_ref)

        dg_ref[...] += jnp.sum(dy * xh, axis=0, keepdims=True)
        db_ref[...] += jnp.sum(dy, axis=0, keepdims=True)

    return _call(body, name, [_sds((S, D), F32), _sds((S, D), BF16), _sds((1, D), F32), _sds((1, D), F32)],
                 (S // tm,), [_rows(tm, D)] * (n + 1) + [_whole((1, D))],
                 [_rows(tm, D), _rows(tm, D), _whole((1, D)), _whole((1, D))], sem=("arbitrary",))(*dys, z, g)


def _rms(x, g):
    return x * lax.rsqrt(jnp.mean(x * x, axis=-1, keepdims=True) + RMS_EPS) * g


def _prep1(h, tabs, qg, kvg, name):
    S = h.shape[0]
    tm = 256
    cm, sam, sbm, cr, sr = tabs

    def body(h_ref, cm_ref, sam_ref, sbm_ref, cr_ref, sr_ref, qg_ref, kvg_ref,
             qn_ref, kvn_ref, kr_ref, rq_ref, rk_ref, rv_ref):
        qn_ref[...] = _rms(h_ref[:, 0:Q_LORA], qg_ref[...]).astype(BF16)
        kvn_ref[...] = _rms(h_ref[:, Q_LORA:Q_LORA + KV_LORA], kvg_ref[...]).astype(BF16)
        kr_ref[...] = _rope_group(h_ref[:, 768:896], cm_ref[...], sam_ref[...], sbm_ref[...])
        c, s = cr_ref[...], sr_ref[...]
        for hd in range(RET_HEADS):
            for src, dst, scale in ((MLA_IN, rq_ref, RET_SCALE), (MLA_IN + 1024, rk_ref, None)):
                t1 = h_ref[:, src + hd * 256:src + hd * 256 + 128]
                t2 = h_ref[:, src + hd * 256 + 128:src + hd * 256 + 256]
                o1, o2 = t1 * c - t2 * s, t2 * c + t1 * s
                if scale is not None:
                    o1, o2 = o1 * scale, o2 * scale
                dst[:, hd * 256:hd * 256 + 128] = o1.astype(BF16)
                dst[:, hd * 256 + 128:hd * 256 + 256] = o2.astype(BF16)
        rv_ref[...] = h_ref[:, MLA_IN + 2048:MLA_IN + 3072].astype(BF16)

    t128 = _rows(tm, LANES)
    return _call(body, name,
                 [_sds((S, Q_LORA), BF16), _sds((S, KV_LORA), BF16), _sds((S, LANES), F32),
                  _sds((S, 1024), BF16), _sds((S, 1024), BF16), _sds((S, 1024), BF16)],
                 (S // tm,),
                 [_rows(tm, D_IN_PAD), t128, t128, t128, t128, t128, _whole((1, Q_LORA)), _whole((1, KV_LORA))],
                 [_rows(tm, Q_LORA), _rows(tm, KV_LORA), t128, _rows(tm, 1024), _rows(tm, 1024), _rows(tm, 1024)],
                 sem=("parallel",))(h, cm, sam, sbm, cr, sr, qg, kvg)


def _prep1_bwd(dqn, dkvn, dkr, drq, drk, drv, drg, h, tabs, qg, kvg, name):
    S = h.shape[0]
    tm = 256
    cm, sam, sbm, cr, sr = tabs

    def rms_bwd(x, g, dy):
        r = lax.rsqrt(jnp.mean(x * x, axis=-1, keepdims=True) + RMS_EPS)
        dyg = dy * g
        dx = r * dyg - x * (r * r * r) * jnp.mean(dyg * x, axis=-1, keepdims=True)
        return dx, jnp.sum(dy * x * r, axis=0, keepdims=True)

    def body(dqn_ref, dkvn_ref, dkr_ref, drq_ref, drk_ref, drv_ref, drg_ref, h_ref,
             cm_ref, sam_ref, sbm_ref, cr_ref, sr_ref, qg_ref, kvg_ref, dh_ref, dqg_ref, dkvg_ref):
        dcq, dqg = rms_bwd(h_ref[:, 0:Q_LORA], qg_ref[...], dqn_ref[...])
        dckv, dkvg = rms_bwd(h_ref[:, Q_LORA:Q_LORA + KV_LORA], kvg_ref[...], dkvn_ref[...])
        dh_ref[:, 0:Q_LORA] = dcq.astype(BF16)
        dh_ref[:, Q_LORA:Q_LORA + KV_LORA] = dckv.astype(BF16)
        dh_ref[:, 768:896] = _rope_group(dkr_ref[...], cm_ref[...], -sam_ref[...], -sbm_ref[...]).astype(BF16)
        dh_ref[:, 896:1024] = jnp.zeros((tm, LANES), BF16)
        c, s = cr_ref[...], sr_ref[...]
        for hd in range(RET_HEADS):
            for src, dst, scale in ((drq_ref, MLA_IN, RET_SCALE), (drk_ref, MLA_IN + 1024, None)):
                d1 = src[:, hd * 256:hd * 256 + 128]
                d2 = src[:, hd * 256 + 128:hd * 256 + 256]
                if scale is not None:
                    d1, d2 = d1 * scale, d2 * scale
                dh_ref[:, dst + hd * 256:dst + hd * 256 + 128] = (d1 * c + d2 * s).astype(BF16)
                dh_ref[:, dst + hd * 256 + 128:dst + hd * 256 + 256] = (d2 * c - d1 * s).astype(BF16)
        dh_ref[:, MLA_IN + 2048:MLA_IN + 3072] = drv_ref[...].astype(BF16)
        dh_ref[:, MLA_IN + 3072:MLA_IN + 4096] = drg_ref[...].astype(BF16)

        @pl.when(pl.program_id(0) == 0)
        def _():
            dqg_ref[...] = jnp.zeros_like(dqg_ref)
            dkvg_ref[...] = jnp.zeros_like(dkvg_ref)

        dqg_ref[...] += dqg
        dkvg_ref[...] += dkvg

    t128 = _rows(tm, LANES)
    return _call(body, name,
                 [_sds((S, D_IN_PAD), BF16), _sds((1, Q_LORA), F32), _sds((1, KV_LORA), F32)],
                 (S // tm,),
                 [_rows(tm, Q_LORA), _rows(tm, KV_LORA), t128, _rows(tm, 1024), _rows(tm, 1024), _rows(tm, 1024),
                  _rows(tm, 1024), _rows(tm, D_IN_PAD), t128, t128, t128, t128, t128,
                  _whole((1, Q_LORA)), _whole((1, KV_LORA))],
                 [_rows(tm, D_IN_PAD), _whole((1, Q_LORA)), _whole((1, KV_LORA))],
                 sem=("arbitrary",))(dqn, dkvn, dkr, drq, drk, drv, drg, h, cm, sam, sbm, cr, sr, qg, kvg)


def _prep2(q, kv, kr, tabs, name):
    S = q.shape[0]
    tm = 256
    cm, sam, sbm = tabs[:3]

    def body(q_ref, kv_ref, kr_ref, cm_ref, sam_ref, sbm_ref, qo_ref, ko_ref, vo_ref):
        c, sa, sb = cm_ref[...], sam_ref[...], sbm_ref[...]
        krb = kr_ref[...].astype(BF16)
        for hd in range(MLA_HEADS):
            o = hd * HEAD_PAD
            qo_ref[:, o:o + 128] = q_ref[:, o:o + 128].astype(BF16)
            qo_ref[:, o + 128:o + 256] = _rope_group(q_ref[:, o + 128:o + 256], c, sa, sb).astype(BF16)
            ko_ref[:, o:o + 128] = kv_ref[:, hd * 128:hd * 128 + 128].astype(BF16)
            ko_ref[:, o + 128:o + 256] = krb
        vo_ref[...] = kv_ref[:, 1024:2048].astype(BF16)

    t128 = _rows(tm, LANES)
    return _call(body, name, [_sds((S, 2048), BF16), _sds((S, 2048), BF16), _sds((S, 1024), BF16)], (S // tm,),
                 [_rows(tm, 2048), _rows(tm, 2048), t128, t128, t128, t128],
                 [_rows(tm, 2048), _rows(tm, 2048), _rows(tm, 1024)], sem=("parallel",))(q, kv, kr, cm, sam, sbm)


def _prep2_bwd(dqm, dkm, dvm, tabs, name):
    S = dqm.shape[0]
    tm = 256
    cm, sam, sbm = tabs[:3]

    def body(dq_ref, dk_ref, dv_ref, cm_ref, sam_ref, sbm_ref, dqo_ref, dkvo_ref, dkr_ref):
        c, sa, sb = cm_ref[...], -sam_ref[...], -sbm_ref[...]
        dkr = None
        for hd in range(MLA_HEADS):
            o = hd * HEAD_PAD
            dqo_ref[:, o:o + 128] = dq_ref[:, o:o + 128].astype(BF16)
            dqo_ref[:, o + 128:o + 256] = _rope_group(dq_ref[:, o + 128:o + 256], c, sa, sb).astype(BF16)
            dkvo_ref[:, hd * 128:hd * 128 + 128] = dk_ref[:, o:o + 128].astype(BF16)
            t = dk_ref[:, o + 128:o + 256]
            dkr = t if dkr is None else dkr + t
        dkvo_ref[:, 1024:2048] = dv_ref[...].astype(BF16)
        dkr_ref[...] = dkr

    t128 = _rows(tm, LANES)
    return _call(body, name, [_sds((S, 2048), BF16), _sds((S, 2048), BF16), _sds((S, LANES), F32)], (S // tm,),
                 [_rows(tm, 2048), _rows(tm, 2048), _rows(tm, 1024), t128, t128, t128],
                 [_rows(tm, 2048), _rows(tm, 2048), t128], sem=("parallel",))(dqm, dkm, dvm, cm, sam, sbm)


def _gn_gate(a, o, h, gg, gb, name):
    S = a.shape[0]
    tm = 256

    def body(a_ref, o_ref, rg_ref, gg_ref, gb_ref, mix_ref):
        mix_ref[:, 0:1024] = a_ref[...].astype(BF16)
        for hd in range(RET_HEADS):
            sl = slice(hd * 256, hd * 256 + 256)
            ov = o_ref[:, sl]
            mu = jnp.mean(ov, axis=-1, keepdims=True)
            oc = ov - mu
            var = jnp.mean(oc * oc, axis=-1, keepdims=True)
            y = oc * lax.rsqrt(var + GN_EPS) * gg_ref[:, sl] + gb_ref[:, sl]
            rg = rg_ref[:, sl]
            mix_ref[:, 1024 + hd * 256:1024 + hd * 256 + 256] = (rg * _sigmoid(rg) * y).astype(BF16)

    return _call(body, name, _sds((S, 2048), BF16), (S // tm,),
                 [_rows(tm, 1024), _rows(tm, 1024), _rows(tm, 1024, 4), _whole((1, 1024)), _whole((1, 1024))],
                 _rows(tm, 2048), sem=("parallel",))(a, o, h, gg, gb)


def _gn_gate_bwd(dmixin, o, h, gg, gb, name):
    S = o.shape[0]
    tm = 256

    def body(dr_ref, o_ref, rg_ref, gg_ref, gb_ref, do_ref, drg_ref, dgg_ref, dgb_ref):
        @pl.when(pl.program_id(0) == 0)
        def _():
            dgg_ref[...] = jnp.zeros_like(dgg_ref)
            dgb_ref[...] = jnp.zeros_like(dgb_ref)

        for hd in range(RET_HEADS):
            sl = slice(hd * 256, hd * 256 + 256)
            ov = o_ref[:, sl]
            mu = jnp.mean(ov, axis=-1, keepdims=True)
            oc = ov - mu
            var = jnp.mean(oc * oc, axis=-1, keepdims=True)
            rstd = lax.rsqrt(var + GN_EPS)
            xh = oc * rstd
            g = gg_ref[:, sl]
            y = xh * g + gb_ref[:, sl]
            rg = rg_ref[:, sl]
            sg = _sigmoid(rg)
            dr = dr_ref[:, sl]
            dy = dr * (rg * sg)
            drg_ref[:, sl] = dr * y * (sg * (1.0 + rg * (1.0 - sg)))
            dgg_ref[:, sl] += jnp.sum(dy * xh, axis=0, keepdims=True)
            dgb_ref[:, sl] += jnp.sum(dy, axis=0, keepdims=True)
            dxh = dy * g
            do = rstd * (dxh - jnp.mean(dxh, axis=-1, keepdims=True) - xh * jnp.mean(dxh * xh, axis=-1, keepdims=True))
            do_ref[:, sl] = do.astype(BF16)

    return _call(body, name,
                 [_sds((S, 1024), BF16), _sds((S, 1024), F32), _sds((1, 1024), F32), _sds((1, 1024), F32)],
                 (S // tm,),
                 [_rows(tm, 1024, 1), _rows(tm, 1024), _rows(tm, 1024, 4), _whole((1, 1024)), _whole((1, 1024))],
                 [_rows(tm, 1024), _rows(tm, 1024), _whole((1, 1024)), _whole((1, 1024))],
                 sem=("arbitrary",))(dmixin, o, h, gg, gb)


def _swiglu(gu, name):
    S = gu.shape[0]
    tm = 256

    def body(g_ref, u_ref, o_ref):
        g = g_ref[...]
        o_ref[...] = (g * _sigmoid(g) * u_ref[...]).astype(BF16)

    return _call(body, name, _sds((S, D_FF), BF16), (S // tm,), [_rows(tm, D_FF, 0), _rows(tm, D_FF, 1)],
                 _rows(tm, D_FF), sem=("parallel",))(gu, gu)


def _swiglu_bwd(gu, dact, name):
    S = gu.shape[0]
    tm = 128

    def body(g_ref, u_ref, d_ref, o_ref):
        g, u, d = g_ref[...], u_ref[...], d_ref[...]
        sg = _sigmoid(g)
        o_ref[:, 0:D_FF] = (d * u * (sg * (1.0 + g * (1.0 - sg)))).astype(BF16)
        o_ref[:, D_FF:2 * D_FF] = (d * (g * sg)).astype(BF16)

    return _call(body, name, _sds((S, 2 * D_FF), BF16), (S // tm,),
                 [_rows(tm, D_FF, 0), _rows(tm, D_FF, 1), _rows(tm, D_FF)], _rows(tm, 2 * D_FF),
                 sem=("parallel",))(gu, gu, dact)


def _loss_head(y, target, name):
    S, D = y.shape
    tm = 256

    def body(y_ref, t_ref, dy_ref, acc_ref):
        e = y_ref[...] - t_ref[...]
        dy_ref[...] = e / D

        @pl.when(pl.program_id(0) == 0)
        def _():
            acc_ref[...] = jnp.zeros_like(acc_ref)

        acc_ref[...] += jnp.sum(e * e, axis=0, keepdims=True)

    return _call(body, name, [_sds((S, D), F32), _sds((1, D), F32)], (S // tm,), [_rows(tm, D), _rows(tm, D)],
                 [_rows(tm, D), _whole((1, D))], sem=("arbitrary",))(y, target)


def _chunk_mask(T):
    r = lax.shift_right_logical(lax.broadcasted_iota(jnp.int32, (T, T), 0), 6)
    c = lax.shift_right_logical(lax.broadcasted_iota(jnp.int32, (T, T), 1), 6)
    return r >= c


def _dot_nt(a, b):
    return lax.dot_general(a, b, (((1,), (1,)), ((), ())), preferred_element_type=F32)


def _dot_tn(a, b):
    return lax.dot_general(a, b, (((0,), (0,)), ((), ())), preferred_element_type=F32)


def _decay_tables(T):
    lg = jnp.log1p(-jnp.exp2(-5.0 - jnp.arange(RET_HEADS, dtype=F32)))
    idx = jnp.arange(T, dtype=F32)
    diff = idx[:, None] - idx[None, :]
    rel = jnp.exp(lg[:, None, None] * diff[None])
    cid = jnp.arange(T) // CHUNK
    mask = (cid[:, None] >= cid[None, :]).astype(F32)
    reld = jnp.exp(lg[:, None, None] * jnp.abs(diff)[None]) * mask[None]
    lgrow = jnp.broadcast_to(lg[:, None, None], (RET_HEADS, 1, LANES))
    return lgrow, rel, reld


def _attn_fwd(q, k, v, heads, dk, dv, softmax, name, tables=None):
    S = q.shape[0]
    T = ATT_BLOCK
    nq = S // T
    rep = T // LANES

    def body(*refs):
        if softmax:
            q_ref, k_ref, v_ref, o_ref, lse_ref, m_sc, l_sc, acc_sc = refs
        else:
            q_ref, k_ref, v_ref, lg_ref, rel_ref, reld_ref, o_ref, acc_sc = refs
        i = pl.program_id(1)
        qv = q_ref[...]

        def kv_block(j):
            rows = pl.ds(pl.multiple_of(j * T, T), T)
            return k_ref[rows, :], v_ref[rows, :]

        kb, vb = kv_block(i)
        s = _dot_nt(qv, kb)
        if softmax:
            s = jnp.where(_chunk_mask(T), s * MLA_SCALE, NEG)
            m = jnp.max(s, axis=-1, keepdims=True)
            p = jnp.exp(s - m)
            m_sc[...] = jnp.broadcast_to(m, (T, LANES))
            l_sc[...] = jnp.broadcast_to(jnp.sum(p, axis=-1, keepdims=True), (T, LANES))
        else:
            p = s * reld_ref[0]
        acc_sc[...] = jnp.dot(p.astype(BF16), vb, preferred_element_type=F32)

        def step(j, carry):
            kb, vb = kv_block(j)
            s = _dot_nt(qv, kb)
            if softmax:
                s = s * MLA_SCALE
                m_prev = m_sc[...]
                m_next = jnp.maximum(m_prev, jnp.max(s, axis=-1, keepdims=True))
                alpha = jnp.exp(m_prev - m_next)
                p = jnp.exp(s - jnp.tile(m_next, (1, rep)))
                l_sc[...] = alpha * l_sc[...] + jnp.sum(p, axis=-1, keepdims=True)
                m_sc[...] = m_next
                acc_sc[...] = acc_sc[...] * jnp.tile(alpha, (1, dv // LANES)) + jnp.dot(
                    p.astype(BF16), vb, preferred_element_type=F32)
            else:
                fac = jnp.exp(lg_ref[0] * ((i - j) * T).astype(F32))
                p = s * (rel_ref[0] * jnp.tile(fac, (1, rep)))
                acc_sc[...] += jnp.dot(p.astype(BF16), vb, preferred_element_type=F32)
            return carry

        lax.fori_loop(0, i, step, 0)
        if softmax:
            l = l_sc[...]
            o_ref[...] = acc_sc[...] / jnp.tile(l, (1, dv // LANES))
            lse_ref[...] = m_sc[...] + jnp.log(l)
        else:
            o_ref[...] = acc_sc[...]

    in_specs = [pl.BlockSpec((T, dk), lambda h, i: (i, h)), pl.BlockSpec((S, dk), lambda h, i: (0, h)),
                pl.BlockSpec((S, dv), lambda h, i: (0, h))]
    o_spec = pl.BlockSpec((T, dv), lambda h, i: (i, h))
    if softmax:
        return _call(body, name, [_sds((S, heads * dv), F32), _sds((S, heads * LANES), F32)], (heads, nq), in_specs,
                     [o_spec, pl.BlockSpec((T, LANES), lambda h, i: (i, h))],
                     scratch=[pltpu.VMEM((T, LANES), F32), pltpu.VMEM((T, LANES), F32), pltpu.VMEM((T, dv), F32)],
                     sem=("parallel", "arbitrary"))(q, k, v)
    lgrow, rel, reld = tables
    in_specs += [pl.BlockSpec((1, 1, LANES), lambda h, i: (h, 0, 0)), pl.BlockSpec((1, T, T), lambda h, i: (h, 0, 0)),
                 pl.BlockSpec((1, T, T), lambda h, i: (h, 0, 0))]
    return _call(body, name, _sds((S, heads * dv), F32), (heads, nq), in_specs, o_spec,
                 scratch=[pltpu.VMEM((T, dv), F32)], sem=("parallel", "arbitrary"))(q, k, v, lgrow, rel, reld)


def _attn_bwd(q, k, v, do, heads, dk, dv, softmax, name, o=None, lse=None, tables=None):
    S = q.shape[0]
    T = ATT_BLOCK
    nq = S // T
    rep = T // LANES

    def body(*refs):
        if softmax:
            q_ref, k_ref, v_ref, do_ref, o_ref, lse_ref, dq_ref, dk_ref, dv_ref, dq_sc = refs
        else:
            q_ref, k_ref, v_ref, do_ref, lg_ref, rel_ref, reld_ref, dq_ref, dk_ref, dv_ref, dq_sc = refs
        i = pl.program_id(1)

        @pl.when(i == 0)
        def _():
            dk_ref[...] = jnp.zeros_like(dk_ref)
            dv_ref[...] = jnp.zeros_like(dv_ref)

        qv = q_ref[...]
        dof = do_ref[...].astype(F32)
        dov = dof.astype(BF16)
        if softmax:
            delta = jnp.sum(dof * o_ref[...], axis=-1, keepdims=True)
            lse_t = jnp.tile(lse_ref[...], (1, rep))
        dq_sc[...] = jnp.zeros_like(dq_sc)

        def block(j, diagonal):
            rows = pl.ds(pl.multiple_of(j * T, T), T)
            kb, vb = k_ref[rows, :], v_ref[rows, :]
            s = _dot_nt(qv, kb)
            dp = _dot_nt(dov, vb)
            if softmax:
                s = s * MLA_SCALE
                if diagonal:
                    s = jnp.where(_chunk_mask(T), s, NEG)
                p = jnp.exp(s - lse_t)
                ds = p * (dp - delta) * MLA_SCALE
            else:
                if diagonal:
                    dec = reld_ref[0]
                else:
                    fac = jnp.exp(lg_ref[0] * ((i - j) * T).astype(F32))
                    dec = rel_ref[0] * jnp.tile(fac, (1, rep))
                p = s * dec
                ds = dp * dec
            dsb = ds.astype(BF16)
            dv_ref[rows, :] += _dot_tn(p.astype(BF16), dov)
            dk_ref[rows, :] += _dot_tn(dsb, qv)
            dq_sc[...] += jnp.dot(dsb, kb, preferred_element_type=F32)

        block(i, True)

        def step(j, carry):
            block(j, False)
            return carry

        lax.fori_loop(0, i, step, 0)
        dq_ref[...] = dq_sc[...]

    qspec = pl.BlockSpec((T, dk), lambda h, i: (i, h))
    kspec = pl.BlockSpec((S, dk), lambda h, i: (0, h))
    vspec = pl.BlockSpec((S, dv), lambda h, i: (0, h))
    dospec = pl.BlockSpec((T, dv), lambda h, i: (i, h))
    in_specs = [qspec, kspec, vspec, dospec]
    args = [q, k, v, do]
    if softmax:
        in_specs += [dospec, pl.BlockSpec((T, LANES), lambda h, i: (i, h))]
        args += [o, lse]
    else:
        in_specs += [pl.BlockSpec((1, 1, LANES), lambda h, i: (h, 0, 0)),
                     pl.BlockSpec((1, T, T), lambda h, i: (h, 0, 0)), pl.BlockSpec((1, T, T), lambda h, i: (h, 0, 0))]
        args += list(tables)
    return _call(body, name, [_sds((S, heads * dk), F32), _sds((S, heads * dk), F32), _sds((S, heads * dv), F32)],
                 (heads, nq), in_specs, [qspec, kspec, vspec], scratch=[pltpu.VMEM((T, dk), F32)],
                 sem=("parallel", "arbitrary"))(*args)


def _rope_tables(pos):
    def tables(dim):
        inv_freq = ROPE_THETA ** (-jnp.arange(0, dim, 2, dtype=F32) / dim)
        ang = pos.astype(F32)[:, None] * inv_freq
        return jnp.cos(ang), jnp.sin(ang)

    cm, sm = tables(ROPE)
    S = pos.shape[0]
    z32, z64 = jnp.zeros((S, 32), F32), jnp.zeros((S, 64), F32)
    cr, sr = tables(RET_DK)
    return (jnp.concatenate([cm, cm, z64], 1), jnp.concatenate([z32, sm, z64], 1),
            jnp.concatenate([-sm, z32, z64], 1), cr, sr)


def _row(v):
    return v.reshape(1, -1).astype(F32)


def _local_step(x, pos, target, W, P):
    tabs = _rope_tables(pos)
    dtabs = _decay_tables(ATT_BLOCK)
    xf, xb = _ln_fwd([x], [1.0], _row(P["ln_in_g"]), _row(P["ln_in_b"]), "ln_in", False)
    saved = []
    for l in range(DEPTH):
        w = W[l]
        t = f"_l{l}"
        h = _matmul(xb, w["w_in"], "mm_h" + t)
        qn, kvn, kr, rq, rk, rv = _prep1(h, tabs, _row(P["q_norm_g"][l]), _row(P["kv_norm_g"][l]), "prep1" + t)
        q = _matmul(qn, w["w_uq"], "mm_q" + t)
        kv = _matmul(kvn, w["w_ukv"], "mm_kv" + t)
        qm, km, vm = _prep2(q, kv, kr, tabs, "prep2" + t)
        a, lse = _attn_fwd(qm, km, vm, MLA_HEADS, HEAD_PAD, VDIM, True, "mla_fwd" + t)
        o = _attn_fwd(rq, rk, rv, RET_HEADS, RET_DK, RET_DV, False, "ret_fwd" + t, tables=dtabs)
        mixin = _gn_gate(a, o, h, _row(P["ret_gn_g"][l]), _row(P["ret_gn_b"][l]), "gn_gate" + t)
        mix = _matmul(mixin, w["w_out"], "mm_mix" + t)
        z1, x1f, x1b = _ln_fwd([xf, mix], [ALPHA, 1.0], _row(P["ln1_g"][l]), _row(P["ln1_b"][l]), "ln1" + t, True)
        gu = _matmul(x1b, w["w_gu"], "mm_gu" + t)
        act = _swiglu(gu, "swiglu" + t)
        f = _matmul(act, w["w_down"], "mm_down" + t)
        z2, x2f, x2b = _ln_fwd([x1f, f], [ALPHA, 1.0], _row(P["ln2_g"][l]), _row(P["ln2_b"][l]), "ln2" + t, True)
        saved.append(dict(xb=xb, h=h, qn=qn, kvn=kvn, rq=rq, rk=rk, rv=rv, qm=qm, km=km, vm=vm, a=a, lse=lse, o=o,
                          mixin=mixin, z1=z1, x1b=x1b, gu=gu, act=act, z2=z2))
        xf, xb = x2f, x2b

    dy, sqerr = _loss_head(xf, target, "loss_head")
    dW = [None] * DEPTH
    dP = {}
    dys, coefs = [dy], [1.0]
    for l in reversed(range(DEPTH)):
        w, sv = W[l], saved[l]
        t = f"_l{l}"
        dz2, dz2b, dg, db = _ln_bwd(dys, coefs, sv["z2"], _row(P["ln2_g"][l]), "ln2_bwd" + t)
        dP[("ln2_g", l)], dP[("ln2_b", l)] = dg, db
        g = {}
        g["w_down"] = _matmul(sv["act"], dz2b, "mm_dw_down" + t, ta=True, out_dtype=BF16)
        dact = _matmul(dz2b, w["w_down"], "mm_dact" + t, tb=True)
        dgu = _swiglu_bwd(sv["gu"], dact, "swiglu_bwd" + t)
        g["w_gu"] = _matmul(sv["x1b"], dgu, "mm_dw_gu" + t, ta=True, out_dtype=BF16)
        dx1 = _matmul(dgu, w["w_gu"], "mm_dx1" + t, tb=True)
        dz1, dz1b, dg, db = _ln_bwd([dz2, dx1], [ALPHA, 1.0], sv["z1"], _row(P["ln1_g"][l]), "ln1_bwd" + t)
        dP[("ln1_g", l)], dP[("ln1_b", l)] = dg, db
        g["w_out"] = _matmul(sv["mixin"], dz1b, "mm_dw_out" + t, ta=True, out_dtype=BF16)
        dmixin = _matmul(dz1b, w["w_out"], "mm_dmixin" + t, tb=True)
        do, drg, dgg, dgb = _gn_gate_bwd(dmixin, sv["o"], sv["h"], _row(P["ret_gn_g"][l]), _row(P["ret_gn_b"][l]),
                                         "gn_gate_bwd" + t)
        dP[("ret_gn_g", l)], dP[("ret_gn_b", l)] = dgg, dgb
        drq, drk, drv = _attn_bwd(sv["rq"], sv["rk"], sv["rv"], do, RET_HEADS, RET_DK, RET_DV, False,
                                  "ret_bwd" + t, tables=dtabs)
        dqm, dkm, dvm = _attn_bwd(sv["qm"], sv["km"], sv["vm"], dmixin, MLA_HEADS, HEAD_PAD, VDIM, True,
                                  "mla_bwd" + t, o=sv["a"], lse=sv["lse"])
        dq, dkv, dkr = _prep2_bwd(dqm, dkm, dvm, tabs, "prep2_bwd" + t)
        g["w_uq"] = _matmul(sv["qn"], dq, "mm_dw_uq" + t, ta=True, out_dtype=BF16)
        dqn = _matmul(dq, w["w_uq"], "mm_dqn" + t, tb=True)
        g["w_ukv"] = _matmul(sv["kvn"], dkv, "mm_dw_ukv" + t, ta=True, out_dtype=BF16)
        dkvn = _matmul(dkv, w["w_ukv"], "mm_dkvn" + t, tb=True)
        dh, dqg, dkvg = _prep1_bwd(dqn, dkvn, dkr, drq, drk, drv, drg, sv["h"], tabs, _row(P["q_norm_g"][l]),
                                   _row(P["kv_norm_g"][l]), "prep1_bwd" + t)
        dP[("q_norm_g", l)], dP[("kv_norm_g", l)] = dqg, dkvg
        g["w_in"] = _matmul(sv["xb"], dh, "mm_dw_in" + t, ta=True, out_dtype=BF16)
        dxl = _matmul(dh, w["w_in"], "mm_dxl" + t, tb=True)
        dW[l] = g
        dys, coefs = [dz1, dxl], [ALPHA, 1.0]
    grad_x, _, dg, db = _ln_bwd(dys, coefs, x, _row(P["ln_in_g"]), "ln_in_bwd")
    dP[("ln_in_g", None)], dP[("ln_in_b", None)] = dg, db
    return sqerr, grad_x, dW, dP


def _flatten_shards(parts):
    lead = parts[BIG[0]].shape[:-2]
    used = sum(r * c for r, c in BIG_SHARD.values())
    tail = jnp.zeros(lead + (FLAT_ROWS * FLAT_W - used,), parts[BIG[0]].dtype)
    flat = jnp.concatenate([parts[n].reshape(lead + (-1,)) for n in BIG] + [tail], axis=-1)
    return flat.reshape(lead + (FLAT_ROWS, FLAT_W))


def _unflatten_shards(flat):
    lead = flat.shape[:-2]
    v = flat.reshape(lead + (-1,))
    out, at = {}, 0
    for n in BIG:
        r, c = BIG_SHARD[n]
        out[n] = v[..., at:at + r * c].reshape(lead + (r, c))
        at += r * c
    return out


def _internal_weights(gathered, l):
    sh = _unflatten_shards(gathered[:, l])
    cols = lambda n: jnp.concatenate([sh[n][j] for j in range(N_CHIPS)], axis=1)
    rows = lambda n: sh[n].reshape(-1, sh[n].shape[-1])
    w_in = cols("w_in")
    w_in = jnp.concatenate([w_in[:, :MLA_IN_USED], jnp.zeros((D_MODEL, MLA_IN - MLA_IN_USED), BF16),
                            w_in[:, MLA_IN_USED:]], axis=1)
    w_uq = cols("w_uq").reshape(Q_LORA, MLA_HEADS, NOPE + ROPE)
    w_uq = jnp.pad(w_uq, ((0, 0), (0, 0), (0, HEAD_PAD - NOPE - ROPE))).reshape(Q_LORA, MLA_HEADS * HEAD_PAD)
    w_ukv = cols("w_ukv").reshape(KV_LORA, MLA_HEADS, NOPE + VDIM)
    w_ukv = jnp.concatenate([w_ukv[:, :, :NOPE].reshape(KV_LORA, -1), w_ukv[:, :, NOPE:].reshape(KV_LORA, -1)], axis=1)
    return dict(w_in=w_in, w_uq=w_uq, w_ukv=w_ukv, w_out=rows("w_out"),
                w_gu=jnp.concatenate([cols("w_gate"), cols("w_up")], axis=1), w_down=rows("w_down"))


def _grad_shards(g):
    def cols(v):
        return jnp.stack(jnp.split(v, N_CHIPS, axis=1))

    def rows(v):
        return v.reshape(N_CHIPS, -1, v.shape[-1])

    w_in = jnp.concatenate([g["w_in"][:, :MLA_IN_USED], g["w_in"][:, MLA_IN:]], axis=1)
    w_uq = g["w_uq"].reshape(Q_LORA, MLA_HEADS, HEAD_PAD)[:, :, :NOPE + ROPE].reshape(Q_LORA, -1)
    kn = g["w_ukv"][:, :MLA_HEADS * NOPE].reshape(KV_LORA, MLA_HEADS, NOPE)
    vv = g["w_ukv"][:, MLA_HEADS * NOPE:].reshape(KV_LORA, MLA_HEADS, VDIM)
    w_ukv = jnp.concatenate([kn, vv], axis=2).reshape(KV_LORA, -1)
    return dict(w_in=cols(w_in), w_uq=cols(w_uq), w_ukv=cols(w_ukv), w_out=rows(g["w_out"]),
                w_gate=cols(g["w_gu"][:, :D_FF]), w_up=cols(g["w_gu"][:, D_FF:]), w_down=rows(g["w_down"]))


def _small_layout(P):
    out, at = {}, 0
    for n in SMALL:
        out[n] = (at, P[n].size)
        at += P[n].size
    return out, at


def _flatten_small(P, last):
    v = jnp.concatenate([P[n].reshape(-1).astype(F32) for n in SMALL] + [last.reshape(-1).astype(F32)])
    return jnp.pad(v, (0, SMALL_ROWS * FLAT_W - v.size)).reshape(SMALL_ROWS, FLAT_W)


HBM = pl.BlockSpec(memory_space=pltpu.HBM)


def _place():
    return lax.axis_index("x"), lax.axis_index("y"), lax.axis_index("c")


def _other_chips(x, y):
    return [(1 - x, y), (x, 1 - y), (1 - x, 1 - y)]


def _rcopy(src, dst, ssem, rsem, dev):
    return pltpu.make_async_remote_copy(src_ref=src, dst_ref=dst, send_sem=ssem, recv_sem=rsem, device_id=dev,
                                        device_id_type=MESH)


def _comm_call(body, name, out_shape, n_in, scratch):
    many = isinstance(out_shape, (list, tuple))
    return pl.pallas_call(body, name=name, out_shape=out_shape, in_specs=[HBM] * n_in,
                          out_specs=[HBM] * len(out_shape) if many else HBM, scratch_shapes=scratch)


def _allgather_weights(wflat):
    def body(w_ref, g_ref, ssem, rsem, fssem, frsem, lsem):
        x, y, c = _place()
        j = 2 * x + y
        sib = (x, y, 1 - c)
        chips = _other_chips(x, y)
        own = pltpu.make_async_copy(w_ref, g_ref.at[j], lsem)
        own.start()
        sends = [_rcopy(w_ref.at[c], g_ref.at[j, c], ssem.at[t], rsem.at[t], (cx, cy, c))
                 for t, (cx, cy) in enumerate(chips)]
        for cp in sends:
            cp.start()
        passed = []
        for t, (cx, cy) in enumerate(chips):
            blk = g_ref.at[2 * cx + cy, c]
            _rcopy(blk, blk, ssem.at[t], rsem.at[t], (cx, cy, c)).wait_recv()
            cp = _rcopy(blk, blk, fssem.at[t], frsem.at[t], sib)
            cp.start()
            passed.append(cp)
        for t, (cx, cy) in enumerate(chips):
            blk = g_ref.at[2 * cx + cy, 1 - c]
            _rcopy(blk, blk, fssem.at[t], frsem.at[t], sib).wait_recv()
        for cp in sends + passed:
            cp.wait_send()
        own.wait()

    sem3 = pltpu.SemaphoreType.DMA((3,))
    return _comm_call(body, "allgather_weights", _sds((N_CHIPS,) + wflat.shape, wflat.dtype), 1,
                      [sem3, sem3, sem3, sem3, pltpu.SemaphoreType.DMA(())])(wflat)


def _swap_layers(gd):
    def body(gd_ref, out_ref, ssem, rsem):
        x, y, c = _place()
        cps = [_rcopy(gd_ref.at[jj, 1 - c], out_ref.at[jj], ssem.at[jj], rsem.at[jj], (x, y, 1 - c))
               for jj in range(N_CHIPS)]
        for cp in cps:
            cp.start()
        for cp in cps:
            cp.wait()

    sem4 = pltpu.SemaphoreType.DMA((N_CHIPS,))
    return _comm_call(body, "swap_layers", _sds((N_CHIPS,) + gd.shape[2:], gd.dtype), 1, [sem4, sem4])(gd)


def _exchange_partials(part, small):
    def body(p_ref, s_ref, rcv_ref, all_ref, ssem, rsem, sssem, srsem, lsem):
        x, y, c = _place()
        own = pltpu.make_async_copy(s_ref, all_ref.at[0], lsem)
        own.start()
        cps = [_rcopy(p_ref.at[2 * cx + cy], rcv_ref.at[t], ssem.at[t], rsem.at[t], (cx, cy, c))
               for t, (cx, cy) in enumerate(_other_chips(x, y))]
        for r in range(1, 8):
            fx, fy, fc = (r >> 2) & 1, (r >> 1) & 1, r & 1
            dev = (1 - x if fx else x, 1 - y if fy else y, 1 - c if fc else c)
            cps.append(_rcopy(s_ref, all_ref.at[r], sssem.at[r - 1], srsem.at[r - 1], dev))
        for cp in cps:
            cp.start()
        for cp in cps:
            cp.wait()
        own.wait()

    sem3, sem7 = pltpu.SemaphoreType.DMA((3,)), pltpu.SemaphoreType.DMA((7,))
    return _comm_call(body, "exchange_partials",
                      [_sds((3,) + part.shape[1:], part.dtype), _sds((8,) + small.shape, small.dtype)], 2,
                      [sem3, sem3, sem7, sem7, pltpu.SemaphoreType.DMA(())])(part, small)


def _share_reduced(red):
    def body(r_ref, out_ref, ssem, rsem, lsem):
        x, y, c = _place()
        own = pltpu.make_async_copy(r_ref, out_ref.at[c], lsem)
        own.start()
        cp = _rcopy(r_ref, out_ref.at[c], ssem, rsem, (x, y, 1 - c))
        cp.start()
        cp.wait()
        own.wait()

    dma = pltpu.SemaphoreType.DMA(())
    return _comm_call(body, "share_reduced", _sds((DEPTH,) + red.shape, red.dtype), 1, [dma, dma, dma])(red)


def _add_pair(gd, got, c):
    _, _, R, W = gd.shape
    tm = _pick(R, (512, 256, 128, 8))

    def body(c_ref, a_ref, b_ref, o_ref):
        o_ref[...] = (a_ref[...].astype(F32) + b_ref[...].astype(F32)).astype(o_ref.dtype)

    grid_spec = pltpu.PrefetchScalarGridSpec(
        num_scalar_prefetch=1, grid=(N_CHIPS, R // tm),
        in_specs=[pl.BlockSpec((None, None, tm, W), lambda j, i, c_ref: (j, c_ref[0], i, 0)),
                  pl.BlockSpec((None, tm, W), lambda j, i, c_ref: (j, i, 0))],
        out_specs=pl.BlockSpec((None, tm, W), lambda j, i, c_ref: (j, i, 0)))
    return pl.pallas_call(body, name="add_pair", grid_spec=grid_spec, out_shape=_sds((N_CHIPS, R, W), gd.dtype),
                          compiler_params=pltpu.CompilerParams(dimension_semantics=("parallel", "parallel"),
                                                               vmem_limit_bytes=VMEM_LIMIT))(c, gd, got)


def _add_chips(part, rcv, j):
    _, R, W = part.shape
    tm = _pick(R, (512, 256, 128, 8))

    def body(j_ref, p_ref, r0_ref, r1_ref, r2_ref, o_ref):
        o_ref[...] = ((p_ref[...].astype(F32) + r0_ref[...].astype(F32)) + r1_ref[...].astype(F32)) + r2_ref[...].astype(F32)

    def slot(t):
        return pl.BlockSpec((None, tm, W), lambda i, j_ref: (t, i, 0))

    grid_spec = pltpu.PrefetchScalarGridSpec(
        num_scalar_prefetch=1, grid=(R // tm,),
        in_specs=[pl.BlockSpec((None, tm, W), lambda i, j_ref: (j_ref[0], i, 0)), slot(0), slot(1), slot(2)],
        out_specs=pl.BlockSpec((tm, W), lambda i, j_ref: (i, 0)))
    return pl.pallas_call(body, name="add_chips", grid_spec=grid_spec, out_shape=_sds((R, W), F32),
                          compiler_params=pltpu.CompilerParams(dimension_semantics=("parallel",),
                                                               vmem_limit_bytes=VMEM_LIMIT))(j, part, rcv, rcv, rcv)


def _sum_small(allsmall, slots):
    _, R, W = allsmall.shape

    def body(slots_ref, a_ref, o_ref):
        acc = a_ref[slots_ref[0]]
        for d in range(1, 8):
            acc = acc + a_ref[slots_ref[d]]
        o_ref[...] = acc

    grid_spec = pltpu.PrefetchScalarGridSpec(
        num_scalar_prefetch=1, grid=(1,),
        in_specs=[pl.BlockSpec((8, R, W), lambda i, s_ref: (0, 0, 0))],
        out_specs=pl.BlockSpec((R, W), lambda i, s_ref: (0, 0)))
    return pl.pallas_call(body, name="sum_small", grid_spec=grid_spec, out_shape=_sds((R, W), F32))(slots, allsmall)


def _adamw(w, g, m, v, name):
    R, C = w.shape
    tm = _pick(R, (256, 128, 64, 32, 8))

    def body(w_ref, g_ref, m_ref, v_ref, d_ref, mo_ref, vo_ref):
        gv = g_ref[...]
        mn = ADAM_B1 * m_ref[...] + (1.0 - ADAM_B1) * gv
        vn = ADAM_B2 * v_ref[...] + (1.0 - ADAM_B2) * (gv * gv)
        m_hat = mn / (1.0 - ADAM_B1 ** ADAM_STEP)
        v_hat = vn / (1.0 - ADAM_B2 ** ADAM_STEP)
        d_ref[...] = -ADAM_LR * (m_hat / (jnp.sqrt(v_hat) + ADAM_EPS) + ADAM_WD * w_ref[...])
        mo_ref[...] = mn
        vo_ref[...] = vn

    spec = _rows(tm, C)
    return _call(body, name, [_sds((R, C), F32)] * 3, (R // tm,), [spec] * 4, [spec] * 3, sem=("parallel",))(w, g, m, v)


def kernel(x, positions, ln_in_g, ln_in_b, w_in, q_norm_g, kv_norm_g, w_uq, w_ukv, ret_gn_g, ret_gn_b, w_out, ln1_g, ln1_b, w_gate, w_up, w_down, ln2_g, ln2_b, loss_target, m_ln_in_g, m_ln_in_b, m_w_in, m_q_norm_g, m_kv_norm_g, m_w_uq, m_w_ukv, m_ret_gn_g, m_ret_gn_b, m_w_out, m_ln1_g, m_ln1_b, m_w_gate, m_w_up, m_w_down, m_ln2_g, m_ln2_b, v_ln_in_g, v_ln_in_b, v_w_in, v_q_norm_g, v_kv_norm_g, v_w_uq, v_w_ukv, v_ret_gn_g, v_ret_gn_b, v_w_out, v_ln1_g, v_ln1_b, v_w_gate, v_w_up, v_w_down, v_ln2_g, v_ln2_b):
    given = dict(locals())
    Wt = {n: given[n] for n in WEIGHTS}
    Mo = {n: given["m_" + n] for n in WEIGHTS}
    Vo = {n: given["v_" + n] for n in WEIGHTS}
    cx, cy, cc = _place()
    chip = (2 * cx + cy).astype(jnp.int32)
    core = cc.astype(jnp.int32)
    me = 4 * cx + 2 * cy + cc

    wflat = _flatten_shards({n: Wt[n].astype(BF16) for n in BIG})
    gathered = _allgather_weights(wflat)
    W = [_internal_weights(gathered, l) for l in range(DEPTH)]

    sqerr, grad_x, dW, dP = _local_step(x[0], positions[0], loss_target[0], W, Wt)

    shards = [_grad_shards(dW[l]) for l in range(DEPTH)]
    gd = _flatten_shards({n: jnp.stack([shards[l][n] for l in range(DEPTH)], axis=1) for n in BIG})
    part = _add_pair(gd, _swap_layers(gd), core.reshape(1))
    small_g = {n: (dP[(n, None)] if Wt[n].ndim == 1 else jnp.stack([dP[(n, l)] for l in range(DEPTH)])) for n in SMALL}
    local_loss = 0.5 * jnp.sum(sqerr) / D_MODEL
    rcv, allsmall = _exchange_partials(part, _flatten_small(small_g, local_loss))
    red = _share_reduced(_add_chips(part, rcv, chip.reshape(1)))
    big_g = _unflatten_shards(red)
    small_sum = _sum_small(allsmall, jnp.bitwise_xor(jnp.arange(8, dtype=jnp.int32), me.astype(jnp.int32))).reshape(-1)
    layout, n_small = _small_layout(Wt)
    loss = small_sum[n_small]

    grads, deltas, new_m, new_v = {}, {}, {}, {}
    for n in BIG:
        g = big_g[n]
        two_d = lambda a: a.reshape(-1, a.shape[-1])
        d, mn, vn = _adamw(two_d(Wt[n]), two_d(g), two_d(Mo[n]), two_d(Vo[n]), "adamw_" + n)
        grads[n], deltas[n], new_m[n], new_v[n] = g, d.reshape(g.shape), mn.reshape(g.shape), vn.reshape(g.shape)
    zero = jnp.zeros((), F32)
    d, mn, vn = _adamw(_flatten_small(Wt, zero), small_sum.reshape(SMALL_ROWS, FLAT_W), _flatten_small(Mo, zero),
                       _flatten_small(Vo, zero), "adamw_small")
    for n in SMALL:
        at, size = layout[n]
        pick = lambda a: a.reshape(-1)[at:at + size].reshape(Wt[n].shape)
        grads[n], deltas[n], new_m[n], new_v[n] = pick(small_sum), pick(d), pick(mn), pick(vn)

    return (loss, grad_x[None], *[grads[n] for n in WEIGHTS], *[deltas[n] for n in WEIGHTS],
            *[new_m[n] for n in WEIGHTS], *[new_v[n] for n in WEIGHTS])
```

```python
import functools

import jax
import jax.numpy as jnp
from jax import lax
from jax.experimental import pallas as pl
from jax.experimental.pallas import tpu as pltpu

F32 = jnp.float32
BF16 = jnp.bfloat16

D_MODEL = 2048
DEPTH = 2
CHUNK = 64
MLA_HEADS = 8
Q_LORA = 512
KV_LORA = 256
NOPE = 128
ROPE = 64
VDIM = 128
RET_HEADS = 4
RET_DK = 256
RET_DV = 256
D_FF = 5632
D_IN = 4928
ROPE_THETA = 10000.0
LN_EPS = 1e-5
RMS_EPS = 1e-6
GN_EPS = 1e-5
ALPHA = (2 * DEPTH) ** 0.25
MLA_SCALE = (NOPE + ROPE) ** -0.5
RET_SCALE = RET_DK ** -0.5
ADAM_LR = 0.001
ADAM_B1 = 0.9
ADAM_B2 = 0.999
ADAM_EPS = 1e-08
ADAM_WD = 0.01
ADAM_STEP = 10

LANES = 128
HEAD_PAD = 256
MLA_IN = 1024
MLA_IN_USED = Q_LORA + KV_LORA + ROPE
D_IN_PAD = MLA_IN + 4 * 1024
ATT_BLOCK = 256
NEG = -1e30
VMEM_LIMIT = 56 * 1024 * 1024

N_CHIPS = 4
FLAT_W = 1024
BIG = ("w_in", "w_uq", "w_ukv", "w_out", "w_gate", "w_up", "w_down")
BIG_SHARD = {"w_in": (2048, 1232), "w_uq": (512, 384), "w_ukv": (256, 512), "w_out": (512, 2048),
             "w_gate": (2048, 1408), "w_up": (2048, 1408), "w_down": (1408, 2048)}
SMALL = ("ln_in_g", "ln_in_b", "q_norm_g", "kv_norm_g", "ret_gn_g", "ret_gn_b", "ln1_g", "ln1_b", "ln2_g", "ln2_b")
WEIGHTS = ("ln_in_g", "ln_in_b", "w_in", "q_norm_g", "kv_norm_g", "w_uq", "w_ukv", "ret_gn_g", "ret_gn_b", "w_out",
           "ln1_g", "ln1_b", "w_gate", "w_up", "w_down", "ln2_g", "ln2_b")
SMALL_ROWS = 32

MESH = pl.DeviceIdType.MESH


def _pick(dim, cands):
    for c in cands:
        if dim % c == 0:
            return c
    return dim


def _call(body, name, out_shape, grid, in_specs, out_specs, scratch=(), sem=None):
    return pl.pallas_call(
        body, name=name, out_shape=out_shape, grid=grid, in_specs=in_specs, out_specs=out_specs,
        scratch_shapes=list(scratch),
        compiler_params=pltpu.CompilerParams(dimension_semantics=sem, vmem_limit_bytes=VMEM_LIMIT))


def _rows(tm, w, col=0):
    return pl.BlockSpec((tm, w), lambda i: (i, col))


def _whole(shape):
    return pl.BlockSpec(shape, lambda i: (0,) * len(shape))


def _sds(shape, dtype):
    return jax.ShapeDtypeStruct(shape, dtype)


def _matmul(a, b, name, ta=False, tb=False, out_dtype=F32):
    (K, M) = a.shape if ta else a.shape[::-1]
    (N, Kb) = b.shape if tb else b.shape[::-1]
    assert K == Kb, (a.shape, b.shape, ta, tb)
    tm = _pick(M, (512, 256, 128))
    tn = _pick(N, (1024, 512, 256, 128))
    tk = _pick(K, (2048, 1408, 1280, 1024, 512, 256))
    nk = K // tk
    dn = (((0 if ta else 1,), (1 if tb else 0,)), ((), ()))

    def body(a_ref, b_ref, o_ref, acc_ref):
        k = pl.program_id(2)
        p = lax.dot_general(a_ref[...].astype(BF16), b_ref[...].astype(BF16), dn, preferred_element_type=F32)
        if nk == 1:
            o_ref[...] = p.astype(out_dtype)
        else:
            @pl.when(k == 0)
            def _():
                acc_ref[...] = p

            @pl.when(jnp.logical_and(k > 0, k < nk - 1))
            def _():
                acc_ref[...] += p

            @pl.when(k == nk - 1)
            def _():
                o_ref[...] = (acc_ref[...] + p).astype(out_dtype)

    a_spec = pl.BlockSpec((tk, tm), lambda i, j, k: (k, i)) if ta else pl.BlockSpec((tm, tk), lambda i, j, k: (i, k))
    b_spec = pl.BlockSpec((tn, tk), lambda i, j, k: (j, k)) if tb else pl.BlockSpec((tk, tn), lambda i, j, k: (k, j))
    return _call(body, name, _sds((M, N), out_dtype), (M // tm, N // tn, nk), [a_spec, b_spec],
                 pl.BlockSpec((tm, tn), lambda i, j, k: (i, j)), scratch=[pltpu.VMEM((tm, tn), F32)],
                 sem=("parallel", "parallel", "arbitrary"))(a, b)


def _sigmoid(x):
    return 1.0 / (1.0 + jnp.exp(-x))


def _rope_group(r, c, sa, sb):
    return r * c + pltpu.roll(r, 32, 1) * sa + pltpu.roll(r, 96, 1) * sb


def _ln_fwd(xs, coefs, g, b, name, want_z):
    S, D = xs[0].shape
    tm = 256
    n = len(xs)

    def body(*refs):
        x_refs, g_ref, b_ref, outs = refs[:n], refs[n], refs[n + 1], refs[n + 2:]
        z = None
        for cf, r in zip(coefs, x_refs):
            t = r[...] if cf == 1.0 else cf * r[...]
            z = t if z is None else z + t
        mu = jnp.mean(z, axis=-1, keepdims=True)
        zc = z - mu
        var = jnp.mean(zc * zc, axis=-1, keepdims=True)
        y = zc * lax.rsqrt(var + LN_EPS) * g_ref[...] + b_ref[...]
        if want_z:
            outs[0][...] = z
        outs[-2][...] = y
        outs[-1][...] = y.astype(BF16)

    out_shape = [_sds((S, D), F32)] * (2 if want_z else 1) + [_sds((S, D), BF16)]
    return _call(body, name, out_shape, (S // tm,), [_rows(tm, D)] * n + [_whole((1, D))] * 2,
                 [_rows(tm, D)] * len(out_shape), sem=("parallel",))(*xs, g, b)


def _ln_bwd(dys, coefs, z, g, name):
    S, D = z.shape
    tm = 256
    n = len(dys)

    def body(*refs):
        dy_refs, z_ref, g_ref = refs[:n], refs[n], refs[n + 1]
        dz_ref, dzb_ref, dg_ref, db_ref = refs[n + 2:]
        dy = None
        for cf, r in zip(coefs, dy_refs):
            t = r[...] if cf == 1.0 else cf * r[...]
            dy = t if dy is None else dy + t
        zv = z_ref[...]
        mu = jnp.mean(zv, axis=-1, keepdims=True)
        zc = zv - mu
        var = jnp.mean(zc * zc, axis=-1, keepdims=True)
        rstd = lax.rsqrt(var + LN_EPS)
        xh = zc * rstd
        dyg = dy * g_ref[...]
        dz = rstd * (dyg - jnp.mean(dyg, axis=-1, keepdims=True) - xh * jnp.mean(dyg * xh, axis=-1, keepdims=True))
        dz_ref[...] = dz
        dzb_ref[...] = dz.astype(BF16)

        @pl.when(pl.program_id(0) == 0)
        def _():
            dg_ref[...] = jnp.zeros_like(dg_ref)
            db_ref[...] = jnp.zeros_like(db_ref)

        dg_ref[...] += jnp.sum(dy * xh, axis=0, keepdims=True)
        db_ref[...] += jnp.sum(dy, axis=0, keepdims=True)

    return _call(body, name, [_sds((S, D), F32), _sds((S, D), BF16), _sds((1, D), F32), _sds((1, D), F32)],
                 (S // tm,), [_rows(tm, D)] * (n + 1) + [_whole((1, D))],
                 [_rows(tm, D), _rows(tm, D), _whole((1, D)), _whole((1, D))], sem=("arbitrary",))(*dys, z, g)


def _rms(x, g):
    return x * lax.rsqrt(jnp.mean(x * x, axis=-1, keepdims=True) + RMS_EPS) * g


def _prep1(h, tabs, qg, kvg, name):
    S = h.shape[0]
    tm = 256
    cm, sam, sbm, cr, sr = tabs

    def body(h_ref, cm_ref, sam_ref, sbm_ref, cr_ref, sr_ref, qg_ref, kvg_ref,
             qn_ref, kvn_ref, kr_ref, rq_ref, rk_ref, rv_ref):
        qn_ref[...] = _rms(h_ref[:, 0:Q_LORA], qg_ref[...]).astype(BF16)
        kvn_ref[...] = _rms(h_ref[:, Q_LORA:Q_LORA + KV_LORA], kvg_ref[...]).astype(BF16)
        kr_ref[...] = _rope_group(h_ref[:, 768:896], cm_ref[...], sam_ref[...], sbm_ref[...])
        c, s = cr_ref[...], sr_ref[...]
        for hd in range(RET_HEADS):
            for src, dst, scale in ((MLA_IN, rq_ref, RET_SCALE), (MLA_IN + 1024, rk_ref, None)):
                t1 = h_ref[:, src + hd * 256:src + hd * 256 + 128]
                t2 = h_ref[:, src + hd * 256 + 128:src + hd * 256 + 256]
                o1, o2 = t1 * c - t2 * s, t2 * c + t1 * s
                if scale is not None:
                    o1, o2 = o1 * scale, o2 * scale
                dst[:, hd * 256:hd * 256 + 128] = o1.astype(BF16)
                dst[:, hd * 256 + 128:hd * 256 + 256] = o2.astype(BF16)
        rv_ref[...] = h_ref[:, MLA_IN + 2048:MLA_IN + 3072].astype(BF16)

    t128 = _rows(tm, LANES)
    return _call(body, name,
                 [_sds((S, Q_LORA), BF16), _sds((S, KV_LORA), BF16), _sds((S, LANES), F32),
                  _sds((S, 1024), BF16), _sds((S, 1024), BF16), _sds((S, 1024), BF16)],
                 (S // tm,),
                 [_rows(tm, D_IN_PAD), t128, t128, t128, t128, t128, _whole((1, Q_LORA)), _whole((1, KV_LORA))],
                 [_rows(tm, Q_LORA), _rows(tm, KV_LORA), t128, _rows(tm, 1024), _rows(tm, 1024), _rows(tm, 1024)],
                 sem=("parallel",))(h, cm, sam, sbm, cr, sr, qg, kvg)


def _prep1_bwd(dqn, dkvn, dkr, drq, drk, drv, drg, h, tabs, qg, kvg, name):
    S = h.shape[0]
    tm = 256
    cm, sam, sbm, cr, sr = tabs

    def rms_bwd(x, g, dy):
        r = lax.rsqrt(jnp.mean(x * x, axis=-1, keepdims=True) + RMS_EPS)
        dyg = dy * g
        dx = r * dyg - x * (r * r * r) * jnp.mean(dyg * x, axis=-1, keepdims=True)
        return dx, jnp.sum(dy * x * r, axis=0, keepdims=True)

    def body(dqn_ref, dkvn_ref, dkr_ref, drq_ref, drk_ref, drv_ref, drg_ref, h_ref,
             cm_ref, sam_ref, sbm_ref, cr_ref, sr_ref, qg_ref, kvg_ref, dh_ref, dqg_ref, dkvg_ref):
        dcq, dqg = rms_bwd(h_ref[:, 0:Q_LORA], qg_ref[...], dqn_ref[...])
        dckv, dkvg = rms_bwd(h_ref[:, Q_LORA:Q_LORA + KV_LORA], kvg_ref[...], dkvn_ref[...])
        dh_ref[:, 0:Q_LORA] = dcq.astype(BF16)
        dh_ref[:, Q_LORA:Q_LORA + KV_LORA] = dckv.astype(BF16)
        dh_ref[:, 768:896] = _rope_group(dkr_ref[...], cm_ref[...], -sam_ref[...], -sbm_ref[...]).astype(BF16)
        dh_ref[:, 896:1024] = jnp.zeros((tm, LANES), BF16)
        c, s = cr_ref[...], sr_ref[...]
        for hd in range(RET_HEADS):
            for src, dst, scale in ((drq_ref, MLA_IN, RET_SCALE), (drk_ref, MLA_IN + 1024, None)):
                d1 = src[:, hd * 256:hd * 256 + 128]
                d2 = src[:, hd * 256 + 128:hd * 256 + 256]
                if scale is not None:
                    d1, d2 = d1 * scale, d2 * scale
                dh_ref[:, dst + hd * 256:dst + hd * 256 + 128] = (d1 * c + d2 * s).astype(BF16)
                dh_ref[:, dst + hd * 256 + 128:dst + hd * 256 + 256] = (d2 * c - d1 * s).astype(BF16)
        dh_ref[:, MLA_IN + 2048:MLA_IN + 3072] = drv_ref[...].astype(BF16)
        dh_ref[:, MLA_IN + 3072:MLA_IN + 4096] = drg_ref[...].astype(BF16)

        @pl.when(pl.program_id(0) == 0)
        def _():
            dqg_ref[...] = jnp.zeros_like(dqg_ref)
            dkvg_ref[...] = jnp.zeros_like(dkvg_ref)

        dqg_ref[...] += dqg
        dkvg_ref[...] += dkvg

    t128 = _rows(tm, LANES)
    return _call(body, name,
                 [_sds((S, D_IN_PAD), BF16), _sds((1, Q_LORA), F32), _sds((1, KV_LORA), F32)],
                 (S // tm,),
                 [_rows(tm, Q_LORA), _rows(tm, KV_LORA), t128, _rows(tm, 1024), _rows(tm, 1024), _rows(tm, 1024),
                  _rows(tm, 1024), _rows(tm, D_IN_PAD), t128, t128, t128, t128, t128,
                  _whole((1, Q_LORA)), _whole((1, KV_LORA))],
                 [_rows(tm, D_IN_PAD), _whole((1, Q_LORA)), _whole((1, KV_LORA))],
                 sem=("arbitrary",))(dqn, dkvn, dkr, drq, drk, drv, drg, h, cm, sam, sbm, cr, sr, qg, kvg)


def _prep2(q, kv, kr, tabs, name):
    S = q.shape[0]
    tm = 256
    cm, sam, sbm = tabs[:3]

    def body(q_ref, kv_ref, kr_ref, cm_ref, sam_ref, sbm_ref, qo_ref, ko_ref, vo_ref):
        c, sa, sb = cm_ref[...], sam_ref[...], sbm_ref[...]
        krb = kr_ref[...].astype(BF16)
        for hd in range(MLA_HEADS):
            o = hd * HEAD_PAD
            qo_ref[:, o:o + 128] = q_ref[:, o:o + 128].astype(BF16)
            qo_ref[:, o + 128:o + 256] = _rope_group(q_ref[:, o + 128:o + 256], c, sa, sb).astype(BF16)
            ko_ref[:, o:o + 128] = kv_ref[:, hd * 128:hd * 128 + 128].astype(BF16)
            ko_ref[:, o + 128:o + 256] = krb
        vo_ref[...] = kv_ref[:, 1024:2048].astype(BF16)

    t128 = _rows(tm, LANES)
    return _call(body, name, [_sds((S, 2048), BF16), _sds((S, 2048), BF16), _sds((S, 1024), BF16)], (S // tm,),
                 [_rows(tm, 2048), _rows(tm, 2048), t128, t128, t128, t128],
                 [_rows(tm, 2048), _rows(tm, 2048), _rows(tm, 1024)], sem=("parallel",))(q, kv, kr, cm, sam, sbm)


def _prep2_bwd(dqm, dkm, dvm, tabs, name):
    S = dqm.shape[0]
    tm = 256
    cm, sam, sbm = tabs[:3]

    def body(dq_ref, dk_ref, dv_ref, cm_ref, sam_ref, sbm_ref, dqo_ref, dkvo_ref, dkr_ref):
        c, sa, sb = cm_ref[...], -sam_ref[...], -sbm_ref[...]
        dkr = None
        for hd in range(MLA_HEADS):
            o = hd * HEAD_PAD
            dqo_ref[:, o:o + 128] = dq_ref[:, o:o + 128].astype(BF16)
            dqo_ref[:, o + 128:o + 256] = _rope_group(dq_ref[:, o + 128:o + 256], c, sa, sb).astype(BF16)
            dkvo_ref[:, hd * 128:hd * 128 + 128] = dk_ref[:, o:o + 128].astype(BF16)
            t = dk_ref[:, o + 128:o + 256]
            dkr = t if dkr is None else dkr + t
        dkvo_ref[:, 1024:2048] = dv_ref[...].astype(BF16)
        dkr_ref[...] = dkr

    t128 = _rows(tm, LANES)
    return _call(body, name, [_sds((S, 2048), BF16), _sds((S, 2048), BF16), _sds((S, LANES), F32)], (S // tm,),
                 [_rows(tm, 2048), _rows(tm, 2048), _rows(tm, 1024), t128, t128, t128],
                 [_rows(tm, 2048), _rows(tm, 2048), t128], sem=("parallel",))(dqm, dkm, dvm, cm, sam, sbm)


def _gn_gate(a, o, h, gg, gb, name):
    S = a.shape[0]
    tm = 256

    def body(a_ref, o_ref, rg_ref, gg_ref, gb_ref, mix_ref):
        mix_ref[:, 0:1024] = a_ref[...].astype(BF16)
        for hd in range(RET_HEADS):
            sl = slice(hd * 256, hd * 256 + 256)
            ov = o_ref[:, sl]
            mu = jnp.mean(ov, axis=-1, keepdims=True)
            oc = ov - mu
            var = jnp.mean(oc * oc, axis=-1, keepdims=True)
            y = oc * lax.rsqrt(var + GN_EPS) * gg_ref[:, sl] + gb_ref[:, sl]
            rg = rg_ref[:, sl]
            mix_ref[:, 1024 + hd * 256:1024 + hd * 256 + 256] = (rg * _sigmoid(rg) * y).astype(BF16)

    return _call(body, name, _sds((S, 2048), BF16), (S // tm,),
                 [_rows(tm, 1024), _rows(tm, 1024), _rows(tm, 1024, 4), _whole((1, 1024)), _whole((1, 1024))],
                 _rows(tm, 2048), sem=("parallel",))(a, o, h, gg, gb)


def _gn_gate_bwd(dmixin, o, h, gg, gb, name):
    S = o.shape[0]
    tm = 256

    def body(dr_ref, o_ref, rg_ref, gg_ref, gb_ref, do_ref, drg_ref, dgg_ref, dgb_ref):
        @pl.when(pl.program_id(0) == 0)
        def _():
            dgg_ref[...] = jnp.zeros_like(dgg_ref)
            dgb_ref[...] = jnp.zeros_like(dgb_ref)

        for hd in range(RET_HEADS):
            sl = slice(hd * 256, hd * 256 + 256)
            ov = o_ref[:, sl]
            mu = jnp.mean(ov, axis=-1, keepdims=True)
            oc = ov - mu
            var = jnp.mean(oc * oc, axis=-1, keepdims=True)
            rstd = lax.rsqrt(var + GN_EPS)
            xh = oc * rstd
            g = gg_ref[:, sl]
            y = xh * g + gb_ref[:, sl]
            rg = rg_ref[:, sl]
            sg = _sigmoid(rg)
            dr = dr_ref[:, sl]
            dy = dr * (rg * sg)
            drg_ref[:, sl] = dr * y * (sg * (1.0 + rg * (1.0 - sg)))
            dgg_ref[:, sl] += jnp.sum(dy * xh, axis=0, keepdims=True)
            dgb_ref[:, sl] += jnp.sum(dy, axis=0, keepdims=True)
            dxh = dy * g
            do = rstd * (dxh - jnp.mean(dxh, axis=-1, keepdims=True) - xh * jnp.mean(dxh * xh, axis=-1, keepdims=True))
            do_ref[:, sl] = do.astype(BF16)

    return _call(body, name,
                 [_sds((S, 1024), BF16), _sds((S, 1024), F32), _sds((1, 1024), F32), _sds((1, 1024), F32)],
                 (S // tm,),
                 [_rows(tm, 1024, 1), _rows(tm, 1024), _rows(tm, 1024, 4), _whole((1, 1024)), _whole((1, 1024))],
                 [_rows(tm, 1024), _rows(tm, 1024), _whole((1, 1024)), _whole((1, 1024))],
                 sem=("arbitrary",))(dmixin, o, h, gg, gb)


def _swiglu(gu, name):
    S = gu.shape[0]
    tm = 256

    def body(g_ref, u_ref, o_ref):
        g = g_ref[...]
        o_ref[...] = (g * _sigmoid(g) * u_ref[...]).astype(BF16)

    return _call(body, name, _sds((S, D_FF), BF16), (S // tm,), [_rows(tm, D_FF, 0), _rows(tm, D_FF, 1)],
                 _rows(tm, D_FF), sem=("parallel",))(gu, gu)


def _swiglu_bwd(gu, dact, name):
    S = gu.shape[0]
    tm = 128

    def body(g_ref, u_ref, d_ref, o_ref):
        g, u, d = g_ref[...], u_ref[...], d_ref[...]
        sg = _sigmoid(g)
        o_ref[:, 0:D_FF] = (d * u * (sg * (1.0 + g * (1.0 - sg)))).astype(BF16)
        o_ref[:, D_FF:2 * D_FF] = (d * (g * sg)).astype(BF16)

    return _call(body, name, _sds((S, 2 * D_FF), BF16), (S // tm,),
                 [_rows(tm, D_FF, 0), _rows(tm, D_FF, 1), _rows(tm, D_FF)], _rows(tm, 2 * D_FF),
                 sem=("parallel",))(gu, gu, dact)


def _loss_head(y, target, name):
    S, D = y.shape
    tm = 256

    def body(y_ref, t_ref, dy_ref, acc_ref):
        e = y_ref[...] - t_ref[...]
        dy_ref[...] = e / D

        @pl.when(pl.program_id(0) == 0)
        def _():
            acc_ref[...] = jnp.zeros_like(acc_ref)

        acc_ref[...] += jnp.sum(e * e, axis=0, keepdims=True)

    return _call(body, name, [_sds((S, D), F32), _sds((1, D), F32)], (S // tm,), [_rows(tm, D), _rows(tm, D)],
                 [_rows(tm, D), _whole((1, D))], sem=("arbitrary",))(y, target)


def _chunk_mask(T):
    r = lax.shift_right_logical(lax.broadcasted_iota(jnp.int32, (T, T), 0), 6)
    c = lax.shift_right_logical(lax.broadcasted_iota(jnp.int32, (T, T), 1), 6)
    return r >= c


def _dot_nt(a, b):
    return lax.dot_general(a, b, (((1,), (1,)), ((), ())), preferred_element_type=F32)


def _dot_tn(a, b):
    return lax.dot_general(a, b, (((0,), (0,)), ((), ())), preferred_element_type=F32)


def _decay_tables(T):
    lg = jnp.log1p(-jnp.exp2(-5.0 - jnp.arange(RET_HEADS, dtype=F32)))
    idx = jnp.arange(T, dtype=F32)
    diff = idx[:, None] - idx[None, :]
    rel = jnp.exp(lg[:, None, None] * diff[None])
    cid = jnp.arange(T) // CHUNK
    mask = (cid[:, None] >= cid[None, :]).astype(F32)
    reld = jnp.exp(lg[:, None, None] * jnp.abs(diff)[None]) * mask[None]
    lgrow = jnp.broadcast_to(lg[:, None, None], (RET_HEADS, 1, LANES))
    return lgrow, rel, reld


def _attn_fwd(q, k, v, heads, dk, dv, softmax, name, tables=None):
    S = q.shape[0]
    T = ATT_BLOCK
    nq = S // T
    rep = T // LANES

    def body(*refs):
        if softmax:
            q_ref, k_ref, v_ref, o_ref, lse_ref, m_sc, l_sc, acc_sc = refs
        else:
            q_ref, k_ref, v_ref, lg_ref, rel_ref, reld_ref, o_ref, acc_sc = refs
        i = pl.program_id(1)
        qv = q_ref[...]

        def kv_block(j):
            rows = pl.ds(pl.multiple_of(j * T, T), T)
            return k_ref[rows, :], v_ref[rows, :]

        kb, vb = kv_block(i)
        s = _dot_nt(qv, kb)
        if softmax:
            s = jnp.where(_chunk_mask(T), s * MLA_SCALE, NEG)
            m = jnp.max(s, axis=-1, keepdims=True)
            p = jnp.exp(s - m)
            m_sc[...] = jnp.broadcast_to(m, (T, LANES))
            l_sc[...] = jnp.broadcast_to(jnp.sum(p, axis=-1, keepdims=True), (T, LANES))
        else:
            p = s * reld_ref[0]
        acc_sc[...] = jnp.dot(p.astype(BF16), vb, preferred_element_type=F32)

        def step(j, carry):
            kb, vb = kv_block(j)
            s = _dot_nt(qv, kb)
            if softmax:
                s = s * MLA_SCALE
                m_prev = m_sc[...]
                m_next = jnp.maximum(m_prev, jnp.max(s, axis=-1, keepdims=True))
                alpha = jnp.exp(m_prev - m_next)
                p = jnp.exp(s - jnp.tile(m_next, (1, rep)))
                l_sc[...] = alpha * l_sc[...] + jnp.sum(p, axis=-1, keepdims=True)
                m_sc[...] = m_next
                acc_sc[...] = acc_sc[...] * jnp.tile(alpha, (1, dv // LANES)) + jnp.dot(
                    p.astype(BF16), vb, preferred_element_type=F32)
            else:
                fac = jnp.exp(lg_ref[0] * ((i - j) * T).astype(F32))
                p = s * (rel_ref[0] * jnp.tile(fac, (1, rep)))
                acc_sc[...] += jnp.dot(p.astype(BF16), vb, preferred_element_type=F32)
            return carry

        lax.fori_loop(0, i, step, 0)
        if softmax:
            l = l_sc[...]
            o_ref[...] = acc_sc[...] / jnp.tile(l, (1, dv // LANES))
            lse_ref[...] = m_sc[...] + jnp.log(l)
        else:
            o_ref[...] = acc_sc[...]

    in_specs = [pl.BlockSpec((T, dk), lambda h, i: (i, h)), pl.BlockSpec((S, dk), lambda h, i: (0, h)),
                pl.BlockSpec((S, dv), lambda h, i: (0, h))]
    o_spec = pl.BlockSpec((T, dv), lambda h, i: (i, h))
    if softmax:
        return _call(body, name, [_sds((S, heads * dv), F32), _sds((S, heads * LANES), F32)], (heads, nq), in_specs,
                     [o_spec, pl.BlockSpec((T, LANES), lambda h, i: (i, h))],
                     scratch=[pltpu.VMEM((T, LANES), F32), pltpu.VMEM((T, LANES), F32), pltpu.VMEM((T, dv), F32)],
                     sem=("parallel", "arbitrary"))(q, k, v)
    lgrow, rel, reld = tables
    in_specs += [pl.BlockSpec((1, 1, LANES), lambda h, i: (h, 0, 0)), pl.BlockSpec((1, T, T), lambda h, i: (h, 0, 0)),
                 pl.BlockSpec((1, T, T), lambda h, i: (h, 0, 0))]
    return _call(body, name, _sds((S, heads * dv), F32), (heads, nq), in_specs, o_spec,
                 scratch=[pltpu.VMEM((T, dv), F32)], sem=("parallel", "arbitrary"))(q, k, v, lgrow, rel, reld)


def _attn_bwd(q, k, v, do, heads, dk, dv, softmax, name, o=None, lse=None, tables=None):
    S = q.shape[0]
    T = ATT_BLOCK
    nq = S // T
    rep = T // LANES

    def body(*refs):
        if softmax:
            q_ref, k_ref, v_ref, do_ref, o_ref, lse_ref, dq_ref, dk_ref, dv_ref, dq_sc = refs
        else:
            q_ref, k_ref, v_ref, do_ref, lg_ref, rel_ref, reld_ref, dq_ref, dk_ref, dv_ref, dq_sc = refs
        i = pl.program_id(1)

        @pl.when(i == 0)
        def _():
            dk_ref[...] = jnp.zeros_like(dk_ref)
            dv_ref[...] = jnp.zeros_like(dv_ref)

        qv = q_ref[...]
        dof = do_ref[...].astype(F32)
        dov = dof.astype(BF16)
        if softmax:
            delta = jnp.sum(dof * o_ref[...], axis=-1, keepdims=True)
            lse_t = jnp.tile(lse_ref[...], (1, rep))
        dq_sc[...] = jnp.zeros_like(dq_sc)

        def block(j, diagonal):
            rows = pl.ds(pl.multiple_of(j * T, T), T)
            kb, vb = k_ref[rows, :], v_ref[rows, :]
            s = _dot_nt(qv, kb)
            dp = _dot_nt(dov, vb)
            if softmax:
                s = s * MLA_SCALE
                if diagonal:
                    s = jnp.where(_chunk_mask(T), s, NEG)
                p = jnp.exp(s - lse_t)
                ds = p * (dp - delta) * MLA_SCALE
            else:
                if diagonal:
                    dec = reld_ref[0]
                else:
                    fac = jnp.exp(lg_ref[0] * ((i - j) * T).astype(F32))
                    dec = rel_ref[0] * jnp.tile(fac, (1, rep))
                p = s * dec
                ds = dp * dec
            dsb = ds.astype(BF16)
            dv_ref[rows, :] += _dot_tn(p.astype(BF16), dov)
            dk_ref[rows, :] += _dot_tn(dsb, qv)
            dq_sc[...] += jnp.dot(dsb, kb, preferred_element_type=F32)

        block(i, True)

        def step(j, carry):
            block(j, False)
            return carry

        lax.fori_loop(0, i, step, 0)
        dq_ref[...] = dq_sc[...]

    qspec = pl.BlockSpec((T, dk), lambda h, i: (i, h))
    kspec = pl.BlockSpec((S, dk), lambda h, i: (0, h))
    vspec = pl.BlockSpec((S, dv), lambda h, i: (0, h))
    dospec = pl.BlockSpec((T, dv), lambda h, i: (i, h))
    in_specs = [qspec, kspec, vspec, dospec]
    args = [q, k, v, do]
    if softmax:
        in_specs += [dospec, pl.BlockSpec((T, LANES), lambda h, i: (i, h))]
        args += [o, lse]
    else:
        in_specs += [pl.BlockSpec((1, 1, LANES), lambda h, i: (h, 0, 0)),
                     pl.BlockSpec((1, T, T), lambda h, i: (h, 0, 0)), pl.BlockSpec((1, T, T), lambda h, i: (h, 0, 0))]
        args += list(tables)
    return _call(body, name, [_sds((S, heads * dk), F32), _sds((S, heads * dk), F32), _sds((S, heads * dv), F32)],
                 (heads, nq), in_specs, [qspec, kspec, vspec], scratch=[pltpu.VMEM((T, dk), F32)],
                 sem=("parallel", "arbitrary"))(*args)


def _rope_tables(pos):
    def tables(dim):
        inv_freq = ROPE_THETA ** (-jnp.arange(0, dim, 2, dtype=F32) / dim)
        ang = pos.astype(F32)[:, None] * inv_freq
        return jnp.cos(ang), jnp.sin(ang)

    cm, sm = tables(ROPE)
    S = pos.shape[0]
    z32, z64 = jnp.zeros((S, 32), F32), jnp.zeros((S, 64), F32)
    cr, sr = tables(RET_DK)
    return (jnp.concatenate([cm, cm, z64], 1), jnp.concatenate([z32, sm, z64], 1),
            jnp.concatenate([-sm, z32, z64], 1), cr, sr)


def _row(v):
    return v.reshape(1, -1).astype(F32)


def _local_step(x, pos, target, W, P):
    tabs = _rope_tables(pos)
    dtabs = _decay_tables(ATT_BLOCK)
    xf, xb = _ln_fwd([x], [1.0], _row(P["ln_in_g"]), _row(P["ln_in_b"]), "ln_in", False)
    saved = []
    for l in range(DEPTH):
        w = W[l]
        t = f"_l{l}"
        h = _matmul(xb, w["w_in"], "mm_h" + t)
        qn, kvn, kr, rq, rk, rv = _prep1(h, tabs, _row(P["q_norm_g"][l]), _row(P["kv_norm_g"][l]), "prep1" + t)
        q = _matmul(qn, w["w_uq"], "mm_q" + t)
        kv = _matmul(kvn, w["w_ukv"], "mm_kv" + t)
        qm, km, vm = _prep2(q, kv, kr, tabs, "prep2" + t)
        a, lse = _attn_fwd(qm, km, vm, MLA_HEADS, HEAD_PAD, VDIM, True, "mla_fwd" + t)
        o = _attn_fwd(rq, rk, rv, RET_HEADS, RET_DK, RET_DV, False, "ret_fwd" + t, tables=dtabs)
        mixin = _gn_gate(a, o, h, _row(P["ret_gn_g"][l]), _row(P["ret_gn_b"][l]), "gn_gate" + t)
        mix = _matmul(mixin, w["w_out"], "mm_mix" + t)
        z1, x1f, x1b = _ln_fwd([xf, mix], [ALPHA, 1.0], _row(P["ln1_g"][l]), _row(P["ln1_b"][l]), "ln1" + t, True)
        gu = _matmul(x1b, w["w_gu"], "mm_gu" + t)
        act = _swiglu(gu, "swiglu" + t)
        f = _matmul(act, w["w_down"], "mm_down" + t)
        z2, x2f, x2b = _ln_fwd([x1f, f], [ALPHA, 1.0], _row(P["ln2_g"][l]), _row(P["ln2_b"][l]), "ln2" + t, True)
        saved.append(dict(xb=xb, h=h, qn=qn, kvn=kvn, rq=rq, rk=rk, rv=rv, qm=qm, km=km, vm=vm, a=a, lse=lse, o=o,
                          mixin=mixin, z1=z1, x1b=x1b, gu=gu, act=act, z2=z2))
        xf, xb = x2f, x2b

    dy, sqerr = _loss_head(xf, target, "loss_head")
    dW = [None] * DEPTH
    dP = {}
    dys, coefs = [dy], [1.0]
    for l in reversed(range(DEPTH)):
        w, sv = W[l], saved[l]
        t = f"_l{l}"
        dz2, dz2b, dg, db = _ln_bwd(dys, coefs, sv["z2"], _row(P["ln2_g"][l]), "ln2_bwd" + t)
        dP[("ln2_g", l)], dP[("ln2_b", l)] = dg, db
        g = {}
        g["w_down"] = _matmul(sv["act"], dz2b, "mm_dw_down" + t, ta=True, out_dtype=BF16)
        dact = _matmul(dz2b, w["w_down"], "mm_dact" + t, tb=True)
        dgu = _swiglu_bwd(sv["gu"], dact, "swiglu_bwd" + t)
        g["w_gu"] = _matmul(sv["x1b"], dgu, "mm_dw_gu" + t, ta=True, out_dtype=BF16)
        dx1 = _matmul(dgu, w["w_gu"], "mm_dx1" + t, tb=True)
        dz1, dz1b, dg, db = _ln_bwd([dz2, dx1], [ALPHA, 1.0], sv["z1"], _row(P["ln1_g"][l]), "ln1_bwd" + t)
        dP[("ln1_g", l)], dP[("ln1_b", l)] = dg, db
        g["w_out"] = _matmul(sv["mixin"], dz1b, "mm_dw_out" + t, ta=True, out_dtype=BF16)
        dmixin = _matmul(dz1b, w["w_out"], "mm_dmixin" + t, tb=True)
        do, drg, dgg, dgb = _gn_gate_bwd(dmixin, sv["o"], sv["h"], _row(P["ret_gn_g"][l]), _row(P["ret_gn_b"][l]),
                                         "gn_gate_bwd" + t)
        dP[("ret_gn_g", l)], dP[("ret_gn_b", l)] = dgg, dgb
        drq, drk, drv = _attn_bwd(sv["rq"], sv["rk"], sv["rv"], do, RET_HEADS, RET_DK, RET_DV, False,
                                  "ret_bwd" + t, tables=dtabs)
        dqm, dkm, dvm = _attn_bwd(sv["qm"], sv["km"], sv["vm"], dmixin, MLA_HEADS, HEAD_PAD, VDIM, True,
                                  "mla_bwd" + t, o=sv["a"], lse=sv["lse"])
        dq, dkv, dkr = _prep2_bwd(dqm, dkm, dvm, tabs, "prep2_bwd" + t)
        g["w_uq"] = _matmul(sv["qn"], dq, "mm_dw_uq" + t, ta=True, out_dtype=BF16)
        dqn = _matmul(dq, w["w_uq"], "mm_dqn" + t, tb=True)
        g["w_ukv"] = _matmul(sv["kvn"], dkv, "mm_dw_ukv" + t, ta=True, out_dtype=BF16)
        dkvn = _matmul(dkv, w["w_ukv"], "mm_dkvn" + t, tb=True)
        dh, dqg, dkvg = _prep1_bwd(dqn, dkvn, dkr, drq, drk, drv, drg, sv["h"], tabs, _row(P["q_norm_g"][l]),
                                   _row(P["kv_norm_g"][l]), "prep1_bwd" + t)
        dP[("q_norm_g", l)], dP[("kv_norm_g", l)] = dqg, dkvg
        g["w_in"] = _matmul(sv["xb"], dh, "mm_dw_in" + t, ta=True, out_dtype=BF16)
        dxl = _matmul(dh, w["w_in"], "mm_dxl" + t, tb=True)
        dW[l] = g
        dys, coefs = [dz1, dxl], [ALPHA, 1.0]
    grad_x, _, dg, db = _ln_bwd(dys, coefs, x, _row(P["ln_in_g"]), "ln_in_bwd")
    dP[("ln_in_g", None)], dP[("ln_in_b", None)] = dg, db
    return sqerr, grad_x, dW, dP


def _internal_weights(G, l):
    cat = lambda parts: jnp.concatenate(parts, axis=1)
    cols = lambda n: cat([G[n][j, l] for j in range(N_CHIPS)])
    rows = lambda n: jnp.concatenate([G[n][j, l] for j in range(N_CHIPS)], axis=0)
    first = G["w_in"][0, l]
    w_in = cat([first[:, :MLA_IN_USED], jnp.zeros((D_MODEL, MLA_IN - MLA_IN_USED), BF16), first[:, MLA_IN_USED:]]
               + [G["w_in"][j, l] for j in range(1, N_CHIPS)])
    uq, ukv = cols("w_uq"), cols("w_ukv")
    hw = NOPE + ROPE
    pad = jnp.zeros((Q_LORA, HEAD_PAD - hw), BF16)
    w_uq = cat([p for h in range(MLA_HEADS) for p in (uq[:, h * hw:(h + 1) * hw], pad)])
    w_ukv = cat([ukv[:, 256 * h:256 * h + NOPE] for h in range(MLA_HEADS)]
                + [ukv[:, 256 * h + NOPE:256 * h + 256] for h in range(MLA_HEADS)])
    return dict(w_in=w_in, w_uq=w_uq, w_ukv=w_ukv, w_out=rows("w_out"),
                w_gu=cat([G["w_gate"][j, l] for j in range(N_CHIPS)] + [G["w_up"][j, l] for j in range(N_CHIPS)]),
                w_down=rows("w_down"))


def _grad_shards(g):
    cat = lambda parts: jnp.concatenate(parts, axis=1)
    wi, uq, ukv, gu = g["w_in"], g["w_uq"], g["w_ukv"], g["w_gu"]
    ci, cq, cg = BIG_SHARD["w_in"][1], NOPE + ROPE, BIG_SHARD["w_gate"][1]
    shift = MLA_IN - MLA_IN_USED
    w_in = [cat([wi[:, :MLA_IN_USED], wi[:, MLA_IN:ci + shift]])]
    w_in += [wi[:, ci * j + shift:ci * (j + 1) + shift] for j in range(1, N_CHIPS)]
    w_uq = [cat([uq[:, HEAD_PAD * h:HEAD_PAD * h + cq] for h in (2 * j, 2 * j + 1)]) for j in range(N_CHIPS)]
    w_ukv = [cat([ukv[:, o + NOPE * h:o + NOPE * (h + 1)] for h in (2 * j, 2 * j + 1) for o in (0, MLA_HEADS * NOPE)])
             for j in range(N_CHIPS)]
    ro, rd = BIG_SHARD["w_out"][0], BIG_SHARD["w_down"][0]
    return dict(w_in=w_in, w_uq=w_uq, w_ukv=w_ukv,
                w_out=[g["w_out"][ro * j:ro * (j + 1)] for j in range(N_CHIPS)],
                w_gate=[gu[:, cg * j:cg * (j + 1)] for j in range(N_CHIPS)],
                w_up=[gu[:, D_FF + cg * j:D_FF + cg * (j + 1)] for j in range(N_CHIPS)],
                w_down=[g["w_down"][rd * j:rd * (j + 1)] for j in range(N_CHIPS)])


def _small_layout(P):
    out, at = {}, 0
    for n in SMALL:
        out[n] = (at, P[n].size)
        at += P[n].size
    return out, at


def _flatten_small(P, last):
    v = jnp.concatenate([P[n].reshape(-1).astype(F32) for n in SMALL] + [last.reshape(-1).astype(F32)])
    return jnp.pad(v, (0, SMALL_ROWS * FLAT_W - v.size)).reshape(SMALL_ROWS, FLAT_W)


HBM = pl.BlockSpec(memory_space=pltpu.HBM)


def _place():
    return lax.axis_index("x"), lax.axis_index("y"), lax.axis_index("c")


def _other_chips(x, y):
    return [(1 - x, y), (x, 1 - y), (1 - x, 1 - y)]


def _rcopy(src, dst, ssem, rsem, dev):
    return pltpu.make_async_remote_copy(src_ref=src, dst_ref=dst, send_sem=ssem, recv_sem=rsem, device_id=dev,
                                        device_id_type=MESH)


def _comm_call(body, name, out_shape, n_in, scratch):
    many = isinstance(out_shape, (list, tuple))
    return pl.pallas_call(body, name=name, out_shape=out_shape, in_specs=[HBM] * n_in,
                          out_specs=[HBM] * len(out_shape) if many else HBM, scratch_shapes=scratch)


NB = len(BIG)


def _allgather_weights(ws):
    def body(*refs):
        w_refs, g_refs = refs[:NB], refs[NB:2 * NB]
        ssem, rsem, fssem, frsem, lsem = refs[2 * NB:]
        x, y, c = _place()
        j = 2 * x + y
        sib = (x, y, 1 - c)
        chips = _other_chips(x, y)
        owns = [pltpu.make_async_copy(w_refs[n], g_refs[n].at[j], lsem.at[n]) for n in range(NB)]
        sends = [_rcopy(w_refs[n].at[c], g_refs[n].at[j, c], ssem.at[3 * n + t], rsem.at[3 * n + t], (cx, cy, c))
                 for n in range(NB) for t, (cx, cy) in enumerate(chips)]
        for cp in owns + sends:
            cp.start()
        passed = []
        for n in range(NB):
            for t, (cx, cy) in enumerate(chips):
                blk = g_refs[n].at[2 * cx + cy, c]
                _rcopy(blk, blk, ssem.at[3 * n + t], rsem.at[3 * n + t], (cx, cy, c)).wait_recv()
                cp = _rcopy(blk, blk, fssem.at[3 * n + t], frsem.at[3 * n + t], sib)
                cp.start()
                passed.append(cp)
        for n in range(NB):
            for t, (cx, cy) in enumerate(chips):
                blk = g_refs[n].at[2 * cx + cy, 1 - c]
                _rcopy(blk, blk, fssem.at[3 * n + t], frsem.at[3 * n + t], sib).wait_recv()
        for cp in sends + passed:
            cp.wait_send()
        for cp in owns:
            cp.wait()

    sems = pltpu.SemaphoreType.DMA((3 * NB,))
    return _comm_call(body, "allgather_weights", [_sds((N_CHIPS,) + w.shape, w.dtype) for w in ws], NB,
                      [sems, sems, sems, sems, pltpu.SemaphoreType.DMA((NB,))])(*ws)


def _swap_layers(gds):
    def body(*refs):
        gd_refs, out_refs, (ssem, rsem) = refs[:NB], refs[NB:2 * NB], refs[2 * NB:]
        x, y, c = _place()
        cps = [_rcopy(gd_refs[n].at[jj, 1 - c], out_refs[n].at[jj], ssem.at[N_CHIPS * n + jj],
                      rsem.at[N_CHIPS * n + jj], (x, y, 1 - c)) for n in range(NB) for jj in range(N_CHIPS)]
        for cp in cps:
            cp.start()
        for cp in cps:
            cp.wait()

    sems = pltpu.SemaphoreType.DMA((N_CHIPS * NB,))
    return _comm_call(body, "swap_layers", [_sds((N_CHIPS,) + g.shape[2:], g.dtype) for g in gds], NB,
                      [sems, sems])(*gds)


def _exchange_partials(parts, small):
    def body(*refs):
        p_refs, s_ref, rcv_refs, all_ref = refs[:NB], refs[NB], refs[NB + 1:2 * NB + 1], refs[2 * NB + 1]
        ssem, rsem, sssem, srsem, lsem = refs[2 * NB + 2:]
        x, y, c = _place()
        me = 4 * x + 2 * y + c
        own = pltpu.make_async_copy(s_ref, all_ref.at[me], lsem)
        own.start()
        cps = [_rcopy(p_refs[n].at[2 * cx + cy], rcv_refs[n].at[t], ssem.at[3 * n + t], rsem.at[3 * n + t], (cx, cy, c))
               for n in range(NB) for t, (cx, cy) in enumerate(_other_chips(x, y))]
        for cp in cps:
            cp.start()
        small_cps = []
        for r in range(1, 8):
            fx, fy, fc = (r >> 2) & 1, (r >> 1) & 1, r & 1
            px, py, pc = (1 - x if fx else x, 1 - y if fy else y, 1 - c if fc else c)
            peer = 4 * px + 2 * py + pc
            send = _rcopy(s_ref, all_ref.at[me], sssem.at[r - 1], srsem.at[me], (px, py, pc))
            send.start()
            small_cps.append((send, _rcopy(s_ref, all_ref.at[peer], sssem.at[r - 1], srsem.at[peer], (px, py, pc))))
        for cp in cps:
            cp.wait()
        for send, recv in small_cps:
            send.wait_send()
            recv.wait_recv()
        own.wait()

    sem3, sem7, sem8 = (pltpu.SemaphoreType.DMA((k,)) for k in (3 * NB, 7, 8))
    return _comm_call(body, "exchange_partials",
                      [_sds((3,) + p.shape[1:], p.dtype) for p in parts] + [_sds((8,) + small.shape, small.dtype)],
                      NB + 1, [sem3, sem3, sem7, sem8, pltpu.SemaphoreType.DMA(())])(*parts, small)


def _share_reduced(reds):
    def body(*refs):
        r_refs, out_refs, (ssem, rsem, lsem) = refs[:NB], refs[NB:2 * NB], refs[2 * NB:]
        x, y, c = _place()
        owns = [pltpu.make_async_copy(r_refs[n], out_refs[n].at[c], lsem.at[n]) for n in range(NB)]
        cps = [_rcopy(r_refs[n], out_refs[n].at[c], ssem.at[n], rsem.at[n], (x, y, 1 - c)) for n in range(NB)]
        for cp in owns + cps:
            cp.start()
        for cp in cps + owns:
            cp.wait()

    sems = pltpu.SemaphoreType.DMA((NB,))
    return _comm_call(body, "share_reduced", [_sds((DEPTH,) + r.shape, r.dtype) for r in reds], NB,
                      [sems, sems, sems])(*reds)


def _add_pair(gd, got, c, name):
    _, _, R, W = gd.shape
    tm = _pick(R, (512, 256, 128, 8))

    def body(c_ref, a_ref, b_ref, o_ref):
        o_ref[...] = (a_ref[...].astype(F32) + b_ref[...].astype(F32)).astype(o_ref.dtype)

    grid_spec = pltpu.PrefetchScalarGridSpec(
        num_scalar_prefetch=1, grid=(N_CHIPS, R // tm),
        in_specs=[pl.BlockSpec((None, None, tm, W), lambda j, i, c_ref: (j, c_ref[0], i, 0)),
                  pl.BlockSpec((None, tm, W), lambda j, i, c_ref: (j, i, 0))],
        out_specs=pl.BlockSpec((None, tm, W), lambda j, i, c_ref: (j, i, 0)))
    return pl.pallas_call(body, name=name, grid_spec=grid_spec, out_shape=_sds((N_CHIPS, R, W), gd.dtype),
                          compiler_params=pltpu.CompilerParams(dimension_semantics=("parallel", "parallel"),
                                                               vmem_limit_bytes=VMEM_LIMIT))(c, gd, got)


def _add_chips(part, rcv, j, name):
    _, R, W = part.shape
    tm = _pick(R, (512, 256, 128, 8))

    def body(j_ref, p_ref, r0_ref, r1_ref, r2_ref, o_ref):
        o_ref[...] = ((p_ref[...].astype(F32) + r0_ref[...].astype(F32)) + r1_ref[...].astype(F32)) + r2_ref[...].astype(F32)

    def slot(t):
        return pl.BlockSpec((None, tm, W), lambda i, j_ref: (t, i, 0))

    grid_spec = pltpu.PrefetchScalarGridSpec(
        num_scalar_prefetch=1, grid=(R // tm,),
        in_specs=[pl.BlockSpec((None, tm, W), lambda i, j_ref: (j_ref[0], i, 0)), slot(0), slot(1), slot(2)],
        out_specs=pl.BlockSpec((tm, W), lambda i, j_ref: (i, 0)))
    return pl.pallas_call(body, name=name, grid_spec=grid_spec, out_shape=_sds((R, W), F32),
                          compiler_params=pltpu.CompilerParams(dimension_semantics=("parallel",),
                                                               vmem_limit_bytes=VMEM_LIMIT))(j, part, rcv, rcv, rcv)


def _sum_small(allsmall):
    _, R, W = allsmall.shape

    def body(a_ref, o_ref):
        acc = a_ref[0]
        for d in range(1, 8):
            acc = acc + a_ref[d]
        o_ref[...] = acc

    return _call(body, "sum_small", _sds((R, W), F32), (1,), [_whole((8, R, W))], _whole((R, W)),
                 sem=("arbitrary",))(allsmall)


def _adamw(w, g, m, v, name):
    R, C = w.shape
    tm = _pick(R, (256, 128, 64, 32, 8))

    def body(w_ref, g_ref, m_ref, v_ref, d_ref, mo_ref, vo_ref):
        gv = g_ref[...]
        mn = ADAM_B1 * m_ref[...] + (1.0 - ADAM_B1) * gv
        vn = ADAM_B2 * v_ref[...] + (1.0 - ADAM_B2) * (gv * gv)
        m_hat = mn / (1.0 - ADAM_B1 ** ADAM_STEP)
        v_hat = vn / (1.0 - ADAM_B2 ** ADAM_STEP)
        d_ref[...] = -ADAM_LR * (m_hat / (jnp.sqrt(v_hat) + ADAM_EPS) + ADAM_WD * w_ref[...])
        mo_ref[...] = mn
        vo_ref[...] = vn

    spec = _rows(tm, C)
    return _call(body, name, [_sds((R, C), F32)] * 3, (R // tm,), [spec] * 4, [spec] * 3, sem=("parallel",))(w, g, m, v)


def kernel(x, positions, ln_in_g, ln_in_b, w_in, q_norm_g, kv_norm_g, w_uq, w_ukv, ret_gn_g, ret_gn_b, w_out, ln1_g, ln1_b, w_gate, w_up, w_down, ln2_g, ln2_b, loss_target, m_ln_in_g, m_ln_in_b, m_w_in, m_q_norm_g, m_kv_norm_g, m_w_uq, m_w_ukv, m_ret_gn_g, m_ret_gn_b, m_w_out, m_ln1_g, m_ln1_b, m_w_gate, m_w_up, m_w_down, m_ln2_g, m_ln2_b, v_ln_in_g, v_ln_in_b, v_w_in, v_q_norm_g, v_kv_norm_g, v_w_uq, v_w_ukv, v_ret_gn_g, v_ret_gn_b, v_w_out, v_ln1_g, v_ln1_b, v_w_gate, v_w_up, v_w_down, v_ln2_g, v_ln2_b):
    given = dict(locals())
    Wt = {n: given[n] for n in WEIGHTS}
    Mo = {n: given["m_" + n] for n in WEIGHTS}
    Vo = {n: given["v_" + n] for n in WEIGHTS}
    cx, cy, cc = _place()
    chip = (2 * cx + cy).astype(jnp.int32)
    core = cc.astype(jnp.int32)

    gathered = dict(zip(BIG, _allgather_weights([Wt[n].astype(BF16) for n in BIG])))
    W = [_internal_weights(gathered, l) for l in range(DEPTH)]

    sqerr, grad_x, dW, dP = _local_step(x[0], positions[0], loss_target[0], W, Wt)

    shards = [_grad_shards(dW[l]) for l in range(DEPTH)]
    gds = [jnp.stack([jnp.stack([shards[l][n][j] for l in range(DEPTH)]) for j in range(N_CHIPS)]) for n in BIG]
    got = _swap_layers(gds)
    parts = [_add_pair(gds[k], got[k], core.reshape(1), "add_pair_" + n) for k, n in enumerate(BIG)]
    small_g = {n: (dP[(n, None)] if Wt[n].ndim == 1 else jnp.stack([dP[(n, l)] for l in range(DEPTH)])) for n in SMALL}
    local_loss = 0.5 * jnp.sum(sqerr) / D_MODEL
    *rcvs, allsmall = _exchange_partials(parts, _flatten_small(small_g, local_loss))
    reds = [_add_chips(parts[k], rcvs[k], chip.reshape(1), "add_chips_" + n) for k, n in enumerate(BIG)]
    big_g = dict(zip(BIG, _share_reduced(reds)))
    small_sum = _sum_small(allsmall).reshape(-1)
    layout, n_small = _small_layout(Wt)
    loss = small_sum[n_small]

    grads, deltas, new_m, new_v = {}, {}, {}, {}
    for n in BIG:
        g = big_g[n]
        two_d = lambda a: a.reshape(-1, a.shape[-1])
        d, mn, vn = _adamw(two_d(Wt[n]), two_d(g), two_d(Mo[n]), two_d(Vo[n]), "adamw_" + n)
        grads[n], deltas[n], new_m[n], new_v[n] = g, d.reshape(g.shape), mn.reshape(g.shape), vn.reshape(g.shape)
    zero = jnp.zeros((), F32)
    d, mn, vn = _adamw(_flatten_small(Wt, zero), small_sum.reshape(SMALL_ROWS, FLAT_W), _flatten_small(Mo, zero),
                       _flatten_small(Vo, zero), "adamw_small")
    for n in SMALL:
        at, size = layout[n]
        pick = lambda a: a.reshape(-1)[at:at + size].reshape(Wt[n].shape)
        grads[n], deltas[n], new_m[n], new_v[n] = pick(small_sum), pick(d), pick(mn), pick(vn)

    return (loss, grad_x[None], *[grads[n] for n in WEIGHTS], *[deltas[n] for n in WEIGHTS],
            *[new_m[n] for n in WEIGHTS], *[new_v[n] for n in WEIGHTS])
```

```python
import functools

import jax
import jax.numpy as jnp
from jax import lax
from jax.experimental import pallas as pl
from jax.experimental.pallas import tpu as pltpu

F32 = jnp.float32
BF16 = jnp.bfloat16

D_MODEL = 2048
DEPTH = 2
CHUNK = 64
MLA_HEADS = 8
Q_LORA = 512
KV_LORA = 256
NOPE = 128
ROPE = 64
VDIM = 128
RET_HEADS = 4
RET_DK = 256
RET_DV = 256
D_FF = 5632
D_IN = 4928
ROPE_THETA = 10000.0
LN_EPS = 1e-5
RMS_EPS = 1e-6
GN_EPS = 1e-5
ALPHA = (2 * DEPTH) ** 0.25
MLA_SCALE = (NOPE + ROPE) ** -0.5
RET_SCALE = RET_DK ** -0.5
ADAM_LR = 0.001
ADAM_B1 = 0.9
ADAM_B2 = 0.999
ADAM_EPS = 1e-08
ADAM_WD = 0.01
ADAM_STEP = 10

LANES = 128
HEAD_PAD = 256
MLA_IN = 1024
MLA_IN_USED = Q_LORA + KV_LORA + ROPE
D_IN_PAD = MLA_IN + 4 * 1024
ATT_BLOCK = 512
NEG = -1e30
VMEM_LIMIT = 56 * 1024 * 1024

N_CHIPS = 4
FLAT_W = 1024
BIG = ("w_in", "w_uq", "w_ukv", "w_out", "w_gate", "w_up", "w_down")
BIG_SHARD = {"w_in": (2048, 1232), "w_uq": (512, 384), "w_ukv": (256, 512), "w_out": (512, 2048),
             "w_gate": (2048, 1408), "w_up": (2048, 1408), "w_down": (1408, 2048)}
SMALL = ("ln_in_g", "ln_in_b", "q_norm_g", "kv_norm_g", "ret_gn_g", "ret_gn_b", "ln1_g", "ln1_b", "ln2_g", "ln2_b")
WEIGHTS = ("ln_in_g", "ln_in_b", "w_in", "q_norm_g", "kv_norm_g", "w_uq", "w_ukv", "ret_gn_g", "ret_gn_b", "w_out",
           "ln1_g", "ln1_b", "w_gate", "w_up", "w_down", "ln2_g", "ln2_b")
SMALL_ROWS = 32

MESH = pl.DeviceIdType.MESH


def _pick(dim, cands):
    for c in cands:
        if dim % c == 0:
            return c
    return dim


def _call(body, name, out_shape, grid, in_specs, out_specs, scratch=(), sem=None):
    return pl.pallas_call(
        body, name=name, out_shape=out_shape, grid=grid, in_specs=in_specs, out_specs=out_specs,
        scratch_shapes=list(scratch),
        compiler_params=pltpu.CompilerParams(dimension_semantics=sem, vmem_limit_bytes=VMEM_LIMIT))


def _rows(tm, w, col=0):
    return pl.BlockSpec((tm, w), lambda i: (i, col))


def _whole(shape):
    return pl.BlockSpec(shape, lambda i: (0,) * len(shape))


def _sds(shape, dtype):
    return jax.ShapeDtypeStruct(shape, dtype)


def _matmul(a, b, name, ta=False, tb=False, out_dtype=F32):
    (K, M) = a.shape if ta else a.shape[::-1]
    (N, Kb) = b.shape if tb else b.shape[::-1]
    assert K == Kb, (a.shape, b.shape, ta, tb)
    tm = _pick(M, (512, 256, 128))
    tn = _pick(N, (1024, 512, 256, 128))
    tk = _pick(K, (2048, 1408, 1280, 1024, 512, 256))
    nk = K // tk
    dn = (((0 if ta else 1,), (1 if tb else 0,)), ((), ()))

    def body(a_ref, b_ref, o_ref, acc_ref):
        k = pl.program_id(2)
        p = lax.dot_general(a_ref[...].astype(BF16), b_ref[...].astype(BF16), dn, preferred_element_type=F32)
        if nk == 1:
            o_ref[...] = p.astype(out_dtype)
        else:
            @pl.when(k == 0)
            def _():
                acc_ref[...] = p

            @pl.when(jnp.logical_and(k > 0, k < nk - 1))
            def _():
                acc_ref[...] += p

            @pl.when(k == nk - 1)
            def _():
                o_ref[...] = (acc_ref[...] + p).astype(out_dtype)

    a_spec = pl.BlockSpec((tk, tm), lambda i, j, k: (k, i)) if ta else pl.BlockSpec((tm, tk), lambda i, j, k: (i, k))
    b_spec = pl.BlockSpec((tn, tk), lambda i, j, k: (j, k)) if tb else pl.BlockSpec((tk, tn), lambda i, j, k: (k, j))
    return _call(body, name, _sds((M, N), out_dtype), (M // tm, N // tn, nk), [a_spec, b_spec],
                 pl.BlockSpec((tm, tn), lambda i, j, k: (i, j)), scratch=[pltpu.VMEM((tm, tn), F32)],
                 sem=("parallel", "parallel", "arbitrary"))(a, b)


def _sigmoid(x):
    return 1.0 / (1.0 + jnp.exp(-x))


def _rope_group(r, c, sa, sb):
    return r * c + pltpu.roll(r, 32, 1) * sa + pltpu.roll(r, 96, 1) * sb


def _ln_fwd(xs, coefs, g, b, name, want_z):
    S, D = xs[0].shape
    tm = 256
    n = len(xs)

    def body(*refs):
        x_refs, g_ref, b_ref, outs = refs[:n], refs[n], refs[n + 1], refs[n + 2:]
        z = None
        for cf, r in zip(coefs, x_refs):
            t = r[...] if cf == 1.0 else cf * r[...]
            z = t if z is None else z + t
        mu = jnp.mean(z, axis=-1, keepdims=True)
        zc = z - mu
        var = jnp.mean(zc * zc, axis=-1, keepdims=True)
        y = zc * lax.rsqrt(var + LN_EPS) * g_ref[...] + b_ref[...]
        if want_z:
            outs[0][...] = z
        outs[-2][...] = y
        outs[-1][...] = y.astype(BF16)

    out_shape = [_sds((S, D), F32)] * (2 if want_z else 1) + [_sds((S, D), BF16)]
    return _call(body, name, out_shape, (S // tm,), [_rows(tm, D)] * n + [_whole((1, D))] * 2,
                 [_rows(tm, D)] * len(out_shape), sem=("parallel",))(*xs, g, b)


def _ln_bwd(dys, coefs, z, g, name):
    S, D = z.shape
    tm = 256
    n = len(dys)

    def body(*refs):
        dy_refs, z_ref, g_ref = refs[:n], refs[n], refs[n + 1]
        dz_ref, dzb_ref, dg_ref, db_ref = refs[n + 2:]
        dy = None
        for cf, r in zip(coefs, dy_refs):
            t = r[...] if cf == 1.0 else cf * r[...]
            dy = t if dy is None else dy + t
        zv = z_ref[...]
        mu = jnp.mean(zv, axis=-1, keepdims=True)
        zc = zv - mu
        var = jnp.mean(zc * zc, axis=-1, keepdims=True)
        rstd = lax.rsqrt(var + LN_EPS)
        xh = zc * rstd
        dyg = dy * g_ref[...]
        dz = rstd * (dyg - jnp.mean(dyg, axis=-1, keepdims=True) - xh * jnp.mean(dyg * xh, axis=-1, keepdims=True))
        dz_ref[...] = dz
        dzb_ref[...] = dz.astype(BF16)

        @pl.when(pl.program_id(0) == 0)
        def _():
            dg_ref[...] = jnp.zeros_like(dg_ref)
            db_ref[...] = jnp.zeros_like(db_ref)

        dg_ref[...] += jnp.sum(dy * xh, axis=0, keepdims=True)
        db_ref[...] += jnp.sum(dy, axis=0, keepdims=True)

    return _call(body, name, [_sds((S, D), F32), _sds((S, D), BF16), _sds((1, D), F32), _sds((1, D), F32)],
                 (S // tm,), [_rows(tm, D)] * (n + 1) + [_whole((1, D))],
                 [_rows(tm, D), _rows(tm, D), _whole((1, D)), _whole((1, D))], sem=("arbitrary",))(*dys, z, g)


def _rms(x, g):
    return x * lax.rsqrt(jnp.mean(x * x, axis=-1, keepdims=True) + RMS_EPS) * g


def _prep1(h, tabs, qg, kvg, name):
    S = h.shape[0]
    tm = 256
    cm, sam, sbm, cr, sr = tabs

    def body(h_ref, cm_ref, sam_ref, sbm_ref, cr_ref, sr_ref, qg_ref, kvg_ref,
             qn_ref, kvn_ref, kr_ref, rq_ref, rk_ref, rv_ref):
        qn_ref[...] = _rms(h_ref[:, 0:Q_LORA], qg_ref[...]).astype(BF16)
        kvn_ref[...] = _rms(h_ref[:, Q_LORA:Q_LORA + KV_LORA], kvg_ref[...]).astype(BF16)
        kr_ref[...] = _rope_group(h_ref[:, 768:896], cm_ref[...], sam_ref[...], sbm_ref[...])
        c, s = cr_ref[...], sr_ref[...]
        for hd in range(RET_HEADS):
            for src, dst, scale in ((MLA_IN, rq_ref, RET_SCALE), (MLA_IN + 1024, rk_ref, None)):
                t1 = h_ref[:, src + hd * 256:src + hd * 256 + 128]
                t2 = h_ref[:, src + hd * 256 + 128:src + hd * 256 + 256]
                o1, o2 = t1 * c - t2 * s, t2 * c + t1 * s
                if scale is not None:
                    o1, o2 = o1 * scale, o2 * scale
                dst[:, hd * 256:hd * 256 + 128] = o1.astype(BF16)
                dst[:, hd * 256 + 128:hd * 256 + 256] = o2.astype(BF16)
        rv_ref[...] = h_ref[:, MLA_IN + 2048:MLA_IN + 3072].astype(BF16)

    t128 = _rows(tm, LANES)
    return _call(body, name,
                 [_sds((S, Q_LORA), BF16), _sds((S, KV_LORA), BF16), _sds((S, LANES), F32),
                  _sds((S, 1024), BF16), _sds((S, 1024), BF16), _sds((S, 1024), BF16)],
                 (S // tm,),
                 [_rows(tm, D_IN_PAD), t128, t128, t128, t128, t128, _whole((1, Q_LORA)), _whole((1, KV_LORA))],
                 [_rows(tm, Q_LORA), _rows(tm, KV_LORA), t128, _rows(tm, 1024), _rows(tm, 1024), _rows(tm, 1024)],
                 sem=("parallel",))(h, cm, sam, sbm, cr, sr, qg, kvg)


def _prep1_bwd(dqn, dkvn, dkr, drq, drk, drv, drg, h, tabs, qg, kvg, name):
    S = h.shape[0]
    tm = 256
    cm, sam, sbm, cr, sr = tabs

    def rms_bwd(x, g, dy):
        r = lax.rsqrt(jnp.mean(x * x, axis=-1, keepdims=True) + RMS_EPS)
        dyg = dy * g
        dx = r * dyg - x * (r * r * r) * jnp.mean(dyg * x, axis=-1, keepdims=True)
        return dx, jnp.sum(dy * x * r, axis=0, keepdims=True)

    def body(dqn_ref, dkvn_ref, dkr_ref, drq_ref, drk_ref, drv_ref, drg_ref, h_ref,
             cm_ref, sam_ref, sbm_ref, cr_ref, sr_ref, qg_ref, kvg_ref, dh_ref, dqg_ref, dkvg_ref):
        dcq, dqg = rms_bwd(h_ref[:, 0:Q_LORA], qg_ref[...], dqn_ref[...])
        dckv, dkvg = rms_bwd(h_ref[:, Q_LORA:Q_LORA + KV_LORA], kvg_ref[...], dkvn_ref[...])
        dh_ref[:, 0:Q_LORA] = dcq.astype(BF16)
        dh_ref[:, Q_LORA:Q_LORA + KV_LORA] = dckv.astype(BF16)
        dh_ref[:, 768:896] = _rope_group(dkr_ref[...], cm_ref[...], -sam_ref[...], -sbm_ref[...]).astype(BF16)
        dh_ref[:, 896:1024] = jnp.zeros((tm, LANES), BF16)
        c, s = cr_ref[...], sr_ref[...]
        for hd in range(RET_HEADS):
            for src, dst, scale in ((drq_ref, MLA_IN, RET_SCALE), (drk_ref, MLA_IN + 1024, None)):
                d1 = src[:, hd * 256:hd * 256 + 128]
                d2 = src[:, hd * 256 + 128:hd * 256 + 256]
                if scale is not None:
                    d1, d2 = d1 * scale, d2 * scale
                dh_ref[:, dst + hd * 256:dst + hd * 256 + 128] = (d1 * c + d2 * s).astype(BF16)
                dh_ref[:, dst + hd * 256 + 128:dst + hd * 256 + 256] = (d2 * c - d1 * s).astype(BF16)
        dh_ref[:, MLA_IN + 2048:MLA_IN + 3072] = drv_ref[...].astype(BF16)
        dh_ref[:, MLA_IN + 3072:MLA_IN + 4096] = drg_ref[...].astype(BF16)

        @pl.when(pl.program_id(0) == 0)
        def _():
            dqg_ref[...] = jnp.zeros_like(dqg_ref)
            dkvg_ref[...] = jnp.zeros_like(dkvg_ref)

        dqg_ref[...] += dqg
        dkvg_ref[...] += dkvg

    t128 = _rows(tm, LANES)
    return _call(body, name,
                 [_sds((S, D_IN_PAD), BF16), _sds((1, Q_LORA), F32), _sds((1, KV_LORA), F32)],
                 (S // tm,),
                 [_rows(tm, Q_LORA), _rows(tm, KV_LORA), t128, _rows(tm, 1024), _rows(tm, 1024), _rows(tm, 1024),
                  _rows(tm, 1024), _rows(tm, D_IN_PAD), t128, t128, t128, t128, t128,
                  _whole((1, Q_LORA)), _whole((1, KV_LORA))],
                 [_rows(tm, D_IN_PAD), _whole((1, Q_LORA)), _whole((1, KV_LORA))],
                 sem=("arbitrary",))(dqn, dkvn, dkr, drq, drk, drv, drg, h, cm, sam, sbm, cr, sr, qg, kvg)


def _prep2(q, kv, kr, tabs, name):
    S = q.shape[0]
    tm = 256
    cm, sam, sbm = tabs[:3]

    def body(q_ref, kv_ref, kr_ref, cm_ref, sam_ref, sbm_ref, qo_ref, ko_ref, vo_ref):
        c, sa, sb = cm_ref[...], sam_ref[...], sbm_ref[...]
        krb = kr_ref[...].astype(BF16)
        for hd in range(MLA_HEADS):
            o = hd * HEAD_PAD
            qo_ref[:, o:o + 128] = q_ref[:, o:o + 128].astype(BF16)
            qo_ref[:, o + 128:o + 256] = _rope_group(q_ref[:, o + 128:o + 256], c, sa, sb).astype(BF16)
            ko_ref[:, o:o + 128] = kv_ref[:, hd * 128:hd * 128 + 128].astype(BF16)
            ko_ref[:, o + 128:o + 256] = krb
        vo_ref[...] = kv_ref[:, 1024:2048].astype(BF16)

    t128 = _rows(tm, LANES)
    return _call(body, name, [_sds((S, 2048), BF16), _sds((S, 2048), BF16), _sds((S, 1024), BF16)], (S // tm,),
                 [_rows(tm, 2048), _rows(tm, 2048), t128, t128, t128, t128],
                 [_rows(tm, 2048), _rows(tm, 2048), _rows(tm, 1024)], sem=("parallel",))(q, kv, kr, cm, sam, sbm)


def _prep2_bwd(dqm, dkm, dvm, tabs, name):
    S = dqm.shape[0]
    tm = 256
    cm, sam, sbm = tabs[:3]

    def body(dq_ref, dk_ref, dv_ref, cm_ref, sam_ref, sbm_ref, dqo_ref, dkvo_ref, dkr_ref):
        c, sa, sb = cm_ref[...], -sam_ref[...], -sbm_ref[...]
        dkr = None
        for hd in range(MLA_HEADS):
            o = hd * HEAD_PAD
            dqo_ref[:, o:o + 128] = dq_ref[:, o:o + 128].astype(BF16)
            dqo_ref[:, o + 128:o + 256] = _rope_group(dq_ref[:, o + 128:o + 256], c, sa, sb).astype(BF16)
            dkvo_ref[:, hd * 128:hd * 128 + 128] = dk_ref[:, o:o + 128].astype(BF16)
            t = dk_ref[:, o + 128:o + 256]
            dkr = t if dkr is None else dkr + t
        dkvo_ref[:, 1024:2048] = dv_ref[...].astype(BF16)
        dkr_ref[...] = dkr

    t128 = _rows(tm, LANES)
    return _call(body, name, [_sds((S, 2048), BF16), _sds((S, 2048), BF16), _sds((S, LANES), F32)], (S // tm,),
                 [_rows(tm, 2048), _rows(tm, 2048), _rows(tm, 1024), t128, t128, t128],
                 [_rows(tm, 2048), _rows(tm, 2048), t128], sem=("parallel",))(dqm, dkm, dvm, cm, sam, sbm)


def _gn_gate(a, o, h, gg, gb, name):
    S = a.shape[0]
    tm = 256

    def body(a_ref, o_ref, rg_ref, gg_ref, gb_ref, mix_ref):
        mix_ref[:, 0:1024] = a_ref[...].astype(BF16)
        for hd in range(RET_HEADS):
            sl = slice(hd * 256, hd * 256 + 256)
            ov = o_ref[:, sl]
            mu = jnp.mean(ov, axis=-1, keepdims=True)
            oc = ov - mu
            var = jnp.mean(oc * oc, axis=-1, keepdims=True)
            y = oc * lax.rsqrt(var + GN_EPS) * gg_ref[:, sl] + gb_ref[:, sl]
            rg = rg_ref[:, sl]
            mix_ref[:, 1024 + hd * 256:1024 + hd * 256 + 256] = (rg * _sigmoid(rg) * y).astype(BF16)

    return _call(body, name, _sds((S, 2048), BF16), (S // tm,),
                 [_rows(tm, 1024), _rows(tm, 1024), _rows(tm, 1024, 4), _whole((1, 1024)), _whole((1, 1024))],
                 _rows(tm, 2048), sem=("parallel",))(a, o, h, gg, gb)


def _gn_gate_bwd(dmixin, o, h, gg, gb, name):
    S = o.shape[0]
    tm = 256

    def body(dr_ref, o_ref, rg_ref, gg_ref, gb_ref, do_ref, drg_ref, dgg_ref, dgb_ref):
        @pl.when(pl.program_id(0) == 0)
        def _():
            dgg_ref[...] = jnp.zeros_like(dgg_ref)
            dgb_ref[...] = jnp.zeros_like(dgb_ref)

        for hd in range(RET_HEADS):
            sl = slice(hd * 256, hd * 256 + 256)
            ov = o_ref[:, sl]
            mu = jnp.mean(ov, axis=-1, keepdims=True)
            oc = ov - mu
            var = jnp.mean(oc * oc, axis=-1, keepdims=True)
            rstd = lax.rsqrt(var + GN_EPS)
            xh = oc * rstd
            g = gg_ref[:, sl]
            y = xh * g + gb_ref[:, sl]
            rg = rg_ref[:, sl]
            sg = _sigmoid(rg)
            dr = dr_ref[:, sl]
            dy = dr * (rg * sg)
            drg_ref[:, sl] = dr * y * (sg * (1.0 + rg * (1.0 - sg)))
            dgg_ref[:, sl] += jnp.sum(dy * xh, axis=0, keepdims=True)
            dgb_ref[:, sl] += jnp.sum(dy, axis=0, keepdims=True)
            dxh = dy * g
            do = rstd * (dxh - jnp.mean(dxh, axis=-1, keepdims=True) - xh * jnp.mean(dxh * xh, axis=-1, keepdims=True))
            do_ref[:, sl] = do.astype(BF16)

    return _call(body, name,
                 [_sds((S, 1024), BF16), _sds((S, 1024), F32), _sds((1, 1024), F32), _sds((1, 1024), F32)],
                 (S // tm,),
                 [_rows(tm, 1024, 1), _rows(tm, 1024), _rows(tm, 1024, 4), _whole((1, 1024)), _whole((1, 1024))],
                 [_rows(tm, 1024), _rows(tm, 1024), _whole((1, 1024)), _whole((1, 1024))],
                 sem=("arbitrary",))(dmixin, o, h, gg, gb)


def _swiglu(gu, name):
    S = gu.shape[0]
    tm = 256

    def body(g_ref, u_ref, o_ref):
        g = g_ref[...]
        o_ref[...] = (g * _sigmoid(g) * u_ref[...]).astype(BF16)

    return _call(body, name, _sds((S, D_FF), BF16), (S // tm,), [_rows(tm, D_FF, 0), _rows(tm, D_FF, 1)],
                 _rows(tm, D_FF), sem=("parallel",))(gu, gu)


def _swiglu_bwd(gu, dact, name):
    S = gu.shape[0]
    tm = 128

    def body(g_ref, u_ref, d_ref, o_ref):
        g, u, d = g_ref[...], u_ref[...], d_ref[...]
        sg = _sigmoid(g)
        o_ref[:, 0:D_FF] = (d * u * (sg * (1.0 + g * (1.0 - sg)))).astype(BF16)
        o_ref[:, D_FF:2 * D_FF] = (d * (g * sg)).astype(BF16)

    return _call(body, name, _sds((S, 2 * D_FF), BF16), (S // tm,),
                 [_rows(tm, D_FF, 0), _rows(tm, D_FF, 1), _rows(tm, D_FF)], _rows(tm, 2 * D_FF),
                 sem=("parallel",))(gu, gu, dact)


def _loss_head(y, target, name):
    S, D = y.shape
    tm = 256

    def body(y_ref, t_ref, dy_ref, acc_ref):
        e = y_ref[...] - t_ref[...]
        dy_ref[...] = e / D

        @pl.when(pl.program_id(0) == 0)
        def _():
            acc_ref[...] = jnp.zeros_like(acc_ref)

        acc_ref[...] += jnp.sum(e * e, axis=0, keepdims=True)

    return _call(body, name, [_sds((S, D), F32), _sds((1, D), F32)], (S // tm,), [_rows(tm, D), _rows(tm, D)],
                 [_rows(tm, D), _whole((1, D))], sem=("arbitrary",))(y, target)


def _chunk_mask(T):
    r = lax.shift_right_logical(lax.broadcasted_iota(jnp.int32, (T, T), 0), 6)
    c = lax.shift_right_logical(lax.broadcasted_iota(jnp.int32, (T, T), 1), 6)
    return r >= c


def _dot_nt(a, b):
    return lax.dot_general(a, b, (((1,), (1,)), ((), ())), preferred_element_type=F32)


def _dot_tn(a, b):
    return lax.dot_general(a, b, (((0,), (0,)), ((), ())), preferred_element_type=F32)


def _decay_tables(T):
    lg = jnp.log1p(-jnp.exp2(-5.0 - jnp.arange(RET_HEADS, dtype=F32)))
    idx = jnp.arange(T, dtype=F32)
    diff = idx[:, None] - idx[None, :]
    rel = jnp.exp(lg[:, None, None] * diff[None])
    cid = jnp.arange(T) // CHUNK
    mask = (cid[:, None] >= cid[None, :]).astype(F32)
    reld = jnp.exp(lg[:, None, None] * jnp.abs(diff)[None]) * mask[None]
    lgrow = jnp.broadcast_to(lg[:, None, None], (RET_HEADS, 1, LANES))
    return lgrow, rel, reld


def _attn_fwd(q, k, v, heads, dk, dv, softmax, name, tables=None):
    S = q.shape[0]
    T = ATT_BLOCK
    nq = S // T
    rep = T // LANES

    def body(*refs):
        if softmax:
            q_ref, k_ref, v_ref, o_ref, lse_ref, m_sc, l_sc, acc_sc = refs
        else:
            q_ref, k_ref, v_ref, lg_ref, rel_ref, reld_ref, o_ref, acc_sc = refs
        i = pl.program_id(1)
        qv = q_ref[...]

        def kv_block(j):
            rows = pl.ds(pl.multiple_of(j * T, T), T)
            return k_ref[rows, :], v_ref[rows, :]

        kb, vb = kv_block(i)
        s = _dot_nt(qv, kb)
        if softmax:
            s = jnp.where(_chunk_mask(T), s * MLA_SCALE, NEG)
            m = jnp.max(s, axis=-1, keepdims=True)
            p = jnp.exp(s - m)
            m_sc[...] = jnp.broadcast_to(m, (T, LANES))
            l_sc[...] = jnp.broadcast_to(jnp.sum(p, axis=-1, keepdims=True), (T, LANES))
        else:
            p = s * reld_ref[0]
        acc_sc[...] = jnp.dot(p.astype(BF16), vb, preferred_element_type=F32)

        def step(j, carry):
            kb, vb = kv_block(j)
            s = _dot_nt(qv, kb)
            if softmax:
                s = s * MLA_SCALE
                m_prev = m_sc[...]
                m_next = jnp.maximum(m_prev, jnp.max(s, axis=-1, keepdims=True))
                alpha = jnp.exp(m_prev - m_next)
                p = jnp.exp(s - jnp.tile(m_next, (1, rep)))
                l_sc[...] = alpha * l_sc[...] + jnp.sum(p, axis=-1, keepdims=True)
                m_sc[...] = m_next
                acc_sc[...] = acc_sc[...] * jnp.tile(alpha, (1, dv // LANES)) + jnp.dot(
                    p.astype(BF16), vb, preferred_element_type=F32)
            else:
                fac = jnp.exp(lg_ref[0] * ((i - j) * T).astype(F32))
                p = s * (rel_ref[0] * jnp.tile(fac, (1, rep)))
                acc_sc[...] += jnp.dot(p.astype(BF16), vb, preferred_element_type=F32)
            return carry

        lax.fori_loop(0, i, step, 0)
        if softmax:
            l = l_sc[...]
            o_ref[...] = acc_sc[...] / jnp.tile(l, (1, dv // LANES))
            lse_ref[...] = m_sc[...] + jnp.log(l)
        else:
            o_ref[...] = acc_sc[...]

    in_specs = [pl.BlockSpec((T, dk), lambda h, i: (i, h)), pl.BlockSpec((S, dk), lambda h, i: (0, h)),
                pl.BlockSpec((S, dv), lambda h, i: (0, h))]
    o_spec = pl.BlockSpec((T, dv), lambda h, i: (i, h))
    if softmax:
        return _call(body, name, [_sds((S, heads * dv), F32), _sds((S, heads * LANES), F32)], (heads, nq), in_specs,
                     [o_spec, pl.BlockSpec((T, LANES), lambda h, i: (i, h))],
                     scratch=[pltpu.VMEM((T, LANES), F32), pltpu.VMEM((T, LANES), F32), pltpu.VMEM((T, dv), F32)],
                     sem=("parallel", "arbitrary"))(q, k, v)
    lgrow, rel, reld = tables
    in_specs += [pl.BlockSpec((1, 1, LANES), lambda h, i: (h, 0, 0)), pl.BlockSpec((1, T, T), lambda h, i: (h, 0, 0)),
                 pl.BlockSpec((1, T, T), lambda h, i: (h, 0, 0))]
    return _call(body, name, _sds((S, heads * dv), F32), (heads, nq), in_specs, o_spec,
                 scratch=[pltpu.VMEM((T, dv), F32)], sem=("parallel", "arbitrary"))(q, k, v, lgrow, rel, reld)


def _attn_bwd(q, k, v, do, heads, dk, dv, softmax, name, o=None, lse=None, tables=None):
    S = q.shape[0]
    T = ATT_BLOCK
    nq = S // T
    rep = T // LANES

    def body(*refs):
        if softmax:
            q_ref, k_ref, v_ref, do_ref, o_ref, lse_ref, dq_ref, dk_ref, dv_ref, dq_sc = refs
        else:
            q_ref, k_ref, v_ref, do_ref, lg_ref, rel_ref, reld_ref, dq_ref, dk_ref, dv_ref, dq_sc = refs
        i = pl.program_id(1)

        @pl.when(i == 0)
        def _():
            dk_ref[...] = jnp.zeros_like(dk_ref)
            dv_ref[...] = jnp.zeros_like(dv_ref)

        qv = q_ref[...]
        dof = do_ref[...].astype(F32)
        dov = dof.astype(BF16)
        if softmax:
            delta = jnp.sum(dof * o_ref[...], axis=-1, keepdims=True)
            lse_t = jnp.tile(lse_ref[...], (1, rep))
        dq_sc[...] = jnp.zeros_like(dq_sc)

        def block(j, diagonal):
            rows = pl.ds(pl.multiple_of(j * T, T), T)
            kb, vb = k_ref[rows, :], v_ref[rows, :]
            s = _dot_nt(qv, kb)
            dp = _dot_nt(dov, vb)
            if softmax:
                s = s * MLA_SCALE
                if diagonal:
                    s = jnp.where(_chunk_mask(T), s, NEG)
                p = jnp.exp(s - lse_t)
                ds = p * (dp - delta) * MLA_SCALE
            else:
                if diagonal:
                    dec = reld_ref[0]
                else:
                    fac = jnp.exp(lg_ref[0] * ((i - j) * T).astype(F32))
                    dec = rel_ref[0] * jnp.tile(fac, (1, rep))
                p = s * dec
                ds = dp * dec
            dsb = ds.astype(BF16)
            dv_ref[rows, :] += _dot_tn(p.astype(BF16), dov)
            dk_ref[rows, :] += _dot_tn(dsb, qv)
            dq_sc[...] += jnp.dot(dsb, kb, preferred_element_type=F32)

        block(i, True)

        def step(j, carry):
            block(j, False)
            return carry

        lax.fori_loop(0, i, step, 0)
        dq_ref[...] = dq_sc[...]

    qspec = pl.BlockSpec((T, dk), lambda h, i: (i, h))
    kspec = pl.BlockSpec((S, dk), lambda h, i: (0, h))
    vspec = pl.BlockSpec((S, dv), lambda h, i: (0, h))
    dospec = pl.BlockSpec((T, dv), lambda h, i: (i, h))
    in_specs = [qspec, kspec, vspec, dospec]
    args = [q, k, v, do]
    if softmax:
        in_specs += [dospec, pl.BlockSpec((T, LANES), lambda h, i: (i, h))]
        args += [o, lse]
    else:
        in_specs += [pl.BlockSpec((1, 1, LANES), lambda h, i: (h, 0, 0)),
                     pl.BlockSpec((1, T, T), lambda h, i: (h, 0, 0)), pl.BlockSpec((1, T, T), lambda h, i: (h, 0, 0))]
        args += list(tables)
    return _call(body, name, [_sds((S, heads * dk), F32), _sds((S, heads * dk), F32), _sds((S, heads * dv), F32)],
                 (heads, nq), in_specs, [qspec, kspec, vspec], scratch=[pltpu.VMEM((T, dk), F32)],
                 sem=("parallel", "arbitrary"))(*args)


def _rope_tables(pos):
    def tables(dim):
        inv_freq = ROPE_THETA ** (-jnp.arange(0, dim, 2, dtype=F32) / dim)
        ang = pos.astype(F32)[:, None] * inv_freq
        return jnp.cos(ang), jnp.sin(ang)

    cm, sm = tables(ROPE)
    S = pos.shape[0]
    z32, z64 = jnp.zeros((S, 32), F32), jnp.zeros((S, 64), F32)
    cr, sr = tables(RET_DK)
    return (jnp.concatenate([cm, cm, z64], 1), jnp.concatenate([z32, sm, z64], 1),
            jnp.concatenate([-sm, z32, z64], 1), cr, sr)


def _row(v):
    return v.reshape(1, -1).astype(F32)


def _local_step(x, pos, target, W, P):
    tabs = _rope_tables(pos)
    dtabs = _decay_tables(ATT_BLOCK)
    xf, xb = _ln_fwd([x], [1.0], _row(P["ln_in_g"]), _row(P["ln_in_b"]), "ln_in", False)
    saved = []
    for l in range(DEPTH):
        w = W[l]
        t = f"_l{l}"
        h = _matmul(xb, w["w_in"], "mm_h" + t)
        qn, kvn, kr, rq, rk, rv = _prep1(h, tabs, _row(P["q_norm_g"][l]), _row(P["kv_norm_g"][l]), "prep1" + t)
        q = _matmul(qn, w["w_uq"], "mm_q" + t)
        kv = _matmul(kvn, w["w_ukv"], "mm_kv" + t)
        qm, km, vm = _prep2(q, kv, kr, tabs, "prep2" + t)
        a, lse = _attn_fwd(qm, km, vm, MLA_HEADS, HEAD_PAD, VDIM, True, "mla_fwd" + t)
        o = _attn_fwd(rq, rk, rv, RET_HEADS, RET_DK, RET_DV, False, "ret_fwd" + t, tables=dtabs)
        mixin = _gn_gate(a, o, h, _row(P["ret_gn_g"][l]), _row(P["ret_gn_b"][l]), "gn_gate" + t)
        mix = _matmul(mixin, w["w_out"], "mm_mix" + t)
        z1, x1f, x1b = _ln_fwd([xf, mix], [ALPHA, 1.0], _row(P["ln1_g"][l]), _row(P["ln1_b"][l]), "ln1" + t, True)
        gu = _matmul(x1b, w["w_gu"], "mm_gu" + t)
        act = _swiglu(gu, "swiglu" + t)
        f = _matmul(act, w["w_down"], "mm_down" + t)
        z2, x2f, x2b = _ln_fwd([x1f, f], [ALPHA, 1.0], _row(P["ln2_g"][l]), _row(P["ln2_b"][l]), "ln2" + t, True)
        saved.append(dict(xb=xb, h=h, qn=qn, kvn=kvn, rq=rq, rk=rk, rv=rv, qm=qm, km=km, vm=vm, a=a, lse=lse, o=o,
                          mixin=mixin, z1=z1, x1b=x1b, gu=gu, act=act, z2=z2))
        xf, xb = x2f, x2b

    dy, sqerr = _loss_head(xf, target, "loss_head")
    dW = [None] * DEPTH
    dP = {}
    dys, coefs = [dy], [1.0]
    for l in reversed(range(DEPTH)):
        w, sv = W[l], saved[l]
        t = f"_l{l}"
        dz2, dz2b, dg, db = _ln_bwd(dys, coefs, sv["z2"], _row(P["ln2_g"][l]), "ln2_bwd" + t)
        dP[("ln2_g", l)], dP[("ln2_b", l)] = dg, db
        g = {}
        g["w_down"] = _matmul(sv["act"], dz2b, "mm_dw_down" + t, ta=True, out_dtype=BF16)
        dact = _matmul(dz2b, w["w_down"], "mm_dact" + t, tb=True)
        dgu = _swiglu_bwd(sv["gu"], dact, "swiglu_bwd" + t)
        g["w_gu"] = _matmul(sv["x1b"], dgu, "mm_dw_gu" + t, ta=True, out_dtype=BF16)
        dx1 = _matmul(dgu, w["w_gu"], "mm_dx1" + t, tb=True)
        dz1, dz1b, dg, db = _ln_bwd([dz2, dx1], [ALPHA, 1.0], sv["z1"], _row(P["ln1_g"][l]), "ln1_bwd" + t)
        dP[("ln1_g", l)], dP[("ln1_b", l)] = dg, db
        g["w_out"] = _matmul(sv["mixin"], dz1b, "mm_dw_out" + t, ta=True, out_dtype=BF16)
        dmixin = _matmul(dz1b, w["w_out"], "mm_dmixin" + t, tb=True)
        do, drg, dgg, dgb = _gn_gate_bwd(dmixin, sv["o"], sv["h"], _row(P["ret_gn_g"][l]), _row(P["ret_gn_b"][l]),
                                         "gn_gate_bwd" + t)
        dP[("ret_gn_g", l)], dP[("ret_gn_b", l)] = dgg, dgb
        drq, drk, drv = _attn_bwd(sv["rq"], sv["rk"], sv["rv"], do, RET_HEADS, RET_DK, RET_DV, False,
                                  "ret_bwd" + t, tables=dtabs)
        dqm, dkm, dvm = _attn_bwd(sv["qm"], sv["km"], sv["vm"], dmixin, MLA_HEADS, HEAD_PAD, VDIM, True,
                                  "mla_bwd" + t, o=sv["a"], lse=sv["lse"])
        dq, dkv, dkr = _prep2_bwd(dqm, dkm, dvm, tabs, "prep2_bwd" + t)
        g["w_uq"] = _matmul(sv["qn"], dq, "mm_dw_uq" + t, ta=True, out_dtype=BF16)
        dqn = _matmul(dq, w["w_uq"], "mm_dqn" + t, tb=True)
        g["w_ukv"] = _matmul(sv["kvn"], dkv, "mm_dw_ukv" + t, ta=True, out_dtype=BF16)
        dkvn = _matmul(dkv, w["w_ukv"], "mm_dkvn" + t, tb=True)
        dh, dqg, dkvg = _prep1_bwd(dqn, dkvn, dkr, drq, drk, drv, drg, sv["h"], tabs, _row(P["q_norm_g"][l]),
                                   _row(P["kv_norm_g"][l]), "prep1_bwd" + t)
        dP[("q_norm_g", l)], dP[("kv_norm_g", l)] = dqg, dkvg
        g["w_in"] = _matmul(sv["xb"], dh, "mm_dw_in" + t, ta=True, out_dtype=BF16)
        dxl = _matmul(dh, w["w_in"], "mm_dxl" + t, tb=True)
        dW[l] = g
        dys, coefs = [dz1, dxl], [ALPHA, 1.0]
    grad_x, _, dg, db = _ln_bwd(dys, coefs, x, _row(P["ln_in_g"]), "ln_in_bwd")
    dP[("ln_in_g", None)], dP[("ln_in_b", None)] = dg, db
    return sqerr, grad_x, dW, dP


def _internal_weights(G):
    cat = lambda parts: jnp.concatenate(parts, axis=1)
    cols = lambda n: cat([G[n][j] for j in range(N_CHIPS)])
    rows = lambda n: G[n].reshape(-1, G[n].shape[-1])
    first = G["w_in"][0]
    w_in = cat([first[:, :MLA_IN_USED], jnp.zeros((D_MODEL, MLA_IN - MLA_IN_USED), BF16), first[:, MLA_IN_USED:]]
               + [G["w_in"][j] for j in range(1, N_CHIPS)])
    uq, ukv = cols("w_uq"), cols("w_ukv")
    hw = NOPE + ROPE
    pad = jnp.zeros((Q_LORA, HEAD_PAD - hw), BF16)
    w_uq = cat([p for h in range(MLA_HEADS) for p in (uq[:, h * hw:(h + 1) * hw], pad)])
    w_ukv = cat([ukv[:, 256 * h:256 * h + NOPE] for h in range(MLA_HEADS)]
                + [ukv[:, 256 * h + NOPE:256 * h + 256] for h in range(MLA_HEADS)])
    return dict(w_in=w_in, w_uq=w_uq, w_ukv=w_ukv, w_out=rows("w_out"),
                w_gu=cat([G["w_gate"][j] for j in range(N_CHIPS)] + [G["w_up"][j] for j in range(N_CHIPS)]),
                w_down=rows("w_down"))


def _grad_shards(g):
    cat = lambda parts: jnp.concatenate(parts, axis=1)
    wi, uq, ukv, gu = g["w_in"], g["w_uq"], g["w_ukv"], g["w_gu"]
    ci, cq, cg = BIG_SHARD["w_in"][1], NOPE + ROPE, BIG_SHARD["w_gate"][1]
    shift = MLA_IN - MLA_IN_USED
    w_in = [cat([wi[:, :MLA_IN_USED], wi[:, MLA_IN:ci + shift]])]
    w_in += [wi[:, ci * j + shift:ci * (j + 1) + shift] for j in range(1, N_CHIPS)]
    w_uq = [cat([uq[:, HEAD_PAD * h:HEAD_PAD * h + cq] for h in (2 * j, 2 * j + 1)]) for j in range(N_CHIPS)]
    w_ukv = [cat([ukv[:, o + NOPE * h:o + NOPE * (h + 1)] for h in (2 * j, 2 * j + 1) for o in (0, MLA_HEADS * NOPE)])
             for j in range(N_CHIPS)]
    ro, rd = BIG_SHARD["w_out"][0], BIG_SHARD["w_down"][0]
    return dict(w_in=w_in, w_uq=w_uq, w_ukv=w_ukv,
                w_out=[g["w_out"][ro * j:ro * (j + 1)] for j in range(N_CHIPS)],
                w_gate=[gu[:, cg * j:cg * (j + 1)] for j in range(N_CHIPS)],
                w_up=[gu[:, D_FF + cg * j:D_FF + cg * (j + 1)] for j in range(N_CHIPS)],
                w_down=[g["w_down"][rd * j:rd * (j + 1)] for j in range(N_CHIPS)])


def _small_layout(P):
    out, at = {}, 0
    for n in SMALL:
        out[n] = (at, P[n].size)
        at += P[n].size
    return out, at


def _flatten_small(P, last):
    v = jnp.concatenate([P[n].reshape(-1).astype(F32) for n in SMALL] + [last.reshape(-1).astype(F32)])
    return jnp.pad(v, (0, SMALL_ROWS * FLAT_W - v.size)).reshape(SMALL_ROWS, FLAT_W)


HBM = pl.BlockSpec(memory_space=pltpu.HBM)


def _place():
    return lax.axis_index("x"), lax.axis_index("y"), lax.axis_index("c")


def _other_chips(x, y):
    return [(1 - x, y), (x, 1 - y), (1 - x, 1 - y)]


def _rcopy(src, dst, ssem, rsem, dev):
    return pltpu.make_async_remote_copy(src_ref=src, dst_ref=dst, send_sem=ssem, recv_sem=rsem, device_id=dev,
                                        device_id_type=MESH)


def _comm_call(body, name, out_shape, n_in, scratch):
    many = isinstance(out_shape, (list, tuple))
    return pl.pallas_call(body, name=name, out_shape=out_shape, in_specs=[HBM] * n_in,
                          out_specs=[HBM] * len(out_shape) if many else HBM, scratch_shapes=scratch)


NB = len(BIG)


def _half(ref, which):
    rows = ref.shape[0] // 2
    return ref.at[pl.ds(pl.multiple_of(which * rows, 16), rows)]


def _allgather_layer(ws, name):
    def body(*refs):
        w_refs, g_refs = refs[:NB], refs[NB:2 * NB]
        ssem, rsem, fssem, frsem, ossem, orsem = refs[2 * NB:]
        x, y, c = _place()
        j = 2 * x + y
        sib = (x, y, 1 - c)
        chips = _other_chips(x, y)
        owns = [_rcopy(w_refs[n], g_refs[n].at[j], ossem.at[n], orsem.at[n], sib) for n in range(NB)]
        sends = [_rcopy(_half(w_refs[n], c), _half(g_refs[n].at[j], c), ssem.at[3 * n + t], rsem.at[3 * n + t],
                        (cx, cy, c)) for n in range(NB) for t, (cx, cy) in enumerate(chips)]
        for cp in sends + owns:
            cp.start()
        passed = []
        for n in range(NB):
            for t, (cx, cy) in enumerate(chips):
                blk = _half(g_refs[n].at[2 * cx + cy], c)
                _rcopy(blk, blk, ssem.at[3 * n + t], rsem.at[3 * n + t], (cx, cy, c)).wait_recv()
                cp = _rcopy(blk, blk, fssem.at[3 * n + t], frsem.at[3 * n + t], sib)
                cp.start()
                passed.append(cp)
        for n in range(NB):
            for t, (cx, cy) in enumerate(chips):
                blk = _half(g_refs[n].at[2 * cx + cy], 1 - c)
                _rcopy(blk, blk, fssem.at[3 * n + t], frsem.at[3 * n + t], sib).wait_recv()
        for cp in owns:
            cp.wait()
        for cp in sends + passed:
            cp.wait_send()

    sems, sem1 = pltpu.SemaphoreType.DMA((3 * NB,)), pltpu.SemaphoreType.DMA((NB,))
    return _comm_call(body, name, [_sds((N_CHIPS,) + w.shape, w.dtype) for w in ws], NB,
                      [sems, sems, sems, sems, sem1, sem1])(*ws)


def _swap_halves(gds, name):
    def body(*refs):
        gd_refs, out_refs, (ssem, rsem) = refs[:NB], refs[NB:2 * NB], refs[2 * NB:]
        x, y, c = _place()
        cps = [_rcopy(_half(gd_refs[n].at[jj], 1 - c), out_refs[n].at[jj], ssem.at[N_CHIPS * n + jj],
                      rsem.at[N_CHIPS * n + jj], (x, y, 1 - c)) for n in range(NB) for jj in range(N_CHIPS)]
        for cp in cps:
            cp.start()
        for cp in cps:
            cp.wait()

    sems = pltpu.SemaphoreType.DMA((N_CHIPS * NB,))
    return _comm_call(body, name, [_sds((N_CHIPS, g.shape[1] // 2, g.shape[2]), g.dtype) for g in gds], NB,
                      [sems, sems])(*gds)


def _exchange_partials(parts, name):
    def body(*refs):
        p_refs, rcv_refs, (ssem, rsem) = refs[:NB], refs[NB:2 * NB], refs[2 * NB:]
        x, y, c = _place()
        cps = [_rcopy(p_refs[n].at[2 * cx + cy], rcv_refs[n].at[t], ssem.at[3 * n + t], rsem.at[3 * n + t], (cx, cy, c))
               for n in range(NB) for t, (cx, cy) in enumerate(_other_chips(x, y))]
        for cp in cps:
            cp.start()
        for cp in cps:
            cp.wait()

    sem3 = pltpu.SemaphoreType.DMA((3 * NB,))
    return _comm_call(body, name, [_sds((3,) + p.shape[1:], p.dtype) for p in parts], NB, [sem3, sem3])(*parts)


def _allreduce_small(small):
    def body(s_ref, all_ref, sssem, srsem, lsem):
        x, y, c = _place()
        me = 4 * x + 2 * y + c
        own = pltpu.make_async_copy(s_ref, all_ref.at[me], lsem)
        own.start()
        cps = []
        for r in range(1, 8):
            fx, fy, fc = (r >> 2) & 1, (r >> 1) & 1, r & 1
            px, py, pc = (1 - x if fx else x, 1 - y if fy else y, 1 - c if fc else c)
            peer = 4 * px + 2 * py + pc
            send = _rcopy(s_ref, all_ref.at[me], sssem.at[r - 1], srsem.at[me], (px, py, pc))
            send.start()
            cps.append((send, _rcopy(s_ref, all_ref.at[peer], sssem.at[r - 1], srsem.at[peer], (px, py, pc))))
        for send, recv in cps:
            send.wait_send()
            recv.wait_recv()
        own.wait()

    return _comm_call(body, "allreduce_small", [_sds((8,) + small.shape, small.dtype)], 1,
                      [pltpu.SemaphoreType.DMA((7,)), pltpu.SemaphoreType.DMA((8,)), pltpu.SemaphoreType.DMA(())])(small)[0]


def _share_halves(reds, name):
    def body(*refs):
        r_refs, out_refs, (ssem, rsem) = refs[:NB], refs[NB:2 * NB], refs[2 * NB:]
        x, y, c = _place()
        cps = [_rcopy(r_refs[n], out_refs[n], ssem.at[n], rsem.at[n], (x, y, 1 - c)) for n in range(NB)]
        for cp in cps:
            cp.start()
        for cp in cps:
            cp.wait()

    sems = pltpu.SemaphoreType.DMA((NB,))
    return _comm_call(body, name, [_sds(r.shape, r.dtype) for r in reds], NB, [sems, sems])(*reds)


def _add_pair(gd, got, c, name):
    _, R, W = got.shape
    tm = _pick(R, (512, 256, 128, 64))
    nb = R // tm

    def body(c_ref, a_ref, b_ref, o_ref):
        o_ref[...] = (a_ref[...].astype(F32) + b_ref[...].astype(F32)).astype(o_ref.dtype)

    grid_spec = pltpu.PrefetchScalarGridSpec(
        num_scalar_prefetch=1, grid=(N_CHIPS, nb),
        in_specs=[pl.BlockSpec((None, tm, W), lambda j, i, c_ref: (j, c_ref[0] * nb + i, 0)),
                  pl.BlockSpec((None, tm, W), lambda j, i, c_ref: (j, i, 0))],
        out_specs=pl.BlockSpec((None, tm, W), lambda j, i, c_ref: (j, i, 0)))
    return pl.pallas_call(body, name=name, grid_spec=grid_spec, out_shape=_sds((N_CHIPS, R, W), gd.dtype),
                          compiler_params=pltpu.CompilerParams(dimension_semantics=("parallel", "parallel"),
                                                               vmem_limit_bytes=VMEM_LIMIT))(c, gd, got)


def _add_chips(part, rcv, j, name):
    _, R, W = part.shape
    tm = _pick(R, (512, 256, 128, 64))

    def body(j_ref, p_ref, r0_ref, r1_ref, r2_ref, o_ref):
        o_ref[...] = ((p_ref[...].astype(F32) + r0_ref[...].astype(F32)) + r1_ref[...].astype(F32)) + r2_ref[...].astype(F32)

    def slot(t):
        return pl.BlockSpec((None, tm, W), lambda i, j_ref: (t, i, 0))

    grid_spec = pltpu.PrefetchScalarGridSpec(
        num_scalar_prefetch=1, grid=(R // tm,),
        in_specs=[pl.BlockSpec((None, tm, W), lambda i, j_ref: (j_ref[0], i, 0)), slot(0), slot(1), slot(2)],
        out_specs=pl.BlockSpec((tm, W), lambda i, j_ref: (i, 0)))
    return pl.pallas_call(body, name=name, grid_spec=grid_spec, out_shape=_sds((R, W), F32),
                          compiler_params=pltpu.CompilerParams(dimension_semantics=("parallel",),
                                                               vmem_limit_bytes=VMEM_LIMIT))(j, part, rcv, rcv, rcv)


def _sum_small(allsmall):
    _, R, W = allsmall.shape

    def body(a_ref, o_ref):
        acc = a_ref[0]
        for d in range(1, 8):
            acc = acc + a_ref[d]
        o_ref[...] = acc

    return _call(body, "sum_small", _sds((R, W), F32), (1,), [_whole((8, R, W))], _whole((R, W)),
                 sem=("arbitrary",))(allsmall)


def _adamw(w, g, m, v, name):
    R, C = w.shape
    tm = _pick(R, (256, 128, 64, 32, 8))

    def body(w_ref, g_ref, m_ref, v_ref, d_ref, mo_ref, vo_ref):
        gv = g_ref[...]
        mn = ADAM_B1 * m_ref[...] + (1.0 - ADAM_B1) * gv
        vn = ADAM_B2 * v_ref[...] + (1.0 - ADAM_B2) * (gv * gv)
        m_hat = mn / (1.0 - ADAM_B1 ** ADAM_STEP)
        v_hat = vn / (1.0 - ADAM_B2 ** ADAM_STEP)
        d_ref[...] = -ADAM_LR * (m_hat / (jnp.sqrt(v_hat) + ADAM_EPS) + ADAM_WD * w_ref[...])
        mo_ref[...] = mn
        vo_ref[...] = vn

    spec = _rows(tm, C)
    return _call(body, name, [_sds((R, C), F32)] * 3, (R // tm,), [spec] * 4, [spec] * 3, sem=("parallel",))(w, g, m, v)


def _adamw_layer(c, w, m, v, mine, other, l, prev, name):
    _, R, C = w.shape
    half = R // 2
    tm = _pick(half, (256, 128, 64))
    nbh = half // tm

    def body(c_ref, w_ref, m_ref, v_ref, a_ref, b_ref, *rest):
        g_ref, d_ref, mo_ref, vo_ref = rest[-4:]
        gv = jnp.where(pl.program_id(0) // nbh == c_ref[0], a_ref[...], b_ref[...])
        mn = ADAM_B1 * m_ref[...] + (1.0 - ADAM_B1) * gv
        vn = ADAM_B2 * v_ref[...] + (1.0 - ADAM_B2) * (gv * gv)
        m_hat = mn / (1.0 - ADAM_B1 ** ADAM_STEP)
        v_hat = vn / (1.0 - ADAM_B2 ** ADAM_STEP)
        g_ref[...] = gv
        d_ref[...] = -ADAM_LR * (m_hat / (jnp.sqrt(v_hat) + ADAM_EPS) + ADAM_WD * w_ref[...])
        mo_ref[...] = mn
        vo_ref[...] = vn

    layer = pl.BlockSpec((None, tm, C), lambda i, c_ref: (l, i, 0))
    halfspec = pl.BlockSpec((tm, C), lambda i, c_ref: (i % nbh, 0))
    n_prev = 0 if prev is None else 4
    grid_spec = pltpu.PrefetchScalarGridSpec(
        num_scalar_prefetch=1, grid=(R // tm,),
        in_specs=[layer] * 3 + [halfspec] * 2 + [pl.BlockSpec(memory_space=pl.ANY)] * n_prev,
        out_specs=[layer] * 4)
    return pl.pallas_call(body, name=name, grid_spec=grid_spec, out_shape=[_sds(w.shape, F32)] * 4,
                          input_output_aliases={6 + k: k for k in range(n_prev)},
                          compiler_params=pltpu.CompilerParams(dimension_semantics=("parallel",),
                                                               vmem_limit_bytes=VMEM_LIMIT))(
        c, w, m, v, mine, other, *(prev or ()))


def kernel(x, positions, ln_in_g, ln_in_b, w_in, q_norm_g, kv_norm_g, w_uq, w_ukv, ret_gn_g, ret_gn_b, w_out, ln1_g, ln1_b, w_gate, w_up, w_down, ln2_g, ln2_b, loss_target, m_ln_in_g, m_ln_in_b, m_w_in, m_q_norm_g, m_kv_norm_g, m_w_uq, m_w_ukv, m_ret_gn_g, m_ret_gn_b, m_w_out, m_ln1_g, m_ln1_b, m_w_gate, m_w_up, m_w_down, m_ln2_g, m_ln2_b, v_ln_in_g, v_ln_in_b, v_w_in, v_q_norm_g, v_kv_norm_g, v_w_uq, v_w_ukv, v_ret_gn_g, v_ret_gn_b, v_w_out, v_ln1_g, v_ln1_b, v_w_gate, v_w_up, v_w_down, v_ln2_g, v_ln2_b):
    given = dict(locals())
    Wt = {n: given[n] for n in WEIGHTS}
    Mo = {n: given["m_" + n] for n in WEIGHTS}
    Vo = {n: given["v_" + n] for n in WEIGHTS}
    cx, cy, cc = _place()
    chip = (2 * cx + cy).astype(jnp.int32)
    core = cc.astype(jnp.int32)

    W = []
    for l in range(DEPTH):
        blocks = _allgather_layer([Wt[n][l].astype(BF16) for n in BIG], f"allgather_l{l}")
        W.append(_internal_weights(dict(zip(BIG, blocks))))

    sqerr, grad_x, dW, dP = _local_step(x[0], positions[0], loss_target[0], W, Wt)

    results = {n: None for n in BIG}
    for l in reversed(range(DEPTH)):
        t = f"_l{l}"
        shards = _grad_shards(dW[l])
        gds = [jnp.stack(shards[n]) for n in BIG]
        got = _swap_halves(gds, "swap_halves" + t)
        parts = [_add_pair(gds[k], got[k], core.reshape(1), "add_pair_" + n + t) for k, n in enumerate(BIG)]
        rcvs = _exchange_partials(parts, "exchange_partials" + t)
        reds = [_add_chips(parts[k], rcvs[k], chip.reshape(1), "add_chips_" + n + t) for k, n in enumerate(BIG)]
        others = _share_halves(reds, "share_halves" + t)
        for k, n in enumerate(BIG):
            results[n] = _adamw_layer(core.reshape(1), Wt[n], Mo[n], Vo[n], reds[k], others[k], l, results[n],
                                      "adamw_" + n + t)

    small_g = {n: (dP[(n, None)] if Wt[n].ndim == 1 else jnp.stack([dP[(n, l)] for l in range(DEPTH)])) for n in SMALL}
    local_loss = 0.5 * jnp.sum(sqerr) / D_MODEL
    small_sum = _sum_small(_allreduce_small(_flatten_small(small_g, local_loss))).reshape(-1)
    layout, n_small = _small_layout(Wt)
    loss = small_sum[n_small]

    grads, deltas, new_m, new_v = {}, {}, {}, {}
    for n in BIG:
        grads[n], deltas[n], new_m[n], new_v[n] = results[n]
    zero = jnp.zeros((), F32)
    d, mn, vn = _adamw(_flatten_small(Wt, zero), small_sum.reshape(SMALL_ROWS, FLAT_W), _flatten_small(Mo, zero),
                       _flatten_small(Vo, zero), "adamw_small")
    for n in SMALL:
        at, size = layout[n]
        pick = lambda a: a.reshape(-1)[at:at + size].reshape(Wt[n].shape)
        grads[n], deltas[n], new_m[n], new_v[n] = pick(small_sum), pick(d), pick(mn), pick(vn)

    return (loss, grad_x[None], *[grads[n] for n in WEIGHTS], *[deltas[n] for n in WEIGHTS],
            *[new_m[n] for n in WEIGHTS], *[new_v[n] for n in WEIGHTS])
```

```python
import functools

import jax
import jax.numpy as jnp
from jax import lax
from jax.experimental import pallas as pl
from jax.experimental.pallas import tpu as pltpu

F32 = jnp.float32
BF16 = jnp.bfloat16

D_MODEL = 2048
DEPTH = 2
CHUNK = 64
MLA_HEADS = 8
Q_LORA = 512
KV_LORA = 256
NOPE = 128
ROPE = 64
VDIM = 128
RET_HEADS = 4
RET_DK = 256
RET_DV = 256
D_FF = 5632
D_IN = 4928
ROPE_THETA = 10000.0
LN_EPS = 1e-5
RMS_EPS = 1e-6
GN_EPS = 1e-5
ALPHA = (2 * DEPTH) ** 0.25
MLA_SCALE = (NOPE + ROPE) ** -0.5
RET_SCALE = RET_DK ** -0.5
ADAM_LR = 0.001
ADAM_B1 = 0.9
ADAM_B2 = 0.999
ADAM_EPS = 1e-08
ADAM_WD = 0.01
ADAM_STEP = 10

LANES = 128
HEAD_PAD = 256
MLA_IN = 1024
MLA_IN_USED = Q_LORA + KV_LORA + ROPE
D_IN_PAD = MLA_IN + 4 * 1024
ATT_BLOCK = 512
NEG = -1e30
VMEM_LIMIT = 56 * 1024 * 1024

N_CHIPS = 4
FLAT_W = 1024
BIG = ("w_in", "w_uq", "w_ukv", "w_out", "w_gate", "w_up", "w_down")
BIG_SHARD = {"w_in": (2048, 1232), "w_uq": (512, 384), "w_ukv": (256, 512), "w_out": (512, 2048),
             "w_gate": (2048, 1408), "w_up": (2048, 1408), "w_down": (1408, 2048)}
SMALL = ("ln_in_g", "ln_in_b", "q_norm_g", "kv_norm_g", "ret_gn_g", "ret_gn_b", "ln1_g", "ln1_b", "ln2_g", "ln2_b")
WEIGHTS = ("ln_in_g", "ln_in_b", "w_in", "q_norm_g", "kv_norm_g", "w_uq", "w_ukv", "ret_gn_g", "ret_gn_b", "w_out",
           "ln1_g", "ln1_b", "w_gate", "w_up", "w_down", "ln2_g", "ln2_b")
SMALL_ROWS = 32

MESH = pl.DeviceIdType.MESH


def _pick(dim, cands):
    for c in cands:
        if dim % c == 0:
            return c
    return dim


HBM = pl.BlockSpec(memory_space=pltpu.HBM)


class _Side:
    def __init__(self, arrays, out_shape, scratch, start, finish):
        self.arrays, self.out_shape, self.scratch, self.start, self.finish = arrays, out_shape, scratch, start, finish


def _call(body, name, out_shape, grid, in_specs, out_specs, scratch=(), sem=None, side=None):
    params = pltpu.CompilerParams(dimension_semantics=sem if side is None else ("arbitrary",) * len(grid),
                                  vmem_limit_bytes=VMEM_LIMIT)
    if side is None:
        return pl.pallas_call(body, name=name, out_shape=out_shape, grid=grid, in_specs=in_specs, out_specs=out_specs,
                              scratch_shapes=list(scratch), compiler_params=params)
    single = not isinstance(out_shape, (list, tuple))
    outs = [out_shape] if single else list(out_shape)
    ospecs = [out_specs] if single else list(out_specs)
    cuts = [len(in_specs), len(side.arrays), len(outs), len(side.out_shape), len(scratch)]
    ends = [sum(cuts[:k + 1]) for k in range(len(cuts))]

    def hosted(*refs):
        ins, s_in, o, s_out, scr = (refs[a:b] for a, b in zip([0] + ends[:-1], ends))
        sems = refs[ends[-1]:]
        ids = [pl.program_id(a) for a in range(len(grid))]
        first = functools.reduce(jnp.logical_and, [i == 0 for i in ids])
        last = functools.reduce(jnp.logical_and, [i == g - 1 for i, g in zip(ids, grid)])

        @pl.when(first)
        def _():
            side.start(s_in, s_out, sems)

        body(*ins, *o, *scr)

        @pl.when(last)
        def _():
            side.finish(s_in, s_out, sems)

    call = pl.pallas_call(hosted, name=name, out_shape=outs + list(side.out_shape), grid=grid,
                          in_specs=list(in_specs) + [HBM] * len(side.arrays),
                          out_specs=ospecs + [HBM] * len(side.out_shape),
                          scratch_shapes=list(scratch) + list(side.scratch), compiler_params=params)

    def run(*args):
        res = call(*args, *side.arrays)
        return (res[0] if single else list(res[:len(outs)])), list(res[len(outs):])

    return run


def _rows(tm, w, col=0):
    return pl.BlockSpec((tm, w), lambda i: (i, col))


def _whole(shape):
    return pl.BlockSpec(shape, lambda i: (0,) * len(shape))


def _sds(shape, dtype):
    return jax.ShapeDtypeStruct(shape, dtype)


def _matmul(a, b, name, ta=False, tb=False, out_dtype=F32, side=None):
    (K, M) = a.shape if ta else a.shape[::-1]
    (N, Kb) = b.shape if tb else b.shape[::-1]
    assert K == Kb, (a.shape, b.shape, ta, tb)
    tm = _pick(M, (512, 256, 128))
    tn = _pick(N, (1024, 512, 256, 128))
    tk = _pick(K, (2048, 1408, 1280, 1024, 512, 256))
    nk = K // tk
    dn = (((0 if ta else 1,), (1 if tb else 0,)), ((), ()))

    def body(a_ref, b_ref, o_ref, acc_ref):
        k = pl.program_id(2)
        p = lax.dot_general(a_ref[...].astype(BF16), b_ref[...].astype(BF16), dn, preferred_element_type=F32)
        if nk == 1:
            o_ref[...] = p.astype(out_dtype)
        else:
            @pl.when(k == 0)
            def _():
                acc_ref[...] = p

            @pl.when(jnp.logical_and(k > 0, k < nk - 1))
            def _():
                acc_ref[...] += p

            @pl.when(k == nk - 1)
            def _():
                o_ref[...] = (acc_ref[...] + p).astype(out_dtype)

    a_spec = pl.BlockSpec((tk, tm), lambda i, j, k: (k, i)) if ta else pl.BlockSpec((tm, tk), lambda i, j, k: (i, k))
    b_spec = pl.BlockSpec((tn, tk), lambda i, j, k: (j, k)) if tb else pl.BlockSpec((tk, tn), lambda i, j, k: (k, j))
    return _call(body, name, _sds((M, N), out_dtype), (M // tm, N // tn, nk), [a_spec, b_spec],
                 pl.BlockSpec((tm, tn), lambda i, j, k: (i, j)), scratch=[pltpu.VMEM((tm, tn), F32)],
                 sem=("parallel", "parallel", "arbitrary"), side=side)(a, b)


def _sigmoid(x):
    return 1.0 / (1.0 + jnp.exp(-x))


def _rope_group(r, c, sa, sb):
    return r * c + pltpu.roll(r, 32, 1) * sa + pltpu.roll(r, 96, 1) * sb


def _ln_fwd(xs, coefs, g, b, name, want_z):
    S, D = xs[0].shape
    tm = 256
    n = len(xs)

    def body(*refs):
        x_refs, g_ref, b_ref, outs = refs[:n], refs[n], refs[n + 1], refs[n + 2:]
        z = None
        for cf, r in zip(coefs, x_refs):
            t = r[...] if cf == 1.0 else cf * r[...]
            z = t if z is None else z + t
        mu = jnp.mean(z, axis=-1, keepdims=True)
        zc = z - mu
        var = jnp.mean(zc * zc, axis=-1, keepdims=True)
        y = zc * lax.rsqrt(var + LN_EPS) * g_ref[...] + b_ref[...]
        if want_z:
            outs[0][...] = z
        outs[-2][...] = y
        outs[-1][...] = y.astype(BF16)

    out_shape = [_sds((S, D), F32)] * (2 if want_z else 1) + [_sds((S, D), BF16)]
    return _call(body, name, out_shape, (S // tm,), [_rows(tm, D)] * n + [_whole((1, D))] * 2,
                 [_rows(tm, D)] * len(out_shape), sem=("parallel",))(*xs, g, b)


def _ln_bwd(dys, coefs, z, g, name):
    S, D = z.shape
    tm = 256
    n = len(dys)

    def body(*refs):
        dy_refs, z_ref, g_ref = refs[:n], refs[n], refs[n + 1]
        dz_ref, dzb_ref, dg_ref, db_ref = refs[n + 2:]
        dy = None
        for cf, r in zip(coefs, dy_refs):
            t = r[...] if cf == 1.0 else cf * r[...]
            dy = t if dy is None else dy + t
        zv = z_ref[...]
        mu = jnp.mean(zv, axis=-1, keepdims=True)
        zc = zv - mu
        var = jnp.mean(zc * zc, axis=-1, keepdims=True)
        rstd = lax.rsqrt(var + LN_EPS)
        xh = zc * rstd
        dyg = dy * g_ref[...]
        dz = rstd * (dyg - jnp.mean(dyg, axis=-1, keepdims=True) - xh * jnp.mean(dyg * xh, axis=-1, keepdims=True))
        dz_ref[...] = dz
        dzb_ref[...] = dz.astype(BF16)

        @pl.when(pl.program_id(0) == 0)
        def _():
            dg_ref[...] = jnp.zeros_like(dg_ref)
            db_ref[...] = jnp.zeros_like(db_ref)

        dg_ref[...] += jnp.sum(dy * xh, axis=0, keepdims=True)
        db_ref[...] += jnp.sum(dy, axis=0, keepdims=True)

    return _call(body, name, [_sds((S, D), F32), _sds((S, D), BF16), _sds((1, D), F32), _sds((1, D), F32)],
                 (S // tm,), [_rows(tm, D)] * (n + 1) + [_whole((1, D))],
                 [_rows(tm, D), _rows(tm, D), _whole((1, D)), _whole((1, D))], sem=("arbitrary",))(*dys, z, g)


def _rms(x, g):
    return x * lax.rsqrt(jnp.mean(x * x, axis=-1, keepdims=True) + RMS_EPS) * g


def _prep1(h, tabs, qg, kvg, name):
    S = h.shape[0]
    tm = 256
    cm, sam, sbm, cr, sr = tabs

    def body(h_ref, cm_ref, sam_ref, sbm_ref, cr_ref, sr_ref, qg_ref, kvg_ref,
             qn_ref, kvn_ref, kr_ref, rq_ref, rk_ref, rv_ref):
        qn_ref[...] = _rms(h_ref[:, 0:Q_LORA], qg_ref[...]).astype(BF16)
        kvn_ref[...] = _rms(h_ref[:, Q_LORA:Q_LORA + KV_LORA], kvg_ref[...]).astype(BF16)
        kr_ref[...] = _rope_group(h_ref[:, 768:896], cm_ref[...], sam_ref[...], sbm_ref[...])
        c, s = cr_ref[...], sr_ref[...]
        for hd in range(RET_HEADS):
            for src, dst, scale in ((MLA_IN, rq_ref, RET_SCALE), (MLA_IN + 1024, rk_ref, None)):
                t1 = h_ref[:, src + hd * 256:src + hd * 256 + 128]
                t2 = h_ref[:, src + hd * 256 + 128:src + hd * 256 + 256]
                o1, o2 = t1 * c - t2 * s, t2 * c + t1 * s
                if scale is not None:
                    o1, o2 = o1 * scale, o2 * scale
                dst[:, hd * 256:hd * 256 + 128] = o1.astype(BF16)
                dst[:, hd * 256 + 128:hd * 256 + 256] = o2.astype(BF16)
        rv_ref[...] = h_ref[:, MLA_IN + 2048:MLA_IN + 3072].astype(BF16)

    t128 = _rows(tm, LANES)
    return _call(body, name,
                 [_sds((S, Q_LORA), BF16), _sds((S, KV_LORA), BF16), _sds((S, LANES), F32),
                  _sds((S, 1024), BF16), _sds((S, 1024), BF16), _sds((S, 1024), BF16)],
                 (S // tm,),
                 [_rows(tm, D_IN_PAD), t128, t128, t128, t128, t128, _whole((1, Q_LORA)), _whole((1, KV_LORA))],
                 [_rows(tm, Q_LORA), _rows(tm, KV_LORA), t128, _rows(tm, 1024), _rows(tm, 1024), _rows(tm, 1024)],
                 sem=("parallel",))(h, cm, sam, sbm, cr, sr, qg, kvg)


def _prep1_bwd(dqn, dkvn, dkr, drq, drk, drv, drg, h, tabs, qg, kvg, name):
    S = h.shape[0]
    tm = 256
    cm, sam, sbm, cr, sr = tabs

    def rms_bwd(x, g, dy):
        r = lax.rsqrt(jnp.mean(x * x, axis=-1, keepdims=True) + RMS_EPS)
        dyg = dy * g
        dx = r * dyg - x * (r * r * r) * jnp.mean(dyg * x, axis=-1, keepdims=True)
        return dx, jnp.sum(dy * x * r, axis=0, keepdims=True)

    def body(dqn_ref, dkvn_ref, dkr_ref, drq_ref, drk_ref, drv_ref, drg_ref, h_ref,
             cm_ref, sam_ref, sbm_ref, cr_ref, sr_ref, qg_ref, kvg_ref, dh_ref, dqg_ref, dkvg_ref):
        dcq, dqg = rms_bwd(h_ref[:, 0:Q_LORA], qg_ref[...], dqn_ref[...])
        dckv, dkvg = rms_bwd(h_ref[:, Q_LORA:Q_LORA + KV_LORA], kvg_ref[...], dkvn_ref[...])
        dh_ref[:, 0:Q_LORA] = dcq.astype(BF16)
        dh_ref[:, Q_LORA:Q_LORA + KV_LORA] = dckv.astype(BF16)
        dh_ref[:, 768:896] = _rope_group(dkr_ref[...], cm_ref[...], -sam_ref[...], -sbm_ref[...]).astype(BF16)
        dh_ref[:, 896:1024] = jnp.zeros((tm, LANES), BF16)
        c, s = cr_ref[...], sr_ref[...]
        for hd in range(RET_HEADS):
            for src, dst, scale in ((drq_ref, MLA_IN, RET_SCALE), (drk_ref, MLA_IN + 1024, None)):
                d1 = src[:, hd * 256:hd * 256 + 128]
                d2 = src[:, hd * 256 + 128:hd * 256 + 256]
                if scale is not None:
                    d1, d2 = d1 * scale, d2 * scale
                dh_ref[:, dst + hd * 256:dst + hd * 256 + 128] = (d1 * c + d2 * s).astype(BF16)
                dh_ref[:, dst + hd * 256 + 128:dst + hd * 256 + 256] = (d2 * c - d1 * s).astype(BF16)
        dh_ref[:, MLA_IN + 2048:MLA_IN + 3072] = drv_ref[...].astype(BF16)
        dh_ref[:, MLA_IN + 3072:MLA_IN + 4096] = drg_ref[...].astype(BF16)

        @pl.when(pl.program_id(0) == 0)
        def _():
            dqg_ref[...] = jnp.zeros_like(dqg_ref)
            dkvg_ref[...] = jnp.zeros_like(dkvg_ref)

        dqg_ref[...] += dqg
        dkvg_ref[...] += dkvg

    t128 = _rows(tm, LANES)
    return _call(body, name,
                 [_sds((S, D_IN_PAD), BF16), _sds((1, Q_LORA), F32), _sds((1, KV_LORA), F32)],
                 (S // tm,),
                 [_rows(tm, Q_LORA), _rows(tm, KV_LORA), t128, _rows(tm, 1024), _rows(tm, 1024), _rows(tm, 1024),
                  _rows(tm, 1024), _rows(tm, D_IN_PAD), t128, t128, t128, t128, t128,
                  _whole((1, Q_LORA)), _whole((1, KV_LORA))],
                 [_rows(tm, D_IN_PAD), _whole((1, Q_LORA)), _whole((1, KV_LORA))],
                 sem=("arbitrary",))(dqn, dkvn, dkr, drq, drk, drv, drg, h, cm, sam, sbm, cr, sr, qg, kvg)


def _prep2(q, kv, kr, tabs, name):
    S = q.shape[0]
    tm = 256
    cm, sam, sbm = tabs[:3]

    def body(q_ref, kv_ref, kr_ref, cm_ref, sam_ref, sbm_ref, qo_ref, ko_ref, vo_ref):
        c, sa, sb = cm_ref[...], sam_ref[...], sbm_ref[...]
        krb = kr_ref[...].astype(BF16)
        for hd in range(MLA_HEADS):
            o = hd * HEAD_PAD
            qo_ref[:, o:o + 128] = q_ref[:, o:o + 128].astype(BF16)
            qo_ref[:, o + 128:o + 256] = _rope_group(q_ref[:, o + 128:o + 256], c, sa, sb).astype(BF16)
            ko_ref[:, o:o + 128] = kv_ref[:, hd * 128:hd * 128 + 128].astype(BF16)
            ko_ref[:, o + 128:o + 256] = krb
        vo_ref[...] = kv_ref[:, 1024:2048].astype(BF16)

    t128 = _rows(tm, LANES)
    return _call(body, name, [_sds((S, 2048), BF16), _sds((S, 2048), BF16), _sds((S, 1024), BF16)], (S // tm,),
                 [_rows(tm, 2048), _rows(tm, 2048), t128, t128, t128, t128],
                 [_rows(tm, 2048), _rows(tm, 2048), _rows(tm, 1024)], sem=("parallel",))(q, kv, kr, cm, sam, sbm)


def _prep2_bwd(dqm, dkm, dvm, tabs, name):
    S = dqm.shape[0]
    tm = 256
    cm, sam, sbm = tabs[:3]

    def body(dq_ref, dk_ref, dv_ref, cm_ref, sam_ref, sbm_ref, dqo_ref, dkvo_ref, dkr_ref):
        c, sa, sb = cm_ref[...], -sam_ref[...], -sbm_ref[...]
        dkr = None
        for hd in range(MLA_HEADS):
            o = hd * HEAD_PAD
            dqo_ref[:, o:o + 128] = dq_ref[:, o:o + 128].astype(BF16)
            dqo_ref[:, o + 128:o + 256] = _rope_group(dq_ref[:, o + 128:o + 256], c, sa, sb).astype(BF16)
            dkvo_ref[:, hd * 128:hd * 128 + 128] = dk_ref[:, o:o + 128].astype(BF16)
            t = dk_ref[:, o + 128:o + 256]
            dkr = t if dkr is None else dkr + t
        dkvo_ref[:, 1024:2048] = dv_ref[...].astype(BF16)
        dkr_ref[...] = dkr

    t128 = _rows(tm, LANES)
    return _call(body, name, [_sds((S, 2048), BF16), _sds((S, 2048), BF16), _sds((S, LANES), F32)], (S // tm,),
                 [_rows(tm, 2048), _rows(tm, 2048), _rows(tm, 1024), t128, t128, t128],
                 [_rows(tm, 2048), _rows(tm, 2048), t128], sem=("parallel",))(dqm, dkm, dvm, cm, sam, sbm)


def _gn_gate(a, o, h, gg, gb, name):
    S = a.shape[0]
    tm = 256

    def body(a_ref, o_ref, rg_ref, gg_ref, gb_ref, mix_ref):
        mix_ref[:, 0:1024] = a_ref[...].astype(BF16)
        for hd in range(RET_HEADS):
            sl = slice(hd * 256, hd * 256 + 256)
            ov = o_ref[:, sl]
            mu = jnp.mean(ov, axis=-1, keepdims=True)
            oc = ov - mu
            var = jnp.mean(oc * oc, axis=-1, keepdims=True)
            y = oc * lax.rsqrt(var + GN_EPS) * gg_ref[:, sl] + gb_ref[:, sl]
            rg = rg_ref[:, sl]
            mix_ref[:, 1024 + hd * 256:1024 + hd * 256 + 256] = (rg * _sigmoid(rg) * y).astype(BF16)

    return _call(body, name, _sds((S, 2048), BF16), (S // tm,),
                 [_rows(tm, 1024), _rows(tm, 1024), _rows(tm, 1024, 4), _whole((1, 1024)), _whole((1, 1024))],
                 _rows(tm, 2048), sem=("parallel",))(a, o, h, gg, gb)


def _gn_gate_bwd(dmixin, o, h, gg, gb, name):
    S = o.shape[0]
    tm = 256

    def body(dr_ref, o_ref, rg_ref, gg_ref, gb_ref, do_ref, drg_ref, dgg_ref, dgb_ref):
        @pl.when(pl.program_id(0) == 0)
        def _():
            dgg_ref[...] = jnp.zeros_like(dgg_ref)
            dgb_ref[...] = jnp.zeros_like(dgb_ref)

        for hd in range(RET_HEADS):
            sl = slice(hd * 256, hd * 256 + 256)
            ov = o_ref[:, sl]
            mu = jnp.mean(ov, axis=-1, keepdims=True)
            oc = ov - mu
            var = jnp.mean(oc * oc, axis=-1, keepdims=True)
            rstd = lax.rsqrt(var + GN_EPS)
            xh = oc * rstd
            g = gg_ref[:, sl]
            y = xh * g + gb_ref[:, sl]
            rg = rg_ref[:, sl]
            sg = _sigmoid(rg)
            dr = dr_ref[:, sl]
            dy = dr * (rg * sg)
            drg_ref[:, sl] = dr * y * (sg * (1.0 + rg * (1.0 - sg)))
            dgg_ref[:, sl] += jnp.sum(dy * xh, axis=0, keepdims=True)
            dgb_ref[:, sl] += jnp.sum(dy, axis=0, keepdims=True)
            dxh = dy * g
            do = rstd * (dxh - jnp.mean(dxh, axis=-1, keepdims=True) - xh * jnp.mean(dxh * xh, axis=-1, keepdims=True))
            do_ref[:, sl] = do.astype(BF16)

    return _call(body, name,
                 [_sds((S, 1024), BF16), _sds((S, 1024), F32), _sds((1, 1024), F32), _sds((1, 1024), F32)],
                 (S // tm,),
                 [_rows(tm, 1024, 1), _rows(tm, 1024), _rows(tm, 1024, 4), _whole((1, 1024)), _whole((1, 1024))],
                 [_rows(tm, 1024), _rows(tm, 1024), _whole((1, 1024)), _whole((1, 1024))],
                 sem=("arbitrary",))(dmixin, o, h, gg, gb)


def _swiglu(gu, name):
    S = gu.shape[0]
    tm = 256

    def body(g_ref, u_ref, o_ref):
        g = g_ref[...]
        o_ref[...] = (g * _sigmoid(g) * u_ref[...]).astype(BF16)

    return _call(body, name, _sds((S, D_FF), BF16), (S // tm,), [_rows(tm, D_FF, 0), _rows(tm, D_FF, 1)],
                 _rows(tm, D_FF), sem=("parallel",))(gu, gu)


def _swiglu_bwd(gu, dact, name):
    S = gu.shape[0]
    tm = 128

    def body(g_ref, u_ref, d_ref, o_ref):
        g, u, d = g_ref[...], u_ref[...], d_ref[...]
        sg = _sigmoid(g)
        o_ref[:, 0:D_FF] = (d * u * (sg * (1.0 + g * (1.0 - sg)))).astype(BF16)
        o_ref[:, D_FF:2 * D_FF] = (d * (g * sg)).astype(BF16)

    return _call(body, name, _sds((S, 2 * D_FF), BF16), (S // tm,),
                 [_rows(tm, D_FF, 0), _rows(tm, D_FF, 1), _rows(tm, D_FF)], _rows(tm, 2 * D_FF),
                 sem=("parallel",))(gu, gu, dact)


def _loss_head(y, target, name):
    S, D = y.shape
    tm = 256

    def body(y_ref, t_ref, dy_ref, acc_ref):
        e = y_ref[...] - t_ref[...]
        dy_ref[...] = e / D

        @pl.when(pl.program_id(0) == 0)
        def _():
            acc_ref[...] = jnp.zeros_like(acc_ref)

        acc_ref[...] += jnp.sum(e * e, axis=0, keepdims=True)

    return _call(body, name, [_sds((S, D), F32), _sds((1, D), F32)], (S // tm,), [_rows(tm, D), _rows(tm, D)],
                 [_rows(tm, D), _whole((1, D))], sem=("arbitrary",))(y, target)


def _chunk_mask(T):
    r = lax.shift_right_logical(lax.broadcasted_iota(jnp.int32, (T, T), 0), 6)
    c = lax.shift_right_logical(lax.broadcasted_iota(jnp.int32, (T, T), 1), 6)
    return r >= c


def _dot_nt(a, b):
    return lax.dot_general(a, b, (((1,), (1,)), ((), ())), preferred_element_type=F32)


def _dot_tn(a, b):
    return lax.dot_general(a, b, (((0,), (0,)), ((), ())), preferred_element_type=F32)


def _decay_tables(T):
    lg = jnp.log1p(-jnp.exp2(-5.0 - jnp.arange(RET_HEADS, dtype=F32)))
    idx = jnp.arange(T, dtype=F32)
    diff = idx[:, None] - idx[None, :]
    rel = jnp.exp(lg[:, None, None] * diff[None])
    cid = jnp.arange(T) // CHUNK
    mask = (cid[:, None] >= cid[None, :]).astype(F32)
    reld = jnp.exp(lg[:, None, None] * jnp.abs(diff)[None]) * mask[None]
    lgrow = jnp.broadcast_to(lg[:, None, None], (RET_HEADS, 1, LANES))
    return lgrow, rel, reld


def _attn_fwd(q, k, v, heads, dk, dv, softmax, name, tables=None, side=None):
    S = q.shape[0]
    T = ATT_BLOCK
    nq = S // T
    rep = T // LANES

    def body(*refs):
        if softmax:
            q_ref, k_ref, v_ref, o_ref, lse_ref, m_sc, l_sc, acc_sc = refs
        else:
            q_ref, k_ref, v_ref, lg_ref, rel_ref, reld_ref, o_ref, acc_sc = refs
        i = pl.program_id(1)
        qv = q_ref[...]

        def kv_block(j):
            rows = pl.ds(pl.multiple_of(j * T, T), T)
            return k_ref[rows, :], v_ref[rows, :]

        kb, vb = kv_block(i)
        s = _dot_nt(qv, kb)
        if softmax:
            s = jnp.where(_chunk_mask(T), s * MLA_SCALE, NEG)
            m = jnp.max(s, axis=-1, keepdims=True)
            p = jnp.exp(s - m)
            m_sc[...] = jnp.broadcast_to(m, (T, LANES))
            l_sc[...] = jnp.broadcast_to(jnp.sum(p, axis=-1, keepdims=True), (T, LANES))
        else:
            p = s * reld_ref[0]
        acc_sc[...] = jnp.dot(p.astype(BF16), vb, preferred_element_type=F32)

        def step(j, carry):
            kb, vb = kv_block(j)
            s = _dot_nt(qv, kb)
            if softmax:
                s = s * MLA_SCALE
                m_prev = m_sc[...]
                m_next = jnp.maximum(m_prev, jnp.max(s, axis=-1, keepdims=True))
                alpha = jnp.exp(m_prev - m_next)
                p = jnp.exp(s - jnp.tile(m_next, (1, rep)))
                l_sc[...] = alpha * l_sc[...] + jnp.sum(p, axis=-1, keepdims=True)
                m_sc[...] = m_next
                acc_sc[...] = acc_sc[...] * jnp.tile(alpha, (1, dv // LANES)) + jnp.dot(
                    p.astype(BF16), vb, preferred_element_type=F32)
            else:
                fac = jnp.exp(lg_ref[0] * ((i - j) * T).astype(F32))
                p = s * (rel_ref[0] * jnp.tile(fac, (1, rep)))
                acc_sc[...] += jnp.dot(p.astype(BF16), vb, preferred_element_type=F32)
            return carry

        lax.fori_loop(0, i, step, 0)
        if softmax:
            l = l_sc[...]
            o_ref[...] = acc_sc[...] / jnp.tile(l, (1, dv // LANES))
            lse_ref[...] = m_sc[...] + jnp.log(l)
        else:
            o_ref[...] = acc_sc[...]

    in_specs = [pl.BlockSpec((T, dk), lambda h, i: (i, h)), pl.BlockSpec((S, dk), lambda h, i: (0, h)),
                pl.BlockSpec((S, dv), lambda h, i: (0, h))]
    o_spec = pl.BlockSpec((T, dv), lambda h, i: (i, h))
    if softmax:
        return _call(body, name, [_sds((S, heads * dv), F32), _sds((S, heads * LANES), F32)], (heads, nq), in_specs,
                     [o_spec, pl.BlockSpec((T, LANES), lambda h, i: (i, h))],
                     scratch=[pltpu.VMEM((T, LANES), F32), pltpu.VMEM((T, LANES), F32), pltpu.VMEM((T, dv), F32)],
                     sem=("parallel", "arbitrary"), side=side)(q, k, v)
    lgrow, rel, reld = tables
    in_specs += [pl.BlockSpec((1, 1, LANES), lambda h, i: (h, 0, 0)), pl.BlockSpec((1, T, T), lambda h, i: (h, 0, 0)),
                 pl.BlockSpec((1, T, T), lambda h, i: (h, 0, 0))]
    return _call(body, name, _sds((S, heads * dv), F32), (heads, nq), in_specs, o_spec,
                 scratch=[pltpu.VMEM((T, dv), F32)], sem=("parallel", "arbitrary"), side=side)(q, k, v, lgrow, rel, reld)


def _attn_bwd(q, k, v, do, heads, dk, dv, softmax, name, o=None, lse=None, tables=None, side=None):
    S = q.shape[0]
    T = ATT_BLOCK
    nq = S // T
    rep = T // LANES

    def body(*refs):
        if softmax:
            q_ref, k_ref, v_ref, do_ref, o_ref, lse_ref, dq_ref, dk_ref, dv_ref, dq_sc = refs
        else:
            q_ref, k_ref, v_ref, do_ref, lg_ref, rel_ref, reld_ref, dq_ref, dk_ref, dv_ref, dq_sc = refs
        i = pl.program_id(1)

        @pl.when(i == 0)
        def _():
            dk_ref[...] = jnp.zeros_like(dk_ref)
            dv_ref[...] = jnp.zeros_like(dv_ref)

        qv = q_ref[...]
        dof = do_ref[...].astype(F32)
        dov = dof.astype(BF16)
        if softmax:
            delta = jnp.sum(dof * o_ref[...], axis=-1, keepdims=True)
            lse_t = jnp.tile(lse_ref[...], (1, rep))
        dq_sc[...] = jnp.zeros_like(dq_sc)

        def block(j, diagonal):
            rows = pl.ds(pl.multiple_of(j * T, T), T)
            kb, vb = k_ref[rows, :], v_ref[rows, :]
            s = _dot_nt(qv, kb)
            dp = _dot_nt(dov, vb)
            if softmax:
                s = s * MLA_SCALE
                if diagonal:
                    s = jnp.where(_chunk_mask(T), s, NEG)
                p = jnp.exp(s - lse_t)
                ds = p * (dp - delta) * MLA_SCALE
            else:
                if diagonal:
                    dec = reld_ref[0]
                else:
                    fac = jnp.exp(lg_ref[0] * ((i - j) * T).astype(F32))
                    dec = rel_ref[0] * jnp.tile(fac, (1, rep))
                p = s * dec
                ds = dp * dec
            dsb = ds.astype(BF16)
            dv_ref[rows, :] += _dot_tn(p.astype(BF16), dov)
            dk_ref[rows, :] += _dot_tn(dsb, qv)
            dq_sc[...] += jnp.dot(dsb, kb, preferred_element_type=F32)

        block(i, True)

        def step(j, carry):
            block(j, False)
            return carry

        lax.fori_loop(0, i, step, 0)
        dq_ref[...] = dq_sc[...]

    qspec = pl.BlockSpec((T, dk), lambda h, i: (i, h))
    kspec = pl.BlockSpec((S, dk), lambda h, i: (0, h))
    vspec = pl.BlockSpec((S, dv), lambda h, i: (0, h))
    dospec = pl.BlockSpec((T, dv), lambda h, i: (i, h))
    in_specs = [qspec, kspec, vspec, dospec]
    args = [q, k, v, do]
    if softmax:
        in_specs += [dospec, pl.BlockSpec((T, LANES), lambda h, i: (i, h))]
        args += [o, lse]
    else:
        in_specs += [pl.BlockSpec((1, 1, LANES), lambda h, i: (h, 0, 0)),
                     pl.BlockSpec((1, T, T), lambda h, i: (h, 0, 0)), pl.BlockSpec((1, T, T), lambda h, i: (h, 0, 0))]
        args += list(tables)
    return _call(body, name, [_sds((S, heads * dk), F32), _sds((S, heads * dk), F32), _sds((S, heads * dv), F32)],
                 (heads, nq), in_specs, [qspec, kspec, vspec], scratch=[pltpu.VMEM((T, dk), F32)],
                 sem=("parallel", "arbitrary"), side=side)(*args)


def _rope_tables(pos):
    def tables(dim):
        inv_freq = ROPE_THETA ** (-jnp.arange(0, dim, 2, dtype=F32) / dim)
        ang = pos.astype(F32)[:, None] * inv_freq
        return jnp.cos(ang), jnp.sin(ang)

    cm, sm = tables(ROPE)
    S = pos.shape[0]
    z32, z64 = jnp.zeros((S, 32), F32), jnp.zeros((S, 64), F32)
    cr, sr = tables(RET_DK)
    return (jnp.concatenate([cm, cm, z64], 1), jnp.concatenate([z32, sm, z64], 1),
            jnp.concatenate([-sm, z32, z64], 1), cr, sr)


def _row(v):
    return v.reshape(1, -1).astype(F32)


def _local_step(x, pos, target, pipe, P):
    tabs = _rope_tables(pos)
    dtabs = _decay_tables(ATT_BLOCK)
    xf, xb = _ln_fwd([x], [1.0], _row(P["ln_in_g"]), _row(P["ln_in_b"]), "ln_in", False)
    pipe.gather_first()
    saved = []
    for l in range(DEPTH):
        w = functools.partial(pipe.weight, l)
        t = f"_l{l}"
        h = pipe.run(_matmul, "mm_h" + t, xb, w("w_in"))
        qn, kvn, kr, rq, rk, rv = _prep1(h, tabs, _row(P["q_norm_g"][l]), _row(P["kv_norm_g"][l]), "prep1" + t)
        q = _matmul(qn, w("w_uq"), "mm_q" + t)
        kv = _matmul(kvn, w("w_ukv"), "mm_kv" + t)
        qm, km, vm = _prep2(q, kv, kr, tabs, "prep2" + t)
        a, lse = pipe.run(_attn_fwd, "mla_fwd" + t, qm, km, vm, MLA_HEADS, HEAD_PAD, VDIM, True)
        o = pipe.run(_attn_fwd, "ret_fwd" + t, rq, rk, rv, RET_HEADS, RET_DK, RET_DV, False, tables=dtabs)
        mixin = _gn_gate(a, o, h, _row(P["ret_gn_g"][l]), _row(P["ret_gn_b"][l]), "gn_gate" + t)
        mix = _matmul(mixin, w("w_out"), "mm_mix" + t)
        z1, x1f, x1b = _ln_fwd([xf, mix], [ALPHA, 1.0], _row(P["ln1_g"][l]), _row(P["ln1_b"][l]), "ln1" + t, True)
        gu = pipe.run(_matmul, "mm_gu" + t, x1b, w("w_gu"))
        act = _swiglu(gu, "swiglu" + t)
        f = pipe.run(_matmul, "mm_down" + t, act, w("w_down"))
        z2, x2f, x2b = _ln_fwd([x1f, f], [ALPHA, 1.0], _row(P["ln2_g"][l]), _row(P["ln2_b"][l]), "ln2" + t, True)
        saved.append(dict(xb=xb, h=h, qn=qn, kvn=kvn, rq=rq, rk=rk, rv=rv, qm=qm, km=km, vm=vm, a=a, lse=lse, o=o,
                          mixin=mixin, z1=z1, x1b=x1b, gu=gu, act=act, z2=z2))
        xf, xb = x2f, x2b

    dy, sqerr = _loss_head(xf, target, "loss_head")
    dP = {}
    dys, coefs = [dy], [1.0]
    for l in reversed(range(DEPTH)):
        w, sv = functools.partial(pipe.weight, l), saved[l]
        t = f"_l{l}"
        dz2, dz2b, dg, db = _ln_bwd(dys, coefs, sv["z2"], _row(P["ln2_g"][l]), "ln2_bwd" + t)
        dP[("ln2_g", l)], dP[("ln2_b", l)] = dg, db
        pipe.reduce(l, w_down=_matmul(sv["act"], dz2b, "mm_dw_down" + t, ta=True, out_dtype=BF16))
        dact = pipe.run(_matmul, "mm_dact" + t, dz2b, w("w_down"), tb=True)
        dgu = _swiglu_bwd(sv["gu"], dact, "swiglu_bwd" + t)
        pipe.reduce(l, w_gu=_matmul(sv["x1b"], dgu, "mm_dw_gu" + t, ta=True, out_dtype=BF16))
        dx1 = pipe.run(_matmul, "mm_dx1" + t, dgu, w("w_gu"), tb=True)
        dz1, dz1b, dg, db = _ln_bwd([dz2, dx1], [ALPHA, 1.0], sv["z1"], _row(P["ln1_g"][l]), "ln1_bwd" + t)
        dP[("ln1_g", l)], dP[("ln1_b", l)] = dg, db
        pipe.reduce(l, w_out=_matmul(sv["mixin"], dz1b, "mm_dw_out" + t, ta=True, out_dtype=BF16))
        dmixin = _matmul(dz1b, w("w_out"), "mm_dmixin" + t, tb=True)
        do, drg, dgg, dgb = _gn_gate_bwd(dmixin, sv["o"], sv["h"], _row(P["ret_gn_g"][l]), _row(P["ret_gn_b"][l]),
                                         "gn_gate_bwd" + t)
        dP[("ret_gn_g", l)], dP[("ret_gn_b", l)] = dgg, dgb
        drq, drk, drv = pipe.run(_attn_bwd, "ret_bwd" + t, sv["rq"], sv["rk"], sv["rv"], do, RET_HEADS, RET_DK, RET_DV,
                                 False, tables=dtabs)
        dqm, dkm, dvm = _attn_bwd(sv["qm"], sv["km"], sv["vm"], dmixin, MLA_HEADS, HEAD_PAD, VDIM, True,
                                  "mla_bwd" + t, o=sv["a"], lse=sv["lse"])
        dq, dkv, dkr = _prep2_bwd(dqm, dkm, dvm, tabs, "prep2_bwd" + t)
        g_uq = _matmul(sv["qn"], dq, "mm_dw_uq" + t, ta=True, out_dtype=BF16)
        dqn = _matmul(dq, w("w_uq"), "mm_dqn" + t, tb=True)
        g_ukv = _matmul(sv["kvn"], dkv, "mm_dw_ukv" + t, ta=True, out_dtype=BF16)
        dkvn = _matmul(dkv, w("w_ukv"), "mm_dkvn" + t, tb=True)
        dh, dqg, dkvg = _prep1_bwd(dqn, dkvn, dkr, drq, drk, drv, drg, sv["h"], tabs, _row(P["q_norm_g"][l]),
                                   _row(P["kv_norm_g"][l]), "prep1_bwd" + t)
        dP[("q_norm_g", l)], dP[("kv_norm_g", l)] = dqg, dkvg
        pipe.reduce(l, w_uq=g_uq, w_ukv=g_ukv, w_in=_matmul(sv["xb"], dh, "mm_dw_in" + t, ta=True, out_dtype=BF16))
        dxl = pipe.run(_matmul, "mm_dxl" + t, dh, w("w_in"), tb=True)
        dys, coefs = [dz1, dxl], [ALPHA, 1.0]
    grad_x, _, dg, db = _ln_bwd(dys, coefs, x, _row(P["ln_in_g"]), "ln_in_bwd")
    dP[("ln_in_g", None)], dP[("ln_in_b", None)] = dg, db
    return sqerr, grad_x, dP


INTERNAL_OF = {"w_in": ("w_in",), "w_uq": ("w_uq",), "w_ukv": ("w_ukv",), "w_out": ("w_out",),
               "w_gu": ("w_gate", "w_up"), "w_down": ("w_down",)}


def _internal_weight(name, *blocks):
    cat = lambda parts: jnp.concatenate(parts, axis=1)
    cols = lambda b: cat([b[j] for j in range(N_CHIPS)])
    b = blocks[0]
    if name in ("w_out", "w_down"):
        return b.reshape(-1, b.shape[-1])
    if name == "w_gu":
        return cat([blk[j] for blk in blocks for j in range(N_CHIPS)])
    if name == "w_in":
        return cat([b[0][:, :MLA_IN_USED], jnp.zeros((D_MODEL, MLA_IN - MLA_IN_USED), BF16), b[0][:, MLA_IN_USED:]]
                   + [b[j] for j in range(1, N_CHIPS)])
    if name == "w_uq":
        uq, hw = cols(b), NOPE + ROPE
        pad = jnp.zeros((Q_LORA, HEAD_PAD - hw), BF16)
        return cat([p for h in range(MLA_HEADS) for p in (uq[:, h * hw:(h + 1) * hw], pad)])
    ukv = cols(b)
    return cat([ukv[:, 256 * h:256 * h + NOPE] for h in range(MLA_HEADS)]
               + [ukv[:, 256 * h + NOPE:256 * h + 256] for h in range(MLA_HEADS)])


def _grad_shards(name, g):
    cat = lambda parts: jnp.concatenate(parts, axis=1)
    if name in ("w_out", "w_down"):
        r = BIG_SHARD[name][0]
        return {name: [g[r * j:r * (j + 1)] for j in range(N_CHIPS)]}
    if name == "w_gu":
        cg = BIG_SHARD["w_gate"][1]
        return {"w_gate": [g[:, cg * j:cg * (j + 1)] for j in range(N_CHIPS)],
                "w_up": [g[:, D_FF + cg * j:D_FF + cg * (j + 1)] for j in range(N_CHIPS)]}
    if name == "w_in":
        ci, shift = BIG_SHARD["w_in"][1], MLA_IN - MLA_IN_USED
        return {name: [cat([g[:, :MLA_IN_USED], g[:, MLA_IN:ci + shift]])]
                + [g[:, ci * j + shift:ci * (j + 1) + shift] for j in range(1, N_CHIPS)]}
    if name == "w_uq":
        cq = NOPE + ROPE
        return {name: [cat([g[:, HEAD_PAD * h:HEAD_PAD * h + cq] for h in (2 * j, 2 * j + 1)]) for j in range(N_CHIPS)]}
    return {name: [cat([g[:, o + NOPE * h:o + NOPE * (h + 1)] for h in (2 * j, 2 * j + 1) for o in (0, MLA_HEADS * NOPE)])
                   for j in range(N_CHIPS)]}


def _small_layout(P):
    out, at = {}, 0
    for n in SMALL:
        out[n] = (at, P[n].size)
        at += P[n].size
    return out, at


def _flatten_small(P, last):
    v = jnp.concatenate([P[n].reshape(-1).astype(F32) for n in SMALL] + [last.reshape(-1).astype(F32)])
    return jnp.pad(v, (0, SMALL_ROWS * FLAT_W - v.size)).reshape(SMALL_ROWS, FLAT_W)


def _place():
    return lax.axis_index("x"), lax.axis_index("y"), lax.axis_index("c")


def _other_chips(x, y):
    return [(1 - x, y), (x, 1 - y), (1 - x, 1 - y)]


def _rcopy(src, dst, ssem, rsem, dev):
    return pltpu.make_async_remote_copy(src_ref=src, dst_ref=dst, send_sem=ssem, recv_sem=rsem, device_id=dev,
                                        device_id_type=MESH)


def _comm_call(body, name, out_shape, n_in, scratch):
    many = isinstance(out_shape, (list, tuple))
    return pl.pallas_call(body, name=name, out_shape=out_shape, in_specs=[HBM] * n_in,
                          out_specs=[HBM] * len(out_shape) if many else HBM, scratch_shapes=scratch)


def _half(ref, which):
    rows = ref.shape[0] // 2
    return ref.at[pl.ds(pl.multiple_of(which * rows, 16), rows)]


def _dma_sems(n):
    return pltpu.SemaphoreType.DMA((n,))


def _allgather_side(ws):
    k = len(ws)

    def copies(w_refs, g_refs, sems):
        ssem, rsem, fssem, frsem, ossem, orsem = sems
        x, y, c = _place()
        j = 2 * x + y
        sib = (x, y, 1 - c)
        nt = [(n, t, cx, cy) for n in range(k) for t, (cx, cy) in enumerate(_other_chips(x, y))]
        owns = [_rcopy(w_refs[n], g_refs[n].at[j], ossem.at[n], orsem.at[n], sib) for n in range(k)]
        sends = [_rcopy(_half(w_refs[n], c), _half(g_refs[n].at[j], c), ssem.at[3 * n + t], rsem.at[3 * n + t],
                        (cx, cy, c)) for n, t, cx, cy in nt]
        landed, passed, relayed = [], [], []
        for n, t, cx, cy in nt:
            mine, other = (_half(g_refs[n].at[2 * cx + cy], h) for h in (c, 1 - c))
            landed.append(_rcopy(mine, mine, ssem.at[3 * n + t], rsem.at[3 * n + t], (cx, cy, c)))
            passed.append(_rcopy(mine, mine, fssem.at[3 * n + t], frsem.at[3 * n + t], sib))
            relayed.append(_rcopy(other, other, fssem.at[3 * n + t], frsem.at[3 * n + t], sib))
        return owns, sends, landed, passed, relayed

    def start(w_refs, g_refs, sems):
        owns, sends, _, _, _ = copies(w_refs, g_refs, sems)
        for cp in sends + owns:
            cp.start()

    def finish(w_refs, g_refs, sems):
        owns, sends, landed, passed, relayed = copies(w_refs, g_refs, sems)
        for got, on in zip(landed, passed):
            got.wait_recv()
            on.start()
        for cp in relayed:
            cp.wait_recv()
        for cp in owns:
            cp.wait()
        for cp in sends + passed:
            cp.wait_send()

    return _Side(list(ws), [_sds((N_CHIPS,) + w.shape, w.dtype) for w in ws],
                 [_dma_sems(3 * k)] * 4 + [_dma_sems(k)] * 2, start, finish)


def _exchange_side(parts):
    k = len(parts)

    def copies(p_refs, rcv_refs, sems):
        ssem, rsem = sems
        x, y, c = _place()
        return [_rcopy(p_refs[n].at[2 * cx + cy], rcv_refs[n].at[t], ssem.at[3 * n + t], rsem.at[3 * n + t], (cx, cy, c))
                for n in range(k) for t, (cx, cy) in enumerate(_other_chips(x, y))]

    def start(p_refs, rcv_refs, sems):
        for cp in copies(p_refs, rcv_refs, sems):
            cp.start()

    def finish(p_refs, rcv_refs, sems):
        for cp in copies(p_refs, rcv_refs, sems):
            cp.wait()

    return _Side(list(parts), [_sds((3,) + p.shape[1:], p.dtype) for p in parts], [_dma_sems(3 * k)] * 2, start, finish)


def _run_side(side, name):
    k_in, k_out = len(side.arrays), len(side.out_shape)

    def body(*refs):
        parts = refs[:k_in], refs[k_in:k_in + k_out], refs[k_in + k_out:]
        side.start(*parts)
        side.finish(*parts)

    return _comm_call(body, name, list(side.out_shape), k_in, list(side.scratch))(*side.arrays)


def _swap_halves(gds, name):
    k = len(gds)

    def body(*refs):
        gd_refs, out_refs, (ssem, rsem) = refs[:k], refs[k:2 * k], refs[2 * k:]
        x, y, c = _place()
        cps = [_rcopy(_half(gd_refs[n].at[jj], 1 - c), out_refs[n].at[jj], ssem.at[N_CHIPS * n + jj],
                      rsem.at[N_CHIPS * n + jj], (x, y, 1 - c)) for n in range(k) for jj in range(N_CHIPS)]
        for cp in cps:
            cp.start()
        for cp in cps:
            cp.wait()

    sems = _dma_sems(N_CHIPS * k)
    return _comm_call(body, name, [_sds((N_CHIPS, g.shape[1] // 2, g.shape[2]), g.dtype) for g in gds], k,
                      [sems, sems])(*gds)


def _allreduce_small(small):
    def body(s_ref, all_ref, sssem, srsem, lsem):
        x, y, c = _place()
        me = 4 * x + 2 * y + c
        own = pltpu.make_async_copy(s_ref, all_ref.at[me], lsem)
        own.start()
        cps = []
        for r in range(1, 8):
            fx, fy, fc = (r >> 2) & 1, (r >> 1) & 1, r & 1
            px, py, pc = (1 - x if fx else x, 1 - y if fy else y, 1 - c if fc else c)
            peer = 4 * px + 2 * py + pc
            send = _rcopy(s_ref, all_ref.at[me], sssem.at[r - 1], srsem.at[me], (px, py, pc))
            send.start()
            cps.append((send, _rcopy(s_ref, all_ref.at[peer], sssem.at[r - 1], srsem.at[peer], (px, py, pc))))
        for send, recv in cps:
            send.wait_send()
            recv.wait_recv()
        own.wait()

    return _comm_call(body, "allreduce_small", [_sds((8,) + small.shape, small.dtype)], 1,
                      [pltpu.SemaphoreType.DMA((7,)), pltpu.SemaphoreType.DMA((8,)), pltpu.SemaphoreType.DMA(())])(small)[0]


def _share_halves(reds, name):
    k = len(reds)

    def body(*refs):
        r_refs, out_refs, (ssem, rsem) = refs[:k], refs[k:2 * k], refs[2 * k:]
        x, y, c = _place()
        cps = [_rcopy(r_refs[n], out_refs[n], ssem.at[n], rsem.at[n], (x, y, 1 - c)) for n in range(k)]
        for cp in cps:
            cp.start()
        for cp in cps:
            cp.wait()

    return _comm_call(body, name, [_sds(r.shape, r.dtype) for r in reds], k, [_dma_sems(k)] * 2)(*reds)


def _add_pair(gd, got, c, name):
    _, R, W = got.shape
    tm = _pick(R, (512, 256, 128, 64))
    nb = R // tm

    def body(c_ref, a_ref, b_ref, o_ref):
        o_ref[...] = (a_ref[...].astype(F32) + b_ref[...].astype(F32)).astype(o_ref.dtype)

    grid_spec = pltpu.PrefetchScalarGridSpec(
        num_scalar_prefetch=1, grid=(N_CHIPS, nb),
        in_specs=[pl.BlockSpec((None, tm, W), lambda j, i, c_ref: (j, c_ref[0] * nb + i, 0)),
                  pl.BlockSpec((None, tm, W), lambda j, i, c_ref: (j, i, 0))],
        out_specs=pl.BlockSpec((None, tm, W), lambda j, i, c_ref: (j, i, 0)))
    return pl.pallas_call(body, name=name, grid_spec=grid_spec, out_shape=_sds((N_CHIPS, R, W), gd.dtype),
                          compiler_params=pltpu.CompilerParams(dimension_semantics=("parallel", "parallel"),
                                                               vmem_limit_bytes=VMEM_LIMIT))(c, gd, got)


def _add_chips(part, rcv, j, name):
    _, R, W = part.shape
    tm = _pick(R, (512, 256, 128, 64))

    def body(j_ref, p_ref, r0_ref, r1_ref, r2_ref, o_ref):
        o_ref[...] = ((p_ref[...].astype(F32) + r0_ref[...].astype(F32)) + r1_ref[...].astype(F32)) + r2_ref[...].astype(F32)

    def slot(t):
        return pl.BlockSpec((None, tm, W), lambda i, j_ref: (t, i, 0))

    grid_spec = pltpu.PrefetchScalarGridSpec(
        num_scalar_prefetch=1, grid=(R // tm,),
        in_specs=[pl.BlockSpec((None, tm, W), lambda i, j_ref: (j_ref[0], i, 0)), slot(0), slot(1), slot(2)],
        out_specs=pl.BlockSpec((tm, W), lambda i, j_ref: (i, 0)))
    return pl.pallas_call(body, name=name, grid_spec=grid_spec, out_shape=_sds((R, W), F32),
                          compiler_params=pltpu.CompilerParams(dimension_semantics=("parallel",),
                                                               vmem_limit_bytes=VMEM_LIMIT))(j, part, rcv, rcv, rcv)


def _sum_small(allsmall):
    _, R, W = allsmall.shape

    def body(a_ref, o_ref):
        acc = a_ref[0]
        for d in range(1, 8):
            acc = acc + a_ref[d]
        o_ref[...] = acc

    return _call(body, "sum_small", _sds((R, W), F32), (1,), [_whole((8, R, W))], _whole((R, W)),
                 sem=("arbitrary",))(allsmall)


def _adamw(w, g, m, v, name):
    R, C = w.shape
    tm = _pick(R, (256, 128, 64, 32, 8))

    def body(w_ref, g_ref, m_ref, v_ref, d_ref, mo_ref, vo_ref):
        gv = g_ref[...]
        mn = ADAM_B1 * m_ref[...] + (1.0 - ADAM_B1) * gv
        vn = ADAM_B2 * v_ref[...] + (1.0 - ADAM_B2) * (gv * gv)
        m_hat = mn / (1.0 - ADAM_B1 ** ADAM_STEP)
        v_hat = vn / (1.0 - ADAM_B2 ** ADAM_STEP)
        d_ref[...] = -ADAM_LR * (m_hat / (jnp.sqrt(v_hat) + ADAM_EPS) + ADAM_WD * w_ref[...])
        mo_ref[...] = mn
        vo_ref[...] = vn

    spec = _rows(tm, C)
    return _call(body, name, [_sds((R, C), F32)] * 3, (R // tm,), [spec] * 4, [spec] * 3, sem=("parallel",))(w, g, m, v)


def _adamw_layer(c, w, m, v, mine, other, l, prev, name):
    _, R, C = w.shape
    half = R // 2
    tm = _pick(half, (256, 128, 64))
    nbh = half // tm

    def body(c_ref, w_ref, m_ref, v_ref, a_ref, b_ref, *rest):
        g_ref, d_ref, mo_ref, vo_ref = rest[-4:]
        gv = jnp.where(pl.program_id(0) // nbh == c_ref[0], a_ref[...], b_ref[...])
        mn = ADAM_B1 * m_ref[...] + (1.0 - ADAM_B1) * gv
        vn = ADAM_B2 * v_ref[...] + (1.0 - ADAM_B2) * (gv * gv)
        m_hat = mn / (1.0 - ADAM_B1 ** ADAM_STEP)
        v_hat = vn / (1.0 - ADAM_B2 ** ADAM_STEP)
        g_ref[...] = gv
        d_ref[...] = -ADAM_LR * (m_hat / (jnp.sqrt(v_hat) + ADAM_EPS) + ADAM_WD * w_ref[...])
        mo_ref[...] = mn
        vo_ref[...] = vn

    layer = pl.BlockSpec((None, tm, C), lambda i, c_ref: (l, i, 0))
    halfspec = pl.BlockSpec((tm, C), lambda i, c_ref: (i % nbh, 0))
    n_prev = 0 if prev is None else 4
    grid_spec = pltpu.PrefetchScalarGridSpec(
        num_scalar_prefetch=1, grid=(R // tm,),
        in_specs=[layer] * 3 + [halfspec] * 2 + [pl.BlockSpec(memory_space=pl.ANY)] * n_prev,
        out_specs=[layer] * 4)
    return pl.pallas_call(body, name=name, grid_spec=grid_spec, out_shape=[_sds(w.shape, F32)] * 4,
                          input_output_aliases={6 + k: k for k in range(n_prev)},
                          compiler_params=pltpu.CompilerParams(dimension_semantics=("parallel",),
                                                               vmem_limit_bytes=VMEM_LIMIT))(
        c, w, m, v, mine, other, *(prev or ()))


FIRST_GATHER = ("w_in", "w_uq", "w_ukv")
GATHER_IN = {
    "mm_h_l0": (0, ("w_out",)), "mla_fwd_l0": (0, ("w_gate", "w_up")), "ret_fwd_l0": (0, ("w_down",)),
    "mm_gu_l0": (1, ("w_in", "w_uq", "w_ukv", "w_out")), "mm_down_l0": (1, ("w_gate",)),
    "mm_h_l1": (1, ("w_up",)), "mla_fwd_l1": (1, ("w_down",))}
EXCHANGE_IN = {
    "mm_dact": ("w_down",), "mm_dx1": ("w_gate", "w_up"), "ret_bwd": ("w_out",), "mm_dxl": ("w_uq", "w_ukv", "w_in")}


class _Pipeline:
    def __init__(self, own, Wt, Mo, Vo, core, chip):
        self.own, self.Wt, self.Mo, self.Vo, self.core, self.chip = own, Wt, Mo, Vo, core, chip
        self.blocks, self.whole, self.parts = {}, {}, {}
        self.results = {n: None for n in BIG}

    def _gathered(self, l, names, blocks):
        for n, b in zip(names, blocks):
            self.blocks[(l, n)] = b

    def gather_first(self):
        side = _allgather_side([self.own[0][n] for n in FIRST_GATHER])
        self._gathered(0, FIRST_GATHER, _run_side(side, "allgather_first"))

    def weight(self, l, name):
        if (l, name) not in self.whole:
            self.whole[(l, name)] = _internal_weight(name, *[self.blocks[(l, n)] for n in INTERNAL_OF[name]])
        return self.whole[(l, name)]

    def run(self, fn, name, *args, **kw):
        base, l = name[:-3], int(name[-1])
        if name in GATHER_IN:
            gl, names = GATHER_IN[name]
            out, blocks = fn(*args, name=name, side=_allgather_side([self.own[gl][n] for n in names]), **kw)
            self._gathered(gl, names, blocks)
            return out
        if base in EXCHANGE_IN:
            names = EXCHANGE_IN[base]
            out, rcvs = fn(*args, name=name, side=_exchange_side([self.parts[(l, n)] for n in names]), **kw)
            self._reduced(l, names, rcvs)
            return out
        return fn(*args, name=name, **kw)

    def reduce(self, l, **grads):
        shards = {}
        for name, g in grads.items():
            shards.update(_grad_shards(name, g))
        names = list(shards)
        gds = [jnp.stack(shards[n]) for n in names]
        got = _swap_halves(gds, f"swap_halves_{names[0]}_l{l}")
        for n, gd, gt in zip(names, gds, got):
            self.parts[(l, n)] = _add_pair(gd, gt, self.core, f"add_pair_{n}_l{l}")

    def _reduced(self, l, names, rcvs):
        reds = [_add_chips(self.parts[(l, n)], rcv, self.chip, f"add_chips_{n}_l{l}") for n, rcv in zip(names, rcvs)]
        others = _share_halves(reds, f"share_halves_{names[0]}_l{l}")
        for n, red, other in zip(names, reds, others):
            self.results[n] = _adamw_layer(self.core, self.Wt[n], self.Mo[n], self.Vo[n], red, other, l,
                                           self.results[n], f"adamw_{n}_l{l}")


def kernel(x, positions, ln_in_g, ln_in_b, w_in, q_norm_g, kv_norm_g, w_uq, w_ukv, ret_gn_g, ret_gn_b, w_out, ln1_g, ln1_b, w_gate, w_up, w_down, ln2_g, ln2_b, loss_target, m_ln_in_g, m_ln_in_b, m_w_in, m_q_norm_g, m_kv_norm_g, m_w_uq, m_w_ukv, m_ret_gn_g, m_ret_gn_b, m_w_out, m_ln1_g, m_ln1_b, m_w_gate, m_w_up, m_w_down, m_ln2_g, m_ln2_b, v_ln_in_g, v_ln_in_b, v_w_in, v_q_norm_g, v_kv_norm_g, v_w_uq, v_w_ukv, v_ret_gn_g, v_ret_gn_b, v_w_out, v_ln1_g, v_ln1_b, v_w_gate, v_w_up, v_w_down, v_ln2_g, v_ln2_b):
    given = dict(locals())
    Wt = {n: given[n] for n in WEIGHTS}
    Mo = {n: given["m_" + n] for n in WEIGHTS}
    Vo = {n: given["v_" + n] for n in WEIGHTS}
    cx, cy, cc = _place()
    chip = (2 * cx + cy).astype(jnp.int32)
    core = cc.astype(jnp.int32)

    own = [{n: Wt[n][l].astype(BF16) for n in BIG} for l in range(DEPTH)]
    pipe = _Pipeline(own, Wt, Mo, Vo, core.reshape(1), chip.reshape(1))
    sqerr, grad_x, dP = _local_step(x[0], positions[0], loss_target[0], pipe, Wt)
    results = pipe.results

    small_g = {n: (dP[(n, None)] if Wt[n].ndim == 1 else jnp.stack([dP[(n, l)] for l in range(DEPTH)])) for n in SMALL}
    local_loss = 0.5 * jnp.sum(sqerr) / D_MODEL
    small_sum = _sum_small(_allreduce_small(_flatten_small(small_g, local_loss))).reshape(-1)
    layout, n_small = _small_layout(Wt)
    loss = small_sum[n_small]

    grads, deltas, new_m, new_v = {}, {}, {}, {}
    for n in BIG:
        grads[n], deltas[n], new_m[n], new_v[n] = results[n]
    zero = jnp.zeros((), F32)
    d, mn, vn = _adamw(_flatten_small(Wt, zero), small_sum.reshape(SMALL_ROWS, FLAT_W), _flatten_small(Mo, zero),
                       _flatten_small(Vo, zero), "adamw_small")
    for n in SMALL:
        at, size = layout[n]
        pick = lambda a: a.reshape(-1)[at:at + size].reshape(Wt[n].shape)
        grads[n], deltas[n], new_m[n], new_v[n] = pick(small_sum), pick(d), pick(mn), pick(vn)

    return (loss, grad_x[None], *[grads[n] for n in WEIGHTS], *[deltas[n] for n in WEIGHTS],
            *[new_m[n] for n in WEIGHTS], *[new_v[n] for n in WEIGHTS])
```

```python
import functools

import jax
import jax.numpy as jnp
from jax import lax
from jax.experimental import pallas as pl
from jax.experimental.pallas import tpu as pltpu

F32 = jnp.float32
BF16 = jnp.bfloat16

D_MODEL = 2048
DEPTH = 2
CHUNK = 64
MLA_HEADS = 8
Q_LORA = 512
KV_LORA = 256
NOPE = 128
ROPE = 64
VDIM = 128
RET_HEADS = 4
RET_DK = 256
RET_DV = 256
D_FF = 5632
D_IN = 4928
ROPE_THETA = 10000.0
LN_EPS = 1e-5
RMS_EPS = 1e-6
GN_EPS = 1e-5
ALPHA = (2 * DEPTH) ** 0.25
MLA_SCALE = (NOPE + ROPE) ** -0.5
RET_SCALE = RET_DK ** -0.5
ADAM_LR = 0.001
ADAM_B1 = 0.9
ADAM_B2 = 0.999
ADAM_EPS = 1e-08
ADAM_WD = 0.01
ADAM_STEP = 10

LANES = 128
HEAD_PAD = 256
MLA_IN = 1024
MLA_IN_USED = Q_LORA + KV_LORA + ROPE
D_IN_PAD = MLA_IN + 4 * 1024
ATT_BLOCK = 512
NEG = -1e30
VMEM_LIMIT = 56 * 1024 * 1024

N_CHIPS = 4
FLAT_W = 1024
BIG = ("w_in", "w_uq", "w_ukv", "w_out", "w_gate", "w_up", "w_down")
BIG_SHARD = {"w_in": (2048, 1232), "w_uq": (512, 384), "w_ukv": (256, 512), "w_out": (512, 2048),
             "w_gate": (2048, 1408), "w_up": (2048, 1408), "w_down": (1408, 2048)}
SMALL = ("ln_in_g", "ln_in_b", "q_norm_g", "kv_norm_g", "ret_gn_g", "ret_gn_b", "ln1_g", "ln1_b", "ln2_g", "ln2_b")
WEIGHTS = ("ln_in_g", "ln_in_b", "w_in", "q_norm_g", "kv_norm_g", "w_uq", "w_ukv", "ret_gn_g", "ret_gn_b", "w_out",
           "ln1_g", "ln1_b", "w_gate", "w_up", "w_down", "ln2_g", "ln2_b")
SMALL_ROWS = 32

MESH = pl.DeviceIdType.MESH


def _pick(dim, cands):
    for c in cands:
        if dim % c == 0:
            return c
    return dim


HBM = pl.BlockSpec(memory_space=pltpu.HBM)


class _Side:
    def __init__(self, arrays, out_shape, scratch, start, finish):
        self.arrays, self.out_shape, self.scratch, self.start, self.finish = arrays, out_shape, scratch, start, finish


def _call(body, name, out_shape, grid, in_specs, out_specs, scratch=(), sem=None, side=None):
    params = pltpu.CompilerParams(dimension_semantics=sem if side is None else ("arbitrary",) * len(grid),
                                  vmem_limit_bytes=VMEM_LIMIT)
    if side is None:
        return pl.pallas_call(body, name=name, out_shape=out_shape, grid=grid, in_specs=in_specs, out_specs=out_specs,
                              scratch_shapes=list(scratch), compiler_params=params)
    single = not isinstance(out_shape, (list, tuple))
    outs = [out_shape] if single else list(out_shape)
    ospecs = [out_specs] if single else list(out_specs)
    cuts = [len(in_specs), len(side.arrays), len(outs), len(side.out_shape), len(scratch)]
    ends = [sum(cuts[:k + 1]) for k in range(len(cuts))]

    def hosted(*refs):
        ins, s_in, o, s_out, scr = (refs[a:b] for a, b in zip([0] + ends[:-1], ends))
        sems = refs[ends[-1]:]
        ids = [pl.program_id(a) for a in range(len(grid))]
        first = functools.reduce(jnp.logical_and, [i == 0 for i in ids])
        last = functools.reduce(jnp.logical_and, [i == g - 1 for i, g in zip(ids, grid)])

        @pl.when(first)
        def _():
            side.start(s_in, s_out, sems)

        body(*ins, *o, *scr)

        @pl.when(last)
        def _():
            side.finish(s_in, s_out, sems)

    call = pl.pallas_call(hosted, name=name, out_shape=outs + list(side.out_shape), grid=grid,
                          in_specs=list(in_specs) + [HBM] * len(side.arrays),
                          out_specs=ospecs + [HBM] * len(side.out_shape),
                          scratch_shapes=list(scratch) + list(side.scratch), compiler_params=params)

    def run(*args):
        res = call(*args, *side.arrays)
        return (res[0] if single else list(res[:len(outs)])), list(res[len(outs):])

    return run


def _rows(tm, w, col=0):
    return pl.BlockSpec((tm, w), lambda i: (i, col))


def _whole(shape):
    return pl.BlockSpec(shape, lambda i: (0,) * len(shape))


def _sds(shape, dtype):
    return jax.ShapeDtypeStruct(shape, dtype)


def _matmul(a, b, name, ta=False, tb=False, out_dtype=F32, side=None):
    (K, M) = a.shape if ta else a.shape[::-1]
    (N, Kb) = b.shape if tb else b.shape[::-1]
    assert K == Kb, (a.shape, b.shape, ta, tb)
    tm = _pick(M, (1024, 512, 256, 128))
    tn = _pick(N, (1024, 512, 256, 128))
    tk = _pick(K, (2816, 2560, 2048, 1024, 512, 256))
    nk = K // tk
    dn = (((0 if ta else 1,), (1 if tb else 0,)), ((), ()))

    def body(a_ref, b_ref, o_ref, acc_ref):
        k = pl.program_id(2)
        if nk == 1:
            o_ref[...] = lax.dot_general(a_ref[...].astype(BF16), b_ref[...].astype(BF16), dn,
                                         preferred_element_type=F32).astype(out_dtype)
        else:
            @pl.when(k == 0)
            def _():
                acc_ref[...] = jnp.zeros_like(acc_ref)

            acc_ref[...] += lax.dot_general(a_ref[...].astype(BF16), b_ref[...].astype(BF16), dn,
                                            preferred_element_type=F32)

            @pl.when(k == nk - 1)
            def _():
                o_ref[...] = acc_ref[...].astype(out_dtype)

    a_spec = pl.BlockSpec((tk, tm), lambda i, j, k: (k, i)) if ta else pl.BlockSpec((tm, tk), lambda i, j, k: (i, k))
    b_spec = pl.BlockSpec((tn, tk), lambda i, j, k: (j, k)) if tb else pl.BlockSpec((tk, tn), lambda i, j, k: (k, j))
    return _call(body, name, _sds((M, N), out_dtype), (M // tm, N // tn, nk), [a_spec, b_spec],
                 pl.BlockSpec((tm, tn), lambda i, j, k: (i, j)), scratch=[pltpu.VMEM((tm, tn), F32)],
                 sem=("parallel", "parallel", "arbitrary"), side=side)(a, b)


def _sigmoid(x):
    return 1.0 / (1.0 + jnp.exp(-x))


def _rope_group(r, c, sa, sb):
    return r * c + pltpu.roll(r, 32, 1) * sa + pltpu.roll(r, 96, 1) * sb


def _ln_fwd(xs, coefs, g, b, name, want_z):
    S, D = xs[0].shape
    tm = 256
    n = len(xs)

    def body(*refs):
        x_refs, g_ref, b_ref, outs = refs[:n], refs[n], refs[n + 1], refs[n + 2:]
        z = None
        for cf, r in zip(coefs, x_refs):
            t = r[...] if cf == 1.0 else cf * r[...]
            z = t if z is None else z + t
        mu = jnp.mean(z, axis=-1, keepdims=True)
        zc = z - mu
        var = jnp.mean(zc * zc, axis=-1, keepdims=True)
        y = zc * lax.rsqrt(var + LN_EPS) * g_ref[...] + b_ref[...]
        if want_z:
            outs[0][...] = z
        outs[-2][...] = y
        outs[-1][...] = y.astype(BF16)

    out_shape = [_sds((S, D), F32)] * (2 if want_z else 1) + [_sds((S, D), BF16)]
    return _call(body, name, out_shape, (S // tm,), [_rows(tm, D)] * n + [_whole((1, D))] * 2,
                 [_rows(tm, D)] * len(out_shape), sem=("parallel",))(*xs, g, b)


def _ln_bwd(dys, coefs, z, g, name):
    S, D = z.shape
    tm = 256
    n = len(dys)

    def body(*refs):
        dy_refs, z_ref, g_ref = refs[:n], refs[n], refs[n + 1]
        dz_ref, dzb_ref, dg_ref, db_ref = refs[n + 2:]
        dy = None
        for cf, r in zip(coefs, dy_refs):
            t = r[...] if cf == 1.0 else cf * r[...]
            dy = t if dy is None else dy + t
        zv = z_ref[...]
        mu = jnp.mean(zv, axis=-1, keepdims=True)
        zc = zv - mu
        var = jnp.mean(zc * zc, axis=-1, keepdims=True)
        rstd = lax.rsqrt(var + LN_EPS)
        xh = zc * rstd
        dyg = dy * g_ref[...]
        dz = rstd * (dyg - jnp.mean(dyg, axis=-1, keepdims=True) - xh * jnp.mean(dyg * xh, axis=-1, keepdims=True))
        dz_ref[...] = dz
        dzb_ref[...] = dz.astype(BF16)

        @pl.when(pl.program_id(0) == 0)
        def _():
            dg_ref[...] = jnp.zeros_like(dg_ref)
            db_ref[...] = jnp.zeros_like(db_ref)

        dg_ref[...] += jnp.sum(dy * xh, axis=0, keepdims=True)
        db_ref[...] += jnp.sum(dy, axis=0, keepdims=True)

    return _call(body, name, [_sds((S, D), F32), _sds((S, D), BF16), _sds((1, D), F32), _sds((1, D), F32)],
                 (S // tm,), [_rows(tm, D)] * (n + 1) + [_whole((1, D))],
                 [_rows(tm, D), _rows(tm, D), _whole((1, D)), _whole((1, D))], sem=("arbitrary",))(*dys, z, g)


def _rms(x, g):
    return x * lax.rsqrt(jnp.mean(x * x, axis=-1, keepdims=True) + RMS_EPS) * g


def _prep1(h, tabs, qg, kvg, name):
    S = h.shape[0]
    tm = 256
    cm, sam, sbm, cr, sr = tabs

    def body(h_ref, cm_ref, sam_ref, sbm_ref, cr_ref, sr_ref, qg_ref, kvg_ref,
             qn_ref, kvn_ref, kr_ref, rq_ref, rk_ref, rv_ref):
        qn_ref[...] = _rms(h_ref[:, 0:Q_LORA], qg_ref[...]).astype(BF16)
        kvn_ref[...] = _rms(h_ref[:, Q_LORA:Q_LORA + KV_LORA], kvg_ref[...]).astype(BF16)
        kr_ref[...] = _rope_group(h_ref[:, 768:896], cm_ref[...], sam_ref[...], sbm_ref[...])
        c, s = cr_ref[...], sr_ref[...]
        for hd in range(RET_HEADS):
            for src, dst, scale in ((MLA_IN, rq_ref, RET_SCALE), (MLA_IN + 1024, rk_ref, None)):
                t1 = h_ref[:, src + hd * 256:src + hd * 256 + 128]
                t2 = h_ref[:, src + hd * 256 + 128:src + hd * 256 + 256]
                o1, o2 = t1 * c - t2 * s, t2 * c + t1 * s
                if scale is not None:
                    o1, o2 = o1 * scale, o2 * scale
                dst[:, hd * 256:hd * 256 + 128] = o1.astype(BF16)
                dst[:, hd * 256 + 128:hd * 256 + 256] = o2.astype(BF16)
        rv_ref[...] = h_ref[:, MLA_IN + 2048:MLA_IN + 3072].astype(BF16)

    t128 = _rows(tm, LANES)
    return _call(body, name,
                 [_sds((S, Q_LORA), BF16), _sds((S, KV_LORA), BF16), _sds((S, LANES), F32),
                  _sds((S, 1024), BF16), _sds((S, 1024), BF16), _sds((S, 1024), BF16)],
                 (S // tm,),
                 [_rows(tm, D_IN_PAD), t128, t128, t128, t128, t128, _whole((1, Q_LORA)), _whole((1, KV_LORA))],
                 [_rows(tm, Q_LORA), _rows(tm, KV_LORA), t128, _rows(tm, 1024), _rows(tm, 1024), _rows(tm, 1024)],
                 sem=("parallel",))(h, cm, sam, sbm, cr, sr, qg, kvg)


def _prep1_bwd(dqn, dkvn, dkr, drq, drk, drv, drg, h, tabs, qg, kvg, name):
    S = h.shape[0]
    tm = 256
    cm, sam, sbm, cr, sr = tabs

    def rms_bwd(x, g, dy):
        r = lax.rsqrt(jnp.mean(x * x, axis=-1, keepdims=True) + RMS_EPS)
        dyg = dy * g
        dx = r * dyg - x * (r * r * r) * jnp.mean(dyg * x, axis=-1, keepdims=True)
        return dx, jnp.sum(dy * x * r, axis=0, keepdims=True)

    def body(dqn_ref, dkvn_ref, dkr_ref, drq_ref, drk_ref, drv_ref, drg_ref, h_ref,
             cm_ref, sam_ref, sbm_ref, cr_ref, sr_ref, qg_ref, kvg_ref, dh_ref, dqg_ref, dkvg_ref):
        dcq, dqg = rms_bwd(h_ref[:, 0:Q_LORA], qg_ref[...], dqn_ref[...])
        dckv, dkvg = rms_bwd(h_ref[:, Q_LORA:Q_LORA + KV_LORA], kvg_ref[...], dkvn_ref[...])
        dh_ref[:, 0:Q_LORA] = dcq.astype(BF16)
        dh_ref[:, Q_LORA:Q_LORA + KV_LORA] = dckv.astype(BF16)
        dh_ref[:, 768:896] = _rope_group(dkr_ref[...], cm_ref[...], -sam_ref[...], -sbm_ref[...]).astype(BF16)
        dh_ref[:, 896:1024] = jnp.zeros((tm, LANES), BF16)
        c, s = cr_ref[...], sr_ref[...]
        for hd in range(RET_HEADS):
            for src, dst, scale in ((drq_ref, MLA_IN, RET_SCALE), (drk_ref, MLA_IN + 1024, None)):
                d1 = src[:, hd * 256:hd * 256 + 128]
                d2 = src[:, hd * 256 + 128:hd * 256 + 256]
                if scale is not None:
                    d1, d2 = d1 * scale, d2 * scale
                dh_ref[:, dst + hd * 256:dst + hd * 256 + 128] = (d1 * c + d2 * s).astype(BF16)
                dh_ref[:, dst + hd * 256 + 128:dst + hd * 256 + 256] = (d2 * c - d1 * s).astype(BF16)
        dh_ref[:, MLA_IN + 2048:MLA_IN + 3072] = drv_ref[...].astype(BF16)
        dh_ref[:, MLA_IN + 3072:MLA_IN + 4096] = drg_ref[...].astype(BF16)

        @pl.when(pl.program_id(0) == 0)
        def _():
            dqg_ref[...] = jnp.zeros_like(dqg_ref)
            dkvg_ref[...] = jnp.zeros_like(dkvg_ref)

        dqg_ref[...] += dqg
        dkvg_ref[...] += dkvg

    t128 = _rows(tm, LANES)
    return _call(body, name,
                 [_sds((S, D_IN_PAD), BF16), _sds((1, Q_LORA), F32), _sds((1, KV_LORA), F32)],
                 (S // tm,),
                 [_rows(tm, Q_LORA), _rows(tm, KV_LORA), t128, _rows(tm, 1024), _rows(tm, 1024), _rows(tm, 1024),
                  _rows(tm, 1024), _rows(tm, D_IN_PAD), t128, t128, t128, t128, t128,
                  _whole((1, Q_LORA)), _whole((1, KV_LORA))],
                 [_rows(tm, D_IN_PAD), _whole((1, Q_LORA)), _whole((1, KV_LORA))],
                 sem=("arbitrary",))(dqn, dkvn, dkr, drq, drk, drv, drg, h, cm, sam, sbm, cr, sr, qg, kvg)


def _prep2(q, kv, kr, tabs, name):
    S = q.shape[0]
    tm = 256
    cm, sam, sbm = tabs[:3]

    def body(q_ref, kv_ref, kr_ref, cm_ref, sam_ref, sbm_ref, qo_ref, ko_ref, vo_ref):
        c, sa, sb = cm_ref[...], sam_ref[...], sbm_ref[...]
        krb = kr_ref[...].astype(BF16)
        for hd in range(MLA_HEADS):
            o = hd * HEAD_PAD
            qo_ref[:, o:o + 128] = q_ref[:, o:o + 128].astype(BF16)
            qo_ref[:, o + 128:o + 256] = _rope_group(q_ref[:, o + 128:o + 256], c, sa, sb).astype(BF16)
            ko_ref[:, o:o + 128] = kv_ref[:, hd * 128:hd * 128 + 128].astype(BF16)
            ko_ref[:, o + 128:o + 256] = krb
        vo_ref[...] = kv_ref[:, 1024:2048].astype(BF16)

    t128 = _rows(tm, LANES)
    return _call(body, name, [_sds((S, 2048), BF16), _sds((S, 2048), BF16), _sds((S, 1024), BF16)], (S // tm,),
                 [_rows(tm, 2048), _rows(tm, 2048), t128, t128, t128, t128],
                 [_rows(tm, 2048), _rows(tm, 2048), _rows(tm, 1024)], sem=("parallel",))(q, kv, kr, cm, sam, sbm)


def _prep2_bwd(dqm, dkm, dvm, tabs, name):
    S = dqm.shape[0]
    tm = 256
    cm, sam, sbm = tabs[:3]

    def body(dq_ref, dk_ref, dv_ref, cm_ref, sam_ref, sbm_ref, dqo_ref, dkvo_ref, dkr_ref):
        c, sa, sb = cm_ref[...], -sam_ref[...], -sbm_ref[...]
        dkr = None
        for hd in range(MLA_HEADS):
            o = hd * HEAD_PAD
            dqo_ref[:, o:o + 128] = dq_ref[:, o:o + 128].astype(BF16)
            dqo_ref[:, o + 128:o + 256] = _rope_group(dq_ref[:, o + 128:o + 256], c, sa, sb).astype(BF16)
            dkvo_ref[:, hd * 128:hd * 128 + 128] = dk_ref[:, o:o + 128].astype(BF16)
            t = dk_ref[:, o + 128:o + 256]
            dkr = t if dkr is None else dkr + t
        dkvo_ref[:, 1024:2048] = dv_ref[...].astype(BF16)
        dkr_ref[...] = dkr

    t128 = _rows(tm, LANES)
    return _call(body, name, [_sds((S, 2048), BF16), _sds((S, 2048), BF16), _sds((S, LANES), F32)], (S // tm,),
                 [_rows(tm, 2048), _rows(tm, 2048), _rows(tm, 1024), t128, t128, t128],
                 [_rows(tm, 2048), _rows(tm, 2048), t128], sem=("parallel",))(dqm, dkm, dvm, cm, sam, sbm)


def _gn_gate(a, o, h, gg, gb, name):
    S = a.shape[0]
    tm = 256

    def body(a_ref, o_ref, rg_ref, gg_ref, gb_ref, mix_ref):
        mix_ref[:, 0:1024] = a_ref[...].astype(BF16)
        for hd in range(RET_HEADS):
            sl = slice(hd * 256, hd * 256 + 256)
            ov = o_ref[:, sl]
            mu = jnp.mean(ov, axis=-1, keepdims=True)
            oc = ov - mu
            var = jnp.mean(oc * oc, axis=-1, keepdims=True)
            y = oc * lax.rsqrt(var + GN_EPS) * gg_ref[:, sl] + gb_ref[:, sl]
            rg = rg_ref[:, sl]
            mix_ref[:, 1024 + hd * 256:1024 + hd * 256 + 256] = (rg * _sigmoid(rg) * y).astype(BF16)

    return _call(body, name, _sds((S, 2048), BF16), (S // tm,),
                 [_rows(tm, 1024), _rows(tm, 1024), _rows(tm, 1024, 4), _whole((1, 1024)), _whole((1, 1024))],
                 _rows(tm, 2048), sem=("parallel",))(a, o, h, gg, gb)


def _gn_gate_bwd(dmixin, o, h, gg, gb, name):
    S = o.shape[0]
    tm = 256

    def body(dr_ref, o_ref, rg_ref, gg_ref, gb_ref, do_ref, drg_ref, dgg_ref, dgb_ref):
        @pl.when(pl.program_id(0) == 0)
        def _():
            dgg_ref[...] = jnp.zeros_like(dgg_ref)
            dgb_ref[...] = jnp.zeros_like(dgb_ref)

        for hd in range(RET_HEADS):
            sl = slice(hd * 256, hd * 256 + 256)
            ov = o_ref[:, sl]
            mu = jnp.mean(ov, axis=-1, keepdims=True)
            oc = ov - mu
            var = jnp.mean(oc * oc, axis=-1, keepdims=True)
            rstd = lax.rsqrt(var + GN_EPS)
            xh = oc * rstd
            g = gg_ref[:, sl]
            y = xh * g + gb_ref[:, sl]
            rg = rg_ref[:, sl]
            sg = _sigmoid(rg)
            dr = dr_ref[:, sl]
            dy = dr * (rg * sg)
            drg_ref[:, sl] = dr * y * (sg * (1.0 + rg * (1.0 - sg)))
            dgg_ref[:, sl] += jnp.sum(dy * xh, axis=0, keepdims=True)
            dgb_ref[:, sl] += jnp.sum(dy, axis=0, keepdims=True)
            dxh = dy * g
            do = rstd * (dxh - jnp.mean(dxh, axis=-1, keepdims=True) - xh * jnp.mean(dxh * xh, axis=-1, keepdims=True))
            do_ref[:, sl] = do.astype(BF16)

    return _call(body, name,
                 [_sds((S, 1024), BF16), _sds((S, 1024), F32), _sds((1, 1024), F32), _sds((1, 1024), F32)],
                 (S // tm,),
                 [_rows(tm, 1024, 1), _rows(tm, 1024), _rows(tm, 1024, 4), _whole((1, 1024)), _whole((1, 1024))],
                 [_rows(tm, 1024), _rows(tm, 1024), _whole((1, 1024)), _whole((1, 1024))],
                 sem=("arbitrary",))(dmixin, o, h, gg, gb)


def _swiglu(gu, name):
    S = gu.shape[0]
    tm = 256

    def body(g_ref, u_ref, o_ref):
        g = g_ref[...]
        o_ref[...] = (g * _sigmoid(g) * u_ref[...]).astype(BF16)

    return _call(body, name, _sds((S, D_FF), BF16), (S // tm,), [_rows(tm, D_FF, 0), _rows(tm, D_FF, 1)],
                 _rows(tm, D_FF), sem=("parallel",))(gu, gu)


def _swiglu_bwd(gu, dact, name):
    S = gu.shape[0]
    tm = 128

    def body(g_ref, u_ref, d_ref, o_ref):
        g, u, d = g_ref[...], u_ref[...], d_ref[...]
        sg = _sigmoid(g)
        o_ref[:, 0:D_FF] = (d * u * (sg * (1.0 + g * (1.0 - sg)))).astype(BF16)
        o_ref[:, D_FF:2 * D_FF] = (d * (g * sg)).astype(BF16)

    return _call(body, name, _sds((S, 2 * D_FF), BF16), (S // tm,),
                 [_rows(tm, D_FF, 0), _rows(tm, D_FF, 1), _rows(tm, D_FF)], _rows(tm, 2 * D_FF),
                 sem=("parallel",))(gu, gu, dact)


def _loss_head(y, target, name):
    S, D = y.shape
    tm = 256

    def body(y_ref, t_ref, dy_ref, acc_ref):
        e = y_ref[...] - t_ref[...]
        dy_ref[...] = e / D

        @pl.when(pl.program_id(0) == 0)
        def _():
            acc_ref[...] = jnp.zeros_like(acc_ref)

        acc_ref[...] += jnp.sum(e * e, axis=0, keepdims=True)

    return _call(body, name, [_sds((S, D), F32), _sds((1, D), F32)], (S // tm,), [_rows(tm, D), _rows(tm, D)],
                 [_rows(tm, D), _whole((1, D))], sem=("arbitrary",))(y, target)


def _chunk_mask(T):
    r = lax.shift_right_logical(lax.broadcasted_iota(jnp.int32, (T, T), 0), 6)
    c = lax.shift_right_logical(lax.broadcasted_iota(jnp.int32, (T, T), 1), 6)
    return r >= c


def _dot_nt(a, b):
    return lax.dot_general(a, b, (((1,), (1,)), ((), ())), preferred_element_type=F32)


def _dot_tn(a, b):
    return lax.dot_general(a, b, (((0,), (0,)), ((), ())), preferred_element_type=F32)


def _decay_tables(T):
    lg = jnp.log1p(-jnp.exp2(-5.0 - jnp.arange(RET_HEADS, dtype=F32)))
    idx = jnp.arange(T, dtype=F32)
    diff = idx[:, None] - idx[None, :]
    rel = jnp.exp(lg[:, None, None] * diff[None])
    cid = jnp.arange(T) // CHUNK
    mask = (cid[:, None] >= cid[None, :]).astype(F32)
    reld = jnp.exp(lg[:, None, None] * jnp.abs(diff)[None]) * mask[None]
    lgrow = jnp.broadcast_to(lg[:, None, None], (RET_HEADS, 1, LANES))
    return lgrow, rel, reld


def _attn_fwd(q, k, v, heads, dk, dv, softmax, name, tables=None, side=None):
    S = q.shape[0]
    T = ATT_BLOCK
    nq = S // T
    rep = T // LANES

    def body(*refs):
        if softmax:
            q_ref, k_ref, v_ref, o_ref, lse_ref, m_sc, l_sc, acc_sc = refs
        else:
            q_ref, k_ref, v_ref, lg_ref, rel_ref, reld_ref, o_ref, acc_sc = refs
        i = pl.program_id(1)
        qv = q_ref[...]

        def kv_block(j):
            rows = pl.ds(pl.multiple_of(j * T, T), T)
            return k_ref[rows, :], v_ref[rows, :]

        kb, vb = kv_block(i)
        s = _dot_nt(qv, kb)
        if softmax:
            s = jnp.where(_chunk_mask(T), s * MLA_SCALE, NEG)
            m = jnp.max(s, axis=-1, keepdims=True)
            p = jnp.exp(s - m)
            m_sc[...] = jnp.broadcast_to(m, (T, LANES))
            l_sc[...] = jnp.broadcast_to(jnp.sum(p, axis=-1, keepdims=True), (T, LANES))
        else:
            p = s * reld_ref[0]
        acc_sc[...] = jnp.dot(p.astype(BF16), vb, preferred_element_type=F32)

        def step(j, carry):
            kb, vb = kv_block(j)
            s = _dot_nt(qv, kb)
            if softmax:
                s = s * MLA_SCALE
                m_prev = m_sc[...]
                m_next = jnp.maximum(m_prev, jnp.max(s, axis=-1, keepdims=True))
                alpha = jnp.exp(m_prev - m_next)
                p = jnp.exp(s - jnp.tile(m_next, (1, rep)))
                l_sc[...] = alpha * l_sc[...] + jnp.sum(p, axis=-1, keepdims=True)
                m_sc[...] = m_next
                acc_sc[...] = acc_sc[...] * jnp.tile(alpha, (1, dv // LANES)) + jnp.dot(
                    p.astype(BF16), vb, preferred_element_type=F32)
            else:
                fac = jnp.exp(lg_ref[0] * ((i - j) * T).astype(F32))
                p = s * (rel_ref[0] * jnp.tile(fac, (1, rep)))
                acc_sc[...] += jnp.dot(p.astype(BF16), vb, preferred_element_type=F32)
            return carry

        lax.fori_loop(0, i, step, 0)
        if softmax:
            l = l_sc[...]
            o_ref[...] = acc_sc[...] / jnp.tile(l, (1, dv // LANES))
            lse_ref[...] = m_sc[...] + jnp.log(l)
        else:
            o_ref[...] = acc_sc[...]

    in_specs = [pl.BlockSpec((T, dk), lambda h, i: (i, h)), pl.BlockSpec((S, dk), lambda h, i: (0, h)),
                pl.BlockSpec((S, dv), lambda h, i: (0, h))]
    o_spec = pl.BlockSpec((T, dv), lambda h, i: (i, h))
    if softmax:
        return _call(body, name, [_sds((S, heads * dv), F32), _sds((S, heads * LANES), F32)], (heads, nq), in_specs,
                     [o_spec, pl.BlockSpec((T, LANES), lambda h, i: (i, h))],
                     scratch=[pltpu.VMEM((T, LANES), F32), pltpu.VMEM((T, LANES), F32), pltpu.VMEM((T, dv), F32)],
                     sem=("parallel", "arbitrary"), side=side)(q, k, v)
    lgrow, rel, reld = tables
    in_specs += [pl.BlockSpec((1, 1, LANES), lambda h, i: (h, 0, 0)), pl.BlockSpec((1, T, T), lambda h, i: (h, 0, 0)),
                 pl.BlockSpec((1, T, T), lambda h, i: (h, 0, 0))]
    return _call(body, name, _sds((S, heads * dv), F32), (heads, nq), in_specs, o_spec,
                 scratch=[pltpu.VMEM((T, dv), F32)], sem=("parallel", "arbitrary"), side=side)(q, k, v, lgrow, rel, reld)


def _attn_bwd(q, k, v, do, heads, dk, dv, softmax, name, o=None, lse=None, tables=None, side=None):
    S = q.shape[0]
    T = ATT_BLOCK
    nq = S // T
    rep = T // LANES

    def body(*refs):
        if softmax:
            q_ref, k_ref, v_ref, do_ref, o_ref, lse_ref, dq_ref, dk_ref, dv_ref, dq_sc = refs
        else:
            q_ref, k_ref, v_ref, do_ref, lg_ref, rel_ref, reld_ref, dq_ref, dk_ref, dv_ref, dq_sc = refs
        i = pl.program_id(1)

        @pl.when(i == 0)
        def _():
            dk_ref[...] = jnp.zeros_like(dk_ref)
            dv_ref[...] = jnp.zeros_like(dv_ref)

        qv = q_ref[...]
        dof = do_ref[...].astype(F32)
        dov = dof.astype(BF16)
        if softmax:
            delta = jnp.sum(dof * o_ref[...], axis=-1, keepdims=True)
            lse_t = jnp.tile(lse_ref[...], (1, rep))
        dq_sc[...] = jnp.zeros_like(dq_sc)

        def block(j, diagonal):
            rows = pl.ds(pl.multiple_of(j * T, T), T)
            kb, vb = k_ref[rows, :], v_ref[rows, :]
            s = _dot_nt(qv, kb)
            dp = _dot_nt(dov, vb)
            if softmax:
                s = s * MLA_SCALE
                if diagonal:
                    s = jnp.where(_chunk_mask(T), s, NEG)
                p = jnp.exp(s - lse_t)
                ds = p * (dp - delta) * MLA_SCALE
            else:
                if diagonal:
                    dec = reld_ref[0]
                else:
                    fac = jnp.exp(lg_ref[0] * ((i - j) * T).astype(F32))
                    dec = rel_ref[0] * jnp.tile(fac, (1, rep))
                p = s * dec
                ds = dp * dec
            dsb = ds.astype(BF16)
            dv_ref[rows, :] += _dot_tn(p.astype(BF16), dov)
            dk_ref[rows, :] += _dot_tn(dsb, qv)
            dq_sc[...] += jnp.dot(dsb, kb, preferred_element_type=F32)

        block(i, True)

        def step(j, carry):
            block(j, False)
            return carry

        lax.fori_loop(0, i, step, 0)
        dq_ref[...] = dq_sc[...]

    qspec = pl.BlockSpec((T, dk), lambda h, i: (i, h))
    kspec = pl.BlockSpec((S, dk), lambda h, i: (0, h))
    vspec = pl.BlockSpec((S, dv), lambda h, i: (0, h))
    dospec = pl.BlockSpec((T, dv), lambda h, i: (i, h))
    in_specs = [qspec, kspec, vspec, dospec]
    args = [q, k, v, do]
    if softmax:
        in_specs += [dospec, pl.BlockSpec((T, LANES), lambda h, i: (i, h))]
        args += [o, lse]
    else:
        in_specs += [pl.BlockSpec((1, 1, LANES), lambda h, i: (h, 0, 0)),
                     pl.BlockSpec((1, T, T), lambda h, i: (h, 0, 0)), pl.BlockSpec((1, T, T), lambda h, i: (h, 0, 0))]
        args += list(tables)
    return _call(body, name, [_sds((S, heads * dk), F32), _sds((S, heads * dk), F32), _sds((S, heads * dv), F32)],
                 (heads, nq), in_specs, [qspec, kspec, vspec], scratch=[pltpu.VMEM((T, dk), F32)],
                 sem=("parallel", "arbitrary"), side=side)(*args)


def _rope_tables(pos):
    def tables(dim):
        inv_freq = ROPE_THETA ** (-jnp.arange(0, dim, 2, dtype=F32) / dim)
        ang = pos.astype(F32)[:, None] * inv_freq
        return jnp.cos(ang), jnp.sin(ang)

    cm, sm = tables(ROPE)
    S = pos.shape[0]
    z32, z64 = jnp.zeros((S, 32), F32), jnp.zeros((S, 64), F32)
    cr, sr = tables(RET_DK)
    return (jnp.concatenate([cm, cm, z64], 1), jnp.concatenate([z32, sm, z64], 1),
            jnp.concatenate([-sm, z32, z64], 1), cr, sr)


def _row(v):
    return v.reshape(1, -1).astype(F32)


def _local_step(x, pos, target, pipe, P):
    tabs = _rope_tables(pos)
    dtabs = _decay_tables(ATT_BLOCK)
    xf, xb = _ln_fwd([x], [1.0], _row(P["ln_in_g"]), _row(P["ln_in_b"]), "ln_in", False)
    pipe.gather_first()
    saved = []
    for l in range(DEPTH):
        w = functools.partial(pipe.weight, l)
        t = f"_l{l}"
        h = pipe.run(_matmul, "mm_h" + t, xb, w("w_in"))
        qn, kvn, kr, rq, rk, rv = _prep1(h, tabs, _row(P["q_norm_g"][l]), _row(P["kv_norm_g"][l]), "prep1" + t)
        q = _matmul(qn, w("w_uq"), "mm_q" + t)
        kv = _matmul(kvn, w("w_ukv"), "mm_kv" + t)
        qm, km, vm = _prep2(q, kv, kr, tabs, "prep2" + t)
        a, lse = pipe.run(_attn_fwd, "mla_fwd" + t, qm, km, vm, MLA_HEADS, HEAD_PAD, VDIM, True)
        o = pipe.run(_attn_fwd, "ret_fwd" + t, rq, rk, rv, RET_HEADS, RET_DK, RET_DV, False, tables=dtabs)
        mixin = _gn_gate(a, o, h, _row(P["ret_gn_g"][l]), _row(P["ret_gn_b"][l]), "gn_gate" + t)
        mix = _matmul(mixin, w("w_out"), "mm_mix" + t)
        z1, x1f, x1b = _ln_fwd([xf, mix], [ALPHA, 1.0], _row(P["ln1_g"][l]), _row(P["ln1_b"][l]), "ln1" + t, True)
        gu = pipe.run(_matmul, "mm_gu" + t, x1b, w("w_gu"))
        act = _swiglu(gu, "swiglu" + t)
        f = pipe.run(_matmul, "mm_down" + t, act, w("w_down"))
        z2, x2f, x2b = _ln_fwd([x1f, f], [ALPHA, 1.0], _row(P["ln2_g"][l]), _row(P["ln2_b"][l]), "ln2" + t, True)
        saved.append(dict(xb=xb, h=h, qn=qn, kvn=kvn, rq=rq, rk=rk, rv=rv, qm=qm, km=km, vm=vm, a=a, lse=lse, o=o,
                          mixin=mixin, z1=z1, x1b=x1b, gu=gu, act=act, z2=z2))
        xf, xb = x2f, x2b

    dy, sqerr = _loss_head(xf, target, "loss_head")
    dP = {}
    dys, coefs = [dy], [1.0]
    for l in reversed(range(DEPTH)):
        w, sv = functools.partial(pipe.weight, l), saved[l]
        t = f"_l{l}"
        dz2, dz2b, dg, db = _ln_bwd(dys, coefs, sv["z2"], _row(P["ln2_g"][l]), "ln2_bwd" + t)
        dP[("ln2_g", l)], dP[("ln2_b", l)] = dg, db
        pipe.reduce(l, w_down=_matmul(sv["act"], dz2b, "mm_dw_down" + t, ta=True, out_dtype=BF16))
        dact = pipe.run(_matmul, "mm_dact" + t, dz2b, w("w_down"), tb=True)
        dgu = _swiglu_bwd(sv["gu"], dact, "swiglu_bwd" + t)
        pipe.reduce(l, w_gu=_matmul(sv["x1b"], dgu, "mm_dw_gu" + t, ta=True, out_dtype=BF16))
        dx1 = pipe.run(_matmul, "mm_dx1" + t, dgu, w("w_gu"), tb=True)
        dz1, dz1b, dg, db = _ln_bwd([dz2, dx1], [ALPHA, 1.0], sv["z1"], _row(P["ln1_g"][l]), "ln1_bwd" + t)
        dP[("ln1_g", l)], dP[("ln1_b", l)] = dg, db
        pipe.reduce(l, w_out=_matmul(sv["mixin"], dz1b, "mm_dw_out" + t, ta=True, out_dtype=BF16))
        dmixin = _matmul(dz1b, w("w_out"), "mm_dmixin" + t, tb=True)
        do, drg, dgg, dgb = _gn_gate_bwd(dmixin, sv["o"], sv["h"], _row(P["ret_gn_g"][l]), _row(P["ret_gn_b"][l]),
                                         "gn_gate_bwd" + t)
        dP[("ret_gn_g", l)], dP[("ret_gn_b", l)] = dgg, dgb
        drq, drk, drv = pipe.run(_attn_bwd, "ret_bwd" + t, sv["rq"], sv["rk"], sv["rv"], do, RET_HEADS, RET_DK, RET_DV,
                                 False, tables=dtabs)
        dqm, dkm, dvm = _attn_bwd(sv["qm"], sv["km"], sv["vm"], dmixin, MLA_HEADS, HEAD_PAD, VDIM, True,
                                  "mla_bwd" + t, o=sv["a"], lse=sv["lse"])
        dq, dkv, dkr = _prep2_bwd(dqm, dkm, dvm, tabs, "prep2_bwd" + t)
        g_uq = _matmul(sv["qn"], dq, "mm_dw_uq" + t, ta=True, out_dtype=BF16)
        dqn = _matmul(dq, w("w_uq"), "mm_dqn" + t, tb=True)
        g_ukv = _matmul(sv["kvn"], dkv, "mm_dw_ukv" + t, ta=True, out_dtype=BF16)
        dkvn = _matmul(dkv, w("w_ukv"), "mm_dkvn" + t, tb=True)
        dh, dqg, dkvg = _prep1_bwd(dqn, dkvn, dkr, drq, drk, drv, drg, sv["h"], tabs, _row(P["q_norm_g"][l]),
                                   _row(P["kv_norm_g"][l]), "prep1_bwd" + t)
        dP[("q_norm_g", l)], dP[("kv_norm_g", l)] = dqg, dkvg
        pipe.reduce(l, w_uq=g_uq, w_ukv=g_ukv, w_in=_matmul(sv["xb"], dh, "mm_dw_in" + t, ta=True, out_dtype=BF16))
        dxl = pipe.run(_matmul, "mm_dxl" + t, dh, w("w_in"), tb=True)
        dys, coefs = [dz1, dxl], [ALPHA, 1.0]
    grad_x, _, dg, db = _ln_bwd(dys, coefs, x, _row(P["ln_in_g"]), "ln_in_bwd")
    dP[("ln_in_g", None)], dP[("ln_in_b", None)] = dg, db
    return sqerr, grad_x, dP


INTERNAL_OF = {"w_in": ("w_in",), "w_uq": ("w_uq",), "w_ukv": ("w_ukv",), "w_out": ("w_out",),
               "w_gu": ("w_gate", "w_up"), "w_down": ("w_down",)}


def _internal_weight(name, *blocks):
    cat = lambda parts: jnp.concatenate(parts, axis=1)
    cols = lambda b: cat([b[j] for j in range(N_CHIPS)])
    b = blocks[0]
    if name in ("w_out", "w_down"):
        return b.reshape(-1, b.shape[-1])
    if name == "w_gu":
        return cat([blk[j] for blk in blocks for j in range(N_CHIPS)])
    if name == "w_in":
        return cat([b[0][:, :MLA_IN_USED], jnp.zeros((D_MODEL, MLA_IN - MLA_IN_USED), BF16), b[0][:, MLA_IN_USED:]]
                   + [b[j] for j in range(1, N_CHIPS)])
    if name == "w_uq":
        uq, hw = cols(b), NOPE + ROPE
        pad = jnp.zeros((Q_LORA, HEAD_PAD - hw), BF16)
        return cat([p for h in range(MLA_HEADS) for p in (uq[:, h * hw:(h + 1) * hw], pad)])
    ukv = cols(b)
    return cat([ukv[:, 256 * h:256 * h + NOPE] for h in range(MLA_HEADS)]
               + [ukv[:, 256 * h + NOPE:256 * h + 256] for h in range(MLA_HEADS)])


def _grad_shards(name, g):
    cat = lambda parts: jnp.concatenate(parts, axis=1)
    if name in ("w_out", "w_down"):
        return {name: g.reshape(N_CHIPS, -1, g.shape[-1])}
    if name == "w_gu":
        cg = BIG_SHARD["w_gate"][1]
        return {"w_gate": [g[:, cg * j:cg * (j + 1)] for j in range(N_CHIPS)],
                "w_up": [g[:, D_FF + cg * j:D_FF + cg * (j + 1)] for j in range(N_CHIPS)]}
    if name == "w_in":
        ci, shift = BIG_SHARD["w_in"][1], MLA_IN - MLA_IN_USED
        return {name: [cat([g[:, :MLA_IN_USED], g[:, MLA_IN:ci + shift]])]
                + [g[:, ci * j + shift:ci * (j + 1) + shift] for j in range(1, N_CHIPS)]}
    if name == "w_uq":
        cq = NOPE + ROPE
        return {name: [cat([g[:, HEAD_PAD * h:HEAD_PAD * h + cq] for h in (2 * j, 2 * j + 1)]) for j in range(N_CHIPS)]}
    return {name: [cat([g[:, o + NOPE * h:o + NOPE * (h + 1)] for h in (2 * j, 2 * j + 1) for o in (0, MLA_HEADS * NOPE)])
                   for j in range(N_CHIPS)]}


def _small_layout(P):
    out, at = {}, 0
    for n in SMALL:
        out[n] = (at, P[n].size)
        at += P[n].size
    return out, at


def _flatten_small(P, last):
    v = jnp.concatenate([P[n].reshape(-1).astype(F32) for n in SMALL] + [last.reshape(-1).astype(F32)])
    return jnp.pad(v, (0, SMALL_ROWS * FLAT_W - v.size)).reshape(SMALL_ROWS, FLAT_W)


def _place():
    return lax.axis_index("x"), lax.axis_index("y"), lax.axis_index("c")


def _other_chips(x, y):
    return [(1 - x, y), (x, 1 - y), (1 - x, 1 - y)]


def _rcopy(src, dst, ssem, rsem, dev):
    return pltpu.make_async_remote_copy(src_ref=src, dst_ref=dst, send_sem=ssem, recv_sem=rsem, device_id=dev,
                                        device_id_type=MESH)


def _comm_call(body, name, out_shape, n_in, scratch):
    many = isinstance(out_shape, (list, tuple))
    return pl.pallas_call(body, name=name, out_shape=out_shape, in_specs=[HBM] * n_in,
                          out_specs=[HBM] * len(out_shape) if many else HBM, scratch_shapes=scratch)


def _half(ref, which):
    rows = ref.shape[0] // 2
    return ref.at[pl.ds(pl.multiple_of(which * rows, 16), rows)]


def _dma_sems(n):
    return pltpu.SemaphoreType.DMA((n,))


def _allgather_side(ws):
    k = len(ws)

    def peers():
        x, y, c = _place()
        return c, 2 * x + y, (x, y, 1 - c), [(n, t, cx, cy) for n in range(k) for t, (cx, cy) in enumerate(_other_chips(x, y))]

    def outgoing(w_refs, g_refs, sems):
        ssem, rsem, _, _, ossem, orsem = sems
        c, j, sib, nt = peers()
        owns = [_rcopy(w_refs[n], g_refs[n].at[j], ossem.at[n], orsem.at[n], sib) for n in range(k)]
        sends = [_rcopy(_half(w_refs[n], c), _half(g_refs[n].at[j], c), ssem.at[3 * n + t], rsem.at[3 * n + t],
                        (cx, cy, c)) for n, t, cx, cy in nt]
        return owns, sends

    def incoming(g_refs, sems):
        ssem, rsem, fssem, frsem, _, _ = sems
        c, _, sib, nt = peers()
        landed, passed, relayed = [], [], []
        for n, t, cx, cy in nt:
            mine, other = (_half(g_refs[n].at[2 * cx + cy], h) for h in (c, 1 - c))
            landed.append(_rcopy(mine, mine, ssem.at[3 * n + t], rsem.at[3 * n + t], (cx, cy, c)))
            passed.append(_rcopy(mine, mine, fssem.at[3 * n + t], frsem.at[3 * n + t], sib))
            relayed.append(_rcopy(other, other, fssem.at[3 * n + t], frsem.at[3 * n + t], sib))
        return landed, passed, relayed

    def start(w_refs, g_refs, sems):
        owns, sends = outgoing(w_refs, g_refs, sems)
        for cp in sends + owns:
            cp.start()

    def finish(w_refs, g_refs, sems):
        owns, sends = outgoing(w_refs, g_refs, sems)
        landed, passed, relayed = incoming(g_refs, sems)
        for got, on in zip(landed, passed):
            got.wait_recv()
            on.start()
        for cp in relayed:
            cp.wait_recv()
        for cp in owns:
            cp.wait()
        for cp in sends + passed:
            cp.wait_send()

    return _Side(list(ws), [_sds((N_CHIPS,) + w.shape, w.dtype) for w in ws],
                 [_dma_sems(3 * k)] * 4 + [_dma_sems(k)] * 2, start, finish)


def _exchange_side(parts):
    k = len(parts)

    def copies(p_refs, rcv_refs, sems):
        ssem, rsem = sems
        x, y, c = _place()
        return [_rcopy(p_refs[n].at[2 * cx + cy], rcv_refs[n].at[t], ssem.at[3 * n + t], rsem.at[3 * n + t], (cx, cy, c))
                for n in range(k) for t, (cx, cy) in enumerate(_other_chips(x, y))]

    def start(p_refs, rcv_refs, sems):
        for cp in copies(p_refs, rcv_refs, sems):
            cp.start()

    def finish(p_refs, rcv_refs, sems):
        for cp in copies(p_refs, rcv_refs, sems):
            cp.wait()

    return _Side(list(parts), [_sds((3,) + p.shape[1:], p.dtype) for p in parts], [_dma_sems(3 * k)] * 2, start, finish)


def _run_side(side, name):
    k_in, k_out = len(side.arrays), len(side.out_shape)

    def body(*refs):
        parts = refs[:k_in], refs[k_in:k_in + k_out], refs[k_in + k_out:]
        side.start(*parts)
        side.finish(*parts)

    return _comm_call(body, name, list(side.out_shape), k_in, list(side.scratch))(*side.arrays)


def _swap_halves(gds, name):
    k = len(gds)

    def body(*refs):
        gd_refs, out_refs, (ssem, rsem) = refs[:k], refs[k:2 * k], refs[2 * k:]
        x, y, c = _place()
        cps = [_rcopy(_half(gd_refs[n].at[jj], 1 - c), out_refs[n].at[jj], ssem.at[N_CHIPS * n + jj],
                      rsem.at[N_CHIPS * n + jj], (x, y, 1 - c)) for n in range(k) for jj in range(N_CHIPS)]
        for cp in cps:
            cp.start()
        for cp in cps:
            cp.wait()

    sems = _dma_sems(N_CHIPS * k)
    return _comm_call(body, name, [_sds((N_CHIPS, g.shape[1] // 2, g.shape[2]), g.dtype) for g in gds], k,
                      [sems, sems])(*gds)


def _allreduce_small(small):
    def body(s_ref, all_ref, sssem, srsem, lsem):
        x, y, c = _place()
        me = 4 * x + 2 * y + c
        own = pltpu.make_async_copy(s_ref, all_ref.at[me], lsem)
        own.start()
        cps = []
        for r in range(1, 8):
            fx, fy, fc = (r >> 2) & 1, (r >> 1) & 1, r & 1
            px, py, pc = (1 - x if fx else x, 1 - y if fy else y, 1 - c if fc else c)
            peer = 4 * px + 2 * py + pc
            send = _rcopy(s_ref, all_ref.at[me], sssem.at[r - 1], srsem.at[me], (px, py, pc))
            send.start()
            cps.append((send, _rcopy(s_ref, all_ref.at[peer], sssem.at[r - 1], srsem.at[peer], (px, py, pc))))
        for send, recv in cps:
            send.wait_send()
            recv.wait_recv()
        own.wait()

    return _comm_call(body, "allreduce_small", [_sds((8,) + small.shape, small.dtype)], 1,
                      [pltpu.SemaphoreType.DMA((7,)), pltpu.SemaphoreType.DMA((8,)), pltpu.SemaphoreType.DMA(())])(small)[0]


def _share_halves(reds, name):
    k = len(reds)

    def body(*refs):
        r_refs, out_refs, (ssem, rsem) = refs[:k], refs[k:2 * k], refs[2 * k:]
        x, y, c = _place()
        cps = [_rcopy(r_refs[n], out_refs[n], ssem.at[n], rsem.at[n], (x, y, 1 - c)) for n in range(k)]
        for cp in cps:
            cp.start()
        for cp in cps:
            cp.wait()

    return _comm_call(body, name, [_sds(r.shape, r.dtype) for r in reds], k, [_dma_sems(k)] * 2)(*reds)


def _add_pair(gd, got, c, name):
    _, R, W = got.shape
    tm = _pick(R, (512, 256, 128, 64))
    nb = R // tm

    def body(c_ref, a_ref, b_ref, o_ref):
        o_ref[...] = (a_ref[...].astype(F32) + b_ref[...].astype(F32)).astype(o_ref.dtype)

    grid_spec = pltpu.PrefetchScalarGridSpec(
        num_scalar_prefetch=1, grid=(N_CHIPS, nb),
        in_specs=[pl.BlockSpec((None, tm, W), lambda j, i, c_ref: (j, c_ref[0] * nb + i, 0)),
                  pl.BlockSpec((None, tm, W), lambda j, i, c_ref: (j, i, 0))],
        out_specs=pl.BlockSpec((None, tm, W), lambda j, i, c_ref: (j, i, 0)))
    return pl.pallas_call(body, name=name, grid_spec=grid_spec, out_shape=_sds((N_CHIPS, R, W), gd.dtype),
                          compiler_params=pltpu.CompilerParams(dimension_semantics=("parallel", "parallel"),
                                                               vmem_limit_bytes=VMEM_LIMIT))(c, gd, got)


def _add_chips(part, rcv, j, name):
    _, R, W = part.shape
    tm = _pick(R, (512, 256, 128, 64))

    def body(j_ref, p_ref, r0_ref, r1_ref, r2_ref, o_ref):
        o_ref[...] = ((p_ref[...].astype(F32) + r0_ref[...].astype(F32)) + r1_ref[...].astype(F32)) + r2_ref[...].astype(F32)

    def slot(t):
        return pl.BlockSpec((None, tm, W), lambda i, j_ref: (t, i, 0))

    grid_spec = pltpu.PrefetchScalarGridSpec(
        num_scalar_prefetch=1, grid=(R // tm,),
        in_specs=[pl.BlockSpec((None, tm, W), lambda i, j_ref: (j_ref[0], i, 0)), slot(0), slot(1), slot(2)],
        out_specs=pl.BlockSpec((tm, W), lambda i, j_ref: (i, 0)))
    return pl.pallas_call(body, name=name, grid_spec=grid_spec, out_shape=_sds((R, W), F32),
                          compiler_params=pltpu.CompilerParams(dimension_semantics=("parallel",),
                                                               vmem_limit_bytes=VMEM_LIMIT))(j, part, rcv, rcv, rcv)


def _sum_small(allsmall):
    _, R, W = allsmall.shape

    def body(a_ref, o_ref):
        acc = a_ref[0]
        for d in range(1, 8):
            acc = acc + a_ref[d]
        o_ref[...] = acc

    return _call(body, "sum_small", _sds((R, W), F32), (1,), [_whole((8, R, W))], _whole((R, W)),
                 sem=("arbitrary",))(allsmall)


def _adamw(w, g, m, v, name):
    R, C = w.shape
    tm = _pick(R, (256, 128, 64, 32, 8))

    def body(w_ref, g_ref, m_ref, v_ref, d_ref, mo_ref, vo_ref):
        gv = g_ref[...]
        mn = ADAM_B1 * m_ref[...] + (1.0 - ADAM_B1) * gv
        vn = ADAM_B2 * v_ref[...] + (1.0 - ADAM_B2) * (gv * gv)
        m_hat = mn / (1.0 - ADAM_B1 ** ADAM_STEP)
        v_hat = vn / (1.0 - ADAM_B2 ** ADAM_STEP)
        d_ref[...] = -ADAM_LR * (m_hat / (jnp.sqrt(v_hat) + ADAM_EPS) + ADAM_WD * w_ref[...])
        mo_ref[...] = mn
        vo_ref[...] = vn

    spec = _rows(tm, C)
    return _call(body, name, [_sds((R, C), F32)] * 3, (R // tm,), [spec] * 4, [spec] * 3, sem=("parallel",))(w, g, m, v)


def _adamw_layer(c, w, m, v, mine, other, l, prev, name):
    _, R, C = w.shape
    half = R // 2
    tm = _pick(half, (256, 128, 64))
    nbh = half // tm

    def body(c_ref, w_ref, m_ref, v_ref, a_ref, b_ref, *rest):
        g_ref, d_ref, mo_ref, vo_ref = rest[-4:]
        gv = jnp.where(pl.program_id(0) // nbh == c_ref[0], a_ref[...], b_ref[...])
        mn = ADAM_B1 * m_ref[...] + (1.0 - ADAM_B1) * gv
        vn = ADAM_B2 * v_ref[...] + (1.0 - ADAM_B2) * (gv * gv)
        m_hat = mn / (1.0 - ADAM_B1 ** ADAM_STEP)
        v_hat = vn / (1.0 - ADAM_B2 ** ADAM_STEP)
        g_ref[...] = gv
        d_ref[...] = -ADAM_LR * (m_hat / (jnp.sqrt(v_hat) + ADAM_EPS) + ADAM_WD * w_ref[...])
        mo_ref[...] = mn
        vo_ref[...] = vn

    layer = pl.BlockSpec((None, tm, C), lambda i, c_ref: (l, i, 0))
    halfspec = pl.BlockSpec((tm, C), lambda i, c_ref: (i % nbh, 0))
    n_prev = 0 if prev is None else 4
    grid_spec = pltpu.PrefetchScalarGridSpec(
        num_scalar_prefetch=1, grid=(R // tm,),
        in_specs=[layer] * 3 + [halfspec] * 2 + [pl.BlockSpec(memory_space=pl.ANY)] * n_prev,
        out_specs=[layer] * 4)
    return pl.pallas_call(body, name=name, grid_spec=grid_spec, out_shape=[_sds(w.shape, F32)] * 4,
                          input_output_aliases={6 + k: k for k in range(n_prev)},
                          compiler_params=pltpu.CompilerParams(dimension_semantics=("parallel",),
                                                               vmem_limit_bytes=VMEM_LIMIT))(
        c, w, m, v, mine, other, *(prev or ()))


FIRST_GATHER = ("w_in", "w_uq", "w_ukv")
GATHER_IN = {
    "mm_h_l0": (0, ("w_out",)), "mla_fwd_l0": (0, ("w_gate", "w_up")), "ret_fwd_l0": (0, ("w_down",)),
    "mm_gu_l0": (1, ("w_in", "w_uq", "w_ukv", "w_out")), "mm_down_l0": (1, ("w_gate",)),
    "mm_h_l1": (1, ("w_up",)), "mla_fwd_l1": (1, ("w_down",))}
EXCHANGE_IN = {
    "mm_dact": ("w_down",), "mm_dx1": ("w_gate", "w_up"), "ret_bwd": ("w_out",), "mm_dxl": ("w_uq", "w_ukv", "w_in")}


class _Pipeline:
    def __init__(self, own, Wt, Mo, Vo, core, chip):
        self.own, self.Wt, self.Mo, self.Vo, self.core, self.chip = own, Wt, Mo, Vo, core, chip
        self.blocks, self.whole, self.parts = {}, {}, {}
        self.results = {n: None for n in BIG}

    def _gathered(self, l, names, blocks):
        for n, b in zip(names, blocks):
            self.blocks[(l, n)] = b

    def gather_first(self):
        side = _allgather_side([self.own[0][n] for n in FIRST_GATHER])
        self._gathered(0, FIRST_GATHER, _run_side(side, "allgather_first"))

    def weight(self, l, name):
        if (l, name) not in self.whole:
            self.whole[(l, name)] = _internal_weight(name, *[self.blocks[(l, n)] for n in INTERNAL_OF[name]])
        return self.whole[(l, name)]

    def run(self, fn, name, *args, **kw):
        base, l = name[:-3], int(name[-1])
        if name in GATHER_IN:
            gl, names = GATHER_IN[name]
            out, blocks = fn(*args, name=name, side=_allgather_side([self.own[gl][n] for n in names]), **kw)
            self._gathered(gl, names, blocks)
            return out
        if base in EXCHANGE_IN:
            names = EXCHANGE_IN[base]
            out, rcvs = fn(*args, name=name, side=_exchange_side([self.parts[(l, n)] for n in names]), **kw)
            self._reduced(l, names, rcvs)
            return out
        return fn(*args, name=name, **kw)

    def reduce(self, l, **grads):
        shards = {}
        for name, g in grads.items():
            shards.update(_grad_shards(name, g))
        names = list(shards)
        gds = [shards[n] if hasattr(shards[n], "shape") else jnp.stack(shards[n]) for n in names]
        got = _swap_halves(gds, f"swap_halves_{names[0]}_l{l}")
        for n, gd, gt in zip(names, gds, got):
            self.parts[(l, n)] = _add_pair(gd, gt, self.core, f"add_pair_{n}_l{l}")

    def _reduced(self, l, names, rcvs):
        reds = [_add_chips(self.parts[(l, n)], rcv, self.chip, f"add_chips_{n}_l{l}") for n, rcv in zip(names, rcvs)]
        others = _share_halves(reds, f"share_halves_{names[0]}_l{l}")
        for n, red, other in zip(names, reds, others):
            self.results[n] = _adamw_layer(self.core, self.Wt[n], self.Mo[n], self.Vo[n], red, other, l,
                                           self.results[n], f"adamw_{n}_l{l}")


def kernel(x, positions, ln_in_g, ln_in_b, w_in, q_norm_g, kv_norm_g, w_uq, w_ukv, ret_gn_g, ret_gn_b, w_out, ln1_g, ln1_b, w_gate, w_up, w_down, ln2_g, ln2_b, loss_target, m_ln_in_g, m_ln_in_b, m_w_in, m_q_norm_g, m_kv_norm_g, m_w_uq, m_w_ukv, m_ret_gn_g, m_ret_gn_b, m_w_out, m_ln1_g, m_ln1_b, m_w_gate, m_w_up, m_w_down, m_ln2_g, m_ln2_b, v_ln_in_g, v_ln_in_b, v_w_in, v_q_norm_g, v_kv_norm_g, v_w_uq, v_w_ukv, v_ret_gn_g, v_ret_gn_b, v_w_out, v_ln1_g, v_ln1_b, v_w_gate, v_w_up, v_w_down, v_ln2_g, v_ln2_b):
    given = dict(locals())
    Wt = {n: given[n] for n in WEIGHTS}
    Mo = {n: given["m_" + n] for n in WEIGHTS}
    Vo = {n: given["v_" + n] for n in WEIGHTS}
    cx, cy, cc = _place()
    chip = (2 * cx + cy).astype(jnp.int32)
    core = cc.astype(jnp.int32)

    own = [{n: Wt[n][l].astype(BF16) for n in BIG} for l in range(DEPTH)]
    pipe = _Pipeline(own, Wt, Mo, Vo, core.reshape(1), chip.reshape(1))
    sqerr, grad_x, dP = _local_step(x[0], positions[0], loss_target[0], pipe, Wt)
    results = pipe.results

    small_g = {n: (dP[(n, None)] if Wt[n].ndim == 1 else jnp.stack([dP[(n, l)] for l in range(DEPTH)])) for n in SMALL}
    local_loss = 0.5 * jnp.sum(sqerr) / D_MODEL
    small_sum = _sum_small(_allreduce_small(_flatten_small(small_g, local_loss))).reshape(-1)
    layout, n_small = _small_layout(Wt)
    loss = small_sum[n_small]

    grads, deltas, new_m, new_v = {}, {}, {}, {}
    for n in BIG:
        grads[n], deltas[n], new_m[n], new_v[n] = results[n]
    zero = jnp.zeros((), F32)
    d, mn, vn = _adamw(_flatten_small(Wt, zero), small_sum.reshape(SMALL_ROWS, FLAT_W), _flatten_small(Mo, zero),
                       _flatten_small(Vo, zero), "adamw_small")
    for n in SMALL:
        at, size = layout[n]
        pick = lambda a: a.reshape(-1)[at:at + size].reshape(Wt[n].shape)
        grads[n], deltas[n], new_m[n], new_v[n] = pick(small_sum), pick(d), pick(mn), pick(vn)

    return (loss, grad_x[None], *[grads[n] for n in WEIGHTS], *[deltas[n] for n in WEIGHTS],
            *[new_m[n] for n in WEIGHTS], *[new_v[n] for n in WEIGHTS])
```

```python
import functools

import jax
import jax.numpy as jnp
from jax import lax
from jax.experimental import pallas as pl
from jax.experimental.pallas import tpu as pltpu

F32 = jnp.float32
BF16 = jnp.bfloat16

D_MODEL = 2048
DEPTH = 2
CHUNK = 64
MLA_HEADS = 8
Q_LORA = 512
KV_LORA = 256
NOPE = 128
ROPE = 64
VDIM = 128
RET_HEADS = 4
RET_DK = 256
RET_DV = 256
D_FF = 5632
D_IN = 4928
ROPE_THETA = 10000.0
LN_EPS = 1e-5
RMS_EPS = 1e-6
GN_EPS = 1e-5
ALPHA = (2 * DEPTH) ** 0.25
MLA_SCALE = (NOPE + ROPE) ** -0.5
RET_SCALE = RET_DK ** -0.5
ADAM_LR = 0.001
ADAM_B1 = 0.9
ADAM_B2 = 0.999
ADAM_EPS = 1e-08
ADAM_WD = 0.01
ADAM_STEP = 10

LANES = 128
HEAD_PAD = 256
MLA_IN = 1024
MLA_IN_USED = Q_LORA + KV_LORA + ROPE
D_IN_PAD = MLA_IN + 4 * 1024
ATT_BLOCK = 512
NEG = -1e30
VMEM_LIMIT = 56 * 1024 * 1024

N_CHIPS = 4
FLAT_W = 1024
BIG = ("w_in", "w_uq", "w_ukv", "w_out", "w_gate", "w_up", "w_down")
BIG_SHARD = {"w_in": (2048, 1232), "w_uq": (512, 384), "w_ukv": (256, 512), "w_out": (512, 2048),
             "w_gate": (2048, 1408), "w_up": (2048, 1408), "w_down": (1408, 2048)}
SMALL = ("ln_in_g", "ln_in_b", "q_norm_g", "kv_norm_g", "ret_gn_g", "ret_gn_b", "ln1_g", "ln1_b", "ln2_g", "ln2_b")
WEIGHTS = ("ln_in_g", "ln_in_b", "w_in", "q_norm_g", "kv_norm_g", "w_uq", "w_ukv", "ret_gn_g", "ret_gn_b", "w_out",
           "ln1_g", "ln1_b", "w_gate", "w_up", "w_down", "ln2_g", "ln2_b")
SMALL_ROWS = 32

MESH = pl.DeviceIdType.MESH


def _pick(dim, cands):
    for c in cands:
        if dim % c == 0:
            return c
    return dim


HBM = pl.BlockSpec(memory_space=pltpu.HBM)


class _Side:
    def __init__(self, arrays, out_shape, scratch, start, finish):
        self.arrays, self.out_shape, self.scratch, self.start, self.finish = arrays, out_shape, scratch, start, finish


def _call(body, name, out_shape, grid, in_specs, out_specs, scratch=(), sem=None, side=None):
    params = pltpu.CompilerParams(dimension_semantics=sem if side is None else ("arbitrary",) * len(grid),
                                  vmem_limit_bytes=VMEM_LIMIT)
    if side is None:
        return pl.pallas_call(body, name=name, out_shape=out_shape, grid=grid, in_specs=in_specs, out_specs=out_specs,
                              scratch_shapes=list(scratch), compiler_params=params)
    single = not isinstance(out_shape, (list, tuple))
    outs = [out_shape] if single else list(out_shape)
    ospecs = [out_specs] if single else list(out_specs)
    cuts = [len(in_specs), len(side.arrays), len(outs), len(side.out_shape), len(scratch)]
    ends = [sum(cuts[:k + 1]) for k in range(len(cuts))]

    def hosted(*refs):
        ins, s_in, o, s_out, scr = (refs[a:b] for a, b in zip([0] + ends[:-1], ends))
        sems = refs[ends[-1]:]
        ids = [pl.program_id(a) for a in range(len(grid))]
        first = functools.reduce(jnp.logical_and, [i == 0 for i in ids])
        last = functools.reduce(jnp.logical_and, [i == g - 1 for i, g in zip(ids, grid)])

        @pl.when(first)
        def _():
            side.start(s_in, s_out, sems)

        body(*ins, *o, *scr)

        @pl.when(last)
        def _():
            side.finish(s_in, s_out, sems)

    call = pl.pallas_call(hosted, name=name, out_shape=outs + list(side.out_shape), grid=grid,
                          in_specs=list(in_specs) + [HBM] * len(side.arrays),
                          out_specs=ospecs + [HBM] * len(side.out_shape),
                          scratch_shapes=list(scratch) + list(side.scratch), compiler_params=params)

    def run(*args):
        res = call(*args, *side.arrays)
        return (res[0] if single else list(res[:len(outs)])), list(res[len(outs):])

    return run


def _rows(tm, w, col=0):
    return pl.BlockSpec((tm, w), lambda i: (i, col))


def _whole(shape):
    return pl.BlockSpec(shape, lambda i: (0,) * len(shape))


def _sds(shape, dtype):
    return jax.ShapeDtypeStruct(shape, dtype)


def _matmul(a, b, name, ta=False, tb=False, out_dtype=F32, side=None):
    (K, M) = a.shape if ta else a.shape[::-1]
    (N, Kb) = b.shape if tb else b.shape[::-1]
    assert K == Kb, (a.shape, b.shape, ta, tb)
    tm = _pick(M, (1024, 512, 256, 128))
    tn = _pick(N, (1024, 512, 256, 128))
    tk = _pick(K, (2816, 2560, 2048, 1024, 512, 256))
    nk = K // tk
    dn = (((0 if ta else 1,), (1 if tb else 0,)), ((), ()))

    def body(a_ref, b_ref, o_ref, acc_ref):
        k = pl.program_id(2)
        if nk == 1:
            o_ref[...] = lax.dot_general(a_ref[...].astype(BF16), b_ref[...].astype(BF16), dn,
                                         preferred_element_type=F32).astype(out_dtype)
        else:
            @pl.when(k == 0)
            def _():
                acc_ref[...] = jnp.zeros_like(acc_ref)

            acc_ref[...] += lax.dot_general(a_ref[...].astype(BF16), b_ref[...].astype(BF16), dn,
                                            preferred_element_type=F32)

            @pl.when(k == nk - 1)
            def _():
                o_ref[...] = acc_ref[...].astype(out_dtype)

    a_spec = pl.BlockSpec((tk, tm), lambda i, j, k: (k, i)) if ta else pl.BlockSpec((tm, tk), lambda i, j, k: (i, k))
    b_spec = pl.BlockSpec((tn, tk), lambda i, j, k: (j, k)) if tb else pl.BlockSpec((tk, tn), lambda i, j, k: (k, j))
    return _call(body, name, _sds((M, N), out_dtype), (M // tm, N // tn, nk), [a_spec, b_spec],
                 pl.BlockSpec((tm, tn), lambda i, j, k: (i, j)), scratch=[pltpu.VMEM((tm, tn), F32)],
                 sem=("parallel", "parallel", "arbitrary"), side=side)(a, b)


def _sigmoid(x):
    return 1.0 / (1.0 + jnp.exp(-x))


def _rope_group(r, c, sa, sb):
    return r * c + pltpu.roll(r, 32, 1) * sa + pltpu.roll(r, 96, 1) * sb


def _ln_fwd(xs, coefs, g, b, name, want_z):
    S, D = xs[0].shape
    tm = 256
    n = len(xs)

    def body(*refs):
        x_refs, g_ref, b_ref, outs = refs[:n], refs[n], refs[n + 1], refs[n + 2:]
        z = None
        for cf, r in zip(coefs, x_refs):
            t = r[...] if cf == 1.0 else cf * r[...]
            z = t if z is None else z + t
        mu = jnp.mean(z, axis=-1, keepdims=True)
        zc = z - mu
        var = jnp.mean(zc * zc, axis=-1, keepdims=True)
        y = zc * lax.rsqrt(var + LN_EPS) * g_ref[...] + b_ref[...]
        if want_z:
            outs[0][...] = z
        outs[-2][...] = y
        outs[-1][...] = y.astype(BF16)

    out_shape = [_sds((S, D), F32)] * (2 if want_z else 1) + [_sds((S, D), BF16)]
    return _call(body, name, out_shape, (S // tm,), [_rows(tm, D)] * n + [_whole((1, D))] * 2,
                 [_rows(tm, D)] * len(out_shape), sem=("parallel",))(*xs, g, b)


def _ln_bwd(dys, coefs, z, g, name):
    S, D = z.shape
    tm = 256
    n = len(dys)

    def body(*refs):
        dy_refs, z_ref, g_ref = refs[:n], refs[n], refs[n + 1]
        dz_ref, dzb_ref, dg_ref, db_ref = refs[n + 2:]
        dy = None
        for cf, r in zip(coefs, dy_refs):
            t = r[...] if cf == 1.0 else cf * r[...]
            dy = t if dy is None else dy + t
        zv = z_ref[...]
        mu = jnp.mean(zv, axis=-1, keepdims=True)
        zc = zv - mu
        var = jnp.mean(zc * zc, axis=-1, keepdims=True)
        rstd = lax.rsqrt(var + LN_EPS)
        xh = zc * rstd
        dyg = dy * g_ref[...]
        dz = rstd * (dyg - jnp.mean(dyg, axis=-1, keepdims=True) - xh * jnp.mean(dyg * xh, axis=-1, keepdims=True))
        dz_ref[...] = dz
        dzb_ref[...] = dz.astype(BF16)

        @pl.when(pl.program_id(0) == 0)
        def _():
            dg_ref[...] = jnp.zeros_like(dg_ref)
            db_ref[...] = jnp.zeros_like(db_ref)

        dg_ref[...] += jnp.sum(dy * xh, axis=0, keepdims=True)
        db_ref[...] += jnp.sum(dy, axis=0, keepdims=True)

    return _call(body, name, [_sds((S, D), F32), _sds((S, D), BF16), _sds((1, D), F32), _sds((1, D), F32)],
                 (S // tm,), [_rows(tm, D)] * (n + 1) + [_whole((1, D))],
                 [_rows(tm, D), _rows(tm, D), _whole((1, D)), _whole((1, D))], sem=("arbitrary",))(*dys, z, g)


def _rms(x, g):
    return x * lax.rsqrt(jnp.mean(x * x, axis=-1, keepdims=True) + RMS_EPS) * g


def _prep1(h, tabs, qg, kvg, name):
    S = h.shape[0]
    tm = 256
    cm, sam, sbm, cr, sr = tabs

    def body(h_ref, cm_ref, sam_ref, sbm_ref, cr_ref, sr_ref, qg_ref, kvg_ref,
             qn_ref, kvn_ref, kr_ref, rq_ref, rk_ref, rv_ref):
        qn_ref[...] = _rms(h_ref[:, 0:Q_LORA], qg_ref[...]).astype(BF16)
        kvn_ref[...] = _rms(h_ref[:, Q_LORA:Q_LORA + KV_LORA], kvg_ref[...]).astype(BF16)
        kr_ref[...] = _rope_group(h_ref[:, 768:896], cm_ref[...], sam_ref[...], sbm_ref[...])
        c, s = cr_ref[...], sr_ref[...]
        for hd in range(RET_HEADS):
            for src, dst, scale in ((MLA_IN, rq_ref, RET_SCALE), (MLA_IN + 1024, rk_ref, None)):
                t1 = h_ref[:, src + hd * 256:src + hd * 256 + 128]
                t2 = h_ref[:, src + hd * 256 + 128:src + hd * 256 + 256]
                o1, o2 = t1 * c - t2 * s, t2 * c + t1 * s
                if scale is not None:
                    o1, o2 = o1 * scale, o2 * scale
                dst[:, hd * 256:hd * 256 + 128] = o1.astype(BF16)
                dst[:, hd * 256 + 128:hd * 256 + 256] = o2.astype(BF16)
        rv_ref[...] = h_ref[:, MLA_IN + 2048:MLA_IN + 3072].astype(BF16)

    t128 = _rows(tm, LANES)
    return _call(body, name,
                 [_sds((S, Q_LORA), BF16), _sds((S, KV_LORA), BF16), _sds((S, LANES), F32),
                  _sds((S, 1024), BF16), _sds((S, 1024), BF16), _sds((S, 1024), BF16)],
                 (S // tm,),
                 [_rows(tm, D_IN_PAD), t128, t128, t128, t128, t128, _whole((1, Q_LORA)), _whole((1, KV_LORA))],
                 [_rows(tm, Q_LORA), _rows(tm, KV_LORA), t128, _rows(tm, 1024), _rows(tm, 1024), _rows(tm, 1024)],
                 sem=("parallel",))(h, cm, sam, sbm, cr, sr, qg, kvg)


def _prep1_bwd(dqn, dkvn, dkr, drq, drk, drv, drg, h, tabs, qg, kvg, name):
    S = h.shape[0]
    tm = 256
    cm, sam, sbm, cr, sr = tabs

    def rms_bwd(x, g, dy):
        r = lax.rsqrt(jnp.mean(x * x, axis=-1, keepdims=True) + RMS_EPS)
        dyg = dy * g
        dx = r * dyg - x * (r * r * r) * jnp.mean(dyg * x, axis=-1, keepdims=True)
        return dx, jnp.sum(dy * x * r, axis=0, keepdims=True)

    def body(dqn_ref, dkvn_ref, dkr_ref, drq_ref, drk_ref, drv_ref, drg_ref, h_ref,
             cm_ref, sam_ref, sbm_ref, cr_ref, sr_ref, qg_ref, kvg_ref, dh_ref, dqg_ref, dkvg_ref):
        dcq, dqg = rms_bwd(h_ref[:, 0:Q_LORA], qg_ref[...], dqn_ref[...])
        dckv, dkvg = rms_bwd(h_ref[:, Q_LORA:Q_LORA + KV_LORA], kvg_ref[...], dkvn_ref[...])
        dh_ref[:, 0:Q_LORA] = dcq.astype(BF16)
        dh_ref[:, Q_LORA:Q_LORA + KV_LORA] = dckv.astype(BF16)
        dh_ref[:, 768:896] = _rope_group(dkr_ref[...], cm_ref[...], -sam_ref[...], -sbm_ref[...]).astype(BF16)
        dh_ref[:, 896:1024] = jnp.zeros((tm, LANES), BF16)
        c, s = cr_ref[...], sr_ref[...]
        for hd in range(RET_HEADS):
            for src, dst, scale in ((drq_ref, MLA_IN, RET_SCALE), (drk_ref, MLA_IN + 1024, None)):
                d1 = src[:, hd * 256:hd * 256 + 128]
                d2 = src[:, hd * 256 + 128:hd * 256 + 256]
                if scale is not None:
                    d1, d2 = d1 * scale, d2 * scale
                dh_ref[:, dst + hd * 256:dst + hd * 256 + 128] = (d1 * c + d2 * s).astype(BF16)
                dh_ref[:, dst + hd * 256 + 128:dst + hd * 256 + 256] = (d2 * c - d1 * s).astype(BF16)
        dh_ref[:, MLA_IN + 2048:MLA_IN + 3072] = drv_ref[...].astype(BF16)
        dh_ref[:, MLA_IN + 3072:MLA_IN + 4096] = drg_ref[...].astype(BF16)

        @pl.when(pl.program_id(0) == 0)
        def _():
            dqg_ref[...] = jnp.zeros_like(dqg_ref)
            dkvg_ref[...] = jnp.zeros_like(dkvg_ref)

        dqg_ref[...] += dqg
        dkvg_ref[...] += dkvg

    t128 = _rows(tm, LANES)
    return _call(body, name,
                 [_sds((S, D_IN_PAD), BF16), _sds((1, Q_LORA), F32), _sds((1, KV_LORA), F32)],
                 (S // tm,),
                 [_rows(tm, Q_LORA), _rows(tm, KV_LORA), t128, _rows(tm, 1024), _rows(tm, 1024), _rows(tm, 1024),
                  _rows(tm, 1024), _rows(tm, D_IN_PAD), t128, t128, t128, t128, t128,
                  _whole((1, Q_LORA)), _whole((1, KV_LORA))],
                 [_rows(tm, D_IN_PAD), _whole((1, Q_LORA)), _whole((1, KV_LORA))],
                 sem=("arbitrary",))(dqn, dkvn, dkr, drq, drk, drv, drg, h, cm, sam, sbm, cr, sr, qg, kvg)


def _prep2(q, kv, kr, tabs, name):
    S = q.shape[0]
    tm = 256
    cm, sam, sbm = tabs[:3]

    def body(q_ref, kv_ref, kr_ref, cm_ref, sam_ref, sbm_ref, qo_ref, ko_ref, vo_ref):
        c, sa, sb = cm_ref[...], sam_ref[...], sbm_ref[...]
        krb = kr_ref[...].astype(BF16)
        for hd in range(MLA_HEADS):
            o = hd * HEAD_PAD
            qo_ref[:, o:o + 128] = q_ref[:, o:o + 128].astype(BF16)
            qo_ref[:, o + 128:o + 256] = _rope_group(q_ref[:, o + 128:o + 256], c, sa, sb).astype(BF16)
            ko_ref[:, o:o + 128] = kv_ref[:, hd * 128:hd * 128 + 128].astype(BF16)
            ko_ref[:, o + 128:o + 256] = krb
        vo_ref[...] = kv_ref[:, 1024:2048].astype(BF16)

    t128 = _rows(tm, LANES)
    return _call(body, name, [_sds((S, 2048), BF16), _sds((S, 2048), BF16), _sds((S, 1024), BF16)], (S // tm,),
                 [_rows(tm, 2048), _rows(tm, 2048), t128, t128, t128, t128],
                 [_rows(tm, 2048), _rows(tm, 2048), _rows(tm, 1024)], sem=("parallel",))(q, kv, kr, cm, sam, sbm)


def _prep2_bwd(dqm, dkm, dvm, tabs, name):
    S = dqm.shape[0]
    tm = 256
    cm, sam, sbm = tabs[:3]

    def body(dq_ref, dk_ref, dv_ref, cm_ref, sam_ref, sbm_ref, dqo_ref, dkvo_ref, dkr_ref):
        c, sa, sb = cm_ref[...], -sam_ref[...], -sbm_ref[...]
        dkr = None
        for hd in range(MLA_HEADS):
            o = hd * HEAD_PAD
            dqo_ref[:, o:o + 128] = dq_ref[:, o:o + 128].astype(BF16)
            dqo_ref[:, o + 128:o + 256] = _rope_group(dq_ref[:, o + 128:o + 256], c, sa, sb).astype(BF16)
            dkvo_ref[:, hd * 128:hd * 128 + 128] = dk_ref[:, o:o + 128].astype(BF16)
            t = dk_ref[:, o + 128:o + 256]
            dkr = t if dkr is None else dkr + t
        dkvo_ref[:, 1024:2048] = dv_ref[...].astype(BF16)
        dkr_ref[...] = dkr

    t128 = _rows(tm, LANES)
    return _call(body, name, [_sds((S, 2048), BF16), _sds((S, 2048), BF16), _sds((S, LANES), F32)], (S // tm,),
                 [_rows(tm, 2048), _rows(tm, 2048), _rows(tm, 1024), t128, t128, t128],
                 [_rows(tm, 2048), _rows(tm, 2048), t128], sem=("parallel",))(dqm, dkm, dvm, cm, sam, sbm)


def _gn_gate(a, o, h, gg, gb, name):
    S = a.shape[0]
    tm = 256

    def body(a_ref, o_ref, rg_ref, gg_ref, gb_ref, mix_ref):
        mix_ref[:, 0:1024] = a_ref[...].astype(BF16)
        for hd in range(RET_HEADS):
            sl = slice(hd * 256, hd * 256 + 256)
            ov = o_ref[:, sl]
            mu = jnp.mean(ov, axis=-1, keepdims=True)
            oc = ov - mu
            var = jnp.mean(oc * oc, axis=-1, keepdims=True)
            y = oc * lax.rsqrt(var + GN_EPS) * gg_ref[:, sl] + gb_ref[:, sl]
            rg = rg_ref[:, sl]
            mix_ref[:, 1024 + hd * 256:1024 + hd * 256 + 256] = (rg * _sigmoid(rg) * y).astype(BF16)

    return _call(body, name, _sds((S, 2048), BF16), (S // tm,),
                 [_rows(tm, 1024), _rows(tm, 1024), _rows(tm, 1024, 4), _whole((1, 1024)), _whole((1, 1024))],
                 _rows(tm, 2048), sem=("parallel",))(a, o, h, gg, gb)


def _gn_gate_bwd(dmixin, o, h, gg, gb, name):
    S = o.shape[0]
    tm = 256

    def body(dr_ref, o_ref, rg_ref, gg_ref, gb_ref, do_ref, drg_ref, dgg_ref, dgb_ref):
        @pl.when(pl.program_id(0) == 0)
        def _():
            dgg_ref[...] = jnp.zeros_like(dgg_ref)
            dgb_ref[...] = jnp.zeros_like(dgb_ref)

        for hd in range(RET_HEADS):
            sl = slice(hd * 256, hd * 256 + 256)
            ov = o_ref[:, sl]
            mu = jnp.mean(ov, axis=-1, keepdims=True)
            oc = ov - mu
            var = jnp.mean(oc * oc, axis=-1, keepdims=True)
            rstd = lax.rsqrt(var + GN_EPS)
            xh = oc * rstd
            g = gg_ref[:, sl]
            y = xh * g + gb_ref[:, sl]
            rg = rg_ref[:, sl]
            sg = _sigmoid(rg)
            dr = dr_ref[:, sl]
            dy = dr * (rg * sg)
            drg_ref[:, sl] = dr * y * (sg * (1.0 + rg * (1.0 - sg)))
            dgg_ref[:, sl] += jnp.sum(dy * xh, axis=0, keepdims=True)
            dgb_ref[:, sl] += jnp.sum(dy, axis=0, keepdims=True)
            dxh = dy * g
            do = rstd * (dxh - jnp.mean(dxh, axis=-1, keepdims=True) - xh * jnp.mean(dxh * xh, axis=-1, keepdims=True))
            do_ref[:, sl] = do.astype(BF16)

    return _call(body, name,
                 [_sds((S, 1024), BF16), _sds((S, 1024), F32), _sds((1, 1024), F32), _sds((1, 1024), F32)],
                 (S // tm,),
                 [_rows(tm, 1024, 1), _rows(tm, 1024), _rows(tm, 1024, 4), _whole((1, 1024)), _whole((1, 1024))],
                 [_rows(tm, 1024), _rows(tm, 1024), _whole((1, 1024)), _whole((1, 1024))],
                 sem=("arbitrary",))(dmixin, o, h, gg, gb)


def _swiglu(gu, name):
    S = gu.shape[0]
    tm = 256

    def body(g_ref, u_ref, o_ref):
        g = g_ref[...]
        o_ref[...] = (g * _sigmoid(g) * u_ref[...]).astype(BF16)

    return _call(body, name, _sds((S, D_FF), BF16), (S // tm,), [_rows(tm, D_FF, 0), _rows(tm, D_FF, 1)],
                 _rows(tm, D_FF), sem=("parallel",))(gu, gu)


def _swiglu_bwd(gu, dact, name):
    S = gu.shape[0]
    tm = 128

    def body(g_ref, u_ref, d_ref, o_ref):
        g, u, d = g_ref[...], u_ref[...], d_ref[...]
        sg = _sigmoid(g)
        o_ref[:, 0:D_FF] = (d * u * (sg * (1.0 + g * (1.0 - sg)))).astype(BF16)
        o_ref[:, D_FF:2 * D_FF] = (d * (g * sg)).astype(BF16)

    return _call(body, name, _sds((S, 2 * D_FF), BF16), (S // tm,),
                 [_rows(tm, D_FF, 0), _rows(tm, D_FF, 1), _rows(tm, D_FF)], _rows(tm, 2 * D_FF),
                 sem=("parallel",))(gu, gu, dact)


def _loss_head(y, target, name):
    S, D = y.shape
    tm = 256

    def body(y_ref, t_ref, dy_ref, acc_ref):
        e = y_ref[...] - t_ref[...]
        dy_ref[...] = e / D

        @pl.when(pl.program_id(0) == 0)
        def _():
            acc_ref[...] = jnp.zeros_like(acc_ref)

        acc_ref[...] += jnp.sum(e * e, axis=0, keepdims=True)

    return _call(body, name, [_sds((S, D), F32), _sds((1, D), F32)], (S // tm,), [_rows(tm, D), _rows(tm, D)],
                 [_rows(tm, D), _whole((1, D))], sem=("arbitrary",))(y, target)


def _chunk_mask(T):
    r = lax.shift_right_logical(lax.broadcasted_iota(jnp.int32, (T, T), 0), 6)
    c = lax.shift_right_logical(lax.broadcasted_iota(jnp.int32, (T, T), 1), 6)
    return r >= c


def _dot_nt(a, b):
    return lax.dot_general(a, b, (((1,), (1,)), ((), ())), preferred_element_type=F32)


def _dot_tn(a, b):
    return lax.dot_general(a, b, (((0,), (0,)), ((), ())), preferred_element_type=F32)


def _decay_tables(T):
    lg = jnp.log1p(-jnp.exp2(-5.0 - jnp.arange(RET_HEADS, dtype=F32)))
    idx = jnp.arange(T, dtype=F32)
    diff = idx[:, None] - idx[None, :]
    rel = jnp.exp(lg[:, None, None] * diff[None])
    cid = jnp.arange(T) // CHUNK
    mask = (cid[:, None] >= cid[None, :]).astype(F32)
    reld = jnp.exp(lg[:, None, None] * jnp.abs(diff)[None]) * mask[None]
    lgrow = jnp.broadcast_to(lg[:, None, None], (RET_HEADS, 1, LANES))
    return lgrow, rel, reld


def _attn_fwd(q, k, v, heads, dk, dv, softmax, name, tables=None, side=None):
    S = q.shape[0]
    T = ATT_BLOCK
    nq = S // T
    rep = T // LANES

    def body(*refs):
        if softmax:
            q_ref, k_ref, v_ref, o_ref, lse_ref, m_sc, l_sc, acc_sc = refs
        else:
            q_ref, k_ref, v_ref, lg_ref, rel_ref, reld_ref, o_ref, acc_sc = refs
        i = pl.program_id(1)
        qv = q_ref[...]

        def kv_block(j):
            rows = pl.ds(pl.multiple_of(j * T, T), T)
            return k_ref[rows, :], v_ref[rows, :]

        kb, vb = kv_block(i)
        s = _dot_nt(qv, kb)
        if softmax:
            s = jnp.where(_chunk_mask(T), s * MLA_SCALE, NEG)
            m = jnp.max(s, axis=-1, keepdims=True)
            p = jnp.exp(s - m)
            m_sc[...] = jnp.broadcast_to(m, (T, LANES))
            l_sc[...] = jnp.broadcast_to(jnp.sum(p, axis=-1, keepdims=True), (T, LANES))
        else:
            p = s * reld_ref[0]
        acc_sc[...] = jnp.dot(p.astype(BF16), vb, preferred_element_type=F32)

        def scores(j):
            kb, vb = kv_block(j)
            return _dot_nt(qv, kb), vb

        def update(j, s, vb):
            if softmax:
                s = s * MLA_SCALE
                m_prev = m_sc[...]
                m_next = jnp.maximum(m_prev, jnp.max(s, axis=-1, keepdims=True))
                alpha = jnp.exp(m_prev - m_next)
                p = jnp.exp(s - jnp.tile(m_next, (1, rep)))
                l_sc[...] = alpha * l_sc[...] + jnp.sum(p, axis=-1, keepdims=True)
                m_sc[...] = m_next
                acc_sc[...] = acc_sc[...] * jnp.tile(alpha, (1, dv // LANES)) + jnp.dot(
                    p.astype(BF16), vb, preferred_element_type=F32)
            else:
                fac = jnp.exp(lg_ref[0] * ((i - j) * T).astype(F32))
                p = s * (rel_ref[0] * jnp.tile(fac, (1, rep)))
                acc_sc[...] += jnp.dot(p.astype(BF16), vb, preferred_element_type=F32)

        def pair(jj, carry):
            first, second = scores(2 * jj), scores(2 * jj + 1)
            update(2 * jj, *first)
            update(2 * jj + 1, *second)
            return carry

        lax.fori_loop(0, i // 2, pair, 0)

        @pl.when(i % 2 == 1)
        def _():
            update(i - 1, *scores(i - 1))

        if softmax:
            l = l_sc[...]
            o_ref[...] = acc_sc[...] / jnp.tile(l, (1, dv // LANES))
            lse_ref[...] = m_sc[...] + jnp.log(l)
        else:
            o_ref[...] = acc_sc[...]

    in_specs = [pl.BlockSpec((T, dk), lambda h, i: (i, h)), pl.BlockSpec((S, dk), lambda h, i: (0, h)),
                pl.BlockSpec((S, dv), lambda h, i: (0, h))]
    o_spec = pl.BlockSpec((T, dv), lambda h, i: (i, h))
    if softmax:
        return _call(body, name, [_sds((S, heads * dv), F32), _sds((S, heads * LANES), F32)], (heads, nq), in_specs,
                     [o_spec, pl.BlockSpec((T, LANES), lambda h, i: (i, h))],
                     scratch=[pltpu.VMEM((T, LANES), F32), pltpu.VMEM((T, LANES), F32), pltpu.VMEM((T, dv), F32)],
                     sem=("parallel", "arbitrary"), side=side)(q, k, v)
    lgrow, rel, reld = tables
    in_specs += [pl.BlockSpec((1, 1, LANES), lambda h, i: (h, 0, 0)), pl.BlockSpec((1, T, T), lambda h, i: (h, 0, 0)),
                 pl.BlockSpec((1, T, T), lambda h, i: (h, 0, 0))]
    return _call(body, name, _sds((S, heads * dv), F32), (heads, nq), in_specs, o_spec,
                 scratch=[pltpu.VMEM((T, dv), F32)], sem=("parallel", "arbitrary"), side=side)(q, k, v, lgrow, rel, reld)


def _attn_bwd(q, k, v, do, heads, dk, dv, softmax, name, o=None, lse=None, tables=None, side=None):
    S = q.shape[0]
    T = ATT_BLOCK
    nq = S // T
    rep = T // LANES

    def body(*refs):
        if softmax:
            q_ref, k_ref, v_ref, do_ref, o_ref, lse_ref, dq_ref, dk_ref, dv_ref, dq_sc = refs
        else:
            q_ref, k_ref, v_ref, do_ref, lg_ref, rel_ref, reld_ref, dq_ref, dk_ref, dv_ref, dq_sc = refs
        i = pl.program_id(1)

        @pl.when(i == 0)
        def _():
            dk_ref[...] = jnp.zeros_like(dk_ref)
            dv_ref[...] = jnp.zeros_like(dv_ref)

        qv = q_ref[...]
        dof = do_ref[...].astype(F32)
        dov = dof.astype(BF16)
        if softmax:
            delta = jnp.sum(dof * o_ref[...], axis=-1, keepdims=True)
            lse_t = jnp.tile(lse_ref[...], (1, rep))
        dq_sc[...] = jnp.zeros_like(dq_sc)

        def products(j):
            rows = pl.ds(pl.multiple_of(j * T, T), T)
            kb = k_ref[rows, :]
            return rows, kb, _dot_nt(qv, kb), _dot_nt(dov, v_ref[rows, :])

        def block(j, diagonal, rows, kb, s, dp):
            if softmax:
                s = s * MLA_SCALE
                if diagonal:
                    s = jnp.where(_chunk_mask(T), s, NEG)
                p = jnp.exp(s - lse_t)
                ds = p * (dp - delta) * MLA_SCALE
            else:
                if diagonal:
                    dec = reld_ref[0]
                else:
                    fac = jnp.exp(lg_ref[0] * ((i - j) * T).astype(F32))
                    dec = rel_ref[0] * jnp.tile(fac, (1, rep))
                p = s * dec
                ds = dp * dec
            dsb = ds.astype(BF16)
            dv_ref[rows, :] += _dot_tn(p.astype(BF16), dov)
            dk_ref[rows, :] += _dot_tn(dsb, qv)
            dq_sc[...] += jnp.dot(dsb, kb, preferred_element_type=F32)

        block(i, True, *products(i))

        def pair(jj, carry):
            first, second = products(2 * jj), products(2 * jj + 1)
            block(2 * jj, False, *first)
            block(2 * jj + 1, False, *second)
            return carry

        lax.fori_loop(0, i // 2, pair, 0)

        @pl.when(i % 2 == 1)
        def _():
            block(i - 1, False, *products(i - 1))

        dq_ref[...] = dq_sc[...]

    qspec = pl.BlockSpec((T, dk), lambda h, i: (i, h))
    kspec = pl.BlockSpec((S, dk), lambda h, i: (0, h))
    vspec = pl.BlockSpec((S, dv), lambda h, i: (0, h))
    dospec = pl.BlockSpec((T, dv), lambda h, i: (i, h))
    in_specs = [qspec, kspec, vspec, dospec]
    args = [q, k, v, do]
    if softmax:
        in_specs += [dospec, pl.BlockSpec((T, LANES), lambda h, i: (i, h))]
        args += [o, lse]
    else:
        in_specs += [pl.BlockSpec((1, 1, LANES), lambda h, i: (h, 0, 0)),
                     pl.BlockSpec((1, T, T), lambda h, i: (h, 0, 0)), pl.BlockSpec((1, T, T), lambda h, i: (h, 0, 0))]
        args += list(tables)
    return _call(body, name, [_sds((S, heads * dk), F32), _sds((S, heads * dk), F32), _sds((S, heads * dv), F32)],
                 (heads, nq), in_specs, [qspec, kspec, vspec], scratch=[pltpu.VMEM((T, dk), F32)],
                 sem=("parallel", "arbitrary"), side=side)(*args)


def _rope_tables(pos):
    def tables(dim):
        inv_freq = ROPE_THETA ** (-jnp.arange(0, dim, 2, dtype=F32) / dim)
        ang = pos.astype(F32)[:, None] * inv_freq
        return jnp.cos(ang), jnp.sin(ang)

    cm, sm = tables(ROPE)
    S = pos.shape[0]
    z32, z64 = jnp.zeros((S, 32), F32), jnp.zeros((S, 64), F32)
    cr, sr = tables(RET_DK)
    return (jnp.concatenate([cm, cm, z64], 1), jnp.concatenate([z32, sm, z64], 1),
            jnp.concatenate([-sm, z32, z64], 1), cr, sr)


def _row(v):
    return v.reshape(1, -1).astype(F32)


def _local_step(x, pos, target, pipe, P):
    tabs = _rope_tables(pos)
    dtabs = _decay_tables(ATT_BLOCK)
    xf, xb = _ln_fwd([x], [1.0], _row(P["ln_in_g"]), _row(P["ln_in_b"]), "ln_in", False)
    pipe.gather_first()
    saved = []
    for l in range(DEPTH):
        w = functools.partial(pipe.weight, l)
        t = f"_l{l}"
        h = pipe.run(_matmul, "mm_h" + t, xb, w("w_in"))
        qn, kvn, kr, rq, rk, rv = _prep1(h, tabs, _row(P["q_norm_g"][l]), _row(P["kv_norm_g"][l]), "prep1" + t)
        q = _matmul(qn, w("w_uq"), "mm_q" + t)
        kv = _matmul(kvn, w("w_ukv"), "mm_kv" + t)
        qm, km, vm = _prep2(q, kv, kr, tabs, "prep2" + t)
        a, lse = pipe.run(_attn_fwd, "mla_fwd" + t, qm, km, vm, MLA_HEADS, HEAD_PAD, VDIM, True)
        o = pipe.run(_attn_fwd, "ret_fwd" + t, rq, rk, rv, RET_HEADS, RET_DK, RET_DV, False, tables=dtabs)
        mixin = _gn_gate(a, o, h, _row(P["ret_gn_g"][l]), _row(P["ret_gn_b"][l]), "gn_gate" + t)
        mix = _matmul(mixin, w("w_out"), "mm_mix" + t)
        z1, x1f, x1b = _ln_fwd([xf, mix], [ALPHA, 1.0], _row(P["ln1_g"][l]), _row(P["ln1_b"][l]), "ln1" + t, True)
        gu = pipe.run(_matmul, "mm_gu" + t, x1b, w("w_gu"))
        act = _swiglu(gu, "swiglu" + t)
        f = pipe.run(_matmul, "mm_down" + t, act, w("w_down"))
        z2, x2f, x2b = _ln_fwd([x1f, f], [ALPHA, 1.0], _row(P["ln2_g"][l]), _row(P["ln2_b"][l]), "ln2" + t, True)
        saved.append(dict(xb=xb, h=h, qn=qn, kvn=kvn, rq=rq, rk=rk, rv=rv, qm=qm, km=km, vm=vm, a=a, lse=lse, o=o,
                          mixin=mixin, z1=z1, x1b=x1b, gu=gu, act=act, z2=z2))
        xf, xb = x2f, x2b

    dy, sqerr = _loss_head(xf, target, "loss_head")
    dP = {}
    dys, coefs = [dy], [1.0]
    for l in reversed(range(DEPTH)):
        w, sv = functools.partial(pipe.weight, l), saved[l]
        t = f"_l{l}"
        dz2, dz2b, dg, db = _ln_bwd(dys, coefs, sv["z2"], _row(P["ln2_g"][l]), "ln2_bwd" + t)
        dP[("ln2_g", l)], dP[("ln2_b", l)] = dg, db
        pipe.reduce(l, w_down=_matmul(sv["act"], dz2b, "mm_dw_down" + t, ta=True, out_dtype=BF16))
        dact = pipe.run(_matmul, "mm_dact" + t, dz2b, w("w_down"), tb=True)
        dgu = _swiglu_bwd(sv["gu"], dact, "swiglu_bwd" + t)
        pipe.reduce(l, w_gu=_matmul(sv["x1b"], dgu, "mm_dw_gu" + t, ta=True, out_dtype=BF16))
        dx1 = pipe.run(_matmul, "mm_dx1" + t, dgu, w("w_gu"), tb=True)
        dz1, dz1b, dg, db = _ln_bwd([dz2, dx1], [ALPHA, 1.0], sv["z1"], _row(P["ln1_g"][l]), "ln1_bwd" + t)
        dP[("ln1_g", l)], dP[("ln1_b", l)] = dg, db
        pipe.reduce(l, w_out=_matmul(sv["mixin"], dz1b, "mm_dw_out" + t, ta=True, out_dtype=BF16))
        dmixin = _matmul(dz1b, w("w_out"), "mm_dmixin" + t, tb=True)
        do, drg, dgg, dgb = _gn_gate_bwd(dmixin, sv["o"], sv["h"], _row(P["ret_gn_g"][l]), _row(P["ret_gn_b"][l]),
                                         "gn_gate_bwd" + t)
        dP[("ret_gn_g", l)], dP[("ret_gn_b", l)] = dgg, dgb
        drq, drk, drv = pipe.run(_attn_bwd, "ret_bwd" + t, sv["rq"], sv["rk"], sv["rv"], do, RET_HEADS, RET_DK, RET_DV,
                                 False, tables=dtabs)
        dqm, dkm, dvm = _attn_bwd(sv["qm"], sv["km"], sv["vm"], dmixin, MLA_HEADS, HEAD_PAD, VDIM, True,
                                  "mla_bwd" + t, o=sv["a"], lse=sv["lse"])
        dq, dkv, dkr = _prep2_bwd(dqm, dkm, dvm, tabs, "prep2_bwd" + t)
        g_uq = _matmul(sv["qn"], dq, "mm_dw_uq" + t, ta=True, out_dtype=BF16)
        dqn = _matmul(dq, w("w_uq"), "mm_dqn" + t, tb=True)
        g_ukv = _matmul(sv["kvn"], dkv, "mm_dw_ukv" + t, ta=True, out_dtype=BF16)
        dkvn = _matmul(dkv, w("w_ukv"), "mm_dkvn" + t, tb=True)
        dh, dqg, dkvg = _prep1_bwd(dqn, dkvn, dkr, drq, drk, drv, drg, sv["h"], tabs, _row(P["q_norm_g"][l]),
                                   _row(P["kv_norm_g"][l]), "prep1_bwd" + t)
        dP[("q_norm_g", l)], dP[("kv_norm_g", l)] = dqg, dkvg
        pipe.reduce(l, w_uq=g_uq, w_ukv=g_ukv, w_in=_matmul(sv["xb"], dh, "mm_dw_in" + t, ta=True, out_dtype=BF16))
        dxl = pipe.run(_matmul, "mm_dxl" + t, dh, w("w_in"), tb=True)
        dys, coefs = [dz1, dxl], [ALPHA, 1.0]
    grad_x, _, dg, db = _ln_bwd(dys, coefs, x, _row(P["ln_in_g"]), "ln_in_bwd")
    dP[("ln_in_g", None)], dP[("ln_in_b", None)] = dg, db
    return sqerr, grad_x, dP


INTERNAL_OF = {"w_in": ("w_in",), "w_uq": ("w_uq",), "w_ukv": ("w_ukv",), "w_out": ("w_out",),
               "w_gu": ("w_gate", "w_up"), "w_down": ("w_down",)}


def _internal_weight(name, *blocks):
    cat = lambda parts: jnp.concatenate(parts, axis=1)
    cols = lambda b: cat([b[j] for j in range(N_CHIPS)])
    b = blocks[0]
    if name in ("w_out", "w_down"):
        return b.reshape(-1, b.shape[-1])
    if name == "w_gu":
        return cat([blk[j] for blk in blocks for j in range(N_CHIPS)])
    if name == "w_in":
        return cat([b[0][:, :MLA_IN_USED], jnp.zeros((D_MODEL, MLA_IN - MLA_IN_USED), BF16), b[0][:, MLA_IN_USED:]]
                   + [b[j] for j in range(1, N_CHIPS)])
    if name == "w_uq":
        uq, hw = cols(b), NOPE + ROPE
        pad = jnp.zeros((Q_LORA, HEAD_PAD - hw), BF16)
        return cat([p for h in range(MLA_HEADS) for p in (uq[:, h * hw:(h + 1) * hw], pad)])
    ukv = cols(b)
    return cat([ukv[:, 256 * h:256 * h + NOPE] for h in range(MLA_HEADS)]
               + [ukv[:, 256 * h + NOPE:256 * h + 256] for h in range(MLA_HEADS)])


def _grad_shards(name, g):
    cat = lambda parts: jnp.concatenate(parts, axis=1)
    if name in ("w_out", "w_down"):
        return {name: g.reshape(N_CHIPS, -1, g.shape[-1])}
    if name == "w_gu":
        cg = BIG_SHARD["w_gate"][1]
        return {"w_gate": [g[:, cg * j:cg * (j + 1)] for j in range(N_CHIPS)],
                "w_up": [g[:, D_FF + cg * j:D_FF + cg * (j + 1)] for j in range(N_CHIPS)]}
    if name == "w_in":
        ci, shift = BIG_SHARD["w_in"][1], MLA_IN - MLA_IN_USED
        return {name: [cat([g[:, :MLA_IN_USED], g[:, MLA_IN:ci + shift]])]
                + [g[:, ci * j + shift:ci * (j + 1) + shift] for j in range(1, N_CHIPS)]}
    if name == "w_uq":
        cq = NOPE + ROPE
        return {name: [cat([g[:, HEAD_PAD * h:HEAD_PAD * h + cq] for h in (2 * j, 2 * j + 1)]) for j in range(N_CHIPS)]}
    return {name: [cat([g[:, o + NOPE * h:o + NOPE * (h + 1)] for h in (2 * j, 2 * j + 1) for o in (0, MLA_HEADS * NOPE)])
                   for j in range(N_CHIPS)]}


def _small_layout(P):
    out, at = {}, 0
    for n in SMALL:
        out[n] = (at, P[n].size)
        at += P[n].size
    return out, at


def _flatten_small(P, last):
    v = jnp.concatenate([P[n].reshape(-1).astype(F32) for n in SMALL] + [last.reshape(-1).astype(F32)])
    return jnp.pad(v, (0, SMALL_ROWS * FLAT_W - v.size)).reshape(SMALL_ROWS, FLAT_W)


def _place():
    return lax.axis_index("x"), lax.axis_index("y"), lax.axis_index("c")


def _other_chips(x, y):
    return [(1 - x, y), (x, 1 - y), (1 - x, 1 - y)]


def _rcopy(src, dst, ssem, rsem, dev):
    return pltpu.make_async_remote_copy(src_ref=src, dst_ref=dst, send_sem=ssem, recv_sem=rsem, device_id=dev,
                                        device_id_type=MESH)


def _comm_call(body, name, out_shape, n_in, scratch):
    many = isinstance(out_shape, (list, tuple))
    return pl.pallas_call(body, name=name, out_shape=out_shape, in_specs=[HBM] * n_in,
                          out_specs=[HBM] * len(out_shape) if many else HBM, scratch_shapes=scratch)


def _half(ref, which):
    rows = ref.shape[0] // 2
    return ref.at[pl.ds(pl.multiple_of(which * rows, 16), rows)]


def _dma_sems(n):
    return pltpu.SemaphoreType.DMA((n,))


def _allgather_side(ws):
    k = len(ws)

    def peers():
        x, y, c = _place()
        return c, 2 * x + y, (x, y, 1 - c), [(n, t, cx, cy) for n in range(k) for t, (cx, cy) in enumerate(_other_chips(x, y))]

    def outgoing(w_refs, g_refs, sems):
        ssem, rsem, _, _, ossem, orsem = sems
        c, j, sib, nt = peers()
        owns = [_rcopy(w_refs[n], g_refs[n].at[j], ossem.at[n], orsem.at[n], sib) for n in range(k)]
        sends = [_rcopy(_half(w_refs[n], c), _half(g_refs[n].at[j], c), ssem.at[3 * n + t], rsem.at[3 * n + t],
                        (cx, cy, c)) for n, t, cx, cy in nt]
        return owns, sends

    def incoming(g_refs, sems):
        ssem, rsem, fssem, frsem, _, _ = sems
        c, _, sib, nt = peers()
        landed, passed, relayed = [], [], []
        for n, t, cx, cy in nt:
            mine, other = (_half(g_refs[n].at[2 * cx + cy], h) for h in (c, 1 - c))
            landed.append(_rcopy(mine, mine, ssem.at[3 * n + t], rsem.at[3 * n + t], (cx, cy, c)))
            passed.append(_rcopy(mine, mine, fssem.at[3 * n + t], frsem.at[3 * n + t], sib))
            relayed.append(_rcopy(other, other, fssem.at[3 * n + t], frsem.at[3 * n + t], sib))
        return landed, passed, relayed

    def start(w_refs, g_refs, sems):
        owns, sends = outgoing(w_refs, g_refs, sems)
        for cp in sends + owns:
            cp.start()

    def finish(w_refs, g_refs, sems):
        owns, sends = outgoing(w_refs, g_refs, sems)
        landed, passed, relayed = incoming(g_refs, sems)
        for got, on in zip(landed, passed):
            got.wait_recv()
            on.start()
        for cp in relayed:
            cp.wait_recv()
        for cp in owns:
            cp.wait()
        for cp in sends + passed:
            cp.wait_send()

    return _Side(list(ws), [_sds((N_CHIPS,) + w.shape, w.dtype) for w in ws],
                 [_dma_sems(3 * k)] * 4 + [_dma_sems(k)] * 2, start, finish)


def _exchange_side(parts):
    k = len(parts)

    def copies(p_refs, rcv_refs, sems):
        ssem, rsem = sems
        x, y, c = _place()
        return [_rcopy(p_refs[n].at[2 * cx + cy], rcv_refs[n].at[t], ssem.at[3 * n + t], rsem.at[3 * n + t], (cx, cy, c))
                for n in range(k) for t, (cx, cy) in enumerate(_other_chips(x, y))]

    def start(p_refs, rcv_refs, sems):
        for cp in copies(p_refs, rcv_refs, sems):
            cp.start()

    def finish(p_refs, rcv_refs, sems):
        for cp in copies(p_refs, rcv_refs, sems):
            cp.wait()

    return _Side(list(parts), [_sds((3,) + p.shape[1:], p.dtype) for p in parts], [_dma_sems(3 * k)] * 2, start, finish)


def _run_side(side, name):
    k_in, k_out = len(side.arrays), len(side.out_shape)

    def body(*refs):
        parts = refs[:k_in], refs[k_in:k_in + k_out], refs[k_in + k_out:]
        side.start(*parts)
        side.finish(*parts)

    return _comm_call(body, name, list(side.out_shape), k_in, list(side.scratch))(*side.arrays)


def _swap_halves(gds, name):
    k = len(gds)

    def body(*refs):
        gd_refs, out_refs, (ssem, rsem) = refs[:k], refs[k:2 * k], refs[2 * k:]
        x, y, c = _place()
        cps = [_rcopy(_half(gd_refs[n].at[jj], 1 - c), out_refs[n].at[jj], ssem.at[N_CHIPS * n + jj],
                      rsem.at[N_CHIPS * n + jj], (x, y, 1 - c)) for n in range(k) for jj in range(N_CHIPS)]
        for cp in cps:
            cp.start()
        for cp in cps:
            cp.wait()

    sems = _dma_sems(N_CHIPS * k)
    return _comm_call(body, name, [_sds((N_CHIPS, g.shape[1] // 2, g.shape[2]), g.dtype) for g in gds], k,
                      [sems, sems])(*gds)


def _allreduce_small(small):
    def body(s_ref, all_ref, sssem, srsem, lsem):
        x, y, c = _place()
        me = 4 * x + 2 * y + c
        own = pltpu.make_async_copy(s_ref, all_ref.at[me], lsem)
        own.start()
        cps = []
        for r in range(1, 8):
            fx, fy, fc = (r >> 2) & 1, (r >> 1) & 1, r & 1
            px, py, pc = (1 - x if fx else x, 1 - y if fy else y, 1 - c if fc else c)
            peer = 4 * px + 2 * py + pc
            send = _rcopy(s_ref, all_ref.at[me], sssem.at[r - 1], srsem.at[me], (px, py, pc))
            send.start()
            cps.append((send, _rcopy(s_ref, all_ref.at[peer], sssem.at[r - 1], srsem.at[peer], (px, py, pc))))
        for send, recv in cps:
            send.wait_send()
            recv.wait_recv()
        own.wait()

    return _comm_call(body, "allreduce_small", [_sds((8,) + small.shape, small.dtype)], 1,
                      [pltpu.SemaphoreType.DMA((7,)), pltpu.SemaphoreType.DMA((8,)), pltpu.SemaphoreType.DMA(())])(small)[0]


def _share_halves(reds, name):
    k = len(reds)

    def body(*refs):
        r_refs, out_refs, (ssem, rsem) = refs[:k], refs[k:2 * k], refs[2 * k:]
        x, y, c = _place()
        cps = [_rcopy(r_refs[n], out_refs[n], ssem.at[n], rsem.at[n], (x, y, 1 - c)) for n in range(k)]
        for cp in cps:
            cp.start()
        for cp in cps:
            cp.wait()

    return _comm_call(body, name, [_sds(r.shape, r.dtype) for r in reds], k, [_dma_sems(k)] * 2)(*reds)


def _add_pair(gd, got, c, name):
    _, R, W = got.shape
    tm = _pick(R, (512, 256, 128, 64))
    nb = R // tm

    def body(c_ref, a_ref, b_ref, o_ref):
        o_ref[...] = (a_ref[...].astype(F32) + b_ref[...].astype(F32)).astype(o_ref.dtype)

    grid_spec = pltpu.PrefetchScalarGridSpec(
        num_scalar_prefetch=1, grid=(N_CHIPS, nb),
        in_specs=[pl.BlockSpec((None, tm, W), lambda j, i, c_ref: (j, c_ref[0] * nb + i, 0)),
                  pl.BlockSpec((None, tm, W), lambda j, i, c_ref: (j, i, 0))],
        out_specs=pl.BlockSpec((None, tm, W), lambda j, i, c_ref: (j, i, 0)))
    return pl.pallas_call(body, name=name, grid_spec=grid_spec, out_shape=_sds((N_CHIPS, R, W), gd.dtype),
                          compiler_params=pltpu.CompilerParams(dimension_semantics=("parallel", "parallel"),
                                                               vmem_limit_bytes=VMEM_LIMIT))(c, gd, got)


def _add_chips(part, rcv, j, name):
    _, R, W = part.shape
    tm = _pick(R, (512, 256, 128, 64))

    def body(j_ref, p_ref, r0_ref, r1_ref, r2_ref, o_ref):
        o_ref[...] = ((p_ref[...].astype(F32) + r0_ref[...].astype(F32)) + r1_ref[...].astype(F32)) + r2_ref[...].astype(F32)

    def slot(t):
        return pl.BlockSpec((None, tm, W), lambda i, j_ref: (t, i, 0))

    grid_spec = pltpu.PrefetchScalarGridSpec(
        num_scalar_prefetch=1, grid=(R // tm,),
        in_specs=[pl.BlockSpec((None, tm, W), lambda i, j_ref: (j_ref[0], i, 0)), slot(0), slot(1), slot(2)],
        out_specs=pl.BlockSpec((tm, W), lambda i, j_ref: (i, 0)))
    return pl.pallas_call(body, name=name, grid_spec=grid_spec, out_shape=_sds((R, W), F32),
                          compiler_params=pltpu.CompilerParams(dimension_semantics=("parallel",),
                                                               vmem_limit_bytes=VMEM_LIMIT))(j, part, rcv, rcv, rcv)


def _sum_small(allsmall):
    _, R, W = allsmall.shape

    def body(a_ref, o_ref):
        acc = a_ref[0]
        for d in range(1, 8):
            acc = acc + a_ref[d]
        o_ref[...] = acc

    return _call(body, "sum_small", _sds((R, W), F32), (1,), [_whole((8, R, W))], _whole((R, W)),
                 sem=("arbitrary",))(allsmall)


def _adamw(w, g, m, v, name):
    R, C = w.shape
    tm = _pick(R, (256, 128, 64, 32, 8))

    def body(w_ref, g_ref, m_ref, v_ref, d_ref, mo_ref, vo_ref):
        gv = g_ref[...]
        mn = ADAM_B1 * m_ref[...] + (1.0 - ADAM_B1) * gv
        vn = ADAM_B2 * v_ref[...] + (1.0 - ADAM_B2) * (gv * gv)
        m_hat = mn / (1.0 - ADAM_B1 ** ADAM_STEP)
        v_hat = vn / (1.0 - ADAM_B2 ** ADAM_STEP)
        d_ref[...] = -ADAM_LR * (m_hat / (jnp.sqrt(v_hat) + ADAM_EPS) + ADAM_WD * w_ref[...])
        mo_ref[...] = mn
        vo_ref[...] = vn

    spec = _rows(tm, C)
    return _call(body, name, [_sds((R, C), F32)] * 3, (R // tm,), [spec] * 4, [spec] * 3, sem=("parallel",))(w, g, m, v)


def _adamw_layer(c, w, m, v, mine, other, l, prev, name):
    _, R, C = w.shape
    half = R // 2
    tm = _pick(half, (256, 128, 64))
    nbh = half // tm

    def body(c_ref, w_ref, m_ref, v_ref, a_ref, b_ref, *rest):
        g_ref, d_ref, mo_ref, vo_ref = rest[-4:]
        gv = jnp.where(pl.program_id(0) // nbh == c_ref[0], a_ref[...], b_ref[...])
        mn = ADAM_B1 * m_ref[...] + (1.0 - ADAM_B1) * gv
        vn = ADAM_B2 * v_ref[...] + (1.0 - ADAM_B2) * (gv * gv)
        m_hat = mn / (1.0 - ADAM_B1 ** ADAM_STEP)
        v_hat = vn / (1.0 - ADAM_B2 ** ADAM_STEP)
        g_ref[...] = gv
        d_ref[...] = -ADAM_LR * (m_hat / (jnp.sqrt(v_hat) + ADAM_EPS) + ADAM_WD * w_ref[...])
        mo_ref[...] = mn
        vo_ref[...] = vn

    layer = pl.BlockSpec((None, tm, C), lambda i, c_ref: (l, i, 0))
    halfspec = pl.BlockSpec((tm, C), lambda i, c_ref: (i % nbh, 0))
    n_prev = 0 if prev is None else 4
    grid_spec = pltpu.PrefetchScalarGridSpec(
        num_scalar_prefetch=1, grid=(R // tm,),
        in_specs=[layer] * 3 + [halfspec] * 2 + [pl.BlockSpec(memory_space=pl.ANY)] * n_prev,
        out_specs=[layer] * 4)
    return pl.pallas_call(body, name=name, grid_spec=grid_spec, out_shape=[_sds(w.shape, F32)] * 4,
                          input_output_aliases={6 + k: k for k in range(n_prev)},
                          compiler_params=pltpu.CompilerParams(dimension_semantics=("parallel",),
                                                               vmem_limit_bytes=VMEM_LIMIT))(
        c, w, m, v, mine, other, *(prev or ()))


FIRST_GATHER = ("w_in", "w_uq", "w_ukv")
GATHER_IN = {
    "mm_h_l0": (0, ("w_out",)), "mla_fwd_l0": (0, ("w_gate", "w_up")), "ret_fwd_l0": (0, ("w_down",)),
    "mm_gu_l0": (1, ("w_in", "w_uq", "w_ukv", "w_out")), "mm_down_l0": (1, ("w_gate",)),
    "mm_h_l1": (1, ("w_up",)), "mla_fwd_l1": (1, ("w_down",))}
EXCHANGE_IN = {
    "mm_dact": ("w_down",), "mm_dx1": ("w_gate", "w_up"), "ret_bwd": ("w_out",), "mm_dxl": ("w_uq", "w_ukv", "w_in")}


class _Pipeline:
    def __init__(self, own, Wt, Mo, Vo, core, chip):
        self.own, self.Wt, self.Mo, self.Vo, self.core, self.chip = own, Wt, Mo, Vo, core, chip
        self.blocks, self.whole, self.parts = {}, {}, {}
        self.results = {n: None for n in BIG}

    def _gathered(self, l, names, blocks):
        for n, b in zip(names, blocks):
            self.blocks[(l, n)] = b

    def gather_first(self):
        side = _allgather_side([self.own[0][n] for n in FIRST_GATHER])
        self._gathered(0, FIRST_GATHER, _run_side(side, "allgather_first"))

    def weight(self, l, name):
        if (l, name) not in self.whole:
            self.whole[(l, name)] = _internal_weight(name, *[self.blocks[(l, n)] for n in INTERNAL_OF[name]])
        return self.whole[(l, name)]

    def run(self, fn, name, *args, **kw):
        base, l = name[:-3], int(name[-1])
        if name in GATHER_IN:
            gl, names = GATHER_IN[name]
            out, blocks = fn(*args, name=name, side=_allgather_side([self.own[gl][n] for n in names]), **kw)
            self._gathered(gl, names, blocks)
            return out
        if base in EXCHANGE_IN:
            names = EXCHANGE_IN[base]
            out, rcvs = fn(*args, name=name, side=_exchange_side([self.parts[(l, n)] for n in names]), **kw)
            self._reduced(l, names, rcvs)
            return out
        return fn(*args, name=name, **kw)

    def reduce(self, l, **grads):
        shards = {}
        for name, g in grads.items():
            shards.update(_grad_shards(name, g))
        names = list(shards)
        gds = [shards[n] if hasattr(shards[n], "shape") else jnp.stack(shards[n]) for n in names]
        got = _swap_halves(gds, f"swap_halves_{names[0]}_l{l}")
        for n, gd, gt in zip(names, gds, got):
            self.parts[(l, n)] = _add_pair(gd, gt, self.core, f"add_pair_{n}_l{l}")

    def _reduced(self, l, names, rcvs):
        reds = [_add_chips(self.parts[(l, n)], rcv, self.chip, f"add_chips_{n}_l{l}") for n, rcv in zip(names, rcvs)]
        others = _share_halves(reds, f"share_halves_{names[0]}_l{l}")
        for n, red, other in zip(names, reds, others):
            self.results[n] = _adamw_layer(self.core, self.Wt[n], self.Mo[n], self.Vo[n], red, other, l,
                                           self.results[n], f"adamw_{n}_l{l}")


def kernel(x, positions, ln_in_g, ln_in_b, w_in, q_norm_g, kv_norm_g, w_uq, w_ukv, ret_gn_g, ret_gn_b, w_out, ln1_g, ln1_b, w_gate, w_up, w_down, ln2_g, ln2_b, loss_target, m_ln_in_g, m_ln_in_b, m_w_in, m_q_norm_g, m_kv_norm_g, m_w_uq, m_w_ukv, m_ret_gn_g, m_ret_gn_b, m_w_out, m_ln1_g, m_ln1_b, m_w_gate, m_w_up, m_w_down, m_ln2_g, m_ln2_b, v_ln_in_g, v_ln_in_b, v_w_in, v_q_norm_g, v_kv_norm_g, v_w_uq, v_w_ukv, v_ret_gn_g, v_ret_gn_b, v_w_out, v_ln1_g, v_ln1_b, v_w_gate, v_w_up, v_w_down, v_ln2_g, v_ln2_b):
    given = dict(locals())
    Wt = {n: given[n] for n in WEIGHTS}
    Mo = {n: given["m_" + n] for n in WEIGHTS}
    Vo = {n: given["v_" + n] for n in WEIGHTS}
    cx, cy, cc = _place()
    chip = (2 * cx + cy).astype(jnp.int32)
    core = cc.astype(jnp.int32)

    own = [{n: Wt[n][l].astype(BF16) for n in BIG} for l in range(DEPTH)]
    pipe = _Pipeline(own, Wt, Mo, Vo, core.reshape(1), chip.reshape(1))
    sqerr, grad_x, dP = _local_step(x[0], positions[0], loss_target[0], pipe, Wt)
    results = pipe.results

    small_g = {n: (dP[(n, None)] if Wt[n].ndim == 1 else jnp.stack([dP[(n, l)] for l in range(DEPTH)])) for n in SMALL}
    local_loss = 0.5 * jnp.sum(sqerr) / D_MODEL
    small_sum = _sum_small(_allreduce_small(_flatten_small(small_g, local_loss))).reshape(-1)
    layout, n_small = _small_layout(Wt)
    loss = small_sum[n_small]

    grads, deltas, new_m, new_v = {}, {}, {}, {}
    for n in BIG:
        grads[n], deltas[n], new_m[n], new_v[n] = results[n]
    zero = jnp.zeros((), F32)
    d, mn, vn = _adamw(_flatten_small(Wt, zero), small_sum.reshape(SMALL_ROWS, FLAT_W), _flatten_small(Mo, zero),
                       _flatten_small(Vo, zero), "adamw_small")
    for n in SMALL:
        at, size = layout[n]
        pick = lambda a: a.reshape(-1)[at:at + size].reshape(Wt[n].shape)
        grads[n], deltas[n], new_m[n], new_v[n] = pick(small_sum), pick(d), pick(mn), pick(vn)

    return (loss, grad_x[None], *[grads[n] for n in WEIGHTS], *[deltas[n] for n in WEIGHTS],
            *[new_m[n] for n in WEIGHTS], *[new_v[n] for n in WEIGHTS])
```

```python
import functools

import jax
import jax.numpy as jnp
from jax import lax
from jax.experimental import pallas as pl
from jax.experimental.pallas import tpu as pltpu

F32 = jnp.float32
BF16 = jnp.bfloat16

D_MODEL = 2048
DEPTH = 2
CHUNK = 64
MLA_HEADS = 8
Q_LORA = 512
KV_LORA = 256
NOPE = 128
ROPE = 64
VDIM = 128
RET_HEADS = 4
RET_DK = 256
RET_DV = 256
D_FF = 5632
D_IN = 4928
ROPE_THETA = 10000.0
LN_EPS = 1e-5
RMS_EPS = 1e-6
GN_EPS = 1e-5
ALPHA = (2 * DEPTH) ** 0.25
MLA_SCALE = (NOPE + ROPE) ** -0.5
RET_SCALE = RET_DK ** -0.5
ADAM_LR = 0.001
ADAM_B1 = 0.9
ADAM_B2 = 0.999
ADAM_EPS = 1e-08
ADAM_WD = 0.01
ADAM_STEP = 10

LANES = 128
HEAD_PAD = 256
MLA_IN = 1024
MLA_IN_USED = Q_LORA + KV_LORA + ROPE
D_IN_PAD = MLA_IN + 4 * 1024
ATT_BLOCK = 512
NEG = -1e30
VMEM_LIMIT = 56 * 1024 * 1024

N_CHIPS = 4
FLAT_W = 1024
BIG = ("w_in", "w_uq", "w_ukv", "w_out", "w_gate", "w_up", "w_down")
BIG_SHARD = {"w_in": (2048, 1232), "w_uq": (512, 384), "w_ukv": (256, 512), "w_out": (512, 2048),
             "w_gate": (2048, 1408), "w_up": (2048, 1408), "w_down": (1408, 2048)}
SMALL = ("ln_in_g", "ln_in_b", "q_norm_g", "kv_norm_g", "ret_gn_g", "ret_gn_b", "ln1_g", "ln1_b", "ln2_g", "ln2_b")
WEIGHTS = ("ln_in_g", "ln_in_b", "w_in", "q_norm_g", "kv_norm_g", "w_uq", "w_ukv", "ret_gn_g", "ret_gn_b", "w_out",
           "ln1_g", "ln1_b", "w_gate", "w_up", "w_down", "ln2_g", "ln2_b")
SMALL_ROWS = 32

MESH = pl.DeviceIdType.MESH


def _pick(dim, cands):
    for c in cands:
        if dim % c == 0:
            return c
    return dim


HBM = pl.BlockSpec(memory_space=pltpu.HBM)


class _Side:
    def __init__(self, arrays, out_shape, scratch, start, finish):
        self.arrays, self.out_shape, self.scratch, self.start, self.finish = arrays, out_shape, scratch, start, finish


def _call(body, name, out_shape, grid, in_specs, out_specs, scratch=(), sem=None, side=None):
    params = pltpu.CompilerParams(dimension_semantics=sem if side is None else ("arbitrary",) * len(grid),
                                  vmem_limit_bytes=VMEM_LIMIT)
    if side is None:
        return pl.pallas_call(body, name=name, out_shape=out_shape, grid=grid, in_specs=in_specs, out_specs=out_specs,
                              scratch_shapes=list(scratch), compiler_params=params)
    single = not isinstance(out_shape, (list, tuple))
    outs = [out_shape] if single else list(out_shape)
    ospecs = [out_specs] if single else list(out_specs)
    cuts = [len(in_specs), len(side.arrays), len(outs), len(side.out_shape), len(scratch)]
    ends = [sum(cuts[:k + 1]) for k in range(len(cuts))]

    def hosted(*refs):
        ins, s_in, o, s_out, scr = (refs[a:b] for a, b in zip([0] + ends[:-1], ends))
        sems = refs[ends[-1]:]
        ids = [pl.program_id(a) for a in range(len(grid))]
        first = functools.reduce(jnp.logical_and, [i == 0 for i in ids])
        last = functools.reduce(jnp.logical_and, [i == g - 1 for i, g in zip(ids, grid)])

        @pl.when(first)
        def _():
            side.start(s_in, s_out, sems)

        body(*ins, *o, *scr)

        @pl.when(last)
        def _():
            side.finish(s_in, s_out, sems)

    call = pl.pallas_call(hosted, name=name, out_shape=outs + list(side.out_shape), grid=grid,
                          in_specs=list(in_specs) + [HBM] * len(side.arrays),
                          out_specs=ospecs + [HBM] * len(side.out_shape),
                          scratch_shapes=list(scratch) + list(side.scratch), compiler_params=params)

    def run(*args):
        res = call(*args, *side.arrays)
        return (res[0] if single else list(res[:len(outs)])), list(res[len(outs):])

    return run


def _rows(tm, w, col=0):
    return pl.BlockSpec((tm, w), lambda i: (i, col))


def _whole(shape):
    return pl.BlockSpec(shape, lambda i: (0,) * len(shape))


def _sds(shape, dtype):
    return jax.ShapeDtypeStruct(shape, dtype)


def _matmul(a, b, name, ta=False, tb=False, out_dtype=F32, side=None):
    (K, M) = a.shape if ta else a.shape[::-1]
    (N, Kb) = b.shape if tb else b.shape[::-1]
    assert K == Kb, (a.shape, b.shape, ta, tb)
    tm = _pick(M, (1024, 512, 256, 128))
    tn = _pick(N, (1024, 512, 256, 128))
    tk = _pick(K, (2816, 2560, 2048, 1024, 512, 256))
    nk = K // tk
    dn = (((0 if ta else 1,), (1 if tb else 0,)), ((), ()))

    def body(a_ref, b_ref, o_ref, acc_ref):
        k = pl.program_id(2)
        if nk == 1:
            o_ref[...] = lax.dot_general(a_ref[...].astype(BF16), b_ref[...].astype(BF16), dn,
                                         preferred_element_type=F32).astype(out_dtype)
        else:
            @pl.when(k == 0)
            def _():
                acc_ref[...] = jnp.zeros_like(acc_ref)

            acc_ref[...] += lax.dot_general(a_ref[...].astype(BF16), b_ref[...].astype(BF16), dn,
                                            preferred_element_type=F32)

            @pl.when(k == nk - 1)
            def _():
                o_ref[...] = acc_ref[...].astype(out_dtype)

    a_spec = pl.BlockSpec((tk, tm), lambda i, j, k: (k, i)) if ta else pl.BlockSpec((tm, tk), lambda i, j, k: (i, k))
    b_spec = pl.BlockSpec((tn, tk), lambda i, j, k: (j, k)) if tb else pl.BlockSpec((tk, tn), lambda i, j, k: (k, j))
    return _call(body, name, _sds((M, N), out_dtype), (M // tm, N // tn, nk), [a_spec, b_spec],
                 pl.BlockSpec((tm, tn), lambda i, j, k: (i, j)), scratch=[pltpu.VMEM((tm, tn), F32)],
                 sem=("parallel", "parallel", "arbitrary"), side=side)(a, b)


def _sigmoid(x):
    return 1.0 / (1.0 + jnp.exp(-x))


def _rope_group(r, c, sa, sb):
    return r * c + pltpu.roll(r, 32, 1) * sa + pltpu.roll(r, 96, 1) * sb


def _ln_fwd(xs, coefs, g, b, name, want_z):
    S, D = xs[0].shape
    tm = 256
    n = len(xs)

    def body(*refs):
        x_refs, g_ref, b_ref, outs = refs[:n], refs[n], refs[n + 1], refs[n + 2:]
        z = None
        for cf, r in zip(coefs, x_refs):
            t = r[...] if cf == 1.0 else cf * r[...]
            z = t if z is None else z + t
        mu = jnp.mean(z, axis=-1, keepdims=True)
        zc = z - mu
        var = jnp.mean(zc * zc, axis=-1, keepdims=True)
        y = zc * lax.rsqrt(var + LN_EPS) * g_ref[...] + b_ref[...]
        if want_z:
            outs[0][...] = z
        outs[-2][...] = y
        outs[-1][...] = y.astype(BF16)

    out_shape = [_sds((S, D), F32)] * (2 if want_z else 1) + [_sds((S, D), BF16)]
    return _call(body, name, out_shape, (S // tm,), [_rows(tm, D)] * n + [_whole((1, D))] * 2,
                 [_rows(tm, D)] * len(out_shape), sem=("parallel",))(*xs, g, b)


def _ln_bwd(dys, coefs, z, g, name):
    S, D = z.shape
    tm = 256
    n = len(dys)

    def body(*refs):
        dy_refs, z_ref, g_ref = refs[:n], refs[n], refs[n + 1]
        dz_ref, dzb_ref, dg_ref, db_ref = refs[n + 2:]
        dy = None
        for cf, r in zip(coefs, dy_refs):
            t = r[...] if cf == 1.0 else cf * r[...]
            dy = t if dy is None else dy + t
        zv = z_ref[...]
        mu = jnp.mean(zv, axis=-1, keepdims=True)
        zc = zv - mu
        var = jnp.mean(zc * zc, axis=-1, keepdims=True)
        rstd = lax.rsqrt(var + LN_EPS)
        xh = zc * rstd
        dyg = dy * g_ref[...]
        dz = rstd * (dyg - jnp.mean(dyg, axis=-1, keepdims=True) - xh * jnp.mean(dyg * xh, axis=-1, keepdims=True))
        dz_ref[...] = dz
        dzb_ref[...] = dz.astype(BF16)

        @pl.when(pl.program_id(0) == 0)
        def _():
            dg_ref[...] = jnp.zeros_like(dg_ref)
            db_ref[...] = jnp.zeros_like(db_ref)

        dg_ref[...] += jnp.sum(dy * xh, axis=0, keepdims=True)
        db_ref[...] += jnp.sum(dy, axis=0, keepdims=True)

    return _call(body, name, [_sds((S, D), F32), _sds((S, D), BF16), _sds((1, D), F32), _sds((1, D), F32)],
                 (S // tm,), [_rows(tm, D)] * (n + 1) + [_whole((1, D))],
                 [_rows(tm, D), _rows(tm, D), _whole((1, D)), _whole((1, D))], sem=("arbitrary",))(*dys, z, g)


def _rms(x, g):
    return x * lax.rsqrt(jnp.mean(x * x, axis=-1, keepdims=True) + RMS_EPS) * g


def _prep1(h, tabs, qg, kvg, name):
    S = h.shape[0]
    tm = 256
    cm, sam, sbm, cr, sr = tabs

    def body(h_ref, cm_ref, sam_ref, sbm_ref, cr_ref, sr_ref, qg_ref, kvg_ref,
             qn_ref, kvn_ref, kr_ref, rq_ref, rk_ref, rv_ref):
        qn_ref[...] = _rms(h_ref[:, 0:Q_LORA], qg_ref[...]).astype(BF16)
        kvn_ref[...] = _rms(h_ref[:, Q_LORA:Q_LORA + KV_LORA], kvg_ref[...]).astype(BF16)
        kr_ref[...] = _rope_group(h_ref[:, 768:896], cm_ref[...], sam_ref[...], sbm_ref[...])
        c, s = cr_ref[...], sr_ref[...]
        for hd in range(RET_HEADS):
            for src, dst, scale in ((MLA_IN, rq_ref, RET_SCALE), (MLA_IN + 1024, rk_ref, None)):
                t1 = h_ref[:, src + hd * 256:src + hd * 256 + 128]
                t2 = h_ref[:, src + hd * 256 + 128:src + hd * 256 + 256]
                o1, o2 = t1 * c - t2 * s, t2 * c + t1 * s
                if scale is not None:
                    o1, o2 = o1 * scale, o2 * scale
                dst[:, hd * 256:hd * 256 + 128] = o1.astype(BF16)
                dst[:, hd * 256 + 128:hd * 256 + 256] = o2.astype(BF16)
        rv_ref[...] = h_ref[:, MLA_IN + 2048:MLA_IN + 3072].astype(BF16)

    t128 = _rows(tm, LANES)
    return _call(body, name,
                 [_sds((S, Q_LORA), BF16), _sds((S, KV_LORA), BF16), _sds((S, LANES), F32),
                  _sds((S, 1024), BF16), _sds((S, 1024), BF16), _sds((S, 1024), BF16)],
                 (S // tm,),
                 [_rows(tm, D_IN_PAD), t128, t128, t128, t128, t128, _whole((1, Q_LORA)), _whole((1, KV_LORA))],
                 [_rows(tm, Q_LORA), _rows(tm, KV_LORA), t128, _rows(tm, 1024), _rows(tm, 1024), _rows(tm, 1024)],
                 sem=("parallel",))(h, cm, sam, sbm, cr, sr, qg, kvg)


def _prep1_bwd(dqn, dkvn, dkr, drq, drk, drv, drg, h, tabs, qg, kvg, name):
    S = h.shape[0]
    tm = 256
    cm, sam, sbm, cr, sr = tabs

    def rms_bwd(x, g, dy):
        r = lax.rsqrt(jnp.mean(x * x, axis=-1, keepdims=True) + RMS_EPS)
        dyg = dy * g
        dx = r * dyg - x * (r * r * r) * jnp.mean(dyg * x, axis=-1, keepdims=True)
        return dx, jnp.sum(dy * x * r, axis=0, keepdims=True)

    def body(dqn_ref, dkvn_ref, dkr_ref, drq_ref, drk_ref, drv_ref, drg_ref, h_ref,
             cm_ref, sam_ref, sbm_ref, cr_ref, sr_ref, qg_ref, kvg_ref, dh_ref, dqg_ref, dkvg_ref):
        dcq, dqg = rms_bwd(h_ref[:, 0:Q_LORA], qg_ref[...], dqn_ref[...])
        dckv, dkvg = rms_bwd(h_ref[:, Q_LORA:Q_LORA + KV_LORA], kvg_ref[...], dkvn_ref[...])
        dh_ref[:, 0:Q_LORA] = dcq.astype(BF16)
        dh_ref[:, Q_LORA:Q_LORA + KV_LORA] = dckv.astype(BF16)
        dh_ref[:, 768:896] = _rope_group(dkr_ref[...], cm_ref[...], -sam_ref[...], -sbm_ref[...]).astype(BF16)
        dh_ref[:, 896:1024] = jnp.zeros((tm, LANES), BF16)
        c, s = cr_ref[...], sr_ref[...]
        for hd in range(RET_HEADS):
            for src, dst, scale in ((drq_ref, MLA_IN, RET_SCALE), (drk_ref, MLA_IN + 1024, None)):
                d1 = src[:, hd * 256:hd * 256 + 128]
                d2 = src[:, hd * 256 + 128:hd * 256 + 256]
                if scale is not None:
                    d1, d2 = d1 * scale, d2 * scale
                dh_ref[:, dst + hd * 256:dst + hd * 256 + 128] = (d1 * c + d2 * s).astype(BF16)
                dh_ref[:, dst + hd * 256 + 128:dst + hd * 256 + 256] = (d2 * c - d1 * s).astype(BF16)
        dh_ref[:, MLA_IN + 2048:MLA_IN + 3072] = drv_ref[...].astype(BF16)
        dh_ref[:, MLA_IN + 3072:MLA_IN + 4096] = drg_ref[...].astype(BF16)

        @pl.when(pl.program_id(0) == 0)
        def _():
            dqg_ref[...] = jnp.zeros_like(dqg_ref)
            dkvg_ref[...] = jnp.zeros_like(dkvg_ref)

        dqg_ref[...] += dqg
        dkvg_ref[...] += dkvg

    t128 = _rows(tm, LANES)
    return _call(body, name,
                 [_sds((S, D_IN_PAD), BF16), _sds((1, Q_LORA), F32), _sds((1, KV_LORA), F32)],
                 (S // tm,),
                 [_rows(tm, Q_LORA), _rows(tm, KV_LORA), t128, _rows(tm, 1024), _rows(tm, 1024), _rows(tm, 1024),
                  _rows(tm, 1024), _rows(tm, D_IN_PAD), t128, t128, t128, t128, t128,
                  _whole((1, Q_LORA)), _whole((1, KV_LORA))],
                 [_rows(tm, D_IN_PAD), _whole((1, Q_LORA)), _whole((1, KV_LORA))],
                 sem=("arbitrary",))(dqn, dkvn, dkr, drq, drk, drv, drg, h, cm, sam, sbm, cr, sr, qg, kvg)


def _prep2(q, kv, kr, tabs, name):
    S = q.shape[0]
    tm = 256
    cm, sam, sbm = tabs[:3]

    def body(q_ref, kv_ref, kr_ref, cm_ref, sam_ref, sbm_ref, qo_ref, ko_ref, vo_ref):
        c, sa, sb = cm_ref[...], sam_ref[...], sbm_ref[...]
        krb = kr_ref[...].astype(BF16)
        for hd in range(MLA_HEADS):
            o = hd * HEAD_PAD
            qo_ref[:, o:o + 128] = q_ref[:, o:o + 128].astype(BF16)
            qo_ref[:, o + 128:o + 256] = _rope_group(q_ref[:, o + 128:o + 256], c, sa, sb).astype(BF16)
            ko_ref[:, o:o + 128] = kv_ref[:, hd * 128:hd * 128 + 128].astype(BF16)
            ko_ref[:, o + 128:o + 256] = krb
        vo_ref[...] = kv_ref[:, 1024:2048].astype(BF16)

    t128 = _rows(tm, LANES)
    return _call(body, name, [_sds((S, 2048), BF16), _sds((S, 2048), BF16), _sds((S, 1024), BF16)], (S // tm,),
                 [_rows(tm, 2048), _rows(tm, 2048), t128, t128, t128, t128],
                 [_rows(tm, 2048), _rows(tm, 2048), _rows(tm, 1024)], sem=("parallel",))(q, kv, kr, cm, sam, sbm)


def _prep2_bwd(dqm, dkm, dvm, tabs, name):
    S = dqm.shape[0]
    tm = 256
    cm, sam, sbm = tabs[:3]

    def body(dq_ref, dk_ref, dv_ref, cm_ref, sam_ref, sbm_ref, dqo_ref, dkvo_ref, dkr_ref):
        c, sa, sb = cm_ref[...], -sam_ref[...], -sbm_ref[...]
        dkr = None
        for hd in range(MLA_HEADS):
            o = hd * HEAD_PAD
            dqo_ref[:, o:o + 128] = dq_ref[:, o:o + 128].astype(BF16)
            dqo_ref[:, o + 128:o + 256] = _rope_group(dq_ref[:, o + 128:o + 256], c, sa, sb).astype(BF16)
            dkvo_ref[:, hd * 128:hd * 128 + 128] = dk_ref[:, o:o + 128].astype(BF16)
            t = dk_ref[:, o + 128:o + 256]
            dkr = t if dkr is None else dkr + t
        dkvo_ref[:, 1024:2048] = dv_ref[...].astype(BF16)
        dkr_ref[...] = dkr

    t128 = _rows(tm, LANES)
    return _call(body, name, [_sds((S, 2048), BF16), _sds((S, 2048), BF16), _sds((S, LANES), F32)], (S // tm,),
                 [_rows(tm, 2048), _rows(tm, 2048), _rows(tm, 1024), t128, t128, t128],
                 [_rows(tm, 2048), _rows(tm, 2048), t128], sem=("parallel",))(dqm, dkm, dvm, cm, sam, sbm)


def _gn_gate(a, o, h, gg, gb, name):
    S = a.shape[0]
    tm = 256

    def body(a_ref, o_ref, rg_ref, gg_ref, gb_ref, mix_ref):
        mix_ref[:, 0:1024] = a_ref[...].astype(BF16)
        for hd in range(RET_HEADS):
            sl = slice(hd * 256, hd * 256 + 256)
            ov = o_ref[:, sl]
            mu = jnp.mean(ov, axis=-1, keepdims=True)
            oc = ov - mu
            var = jnp.mean(oc * oc, axis=-1, keepdims=True)
            y = oc * lax.rsqrt(var + GN_EPS) * gg_ref[:, sl] + gb_ref[:, sl]
            rg = rg_ref[:, sl]
            mix_ref[:, 1024 + hd * 256:1024 + hd * 256 + 256] = (rg * _sigmoid(rg) * y).astype(BF16)

    return _call(body, name, _sds((S, 2048), BF16), (S // tm,),
                 [_rows(tm, 1024), _rows(tm, 1024), _rows(tm, 1024, 4), _whole((1, 1024)), _whole((1, 1024))],
                 _rows(tm, 2048), sem=("parallel",))(a, o, h, gg, gb)


def _gn_gate_bwd(dmixin, o, h, gg, gb, name):
    S = o.shape[0]
    tm = 256

    def body(dr_ref, o_ref, rg_ref, gg_ref, gb_ref, do_ref, drg_ref, dgg_ref, dgb_ref):
        @pl.when(pl.program_id(0) == 0)
        def _():
            dgg_ref[...] = jnp.zeros_like(dgg_ref)
            dgb_ref[...] = jnp.zeros_like(dgb_ref)

        for hd in range(RET_HEADS):
            sl = slice(hd * 256, hd * 256 + 256)
            ov = o_ref[:, sl]
            mu = jnp.mean(ov, axis=-1, keepdims=True)
            oc = ov - mu
            var = jnp.mean(oc * oc, axis=-1, keepdims=True)
            rstd = lax.rsqrt(var + GN_EPS)
            xh = oc * rstd
            g = gg_ref[:, sl]
            y = xh * g + gb_ref[:, sl]
            rg = rg_ref[:, sl]
            sg = _sigmoid(rg)
            dr = dr_ref[:, sl]
            dy = dr * (rg * sg)
            drg_ref[:, sl] = dr * y * (sg * (1.0 + rg * (1.0 - sg)))
            dgg_ref[:, sl] += jnp.sum(dy * xh, axis=0, keepdims=True)
            dgb_ref[:, sl] += jnp.sum(dy, axis=0, keepdims=True)
            dxh = dy * g
            do = rstd * (dxh - jnp.mean(dxh, axis=-1, keepdims=True) - xh * jnp.mean(dxh * xh, axis=-1, keepdims=True))
            do_ref[:, sl] = do.astype(BF16)

    return _call(body, name,
                 [_sds((S, 1024), BF16), _sds((S, 1024), F32), _sds((1, 1024), F32), _sds((1, 1024), F32)],
                 (S // tm,),
                 [_rows(tm, 1024, 1), _rows(tm, 1024), _rows(tm, 1024, 4), _whole((1, 1024)), _whole((1, 1024))],
                 [_rows(tm, 1024), _rows(tm, 1024), _whole((1, 1024)), _whole((1, 1024))],
                 sem=("arbitrary",))(dmixin, o, h, gg, gb)


GU_BLOCK = D_FF // N_CHIPS


def _matmul_swiglu(x, w_gu, name, side=None):
    S, K = x.shape
    tm = _pick(S, (512, 256, 128))
    tn = 2 * GU_BLOCK

    def body(x_ref, w_ref, gu_ref, act_ref):
        r = jnp.dot(x_ref[...], w_ref[...], preferred_element_type=F32)
        g, u = r[:, :GU_BLOCK], r[:, GU_BLOCK:]
        gu_ref[...] = r.astype(BF16)
        act_ref[...] = (g * _sigmoid(g) * u).astype(BF16)

    return _call(body, name, [_sds((S, 2 * D_FF), BF16), _sds((S, D_FF), BF16)], (S // tm, N_CHIPS),
                 [pl.BlockSpec((tm, K), lambda i, j: (i, 0)), pl.BlockSpec((K, tn), lambda i, j: (0, j))],
                 [pl.BlockSpec((tm, tn), lambda i, j: (i, j)), pl.BlockSpec((tm, GU_BLOCK), lambda i, j: (i, j))],
                 sem=("parallel", "parallel"), side=side)(x, w_gu)


def _swiglu_bwd(gu, dact, name):
    S = gu.shape[0]
    tm = 128

    def body(gu_ref, d_ref, o_ref):
        for j in range(N_CHIPS):
            at = 2 * GU_BLOCK * j
            g = gu_ref[:, at:at + GU_BLOCK].astype(F32)
            u = gu_ref[:, at + GU_BLOCK:at + 2 * GU_BLOCK].astype(F32)
            d = d_ref[:, GU_BLOCK * j:GU_BLOCK * (j + 1)]
            sg = _sigmoid(g)
            o_ref[:, at:at + GU_BLOCK] = (d * u * (sg * (1.0 + g * (1.0 - sg)))).astype(BF16)
            o_ref[:, at + GU_BLOCK:at + 2 * GU_BLOCK] = (d * (g * sg)).astype(BF16)

    return _call(body, name, _sds((S, 2 * D_FF), BF16), (S // tm,), [_rows(tm, 2 * D_FF), _rows(tm, D_FF)],
                 _rows(tm, 2 * D_FF), sem=("parallel",))(gu, dact)


def _loss_head(y, target, name):
    S, D = y.shape
    tm = 256

    def body(y_ref, t_ref, dy_ref, acc_ref):
        e = y_ref[...] - t_ref[...]
        dy_ref[...] = e / D

        @pl.when(pl.program_id(0) == 0)
        def _():
            acc_ref[...] = jnp.zeros_like(acc_ref)

        acc_ref[...] += jnp.sum(e * e, axis=0, keepdims=True)

    return _call(body, name, [_sds((S, D), F32), _sds((1, D), F32)], (S // tm,), [_rows(tm, D), _rows(tm, D)],
                 [_rows(tm, D), _whole((1, D))], sem=("arbitrary",))(y, target)


def _chunk_mask(T):
    r = lax.shift_right_logical(lax.broadcasted_iota(jnp.int32, (T, T), 0), 6)
    c = lax.shift_right_logical(lax.broadcasted_iota(jnp.int32, (T, T), 1), 6)
    return r >= c


def _dot_nt(a, b):
    return lax.dot_general(a, b, (((1,), (1,)), ((), ())), preferred_element_type=F32)


def _dot_tn(a, b):
    return lax.dot_general(a, b, (((0,), (0,)), ((), ())), preferred_element_type=F32)


def _decay_tables(T):
    lg = jnp.log1p(-jnp.exp2(-5.0 - jnp.arange(RET_HEADS, dtype=F32)))
    idx = jnp.arange(T, dtype=F32)
    diff = idx[:, None] - idx[None, :]
    rel = jnp.exp(lg[:, None, None] * diff[None])
    cid = jnp.arange(T) // CHUNK
    mask = (cid[:, None] >= cid[None, :]).astype(F32)
    reld = jnp.exp(lg[:, None, None] * jnp.abs(diff)[None]) * mask[None]
    lgrow = jnp.broadcast_to(lg[:, None, None], (RET_HEADS, 1, LANES))
    return lgrow, rel, reld


def _attn_fwd(q, k, v, heads, dk, dv, softmax, name, tables=None, side=None):
    S = q.shape[0]
    T = ATT_BLOCK
    nq = S // T
    rep = T // LANES

    def body(*refs):
        if softmax:
            q_ref, k_ref, v_ref, o_ref, lse_ref, m_sc, l_sc, acc_sc = refs
        else:
            q_ref, k_ref, v_ref, lg_ref, rel_ref, reld_ref, o_ref, acc_sc = refs
        i = pl.program_id(1)
        qv = q_ref[...]

        def kv_block(j):
            rows = pl.ds(pl.multiple_of(j * T, T), T)
            return k_ref[rows, :], v_ref[rows, :]

        kb, vb = kv_block(i)
        s = _dot_nt(qv, kb)
        if softmax:
            s = jnp.where(_chunk_mask(T), s * MLA_SCALE, NEG)
            m = jnp.max(s, axis=-1, keepdims=True)
            p = jnp.exp(s - m)
            m_sc[...] = jnp.broadcast_to(m, (T, LANES))
            l_sc[...] = jnp.broadcast_to(jnp.sum(p, axis=-1, keepdims=True), (T, LANES))
        else:
            p = s * reld_ref[0]
        acc_sc[...] = jnp.dot(p.astype(BF16), vb, preferred_element_type=F32)

        def scores(j):
            kb, vb = kv_block(j)
            return _dot_nt(qv, kb), vb

        def update(j, s, vb):
            if softmax:
                s = s * MLA_SCALE
                m_prev = m_sc[...]
                m_next = jnp.maximum(m_prev, jnp.max(s, axis=-1, keepdims=True))
                alpha = jnp.exp(m_prev - m_next)
                p = jnp.exp(s - jnp.tile(m_next, (1, rep)))
                l_sc[...] = alpha * l_sc[...] + jnp.sum(p, axis=-1, keepdims=True)
                m_sc[...] = m_next
                acc_sc[...] = acc_sc[...] * jnp.tile(alpha, (1, dv // LANES)) + jnp.dot(
                    p.astype(BF16), vb, preferred_element_type=F32)
            else:
                fac = jnp.exp(lg_ref[0] * ((i - j) * T).astype(F32))
                p = s * (rel_ref[0] * jnp.tile(fac, (1, rep)))
                acc_sc[...] += jnp.dot(p.astype(BF16), vb, preferred_element_type=F32)

        def pair(jj, carry):
            first, second = scores(2 * jj), scores(2 * jj + 1)
            update(2 * jj, *first)
            update(2 * jj + 1, *second)
            return carry

        lax.fori_loop(0, i // 2, pair, 0)

        @pl.when(i % 2 == 1)
        def _():
            update(i - 1, *scores(i - 1))

        if softmax:
            l = l_sc[...]
            o_ref[...] = acc_sc[...] / jnp.tile(l, (1, dv // LANES))
            lse_ref[...] = m_sc[...] + jnp.log(l)
        else:
            o_ref[...] = acc_sc[...]

    in_specs = [pl.BlockSpec((T, dk), lambda h, i: (i, h)), pl.BlockSpec((S, dk), lambda h, i: (0, h)),
                pl.BlockSpec((S, dv), lambda h, i: (0, h))]
    o_spec = pl.BlockSpec((T, dv), lambda h, i: (i, h))
    if softmax:
        return _call(body, name, [_sds((S, heads * dv), F32), _sds((S, heads * LANES), F32)], (heads, nq), in_specs,
                     [o_spec, pl.BlockSpec((T, LANES), lambda h, i: (i, h))],
                     scratch=[pltpu.VMEM((T, LANES), F32), pltpu.VMEM((T, LANES), F32), pltpu.VMEM((T, dv), F32)],
                     sem=("parallel", "arbitrary"), side=side)(q, k, v)
    lgrow, rel, reld = tables
    in_specs += [pl.BlockSpec((1, 1, LANES), lambda h, i: (h, 0, 0)), pl.BlockSpec((1, T, T), lambda h, i: (h, 0, 0)),
                 pl.BlockSpec((1, T, T), lambda h, i: (h, 0, 0))]
    return _call(body, name, _sds((S, heads * dv), F32), (heads, nq), in_specs, o_spec,
                 scratch=[pltpu.VMEM((T, dv), F32)], sem=("parallel", "arbitrary"), side=side)(q, k, v, lgrow, rel, reld)


def _attn_bwd(q, k, v, do, heads, dk, dv, softmax, name, o=None, lse=None, tables=None, side=None):
    S = q.shape[0]
    T = ATT_BLOCK
    nq = S // T
    rep = T // LANES

    def body(*refs):
        if softmax:
            q_ref, k_ref, v_ref, do_ref, o_ref, lse_ref, dq_ref, dk_ref, dv_ref, dq_sc = refs
        else:
            q_ref, k_ref, v_ref, do_ref, lg_ref, rel_ref, reld_ref, dq_ref, dk_ref, dv_ref, dq_sc = refs
        i = pl.program_id(1)

        @pl.when(i == 0)
        def _():
            dk_ref[...] = jnp.zeros_like(dk_ref)
            dv_ref[...] = jnp.zeros_like(dv_ref)

        qv = q_ref[...]
        dof = do_ref[...].astype(F32)
        dov = dof.astype(BF16)
        if softmax:
            delta = jnp.sum(dof * o_ref[...], axis=-1, keepdims=True)
            lse_t = jnp.tile(lse_ref[...], (1, rep))
        dq_sc[...] = jnp.zeros_like(dq_sc)

        def products(j):
            rows = pl.ds(pl.multiple_of(j * T, T), T)
            kb = k_ref[rows, :]
            return rows, kb, _dot_nt(qv, kb), _dot_nt(dov, v_ref[rows, :])

        def block(j, diagonal, rows, kb, s, dp):
            if softmax:
                s = s * MLA_SCALE
                if diagonal:
                    s = jnp.where(_chunk_mask(T), s, NEG)
                p = jnp.exp(s - lse_t)
                ds = p * (dp - delta) * MLA_SCALE
            else:
                if diagonal:
                    dec = reld_ref[0]
                else:
                    fac = jnp.exp(lg_ref[0] * ((i - j) * T).astype(F32))
                    dec = rel_ref[0] * jnp.tile(fac, (1, rep))
                p = s * dec
                ds = dp * dec
            dsb = ds.astype(BF16)
            dv_ref[rows, :] += _dot_tn(p.astype(BF16), dov)
            dk_ref[rows, :] += _dot_tn(dsb, qv)
            dq_sc[...] += jnp.dot(dsb, kb, preferred_element_type=F32)

        block(i, True, *products(i))

        def pair(jj, carry):
            first, second = products(2 * jj), products(2 * jj + 1)
            block(2 * jj, False, *first)
            block(2 * jj + 1, False, *second)
            return carry

        lax.fori_loop(0, i // 2, pair, 0)

        @pl.when(i % 2 == 1)
        def _():
            block(i - 1, False, *products(i - 1))

        dq_ref[...] = dq_sc[...]

    qspec = pl.BlockSpec((T, dk), lambda h, i: (i, h))
    kspec = pl.BlockSpec((S, dk), lambda h, i: (0, h))
    vspec = pl.BlockSpec((S, dv), lambda h, i: (0, h))
    dospec = pl.BlockSpec((T, dv), lambda h, i: (i, h))
    in_specs = [qspec, kspec, vspec, dospec]
    args = [q, k, v, do]
    if softmax:
        in_specs += [dospec, pl.BlockSpec((T, LANES), lambda h, i: (i, h))]
        args += [o, lse]
    else:
        in_specs += [pl.BlockSpec((1, 1, LANES), lambda h, i: (h, 0, 0)),
                     pl.BlockSpec((1, T, T), lambda h, i: (h, 0, 0)), pl.BlockSpec((1, T, T), lambda h, i: (h, 0, 0))]
        args += list(tables)
    return _call(body, name, [_sds((S, heads * dk), F32), _sds((S, heads * dk), F32), _sds((S, heads * dv), F32)],
                 (heads, nq), in_specs, [qspec, kspec, vspec], scratch=[pltpu.VMEM((T, dk), F32)],
                 sem=("parallel", "arbitrary"), side=side)(*args)


def _rope_tables(pos):
    def tables(dim):
        inv_freq = ROPE_THETA ** (-jnp.arange(0, dim, 2, dtype=F32) / dim)
        ang = pos.astype(F32)[:, None] * inv_freq
        return jnp.cos(ang), jnp.sin(ang)

    cm, sm = tables(ROPE)
    S = pos.shape[0]
    z32, z64 = jnp.zeros((S, 32), F32), jnp.zeros((S, 64), F32)
    cr, sr = tables(RET_DK)
    return (jnp.concatenate([cm, cm, z64], 1), jnp.concatenate([z32, sm, z64], 1),
            jnp.concatenate([-sm, z32, z64], 1), cr, sr)


def _row(v):
    return v.reshape(1, -1).astype(F32)


def _local_step(x, pos, target, pipe, P):
    tabs = _rope_tables(pos)
    dtabs = _decay_tables(ATT_BLOCK)
    xf, xb = _ln_fwd([x], [1.0], _row(P["ln_in_g"]), _row(P["ln_in_b"]), "ln_in", False)
    pipe.gather_first()
    saved = []
    for l in range(DEPTH):
        w = functools.partial(pipe.weight, l)
        t = f"_l{l}"
        h = pipe.run(_matmul, "mm_h" + t, xb, w("w_in"))
        qn, kvn, kr, rq, rk, rv = _prep1(h, tabs, _row(P["q_norm_g"][l]), _row(P["kv_norm_g"][l]), "prep1" + t)
        q = _matmul(qn, w("w_uq"), "mm_q" + t)
        kv = _matmul(kvn, w("w_ukv"), "mm_kv" + t)
        qm, km, vm = _prep2(q, kv, kr, tabs, "prep2" + t)
        a, lse = pipe.run(_attn_fwd, "mla_fwd" + t, qm, km, vm, MLA_HEADS, HEAD_PAD, VDIM, True)
        o = pipe.run(_attn_fwd, "ret_fwd" + t, rq, rk, rv, RET_HEADS, RET_DK, RET_DV, False, tables=dtabs)
        mixin = _gn_gate(a, o, h, _row(P["ret_gn_g"][l]), _row(P["ret_gn_b"][l]), "gn_gate" + t)
        mix = _matmul(mixin, w("w_out"), "mm_mix" + t)
        z1, x1f, x1b = _ln_fwd([xf, mix], [ALPHA, 1.0], _row(P["ln1_g"][l]), _row(P["ln1_b"][l]), "ln1" + t, True)
        gu, act = pipe.run(_matmul_swiglu, "mm_gu" + t, x1b, w("w_gu"))
        f = pipe.run(_matmul, "mm_down" + t, act, w("w_down"))
        z2, x2f, x2b = _ln_fwd([x1f, f], [ALPHA, 1.0], _row(P["ln2_g"][l]), _row(P["ln2_b"][l]), "ln2" + t, True)
        saved.append(dict(xb=xb, h=h, qn=qn, kvn=kvn, rq=rq, rk=rk, rv=rv, qm=qm, km=km, vm=vm, a=a, lse=lse, o=o,
                          mixin=mixin, z1=z1, x1b=x1b, gu=gu, act=act, z2=z2))
        xf, xb = x2f, x2b

    dy, sqerr = _loss_head(xf, target, "loss_head")
    dP = {}
    dys, coefs = [dy], [1.0]
    for l in reversed(range(DEPTH)):
        w, sv = functools.partial(pipe.weight, l), saved[l]
        t = f"_l{l}"
        dz2, dz2b, dg, db = _ln_bwd(dys, coefs, sv["z2"], _row(P["ln2_g"][l]), "ln2_bwd" + t)
        dP[("ln2_g", l)], dP[("ln2_b", l)] = dg, db
        pipe.reduce(l, w_down=_matmul(sv["act"], dz2b, "mm_dw_down" + t, ta=True, out_dtype=BF16))
        dact = pipe.run(_matmul, "mm_dact" + t, dz2b, w("w_down"), tb=True)
        dgu = _swiglu_bwd(sv["gu"], dact, "swiglu_bwd" + t)
        pipe.reduce(l, w_gu=_matmul(sv["x1b"], dgu, "mm_dw_gu" + t, ta=True, out_dtype=BF16))
        dx1 = pipe.run(_matmul, "mm_dx1" + t, dgu, w("w_gu"), tb=True)
        dz1, dz1b, dg, db = _ln_bwd([dz2, dx1], [ALPHA, 1.0], sv["z1"], _row(P["ln1_g"][l]), "ln1_bwd" + t)
        dP[("ln1_g", l)], dP[("ln1_b", l)] = dg, db
        pipe.reduce(l, w_out=_matmul(sv["mixin"], dz1b, "mm_dw_out" + t, ta=True, out_dtype=BF16))
        dmixin = _matmul(dz1b, w("w_out"), "mm_dmixin" + t, tb=True)
        do, drg, dgg, dgb = _gn_gate_bwd(dmixin, sv["o"], sv["h"], _row(P["ret_gn_g"][l]), _row(P["ret_gn_b"][l]),
                                         "gn_gate_bwd" + t)
        dP[("ret_gn_g", l)], dP[("ret_gn_b", l)] = dgg, dgb
        drq, drk, drv = pipe.run(_attn_bwd, "ret_bwd" + t, sv["rq"], sv["rk"], sv["rv"], do, RET_HEADS, RET_DK, RET_DV,
                                 False, tables=dtabs)
        dqm, dkm, dvm = _attn_bwd(sv["qm"], sv["km"], sv["vm"], dmixin, MLA_HEADS, HEAD_PAD, VDIM, True,
                                  "mla_bwd" + t, o=sv["a"], lse=sv["lse"])
        dq, dkv, dkr = _prep2_bwd(dqm, dkm, dvm, tabs, "prep2_bwd" + t)
        g_uq = _matmul(sv["qn"], dq, "mm_dw_uq" + t, ta=True, out_dtype=BF16)
        dqn = _matmul(dq, w("w_uq"), "mm_dqn" + t, tb=True)
        g_ukv = _matmul(sv["kvn"], dkv, "mm_dw_ukv" + t, ta=True, out_dtype=BF16)
        dkvn = _matmul(dkv, w("w_ukv"), "mm_dkvn" + t, tb=True)
        dh, dqg, dkvg = _prep1_bwd(dqn, dkvn, dkr, drq, drk, drv, drg, sv["h"], tabs, _row(P["q_norm_g"][l]),
                                   _row(P["kv_norm_g"][l]), "prep1_bwd" + t)
        dP[("q_norm_g", l)], dP[("kv_norm_g", l)] = dqg, dkvg
        pipe.reduce(l, w_uq=g_uq, w_ukv=g_ukv, w_in=_matmul(sv["xb"], dh, "mm_dw_in" + t, ta=True, out_dtype=BF16))
        dxl = pipe.run(_matmul, "mm_dxl" + t, dh, w("w_in"), tb=True)
        dys, coefs = [dz1, dxl], [ALPHA, 1.0]
    grad_x, _, dg, db = _ln_bwd(dys, coefs, x, _row(P["ln_in_g"]), "ln_in_bwd")
    dP[("ln_in_g", None)], dP[("ln_in_b", None)] = dg, db
    return sqerr, grad_x, dP


INTERNAL_OF = {"w_in": ("w_in",), "w_uq": ("w_uq",), "w_ukv": ("w_ukv",), "w_out": ("w_out",),
               "w_gu": ("w_gate", "w_up"), "w_down": ("w_down",)}


def _internal_weight(name, *blocks):
    cat = lambda parts: jnp.concatenate(parts, axis=1)
    cols = lambda b: cat([b[j] for j in range(N_CHIPS)])
    b = blocks[0]
    if name in ("w_out", "w_down"):
        return b.reshape(-1, b.shape[-1])
    if name == "w_gu":
        return cat([blk[j] for j in range(N_CHIPS) for blk in blocks])
    if name == "w_in":
        return cat([b[0][:, :MLA_IN_USED], jnp.zeros((D_MODEL, MLA_IN - MLA_IN_USED), BF16), b[0][:, MLA_IN_USED:]]
                   + [b[j] for j in range(1, N_CHIPS)])
    if name == "w_uq":
        uq, hw = cols(b), NOPE + ROPE
        pad = jnp.zeros((Q_LORA, HEAD_PAD - hw), BF16)
        return cat([p for h in range(MLA_HEADS) for p in (uq[:, h * hw:(h + 1) * hw], pad)])
    ukv = cols(b)
    return cat([ukv[:, 256 * h:256 * h + NOPE] for h in range(MLA_HEADS)]
               + [ukv[:, 256 * h + NOPE:256 * h + 256] for h in range(MLA_HEADS)])


def _grad_shards(name, g):
    cat = lambda parts: jnp.concatenate(parts, axis=1)
    if name in ("w_out", "w_down"):
        return {name: g.reshape(N_CHIPS, -1, g.shape[-1])}
    if name == "w_gu":
        return {"w_gate": [g[:, 2 * GU_BLOCK * j:2 * GU_BLOCK * j + GU_BLOCK] for j in range(N_CHIPS)],
                "w_up": [g[:, 2 * GU_BLOCK * j + GU_BLOCK:2 * GU_BLOCK * (j + 1)] for j in range(N_CHIPS)]}
    if name == "w_in":
        ci, shift = BIG_SHARD["w_in"][1], MLA_IN - MLA_IN_USED
        return {name: [cat([g[:, :MLA_IN_USED], g[:, MLA_IN:ci + shift]])]
                + [g[:, ci * j + shift:ci * (j + 1) + shift] for j in range(1, N_CHIPS)]}
    if name == "w_uq":
        cq = NOPE + ROPE
        return {name: [cat([g[:, HEAD_PAD * h:HEAD_PAD * h + cq] for h in (2 * j, 2 * j + 1)]) for j in range(N_CHIPS)]}
    return {name: [cat([g[:, o + NOPE * h:o + NOPE * (h + 1)] for h in (2 * j, 2 * j + 1) for o in (0, MLA_HEADS * NOPE)])
                   for j in range(N_CHIPS)]}


def _small_layout(P):
    out, at = {}, 0
    for n in SMALL:
        out[n] = (at, P[n].size)
        at += P[n].size
    return out, at


def _flatten_small(P, last):
    v = jnp.concatenate([P[n].reshape(-1).astype(F32) for n in SMALL] + [last.reshape(-1).astype(F32)])
    return jnp.pad(v, (0, SMALL_ROWS * FLAT_W - v.size)).reshape(SMALL_ROWS, FLAT_W)


def _place():
    return lax.axis_index("x"), lax.axis_index("y"), lax.axis_index("c")


def _other_chips(x, y):
    return [(1 - x, y), (x, 1 - y), (1 - x, 1 - y)]


def _rcopy(src, dst, ssem, rsem, dev):
    return pltpu.make_async_remote_copy(src_ref=src, dst_ref=dst, send_sem=ssem, recv_sem=rsem, device_id=dev,
                                        device_id_type=MESH)


def _comm_call(body, name, out_shape, n_in, scratch):
    many = isinstance(out_shape, (list, tuple))
    return pl.pallas_call(body, name=name, out_shape=out_shape, in_specs=[HBM] * n_in,
                          out_specs=[HBM] * len(out_shape) if many else HBM, scratch_shapes=scratch)


def _half(ref, which):
    rows = ref.shape[0] // 2
    return ref.at[pl.ds(pl.multiple_of(which * rows, 16), rows)]


def _dma_sems(n):
    return pltpu.SemaphoreType.DMA((n,))


def _allgather_side(ws):
    k = len(ws)

    def peers():
        x, y, c = _place()
        return c, 2 * x + y, (x, y, 1 - c), [(n, t, cx, cy) for n in range(k) for t, (cx, cy) in enumerate(_other_chips(x, y))]

    def outgoing(w_refs, g_refs, sems):
        ssem, rsem, _, _, ossem, orsem = sems
        c, j, sib, nt = peers()
        owns = [_rcopy(w_refs[n], g_refs[n].at[j], ossem.at[n], orsem.at[n], sib) for n in range(k)]
        sends = [_rcopy(_half(w_refs[n], c), _half(g_refs[n].at[j], c), ssem.at[3 * n + t], rsem.at[3 * n + t],
                        (cx, cy, c)) for n, t, cx, cy in nt]
        return owns, sends

    def incoming(g_refs, sems):
        ssem, rsem, fssem, frsem, _, _ = sems
        c, _, sib, nt = peers()
        landed, passed, relayed = [], [], []
        for n, t, cx, cy in nt:
            mine, other = (_half(g_refs[n].at[2 * cx + cy], h) for h in (c, 1 - c))
            landed.append(_rcopy(mine, mine, ssem.at[3 * n + t], rsem.at[3 * n + t], (cx, cy, c)))
            passed.append(_rcopy(mine, mine, fssem.at[3 * n + t], frsem.at[3 * n + t], sib))
            relayed.append(_rcopy(other, other, fssem.at[3 * n + t], frsem.at[3 * n + t], sib))
        return landed, passed, relayed

    def start(w_refs, g_refs, sems):
        owns, sends = outgoing(w_refs, g_refs, sems)
        for cp in sends + owns:
            cp.start()

    def finish(w_refs, g_refs, sems):
        owns, sends = outgoing(w_refs, g_refs, sems)
        landed, passed, relayed = incoming(g_refs, sems)
        for got, on in zip(landed, passed):
            got.wait_recv()
            on.start()
        for cp in relayed:
            cp.wait_recv()
        for cp in owns:
            cp.wait()
        for cp in sends + passed:
            cp.wait_send()

    return _Side(list(ws), [_sds((N_CHIPS,) + w.shape, w.dtype) for w in ws],
                 [_dma_sems(3 * k)] * 4 + [_dma_sems(k)] * 2, start, finish)


def _exchange_side(parts):
    k = len(parts)

    def copies(p_refs, rcv_refs, sems):
        ssem, rsem = sems
        x, y, c = _place()
        return [_rcopy(p_refs[n].at[2 * cx + cy], rcv_refs[n].at[t], ssem.at[3 * n + t], rsem.at[3 * n + t], (cx, cy, c))
                for n in range(k) for t, (cx, cy) in enumerate(_other_chips(x, y))]

    def start(p_refs, rcv_refs, sems):
        for cp in copies(p_refs, rcv_refs, sems):
            cp.start()

    def finish(p_refs, rcv_refs, sems):
        for cp in copies(p_refs, rcv_refs, sems):
            cp.wait()

    return _Side(list(parts), [_sds((3,) + p.shape[1:], p.dtype) for p in parts], [_dma_sems(3 * k)] * 2, start, finish)


def _run_side(side, name):
    k_in, k_out = len(side.arrays), len(side.out_shape)

    def body(*refs):
        parts = refs[:k_in], refs[k_in:k_in + k_out], refs[k_in + k_out:]
        side.start(*parts)
        side.finish(*parts)

    return _comm_call(body, name, list(side.out_shape), k_in, list(side.scratch))(*side.arrays)


def _swap_halves(gds, name):
    k = len(gds)

    def body(*refs):
        gd_refs, out_refs, (ssem, rsem) = refs[:k], refs[k:2 * k], refs[2 * k:]
        x, y, c = _place()
        cps = [_rcopy(_half(gd_refs[n].at[jj], 1 - c), out_refs[n].at[jj], ssem.at[N_CHIPS * n + jj],
                      rsem.at[N_CHIPS * n + jj], (x, y, 1 - c)) for n in range(k) for jj in range(N_CHIPS)]
        for cp in cps:
            cp.start()
        for cp in cps:
            cp.wait()

    sems = _dma_sems(N_CHIPS * k)
    return _comm_call(body, name, [_sds((N_CHIPS, g.shape[1] // 2, g.shape[2]), g.dtype) for g in gds], k,
                      [sems, sems])(*gds)


def _allreduce_small(small):
    def body(s_ref, all_ref, sssem, srsem, lsem):
        x, y, c = _place()
        me = 4 * x + 2 * y + c
        own = pltpu.make_async_copy(s_ref, all_ref.at[me], lsem)
        own.start()
        cps = []
        for r in range(1, 8):
            fx, fy, fc = (r >> 2) & 1, (r >> 1) & 1, r & 1
            px, py, pc = (1 - x if fx else x, 1 - y if fy else y, 1 - c if fc else c)
            peer = 4 * px + 2 * py + pc
            send = _rcopy(s_ref, all_ref.at[me], sssem.at[r - 1], srsem.at[me], (px, py, pc))
            send.start()
            cps.append((send, _rcopy(s_ref, all_ref.at[peer], sssem.at[r - 1], srsem.at[peer], (px, py, pc))))
        for send, recv in cps:
            send.wait_send()
            recv.wait_recv()
        own.wait()

    return _comm_call(body, "allreduce_small", [_sds((8,) + small.shape, small.dtype)], 1,
                      [pltpu.SemaphoreType.DMA((7,)), pltpu.SemaphoreType.DMA((8,)), pltpu.SemaphoreType.DMA(())])(small)[0]


def _share_halves(reds, name):
    k = len(reds)

    def body(*refs):
        r_refs, out_refs, (ssem, rsem) = refs[:k], refs[k:2 * k], refs[2 * k:]
        x, y, c = _place()
        cps = [_rcopy(r_refs[n], out_refs[n], ssem.at[n], rsem.at[n], (x, y, 1 - c)) for n in range(k)]
        for cp in cps:
            cp.start()
        for cp in cps:
            cp.wait()

    return _comm_call(body, name, [_sds(r.shape, r.dtype) for r in reds], k, [_dma_sems(k)] * 2)(*reds)


def _add_pair(gd, got, c, name):
    _, R, W = got.shape
    tm = _pick(R, (512, 256, 128, 64))
    nb = R // tm

    def body(c_ref, a_ref, b_ref, o_ref):
        o_ref[...] = (a_ref[...].astype(F32) + b_ref[...].astype(F32)).astype(o_ref.dtype)

    grid_spec = pltpu.PrefetchScalarGridSpec(
        num_scalar_prefetch=1, grid=(N_CHIPS, nb),
        in_specs=[pl.BlockSpec((None, tm, W), lambda j, i, c_ref: (j, c_ref[0] * nb + i, 0)),
                  pl.BlockSpec((None, tm, W), lambda j, i, c_ref: (j, i, 0))],
        out_specs=pl.BlockSpec((None, tm, W), lambda j, i, c_ref: (j, i, 0)))
    return pl.pallas_call(body, name=name, grid_spec=grid_spec, out_shape=_sds((N_CHIPS, R, W), gd.dtype),
                          compiler_params=pltpu.CompilerParams(dimension_semantics=("parallel", "parallel"),
                                                               vmem_limit_bytes=VMEM_LIMIT))(c, gd, got)


def _add_chips(part, rcv, j, name):
    _, R, W = part.shape
    tm = _pick(R, (512, 256, 128, 64))

    def body(j_ref, p_ref, r0_ref, r1_ref, r2_ref, o_ref):
        o_ref[...] = ((p_ref[...].astype(F32) + r0_ref[...].astype(F32)) + r1_ref[...].astype(F32)) + r2_ref[...].astype(F32)

    def slot(t):
        return pl.BlockSpec((None, tm, W), lambda i, j_ref: (t, i, 0))

    grid_spec = pltpu.PrefetchScalarGridSpec(
        num_scalar_prefetch=1, grid=(R // tm,),
        in_specs=[pl.BlockSpec((None, tm, W), lambda i, j_ref: (j_ref[0], i, 0)), slot(0), slot(1), slot(2)],
        out_specs=pl.BlockSpec((tm, W), lambda i, j_ref: (i, 0)))
    return pl.pallas_call(body, name=name, grid_spec=grid_spec, out_shape=_sds((R, W), F32),
                          compiler_params=pltpu.CompilerParams(dimension_semantics=("parallel",),
                                                               vmem_limit_bytes=VMEM_LIMIT))(j, part, rcv, rcv, rcv)


def _sum_small(allsmall):
    _, R, W = allsmall.shape

    def body(a_ref, o_ref):
        acc = a_ref[0]
        for d in range(1, 8):
            acc = acc + a_ref[d]
        o_ref[...] = acc

    return _call(body, "sum_small", _sds((R, W), F32), (1,), [_whole((8, R, W))], _whole((R, W)),
                 sem=("arbitrary",))(allsmall)


def _adamw(w, g, m, v, name):
    R, C = w.shape
    tm = _pick(R, (256, 128, 64, 32, 8))

    def body(w_ref, g_ref, m_ref, v_ref, d_ref, mo_ref, vo_ref):
        gv = g_ref[...]
        mn = ADAM_B1 * m_ref[...] + (1.0 - ADAM_B1) * gv
        vn = ADAM_B2 * v_ref[...] + (1.0 - ADAM_B2) * (gv * gv)
        m_hat = mn / (1.0 - ADAM_B1 ** ADAM_STEP)
        v_hat = vn / (1.0 - ADAM_B2 ** ADAM_STEP)
        d_ref[...] = -ADAM_LR * (m_hat / (jnp.sqrt(v_hat) + ADAM_EPS) + ADAM_WD * w_ref[...])
        mo_ref[...] = mn
        vo_ref[...] = vn

    spec = _rows(tm, C)
    return _call(body, name, [_sds((R, C), F32)] * 3, (R // tm,), [spec] * 4, [spec] * 3, sem=("parallel",))(w, g, m, v)


def _adamw_layer(c, w, m, v, mine, other, l, prev, name):
    _, R, C = w.shape
    half = R // 2
    tm = _pick(half, (256, 128, 64))
    nbh = half // tm

    def body(c_ref, w_ref, m_ref, v_ref, a_ref, b_ref, *rest):
        g_ref, d_ref, mo_ref, vo_ref = rest[-4:]
        gv = jnp.where(pl.program_id(0) // nbh == c_ref[0], a_ref[...], b_ref[...])
        mn = ADAM_B1 * m_ref[...] + (1.0 - ADAM_B1) * gv
        vn = ADAM_B2 * v_ref[...] + (1.0 - ADAM_B2) * (gv * gv)
        m_hat = mn / (1.0 - ADAM_B1 ** ADAM_STEP)
        v_hat = vn / (1.0 - ADAM_B2 ** ADAM_STEP)
        g_ref[...] = gv
        d_ref[...] = -ADAM_LR * (m_hat / (jnp.sqrt(v_hat) + ADAM_EPS) + ADAM_WD * w_ref[...])
        mo_ref[...] = mn
        vo_ref[...] = vn

    layer = pl.BlockSpec((None, tm, C), lambda i, c_ref: (l, i, 0))
    halfspec = pl.BlockSpec((tm, C), lambda i, c_ref: (i % nbh, 0))
    n_prev = 0 if prev is None else 4
    grid_spec = pltpu.PrefetchScalarGridSpec(
        num_scalar_prefetch=1, grid=(R // tm,),
        in_specs=[layer] * 3 + [halfspec] * 2 + [pl.BlockSpec(memory_space=pl.ANY)] * n_prev,
        out_specs=[layer] * 4)
    return pl.pallas_call(body, name=name, grid_spec=grid_spec, out_shape=[_sds(w.shape, F32)] * 4,
                          input_output_aliases={6 + k: k for k in range(n_prev)},
                          compiler_params=pltpu.CompilerParams(dimension_semantics=("parallel",),
                                                               vmem_limit_bytes=VMEM_LIMIT))(
        c, w, m, v, mine, other, *(prev or ()))


FIRST_GATHER = ("w_in", "w_uq", "w_ukv")
GATHER_IN = {
    "mm_h_l0": (0, ("w_out",)), "mla_fwd_l0": (0, ("w_gate", "w_up")), "ret_fwd_l0": (1, ("w_uq", "w_ukv", "w_out")),
    "mm_gu_l0": (0, ("w_down",)), "mm_down_l0": (1, ("w_in",)),
    "mla_fwd_l1": (1, ("w_gate", "w_up")), "mm_gu_l1": (1, ("w_down",))}
EXCHANGE_IN = {
    "mm_dact": ("w_down",), "mm_dx1": ("w_gate", "w_up"), "ret_bwd": ("w_out",), "mm_dxl": ("w_uq", "w_ukv", "w_in")}


class _Pipeline:
    def __init__(self, own, Wt, Mo, Vo, core, chip):
        self.own, self.Wt, self.Mo, self.Vo, self.core, self.chip = own, Wt, Mo, Vo, core, chip
        self.blocks, self.whole, self.parts = {}, {}, {}
        self.results = {n: None for n in BIG}

    def _gathered(self, l, names, blocks):
        for n, b in zip(names, blocks):
            self.blocks[(l, n)] = b

    def gather_first(self):
        side = _allgather_side([self.own[0][n] for n in FIRST_GATHER])
        self._gathered(0, FIRST_GATHER, _run_side(side, "allgather_first"))

    def weight(self, l, name):
        if (l, name) not in self.whole:
            self.whole[(l, name)] = _internal_weight(name, *[self.blocks[(l, n)] for n in INTERNAL_OF[name]])
        return self.whole[(l, name)]

    def run(self, fn, name, *args, **kw):
        base, l = name[:-3], int(name[-1])
        if name in GATHER_IN:
            gl, names = GATHER_IN[name]
            out, blocks = fn(*args, name=name, side=_allgather_side([self.own[gl][n] for n in names]), **kw)
            self._gathered(gl, names, blocks)
            return out
        if base in EXCHANGE_IN:
            names = EXCHANGE_IN[base]
            out, rcvs = fn(*args, name=name, side=_exchange_side([self.parts[(l, n)] for n in names]), **kw)
            self._reduced(l, names, rcvs)
            return out
        return fn(*args, name=name, **kw)

    def reduce(self, l, **grads):
        shards = {}
        for name, g in grads.items():
            shards.update(_grad_shards(name, g))
        names = list(shards)
        gds = [shards[n] if hasattr(shards[n], "shape") else jnp.stack(shards[n]) for n in names]
        got = _swap_halves(gds, f"swap_halves_{names[0]}_l{l}")
        for n, gd, gt in zip(names, gds, got):
            self.parts[(l, n)] = _add_pair(gd, gt, self.core, f"add_pair_{n}_l{l}")

    def _reduced(self, l, names, rcvs):
        reds = [_add_chips(self.parts[(l, n)], rcv, self.chip, f"add_chips_{n}_l{l}") for n, rcv in zip(names, rcvs)]
        others = _share_halves(reds, f"share_halves_{names[0]}_l{l}")
        for n, red, other in zip(names, reds, others):
            self.results[n] = _adamw_layer(self.core, self.Wt[n], self.Mo[n], self.Vo[n], red, other, l,
                                           self.results[n], f"adamw_{n}_l{l}")


def kernel(x, positions, ln_in_g, ln_in_b, w_in, q_norm_g, kv_norm_g, w_uq, w_ukv, ret_gn_g, ret_gn_b, w_out, ln1_g, ln1_b, w_gate, w_up, w_down, ln2_g, ln2_b, loss_target, m_ln_in_g, m_ln_in_b, m_w_in, m_q_norm_g, m_kv_norm_g, m_w_uq, m_w_ukv, m_ret_gn_g, m_ret_gn_b, m_w_out, m_ln1_g, m_ln1_b, m_w_gate, m_w_up, m_w_down, m_ln2_g, m_ln2_b, v_ln_in_g, v_ln_in_b, v_w_in, v_q_norm_g, v_kv_norm_g, v_w_uq, v_w_ukv, v_ret_gn_g, v_ret_gn_b, v_w_out, v_ln1_g, v_ln1_b, v_w_gate, v_w_up, v_w_down, v_ln2_g, v_ln2_b):
    given = dict(locals())
    Wt = {n: given[n] for n in WEIGHTS}
    Mo = {n: given["m_" + n] for n in WEIGHTS}
    Vo = {n: given["v_" + n] for n in WEIGHTS}
    cx, cy, cc = _place()
    chip = (2 * cx + cy).astype(jnp.int32)
    core = cc.astype(jnp.int32)

    own = [{n: Wt[n][l].astype(BF16) for n in BIG} for l in range(DEPTH)]
    pipe = _Pipeline(own, Wt, Mo, Vo, core.reshape(1), chip.reshape(1))
    sqerr, grad_x, dP = _local_step(x[0], positions[0], loss_target[0], pipe, Wt)
    results = pipe.results

    small_g = {n: (dP[(n, None)] if Wt[n].ndim == 1 else jnp.stack([dP[(n, l)] for l in range(DEPTH)])) for n in SMALL}
    local_loss = 0.5 * jnp.sum(sqerr) / D_MODEL
    small_sum = _sum_small(_allreduce_small(_flatten_small(small_g, local_loss))).reshape(-1)
    layout, n_small = _small_layout(Wt)
    loss = small_sum[n_small]

    grads, deltas, new_m, new_v = {}, {}, {}, {}
    for n in BIG:
        grads[n], deltas[n], new_m[n], new_v[n] = results[n]
    zero = jnp.zeros((), F32)
    d, mn, vn = _adamw(_flatten_small(Wt, zero), small_sum.reshape(SMALL_ROWS, FLAT_W), _flatten_small(Mo, zero),
                       _flatten_small(Vo, zero), "adamw_small")
    for n in SMALL:
        at, size = layout[n]
        pick = lambda a: a.reshape(-1)[at:at + size].reshape(Wt[n].shape)
        grads[n], deltas[n], new_m[n], new_v[n] = pick(small_sum), pick(d), pick(mn), pick(vn)

    return (loss, grad_x[None], *[grads[n] for n in WEIGHTS], *[deltas[n] for n in WEIGHTS],
            *[new_m[n] for n in WEIGHTS], *[new_v[n] for n in WEIGHTS])
```

```python
import functools

import jax
import jax.numpy as jnp
from jax import lax
from jax.experimental import pallas as pl
from jax.experimental.pallas import tpu as pltpu

F32 = jnp.float32
BF16 = jnp.bfloat16

D_MODEL = 2048
DEPTH = 2
CHUNK = 64
MLA_HEADS = 8
Q_LORA = 512
KV_LORA = 256
NOPE = 128
ROPE = 64
VDIM = 128
RET_HEADS = 4
RET_DK = 256
RET_DV = 256
D_FF = 5632
D_IN = 4928
ROPE_THETA = 10000.0
LN_EPS = 1e-5
RMS_EPS = 1e-6
GN_EPS = 1e-5
ALPHA = (2 * DEPTH) ** 0.25
MLA_SCALE = (NOPE + ROPE) ** -0.5
RET_SCALE = RET_DK ** -0.5
ADAM_LR = 0.001
ADAM_B1 = 0.9
ADAM_B2 = 0.999
ADAM_EPS = 1e-08
ADAM_WD = 0.01
ADAM_STEP = 10

LANES = 128
HEAD_PAD = 256
MLA_IN = 1024
MLA_IN_USED = Q_LORA + KV_LORA + ROPE
D_IN_PAD = MLA_IN + 4 * 1024
ATT_BLOCK = 512
NEG = -1e30
VMEM_LIMIT = 56 * 1024 * 1024

N_CHIPS = 4
FLAT_W = 1024
BIG = ("w_in", "w_uq", "w_ukv", "w_out", "w_gate", "w_up", "w_down")
BIG_SHARD = {"w_in": (2048, 1232), "w_uq": (512, 384), "w_ukv": (256, 512), "w_out": (512, 2048),
             "w_gate": (2048, 1408), "w_up": (2048, 1408), "w_down": (1408, 2048)}
SMALL = ("ln_in_g", "ln_in_b", "q_norm_g", "kv_norm_g", "ret_gn_g", "ret_gn_b", "ln1_g", "ln1_b", "ln2_g", "ln2_b")
WEIGHTS = ("ln_in_g", "ln_in_b", "w_in", "q_norm_g", "kv_norm_g", "w_uq", "w_ukv", "ret_gn_g", "ret_gn_b", "w_out",
           "ln1_g", "ln1_b", "w_gate", "w_up", "w_down", "ln2_g", "ln2_b")
SMALL_ROWS = 32

MESH = pl.DeviceIdType.MESH


def _pick(dim, cands):
    for c in cands:
        if dim % c == 0:
            return c
    return dim


HBM = pl.BlockSpec(memory_space=pltpu.HBM)


class _Side:
    def __init__(self, arrays, out_shape, scratch, start, finish):
        self.arrays, self.out_shape, self.scratch, self.start, self.finish = arrays, out_shape, scratch, start, finish


def _call(body, name, out_shape, grid, in_specs, out_specs, scratch=(), sem=None, side=None):
    params = pltpu.CompilerParams(dimension_semantics=sem if side is None else ("arbitrary",) * len(grid),
                                  vmem_limit_bytes=VMEM_LIMIT)
    if side is None:
        return pl.pallas_call(body, name=name, out_shape=out_shape, grid=grid, in_specs=in_specs, out_specs=out_specs,
                              scratch_shapes=list(scratch), compiler_params=params)
    single = not isinstance(out_shape, (list, tuple))
    outs = [out_shape] if single else list(out_shape)
    ospecs = [out_specs] if single else list(out_specs)
    cuts = [len(in_specs), len(side.arrays), len(outs), len(side.out_shape), len(scratch)]
    ends = [sum(cuts[:k + 1]) for k in range(len(cuts))]

    def hosted(*refs):
        ins, s_in, o, s_out, scr = (refs[a:b] for a, b in zip([0] + ends[:-1], ends))
        sems = refs[ends[-1]:]
        ids = [pl.program_id(a) for a in range(len(grid))]
        first = functools.reduce(jnp.logical_and, [i == 0 for i in ids])
        last = functools.reduce(jnp.logical_and, [i == g - 1 for i, g in zip(ids, grid)])

        @pl.when(first)
        def _():
            side.start(s_in, s_out, sems)

        body(*ins, *o, *scr)

        @pl.when(last)
        def _():
            side.finish(s_in, s_out, sems)

    call = pl.pallas_call(hosted, name=name, out_shape=outs + list(side.out_shape), grid=grid,
                          in_specs=list(in_specs) + [HBM] * len(side.arrays),
                          out_specs=ospecs + [HBM] * len(side.out_shape),
                          scratch_shapes=list(scratch) + list(side.scratch), compiler_params=params)

    def run(*args):
        res = call(*args, *side.arrays)
        return (res[0] if single else list(res[:len(outs)])), list(res[len(outs):])

    return run


def _rows(tm, w, col=0):
    return pl.BlockSpec((tm, w), lambda i: (i, col))


def _whole(shape):
    return pl.BlockSpec(shape, lambda i: (0,) * len(shape))


def _sds(shape, dtype):
    return jax.ShapeDtypeStruct(shape, dtype)


def _matmul(a, b, name, ta=False, tb=False, out_dtype=F32, side=None):
    (K, M) = a.shape if ta else a.shape[::-1]
    (N, Kb) = b.shape if tb else b.shape[::-1]
    assert K == Kb, (a.shape, b.shape, ta, tb)
    tm = _pick(M, (1024, 512, 256, 128))
    tn = _pick(N, (1024, 512, 256, 128))
    tk = _pick(K, (2816, 2560, 2048, 1024, 512, 256))
    nk = K // tk
    dn = (((0 if ta else 1,), (1 if tb else 0,)), ((), ()))

    def body(a_ref, b_ref, o_ref, acc_ref):
        k = pl.program_id(2)
        if nk == 1:
            o_ref[...] = lax.dot_general(a_ref[...].astype(BF16), b_ref[...].astype(BF16), dn,
                                         preferred_element_type=F32).astype(out_dtype)
        else:
            @pl.when(k == 0)
            def _():
                acc_ref[...] = jnp.zeros_like(acc_ref)

            acc_ref[...] += lax.dot_general(a_ref[...].astype(BF16), b_ref[...].astype(BF16), dn,
                                            preferred_element_type=F32)

            @pl.when(k == nk - 1)
            def _():
                o_ref[...] = acc_ref[...].astype(out_dtype)

    a_spec = pl.BlockSpec((tk, tm), lambda i, j, k: (k, i)) if ta else pl.BlockSpec((tm, tk), lambda i, j, k: (i, k))
    b_spec = pl.BlockSpec((tn, tk), lambda i, j, k: (j, k)) if tb else pl.BlockSpec((tk, tn), lambda i, j, k: (k, j))
    return _call(body, name, _sds((M, N), out_dtype), (M // tm, N // tn, nk), [a_spec, b_spec],
                 pl.BlockSpec((tm, tn), lambda i, j, k: (i, j)), scratch=[pltpu.VMEM((tm, tn), F32)],
                 sem=("parallel", "parallel", "arbitrary"), side=side)(a, b)


def _sigmoid(x):
    return 1.0 / (1.0 + jnp.exp(-x))


def _rope_group(r, c, sa, sb):
    return r * c + pltpu.roll(r, 32, 1) * sa + pltpu.roll(r, 96, 1) * sb


def _ln_fwd(xs, coefs, g, b, name, want_z):
    S, D = xs[0].shape
    tm = 256
    n = len(xs)

    def body(*refs):
        x_refs, g_ref, b_ref, outs = refs[:n], refs[n], refs[n + 1], refs[n + 2:]
        z = None
        for cf, r in zip(coefs, x_refs):
            t = r[...] if cf == 1.0 else cf * r[...]
            z = t if z is None else z + t
        mu = jnp.mean(z, axis=-1, keepdims=True)
        zc = z - mu
        var = jnp.mean(zc * zc, axis=-1, keepdims=True)
        y = zc * lax.rsqrt(var + LN_EPS) * g_ref[...] + b_ref[...]
        if want_z:
            outs[0][...] = z
        outs[-2][...] = y
        outs[-1][...] = y.astype(BF16)

    out_shape = [_sds((S, D), F32)] * (2 if want_z else 1) + [_sds((S, D), BF16)]
    return _call(body, name, out_shape, (S // tm,), [_rows(tm, D)] * n + [_whole((1, D))] * 2,
                 [_rows(tm, D)] * len(out_shape), sem=("parallel",))(*xs, g, b)


def _ln_bwd(dys, coefs, z, g, name):
    S, D = z.shape
    tm = 256
    n = len(dys)

    def body(*refs):
        dy_refs, z_ref, g_ref = refs[:n], refs[n], refs[n + 1]
        dz_ref, dzb_ref, dg_ref, db_ref = refs[n + 2:]
        dy = None
        for cf, r in zip(coefs, dy_refs):
            t = r[...] if cf == 1.0 else cf * r[...]
            dy = t if dy is None else dy + t
        zv = z_ref[...]
        mu = jnp.mean(zv, axis=-1, keepdims=True)
        zc = zv - mu
        var = jnp.mean(zc * zc, axis=-1, keepdims=True)
        rstd = lax.rsqrt(var + LN_EPS)
        xh = zc * rstd
        dyg = dy * g_ref[...]
        dz = rstd * (dyg - jnp.mean(dyg, axis=-1, keepdims=True) - xh * jnp.mean(dyg * xh, axis=-1, keepdims=True))
        dz_ref[...] = dz
        dzb_ref[...] = dz.astype(BF16)

        @pl.when(pl.program_id(0) == 0)
        def _():
            dg_ref[...] = jnp.zeros_like(dg_ref)
            db_ref[...] = jnp.zeros_like(db_ref)

        dg_ref[...] += jnp.sum(dy * xh, axis=0, keepdims=True)
        db_ref[...] += jnp.sum(dy, axis=0, keepdims=True)

    return _call(body, name, [_sds((S, D), F32), _sds((S, D), BF16), _sds((1, D), F32), _sds((1, D), F32)],
                 (S // tm,), [_rows(tm, D)] * (n + 1) + [_whole((1, D))],
                 [_rows(tm, D), _rows(tm, D), _whole((1, D)), _whole((1, D))], sem=("arbitrary",))(*dys, z, g)


def _rms(x, g):
    return x * lax.rsqrt(jnp.mean(x * x, axis=-1, keepdims=True) + RMS_EPS) * g


def _prep1(h, tabs, qg, kvg, name):
    S = h.shape[0]
    tm = 256
    cm, sam, sbm, cr, sr = tabs

    def body(h_ref, cm_ref, sam_ref, sbm_ref, cr_ref, sr_ref, qg_ref, kvg_ref,
             qn_ref, kvn_ref, kr_ref, rq_ref, rk_ref, rv_ref):
        qn_ref[...] = _rms(h_ref[:, 0:Q_LORA], qg_ref[...]).astype(BF16)
        kvn_ref[...] = _rms(h_ref[:, Q_LORA:Q_LORA + KV_LORA], kvg_ref[...]).astype(BF16)
        kr_ref[...] = _rope_group(h_ref[:, 768:896], cm_ref[...], sam_ref[...], sbm_ref[...])
        c, s = cr_ref[...], sr_ref[...]
        for hd in range(RET_HEADS):
            for src, dst, scale in ((MLA_IN, rq_ref, RET_SCALE), (MLA_IN + 1024, rk_ref, None)):
                t1 = h_ref[:, src + hd * 256:src + hd * 256 + 128]
                t2 = h_ref[:, src + hd * 256 + 128:src + hd * 256 + 256]
                o1, o2 = t1 * c - t2 * s, t2 * c + t1 * s
                if scale is not None:
                    o1, o2 = o1 * scale, o2 * scale
                dst[:, hd * 256:hd * 256 + 128] = o1.astype(BF16)
                dst[:, hd * 256 + 128:hd * 256 + 256] = o2.astype(BF16)
        rv_ref[...] = h_ref[:, MLA_IN + 2048:MLA_IN + 3072].astype(BF16)

    t128 = _rows(tm, LANES)
    return _call(body, name,
                 [_sds((S, Q_LORA), BF16), _sds((S, KV_LORA), BF16), _sds((S, LANES), F32),
                  _sds((S, 1024), BF16), _sds((S, 1024), BF16), _sds((S, 1024), BF16)],
                 (S // tm,),
                 [_rows(tm, D_IN_PAD), t128, t128, t128, t128, t128, _whole((1, Q_LORA)), _whole((1, KV_LORA))],
                 [_rows(tm, Q_LORA), _rows(tm, KV_LORA), t128, _rows(tm, 1024), _rows(tm, 1024), _rows(tm, 1024)],
                 sem=("parallel",))(h, cm, sam, sbm, cr, sr, qg, kvg)


def _prep1_bwd(dqn, dkvn, dkr, drq, drk, drv, drg, h, tabs, qg, kvg, name):
    S = h.shape[0]
    tm = 256
    cm, sam, sbm, cr, sr = tabs

    def rms_bwd(x, g, dy):
        r = lax.rsqrt(jnp.mean(x * x, axis=-1, keepdims=True) + RMS_EPS)
        dyg = dy * g
        dx = r * dyg - x * (r * r * r) * jnp.mean(dyg * x, axis=-1, keepdims=True)
        return dx, jnp.sum(dy * x * r, axis=0, keepdims=True)

    def body(dqn_ref, dkvn_ref, dkr_ref, drq_ref, drk_ref, drv_ref, drg_ref, h_ref,
             cm_ref, sam_ref, sbm_ref, cr_ref, sr_ref, qg_ref, kvg_ref, dh_ref, dqg_ref, dkvg_ref):
        dcq, dqg = rms_bwd(h_ref[:, 0:Q_LORA], qg_ref[...], dqn_ref[...])
        dckv, dkvg = rms_bwd(h_ref[:, Q_LORA:Q_LORA + KV_LORA], kvg_ref[...], dkvn_ref[...])
        dh_ref[:, 0:Q_LORA] = dcq.astype(BF16)
        dh_ref[:, Q_LORA:Q_LORA + KV_LORA] = dckv.astype(BF16)
        dh_ref[:, 768:896] = _rope_group(dkr_ref[...], cm_ref[...], -sam_ref[...], -sbm_ref[...]).astype(BF16)
        dh_ref[:, 896:1024] = jnp.zeros((tm, LANES), BF16)
        c, s = cr_ref[...], sr_ref[...]
        for hd in range(RET_HEADS):
            for src, dst, scale in ((drq_ref, MLA_IN, RET_SCALE), (drk_ref, MLA_IN + 1024, None)):
                d1 = src[:, hd * 256:hd * 256 + 128]
                d2 = src[:, hd * 256 + 128:hd * 256 + 256]
                if scale is not None:
                    d1, d2 = d1 * scale, d2 * scale
                dh_ref[:, dst + hd * 256:dst + hd * 256 + 128] = (d1 * c + d2 * s).astype(BF16)
                dh_ref[:, dst + hd * 256 + 128:dst + hd * 256 + 256] = (d2 * c - d1 * s).astype(BF16)
        dh_ref[:, MLA_IN + 2048:MLA_IN + 3072] = drv_ref[...].astype(BF16)
        dh_ref[:, MLA_IN + 3072:MLA_IN + 4096] = drg_ref[...].astype(BF16)

        @pl.when(pl.program_id(0) == 0)
        def _():
            dqg_ref[...] = jnp.zeros_like(dqg_ref)
            dkvg_ref[...] = jnp.zeros_like(dkvg_ref)

        dqg_ref[...] += dqg
        dkvg_ref[...] += dkvg

    t128 = _rows(tm, LANES)
    return _call(body, name,
                 [_sds((S, D_IN_PAD), BF16), _sds((1, Q_LORA), F32), _sds((1, KV_LORA), F32)],
                 (S // tm,),
                 [_rows(tm, Q_LORA), _rows(tm, KV_LORA), t128, _rows(tm, 1024), _rows(tm, 1024), _rows(tm, 1024),
                  _rows(tm, 1024), _rows(tm, MLA_IN), t128, t128, t128, t128, t128,
                  _whole((1, Q_LORA)), _whole((1, KV_LORA))],
                 [_rows(tm, D_IN_PAD), _whole((1, Q_LORA)), _whole((1, KV_LORA))],
                 sem=("arbitrary",))(dqn, dkvn, dkr, drq, drk, drv, drg, h, cm, sam, sbm, cr, sr, qg, kvg)


def _prep2(q, kv, kr, tabs, name):
    S = q.shape[0]
    tm = 256
    cm, sam, sbm = tabs[:3]

    def body(q_ref, kv_ref, kr_ref, cm_ref, sam_ref, sbm_ref, qo_ref, ko_ref, vo_ref):
        c, sa, sb = cm_ref[...], sam_ref[...], sbm_ref[...]
        krb = kr_ref[...].astype(BF16)
        for hd in range(MLA_HEADS):
            o = hd * HEAD_PAD
            qo_ref[:, o:o + 128] = q_ref[:, o:o + 128].astype(BF16)
            qo_ref[:, o + 128:o + 256] = _rope_group(q_ref[:, o + 128:o + 256], c, sa, sb).astype(BF16)
            ko_ref[:, o:o + 128] = kv_ref[:, hd * 128:hd * 128 + 128].astype(BF16)
            ko_ref[:, o + 128:o + 256] = krb
        vo_ref[...] = kv_ref[:, 1024:2048].astype(BF16)

    t128 = _rows(tm, LANES)
    return _call(body, name, [_sds((S, 2048), BF16), _sds((S, 2048), BF16), _sds((S, 1024), BF16)], (S // tm,),
                 [_rows(tm, 2048), _rows(tm, 2048), t128, t128, t128, t128],
                 [_rows(tm, 2048), _rows(tm, 2048), _rows(tm, 1024)], sem=("parallel",))(q, kv, kr, cm, sam, sbm)


def _prep2_bwd(dqm, dkm, dvm, tabs, name):
    S = dqm.shape[0]
    tm = 256
    cm, sam, sbm = tabs[:3]

    def body(dq_ref, dk_ref, dv_ref, cm_ref, sam_ref, sbm_ref, dqo_ref, dkvo_ref, dkr_ref):
        c, sa, sb = cm_ref[...], -sam_ref[...], -sbm_ref[...]
        dkr = None
        for hd in range(MLA_HEADS):
            o = hd * HEAD_PAD
            dqo_ref[:, o:o + 128] = dq_ref[:, o:o + 128].astype(BF16)
            dqo_ref[:, o + 128:o + 256] = _rope_group(dq_ref[:, o + 128:o + 256], c, sa, sb).astype(BF16)
            dkvo_ref[:, hd * 128:hd * 128 + 128] = dk_ref[:, o:o + 128].astype(BF16)
            t = dk_ref[:, o + 128:o + 256]
            dkr = t if dkr is None else dkr + t
        dkvo_ref[:, 1024:2048] = dv_ref[...].astype(BF16)
        dkr_ref[...] = dkr

    t128 = _rows(tm, LANES)
    return _call(body, name, [_sds((S, 2048), BF16), _sds((S, 2048), BF16), _sds((S, LANES), F32)], (S // tm,),
                 [_rows(tm, 2048), _rows(tm, 2048), _rows(tm, 1024), t128, t128, t128],
                 [_rows(tm, 2048), _rows(tm, 2048), t128], sem=("parallel",))(dqm, dkm, dvm, cm, sam, sbm)


def _gn_gate(a, o, h, gg, gb, name):
    S = a.shape[0]
    tm = 256

    def body(a_ref, o_ref, rg_ref, gg_ref, gb_ref, mix_ref):
        mix_ref[:, 0:1024] = a_ref[...].astype(BF16)
        for hd in range(RET_HEADS):
            sl = slice(hd * 256, hd * 256 + 256)
            ov = o_ref[:, sl]
            mu = jnp.mean(ov, axis=-1, keepdims=True)
            oc = ov - mu
            var = jnp.mean(oc * oc, axis=-1, keepdims=True)
            y = oc * lax.rsqrt(var + GN_EPS) * gg_ref[:, sl] + gb_ref[:, sl]
            rg = rg_ref[:, sl]
            mix_ref[:, 1024 + hd * 256:1024 + hd * 256 + 256] = (rg * _sigmoid(rg) * y).astype(BF16)

    return _call(body, name, _sds((S, 2048), BF16), (S // tm,),
                 [_rows(tm, 1024), _rows(tm, 1024), _rows(tm, 1024, 4), _whole((1, 1024)), _whole((1, 1024))],
                 _rows(tm, 2048), sem=("parallel",))(a, o, h, gg, gb)


def _gn_gate_bwd(dmixin, o, h, gg, gb, name):
    S = o.shape[0]
    tm = 256

    def body(dr_ref, o_ref, rg_ref, gg_ref, gb_ref, do_ref, drg_ref, dgg_ref, dgb_ref):
        @pl.when(pl.program_id(0) == 0)
        def _():
            dgg_ref[...] = jnp.zeros_like(dgg_ref)
            dgb_ref[...] = jnp.zeros_like(dgb_ref)

        for hd in range(RET_HEADS):
            sl = slice(hd * 256, hd * 256 + 256)
            ov = o_ref[:, sl]
            mu = jnp.mean(ov, axis=-1, keepdims=True)
            oc = ov - mu
            var = jnp.mean(oc * oc, axis=-1, keepdims=True)
            rstd = lax.rsqrt(var + GN_EPS)
            xh = oc * rstd
            g = gg_ref[:, sl]
            y = xh * g + gb_ref[:, sl]
            rg = rg_ref[:, sl]
            sg = _sigmoid(rg)
            dr = dr_ref[:, sl]
            dy = dr * (rg * sg)
            drg_ref[:, sl] = dr * y * (sg * (1.0 + rg * (1.0 - sg)))
            dgg_ref[:, sl] += jnp.sum(dy * xh, axis=0, keepdims=True)
            dgb_ref[:, sl] += jnp.sum(dy, axis=0, keepdims=True)
            dxh = dy * g
            do = rstd * (dxh - jnp.mean(dxh, axis=-1, keepdims=True) - xh * jnp.mean(dxh * xh, axis=-1, keepdims=True))
            do_ref[:, sl] = do.astype(BF16)

    return _call(body, name,
                 [_sds((S, 1024), BF16), _sds((S, 1024), F32), _sds((1, 1024), F32), _sds((1, 1024), F32)],
                 (S // tm,),
                 [_rows(tm, 1024, 1), _rows(tm, 1024), _rows(tm, 1024, 4), _whole((1, 1024)), _whole((1, 1024))],
                 [_rows(tm, 1024), _rows(tm, 1024), _whole((1, 1024)), _whole((1, 1024))],
                 sem=("arbitrary",))(dmixin, o, h, gg, gb)


GU_BLOCK = D_FF // N_CHIPS


def _matmul_swiglu(x, w_gu, name, side=None):
    S, K = x.shape
    tm = _pick(S, (512, 256, 128))
    tn = 2 * GU_BLOCK

    def body(x_ref, w_ref, gu_ref, act_ref):
        r = jnp.dot(x_ref[...], w_ref[...], preferred_element_type=F32)
        g, u = r[:, :GU_BLOCK], r[:, GU_BLOCK:]
        gu_ref[...] = r.astype(BF16)
        act_ref[...] = (g * _sigmoid(g) * u).astype(BF16)

    return _call(body, name, [_sds((S, 2 * D_FF), BF16), _sds((S, D_FF), BF16)], (S // tm, N_CHIPS),
                 [pl.BlockSpec((tm, K), lambda i, j: (i, 0)), pl.BlockSpec((K, tn), lambda i, j: (0, j))],
                 [pl.BlockSpec((tm, tn), lambda i, j: (i, j)), pl.BlockSpec((tm, GU_BLOCK), lambda i, j: (i, j))],
                 sem=("parallel", "parallel"), side=side)(x, w_gu)


def _swiglu_bwd(gu, dact, name):
    S = gu.shape[0]
    tm = 128

    def body(gu_ref, d_ref, o_ref):
        for j in range(N_CHIPS):
            at = 2 * GU_BLOCK * j
            g = gu_ref[:, at:at + GU_BLOCK].astype(F32)
            u = gu_ref[:, at + GU_BLOCK:at + 2 * GU_BLOCK].astype(F32)
            d = d_ref[:, GU_BLOCK * j:GU_BLOCK * (j + 1)]
            sg = _sigmoid(g)
            o_ref[:, at:at + GU_BLOCK] = (d * u * (sg * (1.0 + g * (1.0 - sg)))).astype(BF16)
            o_ref[:, at + GU_BLOCK:at + 2 * GU_BLOCK] = (d * (g * sg)).astype(BF16)

    return _call(body, name, _sds((S, 2 * D_FF), BF16), (S // tm,), [_rows(tm, 2 * D_FF), _rows(tm, D_FF)],
                 _rows(tm, 2 * D_FF), sem=("parallel",))(gu, dact)


def _loss_head(y, target, name):
    S, D = y.shape
    tm = 256

    def body(y_ref, t_ref, dy_ref, acc_ref):
        e = y_ref[...] - t_ref[...]
        dy_ref[...] = e / D

        @pl.when(pl.program_id(0) == 0)
        def _():
            acc_ref[...] = jnp.zeros_like(acc_ref)

        acc_ref[...] += jnp.sum(e * e, axis=0, keepdims=True)

    return _call(body, name, [_sds((S, D), F32), _sds((1, D), F32)], (S // tm,), [_rows(tm, D), _rows(tm, D)],
                 [_rows(tm, D), _whole((1, D))], sem=("arbitrary",))(y, target)


def _chunk_mask(T):
    r = lax.shift_right_logical(lax.broadcasted_iota(jnp.int32, (T, T), 0), 6)
    c = lax.shift_right_logical(lax.broadcasted_iota(jnp.int32, (T, T), 1), 6)
    return r >= c


def _dot_nt(a, b):
    return lax.dot_general(a, b, (((1,), (1,)), ((), ())), preferred_element_type=F32)


def _dot_tn(a, b):
    return lax.dot_general(a, b, (((0,), (0,)), ((), ())), preferred_element_type=F32)


def _decay_tables(T):
    lg = jnp.log1p(-jnp.exp2(-5.0 - jnp.arange(RET_HEADS, dtype=F32)))
    idx = jnp.arange(T, dtype=F32)
    diff = idx[:, None] - idx[None, :]
    rel = jnp.exp(lg[:, None, None] * diff[None])
    cid = jnp.arange(T) // CHUNK
    mask = (cid[:, None] >= cid[None, :]).astype(F32)
    reld = jnp.exp(lg[:, None, None] * jnp.abs(diff)[None]) * mask[None]
    lgrow = jnp.broadcast_to(lg[:, None, None], (RET_HEADS, 1, LANES))
    return lgrow, rel, reld


def _attn_fwd(q, k, v, heads, dk, dv, softmax, name, tables=None, side=None):
    S = q.shape[0]
    T = ATT_BLOCK
    nq = S // T
    rep = T // LANES

    def body(*refs):
        if softmax:
            q_ref, k_ref, v_ref, o_ref, lse_ref, m_sc, l_sc, acc_sc = refs
        else:
            q_ref, k_ref, v_ref, lg_ref, rel_ref, reld_ref, o_ref, acc_sc = refs
        i = pl.program_id(1)
        qv = q_ref[...]

        def kv_block(j):
            rows = pl.ds(pl.multiple_of(j * T, T), T)
            return k_ref[rows, :], v_ref[rows, :]

        kb, vb = kv_block(i)
        s = _dot_nt(qv, kb)
        if softmax:
            s = jnp.where(_chunk_mask(T), s * MLA_SCALE, NEG)
            m = jnp.max(s, axis=-1, keepdims=True)
            p = jnp.exp(s - m)
            m_sc[...] = jnp.broadcast_to(m, (T, LANES))
            l_sc[...] = jnp.broadcast_to(jnp.sum(p, axis=-1, keepdims=True), (T, LANES))
        else:
            p = s * reld_ref[0]
        acc_sc[...] = jnp.dot(p.astype(BF16), vb, preferred_element_type=F32)

        def scores(j):
            kb, vb = kv_block(j)
            return _dot_nt(qv, kb), vb

        def update(j, s, vb):
            if softmax:
                s = s * MLA_SCALE
                m_prev = m_sc[...]
                m_next = jnp.maximum(m_prev, jnp.max(s, axis=-1, keepdims=True))
                alpha = jnp.exp(m_prev - m_next)
                p = jnp.exp(s - jnp.tile(m_next, (1, rep)))
                l_sc[...] = alpha * l_sc[...] + jnp.sum(p, axis=-1, keepdims=True)
                m_sc[...] = m_next
                acc_sc[...] = acc_sc[...] * jnp.tile(alpha, (1, dv // LANES)) + jnp.dot(
                    p.astype(BF16), vb, preferred_element_type=F32)
            else:
                fac = jnp.exp(lg_ref[0] * ((i - j) * T).astype(F32))
                p = s * (rel_ref[0] * jnp.tile(fac, (1, rep)))
                acc_sc[...] += jnp.dot(p.astype(BF16), vb, preferred_element_type=F32)

        def pair(jj, carry):
            first, second = scores(2 * jj), scores(2 * jj + 1)
            update(2 * jj, *first)
            update(2 * jj + 1, *second)
            return carry

        lax.fori_loop(0, i // 2, pair, 0)

        @pl.when(i % 2 == 1)
        def _():
            update(i - 1, *scores(i - 1))

        if softmax:
            l = l_sc[...]
            o_ref[...] = acc_sc[...] / jnp.tile(l, (1, dv // LANES))
            lse_ref[...] = m_sc[...] + jnp.log(l)
        else:
            o_ref[...] = acc_sc[...]

    in_specs = [pl.BlockSpec((T, dk), lambda h, i: (i, h)), pl.BlockSpec((S, dk), lambda h, i: (0, h)),
                pl.BlockSpec((S, dv), lambda h, i: (0, h))]
    o_spec = pl.BlockSpec((T, dv), lambda h, i: (i, h))
    if softmax:
        return _call(body, name, [_sds((S, heads * dv), F32), _sds((S, heads * LANES), F32)], (heads, nq), in_specs,
                     [o_spec, pl.BlockSpec((T, LANES), lambda h, i: (i, h))],
                     scratch=[pltpu.VMEM((T, LANES), F32), pltpu.VMEM((T, LANES), F32), pltpu.VMEM((T, dv), F32)],
                     sem=("parallel", "arbitrary"), side=side)(q, k, v)
    lgrow, rel, reld = tables
    in_specs += [pl.BlockSpec((1, 1, LANES), lambda h, i: (h, 0, 0)), pl.BlockSpec((1, T, T), lambda h, i: (h, 0, 0)),
                 pl.BlockSpec((1, T, T), lambda h, i: (h, 0, 0))]
    return _call(body, name, _sds((S, heads * dv), F32), (heads, nq), in_specs, o_spec,
                 scratch=[pltpu.VMEM((T, dv), F32)], sem=("parallel", "arbitrary"), side=side)(q, k, v, lgrow, rel, reld)


def _attn_bwd(q, k, v, do, heads, dk, dv, softmax, name, o=None, lse=None, tables=None, side=None):
    S = q.shape[0]
    T = ATT_BLOCK
    nq = S // T
    rep = T // LANES

    def body(*refs):
        if softmax:
            q_ref, k_ref, v_ref, do_ref, o_ref, lse_ref, dq_ref, dk_ref, dv_ref, dq_sc = refs
        else:
            q_ref, k_ref, v_ref, do_ref, lg_ref, rel_ref, reld_ref, dq_ref, dk_ref, dv_ref, dq_sc = refs
        i = pl.program_id(1)

        @pl.when(i == 0)
        def _():
            dk_ref[...] = jnp.zeros_like(dk_ref)
            dv_ref[...] = jnp.zeros_like(dv_ref)

        qv = q_ref[...]
        dof = do_ref[...].astype(F32)
        dov = dof.astype(BF16)
        if softmax:
            delta = jnp.sum(dof * o_ref[...], axis=-1, keepdims=True)
            lse_t = jnp.tile(lse_ref[...], (1, rep))
        dq_sc[...] = jnp.zeros_like(dq_sc)

        def products(j):
            rows = pl.ds(pl.multiple_of(j * T, T), T)
            kb = k_ref[rows, :]
            return rows, kb, _dot_nt(qv, kb), _dot_nt(dov, v_ref[rows, :])

        def block(j, diagonal, rows, kb, s, dp):
            if softmax:
                s = s * MLA_SCALE
                if diagonal:
                    s = jnp.where(_chunk_mask(T), s, NEG)
                p = jnp.exp(s - lse_t)
                ds = p * (dp - delta) * MLA_SCALE
            else:
                if diagonal:
                    dec = reld_ref[0]
                else:
                    fac = jnp.exp(lg_ref[0] * ((i - j) * T).astype(F32))
                    dec = rel_ref[0] * jnp.tile(fac, (1, rep))
                p = s * dec
                ds = dp * dec
            dsb = ds.astype(BF16)
            dv_ref[rows, :] += _dot_tn(p.astype(BF16), dov)
            dk_ref[rows, :] += _dot_tn(dsb, qv)
            dq_sc[...] += jnp.dot(dsb, kb, preferred_element_type=F32)

        block(i, True, *products(i))

        def pair(jj, carry):
            first, second = products(2 * jj), products(2 * jj + 1)
            block(2 * jj, False, *first)
            block(2 * jj + 1, False, *second)
            return carry

        lax.fori_loop(0, i // 2, pair, 0)

        @pl.when(i % 2 == 1)
        def _():
            block(i - 1, False, *products(i - 1))

        dq_ref[...] = dq_sc[...]

    qspec = pl.BlockSpec((T, dk), lambda h, i: (i, h))
    kspec = pl.BlockSpec((S, dk), lambda h, i: (0, h))
    vspec = pl.BlockSpec((S, dv), lambda h, i: (0, h))
    dospec = pl.BlockSpec((T, dv), lambda h, i: (i, h))
    in_specs = [qspec, kspec, vspec, dospec]
    args = [q, k, v, do]
    if softmax:
        in_specs += [dospec, pl.BlockSpec((T, LANES), lambda h, i: (i, h))]
        args += [o, lse]
    else:
        in_specs += [pl.BlockSpec((1, 1, LANES), lambda h, i: (h, 0, 0)),
                     pl.BlockSpec((1, T, T), lambda h, i: (h, 0, 0)), pl.BlockSpec((1, T, T), lambda h, i: (h, 0, 0))]
        args += list(tables)
    return _call(body, name, [_sds((S, heads * dk), F32), _sds((S, heads * dk), F32), _sds((S, heads * dv), F32)],
                 (heads, nq), in_specs, [qspec, kspec, vspec], scratch=[pltpu.VMEM((T, dk), F32)],
                 sem=("parallel", "arbitrary"), side=side)(*args)


def _rope_tables(pos):
    def tables(dim):
        inv_freq = ROPE_THETA ** (-jnp.arange(0, dim, 2, dtype=F32) / dim)
        ang = pos.astype(F32)[:, None] * inv_freq
        return jnp.cos(ang), jnp.sin(ang)

    cm, sm = tables(ROPE)
    S = pos.shape[0]
    z32, z64 = jnp.zeros((S, 32), F32), jnp.zeros((S, 64), F32)
    cr, sr = tables(RET_DK)
    return (jnp.concatenate([cm, cm, z64], 1), jnp.concatenate([z32, sm, z64], 1),
            jnp.concatenate([-sm, z32, z64], 1), cr, sr)


def _row(v):
    return v.reshape(1, -1).astype(F32)


def _local_step(x, pos, target, pipe, P):
    tabs = _rope_tables(pos)
    dtabs = _decay_tables(ATT_BLOCK)
    xf, xb = _ln_fwd([x], [1.0], _row(P["ln_in_g"]), _row(P["ln_in_b"]), "ln_in", False)
    pipe.gather_first()
    saved = []
    for l in range(DEPTH):
        w = functools.partial(pipe.weight, l)
        t = f"_l{l}"
        h = pipe.run(_matmul, "mm_h" + t, xb, w("w_in"))
        qn, kvn, kr, rq, rk, rv = _prep1(h, tabs, _row(P["q_norm_g"][l]), _row(P["kv_norm_g"][l]), "prep1" + t)
        q = _matmul(qn, w("w_uq"), "mm_q" + t)
        kv = _matmul(kvn, w("w_ukv"), "mm_kv" + t)
        qm, km, vm = _prep2(q, kv, kr, tabs, "prep2" + t)
        a, lse = pipe.run(_attn_fwd, "mla_fwd" + t, qm, km, vm, MLA_HEADS, HEAD_PAD, VDIM, True)
        o = pipe.run(_attn_fwd, "ret_fwd" + t, rq, rk, rv, RET_HEADS, RET_DK, RET_DV, False, tables=dtabs)
        mixin = _gn_gate(a, o, h, _row(P["ret_gn_g"][l]), _row(P["ret_gn_b"][l]), "gn_gate" + t)
        mix = _matmul(mixin, w("w_out"), "mm_mix" + t)
        z1, x1f, x1b = _ln_fwd([xf, mix], [ALPHA, 1.0], _row(P["ln1_g"][l]), _row(P["ln1_b"][l]), "ln1" + t, True)
        gu, act = pipe.run(_matmul_swiglu, "mm_gu" + t, x1b, w("w_gu"))
        f = pipe.run(_matmul, "mm_down" + t, act, w("w_down"))
        z2, x2f, x2b = _ln_fwd([x1f, f], [ALPHA, 1.0], _row(P["ln2_g"][l]), _row(P["ln2_b"][l]), "ln2" + t, True)
        saved.append(dict(xb=xb, h=h, qn=qn, kvn=kvn, rq=rq, rk=rk, rv=rv, qm=qm, km=km, vm=vm, a=a, lse=lse, o=o,
                          mixin=mixin, z1=z1, x1b=x1b, gu=gu, act=act, z2=z2))
        xf, xb = x2f, x2b

    dy, sqerr = _loss_head(xf, target, "loss_head")
    dP = {}
    dys, coefs = [dy], [1.0]
    for l in reversed(range(DEPTH)):
        w, sv = functools.partial(pipe.weight, l), saved[l]
        t = f"_l{l}"
        dz2, dz2b, dg, db = _ln_bwd(dys, coefs, sv["z2"], _row(P["ln2_g"][l]), "ln2_bwd" + t)
        dP[("ln2_g", l)], dP[("ln2_b", l)] = dg, db
        pipe.reduce(l, w_down=pipe.run(_matmul, "mm_dw_down" + t, sv["act"], dz2b, ta=True, out_dtype=BF16))
        dact = _matmul(dz2b, w("w_down"), "mm_dact" + t, tb=True)
        dgu = _swiglu_bwd(sv["gu"], dact, "swiglu_bwd" + t)
        pipe.reduce(l, w_gu=pipe.run(_matmul, "mm_dw_gu" + t, sv["x1b"], dgu, ta=True, out_dtype=BF16))
        dx1 = pipe.run(_matmul, "mm_dx1" + t, dgu, w("w_gu"), tb=True)
        dz1, dz1b, dg, db = _ln_bwd([dz2, dx1], [ALPHA, 1.0], sv["z1"], _row(P["ln1_g"][l]), "ln1_bwd" + t)
        dP[("ln1_g", l)], dP[("ln1_b", l)] = dg, db
        pipe.reduce(l, w_out=_matmul(sv["mixin"], dz1b, "mm_dw_out" + t, ta=True, out_dtype=BF16))
        dmixin = _matmul(dz1b, w("w_out"), "mm_dmixin" + t, tb=True)
        do, drg, dgg, dgb = _gn_gate_bwd(dmixin, sv["o"], sv["h"], _row(P["ret_gn_g"][l]), _row(P["ret_gn_b"][l]),
                                         "gn_gate_bwd" + t)
        dP[("ret_gn_g", l)], dP[("ret_gn_b", l)] = dgg, dgb
        drq, drk, drv = _attn_bwd(sv["rq"], sv["rk"], sv["rv"], do, RET_HEADS, RET_DK, RET_DV, False, "ret_bwd" + t,
                                  tables=dtabs)
        dqm, dkm, dvm = pipe.run(_attn_bwd, "mla_bwd" + t, sv["qm"], sv["km"], sv["vm"], dmixin, MLA_HEADS, HEAD_PAD,
                                 VDIM, True, o=sv["a"], lse=sv["lse"])
        dq, dkv, dkr = _prep2_bwd(dqm, dkm, dvm, tabs, "prep2_bwd" + t)
        g_uq = _matmul(sv["qn"], dq, "mm_dw_uq" + t, ta=True, out_dtype=BF16)
        dqn = _matmul(dq, w("w_uq"), "mm_dqn" + t, tb=True)
        g_ukv = _matmul(sv["kvn"], dkv, "mm_dw_ukv" + t, ta=True, out_dtype=BF16)
        dkvn = _matmul(dkv, w("w_ukv"), "mm_dkvn" + t, tb=True)
        dh, dqg, dkvg = _prep1_bwd(dqn, dkvn, dkr, drq, drk, drv, drg, sv["h"], tabs, _row(P["q_norm_g"][l]),
                                   _row(P["kv_norm_g"][l]), "prep1_bwd" + t)
        dP[("q_norm_g", l)], dP[("kv_norm_g", l)] = dqg, dkvg
        pipe.reduce(l, w_uq=g_uq, w_ukv=g_ukv, w_in=_matmul(sv["xb"], dh, "mm_dw_in" + t, ta=True, out_dtype=BF16))
        dxl = pipe.run(_matmul, "mm_dxl" + t, dh, w("w_in"), tb=True)
        dys, coefs = [dz1, dxl], [ALPHA, 1.0]
    grad_x, _, dg, db = _ln_bwd(dys, coefs, x, _row(P["ln_in_g"]), "ln_in_bwd")
    dP[("ln_in_g", None)], dP[("ln_in_b", None)] = dg, db
    return sqerr, grad_x, dP


INTERNAL_OF = {"w_in": ("w_in",), "w_uq": ("w_uq",), "w_ukv": ("w_ukv",), "w_out": ("w_out",),
               "w_gu": ("w_gate", "w_up"), "w_down": ("w_down",)}


def _internal_weight(name, *blocks):
    cat = lambda parts: jnp.concatenate(parts, axis=1)
    cols = lambda b: cat([b[j] for j in range(N_CHIPS)])
    b = blocks[0]
    if name in ("w_out", "w_down"):
        return b.reshape(-1, b.shape[-1])
    if name == "w_gu":
        return cat([blk[j] for j in range(N_CHIPS) for blk in blocks])
    if name == "w_in":
        return cat([b[0][:, :MLA_IN_USED], jnp.zeros((D_MODEL, MLA_IN - MLA_IN_USED), BF16), b[0][:, MLA_IN_USED:]]
                   + [b[j] for j in range(1, N_CHIPS)])
    if name == "w_uq":
        uq, hw = cols(b), NOPE + ROPE
        pad = jnp.zeros((Q_LORA, HEAD_PAD - hw), BF16)
        return cat([p for h in range(MLA_HEADS) for p in (uq[:, h * hw:(h + 1) * hw], pad)])
    ukv = cols(b)
    return cat([ukv[:, 256 * h:256 * h + NOPE] for h in range(MLA_HEADS)]
               + [ukv[:, 256 * h + NOPE:256 * h + 256] for h in range(MLA_HEADS)])


def _grad_shards(name, g):
    cat = lambda parts: jnp.concatenate(parts, axis=1)
    if name in ("w_out", "w_down"):
        return {name: g.reshape(N_CHIPS, -1, g.shape[-1])}
    if name == "w_gu":
        return {"w_gate": [g[:, 2 * GU_BLOCK * j:2 * GU_BLOCK * j + GU_BLOCK] for j in range(N_CHIPS)],
                "w_up": [g[:, 2 * GU_BLOCK * j + GU_BLOCK:2 * GU_BLOCK * (j + 1)] for j in range(N_CHIPS)]}
    if name == "w_in":
        ci, shift = BIG_SHARD["w_in"][1], MLA_IN - MLA_IN_USED
        return {name: [cat([g[:, :MLA_IN_USED], g[:, MLA_IN:ci + shift]])]
                + [g[:, ci * j + shift:ci * (j + 1) + shift] for j in range(1, N_CHIPS)]}
    if name == "w_uq":
        cq = NOPE + ROPE
        return {name: [cat([g[:, HEAD_PAD * h:HEAD_PAD * h + cq] for h in (2 * j, 2 * j + 1)]) for j in range(N_CHIPS)]}
    return {name: [cat([g[:, o + NOPE * h:o + NOPE * (h + 1)] for h in (2 * j, 2 * j + 1) for o in (0, MLA_HEADS * NOPE)])
                   for j in range(N_CHIPS)]}


def _small_layout(P):
    out, at = {}, 0
    for n in SMALL:
        out[n] = (at, P[n].size)
        at += P[n].size
    return out, at


def _flatten_small(P, last):
    v = jnp.concatenate([P[n].reshape(-1).astype(F32) for n in SMALL] + [last.reshape(-1).astype(F32)])
    return jnp.pad(v, (0, SMALL_ROWS * FLAT_W - v.size)).reshape(SMALL_ROWS, FLAT_W)


def _place():
    return lax.axis_index("x"), lax.axis_index("y"), lax.axis_index("c")


def _other_chips(x, y):
    return [(1 - x, y), (x, 1 - y), (1 - x, 1 - y)]


def _rcopy(src, dst, ssem, rsem, dev):
    return pltpu.make_async_remote_copy(src_ref=src, dst_ref=dst, send_sem=ssem, recv_sem=rsem, device_id=dev,
                                        device_id_type=MESH)


def _comm_call(body, name, out_shape, n_in, scratch):
    many = isinstance(out_shape, (list, tuple))
    return pl.pallas_call(body, name=name, out_shape=out_shape, in_specs=[HBM] * n_in,
                          out_specs=[HBM] * len(out_shape) if many else HBM, scratch_shapes=scratch)


def _half(ref, which):
    rows = ref.shape[0] // 2
    return ref.at[pl.ds(pl.multiple_of(which * rows, 16), rows)]


def _dma_sems(n):
    return pltpu.SemaphoreType.DMA((n,))


def _allgather_side(ws):
    k = len(ws)

    def peers():
        x, y, c = _place()
        return c, 2 * x + y, (x, y, 1 - c), [(n, t, cx, cy) for n in range(k) for t, (cx, cy) in enumerate(_other_chips(x, y))]

    def outgoing(w_refs, g_refs, sems):
        ssem, rsem, _, _, ossem, orsem = sems
        c, j, sib, nt = peers()
        owns = [_rcopy(w_refs[n], g_refs[n].at[j], ossem.at[n], orsem.at[n], sib) for n in range(k)]
        sends = [_rcopy(_half(w_refs[n], c), _half(g_refs[n].at[j], c), ssem.at[3 * n + t], rsem.at[3 * n + t],
                        (cx, cy, c)) for n, t, cx, cy in nt]
        return owns, sends

    def incoming(g_refs, sems):
        ssem, rsem, fssem, frsem, _, _ = sems
        c, _, sib, nt = peers()
        landed, passed, relayed = [], [], []
        for n, t, cx, cy in nt:
            mine, other = (_half(g_refs[n].at[2 * cx + cy], h) for h in (c, 1 - c))
            landed.append(_rcopy(mine, mine, ssem.at[3 * n + t], rsem.at[3 * n + t], (cx, cy, c)))
            passed.append(_rcopy(mine, mine, fssem.at[3 * n + t], frsem.at[3 * n + t], sib))
            relayed.append(_rcopy(other, other, fssem.at[3 * n + t], frsem.at[3 * n + t], sib))
        return landed, passed, relayed

    def start(w_refs, g_refs, sems):
        owns, sends = outgoing(w_refs, g_refs, sems)
        for cp in sends + owns:
            cp.start()

    def finish(w_refs, g_refs, sems):
        owns, sends = outgoing(w_refs, g_refs, sems)
        landed, passed, relayed = incoming(g_refs, sems)
        for got, on in zip(landed, passed):
            got.wait_recv()
            on.start()
        for cp in relayed:
            cp.wait_recv()
        for cp in owns:
            cp.wait()
        for cp in sends + passed:
            cp.wait_send()

    return _Side(list(ws), [_sds((N_CHIPS,) + w.shape, w.dtype) for w in ws],
                 [_dma_sems(3 * k)] * 4 + [_dma_sems(k)] * 2, start, finish)


def _exchange_side(parts):
    k = len(parts)

    def copies(p_refs, rcv_refs, sems):
        ssem, rsem = sems
        x, y, c = _place()
        return [_rcopy(p_refs[n].at[2 * cx + cy], rcv_refs[n].at[t], ssem.at[3 * n + t], rsem.at[3 * n + t], (cx, cy, c))
                for n in range(k) for t, (cx, cy) in enumerate(_other_chips(x, y))]

    def start(p_refs, rcv_refs, sems):
        for cp in copies(p_refs, rcv_refs, sems):
            cp.start()

    def finish(p_refs, rcv_refs, sems):
        for cp in copies(p_refs, rcv_refs, sems):
            cp.wait()

    return _Side(list(parts), [_sds((3,) + p.shape[1:], p.dtype) for p in parts], [_dma_sems(3 * k)] * 2, start, finish)


def _run_side(side, name):
    k_in, k_out = len(side.arrays), len(side.out_shape)

    def body(*refs):
        parts = refs[:k_in], refs[k_in:k_in + k_out], refs[k_in + k_out:]
        side.start(*parts)
        side.finish(*parts)

    return _comm_call(body, name, list(side.out_shape), k_in, list(side.scratch))(*side.arrays)


def _swap_halves(gds, name):
    k = len(gds)

    def body(*refs):
        gd_refs, out_refs, (ssem, rsem) = refs[:k], refs[k:2 * k], refs[2 * k:]
        x, y, c = _place()
        cps = [_rcopy(_half(gd_refs[n].at[jj], 1 - c), out_refs[n].at[jj], ssem.at[N_CHIPS * n + jj],
                      rsem.at[N_CHIPS * n + jj], (x, y, 1 - c)) for n in range(k) for jj in range(N_CHIPS)]
        for cp in cps:
            cp.start()
        for cp in cps:
            cp.wait()

    sems = _dma_sems(N_CHIPS * k)
    return _comm_call(body, name, [_sds((N_CHIPS, g.shape[1] // 2, g.shape[2]), g.dtype) for g in gds], k,
                      [sems, sems])(*gds)


def _allreduce_small(small):
    def body(s_ref, all_ref, sssem, srsem, lsem):
        x, y, c = _place()
        me = 4 * x + 2 * y + c
        own = pltpu.make_async_copy(s_ref, all_ref.at[me], lsem)
        own.start()
        cps = []
        for r in range(1, 8):
            fx, fy, fc = (r >> 2) & 1, (r >> 1) & 1, r & 1
            px, py, pc = (1 - x if fx else x, 1 - y if fy else y, 1 - c if fc else c)
            peer = 4 * px + 2 * py + pc
            send = _rcopy(s_ref, all_ref.at[me], sssem.at[r - 1], srsem.at[me], (px, py, pc))
            send.start()
            cps.append((send, _rcopy(s_ref, all_ref.at[peer], sssem.at[r - 1], srsem.at[peer], (px, py, pc))))
        for send, recv in cps:
            send.wait_send()
            recv.wait_recv()
        own.wait()

    return _comm_call(body, "allreduce_small", [_sds((8,) + small.shape, small.dtype)], 1,
                      [pltpu.SemaphoreType.DMA((7,)), pltpu.SemaphoreType.DMA((8,)), pltpu.SemaphoreType.DMA(())])(small)[0]


def _share_halves(reds, name):
    k = len(reds)

    def body(*refs):
        r_refs, out_refs, (ssem, rsem) = refs[:k], refs[k:2 * k], refs[2 * k:]
        x, y, c = _place()
        cps = [_rcopy(r_refs[n], out_refs[n], ssem.at[n], rsem.at[n], (x, y, 1 - c)) for n in range(k)]
        for cp in cps:
            cp.start()
        for cp in cps:
            cp.wait()

    return _comm_call(body, name, [_sds(r.shape, r.dtype) for r in reds], k, [_dma_sems(k)] * 2)(*reds)


def _add_pair(gd, got, c, name):
    _, R, W = got.shape
    tm = _pick(R, (512, 256, 128, 64))
    nb = R // tm

    def body(c_ref, a_ref, b_ref, o_ref):
        o_ref[...] = (a_ref[...].astype(F32) + b_ref[...].astype(F32)).astype(o_ref.dtype)

    grid_spec = pltpu.PrefetchScalarGridSpec(
        num_scalar_prefetch=1, grid=(N_CHIPS, nb),
        in_specs=[pl.BlockSpec((None, tm, W), lambda j, i, c_ref: (j, c_ref[0] * nb + i, 0)),
                  pl.BlockSpec((None, tm, W), lambda j, i, c_ref: (j, i, 0))],
        out_specs=pl.BlockSpec((None, tm, W), lambda j, i, c_ref: (j, i, 0)))
    return pl.pallas_call(body, name=name, grid_spec=grid_spec, out_shape=_sds((N_CHIPS, R, W), gd.dtype),
                          compiler_params=pltpu.CompilerParams(dimension_semantics=("parallel", "parallel"),
                                                               vmem_limit_bytes=VMEM_LIMIT))(c, gd, got)


def _add_chips(part, rcv, j, name):
    _, R, W = part.shape
    tm = _pick(R, (512, 256, 128, 64))

    def body(j_ref, p_ref, r0_ref, r1_ref, r2_ref, o_ref):
        o_ref[...] = ((p_ref[...].astype(F32) + r0_ref[...].astype(F32)) + r1_ref[...].astype(F32)) + r2_ref[...].astype(F32)

    def slot(t):
        return pl.BlockSpec((None, tm, W), lambda i, j_ref: (t, i, 0))

    grid_spec = pltpu.PrefetchScalarGridSpec(
        num_scalar_prefetch=1, grid=(R // tm,),
        in_specs=[pl.BlockSpec((None, tm, W), lambda i, j_ref: (j_ref[0], i, 0)), slot(0), slot(1), slot(2)],
        out_specs=pl.BlockSpec((tm, W), lambda i, j_ref: (i, 0)))
    return pl.pallas_call(body, name=name, grid_spec=grid_spec, out_shape=_sds((R, W), F32),
                          compiler_params=pltpu.CompilerParams(dimension_semantics=("parallel",),
                                                               vmem_limit_bytes=VMEM_LIMIT))(j, part, rcv, rcv, rcv)


def _sum_small(allsmall):
    _, R, W = allsmall.shape

    def body(a_ref, o_ref):
        acc = a_ref[0]
        for d in range(1, 8):
            acc = acc + a_ref[d]
        o_ref[...] = acc

    return _call(body, "sum_small", _sds((R, W), F32), (1,), [_whole((8, R, W))], _whole((R, W)),
                 sem=("arbitrary",))(allsmall)


def _adamw(w, g, m, v, name):
    R, C = w.shape
    tm = _pick(R, (256, 128, 64, 32, 8))

    def body(w_ref, g_ref, m_ref, v_ref, d_ref, mo_ref, vo_ref):
        gv = g_ref[...]
        mn = ADAM_B1 * m_ref[...] + (1.0 - ADAM_B1) * gv
        vn = ADAM_B2 * v_ref[...] + (1.0 - ADAM_B2) * (gv * gv)
        m_hat = mn / (1.0 - ADAM_B1 ** ADAM_STEP)
        v_hat = vn / (1.0 - ADAM_B2 ** ADAM_STEP)
        d_ref[...] = -ADAM_LR * (m_hat / (jnp.sqrt(v_hat) + ADAM_EPS) + ADAM_WD * w_ref[...])
        mo_ref[...] = mn
        vo_ref[...] = vn

    spec = _rows(tm, C)
    return _call(body, name, [_sds((R, C), F32)] * 3, (R // tm,), [spec] * 4, [spec] * 3, sem=("parallel",))(w, g, m, v)


def _adamw_layer(c, w, m, v, mine, other, l, prev, name):
    _, R, C = w.shape
    half = R // 2
    tm = _pick(half, (256, 128, 64))
    nbh = half // tm

    def body(c_ref, w_ref, m_ref, v_ref, a_ref, b_ref, *rest):
        g_ref, d_ref, mo_ref, vo_ref = rest[-4:]
        gv = jnp.where(pl.program_id(0) // nbh == c_ref[0], a_ref[...], b_ref[...])
        mn = ADAM_B1 * m_ref[...] + (1.0 - ADAM_B1) * gv
        vn = ADAM_B2 * v_ref[...] + (1.0 - ADAM_B2) * (gv * gv)
        m_hat = mn / (1.0 - ADAM_B1 ** ADAM_STEP)
        v_hat = vn / (1.0 - ADAM_B2 ** ADAM_STEP)
        g_ref[...] = gv
        d_ref[...] = -ADAM_LR * (m_hat / (jnp.sqrt(v_hat) + ADAM_EPS) + ADAM_WD * w_ref[...])
        mo_ref[...] = mn
        vo_ref[...] = vn

    layer = pl.BlockSpec((None, tm, C), lambda i, c_ref: (l, i, 0))
    halfspec = pl.BlockSpec((tm, C), lambda i, c_ref: (i % nbh, 0))
    n_prev = 0 if prev is None else 4
    grid_spec = pltpu.PrefetchScalarGridSpec(
        num_scalar_prefetch=1, grid=(R // tm,),
        in_specs=[layer] * 3 + [halfspec] * 2 + [pl.BlockSpec(memory_space=pl.ANY)] * n_prev,
        out_specs=[layer] * 4)
    return pl.pallas_call(body, name=name, grid_spec=grid_spec, out_shape=[_sds(w.shape, F32)] * 4,
                          input_output_aliases={6 + k: k for k in range(n_prev)},
                          compiler_params=pltpu.CompilerParams(dimension_semantics=("parallel",),
                                                               vmem_limit_bytes=VMEM_LIMIT))(
        c, w, m, v, mine, other, *(prev or ()))


FIRST_GATHER = ("w_in", "w_uq", "w_ukv")
GATHER_IN = {
    "mm_h_l0": (0, ("w_out",)), "mla_fwd_l0": (0, ("w_gate", "w_up")), "ret_fwd_l0": (1, ("w_uq", "w_ukv", "w_out")),
    "mm_gu_l0": (0, ("w_down",)), "mm_down_l0": (1, ("w_in",)),
    "mla_fwd_l1": (1, ("w_gate", "w_up")), "mm_gu_l1": (1, ("w_down",))}
EXCHANGE_IN = {
    "mm_dw_gu_l1": (1, ("w_down",)), "mm_dx1_l1": (1, ("w_gate",)), "mla_bwd_l1": (1, ("w_up", "w_out")),
    "mm_dw_down_l0": (1, ("w_uq", "w_ukv", "w_in")),
    "mm_dw_gu_l0": (0, ("w_down",)), "mm_dx1_l0": (0, ("w_gate",)), "mla_bwd_l0": (0, ("w_up", "w_out")),
    "mm_dxl_l0": (0, ("w_uq", "w_ukv", "w_in"))}


class _Pipeline:
    def __init__(self, own, Wt, Mo, Vo, core, chip):
        self.own, self.Wt, self.Mo, self.Vo, self.core, self.chip = own, Wt, Mo, Vo, core, chip
        self.blocks, self.whole, self.parts = {}, {}, {}
        self.results = {n: None for n in BIG}

    def _gathered(self, l, names, blocks):
        for n, b in zip(names, blocks):
            self.blocks[(l, n)] = b

    def gather_first(self):
        side = _allgather_side([self.own[0][n] for n in FIRST_GATHER])
        self._gathered(0, FIRST_GATHER, _run_side(side, "allgather_first"))

    def weight(self, l, name):
        if (l, name) not in self.whole:
            self.whole[(l, name)] = _internal_weight(name, *[self.blocks[(l, n)] for n in INTERNAL_OF[name]])
        return self.whole[(l, name)]

    def run(self, fn, name, *args, **kw):
        if name in GATHER_IN:
            gl, names = GATHER_IN[name]
            out, blocks = fn(*args, name=name, side=_allgather_side([self.own[gl][n] for n in names]), **kw)
            self._gathered(gl, names, blocks)
            return out
        if name in EXCHANGE_IN:
            gl, names = EXCHANGE_IN[name]
            out, rcvs = fn(*args, name=name, side=_exchange_side([self.parts[(gl, n)] for n in names]), **kw)
            self._reduced(gl, names, rcvs)
            return out
        return fn(*args, name=name, **kw)

    def reduce(self, l, **grads):
        shards = {}
        for name, g in grads.items():
            shards.update(_grad_shards(name, g))
        names = list(shards)
        gds = [shards[n] if hasattr(shards[n], "shape") else jnp.stack(shards[n]) for n in names]
        got = _swap_halves(gds, f"swap_halves_{names[0]}_l{l}")
        for n, gd, gt in zip(names, gds, got):
            self.parts[(l, n)] = _add_pair(gd, gt, self.core, f"add_pair_{n}_l{l}")

    def _reduced(self, l, names, rcvs):
        reds = [_add_chips(self.parts[(l, n)], rcv, self.chip, f"add_chips_{n}_l{l}") for n, rcv in zip(names, rcvs)]
        others = _share_halves(reds, f"share_halves_{names[0]}_l{l}")
        for n, red, other in zip(names, reds, others):
            self.results[n] = _adamw_layer(self.core, self.Wt[n], self.Mo[n], self.Vo[n], red, other, l,
                                           self.results[n], f"adamw_{n}_l{l}")


def kernel(x, positions, ln_in_g, ln_in_b, w_in, q_norm_g, kv_norm_g, w_uq, w_ukv, ret_gn_g, ret_gn_b, w_out, ln1_g, ln1_b, w_gate, w_up, w_down, ln2_g, ln2_b, loss_target, m_ln_in_g, m_ln_in_b, m_w_in, m_q_norm_g, m_kv_norm_g, m_w_uq, m_w_ukv, m_ret_gn_g, m_ret_gn_b, m_w_out, m_ln1_g, m_ln1_b, m_w_gate, m_w_up, m_w_down, m_ln2_g, m_ln2_b, v_ln_in_g, v_ln_in_b, v_w_in, v_q_norm_g, v_kv_norm_g, v_w_uq, v_w_ukv, v_ret_gn_g, v_ret_gn_b, v_w_out, v_ln1_g, v_ln1_b, v_w_gate, v_w_up, v_w_down, v_ln2_g, v_ln2_b):
    given = dict(locals())
    Wt = {n: given[n] for n in WEIGHTS}
    Mo = {n: given["m_" + n] for n in WEIGHTS}
    Vo = {n: given["v_" + n] for n in WEIGHTS}
    cx, cy, cc = _place()
    chip = (2 * cx + cy).astype(jnp.int32)
    core = cc.astype(jnp.int32)

    own = [{n: Wt[n][l].astype(BF16) for n in BIG} for l in range(DEPTH)]
    pipe = _Pipeline(own, Wt, Mo, Vo, core.reshape(1), chip.reshape(1))
    sqerr, grad_x, dP = _local_step(x[0], positions[0], loss_target[0], pipe, Wt)
    results = pipe.results

    small_g = {n: (dP[(n, None)] if Wt[n].ndim == 1 else jnp.stack([dP[(n, l)] for l in range(DEPTH)])) for n in SMALL}
    local_loss = 0.5 * jnp.sum(sqerr) / D_MODEL
    small_sum = _sum_small(_allreduce_small(_flatten_small(small_g, local_loss))).reshape(-1)
    layout, n_small = _small_layout(Wt)
    loss = small_sum[n_small]

    grads, deltas, new_m, new_v = {}, {}, {}, {}
    for n in BIG:
        grads[n], deltas[n], new_m[n], new_v[n] = results[n]
    zero = jnp.zeros((), F32)
    d, mn, vn = _adamw(_flatten_small(Wt, zero), small_sum.reshape(SMALL_ROWS, FLAT_W), _flatten_small(Mo, zero),
                       _flatten_small(Vo, zero), "adamw_small")
    for n in SMALL:
        at, size = layout[n]
        pick = lambda a: a.reshape(-1)[at:at + size].reshape(Wt[n].shape)
        grads[n], deltas[n], new_m[n], new_v[n] = pick(small_sum), pick(d), pick(mn), pick(vn)

    return (loss, grad_x[None], *[grads[n] for n in WEIGHTS], *[deltas[n] for n in WEIGHTS],
            *[new_m[n] for n in WEIGHTS], *[new_v[n] for n in WEIGHTS])
```

```python
import functools

import jax
import jax.numpy as jnp
from jax import lax
from jax.experimental import pallas as pl
from jax.experimental.pallas import tpu as pltpu

F32 = jnp.float32
BF16 = jnp.bfloat16

D_MODEL = 2048
DEPTH = 2
CHUNK = 64
MLA_HEADS = 8
Q_LORA = 512
KV_LORA = 256
NOPE = 128
ROPE = 64
VDIM = 128
RET_HEADS = 4
RET_DK = 256
RET_DV = 256
D_FF = 5632
D_IN = 4928
ROPE_THETA = 10000.0
LN_EPS = 1e-5
RMS_EPS = 1e-6
GN_EPS = 1e-5
ALPHA = (2 * DEPTH) ** 0.25
MLA_SCALE = (NOPE + ROPE) ** -0.5
RET_SCALE = RET_DK ** -0.5
ADAM_LR = 0.001
ADAM_B1 = 0.9
ADAM_B2 = 0.999
ADAM_EPS = 1e-08
ADAM_WD = 0.01
ADAM_STEP = 10

LANES = 128
HEAD_PAD = 256
MLA_IN = 1024
MLA_IN_USED = Q_LORA + KV_LORA + ROPE
D_IN_PAD = MLA_IN + 4 * 1024
ATT_BLOCK = 512
NEG = -1e30
VMEM_LIMIT = 56 * 1024 * 1024

N_CHIPS = 4
FLAT_W = 1024
BIG = ("w_in", "w_uq", "w_ukv", "w_out", "w_gate", "w_up", "w_down")
BIG_SHARD = {"w_in": (2048, 1232), "w_uq": (512, 384), "w_ukv": (256, 512), "w_out": (512, 2048),
             "w_gate": (2048, 1408), "w_up": (2048, 1408), "w_down": (1408, 2048)}
SMALL = ("ln_in_g", "ln_in_b", "q_norm_g", "kv_norm_g", "ret_gn_g", "ret_gn_b", "ln1_g", "ln1_b", "ln2_g", "ln2_b")
WEIGHTS = ("ln_in_g", "ln_in_b", "w_in", "q_norm_g", "kv_norm_g", "w_uq", "w_ukv", "ret_gn_g", "ret_gn_b", "w_out",
           "ln1_g", "ln1_b", "w_gate", "w_up", "w_down", "ln2_g", "ln2_b")
SMALL_ROWS = 32

MESH = pl.DeviceIdType.MESH


def _pick(dim, cands):
    for c in cands:
        if dim % c == 0:
            return c
    return dim


HBM = pl.BlockSpec(memory_space=pltpu.HBM)


class _Side:
    def __init__(self, arrays, out_shape, scratch, start, finish):
        self.arrays, self.out_shape, self.scratch, self.start, self.finish = arrays, out_shape, scratch, start, finish


def _call(body, name, out_shape, grid, in_specs, out_specs, scratch=(), sem=None, side=None):
    params = pltpu.CompilerParams(dimension_semantics=sem if side is None else ("arbitrary",) * len(grid),
                                  vmem_limit_bytes=VMEM_LIMIT)
    if side is None:
        return pl.pallas_call(body, name=name, out_shape=out_shape, grid=grid, in_specs=in_specs, out_specs=out_specs,
                              scratch_shapes=list(scratch), compiler_params=params)
    single = not isinstance(out_shape, (list, tuple))
    outs = [out_shape] if single else list(out_shape)
    ospecs = [out_specs] if single else list(out_specs)
    cuts = [len(in_specs), len(side.arrays), len(outs), len(side.out_shape), len(scratch)]
    ends = [sum(cuts[:k + 1]) for k in range(len(cuts))]

    def hosted(*refs):
        ins, s_in, o, s_out, scr = (refs[a:b] for a, b in zip([0] + ends[:-1], ends))
        sems = refs[ends[-1]:]
        ids = [pl.program_id(a) for a in range(len(grid))]
        first = functools.reduce(jnp.logical_and, [i == 0 for i in ids])
        last = functools.reduce(jnp.logical_and, [i == g - 1 for i, g in zip(ids, grid)])

        @pl.when(first)
        def _():
            side.start(s_in, s_out, sems)

        body(*ins, *o, *scr)

        @pl.when(last)
        def _():
            side.finish(s_in, s_out, sems)

    call = pl.pallas_call(hosted, name=name, out_shape=outs + list(side.out_shape), grid=grid,
                          in_specs=list(in_specs) + [HBM] * len(side.arrays),
                          out_specs=ospecs + [HBM] * len(side.out_shape),
                          scratch_shapes=list(scratch) + list(side.scratch), compiler_params=params)

    def run(*args):
        res = call(*args, *side.arrays)
        return (res[0] if single else list(res[:len(outs)])), list(res[len(outs):])

    return run


def _rows(tm, w, col=0):
    return pl.BlockSpec((tm, w), lambda i: (i, col))


def _whole(shape):
    return pl.BlockSpec(shape, lambda i: (0,) * len(shape))


def _sds(shape, dtype):
    return jax.ShapeDtypeStruct(shape, dtype)


def _matmul(a, b, name, ta=False, tb=False, out_dtype=F32, side=None):
    (K, M) = a.shape if ta else a.shape[::-1]
    (N, Kb) = b.shape if tb else b.shape[::-1]
    assert K == Kb, (a.shape, b.shape, ta, tb)
    tm = _pick(M, (1024, 1408, 512, 256, 128))
    tn = _pick(N, (1024, 512, 256, 128))
    tk = _pick(K, (2816, 2560, 2048, 1024, 512, 256))
    nk = K // tk
    dn = (((0 if ta else 1,), (1 if tb else 0,)), ((), ()))

    def body(a_ref, b_ref, o_ref, acc_ref):
        k = pl.program_id(2)
        if nk == 1:
            o_ref[...] = lax.dot_general(a_ref[...].astype(BF16), b_ref[...].astype(BF16), dn,
                                         preferred_element_type=F32).astype(out_dtype)
        else:
            @pl.when(k == 0)
            def _():
                acc_ref[...] = jnp.zeros_like(acc_ref)

            acc_ref[...] += lax.dot_general(a_ref[...].astype(BF16), b_ref[...].astype(BF16), dn,
                                            preferred_element_type=F32)

            @pl.when(k == nk - 1)
            def _():
                o_ref[...] = acc_ref[...].astype(out_dtype)

    a_spec = pl.BlockSpec((tk, tm), lambda i, j, k: (k, i)) if ta else pl.BlockSpec((tm, tk), lambda i, j, k: (i, k))
    b_spec = pl.BlockSpec((tn, tk), lambda i, j, k: (j, k)) if tb else pl.BlockSpec((tk, tn), lambda i, j, k: (k, j))
    return _call(body, name, _sds((M, N), out_dtype), (M // tm, N // tn, nk), [a_spec, b_spec],
                 pl.BlockSpec((tm, tn), lambda i, j, k: (i, j)), scratch=[pltpu.VMEM((tm, tn), F32)],
                 sem=("parallel", "parallel", "arbitrary"), side=side)(a, b)


def _sigmoid(x):
    return 1.0 / (1.0 + jnp.exp(-x))


def _rope_group(r, c, sa, sb):
    return r * c + pltpu.roll(r, 32, 1) * sa + pltpu.roll(r, 96, 1) * sb


def _ln_fwd(xs, coefs, g, b, name, want_z):
    S, D = xs[0].shape
    tm = 256
    n = len(xs)

    def body(*refs):
        x_refs, g_ref, b_ref, outs = refs[:n], refs[n], refs[n + 1], refs[n + 2:]
        z = None
        for cf, r in zip(coefs, x_refs):
            t = r[...] if cf == 1.0 else cf * r[...]
            z = t if z is None else z + t
        mu = jnp.mean(z, axis=-1, keepdims=True)
        zc = z - mu
        var = jnp.mean(zc * zc, axis=-1, keepdims=True)
        y = zc * lax.rsqrt(var + LN_EPS) * g_ref[...] + b_ref[...]
        if want_z:
            outs[0][...] = z
        outs[-2][...] = y
        outs[-1][...] = y.astype(BF16)

    out_shape = [_sds((S, D), F32)] * (2 if want_z else 1) + [_sds((S, D), BF16)]
    return _call(body, name, out_shape, (S // tm,), [_rows(tm, D)] * n + [_whole((1, D))] * 2,
                 [_rows(tm, D)] * len(out_shape), sem=("parallel",))(*xs, g, b)


def _ln_bwd(dys, coefs, z, g, name):
    S, D = z.shape
    tm = 256
    n = len(dys)

    def body(*refs):
        dy_refs, z_ref, g_ref = refs[:n], refs[n], refs[n + 1]
        dz_ref, dzb_ref, dg_ref, db_ref = refs[n + 2:]
        dy = None
        for cf, r in zip(coefs, dy_refs):
            t = r[...] if cf == 1.0 else cf * r[...]
            dy = t if dy is None else dy + t
        zv = z_ref[...]
        mu = jnp.mean(zv, axis=-1, keepdims=True)
        zc = zv - mu
        var = jnp.mean(zc * zc, axis=-1, keepdims=True)
        rstd = lax.rsqrt(var + LN_EPS)
        xh = zc * rstd
        dyg = dy * g_ref[...]
        dz = rstd * (dyg - jnp.mean(dyg, axis=-1, keepdims=True) - xh * jnp.mean(dyg * xh, axis=-1, keepdims=True))
        dz_ref[...] = dz
        dzb_ref[...] = dz.astype(BF16)

        @pl.when(pl.program_id(0) == 0)
        def _():
            dg_ref[...] = jnp.zeros_like(dg_ref)
            db_ref[...] = jnp.zeros_like(db_ref)

        dg_ref[...] += jnp.sum(dy * xh, axis=0, keepdims=True)
        db_ref[...] += jnp.sum(dy, axis=0, keepdims=True)

    return _call(body, name, [_sds((S, D), F32), _sds((S, D), BF16), _sds((1, D), F32), _sds((1, D), F32)],
                 (S // tm,), [_rows(tm, D)] * (n + 1) + [_whole((1, D))],
                 [_rows(tm, D), _rows(tm, D), _whole((1, D)), _whole((1, D))], sem=("arbitrary",))(*dys, z, g)


def _rms(x, g):
    return x * lax.rsqrt(jnp.mean(x * x, axis=-1, keepdims=True) + RMS_EPS) * g


def _prep1(h, tabs, qg, kvg, name):
    S = h.shape[0]
    tm = 256
    cm, sam, sbm, cr, sr = tabs

    def body(h_ref, cm_ref, sam_ref, sbm_ref, cr_ref, sr_ref, qg_ref, kvg_ref,
             qn_ref, kvn_ref, kr_ref, rq_ref, rk_ref, rv_ref):
        qn_ref[...] = _rms(h_ref[:, 0:Q_LORA], qg_ref[...]).astype(BF16)
        kvn_ref[...] = _rms(h_ref[:, Q_LORA:Q_LORA + KV_LORA], kvg_ref[...]).astype(BF16)
        kr_ref[...] = _rope_group(h_ref[:, 768:896], cm_ref[...], sam_ref[...], sbm_ref[...])
        c, s = cr_ref[...], sr_ref[...]
        for hd in range(RET_HEADS):
            for src, dst, scale in ((MLA_IN, rq_ref, RET_SCALE), (MLA_IN + 1024, rk_ref, None)):
                t1 = h_ref[:, src + hd * 256:src + hd * 256 + 128]
                t2 = h_ref[:, src + hd * 256 + 128:src + hd * 256 + 256]
                o1, o2 = t1 * c - t2 * s, t2 * c + t1 * s
                if scale is not None:
                    o1, o2 = o1 * scale, o2 * scale
                dst[:, hd * 256:hd * 256 + 128] = o1.astype(BF16)
                dst[:, hd * 256 + 128:hd * 256 + 256] = o2.astype(BF16)
        rv_ref[...] = h_ref[:, MLA_IN + 2048:MLA_IN + 3072].astype(BF16)

    t128 = _rows(tm, LANES)
    return _call(body, name,
                 [_sds((S, Q_LORA), BF16), _sds((S, KV_LORA), BF16), _sds((S, LANES), F32),
                  _sds((S, 1024), BF16), _sds((S, 1024), BF16), _sds((S, 1024), BF16)],
                 (S // tm,),
                 [_rows(tm, D_IN_PAD), t128, t128, t128, t128, t128, _whole((1, Q_LORA)), _whole((1, KV_LORA))],
                 [_rows(tm, Q_LORA), _rows(tm, KV_LORA), t128, _rows(tm, 1024), _rows(tm, 1024), _rows(tm, 1024)],
                 sem=("parallel",))(h, cm, sam, sbm, cr, sr, qg, kvg)


def _prep1_bwd(dqn, dkvn, dkr, drq, drk, drv, drg, h, tabs, qg, kvg, name):
    S = h.shape[0]
    tm = 256
    cm, sam, sbm, cr, sr = tabs

    def rms_bwd(x, g, dy):
        r = lax.rsqrt(jnp.mean(x * x, axis=-1, keepdims=True) + RMS_EPS)
        dyg = dy * g
        dx = r * dyg - x * (r * r * r) * jnp.mean(dyg * x, axis=-1, keepdims=True)
        return dx, jnp.sum(dy * x * r, axis=0, keepdims=True)

    def body(dqn_ref, dkvn_ref, dkr_ref, drq_ref, drk_ref, drv_ref, drg_ref, h_ref,
             cm_ref, sam_ref, sbm_ref, cr_ref, sr_ref, qg_ref, kvg_ref, dh_ref, dqg_ref, dkvg_ref):
        dcq, dqg = rms_bwd(h_ref[:, 0:Q_LORA], qg_ref[...], dqn_ref[...])
        dckv, dkvg = rms_bwd(h_ref[:, Q_LORA:Q_LORA + KV_LORA], kvg_ref[...], dkvn_ref[...])
        dh_ref[:, 0:Q_LORA] = dcq.astype(BF16)
        dh_ref[:, Q_LORA:Q_LORA + KV_LORA] = dckv.astype(BF16)
        dh_ref[:, 768:896] = _rope_group(dkr_ref[...], cm_ref[...], -sam_ref[...], -sbm_ref[...]).astype(BF16)
        dh_ref[:, 896:1024] = jnp.zeros((tm, LANES), BF16)
        c, s = cr_ref[...], sr_ref[...]
        for hd in range(RET_HEADS):
            for src, dst, scale in ((drq_ref, MLA_IN, RET_SCALE), (drk_ref, MLA_IN + 1024, None)):
                d1 = src[:, hd * 256:hd * 256 + 128]
                d2 = src[:, hd * 256 + 128:hd * 256 + 256]
                if scale is not None:
                    d1, d2 = d1 * scale, d2 * scale
                dh_ref[:, dst + hd * 256:dst + hd * 256 + 128] = (d1 * c + d2 * s).astype(BF16)
                dh_ref[:, dst + hd * 256 + 128:dst + hd * 256 + 256] = (d2 * c - d1 * s).astype(BF16)
        dh_ref[:, MLA_IN + 2048:MLA_IN + 3072] = drv_ref[...].astype(BF16)
        dh_ref[:, MLA_IN + 3072:MLA_IN + 4096] = drg_ref[...].astype(BF16)

        @pl.when(pl.program_id(0) == 0)
        def _():
            dqg_ref[...] = jnp.zeros_like(dqg_ref)
            dkvg_ref[...] = jnp.zeros_like(dkvg_ref)

        dqg_ref[...] += dqg
        dkvg_ref[...] += dkvg

    t128 = _rows(tm, LANES)
    return _call(body, name,
                 [_sds((S, D_IN_PAD), BF16), _sds((1, Q_LORA), F32), _sds((1, KV_LORA), F32)],
                 (S // tm,),
                 [_rows(tm, Q_LORA), _rows(tm, KV_LORA), t128, _rows(tm, 1024), _rows(tm, 1024), _rows(tm, 1024),
                  _rows(tm, 1024), _rows(tm, MLA_IN), t128, t128, t128, t128, t128,
                  _whole((1, Q_LORA)), _whole((1, KV_LORA))],
                 [_rows(tm, D_IN_PAD), _whole((1, Q_LORA)), _whole((1, KV_LORA))],
                 sem=("arbitrary",))(dqn, dkvn, dkr, drq, drk, drv, drg, h, cm, sam, sbm, cr, sr, qg, kvg)


def _prep2(q, kv, kr, tabs, name):
    S = q.shape[0]
    tm = 256
    cm, sam, sbm = tabs[:3]

    def body(q_ref, kv_ref, kr_ref, cm_ref, sam_ref, sbm_ref, qo_ref, ko_ref, vo_ref):
        c, sa, sb = cm_ref[...], sam_ref[...], sbm_ref[...]
        krb = kr_ref[...].astype(BF16)
        for hd in range(MLA_HEADS):
            o = hd * HEAD_PAD
            qo_ref[:, o:o + 128] = q_ref[:, o:o + 128].astype(BF16)
            qo_ref[:, o + 128:o + 256] = _rope_group(q_ref[:, o + 128:o + 256], c, sa, sb).astype(BF16)
            ko_ref[:, o:o + 128] = kv_ref[:, hd * 128:hd * 128 + 128].astype(BF16)
            ko_ref[:, o + 128:o + 256] = krb
        vo_ref[...] = kv_ref[:, 1024:2048].astype(BF16)

    t128 = _rows(tm, LANES)
    return _call(body, name, [_sds((S, 2048), BF16), _sds((S, 2048), BF16), _sds((S, 1024), BF16)], (S // tm,),
                 [_rows(tm, 2048), _rows(tm, 2048), t128, t128, t128, t128],
                 [_rows(tm, 2048), _rows(tm, 2048), _rows(tm, 1024)], sem=("parallel",))(q, kv, kr, cm, sam, sbm)


def _prep2_bwd(dqm, dkm, dvm, tabs, name):
    S = dqm.shape[0]
    tm = 256
    cm, sam, sbm = tabs[:3]

    def body(dq_ref, dk_ref, dv_ref, cm_ref, sam_ref, sbm_ref, dqo_ref, dkvo_ref, dkr_ref):
        c, sa, sb = cm_ref[...], -sam_ref[...], -sbm_ref[...]
        dkr = None
        for hd in range(MLA_HEADS):
            o = hd * HEAD_PAD
            dqo_ref[:, o:o + 128] = dq_ref[:, o:o + 128].astype(BF16)
            dqo_ref[:, o + 128:o + 256] = _rope_group(dq_ref[:, o + 128:o + 256], c, sa, sb).astype(BF16)
            dkvo_ref[:, hd * 128:hd * 128 + 128] = dk_ref[:, o:o + 128].astype(BF16)
            t = dk_ref[:, o + 128:o + 256]
            dkr = t if dkr is None else dkr + t
        dkvo_ref[:, 1024:2048] = dv_ref[...].astype(BF16)
        dkr_ref[...] = dkr

    t128 = _rows(tm, LANES)
    return _call(body, name, [_sds((S, 2048), BF16), _sds((S, 2048), BF16), _sds((S, LANES), F32)], (S // tm,),
                 [_rows(tm, 2048), _rows(tm, 2048), _rows(tm, 1024), t128, t128, t128],
                 [_rows(tm, 2048), _rows(tm, 2048), t128], sem=("parallel",))(dqm, dkm, dvm, cm, sam, sbm)


def _gn_gate(a, o, h, gg, gb, name):
    S = a.shape[0]
    tm = 256

    def body(a_ref, o_ref, rg_ref, gg_ref, gb_ref, mix_ref):
        mix_ref[:, 0:1024] = a_ref[...].astype(BF16)
        for hd in range(RET_HEADS):
            sl = slice(hd * 256, hd * 256 + 256)
            ov = o_ref[:, sl]
            mu = jnp.mean(ov, axis=-1, keepdims=True)
            oc = ov - mu
            var = jnp.mean(oc * oc, axis=-1, keepdims=True)
            y = oc * lax.rsqrt(var + GN_EPS) * gg_ref[:, sl] + gb_ref[:, sl]
            rg = rg_ref[:, sl]
            mix_ref[:, 1024 + hd * 256:1024 + hd * 256 + 256] = (rg * _sigmoid(rg) * y).astype(BF16)

    return _call(body, name, _sds((S, 2048), BF16), (S // tm,),
                 [_rows(tm, 1024), _rows(tm, 1024), _rows(tm, 1024, 4), _whole((1, 1024)), _whole((1, 1024))],
                 _rows(tm, 2048), sem=("parallel",))(a, o, h, gg, gb)


def _gn_gate_bwd(dmixin, o, h, gg, gb, name):
    S = o.shape[0]
    tm = 256

    def body(dr_ref, o_ref, rg_ref, gg_ref, gb_ref, do_ref, drg_ref, dgg_ref, dgb_ref):
        @pl.when(pl.program_id(0) == 0)
        def _():
            dgg_ref[...] = jnp.zeros_like(dgg_ref)
            dgb_ref[...] = jnp.zeros_like(dgb_ref)

        for hd in range(RET_HEADS):
            sl = slice(hd * 256, hd * 256 + 256)
            ov = o_ref[:, sl]
            mu = jnp.mean(ov, axis=-1, keepdims=True)
            oc = ov - mu
            var = jnp.mean(oc * oc, axis=-1, keepdims=True)
            rstd = lax.rsqrt(var + GN_EPS)
            xh = oc * rstd
            g = gg_ref[:, sl]
            y = xh * g + gb_ref[:, sl]
            rg = rg_ref[:, sl]
            sg = _sigmoid(rg)
            dr = dr_ref[:, sl]
            dy = dr * (rg * sg)
            drg_ref[:, sl] = dr * y * (sg * (1.0 + rg * (1.0 - sg)))
            dgg_ref[:, sl] += jnp.sum(dy * xh, axis=0, keepdims=True)
            dgb_ref[:, sl] += jnp.sum(dy, axis=0, keepdims=True)
            dxh = dy * g
            do = rstd * (dxh - jnp.mean(dxh, axis=-1, keepdims=True) - xh * jnp.mean(dxh * xh, axis=-1, keepdims=True))
            do_ref[:, sl] = do.astype(BF16)

    return _call(body, name,
                 [_sds((S, 1024), BF16), _sds((S, 1024), F32), _sds((1, 1024), F32), _sds((1, 1024), F32)],
                 (S // tm,),
                 [_rows(tm, 1024, 1), _rows(tm, 1024), _rows(tm, 1024, 4), _whole((1, 1024)), _whole((1, 1024))],
                 [_rows(tm, 1024), _rows(tm, 1024), _whole((1, 1024)), _whole((1, 1024))],
                 sem=("arbitrary",))(dmixin, o, h, gg, gb)


GU_BLOCK = D_FF // N_CHIPS


def _matmul_swiglu(x, w_gu, name, side=None):
    S, K = x.shape
    tm = _pick(S, (512, 256, 128))
    tn = 2 * GU_BLOCK

    def body(x_ref, w_ref, gu_ref, act_ref):
        r = jnp.dot(x_ref[...], w_ref[...], preferred_element_type=F32)
        g, u = r[:, :GU_BLOCK], r[:, GU_BLOCK:]
        gu_ref[...] = r.astype(BF16)
        act_ref[...] = (g * _sigmoid(g) * u).astype(BF16)

    return _call(body, name, [_sds((S, 2 * D_FF), BF16), _sds((S, D_FF), BF16)], (S // tm, N_CHIPS),
                 [pl.BlockSpec((tm, K), lambda i, j: (i, 0)), pl.BlockSpec((K, tn), lambda i, j: (0, j))],
                 [pl.BlockSpec((tm, tn), lambda i, j: (i, j)), pl.BlockSpec((tm, GU_BLOCK), lambda i, j: (i, j))],
                 sem=("parallel", "parallel"), side=side)(x, w_gu)


def _swiglu_bwd(gu, dact, name):
    S = gu.shape[0]
    tm = 128

    def body(gu_ref, d_ref, o_ref):
        for j in range(N_CHIPS):
            at = 2 * GU_BLOCK * j
            g = gu_ref[:, at:at + GU_BLOCK].astype(F32)
            u = gu_ref[:, at + GU_BLOCK:at + 2 * GU_BLOCK].astype(F32)
            d = d_ref[:, GU_BLOCK * j:GU_BLOCK * (j + 1)]
            sg = _sigmoid(g)
            o_ref[:, at:at + GU_BLOCK] = (d * u * (sg * (1.0 + g * (1.0 - sg)))).astype(BF16)
            o_ref[:, at + GU_BLOCK:at + 2 * GU_BLOCK] = (d * (g * sg)).astype(BF16)

    return _call(body, name, _sds((S, 2 * D_FF), BF16), (S // tm,), [_rows(tm, 2 * D_FF), _rows(tm, D_FF)],
                 _rows(tm, 2 * D_FF), sem=("parallel",))(gu, dact)


def _loss_head(y, target, name):
    S, D = y.shape
    tm = 256

    def body(y_ref, t_ref, dy_ref, acc_ref):
        e = y_ref[...] - t_ref[...]
        dy_ref[...] = e / D

        @pl.when(pl.program_id(0) == 0)
        def _():
            acc_ref[...] = jnp.zeros_like(acc_ref)

        acc_ref[...] += jnp.sum(e * e, axis=0, keepdims=True)

    return _call(body, name, [_sds((S, D), F32), _sds((1, D), F32)], (S // tm,), [_rows(tm, D), _rows(tm, D)],
                 [_rows(tm, D), _whole((1, D))], sem=("arbitrary",))(y, target)


def _chunk_mask(T):
    r = lax.shift_right_logical(lax.broadcasted_iota(jnp.int32, (T, T), 0), 6)
    c = lax.shift_right_logical(lax.broadcasted_iota(jnp.int32, (T, T), 1), 6)
    return r >= c


def _dot_nt(a, b):
    return lax.dot_general(a, b, (((1,), (1,)), ((), ())), preferred_element_type=F32)


def _dot_tn(a, b):
    return lax.dot_general(a, b, (((0,), (0,)), ((), ())), preferred_element_type=F32)


def _decay_tables(T):
    lg = jnp.log1p(-jnp.exp2(-5.0 - jnp.arange(RET_HEADS, dtype=F32)))
    idx = jnp.arange(T, dtype=F32)
    diff = idx[:, None] - idx[None, :]
    rel = jnp.exp(lg[:, None, None] * diff[None])
    cid = jnp.arange(T) // CHUNK
    mask = (cid[:, None] >= cid[None, :]).astype(F32)
    reld = jnp.exp(lg[:, None, None] * jnp.abs(diff)[None]) * mask[None]
    lgrow = jnp.broadcast_to(lg[:, None, None], (RET_HEADS, 1, LANES))
    return lgrow, rel, reld


def _attn_fwd(q, k, v, heads, dk, dv, softmax, name, tables=None, side=None):
    S = q.shape[0]
    T = ATT_BLOCK
    nq = S // T
    rep = T // LANES

    def body(*refs):
        if softmax:
            q_ref, k_ref, v_ref, o_ref, lse_ref, m_sc, l_sc, acc_sc = refs
        else:
            q_ref, k_ref, v_ref, lg_ref, rel_ref, reld_ref, o_ref, acc_sc = refs
        i = pl.program_id(1)
        qv = q_ref[...]

        def kv_block(j):
            rows = pl.ds(pl.multiple_of(j * T, T), T)
            return k_ref[rows, :], v_ref[rows, :]

        kb, vb = kv_block(i)
        s = _dot_nt(qv, kb)
        if softmax:
            s = jnp.where(_chunk_mask(T), s * MLA_SCALE, NEG)
            m = jnp.max(s, axis=-1, keepdims=True)
            p = jnp.exp(s - m)
            m_sc[...] = jnp.broadcast_to(m, (T, LANES))
            l_sc[...] = jnp.broadcast_to(jnp.sum(p, axis=-1, keepdims=True), (T, LANES))
        else:
            p = s * reld_ref[0]
        acc_sc[...] = jnp.dot(p.astype(BF16), vb, preferred_element_type=F32)

        def scores(j):
            kb, vb = kv_block(j)
            return _dot_nt(qv, kb), vb

        def update(j, s, vb):
            if softmax:
                s = s * MLA_SCALE
                m_prev = m_sc[...]
                m_next = jnp.maximum(m_prev, jnp.max(s, axis=-1, keepdims=True))
                alpha = jnp.exp(m_prev - m_next)
                p = jnp.exp(s - jnp.tile(m_next, (1, rep)))
                l_sc[...] = alpha * l_sc[...] + jnp.sum(p, axis=-1, keepdims=True)
                m_sc[...] = m_next
                acc_sc[...] = acc_sc[...] * jnp.tile(alpha, (1, dv // LANES)) + jnp.dot(
                    p.astype(BF16), vb, preferred_element_type=F32)
            else:
                fac = jnp.exp(lg_ref[0] * ((i - j) * T).astype(F32))
                p = s * (rel_ref[0] * jnp.tile(fac, (1, rep)))
                acc_sc[...] += jnp.dot(p.astype(BF16), vb, preferred_element_type=F32)

        def pair(jj, carry):
            first, second = scores(2 * jj), scores(2 * jj + 1)
            update(2 * jj, *first)
            update(2 * jj + 1, *second)
            return carry

        lax.fori_loop(0, i // 2, pair, 0)

        @pl.when(i % 2 == 1)
        def _():
            update(i - 1, *scores(i - 1))

        if softmax:
            l = l_sc[...]
            o_ref[...] = acc_sc[...] / jnp.tile(l, (1, dv // LANES))
            lse_ref[...] = m_sc[...] + jnp.log(l)
        else:
            o_ref[...] = acc_sc[...]

    in_specs = [pl.BlockSpec((T, dk), lambda h, i: (i, h)), pl.BlockSpec((S, dk), lambda h, i: (0, h)),
                pl.BlockSpec((S, dv), lambda h, i: (0, h))]
    o_spec = pl.BlockSpec((T, dv), lambda h, i: (i, h))
    if softmax:
        return _call(body, name, [_sds((S, heads * dv), F32), _sds((S, heads * LANES), F32)], (heads, nq), in_specs,
                     [o_spec, pl.BlockSpec((T, LANES), lambda h, i: (i, h))],
                     scratch=[pltpu.VMEM((T, LANES), F32), pltpu.VMEM((T, LANES), F32), pltpu.VMEM((T, dv), F32)],
                     sem=("parallel", "arbitrary"), side=side)(q, k, v)
    lgrow, rel, reld = tables
    in_specs += [pl.BlockSpec((1, 1, LANES), lambda h, i: (h, 0, 0)), pl.BlockSpec((1, T, T), lambda h, i: (h, 0, 0)),
                 pl.BlockSpec((1, T, T), lambda h, i: (h, 0, 0))]
    return _call(body, name, _sds((S, heads * dv), F32), (heads, nq), in_specs, o_spec,
                 scratch=[pltpu.VMEM((T, dv), F32)], sem=("parallel", "arbitrary"), side=side)(q, k, v, lgrow, rel, reld)


def _attn_bwd(q, k, v, do, heads, dk, dv, softmax, name, o=None, lse=None, tables=None, side=None):
    S = q.shape[0]
    T = ATT_BLOCK
    nq = S // T
    rep = T // LANES

    def body(*refs):
        if softmax:
            q_ref, k_ref, v_ref, do_ref, o_ref, lse_ref, dq_ref, dk_ref, dv_ref, dq_sc = refs
        else:
            q_ref, k_ref, v_ref, do_ref, lg_ref, rel_ref, reld_ref, dq_ref, dk_ref, dv_ref, dq_sc = refs
        i = pl.program_id(1)

        @pl.when(i == 0)
        def _():
            dk_ref[...] = jnp.zeros_like(dk_ref)
            dv_ref[...] = jnp.zeros_like(dv_ref)

        qv = q_ref[...]
        dof = do_ref[...].astype(F32)
        dov = dof.astype(BF16)
        if softmax:
            delta = jnp.sum(dof * o_ref[...], axis=-1, keepdims=True)
            lse_t = jnp.tile(lse_ref[...], (1, rep))
        dq_sc[...] = jnp.zeros_like(dq_sc)

        def products(j):
            rows = pl.ds(pl.multiple_of(j * T, T), T)
            kb = k_ref[rows, :]
            return rows, kb, _dot_nt(qv, kb), _dot_nt(dov, v_ref[rows, :])

        def block(j, diagonal, rows, kb, s, dp):
            if softmax:
                s = s * MLA_SCALE
                if diagonal:
                    s = jnp.where(_chunk_mask(T), s, NEG)
                p = jnp.exp(s - lse_t)
                ds = p * (dp - delta) * MLA_SCALE
            else:
                if diagonal:
                    dec = reld_ref[0]
                else:
                    fac = jnp.exp(lg_ref[0] * ((i - j) * T).astype(F32))
                    dec = rel_ref[0] * jnp.tile(fac, (1, rep))
                p = s * dec
                ds = dp * dec
            dsb = ds.astype(BF16)
            dv_ref[rows, :] += _dot_tn(p.astype(BF16), dov)
            dk_ref[rows, :] += _dot_tn(dsb, qv)
            dq_sc[...] += jnp.dot(dsb, kb, preferred_element_type=F32)

        block(i, True, *products(i))

        def pair(jj, carry):
            first, second = products(2 * jj), products(2 * jj + 1)
            block(2 * jj, False, *first)
            block(2 * jj + 1, False, *second)
            return carry

        lax.fori_loop(0, i // 2, pair, 0)

        @pl.when(i % 2 == 1)
        def _():
            block(i - 1, False, *products(i - 1))

        dq_ref[...] = dq_sc[...]

    qspec = pl.BlockSpec((T, dk), lambda h, i: (i, h))
    kspec = pl.BlockSpec((S, dk), lambda h, i: (0, h))
    vspec = pl.BlockSpec((S, dv), lambda h, i: (0, h))
    dospec = pl.BlockSpec((T, dv), lambda h, i: (i, h))
    in_specs = [qspec, kspec, vspec, dospec]
    args = [q, k, v, do]
    if softmax:
        in_specs += [dospec, pl.BlockSpec((T, LANES), lambda h, i: (i, h))]
        args += [o, lse]
    else:
        in_specs += [pl.BlockSpec((1, 1, LANES), lambda h, i: (h, 0, 0)),
                     pl.BlockSpec((1, T, T), lambda h, i: (h, 0, 0)), pl.BlockSpec((1, T, T), lambda h, i: (h, 0, 0))]
        args += list(tables)
    return _call(body, name, [_sds((S, heads * dk), F32), _sds((S, heads * dk), F32), _sds((S, heads * dv), F32)],
                 (heads, nq), in_specs, [qspec, kspec, vspec], scratch=[pltpu.VMEM((T, dk), F32)],
                 sem=("parallel", "arbitrary"), side=side)(*args)


def _rope_tables(pos):
    def tables(dim):
        inv_freq = ROPE_THETA ** (-jnp.arange(0, dim, 2, dtype=F32) / dim)
        ang = pos.astype(F32)[:, None] * inv_freq
        return jnp.cos(ang), jnp.sin(ang)

    cm, sm = tables(ROPE)
    S = pos.shape[0]
    z32, z64 = jnp.zeros((S, 32), F32), jnp.zeros((S, 64), F32)
    cr, sr = tables(RET_DK)
    return (jnp.concatenate([cm, cm, z64], 1), jnp.concatenate([z32, sm, z64], 1),
            jnp.concatenate([-sm, z32, z64], 1), cr, sr)


def _row(v):
    return v.reshape(1, -1).astype(F32)


def _local_step(x, pos, target, pipe, P):
    tabs = _rope_tables(pos)
    dtabs = _decay_tables(ATT_BLOCK)
    xf, xb = _ln_fwd([x], [1.0], _row(P["ln_in_g"]), _row(P["ln_in_b"]), "ln_in", False)
    pipe.gather_first()
    saved = []
    for l in range(DEPTH):
        w = functools.partial(pipe.weight, l)
        t = f"_l{l}"
        h = pipe.run(_matmul, "mm_h" + t, xb, w("w_in"))
        qn, kvn, kr, rq, rk, rv = _prep1(h, tabs, _row(P["q_norm_g"][l]), _row(P["kv_norm_g"][l]), "prep1" + t)
        q = _matmul(qn, w("w_uq"), "mm_q" + t)
        kv = _matmul(kvn, w("w_ukv"), "mm_kv" + t)
        qm, km, vm = _prep2(q, kv, kr, tabs, "prep2" + t)
        a, lse = pipe.run(_attn_fwd, "mla_fwd" + t, qm, km, vm, MLA_HEADS, HEAD_PAD, VDIM, True)
        o = pipe.run(_attn_fwd, "ret_fwd" + t, rq, rk, rv, RET_HEADS, RET_DK, RET_DV, False, tables=dtabs)
        mixin = _gn_gate(a, o, h, _row(P["ret_gn_g"][l]), _row(P["ret_gn_b"][l]), "gn_gate" + t)
        mix = _matmul(mixin, w("w_out"), "mm_mix" + t)
        z1, x1f, x1b = _ln_fwd([xf, mix], [ALPHA, 1.0], _row(P["ln1_g"][l]), _row(P["ln1_b"][l]), "ln1" + t, True)
        gu, act = pipe.run(_matmul_swiglu, "mm_gu" + t, x1b, w("w_gu"))
        f = pipe.run(_matmul, "mm_down" + t, act, w("w_down"))
        z2, x2f, x2b = _ln_fwd([x1f, f], [ALPHA, 1.0], _row(P["ln2_g"][l]), _row(P["ln2_b"][l]), "ln2" + t, True)
        saved.append(dict(xb=xb, h=h, qn=qn, kvn=kvn, rq=rq, rk=rk, rv=rv, qm=qm, km=km, vm=vm, a=a, lse=lse, o=o,
                          mixin=mixin, z1=z1, x1b=x1b, gu=gu, act=act, z2=z2))
        xf, xb = x2f, x2b

    dy, sqerr = _loss_head(xf, target, "loss_head")
    dP = {}
    dys, coefs = [dy], [1.0]
    for l in reversed(range(DEPTH)):
        w, sv = functools.partial(pipe.weight, l), saved[l]
        t = f"_l{l}"
        dz2, dz2b, dg, db = _ln_bwd(dys, coefs, sv["z2"], _row(P["ln2_g"][l]), "ln2_bwd" + t)
        dP[("ln2_g", l)], dP[("ln2_b", l)] = dg, db
        pipe.reduce(l, w_down=pipe.run(_matmul, "mm_dw_down" + t, sv["act"], dz2b, ta=True, out_dtype=BF16))
        dact = pipe.run(_matmul, "mm_dact" + t, dz2b, w("w_down"), tb=True)
        dgu = _swiglu_bwd(sv["gu"], dact, "swiglu_bwd" + t)
        pipe.reduce(l, w_gu=pipe.run(_matmul, "mm_dw_gu" + t, sv["x1b"], dgu, ta=True, out_dtype=BF16))
        dx1 = pipe.run(_matmul, "mm_dx1" + t, dgu, w("w_gu"), tb=True)
        dz1, dz1b, dg, db = _ln_bwd([dz2, dx1], [ALPHA, 1.0], sv["z1"], _row(P["ln1_g"][l]), "ln1_bwd" + t)
        dP[("ln1_g", l)], dP[("ln1_b", l)] = dg, db
        pipe.reduce(l, w_out=_matmul(sv["mixin"], dz1b, "mm_dw_out" + t, ta=True, out_dtype=BF16))
        dmixin = pipe.run(_matmul, "mm_dmixin" + t, dz1b, w("w_out"), tb=True)
        do, drg, dgg, dgb = _gn_gate_bwd(dmixin, sv["o"], sv["h"], _row(P["ret_gn_g"][l]), _row(P["ret_gn_b"][l]),
                                         "gn_gate_bwd" + t)
        dP[("ret_gn_g", l)], dP[("ret_gn_b", l)] = dgg, dgb
        drq, drk, drv = pipe.run(_attn_bwd, "ret_bwd" + t, sv["rq"], sv["rk"], sv["rv"], do, RET_HEADS, RET_DK, RET_DV,
                                 False, tables=dtabs)
        dqm, dkm, dvm = pipe.run(_attn_bwd, "mla_bwd" + t, sv["qm"], sv["km"], sv["vm"], dmixin, MLA_HEADS, HEAD_PAD,
                                 VDIM, True, o=sv["a"], lse=sv["lse"])
        dq, dkv, dkr = _prep2_bwd(dqm, dkm, dvm, tabs, "prep2_bwd" + t)
        g_uq = _matmul(sv["qn"], dq, "mm_dw_uq" + t, ta=True, out_dtype=BF16)
        dqn = _matmul(dq, w("w_uq"), "mm_dqn" + t, tb=True)
        g_ukv = _matmul(sv["kvn"], dkv, "mm_dw_ukv" + t, ta=True, out_dtype=BF16)
        dkvn = _matmul(dkv, w("w_ukv"), "mm_dkvn" + t, tb=True)
        dh, dqg, dkvg = _prep1_bwd(dqn, dkvn, dkr, drq, drk, drv, drg, sv["h"], tabs, _row(P["q_norm_g"][l]),
                                   _row(P["kv_norm_g"][l]), "prep1_bwd" + t)
        dP[("q_norm_g", l)], dP[("kv_norm_g", l)] = dqg, dkvg
        pipe.reduce(l, w_uq=g_uq, w_ukv=g_ukv,
                    w_in=pipe.run(_matmul, "mm_dw_in" + t, sv["xb"], dh, ta=True, out_dtype=BF16))
        dxl = pipe.run(_matmul, "mm_dxl" + t, dh, w("w_in"), tb=True)
        dys, coefs = [dz1, dxl], [ALPHA, 1.0]
    grad_x, _, dg, db = _ln_bwd(dys, coefs, x, _row(P["ln_in_g"]), "ln_in_bwd")
    dP[("ln_in_g", None)], dP[("ln_in_b", None)] = dg, db
    return sqerr, grad_x, dP


INTERNAL_OF = {"w_in": ("w_in",), "w_uq": ("w_uq",), "w_ukv": ("w_ukv",), "w_out": ("w_out",),
               "w_gu": ("w_gate", "w_up"), "w_down": ("w_down",)}


def _internal_weight(name, *blocks):
    cat = lambda parts: jnp.concatenate(parts, axis=1)
    cols = lambda b: cat([b[j] for j in range(N_CHIPS)])
    b = blocks[0]
    if name in ("w_out", "w_down"):
        return b.reshape(-1, b.shape[-1])
    if name == "w_gu":
        return cat([blk[j] for j in range(N_CHIPS) for blk in blocks])
    if name == "w_in":
        return cat([b[0][:, :MLA_IN_USED], jnp.zeros((D_MODEL, MLA_IN - MLA_IN_USED), BF16), b[0][:, MLA_IN_USED:]]
                   + [b[j] for j in range(1, N_CHIPS)])
    if name == "w_uq":
        uq, hw = cols(b), NOPE + ROPE
        pad = jnp.zeros((Q_LORA, HEAD_PAD - hw), BF16)
        return cat([p for h in range(MLA_HEADS) for p in (uq[:, h * hw:(h + 1) * hw], pad)])
    ukv = cols(b)
    return cat([ukv[:, 256 * h:256 * h + NOPE] for h in range(MLA_HEADS)]
               + [ukv[:, 256 * h + NOPE:256 * h + 256] for h in range(MLA_HEADS)])


def _grad_shards(name, g):
    cat = lambda parts: jnp.concatenate(parts, axis=1)
    if name in ("w_out", "w_down"):
        return {name: g.reshape(N_CHIPS, -1, g.shape[-1])}
    if name == "w_gu":
        return {"w_gate": [g[:, 2 * GU_BLOCK * j:2 * GU_BLOCK * j + GU_BLOCK] for j in range(N_CHIPS)],
                "w_up": [g[:, 2 * GU_BLOCK * j + GU_BLOCK:2 * GU_BLOCK * (j + 1)] for j in range(N_CHIPS)]}
    if name == "w_in":
        ci, shift = BIG_SHARD["w_in"][1], MLA_IN - MLA_IN_USED
        return {name: [cat([g[:, :MLA_IN_USED], g[:, MLA_IN:ci + shift]])]
                + [g[:, ci * j + shift:ci * (j + 1) + shift] for j in range(1, N_CHIPS)]}
    if name == "w_uq":
        cq = NOPE + ROPE
        return {name: [cat([g[:, HEAD_PAD * h:HEAD_PAD * h + cq] for h in (2 * j, 2 * j + 1)]) for j in range(N_CHIPS)]}
    return {name: [cat([g[:, o + NOPE * h:o + NOPE * (h + 1)] for h in (2 * j, 2 * j + 1) for o in (0, MLA_HEADS * NOPE)])
                   for j in range(N_CHIPS)]}


def _small_layout(P):
    out, at = {}, 0
    for n in SMALL:
        out[n] = (at, P[n].size)
        at += P[n].size
    return out, at


def _flatten_small(P, last):
    v = jnp.concatenate([P[n].reshape(-1).astype(F32) for n in SMALL] + [last.reshape(-1).astype(F32)])
    return jnp.pad(v, (0, SMALL_ROWS * FLAT_W - v.size)).reshape(SMALL_ROWS, FLAT_W)


def _place():
    return lax.axis_index("x"), lax.axis_index("y"), lax.axis_index("c")


def _other_chips(x, y):
    return [(1 - x, y), (x, 1 - y), (1 - x, 1 - y)]


def _rcopy(src, dst, ssem, rsem, dev):
    return pltpu.make_async_remote_copy(src_ref=src, dst_ref=dst, send_sem=ssem, recv_sem=rsem, device_id=dev,
                                        device_id_type=MESH)


def _comm_call(body, name, out_shape, n_in, scratch):
    many = isinstance(out_shape, (list, tuple))
    return pl.pallas_call(body, name=name, out_shape=out_shape, in_specs=[HBM] * n_in,
                          out_specs=[HBM] * len(out_shape) if many else HBM, scratch_shapes=scratch)


def _half(ref, which):
    rows = ref.shape[0] // 2
    return ref.at[pl.ds(pl.multiple_of(which * rows, 16), rows)]


def _dma_sems(n):
    return pltpu.SemaphoreType.DMA((n,))


def _allgather_side(ws):
    k = len(ws)

    def peers():
        x, y, c = _place()
        return c, 2 * x + y, (x, y, 1 - c), [(n, t, cx, cy) for n in range(k) for t, (cx, cy) in enumerate(_other_chips(x, y))]

    def outgoing(w_refs, g_refs, sems):
        ssem, rsem, _, _, ossem, orsem = sems
        c, j, sib, nt = peers()
        owns = [_rcopy(w_refs[n], g_refs[n].at[j], ossem.at[n], orsem.at[n], sib) for n in range(k)]
        sends = [_rcopy(_half(w_refs[n], c), _half(g_refs[n].at[j], c), ssem.at[3 * n + t], rsem.at[3 * n + t],
                        (cx, cy, c)) for n, t, cx, cy in nt]
        return owns, sends

    def incoming(g_refs, sems):
        ssem, rsem, fssem, frsem, _, _ = sems
        c, _, sib, nt = peers()
        landed, passed, relayed = [], [], []
        for n, t, cx, cy in nt:
            mine, other = (_half(g_refs[n].at[2 * cx + cy], h) for h in (c, 1 - c))
            landed.append(_rcopy(mine, mine, ssem.at[3 * n + t], rsem.at[3 * n + t], (cx, cy, c)))
            passed.append(_rcopy(mine, mine, fssem.at[3 * n + t], frsem.at[3 * n + t], sib))
            relayed.append(_rcopy(other, other, fssem.at[3 * n + t], frsem.at[3 * n + t], sib))
        return landed, passed, relayed

    def start(w_refs, g_refs, sems):
        owns, sends = outgoing(w_refs, g_refs, sems)
        for cp in sends + owns:
            cp.start()

    def finish(w_refs, g_refs, sems):
        owns, sends = outgoing(w_refs, g_refs, sems)
        landed, passed, relayed = incoming(g_refs, sems)
        for got, on in zip(landed, passed):
            got.wait_recv()
            on.start()
        for cp in relayed:
            cp.wait_recv()
        for cp in owns:
            cp.wait()
        for cp in sends + passed:
            cp.wait_send()

    return _Side(list(ws), [_sds((N_CHIPS,) + w.shape, w.dtype) for w in ws],
                 [_dma_sems(3 * k)] * 4 + [_dma_sems(k)] * 2, start, finish)


def _exchange_side(parts):
    k = len(parts)

    def copies(p_refs, rcv_refs, sems):
        ssem, rsem = sems
        x, y, c = _place()
        return [_rcopy(p_refs[n].at[2 * cx + cy], rcv_refs[n].at[t], ssem.at[3 * n + t], rsem.at[3 * n + t], (cx, cy, c))
                for n in range(k) for t, (cx, cy) in enumerate(_other_chips(x, y))]

    def start(p_refs, rcv_refs, sems):
        for cp in copies(p_refs, rcv_refs, sems):
            cp.start()

    def finish(p_refs, rcv_refs, sems):
        for cp in copies(p_refs, rcv_refs, sems):
            cp.wait()

    return _Side(list(parts), [_sds((3,) + p.shape[1:], p.dtype) for p in parts], [_dma_sems(3 * k)] * 2, start, finish)


def _run_side(side, name):
    k_in, k_out = len(side.arrays), len(side.out_shape)

    def body(*refs):
        parts = refs[:k_in], refs[k_in:k_in + k_out], refs[k_in + k_out:]
        side.start(*parts)
        side.finish(*parts)

    return _comm_call(body, name, list(side.out_shape), k_in, list(side.scratch))(*side.arrays)


def _sibling_side(arrays, out_shape, n_copies, copies):
    def start(in_refs, out_refs, sems):
        for cp in copies(in_refs, out_refs, sems):
            cp.start()

    def finish(in_refs, out_refs, sems):
        for cp in copies(in_refs, out_refs, sems):
            cp.wait()

    return _Side(list(arrays), out_shape, [_dma_sems(n_copies)] * 2, start, finish)


def _swap_side(gds):
    k = len(gds)

    def copies(gd_refs, out_refs, sems):
        ssem, rsem = sems
        x, y, c = _place()
        return [_rcopy(_half(gd_refs[n].at[jj], 1 - c), out_refs[n].at[jj], ssem.at[N_CHIPS * n + jj],
                       rsem.at[N_CHIPS * n + jj], (x, y, 1 - c)) for n in range(k) for jj in range(N_CHIPS)]

    return _sibling_side(gds, [_sds((N_CHIPS, g.shape[1] // 2, g.shape[2]), g.dtype) for g in gds], N_CHIPS * k, copies)


def _share_side(reds):
    k = len(reds)

    def copies(r_refs, out_refs, sems):
        ssem, rsem = sems
        x, y, c = _place()
        return [_rcopy(r_refs[n], out_refs[n], ssem.at[n], rsem.at[n], (x, y, 1 - c)) for n in range(k)]

    return _sibling_side(reds, [_sds(r.shape, r.dtype) for r in reds], k, copies)


def _join_sides(sides):
    if len(sides) == 1:
        return sides[0]
    cuts = [(len(s.arrays), len(s.out_shape), len(s.scratch)) for s in sides]

    def each(method, in_refs, out_refs, sems):
        a = o = m = 0
        for s, (ka, ko, km) in zip(sides, cuts):
            getattr(s, method)(in_refs[a:a + ka], out_refs[o:o + ko], sems[m:m + km])
            a, o, m = a + ka, o + ko, m + km

    return _Side([x for s in sides for x in s.arrays], [x for s in sides for x in s.out_shape],
                 [x for s in sides for x in s.scratch], functools.partial(each, "start"), functools.partial(each, "finish"))


def _allreduce_small(small):
    def body(s_ref, all_ref, sssem, srsem, lsem):
        x, y, c = _place()
        me = 4 * x + 2 * y + c
        own = pltpu.make_async_copy(s_ref, all_ref.at[me], lsem)
        own.start()
        cps = []
        for r in range(1, 8):
            fx, fy, fc = (r >> 2) & 1, (r >> 1) & 1, r & 1
            px, py, pc = (1 - x if fx else x, 1 - y if fy else y, 1 - c if fc else c)
            peer = 4 * px + 2 * py + pc
            send = _rcopy(s_ref, all_ref.at[me], sssem.at[r - 1], srsem.at[me], (px, py, pc))
            send.start()
            cps.append((send, _rcopy(s_ref, all_ref.at[peer], sssem.at[r - 1], srsem.at[peer], (px, py, pc))))
        for send, recv in cps:
            send.wait_send()
            recv.wait_recv()
        own.wait()

    return _comm_call(body, "allreduce_small", [_sds((8,) + small.shape, small.dtype)], 1,
                      [pltpu.SemaphoreType.DMA((7,)), pltpu.SemaphoreType.DMA((8,)), pltpu.SemaphoreType.DMA(())])(small)[0]


def _add_pair(gd, got, c, name):
    _, R, W = got.shape
    tm = _pick(R, (512, 256, 128, 64))
    nb = R // tm

    def body(c_ref, a_ref, b_ref, o_ref):
        o_ref[...] = (a_ref[...].astype(F32) + b_ref[...].astype(F32)).astype(o_ref.dtype)

    grid_spec = pltpu.PrefetchScalarGridSpec(
        num_scalar_prefetch=1, grid=(N_CHIPS, nb),
        in_specs=[pl.BlockSpec((None, tm, W), lambda j, i, c_ref: (j, c_ref[0] * nb + i, 0)),
                  pl.BlockSpec((None, tm, W), lambda j, i, c_ref: (j, i, 0))],
        out_specs=pl.BlockSpec((None, tm, W), lambda j, i, c_ref: (j, i, 0)))
    return pl.pallas_call(body, name=name, grid_spec=grid_spec, out_shape=_sds((N_CHIPS, R, W), gd.dtype),
                          compiler_params=pltpu.CompilerParams(dimension_semantics=("parallel", "parallel"),
                                                               vmem_limit_bytes=VMEM_LIMIT))(c, gd, got)


def _add_chips(part, rcv, j, name):
    _, R, W = part.shape
    tm = _pick(R, (512, 256, 128, 64))

    def body(j_ref, p_ref, r0_ref, r1_ref, r2_ref, o_ref):
        o_ref[...] = ((p_ref[...].astype(F32) + r0_ref[...].astype(F32)) + r1_ref[...].astype(F32)) + r2_ref[...].astype(F32)

    def slot(t):
        return pl.BlockSpec((None, tm, W), lambda i, j_ref: (t, i, 0))

    grid_spec = pltpu.PrefetchScalarGridSpec(
        num_scalar_prefetch=1, grid=(R // tm,),
        in_specs=[pl.BlockSpec((None, tm, W), lambda i, j_ref: (j_ref[0], i, 0)), slot(0), slot(1), slot(2)],
        out_specs=pl.BlockSpec((tm, W), lambda i, j_ref: (i, 0)))
    return pl.pallas_call(body, name=name, grid_spec=grid_spec, out_shape=_sds((R, W), F32),
                          compiler_params=pltpu.CompilerParams(dimension_semantics=("parallel",),
                                                               vmem_limit_bytes=VMEM_LIMIT))(j, part, rcv, rcv, rcv)


def _sum_small(allsmall):
    _, R, W = allsmall.shape

    def body(a_ref, o_ref):
        acc = a_ref[0]
        for d in range(1, 8):
            acc = acc + a_ref[d]
        o_ref[...] = acc

    return _call(body, "sum_small", _sds((R, W), F32), (1,), [_whole((8, R, W))], _whole((R, W)),
                 sem=("arbitrary",))(allsmall)


def _adamw(w, g, m, v, name):
    R, C = w.shape
    tm = _pick(R, (256, 128, 64, 32, 8))

    def body(w_ref, g_ref, m_ref, v_ref, d_ref, mo_ref, vo_ref):
        gv = g_ref[...]
        mn = ADAM_B1 * m_ref[...] + (1.0 - ADAM_B1) * gv
        vn = ADAM_B2 * v_ref[...] + (1.0 - ADAM_B2) * (gv * gv)
        m_hat = mn / (1.0 - ADAM_B1 ** ADAM_STEP)
        v_hat = vn / (1.0 - ADAM_B2 ** ADAM_STEP)
        d_ref[...] = -ADAM_LR * (m_hat / (jnp.sqrt(v_hat) + ADAM_EPS) + ADAM_WD * w_ref[...])
        mo_ref[...] = mn
        vo_ref[...] = vn

    spec = _rows(tm, C)
    return _call(body, name, [_sds((R, C), F32)] * 3, (R // tm,), [spec] * 4, [spec] * 3, sem=("parallel",))(w, g, m, v)


def _adamw_layer(c, w, m, v, mine, other, l, prev, name):
    _, R, C = w.shape
    half = R // 2
    tm = _pick(half, (256, 128, 64))
    nbh = half // tm

    def body(c_ref, w_ref, m_ref, v_ref, a_ref, b_ref, *rest):
        g_ref, d_ref, mo_ref, vo_ref = rest[-4:]
        gv = jnp.where(pl.program_id(0) // nbh == c_ref[0], a_ref[...], b_ref[...])
        mn = ADAM_B1 * m_ref[...] + (1.0 - ADAM_B1) * gv
        vn = ADAM_B2 * v_ref[...] + (1.0 - ADAM_B2) * (gv * gv)
        m_hat = mn / (1.0 - ADAM_B1 ** ADAM_STEP)
        v_hat = vn / (1.0 - ADAM_B2 ** ADAM_STEP)
        g_ref[...] = gv
        d_ref[...] = -ADAM_LR * (m_hat / (jnp.sqrt(v_hat) + ADAM_EPS) + ADAM_WD * w_ref[...])
        mo_ref[...] = mn
        vo_ref[...] = vn

    layer = pl.BlockSpec((None, tm, C), lambda i, c_ref: (l, i, 0))
    halfspec = pl.BlockSpec((tm, C), lambda i, c_ref: (i % nbh, 0))
    n_prev = 0 if prev is None else 4
    grid_spec = pltpu.PrefetchScalarGridSpec(
        num_scalar_prefetch=1, grid=(R // tm,),
        in_specs=[layer] * 3 + [halfspec] * 2 + [pl.BlockSpec(memory_space=pl.ANY)] * n_prev,
        out_specs=[layer] * 4)
    return pl.pallas_call(body, name=name, grid_spec=grid_spec, out_shape=[_sds(w.shape, F32)] * 4,
                          input_output_aliases={6 + k: k for k in range(n_prev)},
                          compiler_params=pltpu.CompilerParams(dimension_semantics=("parallel",),
                                                               vmem_limit_bytes=VMEM_LIMIT))(
        c, w, m, v, mine, other, *(prev or ()))


FIRST_GATHER = ("w_in", "w_uq", "w_ukv")
G_DOWN, G_GU, G_OUT, G_IN = ("w_down",), ("w_gate", "w_up"), ("w_out",), ("w_uq", "w_ukv", "w_in")


def _backward_jobs(l):
    t = f"_l{l}"
    return {"mm_dact" + t: [("swap", l, G_DOWN)], "mm_dw_gu" + t: [("exchange", l, G_DOWN)],
            "mm_dx1" + t: [("swap", l, G_GU), ("share", l, G_DOWN)], "mm_dmixin" + t: [("swap", l, G_OUT)],
            "ret_bwd" + t: [("exchange", l, ("w_gate",))],
            "mla_bwd" + t: [("exchange", l, ("w_up", "w_out")), ("share", l, ("w_gate",))],
            "mm_dw_in" + t: [("share", l, ("w_up", "w_out"))]}


JOBS = {
    "mm_h_l0": [("gather", 0, ("w_out",))], "mla_fwd_l0": [("gather", 0, ("w_gate", "w_up"))],
    "ret_fwd_l0": [("gather", 1, ("w_uq", "w_ukv", "w_out"))], "mm_gu_l0": [("gather", 0, ("w_down",))],
    "mm_down_l0": [("gather", 1, ("w_in",))], "mla_fwd_l1": [("gather", 1, ("w_gate", "w_up"))],
    "mm_gu_l1": [("gather", 1, ("w_down",))],
    **_backward_jobs(1), **_backward_jobs(0),
    "mm_dxl_l1": [("swap", 1, G_IN)], "mm_dw_down_l0": [("exchange", 1, G_IN)],
    "mm_dact_l0": [("swap", 0, G_DOWN), ("share", 1, G_IN)], "mm_dxl_l0": [("exchange", 0, G_IN)]}
PLANNED = {job for jobs in JOBS.values() for job in jobs}


class _Pipeline:
    def __init__(self, own, Wt, Mo, Vo, core, chip):
        self.own, self.Wt, self.Mo, self.Vo, self.core, self.chip = own, Wt, Mo, Vo, core, chip
        self.blocks, self.whole, self.gds, self.parts, self.reds = {}, {}, {}, {}, {}
        self.results = {n: None for n in BIG}

    def gather_first(self):
        job = ("gather", 0, FIRST_GATHER)
        self._done(*job, _run_side(self._side(*job), "allgather_first"))

    def weight(self, l, name):
        if (l, name) not in self.whole:
            self.whole[(l, name)] = _internal_weight(name, *[self.blocks[(l, n)] for n in INTERNAL_OF[name]])
        return self.whole[(l, name)]

    def run(self, fn, name, *args, **kw):
        jobs = JOBS.get(name, ())
        if not jobs:
            return fn(*args, name=name, **kw)
        sides = [self._side(*job) for job in jobs]
        out, res = fn(*args, name=name, side=_join_sides(sides), **kw)
        for job, side in zip(jobs, sides):
            k = len(side.out_shape)
            self._done(*job, res[:k])
            res = res[k:]
        return out

    def reduce(self, l, **grads):
        shards = {}
        for name, g in grads.items():
            shards.update(_grad_shards(name, g))
        for n, sh in shards.items():
            self.gds[(l, n)] = sh if hasattr(sh, "shape") else jnp.stack(sh)
        self._alone("swap", l, tuple(shards))

    def _alone(self, kind, l, names):
        if (kind, l, names) not in PLANNED:
            self._done(kind, l, names, _run_side(self._side(kind, l, names), f"{kind}_{names[0]}_l{l}"))

    def _side(self, kind, l, names):
        if kind == "gather":
            return _allgather_side([self.own[l][n] for n in names])
        store = {"swap": self.gds, "exchange": self.parts, "share": self.reds}[kind]
        make = {"swap": _swap_side, "exchange": _exchange_side, "share": _share_side}[kind]
        return make([store[(l, n)] for n in names])

    def _done(self, kind, l, names, res):
        for n, r in zip(names, res):
            if kind == "gather":
                self.blocks[(l, n)] = r
            elif kind == "swap":
                self.parts[(l, n)] = _add_pair(self.gds[(l, n)], r, self.core, f"add_pair_{n}_l{l}")
            elif kind == "exchange":
                self.reds[(l, n)] = _add_chips(self.parts[(l, n)], r, self.chip, f"add_chips_{n}_l{l}")
            else:
                self.results[n] = _adamw_layer(self.core, self.Wt[n], self.Mo[n], self.Vo[n], self.reds[(l, n)], r, l,
                                               self.results[n], f"adamw_{n}_l{l}")
        if kind == "exchange":
            self._alone("share", l, names)


def kernel(x, positions, ln_in_g, ln_in_b, w_in, q_norm_g, kv_norm_g, w_uq, w_ukv, ret_gn_g, ret_gn_b, w_out, ln1_g, ln1_b, w_gate, w_up, w_down, ln2_g, ln2_b, loss_target, m_ln_in_g, m_ln_in_b, m_w_in, m_q_norm_g, m_kv_norm_g, m_w_uq, m_w_ukv, m_ret_gn_g, m_ret_gn_b, m_w_out, m_ln1_g, m_ln1_b, m_w_gate, m_w_up, m_w_down, m_ln2_g, m_ln2_b, v_ln_in_g, v_ln_in_b, v_w_in, v_q_norm_g, v_kv_norm_g, v_w_uq, v_w_ukv, v_ret_gn_g, v_ret_gn_b, v_w_out, v_ln1_g, v_ln1_b, v_w_gate, v_w_up, v_w_down, v_ln2_g, v_ln2_b):
    given = dict(locals())
    Wt = {n: given[n] for n in WEIGHTS}
    Mo = {n: given["m_" + n] for n in WEIGHTS}
    Vo = {n: given["v_" + n] for n in WEIGHTS}
    cx, cy, cc = _place()
    chip = (2 * cx + cy).astype(jnp.int32)
    core = cc.astype(jnp.int32)

    own = [{n: Wt[n][l].astype(BF16) for n in BIG} for l in range(DEPTH)]
    pipe = _Pipeline(own, Wt, Mo, Vo, core.reshape(1), chip.reshape(1))
    sqerr, grad_x, dP = _local_step(x[0], positions[0], loss_target[0], pipe, Wt)
    results = pipe.results

    small_g = {n: (dP[(n, None)] if Wt[n].ndim == 1 else jnp.stack([dP[(n, l)] for l in range(DEPTH)])) for n in SMALL}
    local_loss = 0.5 * jnp.sum(sqerr) / D_MODEL
    small_sum = _sum_small(_allreduce_small(_flatten_small(small_g, local_loss))).reshape(-1)
    layout, n_small = _small_layout(Wt)
    loss = small_sum[n_small]

    grads, deltas, new_m, new_v = {}, {}, {}, {}
    for n in BIG:
        grads[n], deltas[n], new_m[n], new_v[n] = results[n]
    zero = jnp.zeros((), F32)
    d, mn, vn = _adamw(_flatten_small(Wt, zero), small_sum.reshape(SMALL_ROWS, FLAT_W), _flatten_small(Mo, zero),
                       _flatten_small(Vo, zero), "adamw_small")
    for n in SMALL:
        at, size = layout[n]
        pick = lambda a: a.reshape(-1)[at:at + size].reshape(Wt[n].shape)
        grads[n], deltas[n], new_m[n], new_v[n] = pick(small_sum), pick(d), pick(mn), pick(vn)

    return (loss, grad_x[None], *[grads[n] for n in WEIGHTS], *[deltas[n] for n in WEIGHTS],
            *[new_m[n] for n in WEIGHTS], *[new_v[n] for n in WEIGHTS])
```

```python
import functools

import jax
import jax.numpy as jnp
from jax import lax
from jax.experimental import pallas as pl
from jax.experimental.pallas import tpu as pltpu

F32 = jnp.float32
BF16 = jnp.bfloat16

D_MODEL = 2048
DEPTH = 2
CHUNK = 64
MLA_HEADS = 8
Q_LORA = 512
KV_LORA = 256
NOPE = 128
ROPE = 64
VDIM = 128
RET_HEADS = 4
RET_DK = 256
RET_DV = 256
D_FF = 5632
D_IN = 4928
ROPE_THETA = 10000.0
LN_EPS = 1e-5
RMS_EPS = 1e-6
GN_EPS = 1e-5
ALPHA = (2 * DEPTH) ** 0.25
MLA_SCALE = (NOPE + ROPE) ** -0.5
RET_SCALE = RET_DK ** -0.5
ADAM_LR = 0.001
ADAM_B1 = 0.9
ADAM_B2 = 0.999
ADAM_EPS = 1e-08
ADAM_WD = 0.01
ADAM_STEP = 10

LANES = 128
HEAD_PAD = 256
MLA_IN = 1024
MLA_IN_USED = Q_LORA + KV_LORA + ROPE
D_IN_PAD = MLA_IN + 4 * 1024
ATT_BLOCK = 512
NEG = -1e30
VMEM_LIMIT = 56 * 1024 * 1024

N_CHIPS = 4
FLAT_W = 1024
BIG = ("w_in", "w_uq", "w_ukv", "w_out", "w_gate", "w_up", "w_down")
BIG_SHARD = {"w_in": (2048, 1232), "w_uq": (512, 384), "w_ukv": (256, 512), "w_out": (512, 2048),
             "w_gate": (2048, 1408), "w_up": (2048, 1408), "w_down": (1408, 2048)}
SMALL = ("ln_in_g", "ln_in_b", "q_norm_g", "kv_norm_g", "ret_gn_g", "ret_gn_b", "ln1_g", "ln1_b", "ln2_g", "ln2_b")
WEIGHTS = ("ln_in_g", "ln_in_b", "w_in", "q_norm_g", "kv_norm_g", "w_uq", "w_ukv", "ret_gn_g", "ret_gn_b", "w_out",
           "ln1_g", "ln1_b", "w_gate", "w_up", "w_down", "ln2_g", "ln2_b")
SMALL_ROWS = 32

MESH = pl.DeviceIdType.MESH


def _pick(dim, cands):
    for c in cands:
        if dim % c == 0:
            return c
    return dim


HBM = pl.BlockSpec(memory_space=pltpu.HBM)


class _Side:
    def __init__(self, arrays, out_shape, scratch, start, finish):
        self.arrays, self.out_shape, self.scratch, self.start, self.finish = arrays, out_shape, scratch, start, finish


def _call(body, name, out_shape, grid, in_specs, out_specs, scratch=(), sem=None, side=None):
    params = pltpu.CompilerParams(dimension_semantics=sem if side is None else ("arbitrary",) * len(grid),
                                  vmem_limit_bytes=VMEM_LIMIT)
    if side is None:
        return pl.pallas_call(body, name=name, out_shape=out_shape, grid=grid, in_specs=in_specs, out_specs=out_specs,
                              scratch_shapes=list(scratch), compiler_params=params)
    single = not isinstance(out_shape, (list, tuple))
    outs = [out_shape] if single else list(out_shape)
    ospecs = [out_specs] if single else list(out_specs)
    cuts = [len(in_specs), len(side.arrays), len(outs), len(side.out_shape), len(scratch)]
    ends = [sum(cuts[:k + 1]) for k in range(len(cuts))]

    def hosted(*refs):
        ins, s_in, o, s_out, scr = (refs[a:b] for a, b in zip([0] + ends[:-1], ends))
        sems = refs[ends[-1]:]
        ids = [pl.program_id(a) for a in range(len(grid))]
        first = functools.reduce(jnp.logical_and, [i == 0 for i in ids])
        last = functools.reduce(jnp.logical_and, [i == g - 1 for i, g in zip(ids, grid)])

        @pl.when(first)
        def _():
            side.start(s_in, s_out, sems)

        body(*ins, *o, *scr)

        @pl.when(last)
        def _():
            side.finish(s_in, s_out, sems)

    call = pl.pallas_call(hosted, name=name, out_shape=outs + list(side.out_shape), grid=grid,
                          in_specs=list(in_specs) + [HBM] * len(side.arrays),
                          out_specs=ospecs + [HBM] * len(side.out_shape),
                          scratch_shapes=list(scratch) + list(side.scratch), compiler_params=params)

    def run(*args):
        res = call(*args, *side.arrays)
        return (res[0] if single else list(res[:len(outs)])), list(res[len(outs):])

    return run


def _rows(tm, w, col=0):
    return pl.BlockSpec((tm, w), lambda i: (i, col))


def _whole(shape):
    return pl.BlockSpec(shape, lambda i: (0,) * len(shape))


def _sds(shape, dtype):
    return jax.ShapeDtypeStruct(shape, dtype)


def _matmul(a, b, name, ta=False, tb=False, out_dtype=F32, side=None):
    (K, M) = a.shape if ta else a.shape[::-1]
    (N, Kb) = b.shape if tb else b.shape[::-1]
    assert K == Kb, (a.shape, b.shape, ta, tb)
    tm = _pick(M, (1024, 1408, 512, 256, 128))
    tn = _pick(N, (1024, 512, 256, 128))
    tk = _pick(K, (2816, 2560, 2048, 1024, 512, 256))
    nk = K // tk
    dn = (((0 if ta else 1,), (1 if tb else 0,)), ((), ()))

    def body(a_ref, b_ref, o_ref, acc_ref):
        k = pl.program_id(2)
        if nk == 1:
            o_ref[...] = lax.dot_general(a_ref[...].astype(BF16), b_ref[...].astype(BF16), dn,
                                         preferred_element_type=F32).astype(out_dtype)
        else:
            @pl.when(k == 0)
            def _():
                acc_ref[...] = jnp.zeros_like(acc_ref)

            acc_ref[...] += lax.dot_general(a_ref[...].astype(BF16), b_ref[...].astype(BF16), dn,
                                            preferred_element_type=F32)

            @pl.when(k == nk - 1)
            def _():
                o_ref[...] = acc_ref[...].astype(out_dtype)

    a_spec = pl.BlockSpec((tk, tm), lambda i, j, k: (k, i)) if ta else pl.BlockSpec((tm, tk), lambda i, j, k: (i, k))
    b_spec = pl.BlockSpec((tn, tk), lambda i, j, k: (j, k)) if tb else pl.BlockSpec((tk, tn), lambda i, j, k: (k, j))
    return _call(body, name, _sds((M, N), out_dtype), (M // tm, N // tn, nk), [a_spec, b_spec],
                 pl.BlockSpec((tm, tn), lambda i, j, k: (i, j)), scratch=[pltpu.VMEM((tm, tn), F32)],
                 sem=("parallel", "parallel", "arbitrary"), side=side)(a, b)


def _sigmoid(x):
    return 1.0 / (1.0 + jnp.exp(-x))


def _rope_group(r, c, sa, sb):
    return r * c + pltpu.roll(r, 32, 1) * sa + pltpu.roll(r, 96, 1) * sb


def _ln_fwd(xs, coefs, g, b, name, want_z):
    S, D = xs[0].shape
    tm = 256
    n = len(xs)

    def body(*refs):
        x_refs, g_ref, b_ref, outs = refs[:n], refs[n], refs[n + 1], refs[n + 2:]
        z = None
        for cf, r in zip(coefs, x_refs):
            t = r[...] if cf == 1.0 else cf * r[...]
            z = t if z is None else z + t
        mu = jnp.mean(z, axis=-1, keepdims=True)
        zc = z - mu
        var = jnp.mean(zc * zc, axis=-1, keepdims=True)
        y = zc * lax.rsqrt(var + LN_EPS) * g_ref[...] + b_ref[...]
        if want_z:
            outs[0][...] = z
        outs[-2][...] = y
        outs[-1][...] = y.astype(BF16)

    out_shape = [_sds((S, D), F32)] * (2 if want_z else 1) + [_sds((S, D), BF16)]
    return _call(body, name, out_shape, (S // tm,), [_rows(tm, D)] * n + [_whole((1, D))] * 2,
                 [_rows(tm, D)] * len(out_shape), sem=("parallel",))(*xs, g, b)


def _ln_bwd(dys, coefs, z, g, name):
    S, D = z.shape
    tm = 256
    n = len(dys)

    def body(*refs):
        dy_refs, z_ref, g_ref = refs[:n], refs[n], refs[n + 1]
        dz_ref, dzb_ref, dg_ref, db_ref = refs[n + 2:]
        dy = None
        for cf, r in zip(coefs, dy_refs):
            t = r[...] if cf == 1.0 else cf * r[...]
            dy = t if dy is None else dy + t
        zv = z_ref[...]
        mu = jnp.mean(zv, axis=-1, keepdims=True)
        zc = zv - mu
        var = jnp.mean(zc * zc, axis=-1, keepdims=True)
        rstd = lax.rsqrt(var + LN_EPS)
        xh = zc * rstd
        dyg = dy * g_ref[...]
        dz = rstd * (dyg - jnp.mean(dyg, axis=-1, keepdims=True) - xh * jnp.mean(dyg * xh, axis=-1, keepdims=True))
        dz_ref[...] = dz
        dzb_ref[...] = dz.astype(BF16)

        @pl.when(pl.program_id(0) == 0)
        def _():
            dg_ref[...] = jnp.zeros_like(dg_ref)
            db_ref[...] = jnp.zeros_like(db_ref)

        dg_ref[...] += jnp.sum(dy * xh, axis=0, keepdims=True)
        db_ref[...] += jnp.sum(dy, axis=0, keepdims=True)

    return _call(body, name, [_sds((S, D), F32), _sds((S, D), BF16), _sds((1, D), F32), _sds((1, D), F32)],
                 (S // tm,), [_rows(tm, D)] * (n + 1) + [_whole((1, D))],
                 [_rows(tm, D), _rows(tm, D), _whole((1, D)), _whole((1, D))], sem=("arbitrary",))(*dys, z, g)


def _rms(x, g):
    return x * lax.rsqrt(jnp.mean(x * x, axis=-1, keepdims=True) + RMS_EPS) * g


def _prep1(h, tabs, qg, kvg, name):
    S = h.shape[0]
    tm = 256
    cm, sam, sbm, cr, sr = tabs

    def body(h_ref, cm_ref, sam_ref, sbm_ref, cr_ref, sr_ref, qg_ref, kvg_ref,
             qn_ref, kvn_ref, kr_ref, rq_ref, rk_ref, rv_ref):
        qn_ref[...] = _rms(h_ref[:, 0:Q_LORA], qg_ref[...]).astype(BF16)
        kvn_ref[...] = _rms(h_ref[:, Q_LORA:Q_LORA + KV_LORA], kvg_ref[...]).astype(BF16)
        kr_ref[...] = _rope_group(h_ref[:, 768:896], cm_ref[...], sam_ref[...], sbm_ref[...])
        c, s = cr_ref[...], sr_ref[...]
        for hd in range(RET_HEADS):
            for src, dst, scale in ((MLA_IN, rq_ref, RET_SCALE), (MLA_IN + 1024, rk_ref, None)):
                t1 = h_ref[:, src + hd * 256:src + hd * 256 + 128]
                t2 = h_ref[:, src + hd * 256 + 128:src + hd * 256 + 256]
                o1, o2 = t1 * c - t2 * s, t2 * c + t1 * s
                if scale is not None:
                    o1, o2 = o1 * scale, o2 * scale
                dst[:, hd * 256:hd * 256 + 128] = o1.astype(BF16)
                dst[:, hd * 256 + 128:hd * 256 + 256] = o2.astype(BF16)
        rv_ref[...] = h_ref[:, MLA_IN + 2048:MLA_IN + 3072].astype(BF16)

    t128 = _rows(tm, LANES)
    return _call(body, name,
                 [_sds((S, Q_LORA), BF16), _sds((S, KV_LORA), BF16), _sds((S, LANES), F32),
                  _sds((S, 1024), BF16), _sds((S, 1024), BF16), _sds((S, 1024), BF16)],
                 (S // tm,),
                 [_rows(tm, D_IN_PAD), t128, t128, t128, t128, t128, _whole((1, Q_LORA)), _whole((1, KV_LORA))],
                 [_rows(tm, Q_LORA), _rows(tm, KV_LORA), t128, _rows(tm, 1024), _rows(tm, 1024), _rows(tm, 1024)],
                 sem=("parallel",))(h, cm, sam, sbm, cr, sr, qg, kvg)


def _prep1_bwd(dqn, dkvn, dkr, drq, drk, drv, drg, h, tabs, qg, kvg, name):
    S = h.shape[0]
    tm = 256
    cm, sam, sbm, cr, sr = tabs

    def rms_bwd(x, g, dy):
        r = lax.rsqrt(jnp.mean(x * x, axis=-1, keepdims=True) + RMS_EPS)
        dyg = dy * g
        dx = r * dyg - x * (r * r * r) * jnp.mean(dyg * x, axis=-1, keepdims=True)
        return dx, jnp.sum(dy * x * r, axis=0, keepdims=True)

    def body(dqn_ref, dkvn_ref, dkr_ref, drq_ref, drk_ref, drv_ref, drg_ref, h_ref,
             cm_ref, sam_ref, sbm_ref, cr_ref, sr_ref, qg_ref, kvg_ref, dh_ref, dqg_ref, dkvg_ref):
        dcq, dqg = rms_bwd(h_ref[:, 0:Q_LORA], qg_ref[...], dqn_ref[...])
        dckv, dkvg = rms_bwd(h_ref[:, Q_LORA:Q_LORA + KV_LORA], kvg_ref[...], dkvn_ref[...])
        dh_ref[:, 0:Q_LORA] = dcq.astype(BF16)
        dh_ref[:, Q_LORA:Q_LORA + KV_LORA] = dckv.astype(BF16)
        dh_ref[:, 768:896] = _rope_group(dkr_ref[...], cm_ref[...], -sam_ref[...], -sbm_ref[...]).astype(BF16)
        dh_ref[:, 896:1024] = jnp.zeros((tm, LANES), BF16)
        c, s = cr_ref[...], sr_ref[...]
        for hd in range(RET_HEADS):
            for src, dst, scale in ((drq_ref, MLA_IN, RET_SCALE), (drk_ref, MLA_IN + 1024, None)):
                d1 = src[:, hd * 256:hd * 256 + 128]
                d2 = src[:, hd * 256 + 128:hd * 256 + 256]
                if scale is not None:
                    d1, d2 = d1 * scale, d2 * scale
                dh_ref[:, dst + hd * 256:dst + hd * 256 + 128] = (d1 * c + d2 * s).astype(BF16)
                dh_ref[:, dst + hd * 256 + 128:dst + hd * 256 + 256] = (d2 * c - d1 * s).astype(BF16)
        dh_ref[:, MLA_IN + 2048:MLA_IN + 3072] = drv_ref[...].astype(BF16)
        dh_ref[:, MLA_IN + 3072:MLA_IN + 4096] = drg_ref[...].astype(BF16)

        @pl.when(pl.program_id(0) == 0)
        def _():
            dqg_ref[...] = jnp.zeros_like(dqg_ref)
            dkvg_ref[...] = jnp.zeros_like(dkvg_ref)

        dqg_ref[...] += dqg
        dkvg_ref[...] += dkvg

    t128 = _rows(tm, LANES)
    return _call(body, name,
                 [_sds((S, D_IN_PAD), BF16), _sds((1, Q_LORA), F32), _sds((1, KV_LORA), F32)],
                 (S // tm,),
                 [_rows(tm, Q_LORA), _rows(tm, KV_LORA), t128, _rows(tm, 1024), _rows(tm, 1024), _rows(tm, 1024),
                  _rows(tm, 1024), _rows(tm, MLA_IN), t128, t128, t128, t128, t128,
                  _whole((1, Q_LORA)), _whole((1, KV_LORA))],
                 [_rows(tm, D_IN_PAD), _whole((1, Q_LORA)), _whole((1, KV_LORA))],
                 sem=("arbitrary",))(dqn, dkvn, dkr, drq, drk, drv, drg, h, cm, sam, sbm, cr, sr, qg, kvg)


def _prep2(q, kv, kr, tabs, name):
    S = q.shape[0]
    tm = 256
    cm, sam, sbm = tabs[:3]

    def body(q_ref, kv_ref, kr_ref, cm_ref, sam_ref, sbm_ref, qo_ref, ko_ref, vo_ref):
        c, sa, sb = cm_ref[...], sam_ref[...], sbm_ref[...]
        krb = kr_ref[...].astype(BF16)
        ones = jnp.ones((tm, LANES), BF16)
        for hd in range(MLA_HEADS):
            o = hd * HEAD_PAD
            qo_ref[:, o:o + 128] = (q_ref[:, o:o + 128] * MLA_SCALE).astype(BF16)
            qo_ref[:, o + 128:o + 256] = (_rope_group(q_ref[:, o + 128:o + 256], c, sa, sb) * MLA_SCALE).astype(BF16)
            ko_ref[:, o:o + 128] = kv_ref[:, hd * 128:hd * 128 + 128].astype(BF16)
            ko_ref[:, o + 128:o + 256] = krb
            vo_ref[:, o:o + 128] = kv_ref[:, 1024 + hd * 128:1024 + hd * 128 + 128].astype(BF16)
            vo_ref[:, o + 128:o + 256] = ones

    t128 = _rows(tm, LANES)
    return _call(body, name, [_sds((S, 2048), BF16)] * 3, (S // tm,),
                 [_rows(tm, 2048), _rows(tm, 2048), t128, t128, t128, t128],
                 [_rows(tm, 2048)] * 3, sem=("parallel",))(q, kv, kr, cm, sam, sbm)


def _prep2_bwd(dqm, dkm, dvm, tabs, name):
    S = dqm.shape[0]
    tm = 256
    cm, sam, sbm = tabs[:3]

    def body(dq_ref, dk_ref, dv_ref, cm_ref, sam_ref, sbm_ref, dqo_ref, dkvo_ref, dkr_ref):
        c, sa, sb = cm_ref[...], -sam_ref[...], -sbm_ref[...]
        dkr = None
        for hd in range(MLA_HEADS):
            o = hd * HEAD_PAD
            dqo_ref[:, o:o + 128] = (dq_ref[:, o:o + 128] * MLA_SCALE).astype(BF16)
            dqo_ref[:, o + 128:o + 256] = (_rope_group(dq_ref[:, o + 128:o + 256], c, sa, sb) * MLA_SCALE).astype(BF16)
            dkvo_ref[:, hd * 128:hd * 128 + 128] = dk_ref[:, o:o + 128].astype(BF16)
            t = dk_ref[:, o + 128:o + 256]
            dkr = t if dkr is None else dkr + t
        dkvo_ref[:, 1024:2048] = dv_ref[...].astype(BF16)
        dkr_ref[...] = dkr

    t128 = _rows(tm, LANES)
    return _call(body, name, [_sds((S, 2048), BF16), _sds((S, 2048), BF16), _sds((S, LANES), F32)], (S // tm,),
                 [_rows(tm, 2048), _rows(tm, 2048), _rows(tm, 1024), t128, t128, t128],
                 [_rows(tm, 2048), _rows(tm, 2048), t128], sem=("parallel",))(dqm, dkm, dvm, cm, sam, sbm)


def _gn_gate(a, o, h, gg, gb, name):
    S = a.shape[0]
    tm = 256

    def body(a_ref, o_ref, rg_ref, gg_ref, gb_ref, mix_ref):
        mix_ref[:, 0:1024] = a_ref[...].astype(BF16)
        for hd in range(RET_HEADS):
            sl = slice(hd * 256, hd * 256 + 256)
            ov = o_ref[:, sl]
            mu = jnp.mean(ov, axis=-1, keepdims=True)
            oc = ov - mu
            var = jnp.mean(oc * oc, axis=-1, keepdims=True)
            y = oc * lax.rsqrt(var + GN_EPS) * gg_ref[:, sl] + gb_ref[:, sl]
            rg = rg_ref[:, sl]
            mix_ref[:, 1024 + hd * 256:1024 + hd * 256 + 256] = (rg * _sigmoid(rg) * y).astype(BF16)

    return _call(body, name, _sds((S, 2048), BF16), (S // tm,),
                 [_rows(tm, 1024), _rows(tm, 1024), _rows(tm, 1024, 4), _whole((1, 1024)), _whole((1, 1024))],
                 _rows(tm, 2048), sem=("parallel",))(a, o, h, gg, gb)


def _gn_gate_bwd(dmixin, o, h, gg, gb, name):
    S = o.shape[0]
    tm = 256

    def body(dr_ref, o_ref, rg_ref, gg_ref, gb_ref, do_ref, drg_ref, dgg_ref, dgb_ref):
        @pl.when(pl.program_id(0) == 0)
        def _():
            dgg_ref[...] = jnp.zeros_like(dgg_ref)
            dgb_ref[...] = jnp.zeros_like(dgb_ref)

        for hd in range(RET_HEADS):
            sl = slice(hd * 256, hd * 256 + 256)
            ov = o_ref[:, sl]
            mu = jnp.mean(ov, axis=-1, keepdims=True)
            oc = ov - mu
            var = jnp.mean(oc * oc, axis=-1, keepdims=True)
            rstd = lax.rsqrt(var + GN_EPS)
            xh = oc * rstd
            g = gg_ref[:, sl]
            y = xh * g + gb_ref[:, sl]
            rg = rg_ref[:, sl]
            sg = _sigmoid(rg)
            dr = dr_ref[:, sl]
            dy = dr * (rg * sg)
            drg_ref[:, sl] = dr * y * (sg * (1.0 + rg * (1.0 - sg)))
            dgg_ref[:, sl] += jnp.sum(dy * xh, axis=0, keepdims=True)
            dgb_ref[:, sl] += jnp.sum(dy, axis=0, keepdims=True)
            dxh = dy * g
            do = rstd * (dxh - jnp.mean(dxh, axis=-1, keepdims=True) - xh * jnp.mean(dxh * xh, axis=-1, keepdims=True))
            do_ref[:, sl] = do.astype(BF16)

    return _call(body, name,
                 [_sds((S, 1024), BF16), _sds((S, 1024), F32), _sds((1, 1024), F32), _sds((1, 1024), F32)],
                 (S // tm,),
                 [_rows(tm, 1024, 1), _rows(tm, 1024), _rows(tm, 1024, 4), _whole((1, 1024)), _whole((1, 1024))],
                 [_rows(tm, 1024), _rows(tm, 1024), _whole((1, 1024)), _whole((1, 1024))],
                 sem=("arbitrary",))(dmixin, o, h, gg, gb)


GU_BLOCK = D_FF // N_CHIPS


def _matmul_swiglu(x, w_gu, name, side=None):
    S, K = x.shape
    tm = _pick(S, (512, 256, 128))
    tn = 2 * GU_BLOCK

    def body(x_ref, w_ref, gu_ref, act_ref):
        r = jnp.dot(x_ref[...], w_ref[...], preferred_element_type=F32)
        g, u = r[:, :GU_BLOCK], r[:, GU_BLOCK:]
        gu_ref[...] = r.astype(BF16)
        act_ref[...] = (g * _sigmoid(g) * u).astype(BF16)

    return _call(body, name, [_sds((S, 2 * D_FF), BF16), _sds((S, D_FF), BF16)], (S // tm, N_CHIPS),
                 [pl.BlockSpec((tm, K), lambda i, j: (i, 0)), pl.BlockSpec((K, tn), lambda i, j: (0, j))],
                 [pl.BlockSpec((tm, tn), lambda i, j: (i, j)), pl.BlockSpec((tm, GU_BLOCK), lambda i, j: (i, j))],
                 sem=("parallel", "parallel"), side=side)(x, w_gu)


def _swiglu_bwd(gu, dact, name):
    S = gu.shape[0]
    tm = 128

    def body(gu_ref, d_ref, o_ref):
        for j in range(N_CHIPS):
            at = 2 * GU_BLOCK * j
            g = gu_ref[:, at:at + GU_BLOCK].astype(F32)
            u = gu_ref[:, at + GU_BLOCK:at + 2 * GU_BLOCK].astype(F32)
            d = d_ref[:, GU_BLOCK * j:GU_BLOCK * (j + 1)]
            sg = _sigmoid(g)
            o_ref[:, at:at + GU_BLOCK] = (d * u * (sg * (1.0 + g * (1.0 - sg)))).astype(BF16)
            o_ref[:, at + GU_BLOCK:at + 2 * GU_BLOCK] = (d * (g * sg)).astype(BF16)

    return _call(body, name, _sds((S, 2 * D_FF), BF16), (S // tm,), [_rows(tm, 2 * D_FF), _rows(tm, D_FF)],
                 _rows(tm, 2 * D_FF), sem=("parallel",))(gu, dact)


def _loss_head(y, target, name):
    S, D = y.shape
    tm = 256

    def body(y_ref, t_ref, dy_ref, acc_ref):
        e = y_ref[...] - t_ref[...]
        dy_ref[...] = e / D

        @pl.when(pl.program_id(0) == 0)
        def _():
            acc_ref[...] = jnp.zeros_like(acc_ref)

        acc_ref[...] += jnp.sum(e * e, axis=0, keepdims=True)

    return _call(body, name, [_sds((S, D), F32), _sds((1, D), F32)], (S // tm,), [_rows(tm, D), _rows(tm, D)],
                 [_rows(tm, D), _whole((1, D))], sem=("arbitrary",))(y, target)


def _chunk_mask(T):
    r = lax.shift_right_logical(lax.broadcasted_iota(jnp.int32, (T, T), 0), 6)
    c = lax.shift_right_logical(lax.broadcasted_iota(jnp.int32, (T, T), 1), 6)
    return r >= c


def _dot_nt(a, b):
    return lax.dot_general(a, b, (((1,), (1,)), ((), ())), preferred_element_type=F32)


def _dot_tn(a, b):
    return lax.dot_general(a, b, (((0,), (0,)), ((), ())), preferred_element_type=F32)


def _decay_tables(T):
    lg = jnp.log1p(-jnp.exp2(-5.0 - jnp.arange(RET_HEADS, dtype=F32)))
    idx = jnp.arange(T, dtype=F32)
    diff = idx[:, None] - idx[None, :]
    rel = jnp.exp(lg[:, None, None] * diff[None])
    cid = jnp.arange(T) // CHUNK
    mask = (cid[:, None] >= cid[None, :]).astype(F32)
    reld = jnp.exp(lg[:, None, None] * jnp.abs(diff)[None]) * mask[None]
    lgrow = jnp.broadcast_to(lg[:, None, None], (RET_HEADS, 1, LANES))
    return lgrow, rel, reld


def _attn_fwd(q, k, v, heads, dk, dv, softmax, name, tables=None, side=None):
    S = q.shape[0]
    T = ATT_BLOCK
    nq = S // T
    rep = T // LANES
    vw = 2 * dv if softmax else dv
    assert not softmax or dv == LANES

    def body(*refs):
        if softmax:
            q_ref, k_ref, v_ref, o_ref, lse_ref, m_sc, acc_sc = refs
        else:
            q_ref, k_ref, v_ref, lg_ref, rel_ref, reld_ref, o_ref, acc_sc = refs
        i = pl.program_id(1)
        qv = q_ref[...]

        def kv_block(j):
            rows = pl.ds(pl.multiple_of(j * T, T), T)
            return k_ref[rows, :], v_ref[rows, :]

        kb, vb = kv_block(i)
        s = _dot_nt(qv, kb)
        if softmax:
            s = jnp.where(_chunk_mask(T), s, NEG)
            m = jnp.max(s, axis=-1, keepdims=True)
            p = jnp.exp(s - m)
            m_sc[...] = jnp.broadcast_to(m, (T, LANES))
        else:
            p = s * reld_ref[0]
        acc_sc[...] = jnp.dot(p.astype(BF16), vb, preferred_element_type=F32)

        def scores(j):
            kb, vb = kv_block(j)
            return _dot_nt(qv, kb), vb

        def update(j, s, vb):
            if softmax:
                m_prev = m_sc[...]
                m_next = jnp.maximum(m_prev, jnp.max(s, axis=-1, keepdims=True))
                alpha = jnp.exp(m_prev - m_next)
                p = jnp.exp(s - jnp.tile(m_next, (1, rep)))
                m_sc[...] = m_next
                acc_sc[...] = acc_sc[...] * jnp.tile(alpha, (1, vw // LANES)) + jnp.dot(
                    p.astype(BF16), vb, preferred_element_type=F32)
            else:
                fac = jnp.exp(lg_ref[0] * ((i - j) * T).astype(F32))
                p = s * (rel_ref[0] * jnp.tile(fac, (1, rep)))
                acc_sc[...] += jnp.dot(p.astype(BF16), vb, preferred_element_type=F32)

        def pair(jj, carry):
            first, second = scores(2 * jj), scores(2 * jj + 1)
            update(2 * jj, *first)
            update(2 * jj + 1, *second)
            return carry

        lax.fori_loop(0, i // 2, pair, 0)

        @pl.when(i % 2 == 1)
        def _():
            update(i - 1, *scores(i - 1))

        if softmax:
            l = acc_sc[:, dv:]
            o_ref[...] = acc_sc[:, :dv] / l
            lse_ref[...] = m_sc[...] + jnp.log(l)
        else:
            o_ref[...] = acc_sc[...]

    in_specs = [pl.BlockSpec((T, dk), lambda h, i: (i, h)), pl.BlockSpec((S, dk), lambda h, i: (0, h)),
                pl.BlockSpec((S, vw), lambda h, i: (0, h))]
    o_spec = pl.BlockSpec((T, dv), lambda h, i: (i, h))
    if softmax:
        return _call(body, name, [_sds((S, heads * dv), F32), _sds((S, heads * LANES), F32)], (heads, nq), in_specs,
                     [o_spec, pl.BlockSpec((T, LANES), lambda h, i: (i, h))],
                     scratch=[pltpu.VMEM((T, LANES), F32), pltpu.VMEM((T, vw), F32)],
                     sem=("parallel", "arbitrary"), side=side)(q, k, v)
    lgrow, rel, reld = tables
    in_specs += [pl.BlockSpec((1, 1, LANES), lambda h, i: (h, 0, 0)), pl.BlockSpec((1, T, T), lambda h, i: (h, 0, 0)),
                 pl.BlockSpec((1, T, T), lambda h, i: (h, 0, 0))]
    return _call(body, name, _sds((S, heads * dv), F32), (heads, nq), in_specs, o_spec,
                 scratch=[pltpu.VMEM((T, dv), F32)], sem=("parallel", "arbitrary"), side=side)(q, k, v, lgrow, rel, reld)


def _attn_bwd(q, k, v, do, heads, dk, dv, softmax, name, o=None, lse=None, tables=None, side=None):
    S = q.shape[0]
    T = ATT_BLOCK
    nq = S // T
    rep = T // LANES

    def body(*refs):
        if softmax:
            q_ref, k_ref, v_ref, do_ref, o_ref, lse_ref, dq_ref, dk_ref, dv_ref, dq_sc = refs
        else:
            q_ref, k_ref, v_ref, do_ref, lg_ref, rel_ref, reld_ref, dq_ref, dk_ref, dv_ref, dq_sc = refs
        i = pl.program_id(1)

        @pl.when(i == 0)
        def _():
            dk_ref[...] = jnp.zeros_like(dk_ref)
            dv_ref[...] = jnp.zeros_like(dv_ref)

        qv = q_ref[...]
        dof = do_ref[...].astype(F32)
        dov = dof.astype(BF16)
        if softmax:
            delta = jnp.sum(dof * o_ref[...], axis=-1, keepdims=True)
            lse_t = jnp.tile(lse_ref[...], (1, rep))
        dq_sc[...] = jnp.zeros_like(dq_sc)

        def products(j):
            rows = pl.ds(pl.multiple_of(j * T, T), T)
            kb = k_ref[rows, :]
            return rows, kb, _dot_nt(qv, kb), _dot_nt(dov, v_ref[rows, :])

        def block(j, diagonal, rows, kb, s, dp):
            if softmax:
                if diagonal:
                    s = jnp.where(_chunk_mask(T), s, NEG)
                p = jnp.exp(s - lse_t)
                ds = p * (dp - delta)
            else:
                if diagonal:
                    dec = reld_ref[0]
                else:
                    fac = jnp.exp(lg_ref[0] * ((i - j) * T).astype(F32))
                    dec = rel_ref[0] * jnp.tile(fac, (1, rep))
                p = s * dec
                ds = dp * dec
            dsb = ds.astype(BF16)
            dv_ref[rows, :] += _dot_tn(p.astype(BF16), dov)
            dk_ref[rows, :] += _dot_tn(dsb, qv)
            dq_sc[...] += jnp.dot(dsb, kb, preferred_element_type=F32)

        block(i, True, *products(i))

        def pair(jj, carry):
            first, second = products(2 * jj), products(2 * jj + 1)
            block(2 * jj, False, *first)
            block(2 * jj + 1, False, *second)
            return carry

        lax.fori_loop(0, i // 2, pair, 0)

        @pl.when(i % 2 == 1)
        def _():
            block(i - 1, False, *products(i - 1))

        dq_ref[...] = dq_sc[...]

    qspec = pl.BlockSpec((T, dk), lambda h, i: (i, h))
    kspec = pl.BlockSpec((S, dk), lambda h, i: (0, h))
    vspec = pl.BlockSpec((S, dv), lambda h, i: (0, h))
    dospec = pl.BlockSpec((T, dv), lambda h, i: (i, h))
    in_specs = [qspec, kspec, vspec, dospec]
    args = [q, k, v, do]
    if softmax:
        in_specs[2] = pl.BlockSpec((S, dv), lambda h, i: (0, 2 * h))
        in_specs += [dospec, pl.BlockSpec((T, LANES), lambda h, i: (i, h))]
        args += [o, lse]
    else:
        in_specs += [pl.BlockSpec((1, 1, LANES), lambda h, i: (h, 0, 0)),
                     pl.BlockSpec((1, T, T), lambda h, i: (h, 0, 0)), pl.BlockSpec((1, T, T), lambda h, i: (h, 0, 0))]
        args += list(tables)
    return _call(body, name, [_sds((S, heads * dk), F32), _sds((S, heads * dk), F32), _sds((S, heads * dv), F32)],
                 (heads, nq), in_specs, [qspec, kspec, vspec], scratch=[pltpu.VMEM((T, dk), F32)],
                 sem=("parallel", "arbitrary"), side=side)(*args)


def _rope_tables(pos):
    def tables(dim):
        inv_freq = ROPE_THETA ** (-jnp.arange(0, dim, 2, dtype=F32) / dim)
        ang = pos.astype(F32)[:, None] * inv_freq
        return jnp.cos(ang), jnp.sin(ang)

    cm, sm = tables(ROPE)
    S = pos.shape[0]
    z32, z64 = jnp.zeros((S, 32), F32), jnp.zeros((S, 64), F32)
    cr, sr = tables(RET_DK)
    return (jnp.concatenate([cm, cm, z64], 1), jnp.concatenate([z32, sm, z64], 1),
            jnp.concatenate([-sm, z32, z64], 1), cr, sr)


def _row(v):
    return v.reshape(1, -1).astype(F32)


def _local_step(x, pos, target, pipe, P):
    tabs = _rope_tables(pos)
    dtabs = _decay_tables(ATT_BLOCK)
    xf, xb = _ln_fwd([x], [1.0], _row(P["ln_in_g"]), _row(P["ln_in_b"]), "ln_in", False)
    pipe.gather_first()
    saved = []
    for l in range(DEPTH):
        w = functools.partial(pipe.weight, l)
        t = f"_l{l}"
        h = pipe.run(_matmul, "mm_h" + t, xb, w("w_in"))
        qn, kvn, kr, rq, rk, rv = _prep1(h, tabs, _row(P["q_norm_g"][l]), _row(P["kv_norm_g"][l]), "prep1" + t)
        q = _matmul(qn, w("w_uq"), "mm_q" + t)
        kv = _matmul(kvn, w("w_ukv"), "mm_kv" + t)
        qm, km, vm = _prep2(q, kv, kr, tabs, "prep2" + t)
        a, lse = pipe.run(_attn_fwd, "mla_fwd" + t, qm, km, vm, MLA_HEADS, HEAD_PAD, VDIM, True)
        o = pipe.run(_attn_fwd, "ret_fwd" + t, rq, rk, rv, RET_HEADS, RET_DK, RET_DV, False, tables=dtabs)
        mixin = _gn_gate(a, o, h, _row(P["ret_gn_g"][l]), _row(P["ret_gn_b"][l]), "gn_gate" + t)
        mix = _matmul(mixin, w("w_out"), "mm_mix" + t)
        z1, x1f, x1b = _ln_fwd([xf, mix], [ALPHA, 1.0], _row(P["ln1_g"][l]), _row(P["ln1_b"][l]), "ln1" + t, True)
        gu, act = pipe.run(_matmul_swiglu, "mm_gu" + t, x1b, w("w_gu"))
        f = pipe.run(_matmul, "mm_down" + t, act, w("w_down"))
        z2, x2f, x2b = _ln_fwd([x1f, f], [ALPHA, 1.0], _row(P["ln2_g"][l]), _row(P["ln2_b"][l]), "ln2" + t, True)
        saved.append(dict(xb=xb, h=h, qn=qn, kvn=kvn, rq=rq, rk=rk, rv=rv, qm=qm, km=km, vm=vm, a=a, lse=lse, o=o,
                          mixin=mixin, z1=z1, x1b=x1b, gu=gu, act=act, z2=z2))
        xf, xb = x2f, x2b

    dy, sqerr = _loss_head(xf, target, "loss_head")
    dP = {}
    dys, coefs = [dy], [1.0]
    for l in reversed(range(DEPTH)):
        w, sv = functools.partial(pipe.weight, l), saved[l]
        t = f"_l{l}"
        dz2, dz2b, dg, db = _ln_bwd(dys, coefs, sv["z2"], _row(P["ln2_g"][l]), "ln2_bwd" + t)
        dP[("ln2_g", l)], dP[("ln2_b", l)] = dg, db
        pipe.reduce(l, w_down=pipe.run(_matmul, "mm_dw_down" + t, sv["act"], dz2b, ta=True, out_dtype=BF16))
        dact = pipe.run(_matmul, "mm_dact" + t, dz2b, w("w_down"), tb=True)
        dgu = _swiglu_bwd(sv["gu"], dact, "swiglu_bwd" + t)
        pipe.reduce(l, w_gu=pipe.run(_matmul, "mm_dw_gu" + t, sv["x1b"], dgu, ta=True, out_dtype=BF16))
        dx1 = pipe.run(_matmul, "mm_dx1" + t, dgu, w("w_gu"), tb=True)
        dz1, dz1b, dg, db = _ln_bwd([dz2, dx1], [ALPHA, 1.0], sv["z1"], _row(P["ln1_g"][l]), "ln1_bwd" + t)
        dP[("ln1_g", l)], dP[("ln1_b", l)] = dg, db
        pipe.reduce(l, w_out=_matmul(sv["mixin"], dz1b, "mm_dw_out" + t, ta=True, out_dtype=BF16))
        dmixin = pipe.run(_matmul, "mm_dmixin" + t, dz1b, w("w_out"), tb=True)
        do, drg, dgg, dgb = _gn_gate_bwd(dmixin, sv["o"], sv["h"], _row(P["ret_gn_g"][l]), _row(P["ret_gn_b"][l]),
                                         "gn_gate_bwd" + t)
        dP[("ret_gn_g", l)], dP[("ret_gn_b", l)] = dgg, dgb
        drq, drk, drv = pipe.run(_attn_bwd, "ret_bwd" + t, sv["rq"], sv["rk"], sv["rv"], do, RET_HEADS, RET_DK, RET_DV,
                                 False, tables=dtabs)
        dqm, dkm, dvm = pipe.run(_attn_bwd, "mla_bwd" + t, sv["qm"], sv["km"], sv["vm"], dmixin, MLA_HEADS, HEAD_PAD,
                                 VDIM, True, o=sv["a"], lse=sv["lse"])
        dq, dkv, dkr = _prep2_bwd(dqm, dkm, dvm, tabs, "prep2_bwd" + t)
        g_uq = _matmul(sv["qn"], dq, "mm_dw_uq" + t, ta=True, out_dtype=BF16)
        dqn = _matmul(dq, w("w_uq"), "mm_dqn" + t, tb=True)
        g_ukv = _matmul(sv["kvn"], dkv, "mm_dw_ukv" + t, ta=True, out_dtype=BF16)
        dkvn = _matmul(dkv, w("w_ukv"), "mm_dkvn" + t, tb=True)
        dh, dqg, dkvg = _prep1_bwd(dqn, dkvn, dkr, drq, drk, drv, drg, sv["h"], tabs, _row(P["q_norm_g"][l]),
                                   _row(P["kv_norm_g"][l]), "prep1_bwd" + t)
        dP[("q_norm_g", l)], dP[("kv_norm_g", l)] = dqg, dkvg
        pipe.reduce(l, w_uq=g_uq, w_ukv=g_ukv,
                    w_in=pipe.run(_matmul, "mm_dw_in" + t, sv["xb"], dh, ta=True, out_dtype=BF16))
        dxl = pipe.run(_matmul, "mm_dxl" + t, dh, w("w_in"), tb=True)
        dys, coefs = [dz1, dxl], [ALPHA, 1.0]
    grad_x, _, dg, db = _ln_bwd(dys, coefs, x, _row(P["ln_in_g"]), "ln_in_bwd")
    dP[("ln_in_g", None)], dP[("ln_in_b", None)] = dg, db
    return sqerr, grad_x, dP


INTERNAL_OF = {"w_in": ("w_in",), "w_uq": ("w_uq",), "w_ukv": ("w_ukv",), "w_out": ("w_out",),
               "w_gu": ("w_gate", "w_up"), "w_down": ("w_down",)}


def _internal_weight(name, *blocks):
    cat = lambda parts: jnp.concatenate(parts, axis=1)
    cols = lambda b: cat([b[j] for j in range(N_CHIPS)])
    b = blocks[0]
    if name in ("w_out", "w_down"):
        return b.reshape(-1, b.shape[-1])
    if name == "w_gu":
        return cat([blk[j] for j in range(N_CHIPS) for blk in blocks])
    if name == "w_in":
        return cat([b[0][:, :MLA_IN_USED], jnp.zeros((D_MODEL, MLA_IN - MLA_IN_USED), BF16), b[0][:, MLA_IN_USED:]]
                   + [b[j] for j in range(1, N_CHIPS)])
    if name == "w_uq":
        uq, hw = cols(b), NOPE + ROPE
        pad = jnp.zeros((Q_LORA, HEAD_PAD - hw), BF16)
        return cat([p for h in range(MLA_HEADS) for p in (uq[:, h * hw:(h + 1) * hw], pad)])
    ukv = cols(b)
    return cat([ukv[:, 256 * h:256 * h + NOPE] for h in range(MLA_HEADS)]
               + [ukv[:, 256 * h + NOPE:256 * h + 256] for h in range(MLA_HEADS)])


def _grad_shards(name, g):
    cat = lambda parts: jnp.concatenate(parts, axis=1)
    if name in ("w_out", "w_down"):
        return {name: g.reshape(N_CHIPS, -1, g.shape[-1])}
    if name == "w_gu":
        return {"w_gate": [g[:, 2 * GU_BLOCK * j:2 * GU_BLOCK * j + GU_BLOCK] for j in range(N_CHIPS)],
                "w_up": [g[:, 2 * GU_BLOCK * j + GU_BLOCK:2 * GU_BLOCK * (j + 1)] for j in range(N_CHIPS)]}
    if name == "w_in":
        ci, shift = BIG_SHARD["w_in"][1], MLA_IN - MLA_IN_USED
        return {name: [cat([g[:, :MLA_IN_USED], g[:, MLA_IN:ci + shift]])]
                + [g[:, ci * j + shift:ci * (j + 1) + shift] for j in range(1, N_CHIPS)]}
    if name == "w_uq":
        cq = NOPE + ROPE
        return {name: [cat([g[:, HEAD_PAD * h:HEAD_PAD * h + cq] for h in (2 * j, 2 * j + 1)]) for j in range(N_CHIPS)]}
    return {name: [cat([g[:, o + NOPE * h:o + NOPE * (h + 1)] for h in (2 * j, 2 * j + 1) for o in (0, MLA_HEADS * NOPE)])
                   for j in range(N_CHIPS)]}


def _small_layout(P):
    out, at = {}, 0
    for n in SMALL:
        out[n] = (at, P[n].size)
        at += P[n].size
    return out, at


def _flatten_small(P, last):
    v = jnp.concatenate([P[n].reshape(-1).astype(F32) for n in SMALL] + [last.reshape(-1).astype(F32)])
    return jnp.pad(v, (0, SMALL_ROWS * FLAT_W - v.size)).reshape(SMALL_ROWS, FLAT_W)


def _place():
    return lax.axis_index("x"), lax.axis_index("y"), lax.axis_index("c")


def _other_chips(x, y):
    return [(1 - x, y), (x, 1 - y), (1 - x, 1 - y)]


def _rcopy(src, dst, ssem, rsem, dev):
    return pltpu.make_async_remote_copy(src_ref=src, dst_ref=dst, send_sem=ssem, recv_sem=rsem, device_id=dev,
                                        device_id_type=MESH)


def _comm_call(body, name, out_shape, n_in, scratch):
    many = isinstance(out_shape, (list, tuple))
    return pl.pallas_call(body, name=name, out_shape=out_shape, in_specs=[HBM] * n_in,
                          out_specs=[HBM] * len(out_shape) if many else HBM, scratch_shapes=scratch)


def _half(ref, which):
    rows = ref.shape[0] // 2
    return ref.at[pl.ds(pl.multiple_of(which * rows, 16), rows)]


def _dma_sems(n):
    return pltpu.SemaphoreType.DMA((n,))


def _allgather_side(ws):
    k = len(ws)

    def peers():
        x, y, c = _place()
        return c, 2 * x + y, (x, y, 1 - c), [(n, t, cx, cy) for n in range(k) for t, (cx, cy) in enumerate(_other_chips(x, y))]

    def outgoing(w_refs, g_refs, sems):
        ssem, rsem, _, _, ossem, orsem = sems
        c, j, sib, nt = peers()
        owns = [_rcopy(w_refs[n], g_refs[n].at[j], ossem.at[n], orsem.at[n], sib) for n in range(k)]
        sends = [_rcopy(_half(w_refs[n], c), _half(g_refs[n].at[j], c), ssem.at[3 * n + t], rsem.at[3 * n + t],
                        (cx, cy, c)) for n, t, cx, cy in nt]
        return owns, sends

    def incoming(g_refs, sems):
        ssem, rsem, fssem, frsem, _, _ = sems
        c, _, sib, nt = peers()
        landed, passed, relayed = [], [], []
        for n, t, cx, cy in nt:
            mine, other = (_half(g_refs[n].at[2 * cx + cy], h) for h in (c, 1 - c))
            landed.append(_rcopy(mine, mine, ssem.at[3 * n + t], rsem.at[3 * n + t], (cx, cy, c)))
            passed.append(_rcopy(mine, mine, fssem.at[3 * n + t], frsem.at[3 * n + t], sib))
            relayed.append(_rcopy(other, other, fssem.at[3 * n + t], frsem.at[3 * n + t], sib))
        return landed, passed, relayed

    def start(w_refs, g_refs, sems):
        owns, sends = outgoing(w_refs, g_refs, sems)
        for cp in sends + owns:
            cp.start()

    def finish(w_refs, g_refs, sems):
        owns, sends = outgoing(w_refs, g_refs, sems)
        landed, passed, relayed = incoming(g_refs, sems)
        for got, on in zip(landed, passed):
            got.wait_recv()
            on.start()
        for cp in relayed:
            cp.wait_recv()
        for cp in owns:
            cp.wait()
        for cp in sends + passed:
            cp.wait_send()

    return _Side(list(ws), [_sds((N_CHIPS,) + w.shape, w.dtype) for w in ws],
                 [_dma_sems(3 * k)] * 4 + [_dma_sems(k)] * 2, start, finish)


def _exchange_side(parts):
    k = len(parts)

    def copies(p_refs, rcv_refs, sems):
        ssem, rsem = sems
        x, y, c = _place()
        return [_rcopy(p_refs[n].at[2 * cx + cy], rcv_refs[n].at[t], ssem.at[3 * n + t], rsem.at[3 * n + t], (cx, cy, c))
                for n in range(k) for t, (cx, cy) in enumerate(_other_chips(x, y))]

    def start(p_refs, rcv_refs, sems):
        for cp in copies(p_refs, rcv_refs, sems):
            cp.start()

    def finish(p_refs, rcv_refs, sems):
        for cp in copies(p_refs, rcv_refs, sems):
            cp.wait()

    return _Side(list(parts), [_sds((3,) + p.shape[1:], p.dtype) for p in parts], [_dma_sems(3 * k)] * 2, start, finish)


def _run_side(side, name):
    k_in, k_out = len(side.arrays), len(side.out_shape)

    def body(*refs):
        parts = refs[:k_in], refs[k_in:k_in + k_out], refs[k_in + k_out:]
        side.start(*parts)
        side.finish(*parts)

    return _comm_call(body, name, list(side.out_shape), k_in, list(side.scratch))(*side.arrays)


def _sibling_side(arrays, out_shape, n_copies, copies):
    def start(in_refs, out_refs, sems):
        for cp in copies(in_refs, out_refs, sems):
            cp.start()

    def finish(in_refs, out_refs, sems):
        for cp in copies(in_refs, out_refs, sems):
            cp.wait()

    return _Side(list(arrays), out_shape, [_dma_sems(n_copies)] * 2, start, finish)


def _swap_side(gds):
    k = len(gds)

    def copies(gd_refs, out_refs, sems):
        ssem, rsem = sems
        x, y, c = _place()
        return [_rcopy(_half(gd_refs[n].at[jj], 1 - c), out_refs[n].at[jj], ssem.at[N_CHIPS * n + jj],
                       rsem.at[N_CHIPS * n + jj], (x, y, 1 - c)) for n in range(k) for jj in range(N_CHIPS)]

    return _sibling_side(gds, [_sds((N_CHIPS, g.shape[1] // 2, g.shape[2]), g.dtype) for g in gds], N_CHIPS * k, copies)


def _share_side(reds):
    k = len(reds)

    def copies(r_refs, out_refs, sems):
        ssem, rsem = sems
        x, y, c = _place()
        return [_rcopy(r_refs[n], out_refs[n], ssem.at[n], rsem.at[n], (x, y, 1 - c)) for n in range(k)]

    return _sibling_side(reds, [_sds(r.shape, r.dtype) for r in reds], k, copies)


def _join_sides(sides):
    if len(sides) == 1:
        return sides[0]
    cuts = [(len(s.arrays), len(s.out_shape), len(s.scratch)) for s in sides]

    def each(method, in_refs, out_refs, sems):
        a = o = m = 0
        for s, (ka, ko, km) in zip(sides, cuts):
            getattr(s, method)(in_refs[a:a + ka], out_refs[o:o + ko], sems[m:m + km])
            a, o, m = a + ka, o + ko, m + km

    return _Side([x for s in sides for x in s.arrays], [x for s in sides for x in s.out_shape],
                 [x for s in sides for x in s.scratch], functools.partial(each, "start"), functools.partial(each, "finish"))


def _allreduce_small(small):
    def body(s_ref, all_ref, sssem, srsem, lsem):
        x, y, c = _place()
        me = 4 * x + 2 * y + c
        own = pltpu.make_async_copy(s_ref, all_ref.at[me], lsem)
        own.start()
        cps = []
        for r in range(1, 8):
            fx, fy, fc = (r >> 2) & 1, (r >> 1) & 1, r & 1
            px, py, pc = (1 - x if fx else x, 1 - y if fy else y, 1 - c if fc else c)
            peer = 4 * px + 2 * py + pc
            send = _rcopy(s_ref, all_ref.at[me], sssem.at[r - 1], srsem.at[me], (px, py, pc))
            send.start()
            cps.append((send, _rcopy(s_ref, all_ref.at[peer], sssem.at[r - 1], srsem.at[peer], (px, py, pc))))
        for send, recv in cps:
            send.wait_send()
            recv.wait_recv()
        own.wait()

    return _comm_call(body, "allreduce_small", [_sds((8,) + small.shape, small.dtype)], 1,
                      [pltpu.SemaphoreType.DMA((7,)), pltpu.SemaphoreType.DMA((8,)), pltpu.SemaphoreType.DMA(())])(small)[0]


def _add_pair(gd, got, c, name):
    _, R, W = got.shape
    tm = _pick(R, (512, 256, 128, 64))
    nb = R // tm

    def body(c_ref, a_ref, b_ref, o_ref):
        o_ref[...] = (a_ref[...].astype(F32) + b_ref[...].astype(F32)).astype(o_ref.dtype)

    grid_spec = pltpu.PrefetchScalarGridSpec(
        num_scalar_prefetch=1, grid=(N_CHIPS, nb),
        in_specs=[pl.BlockSpec((None, tm, W), lambda j, i, c_ref: (j, c_ref[0] * nb + i, 0)),
                  pl.BlockSpec((None, tm, W), lambda j, i, c_ref: (j, i, 0))],
        out_specs=pl.BlockSpec((None, tm, W), lambda j, i, c_ref: (j, i, 0)))
    return pl.pallas_call(body, name=name, grid_spec=grid_spec, out_shape=_sds((N_CHIPS, R, W), gd.dtype),
                          compiler_params=pltpu.CompilerParams(dimension_semantics=("parallel", "parallel"),
                                                               vmem_limit_bytes=VMEM_LIMIT))(c, gd, got)


def _add_chips(part, rcv, j, name):
    _, R, W = part.shape
    tm = _pick(R, (512, 256, 128, 64))

    def body(j_ref, p_ref, r0_ref, r1_ref, r2_ref, o_ref):
        o_ref[...] = ((p_ref[...].astype(F32) + r0_ref[...].astype(F32)) + r1_ref[...].astype(F32)) + r2_ref[...].astype(F32)

    def slot(t):
        return pl.BlockSpec((None, tm, W), lambda i, j_ref: (t, i, 0))

    grid_spec = pltpu.PrefetchScalarGridSpec(
        num_scalar_prefetch=1, grid=(R // tm,),
        in_specs=[pl.BlockSpec((None, tm, W), lambda i, j_ref: (j_ref[0], i, 0)), slot(0), slot(1), slot(2)],
        out_specs=pl.BlockSpec((tm, W), lambda i, j_ref: (i, 0)))
    return pl.pallas_call(body, name=name, grid_spec=grid_spec, out_shape=_sds((R, W), F32),
                          compiler_params=pltpu.CompilerParams(dimension_semantics=("parallel",),
                                                               vmem_limit_bytes=VMEM_LIMIT))(j, part, rcv, rcv, rcv)


def _sum_small(allsmall):
    _, R, W = allsmall.shape

    def body(a_ref, o_ref):
        acc = a_ref[0]
        for d in range(1, 8):
            acc = acc + a_ref[d]
        o_ref[...] = acc

    return _call(body, "sum_small", _sds((R, W), F32), (1,), [_whole((8, R, W))], _whole((R, W)),
                 sem=("arbitrary",))(allsmall)


def _adamw(w, g, m, v, name):
    R, C = w.shape
    tm = _pick(R, (256, 128, 64, 32, 8))

    def body(w_ref, g_ref, m_ref, v_ref, d_ref, mo_ref, vo_ref):
        gv = g_ref[...]
        mn = ADAM_B1 * m_ref[...] + (1.0 - ADAM_B1) * gv
        vn = ADAM_B2 * v_ref[...] + (1.0 - ADAM_B2) * (gv * gv)
        m_hat = mn / (1.0 - ADAM_B1 ** ADAM_STEP)
        v_hat = vn / (1.0 - ADAM_B2 ** ADAM_STEP)
        d_ref[...] = -ADAM_LR * (m_hat / (jnp.sqrt(v_hat) + ADAM_EPS) + ADAM_WD * w_ref[...])
        mo_ref[...] = mn
        vo_ref[...] = vn

    spec = _rows(tm, C)
    return _call(body, name, [_sds((R, C), F32)] * 3, (R // tm,), [spec] * 4, [spec] * 3, sem=("parallel",))(w, g, m, v)


def _adamw_layer(c, w, m, v, mine, other, l, prev, name):
    _, R, C = w.shape
    half = R // 2
    tm = _pick(half, (256, 128, 64))
    nbh = half // tm

    def body(c_ref, w_ref, m_ref, v_ref, a_ref, b_ref, *rest):
        g_ref, d_ref, mo_ref, vo_ref = rest[-4:]
        gv = jnp.where(pl.program_id(0) // nbh == c_ref[0], a_ref[...], b_ref[...])
        mn = ADAM_B1 * m_ref[...] + (1.0 - ADAM_B1) * gv
        vn = ADAM_B2 * v_ref[...] + (1.0 - ADAM_B2) * (gv * gv)
        m_hat = mn / (1.0 - ADAM_B1 ** ADAM_STEP)
        v_hat = vn / (1.0 - ADAM_B2 ** ADAM_STEP)
        g_ref[...] = gv
        d_ref[...] = -ADAM_LR * (m_hat / (jnp.sqrt(v_hat) + ADAM_EPS) + ADAM_WD * w_ref[...])
        mo_ref[...] = mn
        vo_ref[...] = vn

    layer = pl.BlockSpec((None, tm, C), lambda i, c_ref: (l, i, 0))
    halfspec = pl.BlockSpec((tm, C), lambda i, c_ref: (i % nbh, 0))
    n_prev = 0 if prev is None else 4
    grid_spec = pltpu.PrefetchScalarGridSpec(
        num_scalar_prefetch=1, grid=(R // tm,),
        in_specs=[layer] * 3 + [halfspec] * 2 + [pl.BlockSpec(memory_space=pl.ANY)] * n_prev,
        out_specs=[layer] * 4)
    return pl.pallas_call(body, name=name, grid_spec=grid_spec, out_shape=[_sds(w.shape, F32)] * 4,
                          input_output_aliases={6 + k: k for k in range(n_prev)},
                          compiler_params=pltpu.CompilerParams(dimension_semantics=("parallel",),
                                                               vmem_limit_bytes=VMEM_LIMIT))(
        c, w, m, v, mine, other, *(prev or ()))


FIRST_GATHER = ("w_in", "w_uq", "w_ukv")
G_DOWN, G_GU, G_OUT, G_IN = ("w_down",), ("w_gate", "w_up"), ("w_out",), ("w_uq", "w_ukv", "w_in")


def _backward_jobs(l):
    t = f"_l{l}"
    return {"mm_dact" + t: [("swap", l, G_DOWN)], "mm_dw_gu" + t: [("exchange", l, G_DOWN)],
            "mm_dx1" + t: [("swap", l, G_GU), ("share", l, G_DOWN)], "mm_dmixin" + t: [("swap", l, G_OUT)],
            "ret_bwd" + t: [("exchange", l, ("w_gate",))],
            "mla_bwd" + t: [("exchange", l, ("w_up", "w_out")), ("share", l, ("w_gate",))],
            "mm_dw_in" + t: [("share", l, ("w_up", "w_out"))]}


JOBS = {
    "mm_h_l0": [("gather", 0, ("w_out",))], "mla_fwd_l0": [("gather", 0, ("w_gate", "w_up"))],
    "ret_fwd_l0": [("gather", 1, ("w_uq", "w_ukv", "w_out"))], "mm_gu_l0": [("gather", 0, ("w_down",))],
    "mm_down_l0": [("gather", 1, ("w_in",))], "mla_fwd_l1": [("gather", 1, ("w_gate", "w_up"))],
    "mm_gu_l1": [("gather", 1, ("w_down",))],
    **_backward_jobs(1), **_backward_jobs(0),
    "mm_dxl_l1": [("swap", 1, G_IN)], "mm_dw_down_l0": [("exchange", 1, G_IN)],
    "mm_dact_l0": [("swap", 0, G_DOWN), ("share", 1, G_IN)], "mm_dxl_l0": [("exchange", 0, G_IN)]}
PLANNED = {job for jobs in JOBS.values() for job in jobs}


class _Pipeline:
    def __init__(self, own, Wt, Mo, Vo, core, chip):
        self.own, self.Wt, self.Mo, self.Vo, self.core, self.chip = own, Wt, Mo, Vo, core, chip
        self.blocks, self.whole, self.gds, self.parts, self.reds = {}, {}, {}, {}, {}
        self.results = {n: None for n in BIG}

    def gather_first(self):
        job = ("gather", 0, FIRST_GATHER)
        self._done(*job, _run_side(self._side(*job), "allgather_first"))

    def weight(self, l, name):
        if (l, name) not in self.whole:
            self.whole[(l, name)] = _internal_weight(name, *[self.blocks[(l, n)] for n in INTERNAL_OF[name]])
        return self.whole[(l, name)]

    def run(self, fn, name, *args, **kw):
        jobs = JOBS.get(name, ())
        if not jobs:
            return fn(*args, name=name, **kw)
        sides = [self._side(*job) for job in jobs]
        out, res = fn(*args, name=name, side=_join_sides(sides), **kw)
        for job, side in zip(jobs, sides):
            k = len(side.out_shape)
            self._done(*job, res[:k])
            res = res[k:]
        return out

    def reduce(self, l, **grads):
        shards = {}
        for name, g in grads.items():
            shards.update(_grad_shards(name, g))
        for n, sh in shards.items():
            self.gds[(l, n)] = sh if hasattr(sh, "shape") else jnp.stack(sh)
        self._alone("swap", l, tuple(shards))

    def _alone(self, kind, l, names):
        if (kind, l, names) not in PLANNED:
            self._done(kind, l, names, _run_side(self._side(kind, l, names), f"{kind}_{names[0]}_l{l}"))

    def _side(self, kind, l, names):
        if kind == "gather":
            return _allgather_side([self.own[l][n] for n in names])
        store = {"swap": self.gds, "exchange": self.parts, "share": self.reds}[kind]
        make = {"swap": _swap_side, "exchange": _exchange_side, "share": _share_side}[kind]
        return make([store[(l, n)] for n in names])

    def _done(self, kind, l, names, res):
        for n, r in zip(names, res):
            if kind == "gather":
                self.blocks[(l, n)] = r
            elif kind == "swap":
                self.parts[(l, n)] = _add_pair(self.gds[(l, n)], r, self.core, f"add_pair_{n}_l{l}")
            elif kind == "exchange":
                self.reds[(l, n)] = _add_chips(self.parts[(l, n)], r, self.chip, f"add_chips_{n}_l{l}")
            else:
                self.results[n] = _adamw_layer(self.core, self.Wt[n], self.Mo[n], self.Vo[n], self.reds[(l, n)], r, l,
                                               self.results[n], f"adamw_{n}_l{l}")
        if kind == "exchange":
            self._alone("share", l, names)


def kernel(x, positions, ln_in_g, ln_in_b, w_in, q_norm_g, kv_norm_g, w_uq, w_ukv, ret_gn_g, ret_gn_b, w_out, ln1_g, ln1_b, w_gate, w_up, w_down, ln2_g, ln2_b, loss_target, m_ln_in_g, m_ln_in_b, m_w_in, m_q_norm_g, m_kv_norm_g, m_w_uq, m_w_ukv, m_ret_gn_g, m_ret_gn_b, m_w_out, m_ln1_g, m_ln1_b, m_w_gate, m_w_up, m_w_down, m_ln2_g, m_ln2_b, v_ln_in_g, v_ln_in_b, v_w_in, v_q_norm_g, v_kv_norm_g, v_w_uq, v_w_ukv, v_ret_gn_g, v_ret_gn_b, v_w_out, v_ln1_g, v_ln1_b, v_w_gate, v_w_up, v_w_down, v_ln2_g, v_ln2_b):
    given = dict(locals())
    Wt = {n: given[n] for n in WEIGHTS}
    Mo = {n: given["m_" + n] for n in WEIGHTS}
    Vo = {n: given["v_" + n] for n in WEIGHTS}
    cx, cy, cc = _place()
    chip = (2 * cx + cy).astype(jnp.int32)
    core = cc.astype(jnp.int32)

    own = [{n: Wt[n][l].astype(BF16) for n in BIG} for l in range(DEPTH)]
    pipe = _Pipeline(own, Wt, Mo, Vo, core.reshape(1), chip.reshape(1))
    sqerr, grad_x, dP = _local_step(x[0], positions[0], loss_target[0], pipe, Wt)
    results = pipe.results

    small_g = {n: (dP[(n, None)] if Wt[n].ndim == 1 else jnp.stack([dP[(n, l)] for l in range(DEPTH)])) for n in SMALL}
    local_loss = 0.5 * jnp.sum(sqerr) / D_MODEL
    small_sum = _sum_small(_allreduce_small(_flatten_small(small_g, local_loss))).reshape(-1)
    layout, n_small = _small_layout(Wt)
    loss = small_sum[n_small]

    grads, deltas, new_m, new_v = {}, {}, {}, {}
    for n in BIG:
        grads[n], deltas[n], new_m[n], new_v[n] = results[n]
    zero = jnp.zeros((), F32)
    d, mn, vn = _adamw(_flatten_small(Wt, zero), small_sum.reshape(SMALL_ROWS, FLAT_W), _flatten_small(Mo, zero),
                       _flatten_small(Vo, zero), "adamw_small")
    for n in SMALL:
        at, size = layout[n]
        pick = lambda a: a.reshape(-1)[at:at + size].reshape(Wt[n].shape)
        grads[n], deltas[n], new_m[n], new_v[n] = pick(small_sum), pick(d), pick(mn), pick(vn)

    return (loss, grad_x[None], *[grads[n] for n in WEIGHTS], *[deltas[n] for n in WEIGHTS],
            *[new_m[n] for n in WEIGHTS], *[new_v[n] for n in WEIGHTS])
```

```python
import functools

import jax
import jax.numpy as jnp
from jax import lax
from jax.experimental import pallas as pl
from jax.experimental.pallas import tpu as pltpu

F32 = jnp.float32
BF16 = jnp.bfloat16

D_MODEL = 2048
DEPTH = 2
CHUNK = 64
MLA_HEADS = 8
Q_LORA = 512
KV_LORA = 256
NOPE = 128
ROPE = 64
VDIM = 128
RET_HEADS = 4
RET_DK = 256
RET_DV = 256
D_FF = 5632
D_IN = 4928
ROPE_THETA = 10000.0
LN_EPS = 1e-5
RMS_EPS = 1e-6
GN_EPS = 1e-5
ALPHA = (2 * DEPTH) ** 0.25
MLA_SCALE = (NOPE + ROPE) ** -0.5
RET_SCALE = RET_DK ** -0.5
ADAM_LR = 0.001
ADAM_B1 = 0.9
ADAM_B2 = 0.999
ADAM_EPS = 1e-08
ADAM_WD = 0.01
ADAM_STEP = 10

LANES = 128
HEAD_PAD = 256
MLA_IN = 1024
MLA_IN_USED = Q_LORA + KV_LORA + ROPE
D_IN_PAD = MLA_IN + 4 * 1024
ATT_BLOCK = 512
NEG = -1e30
VMEM_LIMIT = 56 * 1024 * 1024

N_CHIPS = 4
FLAT_W = 1024
BIG = ("w_in", "w_uq", "w_ukv", "w_out", "w_gate", "w_up", "w_down")
BIG_SHARD = {"w_in": (2048, 1232), "w_uq": (512, 384), "w_ukv": (256, 512), "w_out": (512, 2048),
             "w_gate": (2048, 1408), "w_up": (2048, 1408), "w_down": (1408, 2048)}
SMALL = ("ln_in_g", "ln_in_b", "q_norm_g", "kv_norm_g", "ret_gn_g", "ret_gn_b", "ln1_g", "ln1_b", "ln2_g", "ln2_b")
WEIGHTS = ("ln_in_g", "ln_in_b", "w_in", "q_norm_g", "kv_norm_g", "w_uq", "w_ukv", "ret_gn_g", "ret_gn_b", "w_out",
           "ln1_g", "ln1_b", "w_gate", "w_up", "w_down", "ln2_g", "ln2_b")
SMALL_ROWS = 32

MESH = pl.DeviceIdType.MESH


def _pick(dim, cands):
    for c in cands:
        if dim % c == 0:
            return c
    return dim


HBM = pl.BlockSpec(memory_space=pltpu.HBM)


class _Side:
    def __init__(self, arrays, out_shape, scratch, start, finish):
        self.arrays, self.out_shape, self.scratch, self.start, self.finish = arrays, out_shape, scratch, start, finish


def _call(body, name, out_shape, grid, in_specs, out_specs, scratch=(), sem=None, side=None):
    params = pltpu.CompilerParams(dimension_semantics=sem if side is None else ("arbitrary",) * len(grid),
                                  vmem_limit_bytes=VMEM_LIMIT)
    if side is None:
        return pl.pallas_call(body, name=name, out_shape=out_shape, grid=grid, in_specs=in_specs, out_specs=out_specs,
                              scratch_shapes=list(scratch), compiler_params=params)
    single = not isinstance(out_shape, (list, tuple))
    outs = [out_shape] if single else list(out_shape)
    ospecs = [out_specs] if single else list(out_specs)
    cuts = [len(in_specs), len(side.arrays), len(outs), len(side.out_shape), len(scratch)]
    ends = [sum(cuts[:k + 1]) for k in range(len(cuts))]

    def hosted(*refs):
        ins, s_in, o, s_out, scr = (refs[a:b] for a, b in zip([0] + ends[:-1], ends))
        sems = refs[ends[-1]:]
        ids = [pl.program_id(a) for a in range(len(grid))]
        first = functools.reduce(jnp.logical_and, [i == 0 for i in ids])
        last = functools.reduce(jnp.logical_and, [i == g - 1 for i, g in zip(ids, grid)])

        @pl.when(first)
        def _():
            side.start(s_in, s_out, sems)

        body(*ins, *o, *scr)

        @pl.when(last)
        def _():
            side.finish(s_in, s_out, sems)

    call = pl.pallas_call(hosted, name=name, out_shape=outs + list(side.out_shape), grid=grid,
                          in_specs=list(in_specs) + [HBM] * len(side.arrays),
                          out_specs=ospecs + [HBM] * len(side.out_shape),
                          scratch_shapes=list(scratch) + list(side.scratch), compiler_params=params)

    def run(*args):
        res = call(*args, *side.arrays)
        return (res[0] if single else list(res[:len(outs)])), list(res[len(outs):])

    return run


def _rows(tm, w, col=0):
    return pl.BlockSpec((tm, w), lambda i: (i, col))


def _whole(shape):
    return pl.BlockSpec(shape, lambda i: (0,) * len(shape))


def _sds(shape, dtype):
    return jax.ShapeDtypeStruct(shape, dtype)


def _matmul(a, b, name, ta=False, tb=False, out_dtype=F32, side=None):
    (K, M) = a.shape if ta else a.shape[::-1]
    (N, Kb) = b.shape if tb else b.shape[::-1]
    assert K == Kb, (a.shape, b.shape, ta, tb)
    tm = _pick(M, (1024, 1408, 512, 256, 128))
    tn = _pick(N, (1024, 512, 256, 128))
    tk = _pick(K, (2816, 2560, 2048, 1024, 512, 256))
    nk = K // tk
    dn = (((0 if ta else 1,), (1 if tb else 0,)), ((), ()))

    def body(a_ref, b_ref, o_ref, acc_ref):
        k = pl.program_id(2)
        if nk == 1:
            o_ref[...] = lax.dot_general(a_ref[...].astype(BF16), b_ref[...].astype(BF16), dn,
                                         preferred_element_type=F32).astype(out_dtype)
        else:
            @pl.when(k == 0)
            def _():
                acc_ref[...] = jnp.zeros_like(acc_ref)

            acc_ref[...] += lax.dot_general(a_ref[...].astype(BF16), b_ref[...].astype(BF16), dn,
                                            preferred_element_type=F32)

            @pl.when(k == nk - 1)
            def _():
                o_ref[...] = acc_ref[...].astype(out_dtype)

    a_spec = pl.BlockSpec((tk, tm), lambda i, j, k: (k, i)) if ta else pl.BlockSpec((tm, tk), lambda i, j, k: (i, k))
    b_spec = pl.BlockSpec((tn, tk), lambda i, j, k: (j, k)) if tb else pl.BlockSpec((tk, tn), lambda i, j, k: (k, j))
    return _call(body, name, _sds((M, N), out_dtype), (M // tm, N // tn, nk), [a_spec, b_spec],
                 pl.BlockSpec((tm, tn), lambda i, j, k: (i, j)), scratch=[pltpu.VMEM((tm, tn), F32)],
                 sem=("parallel", "parallel", "arbitrary"), side=side)(a, b)


def _sigmoid(x):
    return 1.0 / (1.0 + jnp.exp(-x))


def _rope_group(r, c, sa, sb):
    return r * c + pltpu.roll(r, 32, 1) * sa + pltpu.roll(r, 96, 1) * sb


def _ln_fwd(xs, coefs, g, b, name, want_z):
    S, D = xs[0].shape
    tm = 256
    n = len(xs)

    def body(*refs):
        x_refs, g_ref, b_ref, outs = refs[:n], refs[n], refs[n + 1], refs[n + 2:]
        z = None
        for cf, r in zip(coefs, x_refs):
            t = r[...] if cf == 1.0 else cf * r[...]
            z = t if z is None else z + t
        mu = jnp.mean(z, axis=-1, keepdims=True)
        zc = z - mu
        var = jnp.mean(zc * zc, axis=-1, keepdims=True)
        y = zc * lax.rsqrt(var + LN_EPS) * g_ref[...] + b_ref[...]
        if want_z:
            outs[0][...] = z
        outs[-2][...] = y
        outs[-1][...] = y.astype(BF16)

    out_shape = [_sds((S, D), F32)] * (2 if want_z else 1) + [_sds((S, D), BF16)]
    return _call(body, name, out_shape, (S // tm,), [_rows(tm, D)] * n + [_whole((1, D))] * 2,
                 [_rows(tm, D)] * len(out_shape), sem=("parallel",))(*xs, g, b)


def _ln_bwd(dys, coefs, z, g, name):
    S, D = z.shape
    tm = 256
    n = len(dys)

    def body(*refs):
        dy_refs, z_ref, g_ref = refs[:n], refs[n], refs[n + 1]
        dz_ref, dzb_ref, dg_ref, db_ref = refs[n + 2:]
        dy = None
        for cf, r in zip(coefs, dy_refs):
            t = r[...] if cf == 1.0 else cf * r[...]
            dy = t if dy is None else dy + t
        zv = z_ref[...]
        mu = jnp.mean(zv, axis=-1, keepdims=True)
        zc = zv - mu
        var = jnp.mean(zc * zc, axis=-1, keepdims=True)
        rstd = lax.rsqrt(var + LN_EPS)
        xh = zc * rstd
        dyg = dy * g_ref[...]
        dz = rstd * (dyg - jnp.mean(dyg, axis=-1, keepdims=True) - xh * jnp.mean(dyg * xh, axis=-1, keepdims=True))
        dz_ref[...] = dz
        dzb_ref[...] = dz.astype(BF16)

        @pl.when(pl.program_id(0) == 0)
        def _():
            dg_ref[...] = jnp.zeros_like(dg_ref)
            db_ref[...] = jnp.zeros_like(db_ref)

        dg_ref[...] += jnp.sum(dy * xh, axis=0, keepdims=True)
        db_ref[...] += jnp.sum(dy, axis=0, keepdims=True)

    return _call(body, name, [_sds((S, D), F32), _sds((S, D), BF16), _sds((1, D), F32), _sds((1, D), F32)],
                 (S // tm,), [_rows(tm, D)] * (n + 1) + [_whole((1, D))],
                 [_rows(tm, D), _rows(tm, D), _whole((1, D)), _whole((1, D))], sem=("arbitrary",))(*dys, z, g)


def _rms(x, g):
    return x * lax.rsqrt(jnp.mean(x * x, axis=-1, keepdims=True) + RMS_EPS) * g


def _prep1(h, tabs, qg, kvg, name):
    S = h.shape[0]
    tm = 256
    cm, sam, sbm, cr, sr = tabs

    def body(h_ref, cm_ref, sam_ref, sbm_ref, cr_ref, sr_ref, qg_ref, kvg_ref,
             qn_ref, kvn_ref, kr_ref, rq_ref, rk_ref, rv_ref):
        qn_ref[...] = _rms(h_ref[:, 0:Q_LORA], qg_ref[...]).astype(BF16)
        kvn_ref[...] = _rms(h_ref[:, Q_LORA:Q_LORA + KV_LORA], kvg_ref[...]).astype(BF16)
        kr_ref[...] = _rope_group(h_ref[:, 768:896], cm_ref[...], sam_ref[...], sbm_ref[...])
        c, s = cr_ref[...], sr_ref[...]
        for hd in range(RET_HEADS):
            for src, dst, scale in ((MLA_IN, rq_ref, RET_SCALE), (MLA_IN + 1024, rk_ref, None)):
                t1 = h_ref[:, src + hd * 256:src + hd * 256 + 128]
                t2 = h_ref[:, src + hd * 256 + 128:src + hd * 256 + 256]
                o1, o2 = t1 * c - t2 * s, t2 * c + t1 * s
                if scale is not None:
                    o1, o2 = o1 * scale, o2 * scale
                dst[:, hd * 256:hd * 256 + 128] = o1.astype(BF16)
                dst[:, hd * 256 + 128:hd * 256 + 256] = o2.astype(BF16)
        rv_ref[...] = h_ref[:, MLA_IN + 2048:MLA_IN + 3072].astype(BF16)

    t128 = _rows(tm, LANES)
    return _call(body, name,
                 [_sds((S, Q_LORA), BF16), _sds((S, KV_LORA), BF16), _sds((S, LANES), F32),
                  _sds((S, 1024), BF16), _sds((S, 1024), BF16), _sds((S, 1024), BF16)],
                 (S // tm,),
                 [_rows(tm, D_IN_PAD), t128, t128, t128, t128, t128, _whole((1, Q_LORA)), _whole((1, KV_LORA))],
                 [_rows(tm, Q_LORA), _rows(tm, KV_LORA), t128, _rows(tm, 1024), _rows(tm, 1024), _rows(tm, 1024)],
                 sem=("parallel",))(h, cm, sam, sbm, cr, sr, qg, kvg)


def _prep1_bwd(dqn, dkvn, dkr, drq, drk, drv, drg, h, tabs, qg, kvg, name):
    S = h.shape[0]
    tm = 256
    cm, sam, sbm, cr, sr = tabs

    def rms_bwd(x, g, dy):
        r = lax.rsqrt(jnp.mean(x * x, axis=-1, keepdims=True) + RMS_EPS)
        dyg = dy * g
        dx = r * dyg - x * (r * r * r) * jnp.mean(dyg * x, axis=-1, keepdims=True)
        return dx, jnp.sum(dy * x * r, axis=0, keepdims=True)

    def body(dqn_ref, dkvn_ref, dkr_ref, drq_ref, drk_ref, drv_ref, drg_ref, h_ref,
             cm_ref, sam_ref, sbm_ref, cr_ref, sr_ref, qg_ref, kvg_ref, dh_ref, dqg_ref, dkvg_ref):
        dcq, dqg = rms_bwd(h_ref[:, 0:Q_LORA], qg_ref[...], dqn_ref[...])
        dckv, dkvg = rms_bwd(h_ref[:, Q_LORA:Q_LORA + KV_LORA], kvg_ref[...], dkvn_ref[...])
        dh_ref[:, 0:Q_LORA] = dcq.astype(BF16)
        dh_ref[:, Q_LORA:Q_LORA + KV_LORA] = dckv.astype(BF16)
        dh_ref[:, 768:896] = _rope_group(dkr_ref[...], cm_ref[...], -sam_ref[...], -sbm_ref[...]).astype(BF16)
        dh_ref[:, 896:1024] = jnp.zeros((tm, LANES), BF16)
        c, s = cr_ref[...], sr_ref[...]
        for hd in range(RET_HEADS):
            for src, dst, scale in ((drq_ref, MLA_IN, RET_SCALE), (drk_ref, MLA_IN + 1024, None)):
                d1 = src[:, hd * 256:hd * 256 + 128]
                d2 = src[:, hd * 256 + 128:hd * 256 + 256]
                if scale is not None:
                    d1, d2 = d1 * scale, d2 * scale
                dh_ref[:, dst + hd * 256:dst + hd * 256 + 128] = (d1 * c + d2 * s).astype(BF16)
                dh_ref[:, dst + hd * 256 + 128:dst + hd * 256 + 256] = (d2 * c - d1 * s).astype(BF16)
        dh_ref[:, MLA_IN + 2048:MLA_IN + 3072] = drv_ref[...].astype(BF16)
        dh_ref[:, MLA_IN + 3072:MLA_IN + 4096] = drg_ref[...].astype(BF16)

        @pl.when(pl.program_id(0) == 0)
        def _():
            dqg_ref[...] = jnp.zeros_like(dqg_ref)
            dkvg_ref[...] = jnp.zeros_like(dkvg_ref)

        dqg_ref[...] += dqg
        dkvg_ref[...] += dkvg

    t128 = _rows(tm, LANES)
    return _call(body, name,
                 [_sds((S, D_IN_PAD), BF16), _sds((1, Q_LORA), F32), _sds((1, KV_LORA), F32)],
                 (S // tm,),
                 [_rows(tm, Q_LORA), _rows(tm, KV_LORA), t128, _rows(tm, 1024), _rows(tm, 1024), _rows(tm, 1024),
                  _rows(tm, 1024), _rows(tm, MLA_IN), t128, t128, t128, t128, t128,
                  _whole((1, Q_LORA)), _whole((1, KV_LORA))],
                 [_rows(tm, D_IN_PAD), _whole((1, Q_LORA)), _whole((1, KV_LORA))],
                 sem=("arbitrary",))(dqn, dkvn, dkr, drq, drk, drv, drg, h, cm, sam, sbm, cr, sr, qg, kvg)


def _prep2(q, kv, kr, tabs, name):
    S = q.shape[0]
    tm = 256
    cm, sam, sbm = tabs[:3]

    def body(q_ref, kv_ref, kr_ref, cm_ref, sam_ref, sbm_ref, qo_ref, ko_ref, vo_ref):
        c, sa, sb = cm_ref[...], sam_ref[...], sbm_ref[...]
        krb = kr_ref[...].astype(BF16)
        ones = jnp.ones((tm, LANES), BF16)
        for hd in range(MLA_HEADS):
            o = hd * HEAD_PAD
            qo_ref[:, o:o + 128] = (q_ref[:, o:o + 128] * MLA_SCALE).astype(BF16)
            qo_ref[:, o + 128:o + 256] = (_rope_group(q_ref[:, o + 128:o + 256], c, sa, sb) * MLA_SCALE).astype(BF16)
            ko_ref[:, o:o + 128] = kv_ref[:, hd * 128:hd * 128 + 128].astype(BF16)
            ko_ref[:, o + 128:o + 256] = krb
            vo_ref[:, o:o + 128] = kv_ref[:, 1024 + hd * 128:1024 + hd * 128 + 128].astype(BF16)
            vo_ref[:, o + 128:o + 256] = ones

    t128 = _rows(tm, LANES)
    return _call(body, name, [_sds((S, 2048), BF16)] * 3, (S // tm,),
                 [_rows(tm, 2048), _rows(tm, 2048), t128, t128, t128, t128],
                 [_rows(tm, 2048)] * 3, sem=("parallel",))(q, kv, kr, cm, sam, sbm)


def _prep2_bwd(dqm, dkm, dvm, tabs, name):
    S = dqm.shape[0]
    tm = 256
    cm, sam, sbm = tabs[:3]

    def body(dq_ref, dk_ref, dv_ref, cm_ref, sam_ref, sbm_ref, dqo_ref, dkvo_ref, dkr_ref):
        c, sa, sb = cm_ref[...], -sam_ref[...], -sbm_ref[...]
        dkr = None
        for hd in range(MLA_HEADS):
            o = hd * HEAD_PAD
            dqo_ref[:, o:o + 128] = (dq_ref[:, o:o + 128] * MLA_SCALE).astype(BF16)
            dqo_ref[:, o + 128:o + 256] = (_rope_group(dq_ref[:, o + 128:o + 256], c, sa, sb) * MLA_SCALE).astype(BF16)
            dkvo_ref[:, hd * 128:hd * 128 + 128] = dk_ref[:, o:o + 128].astype(BF16)
            t = dk_ref[:, o + 128:o + 256]
            dkr = t if dkr is None else dkr + t
        dkvo_ref[:, 1024:2048] = dv_ref[...].astype(BF16)
        dkr_ref[...] = dkr

    t128 = _rows(tm, LANES)
    return _call(body, name, [_sds((S, 2048), BF16), _sds((S, 2048), BF16), _sds((S, LANES), F32)], (S // tm,),
                 [_rows(tm, 2048), _rows(tm, 2048), _rows(tm, 1024), t128, t128, t128],
                 [_rows(tm, 2048), _rows(tm, 2048), t128], sem=("parallel",))(dqm, dkm, dvm, cm, sam, sbm)


def _gn_gate(a, o, h, gg, gb, name):
    S = a.shape[0]
    tm = 256

    def body(a_ref, o_ref, rg_ref, gg_ref, gb_ref, mix_ref):
        mix_ref[:, 0:1024] = a_ref[...].astype(BF16)
        for hd in range(RET_HEADS):
            sl = slice(hd * 256, hd * 256 + 256)
            ov = o_ref[:, sl]
            mu = jnp.mean(ov, axis=-1, keepdims=True)
            oc = ov - mu
            var = jnp.mean(oc * oc, axis=-1, keepdims=True)
            y = oc * lax.rsqrt(var + GN_EPS) * gg_ref[:, sl] + gb_ref[:, sl]
            rg = rg_ref[:, sl]
            mix_ref[:, 1024 + hd * 256:1024 + hd * 256 + 256] = (rg * _sigmoid(rg) * y).astype(BF16)

    return _call(body, name, _sds((S, 2048), BF16), (S // tm,),
                 [_rows(tm, 1024), _rows(tm, 1024), _rows(tm, 1024, 4), _whole((1, 1024)), _whole((1, 1024))],
                 _rows(tm, 2048), sem=("parallel",))(a, o, h, gg, gb)


def _gn_gate_bwd(dmixin, o, h, gg, gb, name):
    S = o.shape[0]
    tm = 256

    def body(dr_ref, o_ref, rg_ref, gg_ref, gb_ref, do_ref, drg_ref, dgg_ref, dgb_ref):
        @pl.when(pl.program_id(0) == 0)
        def _():
            dgg_ref[...] = jnp.zeros_like(dgg_ref)
            dgb_ref[...] = jnp.zeros_like(dgb_ref)

        for hd in range(RET_HEADS):
            sl = slice(hd * 256, hd * 256 + 256)
            ov = o_ref[:, sl]
            mu = jnp.mean(ov, axis=-1, keepdims=True)
            oc = ov - mu
            var = jnp.mean(oc * oc, axis=-1, keepdims=True)
            rstd = lax.rsqrt(var + GN_EPS)
            xh = oc * rstd
            g = gg_ref[:, sl]
            y = xh * g + gb_ref[:, sl]
            rg = rg_ref[:, sl]
            sg = _sigmoid(rg)
            dr = dr_ref[:, sl]
            dy = dr * (rg * sg)
            drg_ref[:, sl] = dr * y * (sg * (1.0 + rg * (1.0 - sg)))
            dgg_ref[:, sl] += jnp.sum(dy * xh, axis=0, keepdims=True)
            dgb_ref[:, sl] += jnp.sum(dy, axis=0, keepdims=True)
            dxh = dy * g
            do = rstd * (dxh - jnp.mean(dxh, axis=-1, keepdims=True) - xh * jnp.mean(dxh * xh, axis=-1, keepdims=True))
            do_ref[:, sl] = do.astype(BF16)

    return _call(body, name,
                 [_sds((S, 1024), BF16), _sds((S, 1024), F32), _sds((1, 1024), F32), _sds((1, 1024), F32)],
                 (S // tm,),
                 [_rows(tm, 1024, 1), _rows(tm, 1024), _rows(tm, 1024, 4), _whole((1, 1024)), _whole((1, 1024))],
                 [_rows(tm, 1024), _rows(tm, 1024), _whole((1, 1024)), _whole((1, 1024))],
                 sem=("arbitrary",))(dmixin, o, h, gg, gb)


GU_BLOCK = D_FF // N_CHIPS


def _matmul_swiglu(x, w_gu, name, side=None):
    S, K = x.shape
    tm = _pick(S, (512, 256, 128))
    tn = 2 * GU_BLOCK

    def body(x_ref, w_ref, gu_ref, act_ref):
        r = jnp.dot(x_ref[...], w_ref[...], preferred_element_type=F32)
        g, u = r[:, :GU_BLOCK], r[:, GU_BLOCK:]
        gu_ref[...] = r.astype(BF16)
        act_ref[...] = (g * _sigmoid(g) * u).astype(BF16)

    return _call(body, name, [_sds((S, 2 * D_FF), BF16), _sds((S, D_FF), BF16)], (S // tm, N_CHIPS),
                 [pl.BlockSpec((tm, K), lambda i, j: (i, 0)), pl.BlockSpec((K, tn), lambda i, j: (0, j))],
                 [pl.BlockSpec((tm, tn), lambda i, j: (i, j)), pl.BlockSpec((tm, GU_BLOCK), lambda i, j: (i, j))],
                 sem=("parallel", "parallel"), side=side)(x, w_gu)


def _swiglu_bwd(gu, dact, name):
    S = gu.shape[0]
    tm = 128

    def body(gu_ref, d_ref, o_ref):
        for j in range(N_CHIPS):
            at = 2 * GU_BLOCK * j
            g = gu_ref[:, at:at + GU_BLOCK].astype(F32)
            u = gu_ref[:, at + GU_BLOCK:at + 2 * GU_BLOCK].astype(F32)
            d = d_ref[:, GU_BLOCK * j:GU_BLOCK * (j + 1)]
            sg = _sigmoid(g)
            o_ref[:, at:at + GU_BLOCK] = (d * u * (sg * (1.0 + g * (1.0 - sg)))).astype(BF16)
            o_ref[:, at + GU_BLOCK:at + 2 * GU_BLOCK] = (d * (g * sg)).astype(BF16)

    return _call(body, name, _sds((S, 2 * D_FF), BF16), (S // tm,), [_rows(tm, 2 * D_FF), _rows(tm, D_FF)],
                 _rows(tm, 2 * D_FF), sem=("parallel",))(gu, dact)


def _loss_head(y, target, name):
    S, D = y.shape
    tm = 256

    def body(y_ref, t_ref, dy_ref, acc_ref):
        e = y_ref[...] - t_ref[...]
        dy_ref[...] = e / D

        @pl.when(pl.program_id(0) == 0)
        def _():
            acc_ref[...] = jnp.zeros_like(acc_ref)

        acc_ref[...] += jnp.sum(e * e, axis=0, keepdims=True)

    return _call(body, name, [_sds((S, D), F32), _sds((1, D), F32)], (S // tm,), [_rows(tm, D), _rows(tm, D)],
                 [_rows(tm, D), _whole((1, D))], sem=("arbitrary",))(y, target)


def _chunk_mask(T):
    r = lax.shift_right_logical(lax.broadcasted_iota(jnp.int32, (T, T), 0), 6)
    c = lax.shift_right_logical(lax.broadcasted_iota(jnp.int32, (T, T), 1), 6)
    return r >= c


def _dot_nt(a, b):
    return lax.dot_general(a, b, (((1,), (1,)), ((), ())), preferred_element_type=F32)


def _dot_tn(a, b):
    return lax.dot_general(a, b, (((0,), (0,)), ((), ())), preferred_element_type=F32)


def _decay_tables(T):
    lg = jnp.log1p(-jnp.exp2(-5.0 - jnp.arange(RET_HEADS, dtype=F32)))
    idx = jnp.arange(T, dtype=F32)
    diff = idx[:, None] - idx[None, :]
    rel = jnp.exp(lg[:, None, None] * diff[None])
    cid = jnp.arange(T) // CHUNK
    mask = (cid[:, None] >= cid[None, :]).astype(F32)
    reld = jnp.exp(lg[:, None, None] * jnp.abs(diff)[None]) * mask[None]
    lgrow = jnp.broadcast_to(lg[:, None, None], (RET_HEADS, 1, LANES))
    return lgrow, rel, reld


def _attn_fwd(q, k, v, heads, dk, dv, softmax, name, tables=None, side=None):
    S = q.shape[0]
    T = ATT_BLOCK
    nq = S // T
    rep = T // LANES
    vw = 2 * dv if softmax else dv
    assert not softmax or dv == LANES

    def body(*refs):
        if softmax:
            q_ref, k_ref, v_ref, o_ref, lse_ref, m_sc, acc_sc = refs
        else:
            q_ref, k_ref, v_ref, lg_ref, rel_ref, reld_ref, o_ref, acc_sc = refs
        i = pl.program_id(1)
        qv = q_ref[...]

        def kv_block(j):
            rows = pl.ds(pl.multiple_of(j * T, T), T)
            return k_ref[rows, :], v_ref[rows, :]

        kb, vb = kv_block(i)
        s = _dot_nt(qv, kb)
        if softmax:
            s = jnp.where(_chunk_mask(T), s, NEG)
            m = jnp.max(s, axis=-1, keepdims=True)
            p = jnp.exp(s - m)
            m_sc[...] = jnp.broadcast_to(m, (T, LANES))
        else:
            p = s * reld_ref[0]
        acc_sc[...] = jnp.dot(p.astype(BF16), vb, preferred_element_type=F32)

        def scores(j):
            kb, vb = kv_block(j)
            return _dot_nt(qv, kb), vb

        def update(j, s, vb):
            if softmax:
                m_prev = m_sc[...]
                m_next = jnp.maximum(m_prev, jnp.max(s, axis=-1, keepdims=True))
                alpha = jnp.exp(m_prev - m_next)
                p = jnp.exp(s - jnp.tile(m_next, (1, rep)))
                m_sc[...] = m_next
                acc_sc[...] = acc_sc[...] * jnp.tile(alpha, (1, vw // LANES)) + jnp.dot(
                    p.astype(BF16), vb, preferred_element_type=F32)
            else:
                fac = jnp.exp(lg_ref[0] * ((i - j) * T).astype(F32))
                p = s * (rel_ref[0] * jnp.tile(fac, (1, rep)))
                acc_sc[...] += jnp.dot(p.astype(BF16), vb, preferred_element_type=F32)

        def pair(jj, carry):
            first, second = scores(2 * jj), scores(2 * jj + 1)
            update(2 * jj, *first)
            update(2 * jj + 1, *second)
            return carry

        lax.fori_loop(0, i // 2, pair, 0)

        @pl.when(i % 2 == 1)
        def _():
            update(i - 1, *scores(i - 1))

        if softmax:
            l = acc_sc[:, dv:]
            o_ref[...] = acc_sc[:, :dv] / l
            lse_ref[...] = m_sc[...] + jnp.log(l)
        else:
            o_ref[...] = acc_sc[...]

    in_specs = [pl.BlockSpec((T, dk), lambda h, i: (i, h)), pl.BlockSpec((S, dk), lambda h, i: (0, h)),
                pl.BlockSpec((S, vw), lambda h, i: (0, h))]
    o_spec = pl.BlockSpec((T, dv), lambda h, i: (i, h))
    if softmax:
        return _call(body, name, [_sds((S, heads * dv), F32), _sds((S, heads * LANES), F32)], (heads, nq), in_specs,
                     [o_spec, pl.BlockSpec((T, LANES), lambda h, i: (i, h))],
                     scratch=[pltpu.VMEM((T, LANES), F32), pltpu.VMEM((T, vw), F32)],
                     sem=("parallel", "arbitrary"), side=side)(q, k, v)
    lgrow, rel, reld = tables
    in_specs += [pl.BlockSpec((1, 1, LANES), lambda h, i: (h, 0, 0)), pl.BlockSpec((1, T, T), lambda h, i: (h, 0, 0)),
                 pl.BlockSpec((1, T, T), lambda h, i: (h, 0, 0))]
    return _call(body, name, _sds((S, heads * dv), F32), (heads, nq), in_specs, o_spec,
                 scratch=[pltpu.VMEM((T, dv), F32)], sem=("parallel", "arbitrary"), side=side)(q, k, v, lgrow, rel, reld)


def _attn_bwd(q, k, v, do, heads, dk, dv, softmax, name, o=None, lse=None, tables=None, side=None):
    S = q.shape[0]
    T = ATT_BLOCK
    nq = S // T
    rep = T // LANES

    def body(*refs):
        if softmax:
            q_ref, k_ref, v_ref, do_ref, o_ref, lse_ref, dq_ref, dk_ref, dv_ref, dq_sc = refs
        else:
            q_ref, k_ref, v_ref, do_ref, lg_ref, rel_ref, reld_ref, dq_ref, dk_ref, dv_ref, dq_sc = refs
        i = pl.program_id(1)

        @pl.when(i == 0)
        def _():
            dk_ref[...] = jnp.zeros_like(dk_ref)
            dv_ref[...] = jnp.zeros_like(dv_ref)

        qv = q_ref[...]
        dof = do_ref[...].astype(F32)
        dov = dof.astype(BF16)
        if softmax:
            delta = jnp.sum(dof * o_ref[...], axis=-1, keepdims=True)
            lse_t = jnp.tile(lse_ref[...], (1, rep))
        dq_sc[...] = jnp.zeros_like(dq_sc)

        def products(j):
            rows = pl.ds(pl.multiple_of(j * T, T), T)
            kb = k_ref[rows, :]
            return rows, kb, _dot_nt(qv, kb), _dot_nt(dov, v_ref[rows, :])

        def block(j, diagonal, rows, kb, s, dp):
            if softmax:
                if diagonal:
                    s = jnp.where(_chunk_mask(T), s, NEG)
                p = jnp.exp(s - lse_t)
                ds = p * (dp - delta)
            else:
                if diagonal:
                    dec = reld_ref[0]
                else:
                    fac = jnp.exp(lg_ref[0] * ((i - j) * T).astype(F32))
                    dec = rel_ref[0] * jnp.tile(fac, (1, rep))
                p = s * dec
                ds = dp * dec
            dsb = ds.astype(BF16)
            dv_ref[rows, :] += _dot_tn(p.astype(BF16), dov)
            dk_ref[rows, :] += _dot_tn(dsb, qv)
            dq_sc[...] += jnp.dot(dsb, kb, preferred_element_type=F32)

        block(i, True, *products(i))

        def pair(jj, carry):
            first, second = products(2 * jj), products(2 * jj + 1)
            block(2 * jj, False, *first)
            block(2 * jj + 1, False, *second)
            return carry

        lax.fori_loop(0, i // 2, pair, 0)

        @pl.when(i % 2 == 1)
        def _():
            block(i - 1, False, *products(i - 1))

        dq_ref[...] = dq_sc[...]

    qspec = pl.BlockSpec((T, dk), lambda h, i: (i, h))
    kspec = pl.BlockSpec((S, dk), lambda h, i: (0, h))
    vspec = pl.BlockSpec((S, dv), lambda h, i: (0, h))
    dospec = pl.BlockSpec((T, dv), lambda h, i: (i, h))
    in_specs = [qspec, kspec, vspec, dospec]
    args = [q, k, v, do]
    if softmax:
        in_specs[2] = pl.BlockSpec((S, dv), lambda h, i: (0, 2 * h))
        in_specs += [dospec, pl.BlockSpec((T, LANES), lambda h, i: (i, h))]
        args += [o, lse]
    else:
        in_specs += [pl.BlockSpec((1, 1, LANES), lambda h, i: (h, 0, 0)),
                     pl.BlockSpec((1, T, T), lambda h, i: (h, 0, 0)), pl.BlockSpec((1, T, T), lambda h, i: (h, 0, 0))]
        args += list(tables)
    return _call(body, name, [_sds((S, heads * dk), F32), _sds((S, heads * dk), F32), _sds((S, heads * dv), F32)],
                 (heads, nq), in_specs, [qspec, kspec, vspec], scratch=[pltpu.VMEM((T, dk), F32)],
                 sem=("parallel", "arbitrary"), side=side)(*args)


def _rope_tables(pos):
    def tables(dim):
        inv_freq = ROPE_THETA ** (-jnp.arange(0, dim, 2, dtype=F32) / dim)
        ang = pos.astype(F32)[:, None] * inv_freq
        return jnp.cos(ang), jnp.sin(ang)

    cm, sm = tables(ROPE)
    S = pos.shape[0]
    z32, z64 = jnp.zeros((S, 32), F32), jnp.zeros((S, 64), F32)
    cr, sr = tables(RET_DK)
    return (jnp.concatenate([cm, cm, z64], 1), jnp.concatenate([z32, sm, z64], 1),
            jnp.concatenate([-sm, z32, z64], 1), cr, sr)


def _row(v):
    return v.reshape(1, -1).astype(F32)


def _local_step(x, pos, target, pipe, P):
    tabs = _rope_tables(pos)
    dtabs = _decay_tables(ATT_BLOCK)
    xf, xb = _ln_fwd([x], [1.0], _row(P["ln_in_g"]), _row(P["ln_in_b"]), "ln_in", False)
    pipe.gather_first()
    saved = []
    for l in range(DEPTH):
        w = functools.partial(pipe.weight, l)
        t = f"_l{l}"
        h = pipe.run(_matmul, "mm_h" + t, xb, w("w_in"))
        qn, kvn, kr, rq, rk, rv = _prep1(h, tabs, _row(P["q_norm_g"][l]), _row(P["kv_norm_g"][l]), "prep1" + t)
        q = _matmul(qn, w("w_uq"), "mm_q" + t)
        kv = _matmul(kvn, w("w_ukv"), "mm_kv" + t)
        qm, km, vm = _prep2(q, kv, kr, tabs, "prep2" + t)
        a, lse = pipe.run(_attn_fwd, "mla_fwd" + t, qm, km, vm, MLA_HEADS, HEAD_PAD, VDIM, True)
        o = pipe.run(_attn_fwd, "ret_fwd" + t, rq, rk, rv, RET_HEADS, RET_DK, RET_DV, False, tables=dtabs)
        mixin = _gn_gate(a, o, h, _row(P["ret_gn_g"][l]), _row(P["ret_gn_b"][l]), "gn_gate" + t)
        mix = _matmul(mixin, w("w_out"), "mm_mix" + t)
        z1, x1f, x1b = _ln_fwd([xf, mix], [ALPHA, 1.0], _row(P["ln1_g"][l]), _row(P["ln1_b"][l]), "ln1" + t, True)
        gu, act = pipe.run(_matmul_swiglu, "mm_gu" + t, x1b, w("w_gu"))
        f = pipe.run(_matmul, "mm_down" + t, act, w("w_down"))
        z2, x2f, x2b = _ln_fwd([x1f, f], [ALPHA, 1.0], _row(P["ln2_g"][l]), _row(P["ln2_b"][l]), "ln2" + t, True)
        saved.append(dict(xb=xb, h=h, qn=qn, kvn=kvn, rq=rq, rk=rk, rv=rv, qm=qm, km=km, vm=vm, a=a, lse=lse, o=o,
                          mixin=mixin, z1=z1, x1b=x1b, gu=gu, act=act, z2=z2))
        xf, xb = x2f, x2b

    dy, sqerr = _loss_head(xf, target, "loss_head")
    dP = {}
    dys, coefs = [dy], [1.0]
    for l in reversed(range(DEPTH)):
        w, sv = functools.partial(pipe.weight, l), saved[l]
        t = f"_l{l}"
        dz2, dz2b, dg, db = _ln_bwd(dys, coefs, sv["z2"], _row(P["ln2_g"][l]), "ln2_bwd" + t)
        dP[("ln2_g", l)], dP[("ln2_b", l)] = dg, db
        pipe.reduce(l, w_down=pipe.run(_matmul, "mm_dw_down" + t, sv["act"], dz2b, ta=True, out_dtype=BF16))
        dact = pipe.run(_matmul, "mm_dact" + t, dz2b, w("w_down"), tb=True)
        dgu = _swiglu_bwd(sv["gu"], dact, "swiglu_bwd" + t)
        pipe.reduce(l, w_gu=pipe.run(_matmul, "mm_dw_gu" + t, sv["x1b"], dgu, ta=True, out_dtype=BF16))
        dx1 = pipe.run(_matmul, "mm_dx1" + t, dgu, w("w_gu"), tb=True)
        dz1, dz1b, dg, db = _ln_bwd([dz2, dx1], [ALPHA, 1.0], sv["z1"], _row(P["ln1_g"][l]), "ln1_bwd" + t)
        dP[("ln1_g", l)], dP[("ln1_b", l)] = dg, db
        pipe.reduce(l, w_out=_matmul(sv["mixin"], dz1b, "mm_dw_out" + t, ta=True, out_dtype=BF16))
        dmixin = pipe.run(_matmul, "mm_dmixin" + t, dz1b, w("w_out"), tb=True)
        do, drg, dgg, dgb = _gn_gate_bwd(dmixin, sv["o"], sv["h"], _row(P["ret_gn_g"][l]), _row(P["ret_gn_b"][l]),
                                         "gn_gate_bwd" + t)
        dP[("ret_gn_g", l)], dP[("ret_gn_b", l)] = dgg, dgb
        drq, drk, drv = pipe.run(_attn_bwd, "ret_bwd" + t, sv["rq"], sv["rk"], sv["rv"], do, RET_HEADS, RET_DK, RET_DV,
                                 False, tables=dtabs)
        dqm, dkm, dvm = pipe.run(_attn_bwd, "mla_bwd" + t, sv["qm"], sv["km"], sv["vm"], dmixin, MLA_HEADS, HEAD_PAD,
                                 VDIM, True, o=sv["a"], lse=sv["lse"])
        dq, dkv, dkr = _prep2_bwd(dqm, dkm, dvm, tabs, "prep2_bwd" + t)
        g_uq = _matmul(sv["qn"], dq, "mm_dw_uq" + t, ta=True, out_dtype=BF16)
        dqn = _matmul(dq, w("w_uq"), "mm_dqn" + t, tb=True)
        g_ukv = _matmul(sv["kvn"], dkv, "mm_dw_ukv" + t, ta=True, out_dtype=BF16)
        dkvn = _matmul(dkv, w("w_ukv"), "mm_dkvn" + t, tb=True)
        dh, dqg, dkvg = _prep1_bwd(dqn, dkvn, dkr, drq, drk, drv, drg, sv["h"], tabs, _row(P["q_norm_g"][l]),
                                   _row(P["kv_norm_g"][l]), "prep1_bwd" + t)
        dP[("q_norm_g", l)], dP[("kv_norm_g", l)] = dqg, dkvg
        pipe.reduce(l, w_uq=g_uq, w_ukv=g_ukv,
                    w_in=pipe.run(_matmul, "mm_dw_in" + t, sv["xb"], dh, ta=True, out_dtype=BF16))
        dxl = pipe.run(_matmul, "mm_dxl" + t, dh, w("w_in"), tb=True)
        dys, coefs = [dz1, dxl], [ALPHA, 1.0]
    grad_x, _, dg, db = _ln_bwd(dys, coefs, x, _row(P["ln_in_g"]), "ln_in_bwd")
    dP[("ln_in_g", None)], dP[("ln_in_b", None)] = dg, db
    return sqerr, grad_x, dP


INTERNAL_OF = {"w_in": ("w_in",), "w_uq": ("w_uq",), "w_ukv": ("w_ukv",), "w_out": ("w_out",),
               "w_gu": ("w_gate", "w_up"), "w_down": ("w_down",)}


def _internal_weight(name, *blocks):
    cat = lambda parts: jnp.concatenate(parts, axis=1)
    cols = lambda b: cat([b[j] for j in range(N_CHIPS)])
    b = blocks[0]
    if name in ("w_out", "w_down"):
        return b.reshape(-1, b.shape[-1])
    if name == "w_gu":
        return cat([blk[j] for j in range(N_CHIPS) for blk in blocks])
    if name == "w_in":
        return cat([b[0][:, :MLA_IN_USED], jnp.zeros((D_MODEL, MLA_IN - MLA_IN_USED), BF16), b[0][:, MLA_IN_USED:]]
                   + [b[j] for j in range(1, N_CHIPS)])
    if name == "w_uq":
        uq, hw = cols(b), NOPE + ROPE
        pad = jnp.zeros((Q_LORA, HEAD_PAD - hw), BF16)
        return cat([p for h in range(MLA_HEADS) for p in (uq[:, h * hw:(h + 1) * hw], pad)])
    ukv = cols(b)
    return cat([ukv[:, 256 * h:256 * h + NOPE] for h in range(MLA_HEADS)]
               + [ukv[:, 256 * h + NOPE:256 * h + 256] for h in range(MLA_HEADS)])


def _grad_shards(name, g):
    cat = lambda parts: jnp.concatenate(parts, axis=1)
    if name in ("w_out", "w_down"):
        return {name: g.reshape(N_CHIPS, -1, g.shape[-1])}
    if name == "w_gu":
        return {"w_gate": _ColBlocks(g, 0), "w_up": _ColBlocks(g, 1)}
    if name == "w_in":
        ci, shift = BIG_SHARD["w_in"][1], MLA_IN - MLA_IN_USED
        return {name: [cat([g[:, :MLA_IN_USED], g[:, MLA_IN:ci + shift]])]
                + [g[:, ci * j + shift:ci * (j + 1) + shift] for j in range(1, N_CHIPS)]}
    if name == "w_uq":
        cq = NOPE + ROPE
        return {name: [cat([g[:, HEAD_PAD * h:HEAD_PAD * h + cq] for h in (2 * j, 2 * j + 1)]) for j in range(N_CHIPS)]}
    return {name: [cat([g[:, o + NOPE * h:o + NOPE * (h + 1)] for h in (2 * j, 2 * j + 1) for o in (0, MLA_HEADS * NOPE)])
                   for j in range(N_CHIPS)]}


def _small_layout(P):
    out, at = {}, 0
    for n in SMALL:
        out[n] = (at, P[n].size)
        at += P[n].size
    return out, at


def _flatten_small(P, last):
    v = jnp.concatenate([P[n].reshape(-1).astype(F32) for n in SMALL] + [last.reshape(-1).astype(F32)])
    return jnp.pad(v, (0, SMALL_ROWS * FLAT_W - v.size)).reshape(SMALL_ROWS, FLAT_W)


def _place():
    return lax.axis_index("x"), lax.axis_index("y"), lax.axis_index("c")


def _other_chips(x, y):
    return [(1 - x, y), (x, 1 - y), (1 - x, 1 - y)]


def _rcopy(src, dst, ssem, rsem, dev):
    return pltpu.make_async_remote_copy(src_ref=src, dst_ref=dst, send_sem=ssem, recv_sem=rsem, device_id=dev,
                                        device_id_type=MESH)


def _comm_call(body, name, out_shape, n_in, scratch):
    many = isinstance(out_shape, (list, tuple))
    return pl.pallas_call(body, name=name, out_shape=out_shape, in_specs=[HBM] * n_in,
                          out_specs=[HBM] * len(out_shape) if many else HBM, scratch_shapes=scratch)


def _half(ref, which):
    rows = ref.shape[0] // 2
    return ref.at[pl.ds(pl.multiple_of(which * rows, 16), rows)]


def _dma_sems(n):
    return pltpu.SemaphoreType.DMA((n,))


def _allgather_side(ws):
    k = len(ws)

    def peers():
        x, y, c = _place()
        return c, 2 * x + y, (x, y, 1 - c), [(n, t, cx, cy) for n in range(k) for t, (cx, cy) in enumerate(_other_chips(x, y))]

    def outgoing(w_refs, g_refs, sems):
        ssem, rsem, _, _, ossem, orsem = sems
        c, j, sib, nt = peers()
        owns = [_rcopy(w_refs[n], g_refs[n].at[j], ossem.at[n], orsem.at[n], sib) for n in range(k)]
        sends = [_rcopy(_half(w_refs[n], c), _half(g_refs[n].at[j], c), ssem.at[3 * n + t], rsem.at[3 * n + t],
                        (cx, cy, c)) for n, t, cx, cy in nt]
        return owns, sends

    def incoming(g_refs, sems):
        ssem, rsem, fssem, frsem, _, _ = sems
        c, _, sib, nt = peers()
        landed, passed, relayed = [], [], []
        for n, t, cx, cy in nt:
            mine, other = (_half(g_refs[n].at[2 * cx + cy], h) for h in (c, 1 - c))
            landed.append(_rcopy(mine, mine, ssem.at[3 * n + t], rsem.at[3 * n + t], (cx, cy, c)))
            passed.append(_rcopy(mine, mine, fssem.at[3 * n + t], frsem.at[3 * n + t], sib))
            relayed.append(_rcopy(other, other, fssem.at[3 * n + t], frsem.at[3 * n + t], sib))
        return landed, passed, relayed

    def start(w_refs, g_refs, sems):
        owns, sends = outgoing(w_refs, g_refs, sems)
        for cp in sends + owns:
            cp.start()

    def finish(w_refs, g_refs, sems):
        owns, sends = outgoing(w_refs, g_refs, sems)
        landed, passed, relayed = incoming(g_refs, sems)
        for got, on in zip(landed, passed):
            got.wait_recv()
            on.start()
        for cp in relayed:
            cp.wait_recv()
        for cp in owns:
            cp.wait()
        for cp in sends + passed:
            cp.wait_send()

    return _Side(list(ws), [_sds((N_CHIPS,) + w.shape, w.dtype) for w in ws],
                 [_dma_sems(3 * k)] * 4 + [_dma_sems(k)] * 2, start, finish)


def _exchange_side(parts):
    k = len(parts)

    def copies(p_refs, rcv_refs, sems):
        ssem, rsem = sems
        x, y, c = _place()
        return [_rcopy(p_refs[n].at[2 * cx + cy], rcv_refs[n].at[t], ssem.at[3 * n + t], rsem.at[3 * n + t], (cx, cy, c))
                for n in range(k) for t, (cx, cy) in enumerate(_other_chips(x, y))]

    def start(p_refs, rcv_refs, sems):
        for cp in copies(p_refs, rcv_refs, sems):
            cp.start()

    def finish(p_refs, rcv_refs, sems):
        for cp in copies(p_refs, rcv_refs, sems):
            cp.wait()

    return _Side(list(parts), [_sds((3,) + p.shape[1:], p.dtype) for p in parts], [_dma_sems(3 * k)] * 2, start, finish)


def _run_side(side, name):
    k_in, k_out = len(side.arrays), len(side.out_shape)

    def body(*refs):
        parts = refs[:k_in], refs[k_in:k_in + k_out], refs[k_in + k_out:]
        side.start(*parts)
        side.finish(*parts)

    return _comm_call(body, name, list(side.out_shape), k_in, list(side.scratch))(*side.arrays)


def _sibling_side(arrays, out_shape, n_copies, copies):
    def start(in_refs, out_refs, sems):
        for cp in copies(in_refs, out_refs, sems):
            cp.start()

    def finish(in_refs, out_refs, sems):
        for cp in copies(in_refs, out_refs, sems):
            cp.wait()

    return _Side(list(arrays), out_shape, [_dma_sems(n_copies)] * 2, start, finish)


class _ColBlocks:
    def __init__(self, array, off):
        self.array, self.off, self.dtype = array, off, array.dtype
        self.shape = (N_CHIPS, array.shape[0], GU_BLOCK)

    def block(self, ref, jj):
        return ref.at[:, pl.ds((2 * jj + self.off) * GU_BLOCK, GU_BLOCK)]


def _swap_side(gds):
    k = len(gds)

    def copies(gd_refs, out_refs, sems):
        ssem, rsem = sems
        x, y, c = _place()
        blocks = [[g.block(gd_refs[n], jj) if isinstance(g, _ColBlocks) else gd_refs[n].at[jj] for jj in range(N_CHIPS)]
                  for n, g in enumerate(gds)]
        return [_rcopy(_half(blocks[n][jj], 1 - c), out_refs[n].at[jj], ssem.at[N_CHIPS * n + jj],
                       rsem.at[N_CHIPS * n + jj], (x, y, 1 - c)) for n in range(k) for jj in range(N_CHIPS)]

    return _sibling_side([g.array if isinstance(g, _ColBlocks) else g for g in gds],
                         [_sds((N_CHIPS, g.shape[1] // 2, g.shape[2]), g.dtype) for g in gds], N_CHIPS * k, copies)


def _share_side(reds):
    k = len(reds)

    def copies(r_refs, out_refs, sems):
        ssem, rsem = sems
        x, y, c = _place()
        return [_rcopy(r_refs[n], out_refs[n], ssem.at[n], rsem.at[n], (x, y, 1 - c)) for n in range(k)]

    return _sibling_side(reds, [_sds(r.shape, r.dtype) for r in reds], k, copies)


def _join_sides(sides):
    if len(sides) == 1:
        return sides[0]
    cuts = [(len(s.arrays), len(s.out_shape), len(s.scratch)) for s in sides]

    def each(method, in_refs, out_refs, sems):
        a = o = m = 0
        for s, (ka, ko, km) in zip(sides, cuts):
            getattr(s, method)(in_refs[a:a + ka], out_refs[o:o + ko], sems[m:m + km])
            a, o, m = a + ka, o + ko, m + km

    return _Side([x for s in sides for x in s.arrays], [x for s in sides for x in s.out_shape],
                 [x for s in sides for x in s.scratch], functools.partial(each, "start"), functools.partial(each, "finish"))


def _allreduce_small(small):
    def body(s_ref, all_ref, sssem, srsem, lsem):
        x, y, c = _place()
        me = 4 * x + 2 * y + c
        own = pltpu.make_async_copy(s_ref, all_ref.at[me], lsem)
        own.start()
        cps = []
        for r in range(1, 8):
            fx, fy, fc = (r >> 2) & 1, (r >> 1) & 1, r & 1
            px, py, pc = (1 - x if fx else x, 1 - y if fy else y, 1 - c if fc else c)
            peer = 4 * px + 2 * py + pc
            send = _rcopy(s_ref, all_ref.at[me], sssem.at[r - 1], srsem.at[me], (px, py, pc))
            send.start()
            cps.append((send, _rcopy(s_ref, all_ref.at[peer], sssem.at[r - 1], srsem.at[peer], (px, py, pc))))
        for send, recv in cps:
            send.wait_send()
            recv.wait_recv()
        own.wait()

    return _comm_call(body, "allreduce_small", [_sds((8,) + small.shape, small.dtype)], 1,
                      [pltpu.SemaphoreType.DMA((7,)), pltpu.SemaphoreType.DMA((8,)), pltpu.SemaphoreType.DMA(())])(small)[0]


def _add_pair(gd, got, c, name):
    _, R, W = got.shape
    tm = _pick(R, (512, 256, 128, 64))
    nb = R // tm

    def body(c_ref, a_ref, b_ref, o_ref):
        o_ref[...] = (a_ref[...].astype(F32) + b_ref[...].astype(F32)).astype(o_ref.dtype)

    if isinstance(gd, _ColBlocks):
        off = gd.off
        own = pl.BlockSpec((tm, W), lambda j, i, c_ref: (c_ref[0] * nb + i, 2 * j + off))
        gd = gd.array
    else:
        own = pl.BlockSpec((None, tm, W), lambda j, i, c_ref: (j, c_ref[0] * nb + i, 0))
    grid_spec = pltpu.PrefetchScalarGridSpec(
        num_scalar_prefetch=1, grid=(N_CHIPS, nb),
        in_specs=[own, pl.BlockSpec((None, tm, W), lambda j, i, c_ref: (j, i, 0))],
        out_specs=pl.BlockSpec((None, tm, W), lambda j, i, c_ref: (j, i, 0)))
    return pl.pallas_call(body, name=name, grid_spec=grid_spec, out_shape=_sds((N_CHIPS, R, W), gd.dtype),
                          compiler_params=pltpu.CompilerParams(dimension_semantics=("parallel", "parallel"),
                                                               vmem_limit_bytes=VMEM_LIMIT))(c, gd, got)


def _add_chips(part, rcv, j, name):
    _, R, W = part.shape
    tm = _pick(R, (512, 256, 128, 64))

    def body(j_ref, p_ref, r0_ref, r1_ref, r2_ref, o_ref):
        o_ref[...] = ((p_ref[...].astype(F32) + r0_ref[...].astype(F32)) + r1_ref[...].astype(F32)) + r2_ref[...].astype(F32)

    def slot(t):
        return pl.BlockSpec((None, tm, W), lambda i, j_ref: (t, i, 0))

    grid_spec = pltpu.PrefetchScalarGridSpec(
        num_scalar_prefetch=1, grid=(R // tm,),
        in_specs=[pl.BlockSpec((None, tm, W), lambda i, j_ref: (j_ref[0], i, 0)), slot(0), slot(1), slot(2)],
        out_specs=pl.BlockSpec((tm, W), lambda i, j_ref: (i, 0)))
    return pl.pallas_call(body, name=name, grid_spec=grid_spec, out_shape=_sds((R, W), F32),
                          compiler_params=pltpu.CompilerParams(dimension_semantics=("parallel",),
                                                               vmem_limit_bytes=VMEM_LIMIT))(j, part, rcv, rcv, rcv)


def _sum_small(allsmall):
    _, R, W = allsmall.shape

    def body(a_ref, o_ref):
        acc = a_ref[0]
        for d in range(1, 8):
            acc = acc + a_ref[d]
        o_ref[...] = acc

    return _call(body, "sum_small", _sds((R, W), F32), (1,), [_whole((8, R, W))], _whole((R, W)),
                 sem=("arbitrary",))(allsmall)


def _adamw(w, g, m, v, name):
    R, C = w.shape
    tm = _pick(R, (256, 128, 64, 32, 8))

    def body(w_ref, g_ref, m_ref, v_ref, d_ref, mo_ref, vo_ref):
        gv = g_ref[...]
        mn = ADAM_B1 * m_ref[...] + (1.0 - ADAM_B1) * gv
        vn = ADAM_B2 * v_ref[...] + (1.0 - ADAM_B2) * (gv * gv)
        m_hat = mn / (1.0 - ADAM_B1 ** ADAM_STEP)
        v_hat = vn / (1.0 - ADAM_B2 ** ADAM_STEP)
        d_ref[...] = -ADAM_LR * (m_hat / (jnp.sqrt(v_hat) + ADAM_EPS) + ADAM_WD * w_ref[...])
        mo_ref[...] = mn
        vo_ref[...] = vn

    spec = _rows(tm, C)
    return _call(body, name, [_sds((R, C), F32)] * 3, (R // tm,), [spec] * 4, [spec] * 3, sem=("parallel",))(w, g, m, v)


def _adamw_layer(c, w, m, v, mine, other, l, prev, name):
    _, R, C = w.shape
    half = R // 2
    tm = _pick(half, (256, 128, 64))
    nbh = half // tm

    def body(c_ref, w_ref, m_ref, v_ref, a_ref, b_ref, *rest):
        g_ref, d_ref, mo_ref, vo_ref = rest[-4:]
        gv = jnp.where(pl.program_id(0) // nbh == c_ref[0], a_ref[...], b_ref[...])
        mn = ADAM_B1 * m_ref[...] + (1.0 - ADAM_B1) * gv
        vn = ADAM_B2 * v_ref[...] + (1.0 - ADAM_B2) * (gv * gv)
        m_hat = mn / (1.0 - ADAM_B1 ** ADAM_STEP)
        v_hat = vn / (1.0 - ADAM_B2 ** ADAM_STEP)
        g_ref[...] = gv
        d_ref[...] = -ADAM_LR * (m_hat / (jnp.sqrt(v_hat) + ADAM_EPS) + ADAM_WD * w_ref[...])
        mo_ref[...] = mn
        vo_ref[...] = vn

    layer = pl.BlockSpec((None, tm, C), lambda i, c_ref: (l, i, 0))
    halfspec = pl.BlockSpec((tm, C), lambda i, c_ref: (i % nbh, 0))
    n_prev = 0 if prev is None else 4
    grid_spec = pltpu.PrefetchScalarGridSpec(
        num_scalar_prefetch=1, grid=(R // tm,),
        in_specs=[layer] * 3 + [halfspec] * 2 + [pl.BlockSpec(memory_space=pl.ANY)] * n_prev,
        out_specs=[layer] * 4)
    return pl.pallas_call(body, name=name, grid_spec=grid_spec, out_shape=[_sds(w.shape, F32)] * 4,
                          input_output_aliases={6 + k: k for k in range(n_prev)},
                          compiler_params=pltpu.CompilerParams(dimension_semantics=("parallel",),
                                                               vmem_limit_bytes=VMEM_LIMIT))(
        c, w, m, v, mine, other, *(prev or ()))


FIRST_GATHER = ("w_in", "w_uq", "w_ukv")
G_DOWN, G_GU, G_OUT, G_IN = ("w_down",), ("w_gate", "w_up"), ("w_out",), ("w_uq", "w_ukv", "w_in")


def _backward_jobs(l):
    t = f"_l{l}"
    return {"mm_dact" + t: [("swap", l, G_DOWN)], "mm_dw_gu" + t: [("exchange", l, G_DOWN)],
            "mm_dx1" + t: [("swap", l, G_GU), ("share", l, G_DOWN)], "mm_dmixin" + t: [("swap", l, G_OUT)],
            "ret_bwd" + t: [("exchange", l, ("w_gate",))],
            "mla_bwd" + t: [("exchange", l, ("w_up", "w_out")), ("share", l, ("w_gate",))],
            "mm_dw_in" + t: [("share", l, ("w_up", "w_out"))]}


JOBS = {
    "mm_h_l0": [("gather", 0, ("w_up",))], "mla_fwd_l0": [("gather", 0, ("w_gate", "w_out"))],
    "ret_fwd_l0": [("gather", 1, ("w_out",))],
    "mm_gu_l0": [("gather", 0, ("w_down",)), ("gather", 1, ("w_uq", "w_ukv"))],
    "mm_down_l0": [("gather", 1, ("w_in",))], "mm_h_l1": [("gather", 1, ("w_up",))],
    "mla_fwd_l1": [("gather", 1, ("w_gate",))], "mm_gu_l1": [("gather", 1, ("w_down",))],
    **_backward_jobs(1), **_backward_jobs(0),
    "mm_dxl_l1": [("swap", 1, G_IN)], "mm_dw_down_l0": [("exchange", 1, G_IN)],
    "mm_dact_l0": [("swap", 0, G_DOWN), ("share", 1, G_IN)], "mm_dxl_l0": [("exchange", 0, G_IN)]}
PLANNED = {job for jobs in JOBS.values() for job in jobs}


class _Pipeline:
    def __init__(self, own, Wt, Mo, Vo, core, chip):
        self.own, self.Wt, self.Mo, self.Vo, self.core, self.chip = own, Wt, Mo, Vo, core, chip
        self.blocks, self.whole, self.gds, self.parts, self.reds = {}, {}, {}, {}, {}
        self.results = {n: None for n in BIG}

    def gather_first(self):
        job = ("gather", 0, FIRST_GATHER)
        self._done(*job, _run_side(self._side(*job), "allgather_first"))

    def weight(self, l, name):
        if (l, name) not in self.whole:
            self.whole[(l, name)] = _internal_weight(name, *[self.blocks[(l, n)] for n in INTERNAL_OF[name]])
        return self.whole[(l, name)]

    def run(self, fn, name, *args, **kw):
        jobs = JOBS.get(name, ())
        if not jobs:
            return fn(*args, name=name, **kw)
        sides = [self._side(*job) for job in jobs]
        out, res = fn(*args, name=name, side=_join_sides(sides), **kw)
        for job, side in zip(jobs, sides):
            k = len(side.out_shape)
            self._done(*job, res[:k])
            res = res[k:]
        return out

    def reduce(self, l, **grads):
        shards = {}
        for name, g in grads.items():
            shards.update(_grad_shards(name, g))
        for n, sh in shards.items():
            self.gds[(l, n)] = sh if hasattr(sh, "shape") else jnp.stack(sh)
        self._alone("swap", l, tuple(shards))

    def _alone(self, kind, l, names):
        if (kind, l, names) not in PLANNED:
            self._done(kind, l, names, _run_side(self._side(kind, l, names), f"{kind}_{names[0]}_l{l}"))

    def _side(self, kind, l, names):
        if kind == "gather":
            return _allgather_side([self.own[l][n] for n in names])
        store = {"swap": self.gds, "exchange": self.parts, "share": self.reds}[kind]
        make = {"swap": _swap_side, "exchange": _exchange_side, "share": _share_side}[kind]
        return make([store[(l, n)] for n in names])

    def _done(self, kind, l, names, res):
        for n, r in zip(names, res):
            if kind == "gather":
                self.blocks[(l, n)] = r
            elif kind == "swap":
                self.parts[(l, n)] = _add_pair(self.gds[(l, n)], r, self.core, f"add_pair_{n}_l{l}")
            elif kind == "exchange":
                self.reds[(l, n)] = _add_chips(self.parts[(l, n)], r, self.chip, f"add_chips_{n}_l{l}")
            else:
                self.results[n] = _adamw_layer(self.core, self.Wt[n], self.Mo[n], self.Vo[n], self.reds[(l, n)], r, l,
                                               self.results[n], f"adamw_{n}_l{l}")
        if kind == "exchange":
            self._alone("share", l, names)


def kernel(x, positions, ln_in_g, ln_in_b, w_in, q_norm_g, kv_norm_g, w_uq, w_ukv, ret_gn_g, ret_gn_b, w_out, ln1_g, ln1_b, w_gate, w_up, w_down, ln2_g, ln2_b, loss_target, m_ln_in_g, m_ln_in_b, m_w_in, m_q_norm_g, m_kv_norm_g, m_w_uq, m_w_ukv, m_ret_gn_g, m_ret_gn_b, m_w_out, m_ln1_g, m_ln1_b, m_w_gate, m_w_up, m_w_down, m_ln2_g, m_ln2_b, v_ln_in_g, v_ln_in_b, v_w_in, v_q_norm_g, v_kv_norm_g, v_w_uq, v_w_ukv, v_ret_gn_g, v_ret_gn_b, v_w_out, v_ln1_g, v_ln1_b, v_w_gate, v_w_up, v_w_down, v_ln2_g, v_ln2_b):
    given = dict(locals())
    Wt = {n: given[n] for n in WEIGHTS}
    Mo = {n: given["m_" + n] for n in WEIGHTS}
    Vo = {n: given["v_" + n] for n in WEIGHTS}
    cx, cy, cc = _place()
    chip = (2 * cx + cy).astype(jnp.int32)
    core = cc.astype(jnp.int32)

    own = [{n: Wt[n][l].astype(BF16) for n in BIG} for l in range(DEPTH)]
    pipe = _Pipeline(own, Wt, Mo, Vo, core.reshape(1), chip.reshape(1))
    sqerr, grad_x, dP = _local_step(x[0], positions[0], loss_target[0], pipe, Wt)
    results = pipe.results

    small_g = {n: (dP[(n, None)] if Wt[n].ndim == 1 else jnp.stack([dP[(n, l)] for l in range(DEPTH)])) for n in SMALL}
    local_loss = 0.5 * jnp.sum(sqerr) / D_MODEL
    small_sum = _sum_small(_allreduce_small(_flatten_small(small_g, local_loss))).reshape(-1)
    layout, n_small = _small_layout(Wt)
    loss = small_sum[n_small]

    grads, deltas, new_m, new_v = {}, {}, {}, {}
    for n in BIG:
        grads[n], deltas[n], new_m[n], new_v[n] = results[n]
    zero = jnp.zeros((), F32)
    d, mn, vn = _adamw(_flatten_small(Wt, zero), small_sum.reshape(SMALL_ROWS, FLAT_W), _flatten_small(Mo, zero),
                       _flatten_small(Vo, zero), "adamw_small")
    for n in SMALL:
        at, size = layout[n]
        pick = lambda a: a.reshape(-1)[at:at + size].reshape(Wt[n].shape)
        grads[n], deltas[n], new_m[n], new_v[n] = pick(small_sum), pick(d), pick(mn), pick(vn)

    return (loss, grad_x[None], *[grads[n] for n in WEIGHTS], *[deltas[n] for n in WEIGHTS],
            *[new_m[n] for n in WEIGHTS], *[new_v[n] for n in WEIGHTS])
```

```python
import functools

import jax
import jax.numpy as jnp
from jax import lax
from jax.experimental import pallas as pl
from jax.experimental.pallas import tpu as pltpu

F32 = jnp.float32
BF16 = jnp.bfloat16

D_MODEL = 2048
DEPTH = 2
CHUNK = 64
MLA_HEADS = 8
Q_LORA = 512
KV_LORA = 256
NOPE = 128
ROPE = 64
VDIM = 128
RET_HEADS = 4
RET_DK = 256
RET_DV = 256
D_FF = 5632
D_IN = 4928
ROPE_THETA = 10000.0
LN_EPS = 1e-5
RMS_EPS = 1e-6
GN_EPS = 1e-5
ALPHA = (2 * DEPTH) ** 0.25
MLA_SCALE = (NOPE + ROPE) ** -0.5
RET_SCALE = RET_DK ** -0.5
ADAM_LR = 0.001
ADAM_B1 = 0.9
ADAM_B2 = 0.999
ADAM_EPS = 1e-08
ADAM_WD = 0.01
ADAM_STEP = 10

LANES = 128
HEAD_PAD = 256
MLA_IN = 1024
MLA_IN_USED = Q_LORA + KV_LORA + ROPE
D_IN_PAD = MLA_IN + 4 * 1024
ATT_BLOCK = 512
NEG = -1e30
VMEM_LIMIT = 56 * 1024 * 1024

N_CHIPS = 4
FLAT_W = 1024
BIG = ("w_in", "w_uq", "w_ukv", "w_out", "w_gate", "w_up", "w_down")
BIG_SHARD = {"w_in": (2048, 1232), "w_uq": (512, 384), "w_ukv": (256, 512), "w_out": (512, 2048),
             "w_gate": (2048, 1408), "w_up": (2048, 1408), "w_down": (1408, 2048)}
SMALL = ("ln_in_g", "ln_in_b", "q_norm_g", "kv_norm_g", "ret_gn_g", "ret_gn_b", "ln1_g", "ln1_b", "ln2_g", "ln2_b")
WEIGHTS = ("ln_in_g", "ln_in_b", "w_in", "q_norm_g", "kv_norm_g", "w_uq", "w_ukv", "ret_gn_g", "ret_gn_b", "w_out",
           "ln1_g", "ln1_b", "w_gate", "w_up", "w_down", "ln2_g", "ln2_b")
SMALL_ROWS = 32

MESH = pl.DeviceIdType.MESH


def _pick(dim, cands):
    for c in cands:
        if dim % c == 0:
            return c
    return dim


HBM = pl.BlockSpec(memory_space=pltpu.HBM)


class _Side:
    def __init__(self, arrays, out_shape, scratch, start, finish):
        self.arrays, self.out_shape, self.scratch, self.start, self.finish = arrays, out_shape, scratch, start, finish


def _call(body, name, out_shape, grid, in_specs, out_specs, scratch=(), sem=None, side=None):
    params = pltpu.CompilerParams(dimension_semantics=sem if side is None else ("arbitrary",) * len(grid),
                                  vmem_limit_bytes=VMEM_LIMIT)
    if side is None:
        return pl.pallas_call(body, name=name, out_shape=out_shape, grid=grid, in_specs=in_specs, out_specs=out_specs,
                              scratch_shapes=list(scratch), compiler_params=params)
    single = not isinstance(out_shape, (list, tuple))
    outs = [out_shape] if single else list(out_shape)
    ospecs = [out_specs] if single else list(out_specs)
    cuts = [len(in_specs), len(side.arrays), len(outs), len(side.out_shape), len(scratch)]
    ends = [sum(cuts[:k + 1]) for k in range(len(cuts))]

    def hosted(*refs):
        ins, s_in, o, s_out, scr = (refs[a:b] for a, b in zip([0] + ends[:-1], ends))
        sems = refs[ends[-1]:]
        ids = [pl.program_id(a) for a in range(len(grid))]
        first = functools.reduce(jnp.logical_and, [i == 0 for i in ids])
        last = functools.reduce(jnp.logical_and, [i == g - 1 for i, g in zip(ids, grid)])

        @pl.when(first)
        def _():
            side.start(s_in, s_out, sems)

        body(*ins, *o, *scr)

        @pl.when(last)
        def _():
            side.finish(s_in, s_out, sems)

    call = pl.pallas_call(hosted, name=name, out_shape=outs + list(side.out_shape), grid=grid,
                          in_specs=list(in_specs) + [HBM] * len(side.arrays),
                          out_specs=ospecs + [HBM] * len(side.out_shape),
                          scratch_shapes=list(scratch) + list(side.scratch), compiler_params=params)

    def run(*args):
        res = call(*args, *side.arrays)
        return (res[0] if single else list(res[:len(outs)])), list(res[len(outs):])

    return run


def _rows(tm, w, col=0):
    return pl.BlockSpec((tm, w), lambda i: (i, col))


def _whole(shape):
    return pl.BlockSpec(shape, lambda i: (0,) * len(shape))


def _sds(shape, dtype):
    return jax.ShapeDtypeStruct(shape, dtype)


def _matmul(a, b, name, ta=False, tb=False, out_dtype=F32, side=None):
    (K, M) = a.shape if ta else a.shape[::-1]
    (N, Kb) = b.shape if tb else b.shape[::-1]
    assert K == Kb, (a.shape, b.shape, ta, tb)
    tm = _pick(M, (1024, 1408, 512, 256, 128))
    tn = _pick(N, (1024, 512, 256, 128))
    tk = _pick(K, (2816, 2560, 2048, 1024, 512, 256))
    nk = K // tk
    dn = (((0 if ta else 1,), (1 if tb else 0,)), ((), ()))

    def body(a_ref, b_ref, o_ref, acc_ref):
        k = pl.program_id(2)
        if nk == 1:
            o_ref[...] = lax.dot_general(a_ref[...].astype(BF16), b_ref[...].astype(BF16), dn,
                                         preferred_element_type=F32).astype(out_dtype)
        else:
            @pl.when(k == 0)
            def _():
                acc_ref[...] = jnp.zeros_like(acc_ref)

            acc_ref[...] += lax.dot_general(a_ref[...].astype(BF16), b_ref[...].astype(BF16), dn,
                                            preferred_element_type=F32)

            @pl.when(k == nk - 1)
            def _():
                o_ref[...] = acc_ref[...].astype(out_dtype)

    a_spec = pl.BlockSpec((tk, tm), lambda i, j, k: (k, i)) if ta else pl.BlockSpec((tm, tk), lambda i, j, k: (i, k))
    b_spec = pl.BlockSpec((tn, tk), lambda i, j, k: (j, k)) if tb else pl.BlockSpec((tk, tn), lambda i, j, k: (k, j))
    return _call(body, name, _sds((M, N), out_dtype), (M // tm, N // tn, nk), [a_spec, b_spec],
                 pl.BlockSpec((tm, tn), lambda i, j, k: (i, j)), scratch=[pltpu.VMEM((tm, tn), F32)],
                 sem=("parallel", "parallel", "arbitrary"), side=side)(a, b)


def _sigmoid(x):
    return 1.0 / (1.0 + jnp.exp(-x))


def _rope_group(r, c, sa, sb):
    return r * c + pltpu.roll(r, 32, 1) * sa + pltpu.roll(r, 96, 1) * sb


def _ln_fwd(xs, coefs, g, b, name, want_z):
    S, D = xs[0].shape
    tm = 256
    n = len(xs)

    def body(*refs):
        x_refs, g_ref, b_ref, outs = refs[:n], refs[n], refs[n + 1], refs[n + 2:]
        z = None
        for cf, r in zip(coefs, x_refs):
            t = r[...] if cf == 1.0 else cf * r[...]
            z = t if z is None else z + t
        mu = jnp.mean(z, axis=-1, keepdims=True)
        zc = z - mu
        var = jnp.mean(zc * zc, axis=-1, keepdims=True)
        y = zc * lax.rsqrt(var + LN_EPS) * g_ref[...] + b_ref[...]
        if want_z:
            outs[0][...] = z
        outs[-2][...] = y
        outs[-1][...] = y.astype(BF16)

    out_shape = [_sds((S, D), F32)] * (2 if want_z else 1) + [_sds((S, D), BF16)]
    return _call(body, name, out_shape, (S // tm,), [_rows(tm, D)] * n + [_whole((1, D))] * 2,
                 [_rows(tm, D)] * len(out_shape), sem=("parallel",))(*xs, g, b)


def _ln_bwd(dys, coefs, z, g, name):
    S, D = z.shape
    tm = 256
    n = len(dys)

    def body(*refs):
        dy_refs, z_ref, g_ref = refs[:n], refs[n], refs[n + 1]
        dz_ref, dzb_ref, dg_ref, db_ref = refs[n + 2:]
        dy = None
        for cf, r in zip(coefs, dy_refs):
            t = r[...] if cf == 1.0 else cf * r[...]
            dy = t if dy is None else dy + t
        zv = z_ref[...]
        mu = jnp.mean(zv, axis=-1, keepdims=True)
        zc = zv - mu
        var = jnp.mean(zc * zc, axis=-1, keepdims=True)
        rstd = lax.rsqrt(var + LN_EPS)
        xh = zc * rstd
        dyg = dy * g_ref[...]
        dz = rstd * (dyg - jnp.mean(dyg, axis=-1, keepdims=True) - xh * jnp.mean(dyg * xh, axis=-1, keepdims=True))
        dz_ref[...] = dz
        dzb_ref[...] = dz.astype(BF16)

        @pl.when(pl.program_id(0) == 0)
        def _():
            dg_ref[...] = jnp.zeros_like(dg_ref)
            db_ref[...] = jnp.zeros_like(db_ref)

        dg_ref[...] += jnp.sum(dy * xh, axis=0, keepdims=True)
        db_ref[...] += jnp.sum(dy, axis=0, keepdims=True)

    return _call(body, name, [_sds((S, D), F32), _sds((S, D), BF16), _sds((1, D), F32), _sds((1, D), F32)],
                 (S // tm,), [_rows(tm, D)] * (n + 1) + [_whole((1, D))],
                 [_rows(tm, D), _rows(tm, D), _whole((1, D)), _whole((1, D))], sem=("arbitrary",))(*dys, z, g)


def _rms(x, g):
    return x * lax.rsqrt(jnp.mean(x * x, axis=-1, keepdims=True) + RMS_EPS) * g


def _prep1(h, tabs, qg, kvg, name):
    S = h.shape[0]
    tm = 256
    cm, sam, sbm, cr, sr = tabs

    def body(h_ref, cm_ref, sam_ref, sbm_ref, cr_ref, sr_ref, qg_ref, kvg_ref,
             qn_ref, kvn_ref, kr_ref, rq_ref, rk_ref, rv_ref):
        qn_ref[...] = _rms(h_ref[:, 0:Q_LORA], qg_ref[...]).astype(BF16)
        kvn_ref[...] = _rms(h_ref[:, Q_LORA:Q_LORA + KV_LORA], kvg_ref[...]).astype(BF16)
        kr_ref[...] = _rope_group(h_ref[:, 768:896], cm_ref[...], sam_ref[...], sbm_ref[...])
        c, s = cr_ref[...], sr_ref[...]
        for hd in range(RET_HEADS):
            for src, dst, scale in ((MLA_IN, rq_ref, RET_SCALE), (MLA_IN + 1024, rk_ref, None)):
                t1 = h_ref[:, src + hd * 256:src + hd * 256 + 128]
                t2 = h_ref[:, src + hd * 256 + 128:src + hd * 256 + 256]
                o1, o2 = t1 * c - t2 * s, t2 * c + t1 * s
                if scale is not None:
                    o1, o2 = o1 * scale, o2 * scale
                dst[:, hd * 256:hd * 256 + 128] = o1.astype(BF16)
                dst[:, hd * 256 + 128:hd * 256 + 256] = o2.astype(BF16)
        rv_ref[...] = h_ref[:, MLA_IN + 2048:MLA_IN + 3072].astype(BF16)

    t128 = _rows(tm, LANES)
    return _call(body, name,
                 [_sds((S, Q_LORA), BF16), _sds((S, KV_LORA), BF16), _sds((S, LANES), F32),
                  _sds((S, 1024), BF16), _sds((S, 1024), BF16), _sds((S, 1024), BF16)],
                 (S // tm,),
                 [_rows(tm, D_IN_PAD), t128, t128, t128, t128, t128, _whole((1, Q_LORA)), _whole((1, KV_LORA))],
                 [_rows(tm, Q_LORA), _rows(tm, KV_LORA), t128, _rows(tm, 1024), _rows(tm, 1024), _rows(tm, 1024)],
                 sem=("parallel",))(h, cm, sam, sbm, cr, sr, qg, kvg)


def _prep1_bwd(dqn, dkvn, dkr, drq, drk, drv, drg, h, tabs, qg, kvg, name):
    S = h.shape[0]
    tm = 256
    cm, sam, sbm, cr, sr = tabs

    def rms_bwd(x, g, dy):
        r = lax.rsqrt(jnp.mean(x * x, axis=-1, keepdims=True) + RMS_EPS)
        dyg = dy * g
        dx = r * dyg - x * (r * r * r) * jnp.mean(dyg * x, axis=-1, keepdims=True)
        return dx, jnp.sum(dy * x * r, axis=0, keepdims=True)

    def body(dqn_ref, dkvn_ref, dkr_ref, drq_ref, drk_ref, drv_ref, drg_ref, h_ref,
             cm_ref, sam_ref, sbm_ref, cr_ref, sr_ref, qg_ref, kvg_ref, dh_ref, dqg_ref, dkvg_ref):
        dcq, dqg = rms_bwd(h_ref[:, 0:Q_LORA], qg_ref[...], dqn_ref[...])
        dckv, dkvg = rms_bwd(h_ref[:, Q_LORA:Q_LORA + KV_LORA], kvg_ref[...], dkvn_ref[...])
        dh_ref[:, 0:Q_LORA] = dcq.astype(BF16)
        dh_ref[:, Q_LORA:Q_LORA + KV_LORA] = dckv.astype(BF16)
        dh_ref[:, 768:896] = _rope_group(dkr_ref[...], cm_ref[...], -sam_ref[...], -sbm_ref[...]).astype(BF16)
        dh_ref[:, 896:1024] = jnp.zeros((tm, LANES), BF16)
        c, s = cr_ref[...], sr_ref[...]
        for hd in range(RET_HEADS):
            for src, dst, scale in ((drq_ref, MLA_IN, RET_SCALE), (drk_ref, MLA_IN + 1024, None)):
                d1 = src[:, hd * 256:hd * 256 + 128]
                d2 = src[:, hd * 256 + 128:hd * 256 + 256]
                if scale is not None:
                    d1, d2 = d1 * scale, d2 * scale
                dh_ref[:, dst + hd * 256:dst + hd * 256 + 128] = (d1 * c + d2 * s).astype(BF16)
                dh_ref[:, dst + hd * 256 + 128:dst + hd * 256 + 256] = (d2 * c - d1 * s).astype(BF16)
        dh_ref[:, MLA_IN + 2048:MLA_IN + 3072] = drv_ref[...].astype(BF16)
        dh_ref[:, MLA_IN + 3072:MLA_IN + 4096] = drg_ref[...].astype(BF16)

        @pl.when(pl.program_id(0) == 0)
        def _():
            dqg_ref[...] = jnp.zeros_like(dqg_ref)
            dkvg_ref[...] = jnp.zeros_like(dkvg_ref)

        dqg_ref[...] += dqg
        dkvg_ref[...] += dkvg

    t128 = _rows(tm, LANES)
    return _call(body, name,
                 [_sds((S, D_IN_PAD), BF16), _sds((1, Q_LORA), F32), _sds((1, KV_LORA), F32)],
                 (S // tm,),
                 [_rows(tm, Q_LORA), _rows(tm, KV_LORA), t128, _rows(tm, 1024), _rows(tm, 1024), _rows(tm, 1024),
                  _rows(tm, 1024), _rows(tm, MLA_IN), t128, t128, t128, t128, t128,
                  _whole((1, Q_LORA)), _whole((1, KV_LORA))],
                 [_rows(tm, D_IN_PAD), _whole((1, Q_LORA)), _whole((1, KV_LORA))],
                 sem=("arbitrary",))(dqn, dkvn, dkr, drq, drk, drv, drg, h, cm, sam, sbm, cr, sr, qg, kvg)


def _prep2(q, kv, kr, tabs, name):
    S = q.shape[0]
    tm = 256
    cm, sam, sbm = tabs[:3]

    def body(q_ref, kv_ref, kr_ref, cm_ref, sam_ref, sbm_ref, qo_ref, ko_ref, vo_ref):
        c, sa, sb = cm_ref[...], sam_ref[...], sbm_ref[...]
        krb = kr_ref[...].astype(BF16)
        ones = jnp.ones((tm, LANES), BF16)
        for hd in range(MLA_HEADS):
            o = hd * HEAD_PAD
            qo_ref[:, o:o + 128] = (q_ref[:, o:o + 128] * MLA_SCALE).astype(BF16)
            qo_ref[:, o + 128:o + 256] = (_rope_group(q_ref[:, o + 128:o + 256], c, sa, sb) * MLA_SCALE).astype(BF16)
            ko_ref[:, o:o + 128] = kv_ref[:, hd * 128:hd * 128 + 128].astype(BF16)
            ko_ref[:, o + 128:o + 256] = krb
            vo_ref[:, o:o + 128] = kv_ref[:, 1024 + hd * 128:1024 + hd * 128 + 128].astype(BF16)
            vo_ref[:, o + 128:o + 256] = ones

    t128 = _rows(tm, LANES)
    return _call(body, name, [_sds((S, 2048), BF16)] * 3, (S // tm,),
                 [_rows(tm, 2048), _rows(tm, 2048), t128, t128, t128, t128],
                 [_rows(tm, 2048)] * 3, sem=("parallel",))(q, kv, kr, cm, sam, sbm)


def _prep2_bwd(dqm, dkm, dvm, tabs, name):
    S = dqm.shape[0]
    tm = 256
    cm, sam, sbm = tabs[:3]

    def body(dq_ref, dk_ref, dv_ref, cm_ref, sam_ref, sbm_ref, dqo_ref, dkvo_ref, dkr_ref):
        c, sa, sb = cm_ref[...], -sam_ref[...], -sbm_ref[...]
        dkr = None
        for hd in range(MLA_HEADS):
            o = hd * HEAD_PAD
            dqo_ref[:, o:o + 128] = (dq_ref[:, o:o + 128] * MLA_SCALE).astype(BF16)
            dqo_ref[:, o + 128:o + 256] = (_rope_group(dq_ref[:, o + 128:o + 256], c, sa, sb) * MLA_SCALE).astype(BF16)
            dkvo_ref[:, hd * 128:hd * 128 + 128] = dk_ref[:, o:o + 128].astype(BF16)
            t = dk_ref[:, o + 128:o + 256]
            dkr = t if dkr is None else dkr + t
        dkvo_ref[:, 1024:2048] = dv_ref[...].astype(BF16)
        dkr_ref[...] = dkr

    t128 = _rows(tm, LANES)
    return _call(body, name, [_sds((S, 2048), BF16), _sds((S, 2048), BF16), _sds((S, LANES), F32)], (S // tm,),
                 [_rows(tm, 2048), _rows(tm, 2048), _rows(tm, 1024), t128, t128, t128],
                 [_rows(tm, 2048), _rows(tm, 2048), t128], sem=("parallel",))(dqm, dkm, dvm, cm, sam, sbm)


def _gn_gate(a, o, h, gg, gb, name):
    S = a.shape[0]
    tm = 256

    def body(a_ref, o_ref, rg_ref, gg_ref, gb_ref, mix_ref):
        mix_ref[:, 0:1024] = a_ref[...].astype(BF16)
        for hd in range(RET_HEADS):
            sl = slice(hd * 256, hd * 256 + 256)
            ov = o_ref[:, sl]
            mu = jnp.mean(ov, axis=-1, keepdims=True)
            oc = ov - mu
            var = jnp.mean(oc * oc, axis=-1, keepdims=True)
            y = oc * lax.rsqrt(var + GN_EPS) * gg_ref[:, sl] + gb_ref[:, sl]
            rg = rg_ref[:, sl]
            mix_ref[:, 1024 + hd * 256:1024 + hd * 256 + 256] = (rg * _sigmoid(rg) * y).astype(BF16)

    return _call(body, name, _sds((S, 2048), BF16), (S // tm,),
                 [_rows(tm, 1024), _rows(tm, 1024), _rows(tm, 1024, 4), _whole((1, 1024)), _whole((1, 1024))],
                 _rows(tm, 2048), sem=("parallel",))(a, o, h, gg, gb)


def _gn_gate_bwd(dmixin, o, h, gg, gb, name):
    S = o.shape[0]
    tm = 256

    def body(dr_ref, o_ref, rg_ref, gg_ref, gb_ref, do_ref, drg_ref, dgg_ref, dgb_ref):
        @pl.when(pl.program_id(0) == 0)
        def _():
            dgg_ref[...] = jnp.zeros_like(dgg_ref)
            dgb_ref[...] = jnp.zeros_like(dgb_ref)

        for hd in range(RET_HEADS):
            sl = slice(hd * 256, hd * 256 + 256)
            ov = o_ref[:, sl]
            mu = jnp.mean(ov, axis=-1, keepdims=True)
            oc = ov - mu
            var = jnp.mean(oc * oc, axis=-1, keepdims=True)
            rstd = lax.rsqrt(var + GN_EPS)
            xh = oc * rstd
            g = gg_ref[:, sl]
            y = xh * g + gb_ref[:, sl]
            rg = rg_ref[:, sl]
            sg = _sigmoid(rg)
            dr = dr_ref[:, sl]
            dy = dr * (rg * sg)
            drg_ref[:, sl] = dr * y * (sg * (1.0 + rg * (1.0 - sg)))
            dgg_ref[:, sl] += jnp.sum(dy * xh, axis=0, keepdims=True)
            dgb_ref[:, sl] += jnp.sum(dy, axis=0, keepdims=True)
            dxh = dy * g
            do = rstd * (dxh - jnp.mean(dxh, axis=-1, keepdims=True) - xh * jnp.mean(dxh * xh, axis=-1, keepdims=True))
            do_ref[:, sl] = do.astype(BF16)

    return _call(body, name,
                 [_sds((S, 1024), BF16), _sds((S, 1024), F32), _sds((1, 1024), F32), _sds((1, 1024), F32)],
                 (S // tm,),
                 [_rows(tm, 1024, 1), _rows(tm, 1024), _rows(tm, 1024, 4), _whole((1, 1024)), _whole((1, 1024))],
                 [_rows(tm, 1024), _rows(tm, 1024), _whole((1, 1024)), _whole((1, 1024))],
                 sem=("arbitrary",))(dmixin, o, h, gg, gb)


GU_BLOCK = D_FF // N_CHIPS


def _matmul_swiglu(x, w_gu, name, side=None):
    S, K = x.shape
    tm = _pick(S, (512, 256, 128))
    tn = 2 * GU_BLOCK

    def body(x_ref, w_ref, gu_ref, act_ref):
        r = jnp.dot(x_ref[...], w_ref[...], preferred_element_type=F32)
        g, u = r[:, :GU_BLOCK], r[:, GU_BLOCK:]
        gu_ref[...] = r.astype(BF16)
        act_ref[...] = (g * _sigmoid(g) * u).astype(BF16)

    return _call(body, name, [_sds((S, 2 * D_FF), BF16), _sds((S, D_FF), BF16)], (S // tm, N_CHIPS),
                 [pl.BlockSpec((tm, K), lambda i, j: (i, 0)), pl.BlockSpec((K, tn), lambda i, j: (0, j))],
                 [pl.BlockSpec((tm, tn), lambda i, j: (i, j)), pl.BlockSpec((tm, GU_BLOCK), lambda i, j: (i, j))],
                 sem=("parallel", "parallel"), side=side)(x, w_gu)


def _swiglu_bwd(gu, dact, name):
    S = gu.shape[0]
    tm = 128

    def body(gu_ref, d_ref, o_ref):
        for j in range(N_CHIPS):
            at = 2 * GU_BLOCK * j
            g = gu_ref[:, at:at + GU_BLOCK].astype(F32)
            u = gu_ref[:, at + GU_BLOCK:at + 2 * GU_BLOCK].astype(F32)
            d = d_ref[:, GU_BLOCK * j:GU_BLOCK * (j + 1)]
            sg = _sigmoid(g)
            o_ref[:, at:at + GU_BLOCK] = (d * u * (sg * (1.0 + g * (1.0 - sg)))).astype(BF16)
            o_ref[:, at + GU_BLOCK:at + 2 * GU_BLOCK] = (d * (g * sg)).astype(BF16)

    return _call(body, name, _sds((S, 2 * D_FF), BF16), (S // tm,), [_rows(tm, 2 * D_FF), _rows(tm, D_FF)],
                 _rows(tm, 2 * D_FF), sem=("parallel",))(gu, dact)


def _loss_head(y, target, name):
    S, D = y.shape
    tm = 256

    def body(y_ref, t_ref, dy_ref, acc_ref):
        e = y_ref[...] - t_ref[...]
        dy_ref[...] = e / D

        @pl.when(pl.program_id(0) == 0)
        def _():
            acc_ref[...] = jnp.zeros_like(acc_ref)

        acc_ref[...] += jnp.sum(e * e, axis=0, keepdims=True)

    return _call(body, name, [_sds((S, D), F32), _sds((1, D), F32)], (S // tm,), [_rows(tm, D), _rows(tm, D)],
                 [_rows(tm, D), _whole((1, D))], sem=("arbitrary",))(y, target)


def _chunk_mask(T):
    r = lax.shift_right_logical(lax.broadcasted_iota(jnp.int32, (T, T), 0), 6)
    c = lax.shift_right_logical(lax.broadcasted_iota(jnp.int32, (T, T), 1), 6)
    return r >= c


def _dot_nt(a, b):
    return lax.dot_general(a, b, (((1,), (1,)), ((), ())), preferred_element_type=F32)


def _dot_tn(a, b):
    return lax.dot_general(a, b, (((0,), (0,)), ((), ())), preferred_element_type=F32)


def _decay_tables(T):
    lg = jnp.log1p(-jnp.exp2(-5.0 - jnp.arange(RET_HEADS, dtype=F32)))
    idx = jnp.arange(T, dtype=F32)
    diff = idx[:, None] - idx[None, :]
    rel = jnp.exp(lg[:, None, None] * diff[None])
    cid = jnp.arange(T) // CHUNK
    mask = (cid[:, None] >= cid[None, :]).astype(F32)
    reld = jnp.exp(lg[:, None, None] * jnp.abs(diff)[None]) * mask[None]
    lgrow = jnp.broadcast_to(lg[:, None, None], (RET_HEADS, 1, LANES))
    return lgrow, rel, reld


def _attn_fwd(q, k, v, heads, dk, dv, softmax, name, tables=None, side=None):
    S = q.shape[0]
    T = ATT_BLOCK
    nq = S // T
    rep = T // LANES
    vw = 2 * dv if softmax else dv
    assert not softmax or dv == LANES

    def body(*refs):
        if softmax:
            q_ref, k_ref, v_ref, o_ref, lse_ref, m_sc, acc_sc = refs
        else:
            q_ref, k_ref, v_ref, lg_ref, rel_ref, reld_ref, o_ref, acc_sc = refs
        i = pl.program_id(1)
        qv = q_ref[...]

        def kv_block(j):
            rows = pl.ds(pl.multiple_of(j * T, T), T)
            return k_ref[rows, :], v_ref[rows, :]

        kb, vb = kv_block(i)
        s = _dot_nt(qv, kb)
        if softmax:
            s = jnp.where(_chunk_mask(T), s, NEG)
            m = jnp.max(s, axis=-1, keepdims=True)
            p = jnp.exp(s - m)
            m_sc[...] = jnp.broadcast_to(m, (T, LANES))
        else:
            p = s * reld_ref[0]
        acc_sc[...] = jnp.dot(p.astype(BF16), vb, preferred_element_type=F32)

        def scores(j):
            kb, vb = kv_block(j)
            return _dot_nt(qv, kb), vb

        def update(j, s, vb):
            if softmax:
                m_prev = m_sc[...]
                m_next = jnp.maximum(m_prev, jnp.max(s, axis=-1, keepdims=True))
                alpha = jnp.exp(m_prev - m_next)
                p = jnp.exp(s - jnp.tile(m_next, (1, rep)))
                m_sc[...] = m_next
                acc_sc[...] = acc_sc[...] * jnp.tile(alpha, (1, vw // LANES)) + jnp.dot(
                    p.astype(BF16), vb, preferred_element_type=F32)
            else:
                fac = jnp.exp(lg_ref[0] * ((i - j) * T).astype(F32))
                p = s * (rel_ref[0] * jnp.tile(fac, (1, rep)))
                acc_sc[...] += jnp.dot(p.astype(BF16), vb, preferred_element_type=F32)

        def pair(jj, carry):
            first, second = scores(2 * jj), scores(2 * jj + 1)
            update(2 * jj, *first)
            update(2 * jj + 1, *second)
            return carry

        lax.fori_loop(0, i // 2, pair, 0)

        @pl.when(i % 2 == 1)
        def _():
            update(i - 1, *scores(i - 1))

        if softmax:
            l = acc_sc[:, dv:]
            o_ref[...] = acc_sc[:, :dv] / l
            lse_ref[...] = m_sc[...] + jnp.log(l)
        else:
            o_ref[...] = acc_sc[...]

    in_specs = [pl.BlockSpec((T, dk), lambda h, i: (i, h)), pl.BlockSpec((S, dk), lambda h, i: (0, h)),
                pl.BlockSpec((S, vw), lambda h, i: (0, h))]
    o_spec = pl.BlockSpec((T, dv), lambda h, i: (i, h))
    if softmax:
        return _call(body, name, [_sds((S, heads * dv), F32), _sds((S, heads * LANES), F32)], (heads, nq), in_specs,
                     [o_spec, pl.BlockSpec((T, LANES), lambda h, i: (i, h))],
                     scratch=[pltpu.VMEM((T, LANES), F32), pltpu.VMEM((T, vw), F32)],
                     sem=("parallel", "arbitrary"), side=side)(q, k, v)
    lgrow, rel, reld = tables
    in_specs += [pl.BlockSpec((1, 1, LANES), lambda h, i: (h, 0, 0)), pl.BlockSpec((1, T, T), lambda h, i: (h, 0, 0)),
                 pl.BlockSpec((1, T, T), lambda h, i: (h, 0, 0))]
    return _call(body, name, _sds((S, heads * dv), F32), (heads, nq), in_specs, o_spec,
                 scratch=[pltpu.VMEM((T, dv), F32)], sem=("parallel", "arbitrary"), side=side)(q, k, v, lgrow, rel, reld)


def _attn_bwd(q, k, v, do, heads, dk, dv, softmax, name, o=None, lse=None, tables=None, side=None):
    S = q.shape[0]
    T = ATT_BLOCK
    nq = S // T
    rep = T // LANES

    def body(*refs):
        if softmax:
            q_ref, k_ref, v_ref, do_ref, o_ref, lse_ref, dq_ref, dk_ref, dv_ref, dq_sc = refs
        else:
            q_ref, k_ref, v_ref, do_ref, lg_ref, rel_ref, reld_ref, dq_ref, dk_ref, dv_ref, dq_sc = refs
        i = pl.program_id(1)

        @pl.when(i == 0)
        def _():
            dk_ref[...] = jnp.zeros_like(dk_ref)
            dv_ref[...] = jnp.zeros_like(dv_ref)

        qv = q_ref[...]
        dof = do_ref[...].astype(F32)
        dov = dof.astype(BF16)
        if softmax:
            delta = jnp.sum(dof * o_ref[...], axis=-1, keepdims=True)
            lse_t = jnp.tile(lse_ref[...], (1, rep))
        dq_sc[...] = jnp.zeros_like(dq_sc)

        def products(j):
            rows = pl.ds(pl.multiple_of(j * T, T), T)
            kb = k_ref[rows, :]
            return rows, kb, _dot_nt(qv, kb), _dot_nt(dov, v_ref[rows, :])

        def block(j, diagonal, rows, kb, s, dp):
            if softmax:
                if diagonal:
                    s = jnp.where(_chunk_mask(T), s, NEG)
                p = jnp.exp(s - lse_t)
                ds = p * (dp - delta)
            else:
                if diagonal:
                    dec = reld_ref[0]
                else:
                    fac = jnp.exp(lg_ref[0] * ((i - j) * T).astype(F32))
                    dec = rel_ref[0] * jnp.tile(fac, (1, rep))
                p = s * dec
                ds = dp * dec
            dsb = ds.astype(BF16)
            dv_ref[rows, :] += _dot_tn(p.astype(BF16), dov)
            dk_ref[rows, :] += _dot_tn(dsb, qv)
            dq_sc[...] += jnp.dot(dsb, kb, preferred_element_type=F32)

        block(i, True, *products(i))

        def pair(jj, carry):
            first, second = products(2 * jj), products(2 * jj + 1)
            block(2 * jj, False, *first)
            block(2 * jj + 1, False, *second)
            return carry

        lax.fori_loop(0, i // 2, pair, 0)

        @pl.when(i % 2 == 1)
        def _():
            block(i - 1, False, *products(i - 1))

        dq_ref[...] = dq_sc[...]

    qspec = pl.BlockSpec((T, dk), lambda h, i: (i, h))
    kspec = pl.BlockSpec((S, dk), lambda h, i: (0, h))
    vspec = pl.BlockSpec((S, dv), lambda h, i: (0, h))
    dospec = pl.BlockSpec((T, dv), lambda h, i: (i, h))
    in_specs = [qspec, kspec, vspec, dospec]
    args = [q, k, v, do]
    if softmax:
        in_specs[2] = pl.BlockSpec((S, dv), lambda h, i: (0, 2 * h))
        in_specs += [dospec, pl.BlockSpec((T, LANES), lambda h, i: (i, h))]
        args += [o, lse]
    else:
        in_specs += [pl.BlockSpec((1, 1, LANES), lambda h, i: (h, 0, 0)),
                     pl.BlockSpec((1, T, T), lambda h, i: (h, 0, 0)), pl.BlockSpec((1, T, T), lambda h, i: (h, 0, 0))]
        args += list(tables)
    return _call(body, name, [_sds((S, heads * dk), F32), _sds((S, heads * dk), F32), _sds((S, heads * dv), F32)],
                 (heads, nq), in_specs, [qspec, kspec, vspec], scratch=[pltpu.VMEM((T, dk), F32)],
                 sem=("parallel", "arbitrary"), side=side)(*args)


def _rope_tables(pos):
    def tables(dim):
        inv_freq = ROPE_THETA ** (-jnp.arange(0, dim, 2, dtype=F32) / dim)
        ang = pos.astype(F32)[:, None] * inv_freq
        return jnp.cos(ang), jnp.sin(ang)

    cm, sm = tables(ROPE)
    S = pos.shape[0]
    z32, z64 = jnp.zeros((S, 32), F32), jnp.zeros((S, 64), F32)
    cr, sr = tables(RET_DK)
    return (jnp.concatenate([cm, cm, z64], 1), jnp.concatenate([z32, sm, z64], 1),
            jnp.concatenate([-sm, z32, z64], 1), cr, sr)


def _row(v):
    return v.reshape(1, -1).astype(F32)


def _local_step(x, pos, target, pipe, P):
    tabs = _rope_tables(pos)
    dtabs = _decay_tables(ATT_BLOCK)
    xf, xb = _ln_fwd([x], [1.0], _row(P["ln_in_g"]), _row(P["ln_in_b"]), "ln_in", False)
    pipe.gather_first()
    saved = []
    for l in range(DEPTH):
        w = functools.partial(pipe.weight, l)
        t = f"_l{l}"
        h = pipe.run(_matmul, "mm_h" + t, xb, w("w_in"))
        qn, kvn, kr, rq, rk, rv = _prep1(h, tabs, _row(P["q_norm_g"][l]), _row(P["kv_norm_g"][l]), "prep1" + t)
        q = _matmul(qn, w("w_uq"), "mm_q" + t)
        kv = _matmul(kvn, w("w_ukv"), "mm_kv" + t)
        qm, km, vm = _prep2(q, kv, kr, tabs, "prep2" + t)
        a, lse = pipe.run(_attn_fwd, "mla_fwd" + t, qm, km, vm, MLA_HEADS, HEAD_PAD, VDIM, True)
        o = pipe.run(_attn_fwd, "ret_fwd" + t, rq, rk, rv, RET_HEADS, RET_DK, RET_DV, False, tables=dtabs)
        mixin = _gn_gate(a, o, h, _row(P["ret_gn_g"][l]), _row(P["ret_gn_b"][l]), "gn_gate" + t)
        mix = _matmul(mixin, w("w_out"), "mm_mix" + t)
        z1, x1f, x1b = _ln_fwd([xf, mix], [ALPHA, 1.0], _row(P["ln1_g"][l]), _row(P["ln1_b"][l]), "ln1" + t, True)
        gu, act = pipe.run(_matmul_swiglu, "mm_gu" + t, x1b, w("w_gu"))
        f = pipe.run(_matmul, "mm_down" + t, act, w("w_down"))
        z2, x2f, x2b = _ln_fwd([x1f, f], [ALPHA, 1.0], _row(P["ln2_g"][l]), _row(P["ln2_b"][l]), "ln2" + t, True)
        saved.append(dict(xb=xb, h=h, qn=qn, kvn=kvn, rq=rq, rk=rk, rv=rv, qm=qm, km=km, vm=vm, a=a, lse=lse, o=o,
                          mixin=mixin, z1=z1, x1b=x1b, gu=gu, act=act, z2=z2))
        xf, xb = x2f, x2b

    dy, sqerr = _loss_head(xf, target, "loss_head")
    dP = {}
    dys, coefs = [dy], [1.0]
    for l in reversed(range(DEPTH)):
        w, sv = functools.partial(pipe.weight, l), saved[l]
        t = f"_l{l}"
        dz2, dz2b, dg, db = _ln_bwd(dys, coefs, sv["z2"], _row(P["ln2_g"][l]), "ln2_bwd" + t)
        dP[("ln2_g", l)], dP[("ln2_b", l)] = dg, db
        pipe.reduce(l, w_down=pipe.run(_matmul, "mm_dw_down" + t, sv["act"], dz2b, ta=True, out_dtype=BF16))
        dact = pipe.run(_matmul, "mm_dact" + t, dz2b, w("w_down"), tb=True)
        dgu = _swiglu_bwd(sv["gu"], dact, "swiglu_bwd" + t)
        pipe.reduce(l, w_gu=pipe.run(_matmul, "mm_dw_gu" + t, sv["x1b"], dgu, ta=True, out_dtype=BF16))
        dx1 = pipe.run(_matmul, "mm_dx1" + t, dgu, w("w_gu"), tb=True)
        dz1, dz1b, dg, db = _ln_bwd([dz2, dx1], [ALPHA, 1.0], sv["z1"], _row(P["ln1_g"][l]), "ln1_bwd" + t)
        dP[("ln1_g", l)], dP[("ln1_b", l)] = dg, db
        pipe.reduce(l, w_out=_matmul(sv["mixin"], dz1b, "mm_dw_out" + t, ta=True, out_dtype=BF16))
        dmixin = pipe.run(_matmul, "mm_dmixin" + t, dz1b, w("w_out"), tb=True)
        do, drg, dgg, dgb = _gn_gate_bwd(dmixin, sv["o"], sv["h"], _row(P["ret_gn_g"][l]), _row(P["ret_gn_b"][l]),
                                         "gn_gate_bwd" + t)
        dP[("ret_gn_g", l)], dP[("ret_gn_b", l)] = dgg, dgb
        drq, drk, drv = pipe.run(_attn_bwd, "ret_bwd" + t, sv["rq"], sv["rk"], sv["rv"], do, RET_HEADS, RET_DK, RET_DV,
                                 False, tables=dtabs)
        dqm, dkm, dvm = pipe.run(_attn_bwd, "mla_bwd" + t, sv["qm"], sv["km"], sv["vm"], dmixin, MLA_HEADS, HEAD_PAD,
                                 VDIM, True, o=sv["a"], lse=sv["lse"])
        dq, dkv, dkr = _prep2_bwd(dqm, dkm, dvm, tabs, "prep2_bwd" + t)
        g_uq = _matmul(sv["qn"], dq, "mm_dw_uq" + t, ta=True, out_dtype=BF16)
        dqn = _matmul(dq, w("w_uq"), "mm_dqn" + t, tb=True)
        g_ukv = _matmul(sv["kvn"], dkv, "mm_dw_ukv" + t, ta=True, out_dtype=BF16)
        dkvn = _matmul(dkv, w("w_ukv"), "mm_dkvn" + t, tb=True)
        dh, dqg, dkvg = _prep1_bwd(dqn, dkvn, dkr, drq, drk, drv, drg, sv["h"], tabs, _row(P["q_norm_g"][l]),
                                   _row(P["kv_norm_g"][l]), "prep1_bwd" + t)
        dP[("q_norm_g", l)], dP[("kv_norm_g", l)] = dqg, dkvg
        pipe.reduce(l, w_uq=g_uq, w_ukv=g_ukv,
                    w_in=pipe.run(_matmul, "mm_dw_in" + t, sv["xb"], dh, ta=True, out_dtype=BF16))
        dxl = pipe.run(_matmul, "mm_dxl" + t, dh, w("w_in"), tb=True)
        dys, coefs = [dz1, dxl], [ALPHA, 1.0]
    grad_x, _, dg, db = _ln_bwd(dys, coefs, x, _row(P["ln_in_g"]), "ln_in_bwd")
    dP[("ln_in_g", None)], dP[("ln_in_b", None)] = dg, db
    return sqerr, grad_x, dP


INTERNAL_OF = {"w_in": ("w_in",), "w_uq": ("w_uq",), "w_ukv": ("w_ukv",), "w_out": ("w_out",),
               "w_gu": ("w_gate", "w_up@a", "w_up@b"), "w_down": ("w_down",)}
UP_SPLIT = 1024


def _internal_weight(name, *blocks):
    cat = lambda parts: jnp.concatenate(parts, axis=1)
    cols = lambda b: cat([b[j] for j in range(N_CHIPS)])
    b = blocks[0]
    if name in ("w_out", "w_down"):
        return b.reshape(-1, b.shape[-1])
    if name == "w_gu":
        gate, up_a, up_b = blocks
        return cat([p for j in range(N_CHIPS) for p in (gate[j], jnp.concatenate([up_a[j], up_b[j]], axis=0))])
    if name == "w_in":
        return cat([b[0][:, :MLA_IN_USED], jnp.zeros((D_MODEL, MLA_IN - MLA_IN_USED), BF16), b[0][:, MLA_IN_USED:]]
                   + [b[j] for j in range(1, N_CHIPS)])
    if name == "w_uq":
        uq, hw = cols(b), NOPE + ROPE
        pad = jnp.zeros((Q_LORA, HEAD_PAD - hw), BF16)
        return cat([p for h in range(MLA_HEADS) for p in (uq[:, h * hw:(h + 1) * hw], pad)])
    ukv = cols(b)
    return cat([ukv[:, 256 * h:256 * h + NOPE] for h in range(MLA_HEADS)]
               + [ukv[:, 256 * h + NOPE:256 * h + 256] for h in range(MLA_HEADS)])


def _grad_shards(name, g):
    cat = lambda parts: jnp.concatenate(parts, axis=1)
    if name in ("w_out", "w_down"):
        return {name: g.reshape(N_CHIPS, -1, g.shape[-1])}
    if name == "w_gu":
        return {"w_gate": _ColBlocks(g, 0), "w_up": _ColBlocks(g, 1)}
    if name == "w_in":
        ci, shift = BIG_SHARD["w_in"][1], MLA_IN - MLA_IN_USED
        return {name: [cat([g[:, :MLA_IN_USED], g[:, MLA_IN:ci + shift]])]
                + [g[:, ci * j + shift:ci * (j + 1) + shift] for j in range(1, N_CHIPS)]}
    if name == "w_uq":
        cq = NOPE + ROPE
        return {name: [cat([g[:, HEAD_PAD * h:HEAD_PAD * h + cq] for h in (2 * j, 2 * j + 1)]) for j in range(N_CHIPS)]}
    return {name: [cat([g[:, o + NOPE * h:o + NOPE * (h + 1)] for h in (2 * j, 2 * j + 1) for o in (0, MLA_HEADS * NOPE)])
                   for j in range(N_CHIPS)]}


def _small_layout(P):
    out, at = {}, 0
    for n in SMALL:
        out[n] = (at, P[n].size)
        at += P[n].size
    return out, at


def _flatten_small(P, last):
    v = jnp.concatenate([P[n].reshape(-1).astype(F32) for n in SMALL] + [last.reshape(-1).astype(F32)])
    return jnp.pad(v, (0, SMALL_ROWS * FLAT_W - v.size)).reshape(SMALL_ROWS, FLAT_W)


def _place():
    return lax.axis_index("x"), lax.axis_index("y"), lax.axis_index("c")


def _other_chips(x, y):
    return [(1 - x, y), (x, 1 - y), (1 - x, 1 - y)]


def _rcopy(src, dst, ssem, rsem, dev):
    return pltpu.make_async_remote_copy(src_ref=src, dst_ref=dst, send_sem=ssem, recv_sem=rsem, device_id=dev,
                                        device_id_type=MESH)


def _comm_call(body, name, out_shape, n_in, scratch):
    many = isinstance(out_shape, (list, tuple))
    return pl.pallas_call(body, name=name, out_shape=out_shape, in_specs=[HBM] * n_in,
                          out_specs=[HBM] * len(out_shape) if many else HBM, scratch_shapes=scratch)


def _half(ref, which):
    rows = ref.shape[0] // 2
    return ref.at[pl.ds(pl.multiple_of(which * rows, 16), rows)]


def _dma_sems(n):
    return pltpu.SemaphoreType.DMA((n,))


def _allgather_side(ws):
    k = len(ws)

    def peers():
        x, y, c = _place()
        return c, 2 * x + y, (x, y, 1 - c), [(n, t, cx, cy) for n in range(k) for t, (cx, cy) in enumerate(_other_chips(x, y))]

    def outgoing(w_refs, g_refs, sems):
        ssem, rsem, _, _, ossem, orsem = sems
        c, j, sib, nt = peers()
        owns = [_rcopy(w_refs[n], g_refs[n].at[j], ossem.at[n], orsem.at[n], sib) for n in range(k)]
        sends = [_rcopy(_half(w_refs[n], c), _half(g_refs[n].at[j], c), ssem.at[3 * n + t], rsem.at[3 * n + t],
                        (cx, cy, c)) for n, t, cx, cy in nt]
        return owns, sends

    def incoming(g_refs, sems):
        ssem, rsem, fssem, frsem, _, _ = sems
        c, _, sib, nt = peers()
        landed, passed, relayed = [], [], []
        for n, t, cx, cy in nt:
            mine, other = (_half(g_refs[n].at[2 * cx + cy], h) for h in (c, 1 - c))
            landed.append(_rcopy(mine, mine, ssem.at[3 * n + t], rsem.at[3 * n + t], (cx, cy, c)))
            passed.append(_rcopy(mine, mine, fssem.at[3 * n + t], frsem.at[3 * n + t], sib))
            relayed.append(_rcopy(other, other, fssem.at[3 * n + t], frsem.at[3 * n + t], sib))
        return landed, passed, relayed

    def start(w_refs, g_refs, sems):
        owns, sends = outgoing(w_refs, g_refs, sems)
        for cp in sends + owns:
            cp.start()

    def finish(w_refs, g_refs, sems):
        owns, sends = outgoing(w_refs, g_refs, sems)
        landed, passed, relayed = incoming(g_refs, sems)
        for got, on in zip(landed, passed):
            got.wait_recv()
            on.start()
        for cp in relayed:
            cp.wait_recv()
        for cp in owns:
            cp.wait()
        for cp in sends + passed:
            cp.wait_send()

    return _Side(list(ws), [_sds((N_CHIPS,) + w.shape, w.dtype) for w in ws],
                 [_dma_sems(3 * k)] * 4 + [_dma_sems(k)] * 2, start, finish)


def _exchange_side(parts):
    k = len(parts)

    def copies(p_refs, rcv_refs, sems):
        ssem, rsem = sems
        x, y, c = _place()
        return [_rcopy(p_refs[n].at[2 * cx + cy], rcv_refs[n].at[t], ssem.at[3 * n + t], rsem.at[3 * n + t], (cx, cy, c))
                for n in range(k) for t, (cx, cy) in enumerate(_other_chips(x, y))]

    def start(p_refs, rcv_refs, sems):
        for cp in copies(p_refs, rcv_refs, sems):
            cp.start()

    def finish(p_refs, rcv_refs, sems):
        for cp in copies(p_refs, rcv_refs, sems):
            cp.wait()

    return _Side(list(parts), [_sds((3,) + p.shape[1:], p.dtype) for p in parts], [_dma_sems(3 * k)] * 2, start, finish)


def _run_side(side, name):
    k_in, k_out = len(side.arrays), len(side.out_shape)

    def body(*refs):
        parts = refs[:k_in], refs[k_in:k_in + k_out], refs[k_in + k_out:]
        side.start(*parts)
        side.finish(*parts)

    return _comm_call(body, name, list(side.out_shape), k_in, list(side.scratch))(*side.arrays)


def _sibling_side(arrays, out_shape, n_copies, copies):
    def start(in_refs, out_refs, sems):
        for cp in copies(in_refs, out_refs, sems):
            cp.start()

    def finish(in_refs, out_refs, sems):
        for cp in copies(in_refs, out_refs, sems):
            cp.wait()

    return _Side(list(arrays), out_shape, [_dma_sems(n_copies)] * 2, start, finish)


class _ColBlocks:
    def __init__(self, array, off):
        self.array, self.off, self.dtype = array, off, array.dtype
        self.shape = (N_CHIPS, array.shape[0], GU_BLOCK)

    def block(self, ref, jj):
        return ref.at[:, pl.ds((2 * jj + self.off) * GU_BLOCK, GU_BLOCK)]


def _swap_side(gds):
    k = len(gds)

    def copies(gd_refs, out_refs, sems):
        ssem, rsem = sems
        x, y, c = _place()
        blocks = [[g.block(gd_refs[n], jj) if isinstance(g, _ColBlocks) else gd_refs[n].at[jj] for jj in range(N_CHIPS)]
                  for n, g in enumerate(gds)]
        return [_rcopy(_half(blocks[n][jj], 1 - c), out_refs[n].at[jj], ssem.at[N_CHIPS * n + jj],
                       rsem.at[N_CHIPS * n + jj], (x, y, 1 - c)) for n in range(k) for jj in range(N_CHIPS)]

    return _sibling_side([g.array if isinstance(g, _ColBlocks) else g for g in gds],
                         [_sds((N_CHIPS, g.shape[1] // 2, g.shape[2]), g.dtype) for g in gds], N_CHIPS * k, copies)


def _share_side(reds):
    k = len(reds)

    def copies(r_refs, out_refs, sems):
        ssem, rsem = sems
        x, y, c = _place()
        return [_rcopy(r_refs[n], out_refs[n], ssem.at[n], rsem.at[n], (x, y, 1 - c)) for n in range(k)]

    return _sibling_side(reds, [_sds(r.shape, r.dtype) for r in reds], k, copies)


def _join_sides(sides):
    if len(sides) == 1:
        return sides[0]
    cuts = [(len(s.arrays), len(s.out_shape), len(s.scratch)) for s in sides]

    def each(method, in_refs, out_refs, sems):
        a = o = m = 0
        for s, (ka, ko, km) in zip(sides, cuts):
            getattr(s, method)(in_refs[a:a + ka], out_refs[o:o + ko], sems[m:m + km])
            a, o, m = a + ka, o + ko, m + km

    return _Side([x for s in sides for x in s.arrays], [x for s in sides for x in s.out_shape],
                 [x for s in sides for x in s.scratch], functools.partial(each, "start"), functools.partial(each, "finish"))


def _allreduce_small(small):
    def body(s_ref, all_ref, sssem, srsem, lsem):
        x, y, c = _place()
        me = 4 * x + 2 * y + c
        own = pltpu.make_async_copy(s_ref, all_ref.at[me], lsem)
        own.start()
        cps = []
        for r in range(1, 8):
            fx, fy, fc = (r >> 2) & 1, (r >> 1) & 1, r & 1
            px, py, pc = (1 - x if fx else x, 1 - y if fy else y, 1 - c if fc else c)
            peer = 4 * px + 2 * py + pc
            send = _rcopy(s_ref, all_ref.at[me], sssem.at[r - 1], srsem.at[me], (px, py, pc))
            send.start()
            cps.append((send, _rcopy(s_ref, all_ref.at[peer], sssem.at[r - 1], srsem.at[peer], (px, py, pc))))
        for send, recv in cps:
            send.wait_send()
            recv.wait_recv()
        own.wait()

    return _comm_call(body, "allreduce_small", [_sds((8,) + small.shape, small.dtype)], 1,
                      [pltpu.SemaphoreType.DMA((7,)), pltpu.SemaphoreType.DMA((8,)), pltpu.SemaphoreType.DMA(())])(small)[0]


def _add_pair(gd, got, c, name):
    _, R, W = got.shape
    tm = _pick(R, (512, 256, 128, 64))
    nb = R // tm

    def body(c_ref, a_ref, b_ref, o_ref):
        o_ref[...] = (a_ref[...].astype(F32) + b_ref[...].astype(F32)).astype(o_ref.dtype)

    if isinstance(gd, _ColBlocks):
        off = gd.off
        own = pl.BlockSpec((tm, W), lambda j, i, c_ref: (c_ref[0] * nb + i, 2 * j + off))
        gd = gd.array
    else:
        own = pl.BlockSpec((None, tm, W), lambda j, i, c_ref: (j, c_ref[0] * nb + i, 0))
    grid_spec = pltpu.PrefetchScalarGridSpec(
        num_scalar_prefetch=1, grid=(N_CHIPS, nb),
        in_specs=[own, pl.BlockSpec((None, tm, W), lambda j, i, c_ref: (j, i, 0))],
        out_specs=pl.BlockSpec((None, tm, W), lambda j, i, c_ref: (j, i, 0)))
    return pl.pallas_call(body, name=name, grid_spec=grid_spec, out_shape=_sds((N_CHIPS, R, W), gd.dtype),
                          compiler_params=pltpu.CompilerParams(dimension_semantics=("parallel", "parallel"),
                                                               vmem_limit_bytes=VMEM_LIMIT))(c, gd, got)


def _add_chips(part, rcv, j, name):
    _, R, W = part.shape
    tm = _pick(R, (512, 256, 128, 64))

    def body(j_ref, p_ref, r0_ref, r1_ref, r2_ref, o_ref):
        o_ref[...] = ((p_ref[...].astype(F32) + r0_ref[...].astype(F32)) + r1_ref[...].astype(F32)) + r2_ref[...].astype(F32)

    def slot(t):
        return pl.BlockSpec((None, tm, W), lambda i, j_ref: (t, i, 0))

    grid_spec = pltpu.PrefetchScalarGridSpec(
        num_scalar_prefetch=1, grid=(R // tm,),
        in_specs=[pl.BlockSpec((None, tm, W), lambda i, j_ref: (j_ref[0], i, 0)), slot(0), slot(1), slot(2)],
        out_specs=pl.BlockSpec((tm, W), lambda i, j_ref: (i, 0)))
    return pl.pallas_call(body, name=name, grid_spec=grid_spec, out_shape=_sds((R, W), F32),
                          compiler_params=pltpu.CompilerParams(dimension_semantics=("parallel",),
                                                               vmem_limit_bytes=VMEM_LIMIT))(j, part, rcv, rcv, rcv)


def _sum_small(allsmall):
    _, R, W = allsmall.shape

    def body(a_ref, o_ref):
        acc = a_ref[0]
        for d in range(1, 8):
            acc = acc + a_ref[d]
        o_ref[...] = acc

    return _call(body, "sum_small", _sds((R, W), F32), (1,), [_whole((8, R, W))], _whole((R, W)),
                 sem=("arbitrary",))(allsmall)


def _adamw(w, g, m, v, name):
    R, C = w.shape
    tm = _pick(R, (256, 128, 64, 32, 8))

    def body(w_ref, g_ref, m_ref, v_ref, d_ref, mo_ref, vo_ref):
        gv = g_ref[...]
        mn = ADAM_B1 * m_ref[...] + (1.0 - ADAM_B1) * gv
        vn = ADAM_B2 * v_ref[...] + (1.0 - ADAM_B2) * (gv * gv)
        m_hat = mn / (1.0 - ADAM_B1 ** ADAM_STEP)
        v_hat = vn / (1.0 - ADAM_B2 ** ADAM_STEP)
        d_ref[...] = -ADAM_LR * (m_hat / (jnp.sqrt(v_hat) + ADAM_EPS) + ADAM_WD * w_ref[...])
        mo_ref[...] = mn
        vo_ref[...] = vn

    spec = _rows(tm, C)
    return _call(body, name, [_sds((R, C), F32)] * 3, (R // tm,), [spec] * 4, [spec] * 3, sem=("parallel",))(w, g, m, v)


def _adamw_layer(c, w, m, v, mine, other, l, prev, name):
    _, R, C = w.shape
    half = R // 2
    tm = _pick(half, (256, 128, 64))
    nbh = half // tm

    def body(c_ref, w_ref, m_ref, v_ref, a_ref, b_ref, *rest):
        g_ref, d_ref, mo_ref, vo_ref = rest[-4:]
        gv = jnp.where(pl.program_id(0) // nbh == c_ref[0], a_ref[...], b_ref[...])
        mn = ADAM_B1 * m_ref[...] + (1.0 - ADAM_B1) * gv
        vn = ADAM_B2 * v_ref[...] + (1.0 - ADAM_B2) * (gv * gv)
        m_hat = mn / (1.0 - ADAM_B1 ** ADAM_STEP)
        v_hat = vn / (1.0 - ADAM_B2 ** ADAM_STEP)
        g_ref[...] = gv
        d_ref[...] = -ADAM_LR * (m_hat / (jnp.sqrt(v_hat) + ADAM_EPS) + ADAM_WD * w_ref[...])
        mo_ref[...] = mn
        vo_ref[...] = vn

    layer = pl.BlockSpec((None, tm, C), lambda i, c_ref: (l, i, 0))
    halfspec = pl.BlockSpec((tm, C), lambda i, c_ref: (i % nbh, 0))
    n_prev = 0 if prev is None else 4
    grid_spec = pltpu.PrefetchScalarGridSpec(
        num_scalar_prefetch=1, grid=(R // tm,),
        in_specs=[layer] * 3 + [halfspec] * 2 + [pl.BlockSpec(memory_space=pl.ANY)] * n_prev,
        out_specs=[layer] * 4)
    return pl.pallas_call(body, name=name, grid_spec=grid_spec, out_shape=[_sds(w.shape, F32)] * 4,
                          input_output_aliases={6 + k: k for k in range(n_prev)},
                          compiler_params=pltpu.CompilerParams(dimension_semantics=("parallel",),
                                                               vmem_limit_bytes=VMEM_LIMIT))(
        c, w, m, v, mine, other, *(prev or ()))


FIRST_GATHER = ("w_in", "w_uq", "w_ukv")
G_DOWN, G_GU, G_OUT, G_IN = ("w_down",), ("w_gate", "w_up"), ("w_out",), ("w_uq", "w_ukv", "w_in")


def _backward_jobs(l):
    t = f"_l{l}"
    return {"mm_dact" + t: [("swap", l, G_DOWN)], "mm_dw_gu" + t: [("exchange", l, G_DOWN)],
            "mm_dx1" + t: [("swap", l, G_GU), ("share", l, G_DOWN)], "mm_dmixin" + t: [("swap", l, G_OUT)],
            "ret_bwd" + t: [("exchange", l, ("w_gate",))],
            "mla_bwd" + t: [("exchange", l, ("w_up", "w_out")), ("share", l, ("w_gate",))],
            "mm_dw_in" + t: [("share", l, ("w_up", "w_out"))]}


JOBS = {
    "mm_h_l0": [("gather", 0, ("w_up@a",))], "mla_fwd_l0": [("gather", 0, ("w_gate", "w_out"))],
    "ret_fwd_l0": [("gather", 0, ("w_up@b",))],
    "mm_gu_l0": [("gather", 0, ("w_down",)), ("gather", 1, ("w_uq", "w_ukv"))],
    "mm_down_l0": [("gather", 1, ("w_in",))], "mm_h_l1": [("gather", 1, ("w_up@a",))],
    "mla_fwd_l1": [("gather", 1, ("w_gate", "w_out"))], "ret_fwd_l1": [("gather", 1, ("w_up@b",))],
    "mm_gu_l1": [("gather", 1, ("w_down",))],
    **_backward_jobs(1), **_backward_jobs(0),
    "mm_dxl_l1": [("swap", 1, G_IN)],
    "mm_dx1_l0": [("swap", 0, G_GU), ("share", 0, G_DOWN), ("exchange", 1, G_IN)],
    "ret_bwd_l0": [("exchange", 0, ("w_gate",)), ("share", 1, G_IN)], "mm_dxl_l0": [("exchange", 0, G_IN)]}
PLANNED = {job for jobs in JOBS.values() for job in jobs}


class _Pipeline:
    def __init__(self, own, Wt, Mo, Vo, core, chip):
        self.own, self.Wt, self.Mo, self.Vo, self.core, self.chip = own, Wt, Mo, Vo, core, chip
        self.blocks, self.whole, self.gds, self.parts, self.reds = {}, {}, {}, {}, {}
        self.results = {n: None for n in BIG}

    def gather_first(self):
        job = ("gather", 0, FIRST_GATHER)
        self._done(*job, _run_side(self._side(*job), "allgather_first"))

    def weight(self, l, name):
        if (l, name) not in self.whole:
            self.whole[(l, name)] = _internal_weight(name, *[self.blocks[(l, n)] for n in INTERNAL_OF[name]])
        return self.whole[(l, name)]

    def run(self, fn, name, *args, **kw):
        jobs = JOBS.get(name, ())
        if not jobs:
            return fn(*args, name=name, **kw)
        sides = [self._side(*job) for job in jobs]
        out, res = fn(*args, name=name, side=_join_sides(sides), **kw)
        for job, side in zip(jobs, sides):
            k = len(side.out_shape)
            self._done(*job, res[:k])
            res = res[k:]
        return out

    def reduce(self, l, **grads):
        shards = {}
        for name, g in grads.items():
            shards.update(_grad_shards(name, g))
        for n, sh in shards.items():
            self.gds[(l, n)] = sh if hasattr(sh, "shape") else jnp.stack(sh)
        self._alone("swap", l, tuple(shards))

    def _alone(self, kind, l, names):
        if (kind, l, names) not in PLANNED:
            self._done(kind, l, names, _run_side(self._side(kind, l, names), f"{kind}_{names[0]}_l{l}"))

    def _side(self, kind, l, names):
        if kind == "gather":
            return _allgather_side([self.own[l][n] for n in names])
        store = {"swap": self.gds, "exchange": self.parts, "share": self.reds}[kind]
        make = {"swap": _swap_side, "exchange": _exchange_side, "share": _share_side}[kind]
        return make([store[(l, n)] for n in names])

    def _done(self, kind, l, names, res):
        for n, r in zip(names, res):
            if kind == "gather":
                self.blocks[(l, n)] = r
            elif kind == "swap":
                self.parts[(l, n)] = _add_pair(self.gds[(l, n)], r, self.core, f"add_pair_{n}_l{l}")
            elif kind == "exchange":
                self.reds[(l, n)] = _add_chips(self.parts[(l, n)], r, self.chip, f"add_chips_{n}_l{l}")
            else:
                self.results[n] = _adamw_layer(self.core, self.Wt[n], self.Mo[n], self.Vo[n], self.reds[(l, n)], r, l,
                                               self.results[n], f"adamw_{n}_l{l}")
        if kind == "exchange":
            self._alone("share", l, names)


def kernel(x, positions, ln_in_g, ln_in_b, w_in, q_norm_g, kv_norm_g, w_uq, w_ukv, ret_gn_g, ret_gn_b, w_out, ln1_g, ln1_b, w_gate, w_up, w_down, ln2_g, ln2_b, loss_target, m_ln_in_g, m_ln_in_b, m_w_in, m_q_norm_g, m_kv_norm_g, m_w_uq, m_w_ukv, m_ret_gn_g, m_ret_gn_b, m_w_out, m_ln1_g, m_ln1_b, m_w_gate, m_w_up, m_w_down, m_ln2_g, m_ln2_b, v_ln_in_g, v_ln_in_b, v_w_in, v_q_norm_g, v_kv_norm_g, v_w_uq, v_w_ukv, v_ret_gn_g, v_ret_gn_b, v_w_out, v_ln1_g, v_ln1_b, v_w_gate, v_w_up, v_w_down, v_ln2_g, v_ln2_b):
    given = dict(locals())
    Wt = {n: given[n] for n in WEIGHTS}
    Mo = {n: given["m_" + n] for n in WEIGHTS}
    Vo = {n: given["v_" + n] for n in WEIGHTS}
    cx, cy, cc = _place()
    chip = (2 * cx + cy).astype(jnp.int32)
    core = cc.astype(jnp.int32)

    own = [{n: Wt[n][l].astype(BF16) for n in BIG} for l in range(DEPTH)]
    for shard in own:
        shard["w_up@a"], shard["w_up@b"] = shard["w_up"][:UP_SPLIT], shard["w_up"][UP_SPLIT:]
    pipe = _Pipeline(own, Wt, Mo, Vo, core.reshape(1), chip.reshape(1))
    sqerr, grad_x, dP = _local_step(x[0], positions[0], loss_target[0], pipe, Wt)
    results = pipe.results

    small_g = {n: (dP[(n, None)] if Wt[n].ndim == 1 else jnp.stack([dP[(n, l)] for l in range(DEPTH)])) for n in SMALL}
    local_loss = 0.5 * jnp.sum(sqerr) / D_MODEL
    small_sum = _sum_small(_allreduce_small(_flatten_small(small_g, local_loss))).reshape(-1)
    layout, n_small = _small_layout(Wt)
    loss = small_sum[n_small]

    grads, deltas, new_m, new_v = {}, {}, {}, {}
    for n in BIG:
        grads[n], deltas[n], new_m[n], new_v[n] = results[n]
    zero = jnp.zeros((), F32)
    d, mn, vn = _adamw(_flatten_small(Wt, zero), small_sum.reshape(SMALL_ROWS, FLAT_W), _flatten_small(Mo, zero),
                       _flatten_small(Vo, zero), "adamw_small")
    for n in SMALL:
        at, size = layout[n]
        pick = lambda a: a.reshape(-1)[at:at + size].reshape(Wt[n].shape)
        grads[n], deltas[n], new_m[n], new_v[n] = pick(small_sum), pick(d), pick(mn), pick(vn)

    return (loss, grad_x[None], *[grads[n] for n in WEIGHTS], *[deltas[n] for n in WEIGHTS],
            *[new_m[n] for n in WEIGHTS], *[new_v[n] for n in WEIGHTS])
```

```python
import functools

import jax
import jax.numpy as jnp
from jax import lax
from jax.experimental import pallas as pl
from jax.experimental.pallas import tpu as pltpu

F32 = jnp.float32
BF16 = jnp.bfloat16

D_MODEL = 2048
DEPTH = 2
CHUNK = 64
MLA_HEADS = 8
Q_LORA = 512
KV_LORA = 256
NOPE = 128
ROPE = 64
VDIM = 128
RET_HEADS = 4
RET_DK = 256
RET_DV = 256
D_FF = 5632
D_IN = 4928
ROPE_THETA = 10000.0
LN_EPS = 1e-5
RMS_EPS = 1e-6
GN_EPS = 1e-5
ALPHA = (2 * DEPTH) ** 0.25
MLA_SCALE = (NOPE + ROPE) ** -0.5
RET_SCALE = RET_DK ** -0.5
ADAM_LR = 0.001
ADAM_B1 = 0.9
ADAM_B2 = 0.999
ADAM_EPS = 1e-08
ADAM_WD = 0.01
ADAM_STEP = 10

LANES = 128
HEAD_PAD = 256
MLA_IN = 1024
MLA_IN_USED = Q_LORA + KV_LORA + ROPE
D_IN_PAD = MLA_IN + 4 * 1024
ATT_BLOCK = 512
NEG = -1e30
VMEM_LIMIT = 56 * 1024 * 1024

N_CHIPS = 4
FLAT_W = 1024
BIG = ("w_in", "w_uq", "w_ukv", "w_out", "w_gate", "w_up", "w_down")
BIG_SHARD = {"w_in": (2048, 1232), "w_uq": (512, 384), "w_ukv": (256, 512), "w_out": (512, 2048),
             "w_gate": (2048, 1408), "w_up": (2048, 1408), "w_down": (1408, 2048)}
SMALL = ("ln_in_g", "ln_in_b", "q_norm_g", "kv_norm_g", "ret_gn_g", "ret_gn_b", "ln1_g", "ln1_b", "ln2_g", "ln2_b")
WEIGHTS = ("ln_in_g", "ln_in_b", "w_in", "q_norm_g", "kv_norm_g", "w_uq", "w_ukv", "ret_gn_g", "ret_gn_b", "w_out",
           "ln1_g", "ln1_b", "w_gate", "w_up", "w_down", "ln2_g", "ln2_b")
SMALL_ROWS = 32

MESH = pl.DeviceIdType.MESH


def _pick(dim, cands):
    for c in cands:
        if dim % c == 0:
            return c
    return dim


HBM = pl.BlockSpec(memory_space=pltpu.HBM)


class _Side:
    def __init__(self, arrays, out_shape, scratch, start, finish):
        self.arrays, self.out_shape, self.scratch, self.start, self.finish = arrays, out_shape, scratch, start, finish


def _call(body, name, out_shape, grid, in_specs, out_specs, scratch=(), sem=None, side=None):
    params = pltpu.CompilerParams(dimension_semantics=sem if side is None else ("arbitrary",) * len(grid),
                                  vmem_limit_bytes=VMEM_LIMIT)
    if side is None:
        return pl.pallas_call(body, name=name, out_shape=out_shape, grid=grid, in_specs=in_specs, out_specs=out_specs,
                              scratch_shapes=list(scratch), compiler_params=params)
    single = not isinstance(out_shape, (list, tuple))
    outs = [out_shape] if single else list(out_shape)
    ospecs = [out_specs] if single else list(out_specs)
    cuts = [len(in_specs), len(side.arrays), len(outs), len(side.out_shape), len(scratch)]
    ends = [sum(cuts[:k + 1]) for k in range(len(cuts))]

    def hosted(*refs):
        ins, s_in, o, s_out, scr = (refs[a:b] for a, b in zip([0] + ends[:-1], ends))
        sems = refs[ends[-1]:]
        ids = [pl.program_id(a) for a in range(len(grid))]
        first = functools.reduce(jnp.logical_and, [i == 0 for i in ids])
        last = functools.reduce(jnp.logical_and, [i == g - 1 for i, g in zip(ids, grid)])

        @pl.when(first)
        def _():
            side.start(s_in, s_out, sems)

        body(*ins, *o, *scr)

        @pl.when(last)
        def _():
            side.finish(s_in, s_out, sems)

    call = pl.pallas_call(hosted, name=name, out_shape=outs + list(side.out_shape), grid=grid,
                          in_specs=list(in_specs) + [HBM] * len(side.arrays),
                          out_specs=ospecs + [HBM] * len(side.out_shape),
                          scratch_shapes=list(scratch) + list(side.scratch), compiler_params=params)

    def run(*args):
        res = call(*args, *side.arrays)
        return (res[0] if single else list(res[:len(outs)])), list(res[len(outs):])

    return run


def _rows(tm, w, col=0):
    return pl.BlockSpec((tm, w), lambda i: (i, col))


def _whole(shape):
    return pl.BlockSpec(shape, lambda i: (0,) * len(shape))


def _sds(shape, dtype):
    return jax.ShapeDtypeStruct(shape, dtype)


def _matmul(a, b, name, ta=False, tb=False, out_dtype=F32, side=None):
    (K, M) = a.shape if ta else a.shape[::-1]
    (N, Kb) = b.shape if tb else b.shape[::-1]
    assert K == Kb, (a.shape, b.shape, ta, tb)
    tm = _pick(M, (1024, 1408, 512, 256, 128))
    tn = _pick(N, (1024, 512, 256, 128))
    tk = _pick(K, (2816, 2560, 2048, 1024, 512, 256))
    nk = K // tk
    dn = (((0 if ta else 1,), (1 if tb else 0,)), ((), ()))

    def body(a_ref, b_ref, o_ref, acc_ref):
        k = pl.program_id(2)
        if nk == 1:
            o_ref[...] = lax.dot_general(a_ref[...].astype(BF16), b_ref[...].astype(BF16), dn,
                                         preferred_element_type=F32).astype(out_dtype)
        else:
            @pl.when(k == 0)
            def _():
                acc_ref[...] = jnp.zeros_like(acc_ref)

            acc_ref[...] += lax.dot_general(a_ref[...].astype(BF16), b_ref[...].astype(BF16), dn,
                                            preferred_element_type=F32)

            @pl.when(k == nk - 1)
            def _():
                o_ref[...] = acc_ref[...].astype(out_dtype)

    a_spec = pl.BlockSpec((tk, tm), lambda i, j, k: (k, i)) if ta else pl.BlockSpec((tm, tk), lambda i, j, k: (i, k))
    b_spec = pl.BlockSpec((tn, tk), lambda i, j, k: (j, k)) if tb else pl.BlockSpec((tk, tn), lambda i, j, k: (k, j))
    return _call(body, name, _sds((M, N), out_dtype), (M // tm, N // tn, nk), [a_spec, b_spec],
                 pl.BlockSpec((tm, tn), lambda i, j, k: (i, j)), scratch=[pltpu.VMEM((tm, tn), F32)],
                 sem=("parallel", "parallel", "arbitrary"), side=side)(a, b)


def _sigmoid(x):
    return 1.0 / (1.0 + jnp.exp(-x))


def _rope_group(r, c, sa, sb):
    return r * c + pltpu.roll(r, 32, 1) * sa + pltpu.roll(r, 96, 1) * sb


def _ln_fwd(xs, coefs, g, b, name, want_z):
    S, D = xs[0].shape
    tm = 512
    n = len(xs)

    def body(*refs):
        x_refs, g_ref, b_ref, outs = refs[:n], refs[n], refs[n + 1], refs[n + 2:]
        z = None
        for cf, r in zip(coefs, x_refs):
            t = r[...] if cf == 1.0 else cf * r[...]
            z = t if z is None else z + t
        mu = jnp.mean(z, axis=-1, keepdims=True)
        zc = z - mu
        var = jnp.mean(zc * zc, axis=-1, keepdims=True)
        y = zc * lax.rsqrt(var + LN_EPS) * g_ref[...] + b_ref[...]
        if want_z:
            outs[0][...] = z
        outs[-2][...] = y
        outs[-1][...] = y.astype(BF16)

    out_shape = [_sds((S, D), F32)] * (2 if want_z else 1) + [_sds((S, D), BF16)]
    return _call(body, name, out_shape, (S // tm,), [_rows(tm, D)] * n + [_whole((1, D))] * 2,
                 [_rows(tm, D)] * len(out_shape), sem=("parallel",))(*xs, g, b)


def _ln_bwd(dys, coefs, z, g, name):
    S, D = z.shape
    tm = 512
    n = len(dys)

    def body(*refs):
        dy_refs, z_ref, g_ref = refs[:n], refs[n], refs[n + 1]
        dz_ref, dzb_ref, dg_ref, db_ref = refs[n + 2:]
        dy = None
        for cf, r in zip(coefs, dy_refs):
            t = r[...] if cf == 1.0 else cf * r[...]
            dy = t if dy is None else dy + t
        zv = z_ref[...]
        mu = jnp.mean(zv, axis=-1, keepdims=True)
        zc = zv - mu
        var = jnp.mean(zc * zc, axis=-1, keepdims=True)
        rstd = lax.rsqrt(var + LN_EPS)
        xh = zc * rstd
        dyg = dy * g_ref[...]
        dz = rstd * (dyg - jnp.mean(dyg, axis=-1, keepdims=True) - xh * jnp.mean(dyg * xh, axis=-1, keepdims=True))
        dz_ref[...] = dz
        dzb_ref[...] = dz.astype(BF16)

        @pl.when(pl.program_id(0) == 0)
        def _():
            dg_ref[...] = jnp.zeros_like(dg_ref)
            db_ref[...] = jnp.zeros_like(db_ref)

        dg_ref[...] += jnp.sum(dy * xh, axis=0, keepdims=True)
        db_ref[...] += jnp.sum(dy, axis=0, keepdims=True)

    return _call(body, name, [_sds((S, D), F32), _sds((S, D), BF16), _sds((1, D), F32), _sds((1, D), F32)],
                 (S // tm,), [_rows(tm, D)] * (n + 1) + [_whole((1, D))],
                 [_rows(tm, D), _rows(tm, D), _whole((1, D)), _whole((1, D))], sem=("arbitrary",))(*dys, z, g)


def _rms(x, g):
    return x * lax.rsqrt(jnp.mean(x * x, axis=-1, keepdims=True) + RMS_EPS) * g


def _prep1(h, tabs, qg, kvg, name):
    S = h.shape[0]
    tm = 256
    cm, sam, sbm, cr, sr = tabs

    def body(h_ref, cm_ref, sam_ref, sbm_ref, cr_ref, sr_ref, qg_ref, kvg_ref,
             qn_ref, kvn_ref, kr_ref, rq_ref, rk_ref, rv_ref):
        qn_ref[...] = _rms(h_ref[:, 0:Q_LORA], qg_ref[...]).astype(BF16)
        kvn_ref[...] = _rms(h_ref[:, Q_LORA:Q_LORA + KV_LORA], kvg_ref[...]).astype(BF16)
        kr_ref[...] = _rope_group(h_ref[:, 768:896], cm_ref[...], sam_ref[...], sbm_ref[...])
        c, s = cr_ref[...], sr_ref[...]
        for hd in range(RET_HEADS):
            for src, dst, scale in ((MLA_IN, rq_ref, RET_SCALE), (MLA_IN + 1024, rk_ref, None)):
                t1 = h_ref[:, src + hd * 256:src + hd * 256 + 128]
                t2 = h_ref[:, src + hd * 256 + 128:src + hd * 256 + 256]
                o1, o2 = t1 * c - t2 * s, t2 * c + t1 * s
                if scale is not None:
                    o1, o2 = o1 * scale, o2 * scale
                dst[:, hd * 256:hd * 256 + 128] = o1.astype(BF16)
                dst[:, hd * 256 + 128:hd * 256 + 256] = o2.astype(BF16)
        rv_ref[...] = h_ref[:, MLA_IN + 2048:MLA_IN + 3072].astype(BF16)

    t128 = _rows(tm, LANES)
    return _call(body, name,
                 [_sds((S, Q_LORA), BF16), _sds((S, KV_LORA), BF16), _sds((S, LANES), F32),
                  _sds((S, 1024), BF16), _sds((S, 1024), BF16), _sds((S, 1024), BF16)],
                 (S // tm,),
                 [_rows(tm, D_IN_PAD), t128, t128, t128, t128, t128, _whole((1, Q_LORA)), _whole((1, KV_LORA))],
                 [_rows(tm, Q_LORA), _rows(tm, KV_LORA), t128, _rows(tm, 1024), _rows(tm, 1024), _rows(tm, 1024)],
                 sem=("parallel",))(h, cm, sam, sbm, cr, sr, qg, kvg)


def _prep1_bwd(dqn, dkvn, dkr, drq, drk, drv, drg, h, tabs, qg, kvg, name):
    S = h.shape[0]
    tm = 256
    cm, sam, sbm, cr, sr = tabs

    def rms_bwd(x, g, dy):
        r = lax.rsqrt(jnp.mean(x * x, axis=-1, keepdims=True) + RMS_EPS)
        dyg = dy * g
        dx = r * dyg - x * (r * r * r) * jnp.mean(dyg * x, axis=-1, keepdims=True)
        return dx, jnp.sum(dy * x * r, axis=0, keepdims=True)

    def body(dqn_ref, dkvn_ref, dkr_ref, drq_ref, drk_ref, drv_ref, drg_ref, h_ref,
             cm_ref, sam_ref, sbm_ref, cr_ref, sr_ref, qg_ref, kvg_ref, dh_ref, dqg_ref, dkvg_ref):
        dcq, dqg = rms_bwd(h_ref[:, 0:Q_LORA], qg_ref[...], dqn_ref[...])
        dckv, dkvg = rms_bwd(h_ref[:, Q_LORA:Q_LORA + KV_LORA], kvg_ref[...], dkvn_ref[...])
        dh_ref[:, 0:Q_LORA] = dcq.astype(BF16)
        dh_ref[:, Q_LORA:Q_LORA + KV_LORA] = dckv.astype(BF16)
        dh_ref[:, 768:896] = _rope_group(dkr_ref[...], cm_ref[...], -sam_ref[...], -sbm_ref[...]).astype(BF16)
        dh_ref[:, 896:1024] = jnp.zeros((tm, LANES), BF16)
        c, s = cr_ref[...], sr_ref[...]
        for hd in range(RET_HEADS):
            for src, dst, scale in ((drq_ref, MLA_IN, RET_SCALE), (drk_ref, MLA_IN + 1024, None)):
                d1 = src[:, hd * 256:hd * 256 + 128]
                d2 = src[:, hd * 256 + 128:hd * 256 + 256]
                if scale is not None:
                    d1, d2 = d1 * scale, d2 * scale
                dh_ref[:, dst + hd * 256:dst + hd * 256 + 128] = (d1 * c + d2 * s).astype(BF16)
                dh_ref[:, dst + hd * 256 + 128:dst + hd * 256 + 256] = (d2 * c - d1 * s).astype(BF16)
        dh_ref[:, MLA_IN + 2048:MLA_IN + 3072] = drv_ref[...].astype(BF16)
        dh_ref[:, MLA_IN + 3072:MLA_IN + 4096] = drg_ref[...].astype(BF16)

        @pl.when(pl.program_id(0) == 0)
        def _():
            dqg_ref[...] = jnp.zeros_like(dqg_ref)
            dkvg_ref[...] = jnp.zeros_like(dkvg_ref)

        dqg_ref[...] += dqg
        dkvg_ref[...] += dkvg

    t128 = _rows(tm, LANES)
    return _call(body, name,
                 [_sds((S, D_IN_PAD), BF16), _sds((1, Q_LORA), F32), _sds((1, KV_LORA), F32)],
                 (S // tm,),
                 [_rows(tm, Q_LORA), _rows(tm, KV_LORA), t128, _rows(tm, 1024), _rows(tm, 1024), _rows(tm, 1024),
                  _rows(tm, 1024), _rows(tm, MLA_IN), t128, t128, t128, t128, t128,
                  _whole((1, Q_LORA)), _whole((1, KV_LORA))],
                 [_rows(tm, D_IN_PAD), _whole((1, Q_LORA)), _whole((1, KV_LORA))],
                 sem=("arbitrary",))(dqn, dkvn, dkr, drq, drk, drv, drg, h, cm, sam, sbm, cr, sr, qg, kvg)


def _prep2(q, kv, kr, tabs, name):
    S = q.shape[0]
    tm = 256
    cm, sam, sbm = tabs[:3]

    def body(q_ref, kv_ref, kr_ref, cm_ref, sam_ref, sbm_ref, qo_ref, ko_ref, vo_ref):
        c, sa, sb = cm_ref[...], sam_ref[...], sbm_ref[...]
        krb = kr_ref[...].astype(BF16)
        ones = jnp.ones((tm, LANES), BF16)
        for hd in range(MLA_HEADS):
            o = hd * HEAD_PAD
            qo_ref[:, o:o + 128] = (q_ref[:, o:o + 128] * MLA_SCALE).astype(BF16)
            qo_ref[:, o + 128:o + 256] = (_rope_group(q_ref[:, o + 128:o + 256], c, sa, sb) * MLA_SCALE).astype(BF16)
            ko_ref[:, o:o + 128] = kv_ref[:, hd * 128:hd * 128 + 128].astype(BF16)
            ko_ref[:, o + 128:o + 256] = krb
            vo_ref[:, o:o + 128] = kv_ref[:, 1024 + hd * 128:1024 + hd * 128 + 128].astype(BF16)
            vo_ref[:, o + 128:o + 256] = ones

    t128 = _rows(tm, LANES)
    return _call(body, name, [_sds((S, 2048), BF16)] * 3, (S // tm,),
                 [_rows(tm, 2048), _rows(tm, 2048), t128, t128, t128, t128],
                 [_rows(tm, 2048)] * 3, sem=("parallel",))(q, kv, kr, cm, sam, sbm)


def _prep2_bwd(dqm, dkm, dvm, tabs, name):
    S = dqm.shape[0]
    tm = 256
    cm, sam, sbm = tabs[:3]

    def body(dq_ref, dk_ref, dv_ref, cm_ref, sam_ref, sbm_ref, dqo_ref, dkvo_ref, dkr_ref):
        c, sa, sb = cm_ref[...], -sam_ref[...], -sbm_ref[...]
        dkr = None
        for hd in range(MLA_HEADS):
            o = hd * HEAD_PAD
            dqo_ref[:, o:o + 128] = (dq_ref[:, o:o + 128] * MLA_SCALE).astype(BF16)
            dqo_ref[:, o + 128:o + 256] = (_rope_group(dq_ref[:, o + 128:o + 256], c, sa, sb) * MLA_SCALE).astype(BF16)
            dkvo_ref[:, hd * 128:hd * 128 + 128] = dk_ref[:, o:o + 128].astype(BF16)
            t = dk_ref[:, o + 128:o + 256]
            dkr = t if dkr is None else dkr + t
        dkvo_ref[:, 1024:2048] = dv_ref[...].astype(BF16)
        dkr_ref[...] = dkr

    t128 = _rows(tm, LANES)
    return _call(body, name, [_sds((S, 2048), BF16), _sds((S, 2048), BF16), _sds((S, LANES), F32)], (S // tm,),
                 [_rows(tm, 2048), _rows(tm, 2048), _rows(tm, 1024), t128, t128, t128],
                 [_rows(tm, 2048), _rows(tm, 2048), t128], sem=("parallel",))(dqm, dkm, dvm, cm, sam, sbm)


def _gn_gate(a, o, h, gg, gb, name):
    S = a.shape[0]
    tm = 256

    def body(a_ref, o_ref, rg_ref, gg_ref, gb_ref, mix_ref):
        mix_ref[:, 0:1024] = a_ref[...].astype(BF16)
        for hd in range(RET_HEADS):
            sl = slice(hd * 256, hd * 256 + 256)
            ov = o_ref[:, sl]
            mu = jnp.mean(ov, axis=-1, keepdims=True)
            oc = ov - mu
            var = jnp.mean(oc * oc, axis=-1, keepdims=True)
            y = oc * lax.rsqrt(var + GN_EPS) * gg_ref[:, sl] + gb_ref[:, sl]
            rg = rg_ref[:, sl]
            mix_ref[:, 1024 + hd * 256:1024 + hd * 256 + 256] = (rg * _sigmoid(rg) * y).astype(BF16)

    return _call(body, name, _sds((S, 2048), BF16), (S // tm,),
                 [_rows(tm, 1024), _rows(tm, 1024), _rows(tm, 1024, 4), _whole((1, 1024)), _whole((1, 1024))],
                 _rows(tm, 2048), sem=("parallel",))(a, o, h, gg, gb)


def _gn_gate_bwd(dmixin, o, h, gg, gb, name):
    S = o.shape[0]
    tm = 256

    def body(dr_ref, o_ref, rg_ref, gg_ref, gb_ref, do_ref, drg_ref, dgg_ref, dgb_ref):
        @pl.when(pl.program_id(0) == 0)
        def _():
            dgg_ref[...] = jnp.zeros_like(dgg_ref)
            dgb_ref[...] = jnp.zeros_like(dgb_ref)

        for hd in range(RET_HEADS):
            sl = slice(hd * 256, hd * 256 + 256)
            ov = o_ref[:, sl]
            mu = jnp.mean(ov, axis=-1, keepdims=True)
            oc = ov - mu
            var = jnp.mean(oc * oc, axis=-1, keepdims=True)
            rstd = lax.rsqrt(var + GN_EPS)
            xh = oc * rstd
            g = gg_ref[:, sl]
            y = xh * g + gb_ref[:, sl]
            rg = rg_ref[:, sl]
            sg = _sigmoid(rg)
            dr = dr_ref[:, sl]
            dy = dr * (rg * sg)
            drg_ref[:, sl] = dr * y * (sg * (1.0 + rg * (1.0 - sg)))
            dgg_ref[:, sl] += jnp.sum(dy * xh, axis=0, keepdims=True)
            dgb_ref[:, sl] += jnp.sum(dy, axis=0, keepdims=True)
            dxh = dy * g
            do = rstd * (dxh - jnp.mean(dxh, axis=-1, keepdims=True) - xh * jnp.mean(dxh * xh, axis=-1, keepdims=True))
            do_ref[:, sl] = do.astype(BF16)

    return _call(body, name,
                 [_sds((S, 1024), BF16), _sds((S, 1024), F32), _sds((1, 1024), F32), _sds((1, 1024), F32)],
                 (S // tm,),
                 [_rows(tm, 1024, 1), _rows(tm, 1024), _rows(tm, 1024, 4), _whole((1, 1024)), _whole((1, 1024))],
                 [_rows(tm, 1024), _rows(tm, 1024), _whole((1, 1024)), _whole((1, 1024))],
                 sem=("arbitrary",))(dmixin, o, h, gg, gb)


GU_BLOCK = D_FF // N_CHIPS


def _matmul_swiglu(x, w_gu, name, side=None):
    S, K = x.shape
    tm = _pick(S, (512, 256, 128))
    tn = 2 * GU_BLOCK

    def body(x_ref, w_ref, gu_ref, act_ref):
        r = jnp.dot(x_ref[...], w_ref[...], preferred_element_type=F32)
        g, u = r[:, :GU_BLOCK], r[:, GU_BLOCK:]
        gu_ref[...] = r.astype(BF16)
        act_ref[...] = (g * _sigmoid(g) * u).astype(BF16)

    return _call(body, name, [_sds((S, 2 * D_FF), BF16), _sds((S, D_FF), BF16)], (S // tm, N_CHIPS),
                 [pl.BlockSpec((tm, K), lambda i, j: (i, 0)), pl.BlockSpec((K, tn), lambda i, j: (0, j))],
                 [pl.BlockSpec((tm, tn), lambda i, j: (i, j)), pl.BlockSpec((tm, GU_BLOCK), lambda i, j: (i, j))],
                 sem=("parallel", "parallel"), side=side)(x, w_gu)


def _swiglu_bwd(gu, dact, name):
    S = gu.shape[0]
    tm = 256

    def body(gu_ref, d_ref, o_ref):
        for j in range(N_CHIPS):
            at = 2 * GU_BLOCK * j
            g = gu_ref[:, at:at + GU_BLOCK].astype(F32)
            u = gu_ref[:, at + GU_BLOCK:at + 2 * GU_BLOCK].astype(F32)
            d = d_ref[:, GU_BLOCK * j:GU_BLOCK * (j + 1)]
            sg = _sigmoid(g)
            o_ref[:, at:at + GU_BLOCK] = (d * u * (sg * (1.0 + g * (1.0 - sg)))).astype(BF16)
            o_ref[:, at + GU_BLOCK:at + 2 * GU_BLOCK] = (d * (g * sg)).astype(BF16)

    return _call(body, name, _sds((S, 2 * D_FF), BF16), (S // tm,), [_rows(tm, 2 * D_FF), _rows(tm, D_FF)],
                 _rows(tm, 2 * D_FF), sem=("parallel",))(gu, dact)


def _loss_head(y, target, name):
    S, D = y.shape
    tm = 512

    def body(y_ref, t_ref, dy_ref, acc_ref):
        e = y_ref[...] - t_ref[...]
        dy_ref[...] = e / D

        @pl.when(pl.program_id(0) == 0)
        def _():
            acc_ref[...] = jnp.zeros_like(acc_ref)

        acc_ref[...] += jnp.sum(e * e, axis=0, keepdims=True)

    return _call(body, name, [_sds((S, D), F32), _sds((1, D), F32)], (S // tm,), [_rows(tm, D), _rows(tm, D)],
                 [_rows(tm, D), _whole((1, D))], sem=("arbitrary",))(y, target)


def _chunk_mask(T):
    r = lax.shift_right_logical(lax.broadcasted_iota(jnp.int32, (T, T), 0), 6)
    c = lax.shift_right_logical(lax.broadcasted_iota(jnp.int32, (T, T), 1), 6)
    return r >= c


def _dot_nt(a, b):
    return lax.dot_general(a, b, (((1,), (1,)), ((), ())), preferred_element_type=F32)


def _dot_tn(a, b):
    return lax.dot_general(a, b, (((0,), (0,)), ((), ())), preferred_element_type=F32)


def _decay_tables(T):
    lg = jnp.log1p(-jnp.exp2(-5.0 - jnp.arange(RET_HEADS, dtype=F32)))
    idx = jnp.arange(T, dtype=F32)
    diff = idx[:, None] - idx[None, :]
    rel = jnp.exp(lg[:, None, None] * diff[None])
    cid = jnp.arange(T) // CHUNK
    mask = (cid[:, None] >= cid[None, :]).astype(F32)
    reld = jnp.exp(lg[:, None, None] * jnp.abs(diff)[None]) * mask[None]
    lgrow = jnp.broadcast_to(lg[:, None, None], (RET_HEADS, 1, LANES))
    return lgrow, rel, reld


def _attn_fwd(q, k, v, heads, dk, dv, softmax, name, tables=None, side=None):
    S = q.shape[0]
    T = ATT_BLOCK
    nq = S // T
    rep = T // LANES
    vw = 2 * dv if softmax else dv
    assert not softmax or dv == LANES

    def body(*refs):
        if softmax:
            q_ref, k_ref, v_ref, o_ref, lse_ref, m_sc, acc_sc = refs
        else:
            q_ref, k_ref, v_ref, lg_ref, rel_ref, reld_ref, o_ref, acc_sc = refs
        i = pl.program_id(1)
        qv = q_ref[...]

        def kv_block(j):
            rows = pl.ds(pl.multiple_of(j * T, T), T)
            return k_ref[rows, :], v_ref[rows, :]

        kb, vb = kv_block(i)
        s = _dot_nt(qv, kb)
        if softmax:
            s = jnp.where(_chunk_mask(T), s, NEG)
            m = jnp.max(s, axis=-1, keepdims=True)
            p = jnp.exp(s - m)
            m_sc[...] = jnp.broadcast_to(m, (T, LANES))
        else:
            p = s * reld_ref[0]
        acc_sc[...] = jnp.dot(p.astype(BF16), vb, preferred_element_type=F32)

        def scores(j):
            kb, vb = kv_block(j)
            return _dot_nt(qv, kb), vb

        def update(j, s, vb):
            if softmax:
                m_prev = m_sc[...]
                m_next = jnp.maximum(m_prev, jnp.max(s, axis=-1, keepdims=True))
                alpha = jnp.exp(m_prev - m_next)
                p = jnp.exp(s - jnp.tile(m_next, (1, rep)))
                m_sc[...] = m_next
                acc_sc[...] = acc_sc[...] * jnp.tile(alpha, (1, vw // LANES)) + jnp.dot(
                    p.astype(BF16), vb, preferred_element_type=F32)
            else:
                fac = jnp.exp(lg_ref[0] * ((i - j) * T).astype(F32))
                p = s * (rel_ref[0] * jnp.tile(fac, (1, rep)))
                acc_sc[...] += jnp.dot(p.astype(BF16), vb, preferred_element_type=F32)

        def pair(jj, carry):
            first, second = scores(2 * jj), scores(2 * jj + 1)
            update(2 * jj, *first)
            update(2 * jj + 1, *second)
            return carry

        lax.fori_loop(0, i // 2, pair, 0)

        @pl.when(i % 2 == 1)
        def _():
            update(i - 1, *scores(i - 1))

        if softmax:
            l = acc_sc[:, dv:]
            o_ref[...] = acc_sc[:, :dv] / l
            lse_ref[...] = m_sc[...] + jnp.log(l)
        else:
            o_ref[...] = acc_sc[...]

    in_specs = [pl.BlockSpec((T, dk), lambda h, i: (i, h)), pl.BlockSpec((S, dk), lambda h, i: (0, h)),
                pl.BlockSpec((S, vw), lambda h, i: (0, h))]
    o_spec = pl.BlockSpec((T, dv), lambda h, i: (i, h))
    if softmax:
        return _call(body, name, [_sds((S, heads * dv), F32), _sds((S, heads * LANES), F32)], (heads, nq), in_specs,
                     [o_spec, pl.BlockSpec((T, LANES), lambda h, i: (i, h))],
                     scratch=[pltpu.VMEM((T, LANES), F32), pltpu.VMEM((T, vw), F32)],
                     sem=("parallel", "arbitrary"), side=side)(q, k, v)
    lgrow, rel, reld = tables
    in_specs += [pl.BlockSpec((1, 1, LANES), lambda h, i: (h, 0, 0)), pl.BlockSpec((1, T, T), lambda h, i: (h, 0, 0)),
                 pl.BlockSpec((1, T, T), lambda h, i: (h, 0, 0))]
    return _call(body, name, _sds((S, heads * dv), F32), (heads, nq), in_specs, o_spec,
                 scratch=[pltpu.VMEM((T, dv), F32)], sem=("parallel", "arbitrary"), side=side)(q, k, v, lgrow, rel, reld)


def _attn_bwd(q, k, v, do, heads, dk, dv, softmax, name, o=None, lse=None, tables=None, side=None):
    S = q.shape[0]
    T = ATT_BLOCK
    nq = S // T
    rep = T // LANES

    def body(*refs):
        if softmax:
            q_ref, k_ref, v_ref, do_ref, o_ref, lse_ref, dq_ref, dk_ref, dv_ref, dq_sc = refs
        else:
            q_ref, k_ref, v_ref, do_ref, lg_ref, rel_ref, reld_ref, dq_ref, dk_ref, dv_ref, dq_sc = refs
        i = pl.program_id(1)

        @pl.when(i == 0)
        def _():
            dk_ref[...] = jnp.zeros_like(dk_ref)
            dv_ref[...] = jnp.zeros_like(dv_ref)

        qv = q_ref[...]
        dof = do_ref[...].astype(F32)
        dov = dof.astype(BF16)
        if softmax:
            delta = jnp.sum(dof * o_ref[...], axis=-1, keepdims=True)
            lse_t = jnp.tile(lse_ref[...], (1, rep))
        dq_sc[...] = jnp.zeros_like(dq_sc)

        def products(j):
            rows = pl.ds(pl.multiple_of(j * T, T), T)
            kb = k_ref[rows, :]
            return rows, kb, _dot_nt(qv, kb), _dot_nt(dov, v_ref[rows, :])

        def block(j, diagonal, rows, kb, s, dp):
            if softmax:
                if diagonal:
                    s = jnp.where(_chunk_mask(T), s, NEG)
                p = jnp.exp(s - lse_t)
                ds = p * (dp - delta)
            else:
                if diagonal:
                    dec = reld_ref[0]
                else:
                    fac = jnp.exp(lg_ref[0] * ((i - j) * T).astype(F32))
                    dec = rel_ref[0] * jnp.tile(fac, (1, rep))
                p = s * dec
                ds = dp * dec
            dsb = ds.astype(BF16)
            dv_ref[rows, :] += _dot_tn(p.astype(BF16), dov)
            dk_ref[rows, :] += _dot_tn(dsb, qv)
            dq_sc[...] += jnp.dot(dsb, kb, preferred_element_type=F32)

        block(i, True, *products(i))

        def pair(jj, carry):
            first, second = products(2 * jj), products(2 * jj + 1)
            block(2 * jj, False, *first)
            block(2 * jj + 1, False, *second)
            return carry

        lax.fori_loop(0, i // 2, pair, 0)

        @pl.when(i % 2 == 1)
        def _():
            block(i - 1, False, *products(i - 1))

        dq_ref[...] = dq_sc[...]

    qspec = pl.BlockSpec((T, dk), lambda h, i: (i, h))
    kspec = pl.BlockSpec((S, dk), lambda h, i: (0, h))
    vspec = pl.BlockSpec((S, dv), lambda h, i: (0, h))
    dospec = pl.BlockSpec((T, dv), lambda h, i: (i, h))
    in_specs = [qspec, kspec, vspec, dospec]
    args = [q, k, v, do]
    if softmax:
        in_specs[2] = pl.BlockSpec((S, dv), lambda h, i: (0, 2 * h))
        in_specs += [dospec, pl.BlockSpec((T, LANES), lambda h, i: (i, h))]
        args += [o, lse]
    else:
        in_specs += [pl.BlockSpec((1, 1, LANES), lambda h, i: (h, 0, 0)),
                     pl.BlockSpec((1, T, T), lambda h, i: (h, 0, 0)), pl.BlockSpec((1, T, T), lambda h, i: (h, 0, 0))]
        args += list(tables)
    return _call(body, name, [_sds((S, heads * dk), F32), _sds((S, heads * dk), F32), _sds((S, heads * dv), F32)],
                 (heads, nq), in_specs, [qspec, kspec, vspec], scratch=[pltpu.VMEM((T, dk), F32)],
                 sem=("parallel", "arbitrary"), side=side)(*args)


def _rope_tables(pos):
    def tables(dim):
        inv_freq = ROPE_THETA ** (-jnp.arange(0, dim, 2, dtype=F32) / dim)
        ang = pos.astype(F32)[:, None] * inv_freq
        return jnp.cos(ang), jnp.sin(ang)

    cm, sm = tables(ROPE)
    S = pos.shape[0]
    z32, z64 = jnp.zeros((S, 32), F32), jnp.zeros((S, 64), F32)
    cr, sr = tables(RET_DK)
    return (jnp.concatenate([cm, cm, z64], 1), jnp.concatenate([z32, sm, z64], 1),
            jnp.concatenate([-sm, z32, z64], 1), cr, sr)


def _row(v):
    return v.reshape(1, -1).astype(F32)


def _local_step(x, pos, target, pipe, P):
    tabs = _rope_tables(pos)
    dtabs = _decay_tables(ATT_BLOCK)
    xf, xb = _ln_fwd([x], [1.0], _row(P["ln_in_g"]), _row(P["ln_in_b"]), "ln_in", False)
    pipe.gather_first()
    saved = []
    for l in range(DEPTH):
        w = functools.partial(pipe.weight, l)
        t = f"_l{l}"
        h = pipe.run(_matmul, "mm_h" + t, xb, w("w_in"))
        qn, kvn, kr, rq, rk, rv = _prep1(h, tabs, _row(P["q_norm_g"][l]), _row(P["kv_norm_g"][l]), "prep1" + t)
        q = _matmul(qn, w("w_uq"), "mm_q" + t)
        kv = _matmul(kvn, w("w_ukv"), "mm_kv" + t)
        qm, km, vm = _prep2(q, kv, kr, tabs, "prep2" + t)
        a, lse = pipe.run(_attn_fwd, "mla_fwd" + t, qm, km, vm, MLA_HEADS, HEAD_PAD, VDIM, True)
        o = pipe.run(_attn_fwd, "ret_fwd" + t, rq, rk, rv, RET_HEADS, RET_DK, RET_DV, False, tables=dtabs)
        mixin = _gn_gate(a, o, h, _row(P["ret_gn_g"][l]), _row(P["ret_gn_b"][l]), "gn_gate" + t)
        mix = _matmul(mixin, w("w_out"), "mm_mix" + t)
        z1, x1f, x1b = _ln_fwd([xf, mix], [ALPHA, 1.0], _row(P["ln1_g"][l]), _row(P["ln1_b"][l]), "ln1" + t, True)
        gu, act = pipe.run(_matmul_swiglu, "mm_gu" + t, x1b, w("w_gu"))
        f = pipe.run(_matmul, "mm_down" + t, act, w("w_down"))
        z2, x2f, x2b = _ln_fwd([x1f, f], [ALPHA, 1.0], _row(P["ln2_g"][l]), _row(P["ln2_b"][l]), "ln2" + t, True)
        saved.append(dict(xb=xb, h=h, qn=qn, kvn=kvn, rq=rq, rk=rk, rv=rv, qm=qm, km=km, vm=vm, a=a, lse=lse, o=o,
                          mixin=mixin, z1=z1, x1b=x1b, gu=gu, act=act, z2=z2))
        xf, xb = x2f, x2b

    dy, sqerr = _loss_head(xf, target, "loss_head")
    dP = {}
    dys, coefs = [dy], [1.0]
    for l in reversed(range(DEPTH)):
        w, sv = functools.partial(pipe.weight, l), saved[l]
        t = f"_l{l}"
        dz2, dz2b, dg, db = _ln_bwd(dys, coefs, sv["z2"], _row(P["ln2_g"][l]), "ln2_bwd" + t)
        dP[("ln2_g", l)], dP[("ln2_b", l)] = dg, db
        pipe.reduce(l, w_down=pipe.run(_matmul, "mm_dw_down" + t, sv["act"], dz2b, ta=True, out_dtype=BF16))
        dact = pipe.run(_matmul, "mm_dact" + t, dz2b, w("w_down"), tb=True)
        dgu = _swiglu_bwd(sv["gu"], dact, "swiglu_bwd" + t)
        pipe.reduce(l, w_gu=pipe.run(_matmul, "mm_dw_gu" + t, sv["x1b"], dgu, ta=True, out_dtype=BF16))
        dx1 = pipe.run(_matmul, "mm_dx1" + t, dgu, w("w_gu"), tb=True)
        dz1, dz1b, dg, db = _ln_bwd([dz2, dx1], [ALPHA, 1.0], sv["z1"], _row(P["ln1_g"][l]), "ln1_bwd" + t)
        dP[("ln1_g", l)], dP[("ln1_b", l)] = dg, db
        pipe.reduce(l, w_out=_matmul(sv["mixin"], dz1b, "mm_dw_out" + t, ta=True, out_dtype=BF16))
        dmixin = pipe.run(_matmul, "mm_dmixin" + t, dz1b, w("w_out"), tb=True)
        do, drg, dgg, dgb = _gn_gate_bwd(dmixin, sv["o"], sv["h"], _row(P["ret_gn_g"][l]), _row(P["ret_gn_b"][l]),
                                         "gn_gate_bwd" + t)
        dP[("ret_gn_g", l)], dP[("ret_gn_b", l)] = dgg, dgb
        drq, drk, drv = pipe.run(_attn_bwd, "ret_bwd" + t, sv["rq"], sv["rk"], sv["rv"], do, RET_HEADS, RET_DK, RET_DV,
                                 False, tables=dtabs)
        dqm, dkm, dvm = pipe.run(_attn_bwd, "mla_bwd" + t, sv["qm"], sv["km"], sv["vm"], dmixin, MLA_HEADS, HEAD_PAD,
                                 VDIM, True, o=sv["a"], lse=sv["lse"])
        dq, dkv, dkr = _prep2_bwd(dqm, dkm, dvm, tabs, "prep2_bwd" + t)
        g_uq = _matmul(sv["qn"], dq, "mm_dw_uq" + t, ta=True, out_dtype=BF16)
        dqn = _matmul(dq, w("w_uq"), "mm_dqn" + t, tb=True)
        g_ukv = _matmul(sv["kvn"], dkv, "mm_dw_ukv" + t, ta=True, out_dtype=BF16)
        dkvn = _matmul(dkv, w("w_ukv"), "mm_dkvn" + t, tb=True)
        dh, dqg, dkvg = _prep1_bwd(dqn, dkvn, dkr, drq, drk, drv, drg, sv["h"], tabs, _row(P["q_norm_g"][l]),
                                   _row(P["kv_norm_g"][l]), "prep1_bwd" + t)
        dP[("q_norm_g", l)], dP[("kv_norm_g", l)] = dqg, dkvg
        pipe.reduce(l, w_uq=g_uq, w_ukv=g_ukv,
                    w_in=pipe.run(_matmul, "mm_dw_in" + t, sv["xb"], dh, ta=True, out_dtype=BF16))
        dxl = pipe.run(_matmul, "mm_dxl" + t, dh, w("w_in"), tb=True)
        dys, coefs = [dz1, dxl], [ALPHA, 1.0]
    grad_x, _, dg, db = _ln_bwd(dys, coefs, x, _row(P["ln_in_g"]), "ln_in_bwd")
    dP[("ln_in_g", None)], dP[("ln_in_b", None)] = dg, db
    return sqerr, grad_x, dP


INTERNAL_OF = {"w_in": ("w_in",), "w_uq": ("w_uq",), "w_ukv": ("w_ukv",), "w_out": ("w_out",),
               "w_gu": ("w_gate", "w_up"), "w_down": ("w_down",)}
ROW_PIECES = {"w_up": 1024, "w_in": 1024}


def _internal_weight(name, *blocks):
    cat = lambda parts: jnp.concatenate(parts, axis=1)
    cols = lambda b: cat([b[j] for j in range(N_CHIPS)])
    b = blocks[0]
    if name in ("w_out", "w_down"):
        return b.reshape(-1, b.shape[-1])
    if name == "w_gu":
        return cat([blk[j] for j in range(N_CHIPS) for blk in blocks])
    if name == "w_in":
        return cat([b[0][:, :MLA_IN_USED], jnp.zeros((D_MODEL, MLA_IN - MLA_IN_USED), BF16), b[0][:, MLA_IN_USED:]]
                   + [b[j] for j in range(1, N_CHIPS)])
    if name == "w_uq":
        uq, hw = cols(b), NOPE + ROPE
        pad = jnp.zeros((Q_LORA, HEAD_PAD - hw), BF16)
        return cat([p for h in range(MLA_HEADS) for p in (uq[:, h * hw:(h + 1) * hw], pad)])
    ukv = cols(b)
    return cat([ukv[:, 256 * h:256 * h + NOPE] for h in range(MLA_HEADS)]
               + [ukv[:, 256 * h + NOPE:256 * h + 256] for h in range(MLA_HEADS)])


def _grad_shards(name, g):
    cat = lambda parts: jnp.concatenate(parts, axis=1)
    if name in ("w_out", "w_down"):
        return {name: g.reshape(N_CHIPS, -1, g.shape[-1])}
    if name == "w_gu":
        return {"w_gate": _ColBlocks(g, 0), "w_up": _ColBlocks(g, 1)}
    if name == "w_in":
        ci, shift = BIG_SHARD["w_in"][1], MLA_IN - MLA_IN_USED
        return {name: [cat([g[:, :MLA_IN_USED], g[:, MLA_IN:ci + shift]])]
                + [g[:, ci * j + shift:ci * (j + 1) + shift] for j in range(1, N_CHIPS)]}
    if name == "w_uq":
        cq = NOPE + ROPE
        return {name: [cat([g[:, HEAD_PAD * h:HEAD_PAD * h + cq] for h in (2 * j, 2 * j + 1)]) for j in range(N_CHIPS)]}
    return {name: [cat([g[:, o + NOPE * h:o + NOPE * (h + 1)] for h in (2 * j, 2 * j + 1) for o in (0, MLA_HEADS * NOPE)])
                   for j in range(N_CHIPS)]}


def _small_layout(P):
    out, at = {}, 0
    for n in SMALL:
        out[n] = (at, P[n].size)
        at += P[n].size
    return out, at


def _flatten_small(P, last):
    v = jnp.concatenate([P[n].reshape(-1).astype(F32) for n in SMALL] + [last.reshape(-1).astype(F32)])
    return jnp.pad(v, (0, SMALL_ROWS * FLAT_W - v.size)).reshape(SMALL_ROWS, FLAT_W)


def _place():
    return lax.axis_index("x"), lax.axis_index("y"), lax.axis_index("c")


def _other_chips(x, y):
    return [(1 - x, y), (x, 1 - y), (1 - x, 1 - y)]


def _rcopy(src, dst, ssem, rsem, dev):
    return pltpu.make_async_remote_copy(src_ref=src, dst_ref=dst, send_sem=ssem, recv_sem=rsem, device_id=dev,
                                        device_id_type=MESH)


def _comm_call(body, name, out_shape, n_in, scratch):
    many = isinstance(out_shape, (list, tuple))
    return pl.pallas_call(body, name=name, out_shape=out_shape, in_specs=[HBM] * n_in,
                          out_specs=[HBM] * len(out_shape) if many else HBM, scratch_shapes=scratch)


def _half(ref, which):
    rows = ref.shape[0] // 2
    return ref.at[pl.ds(pl.multiple_of(which * rows, 16), rows)]


def _dma_sems(n):
    return pltpu.SemaphoreType.DMA((n,))


def _allgather_side(ws):
    k = len(ws)

    def peers():
        x, y, c = _place()
        return c, 2 * x + y, (x, y, 1 - c), [(n, t, cx, cy) for n in range(k) for t, (cx, cy) in enumerate(_other_chips(x, y))]

    def outgoing(w_refs, g_refs, sems):
        ssem, rsem, _, _, ossem, orsem = sems
        c, j, sib, nt = peers()
        owns = [_rcopy(w_refs[n], g_refs[n].at[j], ossem.at[n], orsem.at[n], sib) for n in range(k)]
        sends = [_rcopy(_half(w_refs[n], c), _half(g_refs[n].at[j], c), ssem.at[3 * n + t], rsem.at[3 * n + t],
                        (cx, cy, c)) for n, t, cx, cy in nt]
        return owns, sends

    def incoming(g_refs, sems):
        ssem, rsem, fssem, frsem, _, _ = sems
        c, _, sib, nt = peers()
        landed, passed, relayed = [], [], []
        for n, t, cx, cy in nt:
            mine, other = (_half(g_refs[n].at[2 * cx + cy], h) for h in (c, 1 - c))
            landed.append(_rcopy(mine, mine, ssem.at[3 * n + t], rsem.at[3 * n + t], (cx, cy, c)))
            passed.append(_rcopy(mine, mine, fssem.at[3 * n + t], frsem.at[3 * n + t], sib))
            relayed.append(_rcopy(other, other, fssem.at[3 * n + t], frsem.at[3 * n + t], sib))
        return landed, passed, relayed

    def start(w_refs, g_refs, sems):
        owns, sends = outgoing(w_refs, g_refs, sems)
        for cp in sends + owns:
            cp.start()

    def finish(w_refs, g_refs, sems):
        owns, sends = outgoing(w_refs, g_refs, sems)
        landed, passed, relayed = incoming(g_refs, sems)
        for got, on in zip(landed, passed):
            got.wait_recv()
            on.start()
        for cp in relayed:
            cp.wait_recv()
        for cp in owns:
            cp.wait()
        for cp in sends + passed:
            cp.wait_send()

    return _Side(list(ws), [_sds((N_CHIPS,) + w.shape, w.dtype) for w in ws],
                 [_dma_sems(3 * k)] * 4 + [_dma_sems(k)] * 2, start, finish)


def _exchange_side(parts):
    k = len(parts)

    def copies(p_refs, rcv_refs, sems):
        ssem, rsem = sems
        x, y, c = _place()
        return [_rcopy(p_refs[n].at[2 * cx + cy], rcv_refs[n].at[t], ssem.at[3 * n + t], rsem.at[3 * n + t], (cx, cy, c))
                for n in range(k) for t, (cx, cy) in enumerate(_other_chips(x, y))]

    def start(p_refs, rcv_refs, sems):
        for cp in copies(p_refs, rcv_refs, sems):
            cp.start()

    def finish(p_refs, rcv_refs, sems):
        for cp in copies(p_refs, rcv_refs, sems):
            cp.wait()

    return _Side(list(parts), [_sds((3,) + p.shape[1:], p.dtype) for p in parts], [_dma_sems(3 * k)] * 2, start, finish)


def _run_side(side, name):
    k_in, k_out = len(side.arrays), len(side.out_shape)

    def body(*refs):
        parts = refs[:k_in], refs[k_in:k_in + k_out], refs[k_in + k_out:]
        side.start(*parts)
        side.finish(*parts)

    return _comm_call(body, name, list(side.out_shape), k_in, list(side.scratch))(*side.arrays)


def _sibling_side(arrays, out_shape, n_copies, copies):
    def start(in_refs, out_refs, sems):
        for cp in copies(in_refs, out_refs, sems):
            cp.start()

    def finish(in_refs, out_refs, sems):
        for cp in copies(in_refs, out_refs, sems):
            cp.wait()

    return _Side(list(arrays), out_shape, [_dma_sems(n_copies)] * 2, start, finish)


class _ColBlocks:
    def __init__(self, array, off):
        self.array, self.off, self.dtype = array, off, array.dtype
        self.shape = (N_CHIPS, array.shape[0], GU_BLOCK)

    def block(self, ref, jj):
        return ref.at[:, pl.ds((2 * jj + self.off) * GU_BLOCK, GU_BLOCK)]


def _swap_side(gds):
    k = len(gds)

    def copies(gd_refs, out_refs, sems):
        ssem, rsem = sems
        x, y, c = _place()
        blocks = [[g.block(gd_refs[n], jj) if isinstance(g, _ColBlocks) else gd_refs[n].at[jj] for jj in range(N_CHIPS)]
                  for n, g in enumerate(gds)]
        return [_rcopy(_half(blocks[n][jj], 1 - c), out_refs[n].at[jj], ssem.at[N_CHIPS * n + jj],
                       rsem.at[N_CHIPS * n + jj], (x, y, 1 - c)) for n in range(k) for jj in range(N_CHIPS)]

    return _sibling_side([g.array if isinstance(g, _ColBlocks) else g for g in gds],
                         [_sds((N_CHIPS, g.shape[1] // 2, g.shape[2]), g.dtype) for g in gds], N_CHIPS * k, copies)


def _share_side(reds):
    k = len(reds)

    def copies(r_refs, out_refs, sems):
        ssem, rsem = sems
        x, y, c = _place()
        return [_rcopy(r_refs[n], out_refs[n], ssem.at[n], rsem.at[n], (x, y, 1 - c)) for n in range(k)]

    return _sibling_side(reds, [_sds(r.shape, r.dtype) for r in reds], k, copies)


def _join_sides(sides):
    if len(sides) == 1:
        return sides[0]
    cuts = [(len(s.arrays), len(s.out_shape), len(s.scratch)) for s in sides]

    def each(method, in_refs, out_refs, sems):
        a = o = m = 0
        for s, (ka, ko, km) in zip(sides, cuts):
            getattr(s, method)(in_refs[a:a + ka], out_refs[o:o + ko], sems[m:m + km])
            a, o, m = a + ka, o + ko, m + km

    return _Side([x for s in sides for x in s.arrays], [x for s in sides for x in s.out_shape],
                 [x for s in sides for x in s.scratch], functools.partial(each, "start"), functools.partial(each, "finish"))


def _allreduce_small(small):
    def body(s_ref, all_ref, sssem, srsem, lsem):
        x, y, c = _place()
        me = 4 * x + 2 * y + c
        own = pltpu.make_async_copy(s_ref, all_ref.at[me], lsem)
        own.start()
        cps = []
        for r in range(1, 8):
            fx, fy, fc = (r >> 2) & 1, (r >> 1) & 1, r & 1
            px, py, pc = (1 - x if fx else x, 1 - y if fy else y, 1 - c if fc else c)
            peer = 4 * px + 2 * py + pc
            send = _rcopy(s_ref, all_ref.at[me], sssem.at[r - 1], srsem.at[me], (px, py, pc))
            send.start()
            cps.append((send, _rcopy(s_ref, all_ref.at[peer], sssem.at[r - 1], srsem.at[peer], (px, py, pc))))
        for send, recv in cps:
            send.wait_send()
            recv.wait_recv()
        own.wait()

    return _comm_call(body, "allreduce_small", [_sds((8,) + small.shape, small.dtype)], 1,
                      [pltpu.SemaphoreType.DMA((7,)), pltpu.SemaphoreType.DMA((8,)), pltpu.SemaphoreType.DMA(())])(small)[0]


def _add_pair(gd, got, c, name):
    _, R, W = got.shape
    tm = _pick(R, (512, 256, 128, 64))
    nb = R // tm

    def body(c_ref, a_ref, b_ref, o_ref):
        o_ref[...] = (a_ref[...].astype(F32) + b_ref[...].astype(F32)).astype(o_ref.dtype)

    if isinstance(gd, _ColBlocks):
        off = gd.off
        own = pl.BlockSpec((tm, W), lambda j, i, c_ref: (c_ref[0] * nb + i, 2 * j + off))
        gd = gd.array
    else:
        own = pl.BlockSpec((None, tm, W), lambda j, i, c_ref: (j, c_ref[0] * nb + i, 0))
    grid_spec = pltpu.PrefetchScalarGridSpec(
        num_scalar_prefetch=1, grid=(N_CHIPS, nb),
        in_specs=[own, pl.BlockSpec((None, tm, W), lambda j, i, c_ref: (j, i, 0))],
        out_specs=pl.BlockSpec((None, tm, W), lambda j, i, c_ref: (j, i, 0)))
    return pl.pallas_call(body, name=name, grid_spec=grid_spec, out_shape=_sds((N_CHIPS, R, W), gd.dtype),
                          compiler_params=pltpu.CompilerParams(dimension_semantics=("parallel", "parallel"),
                                                               vmem_limit_bytes=VMEM_LIMIT))(c, gd, got)


def _add_chips(part, rcv, j, name):
    _, R, W = part.shape
    tm = _pick(R, (512, 256, 128, 64))

    def body(j_ref, p_ref, r0_ref, r1_ref, r2_ref, o_ref):
        o_ref[...] = ((p_ref[...].astype(F32) + r0_ref[...].astype(F32)) + r1_ref[...].astype(F32)) + r2_ref[...].astype(F32)

    def slot(t):
        return pl.BlockSpec((None, tm, W), lambda i, j_ref: (t, i, 0))

    grid_spec = pltpu.PrefetchScalarGridSpec(
        num_scalar_prefetch=1, grid=(R // tm,),
        in_specs=[pl.BlockSpec((None, tm, W), lambda i, j_ref: (j_ref[0], i, 0)), slot(0), slot(1), slot(2)],
        out_specs=pl.BlockSpec((tm, W), lambda i, j_ref: (i, 0)))
    return pl.pallas_call(body, name=name, grid_spec=grid_spec, out_shape=_sds((R, W), F32),
                          compiler_params=pltpu.CompilerParams(dimension_semantics=("parallel",),
                                                               vmem_limit_bytes=VMEM_LIMIT))(j, part, rcv, rcv, rcv)


def _sum_small(allsmall):
    _, R, W = allsmall.shape

    def body(a_ref, o_ref):
        acc = a_ref[0]
        for d in range(1, 8):
            acc = acc + a_ref[d]
        o_ref[...] = acc

    return _call(body, "sum_small", _sds((R, W), F32), (1,), [_whole((8, R, W))], _whole((R, W)),
                 sem=("arbitrary",))(allsmall)


def _adamw(w, g, m, v, name):
    R, C = w.shape
    tm = _pick(R, (256, 128, 64, 32, 8))

    def body(w_ref, g_ref, m_ref, v_ref, d_ref, mo_ref, vo_ref):
        gv = g_ref[...]
        mn = ADAM_B1 * m_ref[...] + (1.0 - ADAM_B1) * gv
        vn = ADAM_B2 * v_ref[...] + (1.0 - ADAM_B2) * (gv * gv)
        m_hat = mn / (1.0 - ADAM_B1 ** ADAM_STEP)
        v_hat = vn / (1.0 - ADAM_B2 ** ADAM_STEP)
        d_ref[...] = -ADAM_LR * (m_hat / (jnp.sqrt(v_hat) + ADAM_EPS) + ADAM_WD * w_ref[...])
        mo_ref[...] = mn
        vo_ref[...] = vn

    spec = _rows(tm, C)
    return _call(body, name, [_sds((R, C), F32)] * 3, (R // tm,), [spec] * 4, [spec] * 3, sem=("parallel",))(w, g, m, v)


def _adamw_layer(c, w, m, v, mine, other, l, prev, name):
    _, R, C = w.shape
    half = R // 2
    tm = _pick(half, (256, 128, 64))
    nbh = half // tm

    def body(c_ref, w_ref, m_ref, v_ref, a_ref, b_ref, *rest):
        g_ref, d_ref, mo_ref, vo_ref = rest[-4:]
        gv = jnp.where(pl.program_id(0) // nbh == c_ref[0], a_ref[...], b_ref[...])
        mn = ADAM_B1 * m_ref[...] + (1.0 - ADAM_B1) * gv
        vn = ADAM_B2 * v_ref[...] + (1.0 - ADAM_B2) * (gv * gv)
        m_hat = mn / (1.0 - ADAM_B1 ** ADAM_STEP)
        v_hat = vn / (1.0 - ADAM_B2 ** ADAM_STEP)
        g_ref[...] = gv
        d_ref[...] = -ADAM_LR * (m_hat / (jnp.sqrt(v_hat) + ADAM_EPS) + ADAM_WD * w_ref[...])
        mo_ref[...] = mn
        vo_ref[...] = vn

    layer = pl.BlockSpec((None, tm, C), lambda i, c_ref: (l, i, 0))
    halfspec = pl.BlockSpec((tm, C), lambda i, c_ref: (i % nbh, 0))
    n_prev = 0 if prev is None else 4
    grid_spec = pltpu.PrefetchScalarGridSpec(
        num_scalar_prefetch=1, grid=(R // tm,),
        in_specs=[layer] * 3 + [halfspec] * 2 + [pl.BlockSpec(memory_space=pl.ANY)] * n_prev,
        out_specs=[layer] * 4)
    return pl.pallas_call(body, name=name, grid_spec=grid_spec, out_shape=[_sds(w.shape, F32)] * 4,
                          input_output_aliases={6 + k: k for k in range(n_prev)},
                          compiler_params=pltpu.CompilerParams(dimension_semantics=("parallel",),
                                                               vmem_limit_bytes=VMEM_LIMIT))(
        c, w, m, v, mine, other, *(prev or ()))


FIRST_GATHER = ("w_in@a", "w_in@b", "w_uq", "w_ukv")
G_DOWN, G_GU, G_OUT, G_IN = ("w_down",), ("w_gate", "w_up"), ("w_out",), ("w_uq", "w_ukv", "w_in")


def _backward_jobs(l):
    t = f"_l{l}"
    return {"mm_dact" + t: [("swap", l, G_DOWN)], "mm_dw_gu" + t: [("exchange", l, G_DOWN)],
            "mm_dx1" + t: [("swap", l, G_GU), ("share", l, G_DOWN)], "mm_dmixin" + t: [("swap", l, G_OUT)],
            "ret_bwd" + t: [("exchange", l, ("w_gate",))],
            "mla_bwd" + t: [("exchange", l, ("w_up", "w_out")), ("share", l, ("w_gate",))],
            "mm_dw_in" + t: [("share", l, ("w_up", "w_out"))]}


JOBS = {
    "mm_h_l0": [("gather", 0, ("w_up@a",))], "mla_fwd_l0": [("gather", 0, ("w_gate", "w_out"))],
    "ret_fwd_l0": [("gather", 0, ("w_up@b",))],
    "mm_gu_l0": [("gather", 0, ("w_down",)), ("gather", 1, ("w_uq", "w_ukv", "w_in@a"))],
    "mm_down_l0": [("gather", 1, ("w_in@b",))], "mm_h_l1": [("gather", 1, ("w_up@a",))],
    "mla_fwd_l1": [("gather", 1, ("w_gate", "w_out"))], "ret_fwd_l1": [("gather", 1, ("w_up@b",))],
    "mm_gu_l1": [("gather", 1, ("w_down",))],
    **_backward_jobs(1), **_backward_jobs(0),
    "mm_dxl_l1": [("swap", 1, G_IN)],
    "mm_dx1_l0": [("swap", 0, G_GU), ("share", 0, G_DOWN), ("exchange", 1, G_IN)],
    "ret_bwd_l0": [("exchange", 0, ("w_gate",)), ("share", 1, G_IN)], "mm_dxl_l0": [("exchange", 0, G_IN)]}
PLANNED = {job for jobs in JOBS.values() for job in jobs}


class _Pipeline:
    def __init__(self, own, Wt, Mo, Vo, core, chip):
        self.own, self.Wt, self.Mo, self.Vo, self.core, self.chip = own, Wt, Mo, Vo, core, chip
        self.blocks, self.whole, self.gds, self.parts, self.reds = {}, {}, {}, {}, {}
        self.results = {n: None for n in BIG}

    def gather_first(self):
        job = ("gather", 0, FIRST_GATHER)
        self._done(*job, _run_side(self._side(*job), "allgather_first"))

    def _gathered(self, l, n):
        if n in ROW_PIECES:
            return jnp.concatenate([self.blocks[(l, n + "@a")], self.blocks[(l, n + "@b")]], axis=1)
        return self.blocks[(l, n)]

    def weight(self, l, name):
        if (l, name) not in self.whole:
            self.whole[(l, name)] = _internal_weight(name, *[self._gathered(l, n) for n in INTERNAL_OF[name]])
        return self.whole[(l, name)]

    def run(self, fn, name, *args, **kw):
        jobs = JOBS.get(name, ())
        if not jobs:
            return fn(*args, name=name, **kw)
        sides = [self._side(*job) for job in jobs]
        out, res = fn(*args, name=name, side=_join_sides(sides), **kw)
        for job, side in zip(jobs, sides):
            k = len(side.out_shape)
            self._done(*job, res[:k])
            res = res[k:]
        return out

    def reduce(self, l, **grads):
        shards = {}
        for name, g in grads.items():
            shards.update(_grad_shards(name, g))
        for n, sh in shards.items():
            self.gds[(l, n)] = sh if hasattr(sh, "shape") else jnp.stack(sh)
        self._alone("swap", l, tuple(shards))

    def _alone(self, kind, l, names):
        if (kind, l, names) not in PLANNED:
            self._done(kind, l, names, _run_side(self._side(kind, l, names), f"{kind}_{names[0]}_l{l}"))

    def _side(self, kind, l, names):
        if kind == "gather":
            return _allgather_side([self.own[l][n] for n in names])
        store = {"swap": self.gds, "exchange": self.parts, "share": self.reds}[kind]
        make = {"swap": _swap_side, "exchange": _exchange_side, "share": _share_side}[kind]
        return make([store[(l, n)] for n in names])

    def _done(self, kind, l, names, res):
        for n, r in zip(names, res):
            if kind == "gather":
                self.blocks[(l, n)] = r
            elif kind == "swap":
                self.parts[(l, n)] = _add_pair(self.gds[(l, n)], r, self.core, f"add_pair_{n}_l{l}")
            elif kind == "exchange":
                self.reds[(l, n)] = _add_chips(self.parts[(l, n)], r, self.chip, f"add_chips_{n}_l{l}")
            else:
                self.results[n] = _adamw_layer(self.core, self.Wt[n], self.Mo[n], self.Vo[n], self.reds[(l, n)], r, l,
                                               self.results[n], f"adamw_{n}_l{l}")
        if kind == "exchange":
            self._alone("share", l, names)


def kernel(x, positions, ln_in_g, ln_in_b, w_in, q_norm_g, kv_norm_g, w_uq, w_ukv, ret_gn_g, ret_gn_b, w_out, ln1_g, ln1_b, w_gate, w_up, w_down, ln2_g, ln2_b, loss_target, m_ln_in_g, m_ln_in_b, m_w_in, m_q_norm_g, m_kv_norm_g, m_w_uq, m_w_ukv, m_ret_gn_g, m_ret_gn_b, m_w_out, m_ln1_g, m_ln1_b, m_w_gate, m_w_up, m_w_down, m_ln2_g, m_ln2_b, v_ln_in_g, v_ln_in_b, v_w_in, v_q_norm_g, v_kv_norm_g, v_w_uq, v_w_ukv, v_ret_gn_g, v_ret_gn_b, v_w_out, v_ln1_g, v_ln1_b, v_w_gate, v_w_up, v_w_down, v_ln2_g, v_ln2_b):
    given = dict(locals())
    Wt = {n: given[n] for n in WEIGHTS}
    Mo = {n: given["m_" + n] for n in WEIGHTS}
    Vo = {n: given["v_" + n] for n in WEIGHTS}
    cx, cy, cc = _place()
    chip = (2 * cx + cy).astype(jnp.int32)
    core = cc.astype(jnp.int32)

    own = [{n: Wt[n][l].astype(BF16) for n in BIG} for l in range(DEPTH)]
    for shard in own:
        for n, at in ROW_PIECES.items():
            shard[n + "@a"], shard[n + "@b"] = shard[n][:at], shard[n][at:]
    pipe = _Pipeline(own, Wt, Mo, Vo, core.reshape(1), chip.reshape(1))
    sqerr, grad_x, dP = _local_step(x[0], positions[0], loss_target[0], pipe, Wt)
    results = pipe.results

    small_g = {n: (dP[(n, None)] if Wt[n].ndim == 1 else jnp.stack([dP[(n, l)] for l in range(DEPTH)])) for n in SMALL}
    local_loss = 0.5 * jnp.sum(sqerr) / D_MODEL
    small_sum = _sum_small(_allreduce_small(_flatten_small(small_g, local_loss))).reshape(-1)
    layout, n_small = _small_layout(Wt)
    loss = small_sum[n_small]

    grads, deltas, new_m, new_v = {}, {}, {}, {}
    for n in BIG:
        grads[n], deltas[n], new_m[n], new_v[n] = results[n]
    zero = jnp.zeros((), F32)
    d, mn, vn = _adamw(_flatten_small(Wt, zero), small_sum.reshape(SMALL_ROWS, FLAT_W), _flatten_small(Mo, zero),
                       _flatten_small(Vo, zero), "adamw_small")
    for n in SMALL:
        at, size = layout[n]
        pick = lambda a: a.reshape(-1)[at:at + size].reshape(Wt[n].shape)
        grads[n], deltas[n], new_m[n], new_v[n] = pick(small_sum), pick(d), pick(mn), pick(vn)

    return (loss, grad_x[None], *[grads[n] for n in WEIGHTS], *[deltas[n] for n in WEIGHTS],
            *[new_m[n] for n in WEIGHTS], *[new_v[n] for n in WEIGHTS])
```

```python
import functools

import jax
import jax.numpy as jnp
from jax import lax
from jax.experimental import pallas as pl
from jax.experimental.pallas import tpu as pltpu

F32 = jnp.float32
BF16 = jnp.bfloat16

D_MODEL = 2048
DEPTH = 2
CHUNK = 64
MLA_HEADS = 8
Q_LORA = 512
KV_LORA = 256
NOPE = 128
ROPE = 64
VDIM = 128
RET_HEADS = 4
RET_DK = 256
RET_DV = 256
D_FF = 5632
D_IN = 4928
ROPE_THETA = 10000.0
LN_EPS = 1e-5
RMS_EPS = 1e-6
GN_EPS = 1e-5
ALPHA = (2 * DEPTH) ** 0.25
MLA_SCALE = (NOPE + ROPE) ** -0.5
RET_SCALE = RET_DK ** -0.5
ADAM_LR = 0.001
ADAM_B1 = 0.9
ADAM_B2 = 0.999
ADAM_EPS = 1e-08
ADAM_WD = 0.01
ADAM_STEP = 10

LANES = 128
HEAD_PAD = 256
MLA_IN = 1024
MLA_IN_USED = Q_LORA + KV_LORA + ROPE
D_IN_PAD = MLA_IN + 4 * 1024
ATT_BLOCK = 512
NEG = -1e30
VMEM_LIMIT = 56 * 1024 * 1024

N_CHIPS = 4
FLAT_W = 1024
BIG = ("w_in", "w_uq", "w_ukv", "w_out", "w_gate", "w_up", "w_down")
BIG_SHARD = {"w_in": (2048, 1232), "w_uq": (512, 384), "w_ukv": (256, 512), "w_out": (512, 2048),
             "w_gate": (2048, 1408), "w_up": (2048, 1408), "w_down": (1408, 2048)}
SMALL = ("ln_in_g", "ln_in_b", "q_norm_g", "kv_norm_g", "ret_gn_g", "ret_gn_b", "ln1_g", "ln1_b", "ln2_g", "ln2_b")
WEIGHTS = ("ln_in_g", "ln_in_b", "w_in", "q_norm_g", "kv_norm_g", "w_uq", "w_ukv", "ret_gn_g", "ret_gn_b", "w_out",
           "ln1_g", "ln1_b", "w_gate", "w_up", "w_down", "ln2_g", "ln2_b")
SMALL_ROWS = 32

MESH = pl.DeviceIdType.MESH


def _pick(dim, cands):
    for c in cands:
        if dim % c == 0:
            return c
    return dim


HBM = pl.BlockSpec(memory_space=pltpu.HBM)


class _Side:
    def __init__(self, arrays, out_shape, scratch, start, finish):
        self.arrays, self.out_shape, self.scratch, self.start, self.finish = arrays, out_shape, scratch, start, finish


def _call(body, name, out_shape, grid, in_specs, out_specs, scratch=(), sem=None, side=None):
    params = pltpu.CompilerParams(dimension_semantics=sem if side is None else ("arbitrary",) * len(grid),
                                  vmem_limit_bytes=VMEM_LIMIT)
    if side is None:
        return pl.pallas_call(body, name=name, out_shape=out_shape, grid=grid, in_specs=in_specs, out_specs=out_specs,
                              scratch_shapes=list(scratch), compiler_params=params)
    single = not isinstance(out_shape, (list, tuple))
    outs = [out_shape] if single else list(out_shape)
    ospecs = [out_specs] if single else list(out_specs)
    cuts = [len(in_specs), len(side.arrays), len(outs), len(side.out_shape), len(scratch)]
    ends = [sum(cuts[:k + 1]) for k in range(len(cuts))]

    def hosted(*refs):
        ins, s_in, o, s_out, scr = (refs[a:b] for a, b in zip([0] + ends[:-1], ends))
        sems = refs[ends[-1]:]
        ids = [pl.program_id(a) for a in range(len(grid))]
        first = functools.reduce(jnp.logical_and, [i == 0 for i in ids])
        last = functools.reduce(jnp.logical_and, [i == g - 1 for i, g in zip(ids, grid)])

        @pl.when(first)
        def _():
            side.start(s_in, s_out, sems)

        body(*ins, *o, *scr)

        @pl.when(last)
        def _():
            side.finish(s_in, s_out, sems)

    call = pl.pallas_call(hosted, name=name, out_shape=outs + list(side.out_shape), grid=grid,
                          in_specs=list(in_specs) + [HBM] * len(side.arrays),
                          out_specs=ospecs + [HBM] * len(side.out_shape),
                          scratch_shapes=list(scratch) + list(side.scratch), compiler_params=params)

    def run(*args):
        res = call(*args, *side.arrays)
        return (res[0] if single else list(res[:len(outs)])), list(res[len(outs):])

    return run


def _rows(tm, w, col=0):
    return pl.BlockSpec((tm, w), lambda i: (i, col))


def _whole(shape):
    return pl.BlockSpec(shape, lambda i: (0,) * len(shape))


def _sds(shape, dtype):
    return jax.ShapeDtypeStruct(shape, dtype)


def _matmul(a, b, name, ta=False, tb=False, out_dtype=F32, side=None):
    (K, M) = a.shape if ta else a.shape[::-1]
    (N, Kb) = b.shape if tb else b.shape[::-1]
    assert K == Kb, (a.shape, b.shape, ta, tb)
    tm = _pick(M, (1024, 1408, 512, 256, 128))
    tn = _pick(N, (1024, 512, 256, 128))
    tk = _pick(K, (2816, 2560, 2048, 1024, 512, 256))
    nk = K // tk
    dn = (((0 if ta else 1,), (1 if tb else 0,)), ((), ()))

    def body(a_ref, b_ref, o_ref, acc_ref):
        k = pl.program_id(2)
        if nk == 1:
            o_ref[...] = lax.dot_general(a_ref[...].astype(BF16), b_ref[...].astype(BF16), dn,
                                         preferred_element_type=F32).astype(out_dtype)
        else:
            @pl.when(k == 0)
            def _():
                acc_ref[...] = jnp.zeros_like(acc_ref)

            acc_ref[...] += lax.dot_general(a_ref[...].astype(BF16), b_ref[...].astype(BF16), dn,
                                            preferred_element_type=F32)

            @pl.when(k == nk - 1)
            def _():
                o_ref[...] = acc_ref[...].astype(out_dtype)

    a_spec = pl.BlockSpec((tk, tm), lambda i, j, k: (k, i)) if ta else pl.BlockSpec((tm, tk), lambda i, j, k: (i, k))
    b_spec = pl.BlockSpec((tn, tk), lambda i, j, k: (j, k)) if tb else pl.BlockSpec((tk, tn), lambda i, j, k: (k, j))
    return _call(body, name, _sds((M, N), out_dtype), (M // tm, N // tn, nk), [a_spec, b_spec],
                 pl.BlockSpec((tm, tn), lambda i, j, k: (i, j)), scratch=[pltpu.VMEM((tm, tn), F32)],
                 sem=("parallel", "parallel", "arbitrary"), side=side)(a, b)


def _sigmoid(x):
    return 1.0 / (1.0 + jnp.exp(-x))


def _rope_group(r, c, sa, sb):
    return r * c + pltpu.roll(r, 32, 1) * sa + pltpu.roll(r, 96, 1) * sb


def _ln_fwd(xs, coefs, g, b, name, want_z):
    S, D = xs[0].shape
    tm = 512
    n = len(xs)

    def body(*refs):
        x_refs, g_ref, b_ref, outs = refs[:n], refs[n], refs[n + 1], refs[n + 2:]
        z = None
        for cf, r in zip(coefs, x_refs):
            t = r[...] if cf == 1.0 else cf * r[...]
            z = t if z is None else z + t
        mu = jnp.mean(z, axis=-1, keepdims=True)
        zc = z - mu
        var = jnp.mean(zc * zc, axis=-1, keepdims=True)
        y = zc * lax.rsqrt(var + LN_EPS) * g_ref[...] + b_ref[...]
        if want_z:
            outs[0][...] = z
        outs[-2][...] = y
        outs[-1][...] = y.astype(BF16)

    out_shape = [_sds((S, D), F32)] * (2 if want_z else 1) + [_sds((S, D), BF16)]
    return _call(body, name, out_shape, (S // tm,), [_rows(tm, D)] * n + [_whole((1, D))] * 2,
                 [_rows(tm, D)] * len(out_shape), sem=("parallel",))(*xs, g, b)


def _ln_bwd(dys, coefs, z, g, name):
    S, D = z.shape
    tm = 512
    n = len(dys)

    def body(*refs):
        dy_refs, z_ref, g_ref = refs[:n], refs[n], refs[n + 1]
        dz_ref, dzb_ref, dg_ref, db_ref = refs[n + 2:]
        dy = None
        for cf, r in zip(coefs, dy_refs):
            t = r[...] if cf == 1.0 else cf * r[...]
            dy = t if dy is None else dy + t
        zv = z_ref[...]
        mu = jnp.mean(zv, axis=-1, keepdims=True)
        zc = zv - mu
        var = jnp.mean(zc * zc, axis=-1, keepdims=True)
        rstd = lax.rsqrt(var + LN_EPS)
        xh = zc * rstd
        dyg = dy * g_ref[...]
        dz = rstd * (dyg - jnp.mean(dyg, axis=-1, keepdims=True) - xh * jnp.mean(dyg * xh, axis=-1, keepdims=True))
        dz_ref[...] = dz
        dzb_ref[...] = dz.astype(BF16)

        @pl.when(pl.program_id(0) == 0)
        def _():
            dg_ref[...] = jnp.zeros_like(dg_ref)
            db_ref[...] = jnp.zeros_like(db_ref)

        dg_ref[...] += jnp.sum(dy * xh, axis=0, keepdims=True)
        db_ref[...] += jnp.sum(dy, axis=0, keepdims=True)

    return _call(body, name, [_sds((S, D), F32), _sds((S, D), BF16), _sds((1, D), F32), _sds((1, D), F32)],
                 (S // tm,), [_rows(tm, D)] * (n + 1) + [_whole((1, D))],
                 [_rows(tm, D), _rows(tm, D), _whole((1, D)), _whole((1, D))], sem=("arbitrary",))(*dys, z, g)


def _rms(x, g):
    return x * lax.rsqrt(jnp.mean(x * x, axis=-1, keepdims=True) + RMS_EPS) * g


def _prep1(h, tabs, qg, kvg, name):
    S = h.shape[0]
    tm = 512
    cm, sam, sbm, cr, sr = tabs

    def body(h_ref, cm_ref, sam_ref, sbm_ref, cr_ref, sr_ref, qg_ref, kvg_ref,
             qn_ref, kvn_ref, kr_ref, rq_ref, rk_ref, rv_ref):
        qn_ref[...] = _rms(h_ref[:, 0:Q_LORA], qg_ref[...]).astype(BF16)
        kvn_ref[...] = _rms(h_ref[:, Q_LORA:Q_LORA + KV_LORA], kvg_ref[...]).astype(BF16)
        kr_ref[...] = _rope_group(h_ref[:, 768:896], cm_ref[...], sam_ref[...], sbm_ref[...])
        c, s = cr_ref[...], sr_ref[...]
        for hd in range(RET_HEADS):
            for src, dst, scale in ((MLA_IN, rq_ref, RET_SCALE), (MLA_IN + 1024, rk_ref, None)):
                t1 = h_ref[:, src + hd * 256:src + hd * 256 + 128]
                t2 = h_ref[:, src + hd * 256 + 128:src + hd * 256 + 256]
                o1, o2 = t1 * c - t2 * s, t2 * c + t1 * s
                if scale is not None:
                    o1, o2 = o1 * scale, o2 * scale
                dst[:, hd * 256:hd * 256 + 128] = o1.astype(BF16)
                dst[:, hd * 256 + 128:hd * 256 + 256] = o2.astype(BF16)
        rv_ref[...] = h_ref[:, MLA_IN + 2048:MLA_IN + 3072].astype(BF16)

    t128 = _rows(tm, LANES)
    return _call(body, name,
                 [_sds((S, Q_LORA), BF16), _sds((S, KV_LORA), BF16), _sds((S, LANES), F32),
                  _sds((S, 1024), BF16), _sds((S, 1024), BF16), _sds((S, 1024), BF16)],
                 (S // tm,),
                 [_rows(tm, D_IN_PAD), t128, t128, t128, t128, t128, _whole((1, Q_LORA)), _whole((1, KV_LORA))],
                 [_rows(tm, Q_LORA), _rows(tm, KV_LORA), t128, _rows(tm, 1024), _rows(tm, 1024), _rows(tm, 1024)],
                 sem=("parallel",))(h, cm, sam, sbm, cr, sr, qg, kvg)


def _prep1_bwd(dqn, dkvn, dkr, drq, drk, drv, drg, h, tabs, qg, kvg, name):
    S = h.shape[0]
    tm = 512
    cm, sam, sbm, cr, sr = tabs

    def rms_bwd(x, g, dy):
        r = lax.rsqrt(jnp.mean(x * x, axis=-1, keepdims=True) + RMS_EPS)
        dyg = dy * g
        dx = r * dyg - x * (r * r * r) * jnp.mean(dyg * x, axis=-1, keepdims=True)
        return dx, jnp.sum(dy * x * r, axis=0, keepdims=True)

    def body(dqn_ref, dkvn_ref, dkr_ref, drq_ref, drk_ref, drv_ref, drg_ref, h_ref,
             cm_ref, sam_ref, sbm_ref, cr_ref, sr_ref, qg_ref, kvg_ref, dh_ref, dqg_ref, dkvg_ref):
        dcq, dqg = rms_bwd(h_ref[:, 0:Q_LORA], qg_ref[...], dqn_ref[...])
        dckv, dkvg = rms_bwd(h_ref[:, Q_LORA:Q_LORA + KV_LORA], kvg_ref[...], dkvn_ref[...])
        dh_ref[:, 0:Q_LORA] = dcq.astype(BF16)
        dh_ref[:, Q_LORA:Q_LORA + KV_LORA] = dckv.astype(BF16)
        dh_ref[:, 768:896] = _rope_group(dkr_ref[...], cm_ref[...], -sam_ref[...], -sbm_ref[...]).astype(BF16)
        dh_ref[:, 896:1024] = jnp.zeros((tm, LANES), BF16)
        c, s = cr_ref[...], sr_ref[...]
        for hd in range(RET_HEADS):
            for src, dst, scale in ((drq_ref, MLA_IN, RET_SCALE), (drk_ref, MLA_IN + 1024, None)):
                d1 = src[:, hd * 256:hd * 256 + 128]
                d2 = src[:, hd * 256 + 128:hd * 256 + 256]
                if scale is not None:
                    d1, d2 = d1 * scale, d2 * scale
                dh_ref[:, dst + hd * 256:dst + hd * 256 + 128] = (d1 * c + d2 * s).astype(BF16)
                dh_ref[:, dst + hd * 256 + 128:dst + hd * 256 + 256] = (d2 * c - d1 * s).astype(BF16)
        dh_ref[:, MLA_IN + 2048:MLA_IN + 3072] = drv_ref[...].astype(BF16)
        dh_ref[:, MLA_IN + 3072:MLA_IN + 4096] = drg_ref[...].astype(BF16)

        @pl.when(pl.program_id(0) == 0)
        def _():
            dqg_ref[...] = jnp.zeros_like(dqg_ref)
            dkvg_ref[...] = jnp.zeros_like(dkvg_ref)

        dqg_ref[...] += dqg
        dkvg_ref[...] += dkvg

    t128 = _rows(tm, LANES)
    return _call(body, name,
                 [_sds((S, D_IN_PAD), BF16), _sds((1, Q_LORA), F32), _sds((1, KV_LORA), F32)],
                 (S // tm,),
                 [_rows(tm, Q_LORA), _rows(tm, KV_LORA), t128, _rows(tm, 1024), _rows(tm, 1024), _rows(tm, 1024),
                  _rows(tm, 1024), _rows(tm, MLA_IN), t128, t128, t128, t128, t128,
                  _whole((1, Q_LORA)), _whole((1, KV_LORA))],
                 [_rows(tm, D_IN_PAD), _whole((1, Q_LORA)), _whole((1, KV_LORA))],
                 sem=("arbitrary",))(dqn, dkvn, dkr, drq, drk, drv, drg, h, cm, sam, sbm, cr, sr, qg, kvg)


def _prep2(q, kv, kr, tabs, name):
    S = q.shape[0]
    tm = 512
    cm, sam, sbm = tabs[:3]

    def body(q_ref, kv_ref, kr_ref, cm_ref, sam_ref, sbm_ref, qo_ref, ko_ref, vo_ref):
        c, sa, sb = cm_ref[...], sam_ref[...], sbm_ref[...]
        krb = kr_ref[...].astype(BF16)
        ones = jnp.ones((tm, LANES), BF16)
        for hd in range(MLA_HEADS):
            o = hd * HEAD_PAD
            qo_ref[:, o:o + 128] = (q_ref[:, o:o + 128] * MLA_SCALE).astype(BF16)
            qo_ref[:, o + 128:o + 256] = (_rope_group(q_ref[:, o + 128:o + 256], c, sa, sb) * MLA_SCALE).astype(BF16)
            ko_ref[:, o:o + 128] = kv_ref[:, hd * 128:hd * 128 + 128].astype(BF16)
            ko_ref[:, o + 128:o + 256] = krb
            vo_ref[:, o:o + 128] = kv_ref[:, 1024 + hd * 128:1024 + hd * 128 + 128].astype(BF16)
            vo_ref[:, o + 128:o + 256] = ones

    t128 = _rows(tm, LANES)
    return _call(body, name, [_sds((S, 2048), BF16)] * 3, (S // tm,),
                 [_rows(tm, 2048), _rows(tm, 2048), t128, t128, t128, t128],
                 [_rows(tm, 2048)] * 3, sem=("parallel",))(q, kv, kr, cm, sam, sbm)


def _prep2_bwd(dqm, dkm, dvm, tabs, name):
    S = dqm.shape[0]
    tm = 512
    cm, sam, sbm = tabs[:3]

    def body(dq_ref, dk_ref, dv_ref, cm_ref, sam_ref, sbm_ref, dqo_ref, dkvo_ref, dkr_ref):
        c, sa, sb = cm_ref[...], -sam_ref[...], -sbm_ref[...]
        dkr = None
        for hd in range(MLA_HEADS):
            o = hd * HEAD_PAD
            dqo_ref[:, o:o + 128] = (dq_ref[:, o:o + 128] * MLA_SCALE).astype(BF16)
            dqo_ref[:, o + 128:o + 256] = (_rope_group(dq_ref[:, o + 128:o + 256], c, sa, sb) * MLA_SCALE).astype(BF16)
            dkvo_ref[:, hd * 128:hd * 128 + 128] = dk_ref[:, o:o + 128].astype(BF16)
            t = dk_ref[:, o + 128:o + 256]
            dkr = t if dkr is None else dkr + t
        dkvo_ref[:, 1024:2048] = dv_ref[...].astype(BF16)
        dkr_ref[...] = dkr

    t128 = _rows(tm, LANES)
    return _call(body, name, [_sds((S, 2048), BF16), _sds((S, 2048), BF16), _sds((S, LANES), F32)], (S // tm,),
                 [_rows(tm, 2048), _rows(tm, 2048), _rows(tm, 1024), t128, t128, t128],
                 [_rows(tm, 2048), _rows(tm, 2048), t128], sem=("parallel",))(dqm, dkm, dvm, cm, sam, sbm)


def _gn_gate(a, o, h, gg, gb, name):
    S = a.shape[0]
    tm = 512

    def body(a_ref, o_ref, rg_ref, gg_ref, gb_ref, mix_ref):
        mix_ref[:, 0:1024] = a_ref[...].astype(BF16)
        for hd in range(RET_HEADS):
            sl = slice(hd * 256, hd * 256 + 256)
            ov = o_ref[:, sl]
            mu = jnp.mean(ov, axis=-1, keepdims=True)
            oc = ov - mu
            var = jnp.mean(oc * oc, axis=-1, keepdims=True)
            y = oc * lax.rsqrt(var + GN_EPS) * gg_ref[:, sl] + gb_ref[:, sl]
            rg = rg_ref[:, sl]
            mix_ref[:, 1024 + hd * 256:1024 + hd * 256 + 256] = (rg * _sigmoid(rg) * y).astype(BF16)

    return _call(body, name, _sds((S, 2048), BF16), (S // tm,),
                 [_rows(tm, 1024), _rows(tm, 1024), _rows(tm, 1024, 4), _whole((1, 1024)), _whole((1, 1024))],
                 _rows(tm, 2048), sem=("parallel",))(a, o, h, gg, gb)


def _gn_gate_bwd(dmixin, o, h, gg, gb, name):
    S = o.shape[0]
    tm = 512

    def body(dr_ref, o_ref, rg_ref, gg_ref, gb_ref, do_ref, drg_ref, dgg_ref, dgb_ref):
        @pl.when(pl.program_id(0) == 0)
        def _():
            dgg_ref[...] = jnp.zeros_like(dgg_ref)
            dgb_ref[...] = jnp.zeros_like(dgb_ref)

        for hd in range(RET_HEADS):
            sl = slice(hd * 256, hd * 256 + 256)
            ov = o_ref[:, sl]
            mu = jnp.mean(ov, axis=-1, keepdims=True)
            oc = ov - mu
            var = jnp.mean(oc * oc, axis=-1, keepdims=True)
            rstd = lax.rsqrt(var + GN_EPS)
            xh = oc * rstd
            g = gg_ref[:, sl]
            y = xh * g + gb_ref[:, sl]
            rg = rg_ref[:, sl]
            sg = _sigmoid(rg)
            dr = dr_ref[:, sl]
            dy = dr * (rg * sg)
            drg_ref[:, sl] = dr * y * (sg * (1.0 + rg * (1.0 - sg)))
            dgg_ref[:, sl] += jnp.sum(dy * xh, axis=0, keepdims=True)
            dgb_ref[:, sl] += jnp.sum(dy, axis=0, keepdims=True)
            dxh = dy * g
            do = rstd * (dxh - jnp.mean(dxh, axis=-1, keepdims=True) - xh * jnp.mean(dxh * xh, axis=-1, keepdims=True))
            do_ref[:, sl] = do.astype(BF16)

    return _call(body, name,
                 [_sds((S, 1024), BF16), _sds((S, 1024), F32), _sds((1, 1024), F32), _sds((1, 1024), F32)],
                 (S // tm,),
                 [_rows(tm, 1024, 1), _rows(tm, 1024), _rows(tm, 1024, 4), _whole((1, 1024)), _whole((1, 1024))],
                 [_rows(tm, 1024), _rows(tm, 1024), _whole((1, 1024)), _whole((1, 1024))],
                 sem=("arbitrary",))(dmixin, o, h, gg, gb)


GU_BLOCK = D_FF // N_CHIPS


def _matmul_swiglu(x, w_gu, name, side=None):
    S, K = x.shape
    tm = _pick(S, (512, 256, 128))
    tn = 2 * GU_BLOCK

    def body(x_ref, w_ref, gu_ref, act_ref):
        r = jnp.dot(x_ref[...], w_ref[...], preferred_element_type=F32)
        g, u = r[:, :GU_BLOCK], r[:, GU_BLOCK:]
        gu_ref[...] = r.astype(BF16)
        act_ref[...] = (g * _sigmoid(g) * u).astype(BF16)

    return _call(body, name, [_sds((S, 2 * D_FF), BF16), _sds((S, D_FF), BF16)], (S // tm, N_CHIPS),
                 [pl.BlockSpec((tm, K), lambda i, j: (i, 0)), pl.BlockSpec((K, tn), lambda i, j: (0, j))],
                 [pl.BlockSpec((tm, tn), lambda i, j: (i, j)), pl.BlockSpec((tm, GU_BLOCK), lambda i, j: (i, j))],
                 sem=("parallel", "parallel"), side=side)(x, w_gu)


def _swiglu_bwd(gu, dact, name):
    S = gu.shape[0]
    tm = 256

    def body(gu_ref, d_ref, o_ref):
        for j in range(N_CHIPS):
            at = 2 * GU_BLOCK * j
            g = gu_ref[:, at:at + GU_BLOCK].astype(F32)
            u = gu_ref[:, at + GU_BLOCK:at + 2 * GU_BLOCK].astype(F32)
            d = d_ref[:, GU_BLOCK * j:GU_BLOCK * (j + 1)]
            sg = _sigmoid(g)
            o_ref[:, at:at + GU_BLOCK] = (d * u * (sg * (1.0 + g * (1.0 - sg)))).astype(BF16)
            o_ref[:, at + GU_BLOCK:at + 2 * GU_BLOCK] = (d * (g * sg)).astype(BF16)

    return _call(body, name, _sds((S, 2 * D_FF), BF16), (S // tm,), [_rows(tm, 2 * D_FF), _rows(tm, D_FF)],
                 _rows(tm, 2 * D_FF), sem=("parallel",))(gu, dact)


def _loss_head(y, target, name):
    S, D = y.shape
    tm = 512

    def body(y_ref, t_ref, dy_ref, acc_ref):
        e = y_ref[...] - t_ref[...]
        dy_ref[...] = e / D

        @pl.when(pl.program_id(0) == 0)
        def _():
            acc_ref[...] = jnp.zeros_like(acc_ref)

        acc_ref[...] += jnp.sum(e * e, axis=0, keepdims=True)

    return _call(body, name, [_sds((S, D), F32), _sds((1, D), F32)], (S // tm,), [_rows(tm, D), _rows(tm, D)],
                 [_rows(tm, D), _whole((1, D))], sem=("arbitrary",))(y, target)


def _chunk_mask(T):
    r = lax.shift_right_logical(lax.broadcasted_iota(jnp.int32, (T, T), 0), 6)
    c = lax.shift_right_logical(lax.broadcasted_iota(jnp.int32, (T, T), 1), 6)
    return r >= c


def _dot_nt(a, b):
    return lax.dot_general(a, b, (((1,), (1,)), ((), ())), preferred_element_type=F32)


def _dot_tn(a, b):
    return lax.dot_general(a, b, (((0,), (0,)), ((), ())), preferred_element_type=F32)


def _decay_tables(T):
    lg = jnp.log1p(-jnp.exp2(-5.0 - jnp.arange(RET_HEADS, dtype=F32)))
    idx = jnp.arange(T, dtype=F32)
    diff = idx[:, None] - idx[None, :]
    rel = jnp.exp(lg[:, None, None] * diff[None])
    cid = jnp.arange(T) // CHUNK
    mask = (cid[:, None] >= cid[None, :]).astype(F32)
    reld = jnp.exp(lg[:, None, None] * jnp.abs(diff)[None]) * mask[None]
    lgrow = jnp.broadcast_to(lg[:, None, None], (RET_HEADS, 1, LANES))
    return lgrow, rel, reld


def _attn_fwd(q, k, v, heads, dk, dv, softmax, name, tables=None, side=None):
    S = q.shape[0]
    T = ATT_BLOCK
    nq = S // T
    rep = T // LANES
    vw = 2 * dv if softmax else dv
    assert not softmax or dv == LANES

    def body(*refs):
        if softmax:
            q_ref, k_ref, v_ref, o_ref, lse_ref, m_sc, acc_sc = refs
        else:
            q_ref, k_ref, v_ref, lg_ref, rel_ref, reld_ref, o_ref, acc_sc = refs
        i = pl.program_id(1)
        qv = q_ref[...]

        def kv_block(j):
            rows = pl.ds(pl.multiple_of(j * T, T), T)
            return k_ref[rows, :], v_ref[rows, :]

        kb, vb = kv_block(i)
        s = _dot_nt(qv, kb)
        if softmax:
            s = jnp.where(_chunk_mask(T), s, NEG)
            m = jnp.max(s, axis=-1, keepdims=True)
            p = jnp.exp(s - m)
            m_sc[...] = jnp.broadcast_to(m, (T, LANES))
        else:
            p = s * reld_ref[0]
        acc_sc[...] = jnp.dot(p.astype(BF16), vb, preferred_element_type=F32)

        def scores(j):
            kb, vb = kv_block(j)
            return _dot_nt(qv, kb), vb

        def update(j, s, vb):
            if softmax:
                m_prev = m_sc[...]
                m_next = jnp.maximum(m_prev, jnp.max(s, axis=-1, keepdims=True))
                alpha = jnp.exp(m_prev - m_next)
                p = jnp.exp(s - jnp.tile(m_next, (1, rep)))
                m_sc[...] = m_next
                acc_sc[...] = acc_sc[...] * jnp.tile(alpha, (1, vw // LANES)) + jnp.dot(
                    p.astype(BF16), vb, preferred_element_type=F32)
            else:
                fac = jnp.exp(lg_ref[0] * ((i - j) * T).astype(F32))
                p = s * (rel_ref[0] * jnp.tile(fac, (1, rep)))
                acc_sc[...] += jnp.dot(p.astype(BF16), vb, preferred_element_type=F32)

        def pair(jj, carry):
            first, second = scores(2 * jj), scores(2 * jj + 1)
            update(2 * jj, *first)
            update(2 * jj + 1, *second)
            return carry

        lax.fori_loop(0, i // 2, pair, 0)

        @pl.when(i % 2 == 1)
        def _():
            update(i - 1, *scores(i - 1))

        if softmax:
            l = acc_sc[:, dv:]
            o_ref[...] = acc_sc[:, :dv] / l
            lse_ref[...] = m_sc[...] + jnp.log(l)
        else:
            o_ref[...] = acc_sc[...]

    in_specs = [pl.BlockSpec((T, dk), lambda h, i: (i, h)), pl.BlockSpec((S, dk), lambda h, i: (0, h)),
                pl.BlockSpec((S, vw), lambda h, i: (0, h))]
    o_spec = pl.BlockSpec((T, dv), lambda h, i: (i, h))
    if softmax:
        return _call(body, name, [_sds((S, heads * dv), F32), _sds((S, heads * LANES), F32)], (heads, nq), in_specs,
                     [o_spec, pl.BlockSpec((T, LANES), lambda h, i: (i, h))],
                     scratch=[pltpu.VMEM((T, LANES), F32), pltpu.VMEM((T, vw), F32)],
                     sem=("parallel", "arbitrary"), side=side)(q, k, v)
    lgrow, rel, reld = tables
    in_specs += [pl.BlockSpec((1, 1, LANES), lambda h, i: (h, 0, 0)), pl.BlockSpec((1, T, T), lambda h, i: (h, 0, 0)),
                 pl.BlockSpec((1, T, T), lambda h, i: (h, 0, 0))]
    return _call(body, name, _sds((S, heads * dv), F32), (heads, nq), in_specs, o_spec,
                 scratch=[pltpu.VMEM((T, dv), F32)], sem=("parallel", "arbitrary"), side=side)(q, k, v, lgrow, rel, reld)


def _attn_bwd(q, k, v, do, heads, dk, dv, softmax, name, o=None, lse=None, tables=None, side=None):
    S = q.shape[0]
    T = ATT_BLOCK
    nq = S // T
    rep = T // LANES

    def body(*refs):
        if softmax:
            q_ref, k_ref, v_ref, do_ref, o_ref, lse_ref, dq_ref, dk_ref, dv_ref, dq_sc = refs
        else:
            q_ref, k_ref, v_ref, do_ref, lg_ref, rel_ref, reld_ref, dq_ref, dk_ref, dv_ref, dq_sc = refs
        i = pl.program_id(1)

        @pl.when(i == 0)
        def _():
            dk_ref[...] = jnp.zeros_like(dk_ref)
            dv_ref[...] = jnp.zeros_like(dv_ref)

        qv = q_ref[...]
        dof = do_ref[...].astype(F32)
        dov = dof.astype(BF16)
        if softmax:
            delta = jnp.sum(dof * o_ref[...], axis=-1, keepdims=True)
            lse_t = jnp.tile(lse_ref[...], (1, rep))
        dq_sc[...] = jnp.zeros_like(dq_sc)

        def products(j):
            rows = pl.ds(pl.multiple_of(j * T, T), T)
            kb = k_ref[rows, :]
            return rows, kb, _dot_nt(qv, kb), _dot_nt(dov, v_ref[rows, :])

        def block(j, diagonal, rows, kb, s, dp):
            if softmax:
                if diagonal:
                    s = jnp.where(_chunk_mask(T), s, NEG)
                p = jnp.exp(s - lse_t)
                ds = p * (dp - delta)
            else:
                if diagonal:
                    dec = reld_ref[0]
                else:
                    fac = jnp.exp(lg_ref[0] * ((i - j) * T).astype(F32))
                    dec = rel_ref[0] * jnp.tile(fac, (1, rep))
                p = s * dec
                ds = dp * dec
            dsb = ds.astype(BF16)
            dv_ref[rows, :] += _dot_tn(p.astype(BF16), dov)
            dk_ref[rows, :] += _dot_tn(dsb, qv)
            dq_sc[...] += jnp.dot(dsb, kb, preferred_element_type=F32)

        block(i, True, *products(i))

        def pair(jj, carry):
            first, second = products(2 * jj), products(2 * jj + 1)
            block(2 * jj, False, *first)
            block(2 * jj + 1, False, *second)
            return carry

        lax.fori_loop(0, i // 2, pair, 0)

        @pl.when(i % 2 == 1)
        def _():
            block(i - 1, False, *products(i - 1))

        dq_ref[...] = dq_sc[...]

    qspec = pl.BlockSpec((T, dk), lambda h, i: (i, h))
    kspec = pl.BlockSpec((S, dk), lambda h, i: (0, h))
    vspec = pl.BlockSpec((S, dv), lambda h, i: (0, h))
    dospec = pl.BlockSpec((T, dv), lambda h, i: (i, h))
    in_specs = [qspec, kspec, vspec, dospec]
    args = [q, k, v, do]
    if softmax:
        in_specs[2] = pl.BlockSpec((S, dv), lambda h, i: (0, 2 * h))
        in_specs += [dospec, pl.BlockSpec((T, LANES), lambda h, i: (i, h))]
        args += [o, lse]
    else:
        in_specs += [pl.BlockSpec((1, 1, LANES), lambda h, i: (h, 0, 0)),
                     pl.BlockSpec((1, T, T), lambda h, i: (h, 0, 0)), pl.BlockSpec((1, T, T), lambda h, i: (h, 0, 0))]
        args += list(tables)
    return _call(body, name, [_sds((S, heads * dk), F32), _sds((S, heads * dk), F32), _sds((S, heads * dv), F32)],
                 (heads, nq), in_specs, [qspec, kspec, vspec], scratch=[pltpu.VMEM((T, dk), F32)],
                 sem=("parallel", "arbitrary"), side=side)(*args)


def _rope_tables(pos):
    def tables(dim):
        inv_freq = ROPE_THETA ** (-jnp.arange(0, dim, 2, dtype=F32) / dim)
        ang = pos.astype(F32)[:, None] * inv_freq
        return jnp.cos(ang), jnp.sin(ang)

    cm, sm = tables(ROPE)
    S = pos.shape[0]
    z32, z64 = jnp.zeros((S, 32), F32), jnp.zeros((S, 64), F32)
    cr, sr = tables(RET_DK)
    return (jnp.concatenate([cm, cm, z64], 1), jnp.concatenate([z32, sm, z64], 1),
            jnp.concatenate([-sm, z32, z64], 1), cr, sr)


def _row(v):
    return v.reshape(1, -1).astype(F32)


def _local_step(x, pos, target, pipe, P):
    tabs = _rope_tables(pos)
    dtabs = _decay_tables(ATT_BLOCK)
    xf, xb = _ln_fwd([x], [1.0], _row(P["ln_in_g"]), _row(P["ln_in_b"]), "ln_in", False)
    pipe.gather_first()
    saved = []
    for l in range(DEPTH):
        w = functools.partial(pipe.weight, l)
        t = f"_l{l}"
        h = pipe.run(_matmul, "mm_h" + t, xb, w("w_in"))
        qn, kvn, kr, rq, rk, rv = _prep1(h, tabs, _row(P["q_norm_g"][l]), _row(P["kv_norm_g"][l]), "prep1" + t)
        q = _matmul(qn, w("w_uq"), "mm_q" + t)
        kv = _matmul(kvn, w("w_ukv"), "mm_kv" + t)
        qm, km, vm = _prep2(q, kv, kr, tabs, "prep2" + t)
        a, lse = pipe.run(_attn_fwd, "mla_fwd" + t, qm, km, vm, MLA_HEADS, HEAD_PAD, VDIM, True)
        o = pipe.run(_attn_fwd, "ret_fwd" + t, rq, rk, rv, RET_HEADS, RET_DK, RET_DV, False, tables=dtabs)
        mixin = _gn_gate(a, o, h, _row(P["ret_gn_g"][l]), _row(P["ret_gn_b"][l]), "gn_gate" + t)
        mix = _matmul(mixin, w("w_out"), "mm_mix" + t)
        z1, x1f, x1b = _ln_fwd([xf, mix], [ALPHA, 1.0], _row(P["ln1_g"][l]), _row(P["ln1_b"][l]), "ln1" + t, True)
        gu, act = pipe.run(_matmul_swiglu, "mm_gu" + t, x1b, w("w_gu"))
        f = pipe.run(_matmul, "mm_down" + t, act, w("w_down"))
        z2, x2f, x2b = _ln_fwd([x1f, f], [ALPHA, 1.0], _row(P["ln2_g"][l]), _row(P["ln2_b"][l]), "ln2" + t, True)
        saved.append(dict(xb=xb, h=h, qn=qn, kvn=kvn, rq=rq, rk=rk, rv=rv, qm=qm, km=km, vm=vm, a=a, lse=lse, o=o,
                          mixin=mixin, z1=z1, x1b=x1b, gu=gu, act=act, z2=z2))
        xf, xb = x2f, x2b

    dy, sqerr = _loss_head(xf, target, "loss_head")
    dP = {}
    dys, coefs = [dy], [1.0]
    for l in reversed(range(DEPTH)):
        w, sv = functools.partial(pipe.weight, l), saved[l]
        t = f"_l{l}"
        dz2, dz2b, dg, db = _ln_bwd(dys, coefs, sv["z2"], _row(P["ln2_g"][l]), "ln2_bwd" + t)
        dP[("ln2_g", l)], dP[("ln2_b", l)] = dg, db
        pipe.reduce(l, w_down=pipe.run(_matmul, "mm_dw_down" + t, sv["act"], dz2b, ta=True, out_dtype=BF16))
        dact = pipe.run(_matmul, "mm_dact" + t, dz2b, w("w_down"), tb=True)
        dgu = _swiglu_bwd(sv["gu"], dact, "swiglu_bwd" + t)
        pipe.reduce(l, w_gu=pipe.run(_matmul, "mm_dw_gu" + t, sv["x1b"], dgu, ta=True, out_dtype=BF16))
        dx1 = pipe.run(_matmul, "mm_dx1" + t, dgu, w("w_gu"), tb=True)
        dz1, dz1b, dg, db = _ln_bwd([dz2, dx1], [ALPHA, 1.0], sv["z1"], _row(P["ln1_g"][l]), "ln1_bwd" + t)
        dP[("ln1_g", l)], dP[("ln1_b", l)] = dg, db
        pipe.reduce(l, w_out=_matmul(sv["mixin"], dz1b, "mm_dw_out" + t, ta=True, out_dtype=BF16))
        dmixin = pipe.run(_matmul, "mm_dmixin" + t, dz1b, w("w_out"), tb=True)
        do, drg, dgg, dgb = _gn_gate_bwd(dmixin, sv["o"], sv["h"], _row(P["ret_gn_g"][l]), _row(P["ret_gn_b"][l]),
                                         "gn_gate_bwd" + t)
        dP[("ret_gn_g", l)], dP[("ret_gn_b", l)] = dgg, dgb
        drq, drk, drv = pipe.run(_attn_bwd, "ret_bwd" + t, sv["rq"], sv["rk"], sv["rv"], do, RET_HEADS, RET_DK, RET_DV,
                                 False, tables=dtabs)
        dqm, dkm, dvm = pipe.run(_attn_bwd, "mla_bwd" + t, sv["qm"], sv["km"], sv["vm"], dmixin, MLA_HEADS, HEAD_PAD,
                                 VDIM, True, o=sv["a"], lse=sv["lse"])
        dq, dkv, dkr = _prep2_bwd(dqm, dkm, dvm, tabs, "prep2_bwd" + t)
        g_uq = _matmul(sv["qn"], dq, "mm_dw_uq" + t, ta=True, out_dtype=BF16)
        dqn = _matmul(dq, w("w_uq"), "mm_dqn" + t, tb=True)
        g_ukv = _matmul(sv["kvn"], dkv, "mm_dw_ukv" + t, ta=True, out_dtype=BF16)
        dkvn = _matmul(dkv, w("w_ukv"), "mm_dkvn" + t, tb=True)
        dh, dqg, dkvg = _prep1_bwd(dqn, dkvn, dkr, drq, drk, drv, drg, sv["h"], tabs, _row(P["q_norm_g"][l]),
                                   _row(P["kv_norm_g"][l]), "prep1_bwd" + t)
        dP[("q_norm_g", l)], dP[("kv_norm_g", l)] = dqg, dkvg
        pipe.reduce(l, w_uq=g_uq, w_ukv=g_ukv,
                    w_in=pipe.run(_matmul, "mm_dw_in" + t, sv["xb"], dh, ta=True, out_dtype=BF16))
        dxl = pipe.run(_matmul, "mm_dxl" + t, dh, w("w_in"), tb=True)
        dys, coefs = [dz1, dxl], [ALPHA, 1.0]
    grad_x, _, dg, db = _ln_bwd(dys, coefs, x, _row(P["ln_in_g"]), "ln_in_bwd")
    dP[("ln_in_g", None)], dP[("ln_in_b", None)] = dg, db
    return sqerr, grad_x, dP


INTERNAL_OF = {"w_in": ("w_in",), "w_uq": ("w_uq",), "w_ukv": ("w_ukv",), "w_out": ("w_out",),
               "w_gu": ("w_gate", "w_up"), "w_down": ("w_down",)}
ROW_PIECES = {"w_up": 1024}


def _internal_weight(name, *blocks):
    cat = lambda parts: jnp.concatenate(parts, axis=1)
    cols = lambda b: cat([b[j] for j in range(N_CHIPS)])
    b = blocks[0]
    if name in ("w_out", "w_down"):
        return b.reshape(-1, b.shape[-1])
    if name == "w_gu":
        return cat([blk[j] for j in range(N_CHIPS) for blk in blocks])
    if name == "w_in":
        return cat([b[0][:, :MLA_IN_USED], jnp.zeros((D_MODEL, MLA_IN - MLA_IN_USED), BF16), b[0][:, MLA_IN_USED:]]
                   + [b[j] for j in range(1, N_CHIPS)])
    if name == "w_uq":
        uq, hw = cols(b), NOPE + ROPE
        pad = jnp.zeros((Q_LORA, HEAD_PAD - hw), BF16)
        return cat([p for h in range(MLA_HEADS) for p in (uq[:, h * hw:(h + 1) * hw], pad)])
    ukv = cols(b)
    return cat([ukv[:, 256 * h:256 * h + NOPE] for h in range(MLA_HEADS)]
               + [ukv[:, 256 * h + NOPE:256 * h + 256] for h in range(MLA_HEADS)])


def _grad_shards(name, g):
    cat = lambda parts: jnp.concatenate(parts, axis=1)
    if name in ("w_out", "w_down"):
        return {name: g.reshape(N_CHIPS, -1, g.shape[-1])}
    if name == "w_gu":
        return {"w_gate": _ColBlocks(g, 0), "w_up": _ColBlocks(g, 1)}
    if name == "w_in":
        ci, shift = BIG_SHARD["w_in"][1], MLA_IN - MLA_IN_USED
        return {name: [cat([g[:, :MLA_IN_USED], g[:, MLA_IN:ci + shift]])]
                + [g[:, ci * j + shift:ci * (j + 1) + shift] for j in range(1, N_CHIPS)]}
    if name == "w_uq":
        cq = NOPE + ROPE
        return {name: [cat([g[:, HEAD_PAD * h:HEAD_PAD * h + cq] for h in (2 * j, 2 * j + 1)]) for j in range(N_CHIPS)]}
    return {name: [cat([g[:, o + NOPE * h:o + NOPE * (h + 1)] for h in (2 * j, 2 * j + 1) for o in (0, MLA_HEADS * NOPE)])
                   for j in range(N_CHIPS)]}


def _small_layout(P):
    out, at = {}, 0
    for n in SMALL:
        out[n] = (at, P[n].size)
        at += P[n].size
    return out, at


def _flatten_small(P, last):
    v = jnp.concatenate([P[n].reshape(-1).astype(F32) for n in SMALL] + [last.reshape(-1).astype(F32)])
    return jnp.pad(v, (0, SMALL_ROWS * FLAT_W - v.size)).reshape(SMALL_ROWS, FLAT_W)


def _place():
    return lax.axis_index("x"), lax.axis_index("y"), lax.axis_index("c")


def _other_chips(x, y):
    return [(1 - x, y), (x, 1 - y), (1 - x, 1 - y)]


def _rcopy(src, dst, ssem, rsem, dev):
    return pltpu.make_async_remote_copy(src_ref=src, dst_ref=dst, send_sem=ssem, recv_sem=rsem, device_id=dev,
                                        device_id_type=MESH)


def _comm_call(body, name, out_shape, n_in, scratch):
    many = isinstance(out_shape, (list, tuple))
    return pl.pallas_call(body, name=name, out_shape=out_shape, in_specs=[HBM] * n_in,
                          out_specs=[HBM] * len(out_shape) if many else HBM, scratch_shapes=scratch)


def _half(ref, which):
    rows = ref.shape[0] // 2
    return ref.at[pl.ds(pl.multiple_of(which * rows, 16), rows)]


def _dma_sems(n):
    return pltpu.SemaphoreType.DMA((n,))


def _allgather_side(ws):
    k = len(ws)

    def peers():
        x, y, c = _place()
        return c, 2 * x + y, (x, y, 1 - c), [(n, t, cx, cy) for n in range(k) for t, (cx, cy) in enumerate(_other_chips(x, y))]

    def outgoing(w_refs, g_refs, sems):
        ssem, rsem, _, _, ossem, orsem = sems
        c, j, sib, nt = peers()
        owns = [_rcopy(w_refs[n], g_refs[n].at[j], ossem.at[n], orsem.at[n], sib) for n in range(k)]
        sends = [_rcopy(_half(w_refs[n], c), _half(g_refs[n].at[j], c), ssem.at[3 * n + t], rsem.at[3 * n + t],
                        (cx, cy, c)) for n, t, cx, cy in nt]
        return owns, sends

    def incoming(g_refs, sems):
        ssem, rsem, fssem, frsem, _, _ = sems
        c, _, sib, nt = peers()
        landed, passed, relayed = [], [], []
        for n, t, cx, cy in nt:
            mine, other = (_half(g_refs[n].at[2 * cx + cy], h) for h in (c, 1 - c))
            landed.append(_rcopy(mine, mine, ssem.at[3 * n + t], rsem.at[3 * n + t], (cx, cy, c)))
            passed.append(_rcopy(mine, mine, fssem.at[3 * n + t], frsem.at[3 * n + t], sib))
            relayed.append(_rcopy(other, other, fssem.at[3 * n + t], frsem.at[3 * n + t], sib))
        return landed, passed, relayed

    def start(w_refs, g_refs, sems):
        owns, sends = outgoing(w_refs, g_refs, sems)
        for cp in sends + owns:
            cp.start()

    def finish(w_refs, g_refs, sems):
        owns, sends = outgoing(w_refs, g_refs, sems)
        landed, passed, relayed = incoming(g_refs, sems)
        for got, on in zip(landed, passed):
            got.wait_recv()
            on.start()
        for cp in relayed:
            cp.wait_recv()
        for cp in owns:
            cp.wait()
        for cp in sends + passed:
            cp.wait_send()

    return _Side(list(ws), [_sds((N_CHIPS,) + w.shape, w.dtype) for w in ws],
                 [_dma_sems(3 * k)] * 4 + [_dma_sems(k)] * 2, start, finish)


def _exchange_side(parts):
    k = len(parts)

    def copies(p_refs, rcv_refs, sems):
        ssem, rsem = sems
        x, y, c = _place()
        return [_rcopy(p_refs[n].at[2 * cx + cy], rcv_refs[n].at[t], ssem.at[3 * n + t], rsem.at[3 * n + t], (cx, cy, c))
                for n in range(k) for t, (cx, cy) in enumerate(_other_chips(x, y))]

    def start(p_refs, rcv_refs, sems):
        for cp in copies(p_refs, rcv_refs, sems):
            cp.start()

    def finish(p_refs, rcv_refs, sems):
        for cp in copies(p_refs, rcv_refs, sems):
            cp.wait()

    return _Side(list(parts), [_sds((3,) + p.shape[1:], p.dtype) for p in parts], [_dma_sems(3 * k)] * 2, start, finish)


def _run_side(side, name):
    k_in, k_out = len(side.arrays), len(side.out_shape)

    def body(*refs):
        parts = refs[:k_in], refs[k_in:k_in + k_out], refs[k_in + k_out:]
        side.start(*parts)
        side.finish(*parts)

    return _comm_call(body, name, list(side.out_shape), k_in, list(side.scratch))(*side.arrays)


def _sibling_side(arrays, out_shape, n_copies, copies):
    def start(in_refs, out_refs, sems):
        for cp in copies(in_refs, out_refs, sems):
            cp.start()

    def finish(in_refs, out_refs, sems):
        for cp in copies(in_refs, out_refs, sems):
            cp.wait()

    return _Side(list(arrays), out_shape, [_dma_sems(n_copies)] * 2, start, finish)


class _ColBlocks:
    def __init__(self, array, off):
        self.array, self.off, self.dtype = array, off, array.dtype
        self.shape = (N_CHIPS, array.shape[0], GU_BLOCK)

    def block(self, ref, jj):
        return ref.at[:, pl.ds((2 * jj + self.off) * GU_BLOCK, GU_BLOCK)]


def _swap_side(gds):
    k = len(gds)

    def copies(gd_refs, out_refs, sems):
        ssem, rsem = sems
        x, y, c = _place()
        blocks = [[g.block(gd_refs[n], jj) if isinstance(g, _ColBlocks) else gd_refs[n].at[jj] for jj in range(N_CHIPS)]
                  for n, g in enumerate(gds)]
        return [_rcopy(_half(blocks[n][jj], 1 - c), out_refs[n].at[jj], ssem.at[N_CHIPS * n + jj],
                       rsem.at[N_CHIPS * n + jj], (x, y, 1 - c)) for n in range(k) for jj in range(N_CHIPS)]

    return _sibling_side([g.array if isinstance(g, _ColBlocks) else g for g in gds],
                         [_sds((N_CHIPS, g.shape[1] // 2, g.shape[2]), g.dtype) for g in gds], N_CHIPS * k, copies)


def _share_side(reds):
    k = len(reds)

    def copies(r_refs, out_refs, sems):
        ssem, rsem = sems
        x, y, c = _place()
        return [_rcopy(r_refs[n], out_refs[n], ssem.at[n], rsem.at[n], (x, y, 1 - c)) for n in range(k)]

    return _sibling_side(reds, [_sds(r.shape, r.dtype) for r in reds], k, copies)


def _join_sides(sides):
    if len(sides) == 1:
        return sides[0]
    cuts = [(len(s.arrays), len(s.out_shape), len(s.scratch)) for s in sides]

    def each(method, in_refs, out_refs, sems):
        a = o = m = 0
        for s, (ka, ko, km) in zip(sides, cuts):
            getattr(s, method)(in_refs[a:a + ka], out_refs[o:o + ko], sems[m:m + km])
            a, o, m = a + ka, o + ko, m + km

    return _Side([x for s in sides for x in s.arrays], [x for s in sides for x in s.out_shape],
                 [x for s in sides for x in s.scratch], functools.partial(each, "start"), functools.partial(each, "finish"))


def _allreduce_small(small):
    def body(s_ref, all_ref, sssem, srsem, lsem):
        x, y, c = _place()
        me = 4 * x + 2 * y + c
        own = pltpu.make_async_copy(s_ref, all_ref.at[me], lsem)
        own.start()
        cps = []
        for r in range(1, 8):
            fx, fy, fc = (r >> 2) & 1, (r >> 1) & 1, r & 1
            px, py, pc = (1 - x if fx else x, 1 - y if fy else y, 1 - c if fc else c)
            peer = 4 * px + 2 * py + pc
            send = _rcopy(s_ref, all_ref.at[me], sssem.at[r - 1], srsem.at[me], (px, py, pc))
            send.start()
            cps.append((send, _rcopy(s_ref, all_ref.at[peer], sssem.at[r - 1], srsem.at[peer], (px, py, pc))))
        for send, recv in cps:
            send.wait_send()
            recv.wait_recv()
        own.wait()

    return _comm_call(body, "allreduce_small", [_sds((8,) + small.shape, small.dtype)], 1,
                      [pltpu.SemaphoreType.DMA((7,)), pltpu.SemaphoreType.DMA((8,)), pltpu.SemaphoreType.DMA(())])(small)[0]


def _add_pair(gd, got, c, name):
    _, R, W = got.shape
    tm = _pick(R, (512, 256, 128, 64))
    nb = R // tm

    def body(c_ref, a_ref, b_ref, o_ref):
        o_ref[...] = (a_ref[...].astype(F32) + b_ref[...].astype(F32)).astype(o_ref.dtype)

    if isinstance(gd, _ColBlocks):
        off = gd.off
        own = pl.BlockSpec((tm, W), lambda j, i, c_ref: (c_ref[0] * nb + i, 2 * j + off))
        gd = gd.array
    else:
        own = pl.BlockSpec((None, tm, W), lambda j, i, c_ref: (j, c_ref[0] * nb + i, 0))
    grid_spec = pltpu.PrefetchScalarGridSpec(
        num_scalar_prefetch=1, grid=(N_CHIPS, nb),
        in_specs=[own, pl.BlockSpec((None, tm, W), lambda j, i, c_ref: (j, i, 0))],
        out_specs=pl.BlockSpec((None, tm, W), lambda j, i, c_ref: (j, i, 0)))
    return pl.pallas_call(body, name=name, grid_spec=grid_spec, out_shape=_sds((N_CHIPS, R, W), gd.dtype),
                          compiler_params=pltpu.CompilerParams(dimension_semantics=("parallel", "parallel"),
                                                               vmem_limit_bytes=VMEM_LIMIT))(c, gd, got)


def _add_chips(part, rcv, j, name):
    _, R, W = part.shape
    tm = _pick(R, (512, 256, 128, 64))

    def body(j_ref, p_ref, r0_ref, r1_ref, r2_ref, o_ref):
        o_ref[...] = ((p_ref[...].astype(F32) + r0_ref[...].astype(F32)) + r1_ref[...].astype(F32)) + r2_ref[...].astype(F32)

    def slot(t):
        return pl.BlockSpec((None, tm, W), lambda i, j_ref: (t, i, 0))

    grid_spec = pltpu.PrefetchScalarGridSpec(
        num_scalar_prefetch=1, grid=(R // tm,),
        in_specs=[pl.BlockSpec((None, tm, W), lambda i, j_ref: (j_ref[0], i, 0)), slot(0), slot(1), slot(2)],
        out_specs=pl.BlockSpec((tm, W), lambda i, j_ref: (i, 0)))
    return pl.pallas_call(body, name=name, grid_spec=grid_spec, out_shape=_sds((R, W), F32),
                          compiler_params=pltpu.CompilerParams(dimension_semantics=("parallel",),
                                                               vmem_limit_bytes=VMEM_LIMIT))(j, part, rcv, rcv, rcv)


def _sum_small(allsmall):
    _, R, W = allsmall.shape

    def body(a_ref, o_ref):
        acc = a_ref[0]
        for d in range(1, 8):
            acc = acc + a_ref[d]
        o_ref[...] = acc

    return _call(body, "sum_small", _sds((R, W), F32), (1,), [_whole((8, R, W))], _whole((R, W)),
                 sem=("arbitrary",))(allsmall)


def _adamw(w, g, m, v, name):
    R, C = w.shape
    tm = _pick(R, (256, 128, 64, 32, 8))

    def body(w_ref, g_ref, m_ref, v_ref, d_ref, mo_ref, vo_ref):
        gv = g_ref[...]
        mn = ADAM_B1 * m_ref[...] + (1.0 - ADAM_B1) * gv
        vn = ADAM_B2 * v_ref[...] + (1.0 - ADAM_B2) * (gv * gv)
        m_hat = mn / (1.0 - ADAM_B1 ** ADAM_STEP)
        v_hat = vn / (1.0 - ADAM_B2 ** ADAM_STEP)
        d_ref[...] = -ADAM_LR * (m_hat / (jnp.sqrt(v_hat) + ADAM_EPS) + ADAM_WD * w_ref[...])
        mo_ref[...] = mn
        vo_ref[...] = vn

    spec = _rows(tm, C)
    return _call(body, name, [_sds((R, C), F32)] * 3, (R // tm,), [spec] * 4, [spec] * 3, sem=("parallel",))(w, g, m, v)


def _adamw_layer(c, w, m, v, mine, other, l, prev, name):
    _, R, C = w.shape
    half = R // 2
    tm = _pick(half, (256, 128, 64))
    nbh = half // tm

    def body(c_ref, w_ref, m_ref, v_ref, a_ref, b_ref, *rest):
        g_ref, d_ref, mo_ref, vo_ref = rest[-4:]
        gv = jnp.where(pl.program_id(0) // nbh == c_ref[0], a_ref[...], b_ref[...])
        mn = ADAM_B1 * m_ref[...] + (1.0 - ADAM_B1) * gv
        vn = ADAM_B2 * v_ref[...] + (1.0 - ADAM_B2) * (gv * gv)
        m_hat = mn / (1.0 - ADAM_B1 ** ADAM_STEP)
        v_hat = vn / (1.0 - ADAM_B2 ** ADAM_STEP)
        g_ref[...] = gv
        d_ref[...] = -ADAM_LR * (m_hat / (jnp.sqrt(v_hat) + ADAM_EPS) + ADAM_WD * w_ref[...])
        mo_ref[...] = mn
        vo_ref[...] = vn

    layer = pl.BlockSpec((None, tm, C), lambda i, c_ref: (l, i, 0))
    halfspec = pl.BlockSpec((tm, C), lambda i, c_ref: (i % nbh, 0))
    n_prev = 0 if prev is None else 4
    grid_spec = pltpu.PrefetchScalarGridSpec(
        num_scalar_prefetch=1, grid=(R // tm,),
        in_specs=[layer] * 3 + [halfspec] * 2 + [pl.BlockSpec(memory_space=pl.ANY)] * n_prev,
        out_specs=[layer] * 4)
    return pl.pallas_call(body, name=name, grid_spec=grid_spec, out_shape=[_sds(w.shape, F32)] * 4,
                          input_output_aliases={6 + k: k for k in range(n_prev)},
                          compiler_params=pltpu.CompilerParams(dimension_semantics=("parallel",),
                                                               vmem_limit_bytes=VMEM_LIMIT))(
        c, w, m, v, mine, other, *(prev or ()))


FIRST_GATHER = ("w_in", "w_uq", "w_ukv")
G_DOWN, G_GU, G_OUT, G_IN = ("w_down",), ("w_gate", "w_up"), ("w_out",), ("w_uq", "w_ukv", "w_in")


def _backward_jobs(l):
    t = f"_l{l}"
    return {"mm_dact" + t: [("swap", l, G_DOWN)], "mm_dw_gu" + t: [("exchange", l, G_DOWN)],
            "mm_dx1" + t: [("swap", l, G_GU), ("share", l, G_DOWN)], "mm_dmixin" + t: [("swap", l, G_OUT)],
            "ret_bwd" + t: [("exchange", l, ("w_gate",))],
            "mla_bwd" + t: [("exchange", l, ("w_up", "w_out")), ("share", l, ("w_gate",))],
            "mm_dw_in" + t: [("share", l, ("w_up", "w_out"))]}


JOBS = {
    "mm_h_l0": [("gather", 0, ("w_up@a",))], "mla_fwd_l0": [("gather", 0, ("w_gate", "w_out"))],
    "ret_fwd_l0": [("gather", 0, ("w_up@b",))],
    "mm_gu_l0": [("gather", 0, ("w_down",)), ("gather", 1, ("w_uq", "w_ukv"))],
    "mm_down_l0": [("gather", 1, ("w_in",))], "mm_h_l1": [("gather", 1, ("w_up@a",))],
    "mla_fwd_l1": [("gather", 1, ("w_gate", "w_out"))], "ret_fwd_l1": [("gather", 1, ("w_up@b",))],
    "mm_gu_l1": [("gather", 1, ("w_down",))],
    **_backward_jobs(1), **_backward_jobs(0),
    "mm_dxl_l1": [("swap", 1, G_IN)],
    "mm_dx1_l0": [("swap", 0, G_GU), ("share", 0, G_DOWN), ("exchange", 1, G_IN)],
    "ret_bwd_l0": [("exchange", 0, ("w_gate",)), ("share", 1, G_IN)], "mm_dxl_l0": [("exchange", 0, G_IN)]}
PLANNED = {job for jobs in JOBS.values() for job in jobs}


class _Pipeline:
    def __init__(self, own, Wt, Mo, Vo, core, chip):
        self.own, self.Wt, self.Mo, self.Vo, self.core, self.chip = own, Wt, Mo, Vo, core, chip
        self.blocks, self.whole, self.gds, self.parts, self.reds = {}, {}, {}, {}, {}
        self.results = {n: None for n in BIG}

    def gather_first(self):
        job = ("gather", 0, FIRST_GATHER)
        self._done(*job, _run_side(self._side(*job), "allgather_first"))

    def _gathered(self, l, n):
        if n in ROW_PIECES:
            return jnp.concatenate([self.blocks[(l, n + "@a")], self.blocks[(l, n + "@b")]], axis=1)
        return self.blocks[(l, n)]

    def weight(self, l, name):
        if (l, name) not in self.whole:
            self.whole[(l, name)] = _internal_weight(name, *[self._gathered(l, n) for n in INTERNAL_OF[name]])
        return self.whole[(l, name)]

    def run(self, fn, name, *args, **kw):
        jobs = JOBS.get(name, ())
        if not jobs:
            return fn(*args, name=name, **kw)
        sides = [self._side(*job) for job in jobs]
        out, res = fn(*args, name=name, side=_join_sides(sides), **kw)
        for job, side in zip(jobs, sides):
            k = len(side.out_shape)
            self._done(*job, res[:k])
            res = res[k:]
        return out

    def reduce(self, l, **grads):
        shards = {}
        for name, g in grads.items():
            shards.update(_grad_shards(name, g))
        for n, sh in shards.items():
            self.gds[(l, n)] = sh if hasattr(sh, "shape") else jnp.stack(sh)
        self._alone("swap", l, tuple(shards))

    def _alone(self, kind, l, names):
        if (kind, l, names) not in PLANNED:
            self._done(kind, l, names, _run_side(self._side(kind, l, names), f"{kind}_{names[0]}_l{l}"))

    def _side(self, kind, l, names):
        if kind == "gather":
            return _allgather_side([self.own[l][n] for n in names])
        store = {"swap": self.gds, "exchange": self.parts, "share": self.reds}[kind]
        make = {"swap": _swap_side, "exchange": _exchange_side, "share": _share_side}[kind]
        return make([store[(l, n)] for n in names])

    def _done(self, kind, l, names, res):
        for n, r in zip(names, res):
            if kind == "gather":
                self.blocks[(l, n)] = r
            elif kind == "swap":
                self.parts[(l, n)] = _add_pair(self.gds[(l, n)], r, self.core, f"add_pair_{n}_l{l}")
            elif kind == "exchange":
                self.reds[(l, n)] = _add_chips(self.parts[(l, n)], r, self.chip, f"add_chips_{n}_l{l}")
            else:
                self.results[n] = _adamw_layer(self.core, self.Wt[n], self.Mo[n], self.Vo[n], self.reds[(l, n)], r, l,
                                               self.results[n], f"adamw_{n}_l{l}")
        if kind == "exchange":
            self._alone("share", l, names)


def kernel(x, positions, ln_in_g, ln_in_b, w_in, q_norm_g, kv_norm_g, w_uq, w_ukv, ret_gn_g, ret_gn_b, w_out, ln1_g, ln1_b, w_gate, w_up, w_down, ln2_g, ln2_b, loss_target, m_ln_in_g, m_ln_in_b, m_w_in, m_q_norm_g, m_kv_norm_g, m_w_uq, m_w_ukv, m_ret_gn_g, m_ret_gn_b, m_w_out, m_ln1_g, m_ln1_b, m_w_gate, m_w_up, m_w_down, m_ln2_g, m_ln2_b, v_ln_in_g, v_ln_in_b, v_w_in, v_q_norm_g, v_kv_norm_g, v_w_uq, v_w_ukv, v_ret_gn_g, v_ret_gn_b, v_w_out, v_ln1_g, v_ln1_b, v_w_gate, v_w_up, v_w_down, v_ln2_g, v_ln2_b):
    given = dict(locals())
    Wt = {n: given[n] for n in WEIGHTS}
    Mo = {n: given["m_" + n] for n in WEIGHTS}
    Vo = {n: given["v_" + n] for n in WEIGHTS}
    cx, cy, cc = _place()
    chip = (2 * cx + cy).astype(jnp.int32)
    core = cc.astype(jnp.int32)

    own = [{n: Wt[n][l].astype(BF16) for n in BIG} for l in range(DEPTH)]
    for shard in own:
        for n, at in ROW_PIECES.items():
            shard[n + "@a"], shard[n + "@b"] = shard[n][:at], shard[n][at:]
    pipe = _Pipeline(own, Wt, Mo, Vo, core.reshape(1), chip.reshape(1))
    sqerr, grad_x, dP = _local_step(x[0], positions[0], loss_target[0], pipe, Wt)
    results = pipe.results

    small_g = {n: (dP[(n, None)] if Wt[n].ndim == 1 else jnp.stack([dP[(n, l)] for l in range(DEPTH)])) for n in SMALL}
    local_loss = 0.5 * jnp.sum(sqerr) / D_MODEL
    small_sum = _sum_small(_allreduce_small(_flatten_small(small_g, local_loss))).reshape(-1)
    layout, n_small = _small_layout(Wt)
    loss = small_sum[n_small]

    grads, deltas, new_m, new_v = {}, {}, {}, {}
    for n in BIG:
        grads[n], deltas[n], new_m[n], new_v[n] = results[n]
    zero = jnp.zeros((), F32)
    d, mn, vn = _adamw(_flatten_small(Wt, zero), small_sum.reshape(SMALL_ROWS, FLAT_W), _flatten_small(Mo, zero),
                       _flatten_small(Vo, zero), "adamw_small")
    for n in SMALL:
        at, size = layout[n]
        pick = lambda a: a.reshape(-1)[at:at + size].reshape(Wt[n].shape)
        grads[n], deltas[n], new_m[n], new_v[n] = pick(small_sum), pick(d), pick(mn), pick(vn)

    return (loss, grad_x[None], *[grads[n] for n in WEIGHTS], *[deltas[n] for n in WEIGHTS],
            *[new_m[n] for n in WEIGHTS], *[new_v[n] for n in WEIGHTS])
```

```python
import functools

import jax
import jax.numpy as jnp
from jax import lax
from jax.experimental import pallas as pl
from jax.experimental.pallas import tpu as pltpu

F32 = jnp.float32
BF16 = jnp.bfloat16

D_MODEL = 2048
DEPTH = 2
CHUNK = 64
MLA_HEADS = 8
Q_LORA = 512
KV_LORA = 256
NOPE = 128
ROPE = 64
VDIM = 128
RET_HEADS = 4
RET_DK = 256
RET_DV = 256
D_FF = 5632
D_IN = 4928
ROPE_THETA = 10000.0
LN_EPS = 1e-5
RMS_EPS = 1e-6
GN_EPS = 1e-5
ALPHA = (2 * DEPTH) ** 0.25
MLA_SCALE = (NOPE + ROPE) ** -0.5
RET_SCALE = RET_DK ** -0.5
ADAM_LR = 0.001
ADAM_B1 = 0.9
ADAM_B2 = 0.999
ADAM_EPS = 1e-08
ADAM_WD = 0.01
ADAM_STEP = 10

LANES = 128
HEAD_PAD = 256
MLA_IN = 1024
MLA_IN_USED = Q_LORA + KV_LORA + ROPE
D_IN_PAD = MLA_IN + 4 * 1024
ATT_BLOCK = 512
NEG = -1e30
VMEM_LIMIT = 56 * 1024 * 1024

N_CHIPS = 4
FLAT_W = 1024
BIG = ("w_in", "w_uq", "w_ukv", "w_out", "w_gate", "w_up", "w_down")
BIG_SHARD = {"w_in": (2048, 1232), "w_uq": (512, 384), "w_ukv": (256, 512), "w_out": (512, 2048),
             "w_gate": (2048, 1408), "w_up": (2048, 1408), "w_down": (1408, 2048)}
SMALL = ("ln_in_g", "ln_in_b", "q_norm_g", "kv_norm_g", "ret_gn_g", "ret_gn_b", "ln1_g", "ln1_b", "ln2_g", "ln2_b")
WEIGHTS = ("ln_in_g", "ln_in_b", "w_in", "q_norm_g", "kv_norm_g", "w_uq", "w_ukv", "ret_gn_g", "ret_gn_b", "w_out",
           "ln1_g", "ln1_b", "w_gate", "w_up", "w_down", "ln2_g", "ln2_b")
SMALL_ROWS = 32

MESH = pl.DeviceIdType.MESH


def _pick(dim, cands):
    for c in cands:
        if dim % c == 0:
            return c
    return dim


HBM = pl.BlockSpec(memory_space=pltpu.HBM)


class _Side:
    def __init__(self, arrays, out_shape, scratch, start, finish):
        self.arrays, self.out_shape, self.scratch, self.start, self.finish = arrays, out_shape, scratch, start, finish


def _call(body, name, out_shape, grid, in_specs, out_specs, scratch=(), sem=None, side=None):
    params = pltpu.CompilerParams(dimension_semantics=sem if side is None else ("arbitrary",) * len(grid),
                                  vmem_limit_bytes=VMEM_LIMIT)
    if side is None:
        return pl.pallas_call(body, name=name, out_shape=out_shape, grid=grid, in_specs=in_specs, out_specs=out_specs,
                              scratch_shapes=list(scratch), compiler_params=params)
    single = not isinstance(out_shape, (list, tuple))
    outs = [out_shape] if single else list(out_shape)
    ospecs = [out_specs] if single else list(out_specs)
    cuts = [len(in_specs), len(side.arrays), len(outs), len(side.out_shape), len(scratch)]
    ends = [sum(cuts[:k + 1]) for k in range(len(cuts))]

    def hosted(*refs):
        ins, s_in, o, s_out, scr = (refs[a:b] for a, b in zip([0] + ends[:-1], ends))
        sems = refs[ends[-1]:]
        ids = [pl.program_id(a) for a in range(len(grid))]
        first = functools.reduce(jnp.logical_and, [i == 0 for i in ids])
        last = functools.reduce(jnp.logical_and, [i == g - 1 for i, g in zip(ids, grid)])

        @pl.when(first)
        def _():
            side.start(s_in, s_out, sems)

        body(*ins, *o, *scr)

        @pl.when(last)
        def _():
            side.finish(s_in, s_out, sems)

    call = pl.pallas_call(hosted, name=name, out_shape=outs + list(side.out_shape), grid=grid,
                          in_specs=list(in_specs) + [HBM] * len(side.arrays),
                          out_specs=ospecs + [HBM] * len(side.out_shape),
                          scratch_shapes=list(scratch) + list(side.scratch), compiler_params=params)

    def run(*args):
        res = call(*args, *side.arrays)
        return (res[0] if single else list(res[:len(outs)])), list(res[len(outs):])

    return run


def _rows(tm, w, col=0):
    return pl.BlockSpec((tm, w), lambda i: (i, col))


def _whole(shape):
    return pl.BlockSpec(shape, lambda i: (0,) * len(shape))


def _sds(shape, dtype):
    return jax.ShapeDtypeStruct(shape, dtype)


def _matmul(a, b, name, ta=False, tb=False, out_dtype=F32, side=None):
    (K, M) = a.shape if ta else a.shape[::-1]
    (N, Kb) = b.shape if tb else b.shape[::-1]
    assert K == Kb, (a.shape, b.shape, ta, tb)
    tm = _pick(M, (1024, 1408, 512, 256, 128))
    tn = _pick(N, (1024, 512, 256, 128))
    tk = _pick(K, (2816, 2560, 2048, 1024, 512, 256))
    nk = K // tk
    dn = (((0 if ta else 1,), (1 if tb else 0,)), ((), ()))

    def body(a_ref, b_ref, o_ref, acc_ref):
        k = pl.program_id(2)
        if nk == 1:
            o_ref[...] = lax.dot_general(a_ref[...].astype(BF16), b_ref[...].astype(BF16), dn,
                                         preferred_element_type=F32).astype(out_dtype)
        else:
            @pl.when(k == 0)
            def _():
                acc_ref[...] = jnp.zeros_like(acc_ref)

            acc_ref[...] += lax.dot_general(a_ref[...].astype(BF16), b_ref[...].astype(BF16), dn,
                                            preferred_element_type=F32)

            @pl.when(k == nk - 1)
            def _():
                o_ref[...] = acc_ref[...].astype(out_dtype)

    a_spec = pl.BlockSpec((tk, tm), lambda i, j, k: (k, i)) if ta else pl.BlockSpec((tm, tk), lambda i, j, k: (i, k))
    b_spec = pl.BlockSpec((tn, tk), lambda i, j, k: (j, k)) if tb else pl.BlockSpec((tk, tn), lambda i, j, k: (k, j))
    return _call(body, name, _sds((M, N), out_dtype), (M // tm, N // tn, nk), [a_spec, b_spec],
                 pl.BlockSpec((tm, tn), lambda i, j, k: (i, j)), scratch=[pltpu.VMEM((tm, tn), F32)],
                 sem=("parallel", "parallel", "arbitrary"), side=side)(a, b)


def _sigmoid(x):
    return 1.0 / (1.0 + jnp.exp(-x))


def _rope_group(r, c, sa, sb):
    return r * c + pltpu.roll(r, 32, 1) * sa + pltpu.roll(r, 96, 1) * sb


def _ln_fwd(xs, coefs, g, b, name, want_z):
    S, D = xs[0].shape
    tm = 512
    n = len(xs)

    def body(*refs):
        x_refs, g_ref, b_ref, outs = refs[:n], refs[n], refs[n + 1], refs[n + 2:]
        z = None
        for cf, r in zip(coefs, x_refs):
            t = r[...] if cf == 1.0 else cf * r[...]
            z = t if z is None else z + t
        mu = jnp.mean(z, axis=-1, keepdims=True)
        zc = z - mu
        var = jnp.mean(zc * zc, axis=-1, keepdims=True)
        y = zc * lax.rsqrt(var + LN_EPS) * g_ref[...] + b_ref[...]
        if want_z:
            outs[0][...] = z
        outs[-2][...] = y
        outs[-1][...] = y.astype(BF16)

    out_shape = [_sds((S, D), F32)] * (2 if want_z else 1) + [_sds((S, D), BF16)]
    return _call(body, name, out_shape, (S // tm,), [_rows(tm, D)] * n + [_whole((1, D))] * 2,
                 [_rows(tm, D)] * len(out_shape), sem=("parallel",))(*xs, g, b)


def _ln_bwd(dys, coefs, z, g, name):
    S, D = z.shape
    tm = 512
    n = len(dys)

    def body(*refs):
        dy_refs, z_ref, g_ref = refs[:n], refs[n], refs[n + 1]
        dz_ref, dzb_ref, dg_ref, db_ref = refs[n + 2:]
        dy = None
        for cf, r in zip(coefs, dy_refs):
            t = r[...] if cf == 1.0 else cf * r[...]
            dy = t if dy is None else dy + t
        zv = z_ref[...]
        mu = jnp.mean(zv, axis=-1, keepdims=True)
        zc = zv - mu
        var = jnp.mean(zc * zc, axis=-1, keepdims=True)
        rstd = lax.rsqrt(var + LN_EPS)
        xh = zc * rstd
        dyg = dy * g_ref[...]
        dz = rstd * (dyg - jnp.mean(dyg, axis=-1, keepdims=True) - xh * jnp.mean(dyg * xh, axis=-1, keepdims=True))
        dz_ref[...] = dz
        dzb_ref[...] = dz.astype(BF16)

        @pl.when(pl.program_id(0) == 0)
        def _():
            dg_ref[...] = jnp.zeros_like(dg_ref)
            db_ref[...] = jnp.zeros_like(db_ref)

        dg_ref[...] += jnp.sum(dy * xh, axis=0, keepdims=True)
        db_ref[...] += jnp.sum(dy, axis=0, keepdims=True)

    return _call(body, name, [_sds((S, D), F32), _sds((S, D), BF16), _sds((1, D), F32), _sds((1, D), F32)],
                 (S // tm,), [_rows(tm, D)] * (n + 1) + [_whole((1, D))],
                 [_rows(tm, D), _rows(tm, D), _whole((1, D)), _whole((1, D))], sem=("arbitrary",))(*dys, z, g)


def _rms(x, g):
    return x * lax.rsqrt(jnp.mean(x * x, axis=-1, keepdims=True) + RMS_EPS) * g


def _prep1(h, tabs, qg, kvg, name):
    S = h.shape[0]
    tm = 512
    cm, sam, sbm, cr, sr = tabs

    def body(h_ref, cm_ref, sam_ref, sbm_ref, cr_ref, sr_ref, qg_ref, kvg_ref,
             qn_ref, kvn_ref, kr_ref, rq_ref, rk_ref, rv_ref):
        qn_ref[...] = _rms(h_ref[:, 0:Q_LORA], qg_ref[...]).astype(BF16)
        kvn_ref[...] = _rms(h_ref[:, Q_LORA:Q_LORA + KV_LORA], kvg_ref[...]).astype(BF16)
        kr_ref[...] = _rope_group(h_ref[:, 768:896], cm_ref[...], sam_ref[...], sbm_ref[...])
        c, s = cr_ref[...], sr_ref[...]
        for hd in range(RET_HEADS):
            for src, dst, scale in ((MLA_IN, rq_ref, RET_SCALE), (MLA_IN + 1024, rk_ref, None)):
                t1 = h_ref[:, src + hd * 256:src + hd * 256 + 128]
                t2 = h_ref[:, src + hd * 256 + 128:src + hd * 256 + 256]
                o1, o2 = t1 * c - t2 * s, t2 * c + t1 * s
                if scale is not None:
                    o1, o2 = o1 * scale, o2 * scale
                dst[:, hd * 256:hd * 256 + 128] = o1.astype(BF16)
                dst[:, hd * 256 + 128:hd * 256 + 256] = o2.astype(BF16)
        rv_ref[...] = h_ref[:, MLA_IN + 2048:MLA_IN + 3072].astype(BF16)

    t128 = _rows(tm, LANES)
    return _call(body, name,
                 [_sds((S, Q_LORA), BF16), _sds((S, KV_LORA), BF16), _sds((S, LANES), F32),
                  _sds((S, 1024), BF16), _sds((S, 1024), BF16), _sds((S, 1024), BF16)],
                 (S // tm,),
                 [_rows(tm, D_IN_PAD), t128, t128, t128, t128, t128, _whole((1, Q_LORA)), _whole((1, KV_LORA))],
                 [_rows(tm, Q_LORA), _rows(tm, KV_LORA), t128, _rows(tm, 1024), _rows(tm, 1024), _rows(tm, 1024)],
                 sem=("parallel",))(h, cm, sam, sbm, cr, sr, qg, kvg)


def _prep1_bwd(dqn, dkvn, dkr, drq, drk, drv, drg, h, tabs, qg, kvg, name):
    S = h.shape[0]
    tm = 512
    cm, sam, sbm, cr, sr = tabs

    def rms_bwd(x, g, dy):
        r = lax.rsqrt(jnp.mean(x * x, axis=-1, keepdims=True) + RMS_EPS)
        dyg = dy * g
        dx = r * dyg - x * (r * r * r) * jnp.mean(dyg * x, axis=-1, keepdims=True)
        return dx, jnp.sum(dy * x * r, axis=0, keepdims=True)

    def body(dqn_ref, dkvn_ref, dkr_ref, drq_ref, drk_ref, drv_ref, drg_ref, h_ref,
             cm_ref, sam_ref, sbm_ref, cr_ref, sr_ref, qg_ref, kvg_ref, dh_ref, dqg_ref, dkvg_ref):
        dcq, dqg = rms_bwd(h_ref[:, 0:Q_LORA], qg_ref[...], dqn_ref[...])
        dckv, dkvg = rms_bwd(h_ref[:, Q_LORA:Q_LORA + KV_LORA], kvg_ref[...], dkvn_ref[...])
        dh_ref[:, 0:Q_LORA] = dcq.astype(BF16)
        dh_ref[:, Q_LORA:Q_LORA + KV_LORA] = dckv.astype(BF16)
        dh_ref[:, 768:896] = _rope_group(dkr_ref[...], cm_ref[...], -sam_ref[...], -sbm_ref[...]).astype(BF16)
        dh_ref[:, 896:1024] = jnp.zeros((tm, LANES), BF16)
        c, s = cr_ref[...], sr_ref[...]
        for hd in range(RET_HEADS):
            for src, dst, scale in ((drq_ref, MLA_IN, RET_SCALE), (drk_ref, MLA_IN + 1024, None)):
                d1 = src[:, hd * 256:hd * 256 + 128]
                d2 = src[:, hd * 256 + 128:hd * 256 + 256]
                if scale is not None:
                    d1, d2 = d1 * scale, d2 * scale
                dh_ref[:, dst + hd * 256:dst + hd * 256 + 128] = (d1 * c + d2 * s).astype(BF16)
                dh_ref[:, dst + hd * 256 + 128:dst + hd * 256 + 256] = (d2 * c - d1 * s).astype(BF16)
        dh_ref[:, MLA_IN + 2048:MLA_IN + 3072] = drv_ref[...].astype(BF16)
        dh_ref[:, MLA_IN + 3072:MLA_IN + 4096] = drg_ref[...].astype(BF16)

        @pl.when(pl.program_id(0) == 0)
        def _():
            dqg_ref[...] = jnp.zeros_like(dqg_ref)
            dkvg_ref[...] = jnp.zeros_like(dkvg_ref)

        dqg_ref[...] += dqg
        dkvg_ref[...] += dkvg

    t128 = _rows(tm, LANES)
    return _call(body, name,
                 [_sds((S, D_IN_PAD), BF16), _sds((1, Q_LORA), F32), _sds((1, KV_LORA), F32)],
                 (S // tm,),
                 [_rows(tm, Q_LORA), _rows(tm, KV_LORA), t128, _rows(tm, 1024), _rows(tm, 1024), _rows(tm, 1024),
                  _rows(tm, 1024), _rows(tm, MLA_IN), t128, t128, t128, t128, t128,
                  _whole((1, Q_LORA)), _whole((1, KV_LORA))],
                 [_rows(tm, D_IN_PAD), _whole((1, Q_LORA)), _whole((1, KV_LORA))],
                 sem=("arbitrary",))(dqn, dkvn, dkr, drq, drk, drv, drg, h, cm, sam, sbm, cr, sr, qg, kvg)


def _prep2(q, kv, kr, tabs, name):
    S = q.shape[0]
    tm = 512
    cm, sam, sbm = tabs[:3]

    def body(q_ref, kv_ref, kr_ref, cm_ref, sam_ref, sbm_ref, qo_ref, ko_ref, vo_ref):
        c, sa, sb = cm_ref[...], sam_ref[...], sbm_ref[...]
        krb = kr_ref[...].astype(BF16)
        ones = jnp.ones((tm, LANES), BF16)
        for hd in range(MLA_HEADS):
            o = hd * HEAD_PAD
            qo_ref[:, o:o + 128] = (q_ref[:, o:o + 128] * MLA_SCALE).astype(BF16)
            qo_ref[:, o + 128:o + 256] = (_rope_group(q_ref[:, o + 128:o + 256], c, sa, sb) * MLA_SCALE).astype(BF16)
            ko_ref[:, o:o + 128] = kv_ref[:, hd * 128:hd * 128 + 128].astype(BF16)
            ko_ref[:, o + 128:o + 256] = krb
            vo_ref[:, o:o + 128] = kv_ref[:, 1024 + hd * 128:1024 + hd * 128 + 128].astype(BF16)
            vo_ref[:, o + 128:o + 256] = ones

    t128 = _rows(tm, LANES)
    return _call(body, name, [_sds((S, 2048), BF16)] * 3, (S // tm,),
                 [_rows(tm, 2048), _rows(tm, 2048), t128, t128, t128, t128],
                 [_rows(tm, 2048)] * 3, sem=("parallel",))(q, kv, kr, cm, sam, sbm)


def _prep2_bwd(dqm, dkm, dvm, tabs, name):
    S = dqm.shape[0]
    tm = 512
    cm, sam, sbm = tabs[:3]

    def body(dq_ref, dk_ref, dv_ref, cm_ref, sam_ref, sbm_ref, dqo_ref, dkvo_ref, dkr_ref):
        c, sa, sb = cm_ref[...], -sam_ref[...], -sbm_ref[...]
        dkr = None
        for hd in range(MLA_HEADS):
            o = hd * HEAD_PAD
            dqo_ref[:, o:o + 128] = (dq_ref[:, o:o + 128] * MLA_SCALE).astype(BF16)
            dqo_ref[:, o + 128:o + 256] = (_rope_group(dq_ref[:, o + 128:o + 256], c, sa, sb) * MLA_SCALE).astype(BF16)
            dkvo_ref[:, hd * 128:hd * 128 + 128] = dk_ref[:, o:o + 128].astype(BF16)
            t = dk_ref[:, o + 128:o + 256]
            dkr = t if dkr is None else dkr + t
        dkvo_ref[:, 1024:2048] = dv_ref[...].astype(BF16)
        dkr_ref[...] = dkr

    t128 = _rows(tm, LANES)
    return _call(body, name, [_sds((S, 2048), BF16), _sds((S, 2048), BF16), _sds((S, LANES), F32)], (S // tm,),
                 [_rows(tm, 2048), _rows(tm, 2048), _rows(tm, 1024), t128, t128, t128],
                 [_rows(tm, 2048), _rows(tm, 2048), t128], sem=("parallel",))(dqm, dkm, dvm, cm, sam, sbm)


def _gn_gate(a, o, h, gg, gb, name):
    S = a.shape[0]
    tm = 512

    def body(a_ref, o_ref, rg_ref, gg_ref, gb_ref, mix_ref):
        mix_ref[:, 0:1024] = a_ref[...].astype(BF16)
        for hd in range(RET_HEADS):
            sl = slice(hd * 256, hd * 256 + 256)
            ov = o_ref[:, sl]
            mu = jnp.mean(ov, axis=-1, keepdims=True)
            oc = ov - mu
            var = jnp.mean(oc * oc, axis=-1, keepdims=True)
            y = oc * lax.rsqrt(var + GN_EPS) * gg_ref[:, sl] + gb_ref[:, sl]
            rg = rg_ref[:, sl]
            mix_ref[:, 1024 + hd * 256:1024 + hd * 256 + 256] = (rg * _sigmoid(rg) * y).astype(BF16)

    return _call(body, name, _sds((S, 2048), BF16), (S // tm,),
                 [_rows(tm, 1024), _rows(tm, 1024), _rows(tm, 1024, 4), _whole((1, 1024)), _whole((1, 1024))],
                 _rows(tm, 2048), sem=("parallel",))(a, o, h, gg, gb)


def _gn_gate_bwd(dmixin, o, h, gg, gb, name):
    S = o.shape[0]
    tm = 512

    def body(dr_ref, o_ref, rg_ref, gg_ref, gb_ref, do_ref, drg_ref, dgg_ref, dgb_ref):
        @pl.when(pl.program_id(0) == 0)
        def _():
            dgg_ref[...] = jnp.zeros_like(dgg_ref)
            dgb_ref[...] = jnp.zeros_like(dgb_ref)

        for hd in range(RET_HEADS):
            sl = slice(hd * 256, hd * 256 + 256)
            ov = o_ref[:, sl]
            mu = jnp.mean(ov, axis=-1, keepdims=True)
            oc = ov - mu
            var = jnp.mean(oc * oc, axis=-1, keepdims=True)
            rstd = lax.rsqrt(var + GN_EPS)
            xh = oc * rstd
            g = gg_ref[:, sl]
            y = xh * g + gb_ref[:, sl]
            rg = rg_ref[:, sl]
            sg = _sigmoid(rg)
            dr = dr_ref[:, sl]
            dy = dr * (rg * sg)
            drg_ref[:, sl] = dr * y * (sg * (1.0 + rg * (1.0 - sg)))
            dgg_ref[:, sl] += jnp.sum(dy * xh, axis=0, keepdims=True)
            dgb_ref[:, sl] += jnp.sum(dy, axis=0, keepdims=True)
            dxh = dy * g
            do = rstd * (dxh - jnp.mean(dxh, axis=-1, keepdims=True) - xh * jnp.mean(dxh * xh, axis=-1, keepdims=True))
            do_ref[:, sl] = do.astype(BF16)

    return _call(body, name,
                 [_sds((S, 1024), BF16), _sds((S, 1024), F32), _sds((1, 1024), F32), _sds((1, 1024), F32)],
                 (S // tm,),
                 [_rows(tm, 1024, 1), _rows(tm, 1024), _rows(tm, 1024, 4), _whole((1, 1024)), _whole((1, 1024))],
                 [_rows(tm, 1024), _rows(tm, 1024), _whole((1, 1024)), _whole((1, 1024))],
                 sem=("arbitrary",))(dmixin, o, h, gg, gb)


GU_BLOCK = D_FF // N_CHIPS


def _matmul_swiglu(x, w_gu, name, side=None):
    S, K = x.shape
    tm = _pick(S, (512, 256, 128))
    tn = 2 * GU_BLOCK

    def body(x_ref, w_ref, gu_ref, act_ref):
        r = jnp.dot(x_ref[...], w_ref[...], preferred_element_type=F32)
        g, u = r[:, :GU_BLOCK], r[:, GU_BLOCK:]
        gu_ref[...] = r.astype(BF16)
        act_ref[...] = (g * _sigmoid(g) * u).astype(BF16)

    return _call(body, name, [_sds((S, 2 * D_FF), BF16), _sds((S, D_FF), BF16)], (S // tm, N_CHIPS),
                 [pl.BlockSpec((tm, K), lambda i, j: (i, 0)), pl.BlockSpec((K, tn), lambda i, j: (0, j))],
                 [pl.BlockSpec((tm, tn), lambda i, j: (i, j)), pl.BlockSpec((tm, GU_BLOCK), lambda i, j: (i, j))],
                 sem=("parallel", "parallel"), side=side)(x, w_gu)


def _matmul_swiglu_bwd(df, w_down, gu, name, side=None):
    S, K = df.shape
    tm = _pick(S, (512, 256, 128))

    def body(df_ref, w_ref, gu_ref, o_ref):
        d = _dot_nt(df_ref[...], w_ref[...])
        g = gu_ref[:, :GU_BLOCK].astype(F32)
        u = gu_ref[:, GU_BLOCK:].astype(F32)
        sg = _sigmoid(g)
        o_ref[:, :GU_BLOCK] = (d * u * (sg * (1.0 + g * (1.0 - sg)))).astype(BF16)
        o_ref[:, GU_BLOCK:] = (d * (g * sg)).astype(BF16)

    gu_spec = pl.BlockSpec((tm, 2 * GU_BLOCK), lambda i, j: (i, j))
    return _call(body, name, _sds((S, 2 * D_FF), BF16), (S // tm, N_CHIPS),
                 [pl.BlockSpec((tm, K), lambda i, j: (i, 0)), pl.BlockSpec((GU_BLOCK, K), lambda i, j: (j, 0)), gu_spec],
                 gu_spec, sem=("parallel", "parallel"), side=side)(df, w_down, gu)


def _loss_head(y, target, name):
    S, D = y.shape
    tm = 512

    def body(y_ref, t_ref, dy_ref, acc_ref):
        e = y_ref[...] - t_ref[...]
        dy_ref[...] = e / D

        @pl.when(pl.program_id(0) == 0)
        def _():
            acc_ref[...] = jnp.zeros_like(acc_ref)

        acc_ref[...] += jnp.sum(e * e, axis=0, keepdims=True)

    return _call(body, name, [_sds((S, D), F32), _sds((1, D), F32)], (S // tm,), [_rows(tm, D), _rows(tm, D)],
                 [_rows(tm, D), _whole((1, D))], sem=("arbitrary",))(y, target)


def _chunk_mask(T):
    r = lax.shift_right_logical(lax.broadcasted_iota(jnp.int32, (T, T), 0), 6)
    c = lax.shift_right_logical(lax.broadcasted_iota(jnp.int32, (T, T), 1), 6)
    return r >= c


def _dot_nt(a, b):
    return lax.dot_general(a, b, (((1,), (1,)), ((), ())), preferred_element_type=F32)


def _dot_tn(a, b):
    return lax.dot_general(a, b, (((0,), (0,)), ((), ())), preferred_element_type=F32)


def _decay_tables(T):
    lg = jnp.log1p(-jnp.exp2(-5.0 - jnp.arange(RET_HEADS, dtype=F32)))
    idx = jnp.arange(T, dtype=F32)
    diff = idx[:, None] - idx[None, :]
    rel = jnp.exp(lg[:, None, None] * diff[None])
    cid = jnp.arange(T) // CHUNK
    mask = (cid[:, None] >= cid[None, :]).astype(F32)
    reld = jnp.exp(lg[:, None, None] * jnp.abs(diff)[None]) * mask[None]
    lgrow = jnp.broadcast_to(lg[:, None, None], (RET_HEADS, 1, LANES))
    return lgrow, rel, reld


def _attn_fwd(q, k, v, heads, dk, dv, softmax, name, tables=None, side=None):
    S = q.shape[0]
    T = ATT_BLOCK
    nq = S // T
    rep = T // LANES
    vw = 2 * dv if softmax else dv
    assert not softmax or dv == LANES

    def body(*refs):
        if softmax:
            q_ref, k_ref, v_ref, o_ref, lse_ref, m_sc, acc_sc = refs
        else:
            q_ref, k_ref, v_ref, lg_ref, rel_ref, reld_ref, o_ref, acc_sc = refs
        i = pl.program_id(1)
        qv = q_ref[...]

        def kv_block(j):
            rows = pl.ds(pl.multiple_of(j * T, T), T)
            return k_ref[rows, :], v_ref[rows, :]

        kb, vb = kv_block(i)
        s = _dot_nt(qv, kb)
        if softmax:
            s = jnp.where(_chunk_mask(T), s, NEG)
            m = jnp.max(s, axis=-1, keepdims=True)
            p = jnp.exp(s - m)
            m_sc[...] = jnp.broadcast_to(m, (T, LANES))
        else:
            p = s * reld_ref[0]
        acc_sc[...] = jnp.dot(p.astype(BF16), vb, preferred_element_type=F32)

        def scores(j):
            kb, vb = kv_block(j)
            return _dot_nt(qv, kb), vb

        def update(j, s, vb):
            if softmax:
                m_prev = m_sc[...]
                m_next = jnp.maximum(m_prev, jnp.max(s, axis=-1, keepdims=True))
                alpha = jnp.exp(m_prev - m_next)
                p = jnp.exp(s - jnp.tile(m_next, (1, rep)))
                m_sc[...] = m_next
                acc_sc[...] = acc_sc[...] * jnp.tile(alpha, (1, vw // LANES)) + jnp.dot(
                    p.astype(BF16), vb, preferred_element_type=F32)
            else:
                fac = jnp.exp(lg_ref[0] * ((i - j) * T).astype(F32))
                p = s * (rel_ref[0] * jnp.tile(fac, (1, rep)))
                acc_sc[...] += jnp.dot(p.astype(BF16), vb, preferred_element_type=F32)

        def pair(jj, carry):
            first, second = scores(2 * jj), scores(2 * jj + 1)
            update(2 * jj, *first)
            update(2 * jj + 1, *second)
            return carry

        lax.fori_loop(0, i // 2, pair, 0)

        @pl.when(i % 2 == 1)
        def _():
            update(i - 1, *scores(i - 1))

        if softmax:
            l = acc_sc[:, dv:]
            o_ref[...] = acc_sc[:, :dv] / l
            lse_ref[...] = m_sc[...] + jnp.log(l)
        else:
            o_ref[...] = acc_sc[...]

    in_specs = [pl.BlockSpec((T, dk), lambda h, i: (i, h)), pl.BlockSpec((S, dk), lambda h, i: (0, h)),
                pl.BlockSpec((S, vw), lambda h, i: (0, h))]
    o_spec = pl.BlockSpec((T, dv), lambda h, i: (i, h))
    if softmax:
        return _call(body, name, [_sds((S, heads * dv), F32), _sds((S, heads * LANES), F32)], (heads, nq), in_specs,
                     [o_spec, pl.BlockSpec((T, LANES), lambda h, i: (i, h))],
                     scratch=[pltpu.VMEM((T, LANES), F32), pltpu.VMEM((T, vw), F32)],
                     sem=("parallel", "arbitrary"), side=side)(q, k, v)
    lgrow, rel, reld = tables
    in_specs += [pl.BlockSpec((1, 1, LANES), lambda h, i: (h, 0, 0)), pl.BlockSpec((1, T, T), lambda h, i: (h, 0, 0)),
                 pl.BlockSpec((1, T, T), lambda h, i: (h, 0, 0))]
    return _call(body, name, _sds((S, heads * dv), F32), (heads, nq), in_specs, o_spec,
                 scratch=[pltpu.VMEM((T, dv), F32)], sem=("parallel", "arbitrary"), side=side)(q, k, v, lgrow, rel, reld)


def _attn_bwd(q, k, v, do, heads, dk, dv, softmax, name, o=None, lse=None, tables=None, side=None):
    S = q.shape[0]
    T = ATT_BLOCK
    nq = S // T
    rep = T // LANES

    def body(*refs):
        if softmax:
            q_ref, k_ref, v_ref, do_ref, o_ref, lse_ref, dq_ref, dk_ref, dv_ref, dq_sc = refs
        else:
            q_ref, k_ref, v_ref, do_ref, lg_ref, rel_ref, reld_ref, dq_ref, dk_ref, dv_ref, dq_sc = refs
        i = pl.program_id(1)

        @pl.when(i == 0)
        def _():
            dk_ref[...] = jnp.zeros_like(dk_ref)
            dv_ref[...] = jnp.zeros_like(dv_ref)

        qv = q_ref[...]
        dof = do_ref[...].astype(F32)
        dov = dof.astype(BF16)
        if softmax:
            delta = jnp.sum(dof * o_ref[...], axis=-1, keepdims=True)
            lse_t = jnp.tile(lse_ref[...], (1, rep))
        dq_sc[...] = jnp.zeros_like(dq_sc)

        def products(j):
            rows = pl.ds(pl.multiple_of(j * T, T), T)
            kb = k_ref[rows, :]
            return rows, kb, _dot_nt(qv, kb), _dot_nt(dov, v_ref[rows, :])

        def block(j, diagonal, rows, kb, s, dp):
            if softmax:
                if diagonal:
                    s = jnp.where(_chunk_mask(T), s, NEG)
                p = jnp.exp(s - lse_t)
                ds = p * (dp - delta)
            else:
                if diagonal:
                    dec = reld_ref[0]
                else:
                    fac = jnp.exp(lg_ref[0] * ((i - j) * T).astype(F32))
                    dec = rel_ref[0] * jnp.tile(fac, (1, rep))
                p = s * dec
                ds = dp * dec
            dsb = ds.astype(BF16)
            dv_ref[rows, :] += _dot_tn(p.astype(BF16), dov)
            dk_ref[rows, :] += _dot_tn(dsb, qv)
            dq_sc[...] += jnp.dot(dsb, kb, preferred_element_type=F32)

        block(i, True, *products(i))

        def pair(jj, carry):
            first, second = products(2 * jj), products(2 * jj + 1)
            block(2 * jj, False, *first)
            block(2 * jj + 1, False, *second)
            return carry

        lax.fori_loop(0, i // 2, pair, 0)

        @pl.when(i % 2 == 1)
        def _():
            block(i - 1, False, *products(i - 1))

        dq_ref[...] = dq_sc[...]

    qspec = pl.BlockSpec((T, dk), lambda h, i: (i, h))
    kspec = pl.BlockSpec((S, dk), lambda h, i: (0, h))
    vspec = pl.BlockSpec((S, dv), lambda h, i: (0, h))
    dospec = pl.BlockSpec((T, dv), lambda h, i: (i, h))
    in_specs = [qspec, kspec, vspec, dospec]
    args = [q, k, v, do]
    if softmax:
        in_specs[2] = pl.BlockSpec((S, dv), lambda h, i: (0, 2 * h))
        in_specs += [dospec, pl.BlockSpec((T, LANES), lambda h, i: (i, h))]
        args += [o, lse]
    else:
        in_specs += [pl.BlockSpec((1, 1, LANES), lambda h, i: (h, 0, 0)),
                     pl.BlockSpec((1, T, T), lambda h, i: (h, 0, 0)), pl.BlockSpec((1, T, T), lambda h, i: (h, 0, 0))]
        args += list(tables)
    return _call(body, name, [_sds((S, heads * dk), F32), _sds((S, heads * dk), F32), _sds((S, heads * dv), F32)],
                 (heads, nq), in_specs, [qspec, kspec, vspec], scratch=[pltpu.VMEM((T, dk), F32)],
                 sem=("parallel", "arbitrary"), side=side)(*args)


def _rope_tables(pos):
    def tables(dim):
        inv_freq = ROPE_THETA ** (-jnp.arange(0, dim, 2, dtype=F32) / dim)
        ang = pos.astype(F32)[:, None] * inv_freq
        return jnp.cos(ang), jnp.sin(ang)

    cm, sm = tables(ROPE)
    S = pos.shape[0]
    z32, z64 = jnp.zeros((S, 32), F32), jnp.zeros((S, 64), F32)
    cr, sr = tables(RET_DK)
    return (jnp.concatenate([cm, cm, z64], 1), jnp.concatenate([z32, sm, z64], 1),
            jnp.concatenate([-sm, z32, z64], 1), cr, sr)


def _row(v):
    return v.reshape(1, -1).astype(F32)


def _local_step(x, pos, target, pipe, P):
    tabs = _rope_tables(pos)
    dtabs = _decay_tables(ATT_BLOCK)
    xf, xb = _ln_fwd([x], [1.0], _row(P["ln_in_g"]), _row(P["ln_in_b"]), "ln_in", False)
    pipe.gather_first()
    saved = []
    for l in range(DEPTH):
        w = functools.partial(pipe.weight, l)
        t = f"_l{l}"
        h = pipe.run(_matmul, "mm_h" + t, xb, w("w_in"))
        qn, kvn, kr, rq, rk, rv = _prep1(h, tabs, _row(P["q_norm_g"][l]), _row(P["kv_norm_g"][l]), "prep1" + t)
        q = _matmul(qn, w("w_uq"), "mm_q" + t)
        kv = _matmul(kvn, w("w_ukv"), "mm_kv" + t)
        qm, km, vm = _prep2(q, kv, kr, tabs, "prep2" + t)
        a, lse = pipe.run(_attn_fwd, "mla_fwd" + t, qm, km, vm, MLA_HEADS, HEAD_PAD, VDIM, True)
        o = pipe.run(_attn_fwd, "ret_fwd" + t, rq, rk, rv, RET_HEADS, RET_DK, RET_DV, False, tables=dtabs)
        mixin = _gn_gate(a, o, h, _row(P["ret_gn_g"][l]), _row(P["ret_gn_b"][l]), "gn_gate" + t)
        mix = _matmul(mixin, w("w_out"), "mm_mix" + t)
        z1, x1f, x1b = _ln_fwd([xf, mix], [ALPHA, 1.0], _row(P["ln1_g"][l]), _row(P["ln1_b"][l]), "ln1" + t, True)
        gu, act = pipe.run(_matmul_swiglu, "mm_gu" + t, x1b, w("w_gu"))
        f = pipe.run(_matmul, "mm_down" + t, act, w("w_down"))
        z2, x2f, x2b = _ln_fwd([x1f, f], [ALPHA, 1.0], _row(P["ln2_g"][l]), _row(P["ln2_b"][l]), "ln2" + t, True)
        saved.append(dict(xb=xb, h=h, qn=qn, kvn=kvn, rq=rq, rk=rk, rv=rv, qm=qm, km=km, vm=vm, a=a, lse=lse, o=o,
                          mixin=mixin, z1=z1, x1b=x1b, gu=gu, act=act, z2=z2))
        xf, xb = x2f, x2b

    dy, sqerr = _loss_head(xf, target, "loss_head")
    dP = {}
    dys, coefs = [dy], [1.0]
    for l in reversed(range(DEPTH)):
        w, sv = functools.partial(pipe.weight, l), saved[l]
        t = f"_l{l}"
        dz2, dz2b, dg, db = _ln_bwd(dys, coefs, sv["z2"], _row(P["ln2_g"][l]), "ln2_bwd" + t)
        dP[("ln2_g", l)], dP[("ln2_b", l)] = dg, db
        pipe.reduce(l, w_down=pipe.run(_matmul, "mm_dw_down" + t, sv["act"], dz2b, ta=True, out_dtype=BF16))
        dgu = pipe.run(_matmul_swiglu_bwd, "mm_dact" + t, dz2b, w("w_down"), sv["gu"])
        pipe.reduce(l, w_gu=pipe.run(_matmul, "mm_dw_gu" + t, sv["x1b"], dgu, ta=True, out_dtype=BF16))
        dx1 = pipe.run(_matmul, "mm_dx1" + t, dgu, w("w_gu"), tb=True)
        dz1, dz1b, dg, db = _ln_bwd([dz2, dx1], [ALPHA, 1.0], sv["z1"], _row(P["ln1_g"][l]), "ln1_bwd" + t)
        dP[("ln1_g", l)], dP[("ln1_b", l)] = dg, db
        pipe.reduce(l, w_out=_matmul(sv["mixin"], dz1b, "mm_dw_out" + t, ta=True, out_dtype=BF16))
        dmixin = pipe.run(_matmul, "mm_dmixin" + t, dz1b, w("w_out"), tb=True)
        do, drg, dgg, dgb = _gn_gate_bwd(dmixin, sv["o"], sv["h"], _row(P["ret_gn_g"][l]), _row(P["ret_gn_b"][l]),
                                         "gn_gate_bwd" + t)
        dP[("ret_gn_g", l)], dP[("ret_gn_b", l)] = dgg, dgb
        drq, drk, drv = pipe.run(_attn_bwd, "ret_bwd" + t, sv["rq"], sv["rk"], sv["rv"], do, RET_HEADS, RET_DK, RET_DV,
                                 False, tables=dtabs)
        dqm, dkm, dvm = pipe.run(_attn_bwd, "mla_bwd" + t, sv["qm"], sv["km"], sv["vm"], dmixin, MLA_HEADS, HEAD_PAD,
                                 VDIM, True, o=sv["a"], lse=sv["lse"])
        dq, dkv, dkr = _prep2_bwd(dqm, dkm, dvm, tabs, "prep2_bwd" + t)
        g_uq = _matmul(sv["qn"], dq, "mm_dw_uq" + t, ta=True, out_dtype=BF16)
        dqn = _matmul(dq, w("w_uq"), "mm_dqn" + t, tb=True)
        g_ukv = _matmul(sv["kvn"], dkv, "mm_dw_ukv" + t, ta=True, out_dtype=BF16)
        dkvn = _matmul(dkv, w("w_ukv"), "mm_dkvn" + t, tb=True)
        dh, dqg, dkvg = _prep1_bwd(dqn, dkvn, dkr, drq, drk, drv, drg, sv["h"], tabs, _row(P["q_norm_g"][l]),
                                   _row(P["kv_norm_g"][l]), "prep1_bwd" + t)
        dP[("q_norm_g", l)], dP[("kv_norm_g", l)] = dqg, dkvg
        pipe.reduce(l, w_uq=g_uq, w_ukv=g_ukv,
                    w_in=pipe.run(_matmul, "mm_dw_in" + t, sv["xb"], dh, ta=True, out_dtype=BF16))
        dxl = pipe.run(_matmul, "mm_dxl" + t, dh, w("w_in"), tb=True)
        dys, coefs = [dz1, dxl], [ALPHA, 1.0]
    grad_x, _, dg, db = _ln_bwd(dys, coefs, x, _row(P["ln_in_g"]), "ln_in_bwd")
    dP[("ln_in_g", None)], dP[("ln_in_b", None)] = dg, db
    return sqerr, grad_x, dP


INTERNAL_OF = {"w_in": ("w_in",), "w_uq": ("w_uq",), "w_ukv": ("w_ukv",), "w_out": ("w_out",),
               "w_gu": ("w_gate", "w_up"), "w_down": ("w_down",)}
ROW_PIECES = {"w_up": 1024}


def _internal_weight(name, *blocks):
    cat = lambda parts: jnp.concatenate(parts, axis=1)
    cols = lambda b: cat([b[j] for j in range(N_CHIPS)])
    b = blocks[0]
    if name in ("w_out", "w_down"):
        return b.reshape(-1, b.shape[-1])
    if name == "w_gu":
        return cat([blk[j] for j in range(N_CHIPS) for blk in blocks])
    if name == "w_in":
        return cat([b[0][:, :MLA_IN_USED], jnp.zeros((D_MODEL, MLA_IN - MLA_IN_USED), BF16), b[0][:, MLA_IN_USED:]]
                   + [b[j] for j in range(1, N_CHIPS)])
    if name == "w_uq":
        uq, hw = cols(b), NOPE + ROPE
        pad = jnp.zeros((Q_LORA, HEAD_PAD - hw), BF16)
        return cat([p for h in range(MLA_HEADS) for p in (uq[:, h * hw:(h + 1) * hw], pad)])
    ukv = cols(b)
    return cat([ukv[:, 256 * h:256 * h + NOPE] for h in range(MLA_HEADS)]
               + [ukv[:, 256 * h + NOPE:256 * h + 256] for h in range(MLA_HEADS)])


def _grad_shards(name, g):
    cat = lambda parts: jnp.concatenate(parts, axis=1)
    if name in ("w_out", "w_down"):
        return {name: g.reshape(N_CHIPS, -1, g.shape[-1])}
    if name == "w_gu":
        return {"w_gate": _ColBlocks(g, 0), "w_up": _ColBlocks(g, 1)}
    if name == "w_in":
        ci, shift = BIG_SHARD["w_in"][1], MLA_IN - MLA_IN_USED
        return {name: [cat([g[:, :MLA_IN_USED], g[:, MLA_IN:ci + shift]])]
                + [g[:, ci * j + shift:ci * (j + 1) + shift] for j in range(1, N_CHIPS)]}
    if name == "w_uq":
        cq = NOPE + ROPE
        return {name: [cat([g[:, HEAD_PAD * h:HEAD_PAD * h + cq] for h in (2 * j, 2 * j + 1)]) for j in range(N_CHIPS)]}
    return {name: [cat([g[:, o + NOPE * h:o + NOPE * (h + 1)] for h in (2 * j, 2 * j + 1) for o in (0, MLA_HEADS * NOPE)])
                   for j in range(N_CHIPS)]}


def _small_layout(P):
    out, at = {}, 0
    for n in SMALL:
        out[n] = (at, P[n].size)
        at += P[n].size
    return out, at


def _flatten_small(P, last):
    v = jnp.concatenate([P[n].reshape(-1).astype(F32) for n in SMALL] + [last.reshape(-1).astype(F32)])
    return jnp.pad(v, (0, SMALL_ROWS * FLAT_W - v.size)).reshape(SMALL_ROWS, FLAT_W)


def _place():
    return lax.axis_index("x"), lax.axis_index("y"), lax.axis_index("c")


def _other_chips(x, y):
    return [(1 - x, y), (x, 1 - y), (1 - x, 1 - y)]


def _rcopy(src, dst, ssem, rsem, dev):
    return pltpu.make_async_remote_copy(src_ref=src, dst_ref=dst, send_sem=ssem, recv_sem=rsem, device_id=dev,
                                        device_id_type=MESH)


def _comm_call(body, name, out_shape, n_in, scratch):
    many = isinstance(out_shape, (list, tuple))
    return pl.pallas_call(body, name=name, out_shape=out_shape, in_specs=[HBM] * n_in,
                          out_specs=[HBM] * len(out_shape) if many else HBM, scratch_shapes=scratch)


def _half(ref, which):
    rows = ref.shape[0] // 2
    return ref.at[pl.ds(pl.multiple_of(which * rows, 16), rows)]


def _dma_sems(n):
    return pltpu.SemaphoreType.DMA((n,))


def _allgather_side(ws):
    k = len(ws)

    def peers():
        x, y, c = _place()
        return c, 2 * x + y, (x, y, 1 - c), [(n, t, cx, cy) for n in range(k) for t, (cx, cy) in enumerate(_other_chips(x, y))]

    def outgoing(w_refs, g_refs, sems):
        ssem, rsem, _, _, ossem, orsem = sems
        c, j, sib, nt = peers()
        owns = [_rcopy(w_refs[n], g_refs[n].at[j], ossem.at[n], orsem.at[n], sib) for n in range(k)]
        sends = [_rcopy(_half(w_refs[n], c), _half(g_refs[n].at[j], c), ssem.at[3 * n + t], rsem.at[3 * n + t],
                        (cx, cy, c)) for n, t, cx, cy in nt]
        return owns, sends

    def incoming(g_refs, sems):
        ssem, rsem, fssem, frsem, _, _ = sems
        c, _, sib, nt = peers()
        landed, passed, relayed = [], [], []
        for n, t, cx, cy in nt:
            mine, other = (_half(g_refs[n].at[2 * cx + cy], h) for h in (c, 1 - c))
            landed.append(_rcopy(mine, mine, ssem.at[3 * n + t], rsem.at[3 * n + t], (cx, cy, c)))
            passed.append(_rcopy(mine, mine, fssem.at[3 * n + t], frsem.at[3 * n + t], sib))
            relayed.append(_rcopy(other, other, fssem.at[3 * n + t], frsem.at[3 * n + t], sib))
        return landed, passed, relayed

    def start(w_refs, g_refs, sems):
        owns, sends = outgoing(w_refs, g_refs, sems)
        for cp in sends + owns:
            cp.start()

    def finish(w_refs, g_refs, sems):
        owns, sends = outgoing(w_refs, g_refs, sems)
        landed, passed, relayed = incoming(g_refs, sems)
        for got, on in zip(landed, passed):
            got.wait_recv()
            on.start()
        for cp in relayed:
            cp.wait_recv()
        for cp in owns:
            cp.wait()
        for cp in sends + passed:
            cp.wait_send()

    return _Side(list(ws), [_sds((N_CHIPS,) + w.shape, w.dtype) for w in ws],
                 [_dma_sems(3 * k)] * 4 + [_dma_sems(k)] * 2, start, finish)


def _exchange_side(parts):
    k = len(parts)

    def copies(p_refs, rcv_refs, sems):
        ssem, rsem = sems
        x, y, c = _place()
        return [_rcopy(p_refs[n].at[2 * cx + cy], rcv_refs[n].at[t], ssem.at[3 * n + t], rsem.at[3 * n + t], (cx, cy, c))
                for n in range(k) for t, (cx, cy) in enumerate(_other_chips(x, y))]

    def start(p_refs, rcv_refs, sems):
        for cp in copies(p_refs, rcv_refs, sems):
            cp.start()

    def finish(p_refs, rcv_refs, sems):
        for cp in copies(p_refs, rcv_refs, sems):
            cp.wait()

    return _Side(list(parts), [_sds((3,) + p.shape[1:], p.dtype) for p in parts], [_dma_sems(3 * k)] * 2, start, finish)


def _run_side(side, name):
    k_in, k_out = len(side.arrays), len(side.out_shape)

    def body(*refs):
        parts = refs[:k_in], refs[k_in:k_in + k_out], refs[k_in + k_out:]
        side.start(*parts)
        side.finish(*parts)

    return _comm_call(body, name, list(side.out_shape), k_in, list(side.scratch))(*side.arrays)


def _sibling_side(arrays, out_shape, n_copies, copies):
    def start(in_refs, out_refs, sems):
        for cp in copies(in_refs, out_refs, sems):
            cp.start()

    def finish(in_refs, out_refs, sems):
        for cp in copies(in_refs, out_refs, sems):
            cp.wait()

    return _Side(list(arrays), out_shape, [_dma_sems(n_copies)] * 2, start, finish)


class _ColBlocks:
    def __init__(self, array, off):
        self.array, self.off, self.dtype = array, off, array.dtype
        self.shape = (N_CHIPS, array.shape[0], GU_BLOCK)

    def block(self, ref, jj):
        return ref.at[:, pl.ds((2 * jj + self.off) * GU_BLOCK, GU_BLOCK)]


def _swap_side(gds):
    k = len(gds)

    def copies(gd_refs, out_refs, sems):
        ssem, rsem = sems
        x, y, c = _place()
        blocks = [[g.block(gd_refs[n], jj) if isinstance(g, _ColBlocks) else gd_refs[n].at[jj] for jj in range(N_CHIPS)]
                  for n, g in enumerate(gds)]
        return [_rcopy(_half(blocks[n][jj], 1 - c), out_refs[n].at[jj], ssem.at[N_CHIPS * n + jj],
                       rsem.at[N_CHIPS * n + jj], (x, y, 1 - c)) for n in range(k) for jj in range(N_CHIPS)]

    return _sibling_side([g.array if isinstance(g, _ColBlocks) else g for g in gds],
                         [_sds((N_CHIPS, g.shape[1] // 2, g.shape[2]), g.dtype) for g in gds], N_CHIPS * k, copies)


def _share_side(reds):
    k = len(reds)

    def copies(r_refs, out_refs, sems):
        ssem, rsem = sems
        x, y, c = _place()
        return [_rcopy(r_refs[n], out_refs[n], ssem.at[n], rsem.at[n], (x, y, 1 - c)) for n in range(k)]

    return _sibling_side(reds, [_sds(r.shape, r.dtype) for r in reds], k, copies)


def _join_sides(sides):
    if len(sides) == 1:
        return sides[0]
    cuts = [(len(s.arrays), len(s.out_shape), len(s.scratch)) for s in sides]

    def each(method, in_refs, out_refs, sems):
        a = o = m = 0
        for s, (ka, ko, km) in zip(sides, cuts):
            getattr(s, method)(in_refs[a:a + ka], out_refs[o:o + ko], sems[m:m + km])
            a, o, m = a + ka, o + ko, m + km

    return _Side([x for s in sides for x in s.arrays], [x for s in sides for x in s.out_shape],
                 [x for s in sides for x in s.scratch], functools.partial(each, "start"), functools.partial(each, "finish"))


def _allreduce_small(small):
    def body(s_ref, all_ref, sssem, srsem, lsem):
        x, y, c = _place()
        me = 4 * x + 2 * y + c
        own = pltpu.make_async_copy(s_ref, all_ref.at[me], lsem)
        own.start()
        cps = []
        for r in range(1, 8):
            fx, fy, fc = (r >> 2) & 1, (r >> 1) & 1, r & 1
            px, py, pc = (1 - x if fx else x, 1 - y if fy else y, 1 - c if fc else c)
            peer = 4 * px + 2 * py + pc
            send = _rcopy(s_ref, all_ref.at[me], sssem.at[r - 1], srsem.at[me], (px, py, pc))
            send.start()
            cps.append((send, _rcopy(s_ref, all_ref.at[peer], sssem.at[r - 1], srsem.at[peer], (px, py, pc))))
        for send, recv in cps:
            send.wait_send()
            recv.wait_recv()
        own.wait()

    return _comm_call(body, "allreduce_small", [_sds((8,) + small.shape, small.dtype)], 1,
                      [pltpu.SemaphoreType.DMA((7,)), pltpu.SemaphoreType.DMA((8,)), pltpu.SemaphoreType.DMA(())])(small)[0]


def _add_pair(gd, got, c, name):
    _, R, W = got.shape
    tm = _pick(R, (512, 256, 128, 64))
    nb = R // tm

    def body(c_ref, a_ref, b_ref, o_ref):
        o_ref[...] = (a_ref[...].astype(F32) + b_ref[...].astype(F32)).astype(o_ref.dtype)

    if isinstance(gd, _ColBlocks):
        off = gd.off
        own = pl.BlockSpec((tm, W), lambda j, i, c_ref: (c_ref[0] * nb + i, 2 * j + off))
        gd = gd.array
    else:
        own = pl.BlockSpec((None, tm, W), lambda j, i, c_ref: (j, c_ref[0] * nb + i, 0))
    grid_spec = pltpu.PrefetchScalarGridSpec(
        num_scalar_prefetch=1, grid=(N_CHIPS, nb),
        in_specs=[own, pl.BlockSpec((None, tm, W), lambda j, i, c_ref: (j, i, 0))],
        out_specs=pl.BlockSpec((None, tm, W), lambda j, i, c_ref: (j, i, 0)))
    return pl.pallas_call(body, name=name, grid_spec=grid_spec, out_shape=_sds((N_CHIPS, R, W), gd.dtype),
                          compiler_params=pltpu.CompilerParams(dimension_semantics=("parallel", "parallel"),
                                                               vmem_limit_bytes=VMEM_LIMIT))(c, gd, got)


def _add_chips(part, rcv, j, name):
    _, R, W = part.shape
    tm = _pick(R, (512, 256, 128, 64))

    def body(j_ref, p_ref, r0_ref, r1_ref, r2_ref, o_ref):
        o_ref[...] = ((p_ref[...].astype(F32) + r0_ref[...].astype(F32)) + r1_ref[...].astype(F32)) + r2_ref[...].astype(F32)

    def slot(t):
        return pl.BlockSpec((None, tm, W), lambda i, j_ref: (t, i, 0))

    grid_spec = pltpu.PrefetchScalarGridSpec(
        num_scalar_prefetch=1, grid=(R // tm,),
        in_specs=[pl.BlockSpec((None, tm, W), lambda i, j_ref: (j_ref[0], i, 0)), slot(0), slot(1), slot(2)],
        out_specs=pl.BlockSpec((tm, W), lambda i, j_ref: (i, 0)))
    return pl.pallas_call(body, name=name, grid_spec=grid_spec, out_shape=_sds((R, W), F32),
                          compiler_params=pltpu.CompilerParams(dimension_semantics=("parallel",),
                                                               vmem_limit_bytes=VMEM_LIMIT))(j, part, rcv, rcv, rcv)


def _sum_small(allsmall):
    _, R, W = allsmall.shape

    def body(a_ref, o_ref):
        acc = a_ref[0]
        for d in range(1, 8):
            acc = acc + a_ref[d]
        o_ref[...] = acc

    return _call(body, "sum_small", _sds((R, W), F32), (1,), [_whole((8, R, W))], _whole((R, W)),
                 sem=("arbitrary",))(allsmall)


def _adamw(w, g, m, v, name):
    R, C = w.shape
    tm = _pick(R, (256, 128, 64, 32, 8))

    def body(w_ref, g_ref, m_ref, v_ref, d_ref, mo_ref, vo_ref):
        gv = g_ref[...]
        mn = ADAM_B1 * m_ref[...] + (1.0 - ADAM_B1) * gv
        vn = ADAM_B2 * v_ref[...] + (1.0 - ADAM_B2) * (gv * gv)
        m_hat = mn / (1.0 - ADAM_B1 ** ADAM_STEP)
        v_hat = vn / (1.0 - ADAM_B2 ** ADAM_STEP)
        d_ref[...] = -ADAM_LR * (m_hat / (jnp.sqrt(v_hat) + ADAM_EPS) + ADAM_WD * w_ref[...])
        mo_ref[...] = mn
        vo_ref[...] = vn

    spec = _rows(tm, C)
    return _call(body, name, [_sds((R, C), F32)] * 3, (R // tm,), [spec] * 4, [spec] * 3, sem=("parallel",))(w, g, m, v)


def _adamw_layer(c, w, m, v, mine, other, l, prev, name):
    _, R, C = w.shape
    half = R // 2
    tm = _pick(half, (256, 128, 64))
    nbh = half // tm

    def body(c_ref, w_ref, m_ref, v_ref, a_ref, b_ref, *rest):
        g_ref, d_ref, mo_ref, vo_ref = rest[-4:]
        gv = jnp.where(pl.program_id(0) // nbh == c_ref[0], a_ref[...], b_ref[...])
        mn = ADAM_B1 * m_ref[...] + (1.0 - ADAM_B1) * gv
        vn = ADAM_B2 * v_ref[...] + (1.0 - ADAM_B2) * (gv * gv)
        m_hat = mn / (1.0 - ADAM_B1 ** ADAM_STEP)
        v_hat = vn / (1.0 - ADAM_B2 ** ADAM_STEP)
        g_ref[...] = gv
        d_ref[...] = -ADAM_LR * (m_hat / (jnp.sqrt(v_hat) + ADAM_EPS) + ADAM_WD * w_ref[...])
        mo_ref[...] = mn
        vo_ref[...] = vn

    layer = pl.BlockSpec((None, tm, C), lambda i, c_ref: (l, i, 0))
    halfspec = pl.BlockSpec((tm, C), lambda i, c_ref: (i % nbh, 0))
    n_prev = 0 if prev is None else 4
    grid_spec = pltpu.PrefetchScalarGridSpec(
        num_scalar_prefetch=1, grid=(R // tm,),
        in_specs=[layer] * 3 + [halfspec] * 2 + [pl.BlockSpec(memory_space=pl.ANY)] * n_prev,
        out_specs=[layer] * 4)
    return pl.pallas_call(body, name=name, grid_spec=grid_spec, out_shape=[_sds(w.shape, F32)] * 4,
                          input_output_aliases={6 + k: k for k in range(n_prev)},
                          compiler_params=pltpu.CompilerParams(dimension_semantics=("parallel",),
                                                               vmem_limit_bytes=VMEM_LIMIT))(
        c, w, m, v, mine, other, *(prev or ()))


FIRST_GATHER = ("w_in", "w_uq", "w_ukv")
G_DOWN, G_GU, G_OUT, G_IN = ("w_down",), ("w_gate", "w_up"), ("w_out",), ("w_uq", "w_ukv", "w_in")


def _backward_jobs(l):
    t = f"_l{l}"
    return {"mm_dact" + t: [("swap", l, G_DOWN)], "mm_dw_gu" + t: [("exchange", l, G_DOWN)],
            "mm_dx1" + t: [("swap", l, G_GU), ("share", l, G_DOWN)], "mm_dmixin" + t: [("swap", l, G_OUT)],
            "ret_bwd" + t: [("exchange", l, ("w_gate",))],
            "mla_bwd" + t: [("exchange", l, ("w_up", "w_out")), ("share", l, ("w_gate",))],
            "mm_dw_in" + t: [("share", l, ("w_up", "w_out"))]}


JOBS = {
    "mm_h_l0": [("gather", 0, ("w_up@a",))], "mla_fwd_l0": [("gather", 0, ("w_gate", "w_out"))],
    "ret_fwd_l0": [("gather", 0, ("w_up@b",))],
    "mm_gu_l0": [("gather", 0, ("w_down",)), ("gather", 1, ("w_uq", "w_ukv"))],
    "mm_down_l0": [("gather", 1, ("w_in",))], "mm_h_l1": [("gather", 1, ("w_up@a",))],
    "mla_fwd_l1": [("gather", 1, ("w_gate", "w_out"))], "ret_fwd_l1": [("gather", 1, ("w_up@b",))],
    "mm_gu_l1": [("gather", 1, ("w_down",))],
    **_backward_jobs(1), **_backward_jobs(0),
    "mm_dxl_l1": [("swap", 1, G_IN)],
    "mm_dx1_l0": [("swap", 0, G_GU), ("share", 0, G_DOWN), ("exchange", 1, G_IN)],
    "ret_bwd_l0": [("exchange", 0, ("w_gate",)), ("share", 1, G_IN)], "mm_dxl_l0": [("exchange", 0, G_IN)]}
PLANNED = {job for jobs in JOBS.values() for job in jobs}


class _Pipeline:
    def __init__(self, own, Wt, Mo, Vo, core, chip):
        self.own, self.Wt, self.Mo, self.Vo, self.core, self.chip = own, Wt, Mo, Vo, core, chip
        self.blocks, self.whole, self.gds, self.parts, self.reds = {}, {}, {}, {}, {}
        self.results = {n: None for n in BIG}

    def gather_first(self):
        job = ("gather", 0, FIRST_GATHER)
        self._done(*job, _run_side(self._side(*job), "allgather_first"))

    def _gathered(self, l, n):
        if n in ROW_PIECES:
            return jnp.concatenate([self.blocks[(l, n + "@a")], self.blocks[(l, n + "@b")]], axis=1)
        return self.blocks[(l, n)]

    def weight(self, l, name):
        if (l, name) not in self.whole:
            self.whole[(l, name)] = _internal_weight(name, *[self._gathered(l, n) for n in INTERNAL_OF[name]])
        return self.whole[(l, name)]

    def run(self, fn, name, *args, **kw):
        jobs = JOBS.get(name, ())
        if not jobs:
            return fn(*args, name=name, **kw)
        sides = [self._side(*job) for job in jobs]
        out, res = fn(*args, name=name, side=_join_sides(sides), **kw)
        for job, side in zip(jobs, sides):
            k = len(side.out_shape)
            self._done(*job, res[:k])
            res = res[k:]
        return out

    def reduce(self, l, **grads):
        shards = {}
        for name, g in grads.items():
            shards.update(_grad_shards(name, g))
        for n, sh in shards.items():
            self.gds[(l, n)] = sh if hasattr(sh, "shape") else jnp.stack(sh)
        self._alone("swap", l, tuple(shards))

    def _alone(self, kind, l, names):
        if (kind, l, names) not in PLANNED:
            self._done(kind, l, names, _run_side(self._side(kind, l, names), f"{kind}_{names[0]}_l{l}"))

    def _side(self, kind, l, names):
        if kind == "gather":
            return _allgather_side([self.own[l][n] for n in names])
        store = {"swap": self.gds, "exchange": self.parts, "share": self.reds}[kind]
        make = {"swap": _swap_side, "exchange": _exchange_side, "share": _share_side}[kind]
        return make([store[(l, n)] for n in names])

    def _done(self, kind, l, names, res):
        for n, r in zip(names, res):
            if kind == "gather":
                self.blocks[(l, n)] = r
            elif kind == "swap":
                self.parts[(l, n)] = _add_pair(self.gds[(l, n)], r, self.core, f"add_pair_{n}_l{l}")
            elif kind == "exchange":
                self.reds[(l, n)] = _add_chips(self.parts[(l, n)], r, self.chip, f"add_chips_{n}_l{l}")
            else:
                self.results[n] = _adamw_layer(self.core, self.Wt[n], self.Mo[n], self.Vo[n], self.reds[(l, n)], r, l,
                                               self.results[n], f"adamw_{n}_l{l}")
        if kind == "exchange":
            self._alone("share", l, names)


def kernel(x, positions, ln_in_g, ln_in_b, w_in, q_norm_g, kv_norm_g, w_uq, w_ukv, ret_gn_g, ret_gn_b, w_out, ln1_g, ln1_b, w_gate, w_up, w_down, ln2_g, ln2_b, loss_target, m_ln_in_g, m_ln_in_b, m_w_in, m_q_norm_g, m_kv_norm_g, m_w_uq, m_w_ukv, m_ret_gn_g, m_ret_gn_b, m_w_out, m_ln1_g, m_ln1_b, m_w_gate, m_w_up, m_w_down, m_ln2_g, m_ln2_b, v_ln_in_g, v_ln_in_b, v_w_in, v_q_norm_g, v_kv_norm_g, v_w_uq, v_w_ukv, v_ret_gn_g, v_ret_gn_b, v_w_out, v_ln1_g, v_ln1_b, v_w_gate, v_w_up, v_w_down, v_ln2_g, v_ln2_b):
    given = dict(locals())
    Wt = {n: given[n] for n in WEIGHTS}
    Mo = {n: given["m_" + n] for n in WEIGHTS}
    Vo = {n: given["v_" + n] for n in WEIGHTS}
    cx, cy, cc = _place()
    chip = (2 * cx + cy).astype(jnp.int32)
    core = cc.astype(jnp.int32)

    own = [{n: Wt[n][l].astype(BF16) for n in BIG} for l in range(DEPTH)]
    for shard in own:
        for n, at in ROW_PIECES.items():
            shard[n + "@a"], shard[n + "@b"] = shard[n][:at], shard[n][at:]
    pipe = _Pipeline(own, Wt, Mo, Vo, core.reshape(1), chip.reshape(1))
    sqerr, grad_x, dP = _local_step(x[0], positions[0], loss_target[0], pipe, Wt)
    results = pipe.results

    small_g = {n: (dP[(n, None)] if Wt[n].ndim == 1 else jnp.stack([dP[(n, l)] for l in range(DEPTH)])) for n in SMALL}
    local_loss = 0.5 * jnp.sum(sqerr) / D_MODEL
    small_sum = _sum_small(_allreduce_small(_flatten_small(small_g, local_loss))).reshape(-1)
    layout, n_small = _small_layout(Wt)
    loss = small_sum[n_small]

    grads, deltas, new_m, new_v = {}, {}, {}, {}
    for n in BIG:
        grads[n], deltas[n], new_m[n], new_v[n] = results[n]
    zero = jnp.zeros((), F32)
    d, mn, vn = _adamw(_flatten_small(Wt, zero), small_sum.reshape(SMALL_ROWS, FLAT_W), _flatten_small(Mo, zero),
                       _flatten_small(Vo, zero), "adamw_small")
    for n in SMALL:
        at, size = layout[n]
        pick = lambda a: a.reshape(-1)[at:at + size].reshape(Wt[n].shape)
        grads[n], deltas[n], new_m[n], new_v[n] = pick(small_sum), pick(d), pick(mn), pick(vn)

    return (loss, grad_x[None], *[grads[n] for n in WEIGHTS], *[deltas[n] for n in WEIGHTS],
            *[new_m[n] for n in WEIGHTS], *[new_v[n] for n in WEIGHTS])
```

```python
import functools

import jax
import jax.numpy as jnp
from jax import lax
from jax.experimental import pallas as pl
from jax.experimental.pallas import tpu as pltpu

F32 = jnp.float32
BF16 = jnp.bfloat16

D_MODEL = 2048
DEPTH = 2
CHUNK = 64
MLA_HEADS = 8
Q_LORA = 512
KV_LORA = 256
NOPE = 128
ROPE = 64
VDIM = 128
RET_HEADS = 4
RET_DK = 256
RET_DV = 256
D_FF = 5632
D_IN = 4928
ROPE_THETA = 10000.0
LN_EPS = 1e-5
RMS_EPS = 1e-6
GN_EPS = 1e-5
ALPHA = (2 * DEPTH) ** 0.25
MLA_SCALE = (NOPE + ROPE) ** -0.5
RET_SCALE = RET_DK ** -0.5
ADAM_LR = 0.001
ADAM_B1 = 0.9
ADAM_B2 = 0.999
ADAM_EPS = 1e-08
ADAM_WD = 0.01
ADAM_STEP = 10

LANES = 128
HEAD_PAD = 256
MLA_IN = 1024
MLA_IN_USED = Q_LORA + KV_LORA + ROPE
D_IN_PAD = MLA_IN + 4 * 1024
ATT_BLOCK = 512
NEG = -1e30
VMEM_LIMIT = 56 * 1024 * 1024

N_CHIPS = 4
FLAT_W = 1024
BIG = ("w_in", "w_uq", "w_ukv", "w_out", "w_gate", "w_up", "w_down")
BIG_SHARD = {"w_in": (2048, 1232), "w_uq": (512, 384), "w_ukv": (256, 512), "w_out": (512, 2048),
             "w_gate": (2048, 1408), "w_up": (2048, 1408), "w_down": (1408, 2048)}
SMALL = ("ln_in_g", "ln_in_b", "q_norm_g", "kv_norm_g", "ret_gn_g", "ret_gn_b", "ln1_g", "ln1_b", "ln2_g", "ln2_b")
WEIGHTS = ("ln_in_g", "ln_in_b", "w_in", "q_norm_g", "kv_norm_g", "w_uq", "w_ukv", "ret_gn_g", "ret_gn_b", "w_out",
           "ln1_g", "ln1_b", "w_gate", "w_up", "w_down", "ln2_g", "ln2_b")
SMALL_ROWS = 32

MESH = pl.DeviceIdType.MESH


def _pick(dim, cands):
    for c in cands:
        if dim % c == 0:
            return c
    return dim


HBM = pl.BlockSpec(memory_space=pltpu.HBM)


class _Side:
    def __init__(self, arrays, out_shape, scratch, start, finish):
        self.arrays, self.out_shape, self.scratch, self.start, self.finish = arrays, out_shape, scratch, start, finish


def _call(body, name, out_shape, grid, in_specs, out_specs, scratch=(), sem=None, side=None):
    params = pltpu.CompilerParams(dimension_semantics=sem if side is None else ("arbitrary",) * len(grid),
                                  vmem_limit_bytes=VMEM_LIMIT)
    if side is None:
        return pl.pallas_call(body, name=name, out_shape=out_shape, grid=grid, in_specs=in_specs, out_specs=out_specs,
                              scratch_shapes=list(scratch), compiler_params=params)
    single = not isinstance(out_shape, (list, tuple))
    outs = [out_shape] if single else list(out_shape)
    ospecs = [out_specs] if single else list(out_specs)
    cuts = [len(in_specs), len(side.arrays), len(outs), len(side.out_shape), len(scratch)]
    ends = [sum(cuts[:k + 1]) for k in range(len(cuts))]

    def hosted(*refs):
        ins, s_in, o, s_out, scr = (refs[a:b] for a, b in zip([0] + ends[:-1], ends))
        sems = refs[ends[-1]:]
        ids = [pl.program_id(a) for a in range(len(grid))]
        first = functools.reduce(jnp.logical_and, [i == 0 for i in ids])
        last = functools.reduce(jnp.logical_and, [i == g - 1 for i, g in zip(ids, grid)])

        @pl.when(first)
        def _():
            side.start(s_in, s_out, sems)

        body(*ins, *o, *scr)

        @pl.when(last)
        def _():
            side.finish(s_in, s_out, sems)

    call = pl.pallas_call(hosted, name=name, out_shape=outs + list(side.out_shape), grid=grid,
                          in_specs=list(in_specs) + [HBM] * len(side.arrays),
                          out_specs=ospecs + [HBM] * len(side.out_shape),
                          scratch_shapes=list(scratch) + list(side.scratch), compiler_params=params)

    def run(*args):
        res = call(*args, *side.arrays)
        return (res[0] if single else list(res[:len(outs)])), list(res[len(outs):])

    return run


def _rows(tm, w, col=0):
    return pl.BlockSpec((tm, w), lambda i: (i, col))


def _whole(shape):
    return pl.BlockSpec(shape, lambda i: (0,) * len(shape))


def _sds(shape, dtype):
    return jax.ShapeDtypeStruct(shape, dtype)


def _matmul(a, b, name, ta=False, tb=False, out_dtype=F32, side=None):
    (K, M) = a.shape if ta else a.shape[::-1]
    (N, Kb) = b.shape if tb else b.shape[::-1]
    assert K == Kb, (a.shape, b.shape, ta, tb)
    tm = _pick(M, (1024, 1408, 512, 256, 128))
    tn = _pick(N, (1024, 512, 256, 128))
    tk = _pick(K, (2816, 2560, 2048, 1024, 512, 256))
    nk = K // tk
    dn = (((0 if ta else 1,), (1 if tb else 0,)), ((), ()))

    def body(a_ref, b_ref, o_ref, acc_ref):
        k = pl.program_id(2)
        if nk == 1:
            o_ref[...] = lax.dot_general(a_ref[...].astype(BF16), b_ref[...].astype(BF16), dn,
                                         preferred_element_type=F32).astype(out_dtype)
        else:
            @pl.when(k == 0)
            def _():
                acc_ref[...] = jnp.zeros_like(acc_ref)

            acc_ref[...] += lax.dot_general(a_ref[...].astype(BF16), b_ref[...].astype(BF16), dn,
                                            preferred_element_type=F32)

            @pl.when(k == nk - 1)
            def _():
                o_ref[...] = acc_ref[...].astype(out_dtype)

    a_spec = pl.BlockSpec((tk, tm), lambda i, j, k: (k, i)) if ta else pl.BlockSpec((tm, tk), lambda i, j, k: (i, k))
    b_spec = pl.BlockSpec((tn, tk), lambda i, j, k: (j, k)) if tb else pl.BlockSpec((tk, tn), lambda i, j, k: (k, j))
    return _call(body, name, _sds((M, N), out_dtype), (M // tm, N // tn, nk), [a_spec, b_spec],
                 pl.BlockSpec((tm, tn), lambda i, j, k: (i, j)), scratch=[pltpu.VMEM((tm, tn), F32)],
                 sem=("parallel", "parallel", "arbitrary"), side=side)(a, b)


def _sigmoid(x):
    return 1.0 / (1.0 + jnp.exp(-x))


def _rope_group(r, c, sa, sb):
    return r * c + pltpu.roll(r, 32, 1) * sa + pltpu.roll(r, 96, 1) * sb


def _ln_fwd(xs, coefs, g, b, name, want_z):
    S, D = xs[0].shape
    tm = 512
    n = len(xs)

    def body(*refs):
        x_refs, g_ref, b_ref, outs = refs[:n], refs[n], refs[n + 1], refs[n + 2:]
        z = None
        for cf, r in zip(coefs, x_refs):
            t = r[...] if cf == 1.0 else cf * r[...]
            z = t if z is None else z + t
        mu = jnp.mean(z, axis=-1, keepdims=True)
        zc = z - mu
        var = jnp.mean(zc * zc, axis=-1, keepdims=True)
        y = zc * lax.rsqrt(var + LN_EPS) * g_ref[...] + b_ref[...]
        if want_z:
            outs[0][...] = z
        outs[-2][...] = y
        outs[-1][...] = y.astype(BF16)

    out_shape = [_sds((S, D), F32)] * (2 if want_z else 1) + [_sds((S, D), BF16)]
    return _call(body, name, out_shape, (S // tm,), [_rows(tm, D)] * n + [_whole((1, D))] * 2,
                 [_rows(tm, D)] * len(out_shape), sem=("parallel",))(*xs, g, b)


def _matmul_ln(a, w, x, g, b, name):
    S, K = a.shape
    D = w.shape[1]
    tm = _pick(S, (256, 128))

    def body(a_ref, w_ref, x_ref, g_ref, b_ref, z_ref, y_ref, yb_ref):
        z = ALPHA * x_ref[...] + jnp.dot(a_ref[...], w_ref[...], preferred_element_type=F32)
        mu = jnp.mean(z, axis=-1, keepdims=True)
        zc = z - mu
        var = jnp.mean(zc * zc, axis=-1, keepdims=True)
        y = zc * lax.rsqrt(var + LN_EPS) * g_ref[...] + b_ref[...]
        z_ref[...] = z
        y_ref[...] = y
        yb_ref[...] = y.astype(BF16)

    return _call(body, name, [_sds((S, D), F32), _sds((S, D), F32), _sds((S, D), BF16)], (S // tm,),
                 [_rows(tm, K), _whole((K, D)), _rows(tm, D), _whole((1, D)), _whole((1, D))], [_rows(tm, D)] * 3,
                 sem=("parallel",))(a, w, x, g, b)


def _ln_bwd(dys, coefs, z, g, name):
    S, D = z.shape
    tm = 512
    n = len(dys)

    def body(*refs):
        dy_refs, z_ref, g_ref = refs[:n], refs[n], refs[n + 1]
        dz_ref, dzb_ref, dg_ref, db_ref = refs[n + 2:]
        dy = None
        for cf, r in zip(coefs, dy_refs):
            t = r[...] if cf == 1.0 else cf * r[...]
            dy = t if dy is None else dy + t
        zv = z_ref[...]
        mu = jnp.mean(zv, axis=-1, keepdims=True)
        zc = zv - mu
        var = jnp.mean(zc * zc, axis=-1, keepdims=True)
        rstd = lax.rsqrt(var + LN_EPS)
        xh = zc * rstd
        dyg = dy * g_ref[...]
        dz = rstd * (dyg - jnp.mean(dyg, axis=-1, keepdims=True) - xh * jnp.mean(dyg * xh, axis=-1, keepdims=True))
        dz_ref[...] = dz
        dzb_ref[...] = dz.astype(BF16)

        @pl.when(pl.program_id(0) == 0)
        def _():
            dg_ref[...] = jnp.zeros_like(dg_ref)
            db_ref[...] = jnp.zeros_like(db_ref)

        dg_ref[...] += jnp.sum(dy * xh, axis=0, keepdims=True)
        db_ref[...] += jnp.sum(dy, axis=0, keepdims=True)

    return _call(body, name, [_sds((S, D), F32), _sds((S, D), BF16), _sds((1, D), F32), _sds((1, D), F32)],
                 (S // tm,), [_rows(tm, D)] * (n + 1) + [_whole((1, D))],
                 [_rows(tm, D), _rows(tm, D), _whole((1, D)), _whole((1, D))], sem=("arbitrary",))(*dys, z, g)


def _rms(x, g):
    return x * lax.rsqrt(jnp.mean(x * x, axis=-1, keepdims=True) + RMS_EPS) * g


def _prep1(h, tabs, qg, kvg, name):
    S = h.shape[0]
    tm = 512
    cm, sam, sbm, cr, sr = tabs

    def body(h_ref, cm_ref, sam_ref, sbm_ref, cr_ref, sr_ref, qg_ref, kvg_ref,
             qn_ref, kvn_ref, kr_ref, rq_ref, rk_ref, rv_ref):
        qn_ref[...] = _rms(h_ref[:, 0:Q_LORA], qg_ref[...]).astype(BF16)
        kvn_ref[...] = _rms(h_ref[:, Q_LORA:Q_LORA + KV_LORA], kvg_ref[...]).astype(BF16)
        kr_ref[...] = _rope_group(h_ref[:, 768:896], cm_ref[...], sam_ref[...], sbm_ref[...])
        c, s = cr_ref[...], sr_ref[...]
        for hd in range(RET_HEADS):
            for src, dst, scale in ((MLA_IN, rq_ref, RET_SCALE), (MLA_IN + 1024, rk_ref, None)):
                t1 = h_ref[:, src + hd * 256:src + hd * 256 + 128]
                t2 = h_ref[:, src + hd * 256 + 128:src + hd * 256 + 256]
                o1, o2 = t1 * c - t2 * s, t2 * c + t1 * s
                if scale is not None:
                    o1, o2 = o1 * scale, o2 * scale
                dst[:, hd * 256:hd * 256 + 128] = o1.astype(BF16)
                dst[:, hd * 256 + 128:hd * 256 + 256] = o2.astype(BF16)
        rv_ref[...] = h_ref[:, MLA_IN + 2048:MLA_IN + 3072].astype(BF16)

    t128 = _rows(tm, LANES)
    return _call(body, name,
                 [_sds((S, Q_LORA), BF16), _sds((S, KV_LORA), BF16), _sds((S, LANES), F32),
                  _sds((S, 1024), BF16), _sds((S, 1024), BF16), _sds((S, 1024), BF16)],
                 (S // tm,),
                 [_rows(tm, D_IN_PAD), t128, t128, t128, t128, t128, _whole((1, Q_LORA)), _whole((1, KV_LORA))],
                 [_rows(tm, Q_LORA), _rows(tm, KV_LORA), t128, _rows(tm, 1024), _rows(tm, 1024), _rows(tm, 1024)],
                 sem=("parallel",))(h, cm, sam, sbm, cr, sr, qg, kvg)


def _prep1_bwd(dqn, dkvn, dkr, drq, drk, drv, drg, h, tabs, qg, kvg, name):
    S = h.shape[0]
    tm = 512
    cm, sam, sbm, cr, sr = tabs

    def rms_bwd(x, g, dy):
        r = lax.rsqrt(jnp.mean(x * x, axis=-1, keepdims=True) + RMS_EPS)
        dyg = dy * g
        dx = r * dyg - x * (r * r * r) * jnp.mean(dyg * x, axis=-1, keepdims=True)
        return dx, jnp.sum(dy * x * r, axis=0, keepdims=True)

    def body(dqn_ref, dkvn_ref, dkr_ref, drq_ref, drk_ref, drv_ref, drg_ref, h_ref,
             cm_ref, sam_ref, sbm_ref, cr_ref, sr_ref, qg_ref, kvg_ref, dh_ref, dqg_ref, dkvg_ref):
        dcq, dqg = rms_bwd(h_ref[:, 0:Q_LORA], qg_ref[...], dqn_ref[...])
        dckv, dkvg = rms_bwd(h_ref[:, Q_LORA:Q_LORA + KV_LORA], kvg_ref[...], dkvn_ref[...])
        dh_ref[:, 0:Q_LORA] = dcq.astype(BF16)
        dh_ref[:, Q_LORA:Q_LORA + KV_LORA] = dckv.astype(BF16)
        dh_ref[:, 768:896] = _rope_group(dkr_ref[...], cm_ref[...], -sam_ref[...], -sbm_ref[...]).astype(BF16)
        dh_ref[:, 896:1024] = jnp.zeros((tm, LANES), BF16)
        c, s = cr_ref[...], sr_ref[...]
        for hd in range(RET_HEADS):
            for src, dst, scale in ((drq_ref, MLA_IN, RET_SCALE), (drk_ref, MLA_IN + 1024, None)):
                d1 = src[:, hd * 256:hd * 256 + 128]
                d2 = src[:, hd * 256 + 128:hd * 256 + 256]
                if scale is not None:
                    d1, d2 = d1 * scale, d2 * scale
                dh_ref[:, dst + hd * 256:dst + hd * 256 + 128] = (d1 * c + d2 * s).astype(BF16)
                dh_ref[:, dst + hd * 256 + 128:dst + hd * 256 + 256] = (d2 * c - d1 * s).astype(BF16)
        dh_ref[:, MLA_IN + 2048:MLA_IN + 3072] = drv_ref[...].astype(BF16)
        dh_ref[:, MLA_IN + 3072:MLA_IN + 4096] = drg_ref[...].astype(BF16)

        @pl.when(pl.program_id(0) == 0)
        def _():
            dqg_ref[...] = jnp.zeros_like(dqg_ref)
            dkvg_ref[...] = jnp.zeros_like(dkvg_ref)

        dqg_ref[...] += dqg
        dkvg_ref[...] += dkvg

    t128 = _rows(tm, LANES)
    return _call(body, name,
                 [_sds((S, D_IN_PAD), BF16), _sds((1, Q_LORA), F32), _sds((1, KV_LORA), F32)],
                 (S // tm,),
                 [_rows(tm, Q_LORA), _rows(tm, KV_LORA), t128, _rows(tm, 1024), _rows(tm, 1024), _rows(tm, 1024),
                  _rows(tm, 1024), _rows(tm, MLA_IN), t128, t128, t128, t128, t128,
                  _whole((1, Q_LORA)), _whole((1, KV_LORA))],
                 [_rows(tm, D_IN_PAD), _whole((1, Q_LORA)), _whole((1, KV_LORA))],
                 sem=("arbitrary",))(dqn, dkvn, dkr, drq, drk, drv, drg, h, cm, sam, sbm, cr, sr, qg, kvg)


def _prep2(q, kv, kr, tabs, name):
    S = q.shape[0]
    tm = 512
    cm, sam, sbm = tabs[:3]

    def body(q_ref, kv_ref, kr_ref, cm_ref, sam_ref, sbm_ref, qo_ref, ko_ref, vo_ref):
        c, sa, sb = cm_ref[...], sam_ref[...], sbm_ref[...]
        krb = kr_ref[...].astype(BF16)
        ones = jnp.ones((tm, LANES), BF16)
        for hd in range(MLA_HEADS):
            o = hd * HEAD_PAD
            qo_ref[:, o:o + 128] = (q_ref[:, o:o + 128] * MLA_SCALE).astype(BF16)
            qo_ref[:, o + 128:o + 256] = (_rope_group(q_ref[:, o + 128:o + 256], c, sa, sb) * MLA_SCALE).astype(BF16)
            ko_ref[:, o:o + 128] = kv_ref[:, hd * 128:hd * 128 + 128].astype(BF16)
            ko_ref[:, o + 128:o + 256] = krb
            vo_ref[:, o:o + 128] = kv_ref[:, 1024 + hd * 128:1024 + hd * 128 + 128].astype(BF16)
            vo_ref[:, o + 128:o + 256] = ones

    t128 = _rows(tm, LANES)
    return _call(body, name, [_sds((S, 2048), BF16)] * 3, (S // tm,),
                 [_rows(tm, 2048), _rows(tm, 2048), t128, t128, t128, t128],
                 [_rows(tm, 2048)] * 3, sem=("parallel",))(q, kv, kr, cm, sam, sbm)


def _prep2_bwd(dqm, dkm, dvm, tabs, name):
    S = dqm.shape[0]
    tm = 512
    cm, sam, sbm = tabs[:3]

    def body(dq_ref, dk_ref, dv_ref, cm_ref, sam_ref, sbm_ref, dqo_ref, dkvo_ref, dkr_ref):
        c, sa, sb = cm_ref[...], -sam_ref[...], -sbm_ref[...]
        dkr = None
        for hd in range(MLA_HEADS):
            o = hd * HEAD_PAD
            dqo_ref[:, o:o + 128] = (dq_ref[:, o:o + 128] * MLA_SCALE).astype(BF16)
            dqo_ref[:, o + 128:o + 256] = (_rope_group(dq_ref[:, o + 128:o + 256], c, sa, sb) * MLA_SCALE).astype(BF16)
            dkvo_ref[:, hd * 128:hd * 128 + 128] = dk_ref[:, o:o + 128].astype(BF16)
            t = dk_ref[:, o + 128:o + 256]
            dkr = t if dkr is None else dkr + t
        dkvo_ref[:, 1024:2048] = dv_ref[...].astype(BF16)
        dkr_ref[...] = dkr

    t128 = _rows(tm, LANES)
    return _call(body, name, [_sds((S, 2048), BF16), _sds((S, 2048), BF16), _sds((S, LANES), F32)], (S // tm,),
                 [_rows(tm, 2048), _rows(tm, 2048), _rows(tm, 1024), t128, t128, t128],
                 [_rows(tm, 2048), _rows(tm, 2048), t128], sem=("parallel",))(dqm, dkm, dvm, cm, sam, sbm)


def _gn_gate(a, o, h, gg, gb, name):
    S = a.shape[0]
    tm = 512

    def body(a_ref, o_ref, rg_ref, gg_ref, gb_ref, mix_ref):
        mix_ref[:, 0:1024] = a_ref[...].astype(BF16)
        for hd in range(RET_HEADS):
            sl = slice(hd * 256, hd * 256 + 256)
            ov = o_ref[:, sl]
            mu = jnp.mean(ov, axis=-1, keepdims=True)
            oc = ov - mu
            var = jnp.mean(oc * oc, axis=-1, keepdims=True)
            y = oc * lax.rsqrt(var + GN_EPS) * gg_ref[:, sl] + gb_ref[:, sl]
            rg = rg_ref[:, sl]
            mix_ref[:, 1024 + hd * 256:1024 + hd * 256 + 256] = (rg * _sigmoid(rg) * y).astype(BF16)

    return _call(body, name, _sds((S, 2048), BF16), (S // tm,),
                 [_rows(tm, 1024), _rows(tm, 1024), _rows(tm, 1024, 4), _whole((1, 1024)), _whole((1, 1024))],
                 _rows(tm, 2048), sem=("parallel",))(a, o, h, gg, gb)


def _gn_gate_bwd(dmixin, o, h, gg, gb, name):
    S = o.shape[0]
    tm = 512

    def body(dr_ref, o_ref, rg_ref, gg_ref, gb_ref, do_ref, drg_ref, dgg_ref, dgb_ref):
        @pl.when(pl.program_id(0) == 0)
        def _():
            dgg_ref[...] = jnp.zeros_like(dgg_ref)
            dgb_ref[...] = jnp.zeros_like(dgb_ref)

        for hd in range(RET_HEADS):
            sl = slice(hd * 256, hd * 256 + 256)
            ov = o_ref[:, sl]
            mu = jnp.mean(ov, axis=-1, keepdims=True)
            oc = ov - mu
            var = jnp.mean(oc * oc, axis=-1, keepdims=True)
            rstd = lax.rsqrt(var + GN_EPS)
            xh = oc * rstd
            g = gg_ref[:, sl]
            y = xh * g + gb_ref[:, sl]
            rg = rg_ref[:, sl]
            sg = _sigmoid(rg)
            dr = dr_ref[:, sl]
            dy = dr * (rg * sg)
            drg_ref[:, sl] = dr * y * (sg * (1.0 + rg * (1.0 - sg)))
            dgg_ref[:, sl] += jnp.sum(dy * xh, axis=0, keepdims=True)
            dgb_ref[:, sl] += jnp.sum(dy, axis=0, keepdims=True)
            dxh = dy * g
            do = rstd * (dxh - jnp.mean(dxh, axis=-1, keepdims=True) - xh * jnp.mean(dxh * xh, axis=-1, keepdims=True))
            do_ref[:, sl] = do.astype(BF16)

    return _call(body, name,
                 [_sds((S, 1024), BF16), _sds((S, 1024), F32), _sds((1, 1024), F32), _sds((1, 1024), F32)],
                 (S // tm,),
                 [_rows(tm, 1024, 1), _rows(tm, 1024), _rows(tm, 1024, 4), _whole((1, 1024)), _whole((1, 1024))],
                 [_rows(tm, 1024), _rows(tm, 1024), _whole((1, 1024)), _whole((1, 1024))],
                 sem=("arbitrary",))(dmixin, o, h, gg, gb)


GU_BLOCK = D_FF // N_CHIPS


def _matmul_swiglu(x, w_gu, name, side=None):
    S, K = x.shape
    tm = _pick(S, (512, 256, 128))
    tn = 2 * GU_BLOCK

    def body(x_ref, w_ref, gu_ref, act_ref):
        r = jnp.dot(x_ref[...], w_ref[...], preferred_element_type=F32)
        g, u = r[:, :GU_BLOCK], r[:, GU_BLOCK:]
        gu_ref[...] = r.astype(BF16)
        act_ref[...] = (g * _sigmoid(g) * u).astype(BF16)

    return _call(body, name, [_sds((S, 2 * D_FF), BF16), _sds((S, D_FF), BF16)], (S // tm, N_CHIPS),
                 [pl.BlockSpec((tm, K), lambda i, j: (i, 0)), pl.BlockSpec((K, tn), lambda i, j: (0, j))],
                 [pl.BlockSpec((tm, tn), lambda i, j: (i, j)), pl.BlockSpec((tm, GU_BLOCK), lambda i, j: (i, j))],
                 sem=("parallel", "parallel"), side=side)(x, w_gu)


def _matmul_swiglu_bwd(df, w_down, gu, name, side=None):
    S, K = df.shape
    tm = _pick(S, (512, 256, 128))

    def body(df_ref, w_ref, gu_ref, o_ref):
        d = _dot_nt(df_ref[...], w_ref[...])
        g = gu_ref[:, :GU_BLOCK].astype(F32)
        u = gu_ref[:, GU_BLOCK:].astype(F32)
        sg = _sigmoid(g)
        o_ref[:, :GU_BLOCK] = (d * u * (sg * (1.0 + g * (1.0 - sg)))).astype(BF16)
        o_ref[:, GU_BLOCK:] = (d * (g * sg)).astype(BF16)

    gu_spec = pl.BlockSpec((tm, 2 * GU_BLOCK), lambda i, j: (i, j))
    return _call(body, name, _sds((S, 2 * D_FF), BF16), (S // tm, N_CHIPS),
                 [pl.BlockSpec((tm, K), lambda i, j: (i, 0)), pl.BlockSpec((GU_BLOCK, K), lambda i, j: (j, 0)), gu_spec],
                 gu_spec, sem=("parallel", "parallel"), side=side)(df, w_down, gu)


def _loss_head(y, target, name):
    S, D = y.shape
    tm = 512

    def body(y_ref, t_ref, dy_ref, acc_ref):
        e = y_ref[...] - t_ref[...]
        dy_ref[...] = e / D

        @pl.when(pl.program_id(0) == 0)
        def _():
            acc_ref[...] = jnp.zeros_like(acc_ref)

        acc_ref[...] += jnp.sum(e * e, axis=0, keepdims=True)

    return _call(body, name, [_sds((S, D), F32), _sds((1, D), F32)], (S // tm,), [_rows(tm, D), _rows(tm, D)],
                 [_rows(tm, D), _whole((1, D))], sem=("arbitrary",))(y, target)


def _chunk_mask(T):
    r = lax.shift_right_logical(lax.broadcasted_iota(jnp.int32, (T, T), 0), 6)
    c = lax.shift_right_logical(lax.broadcasted_iota(jnp.int32, (T, T), 1), 6)
    return r >= c


def _dot_nt(a, b):
    return lax.dot_general(a, b, (((1,), (1,)), ((), ())), preferred_element_type=F32)


def _dot_tn(a, b):
    return lax.dot_general(a, b, (((0,), (0,)), ((), ())), preferred_element_type=F32)


def _decay_tables(T):
    lg = jnp.log1p(-jnp.exp2(-5.0 - jnp.arange(RET_HEADS, dtype=F32)))
    idx = jnp.arange(T, dtype=F32)
    diff = idx[:, None] - idx[None, :]
    rel = jnp.exp(lg[:, None, None] * diff[None])
    cid = jnp.arange(T) // CHUNK
    mask = (cid[:, None] >= cid[None, :]).astype(F32)
    reld = jnp.exp(lg[:, None, None] * jnp.abs(diff)[None]) * mask[None]
    lgrow = jnp.broadcast_to(lg[:, None, None], (RET_HEADS, 1, LANES))
    return lgrow, rel, reld


def _attn_fwd(q, k, v, heads, dk, dv, softmax, name, tables=None, side=None):
    S = q.shape[0]
    T = ATT_BLOCK
    nq = S // T
    rep = T // LANES
    vw = 2 * dv if softmax else dv
    assert not softmax or dv == LANES

    def body(*refs):
        if softmax:
            q_ref, k_ref, v_ref, o_ref, lse_ref, m_sc, acc_sc = refs
        else:
            q_ref, k_ref, v_ref, lg_ref, rel_ref, reld_ref, o_ref, acc_sc = refs
        i = pl.program_id(1)
        qv = q_ref[...]

        def kv_block(j):
            rows = pl.ds(pl.multiple_of(j * T, T), T)
            return k_ref[rows, :], v_ref[rows, :]

        kb, vb = kv_block(i)
        s = _dot_nt(qv, kb)
        if softmax:
            s = jnp.where(_chunk_mask(T), s, NEG)
            m = jnp.max(s, axis=-1, keepdims=True)
            p = jnp.exp(s - m)
            m_sc[...] = jnp.broadcast_to(m, (T, LANES))
        else:
            p = s * reld_ref[0]
        acc_sc[...] = jnp.dot(p.astype(BF16), vb, preferred_element_type=F32)

        def scores(j):
            kb, vb = kv_block(j)
            return _dot_nt(qv, kb), vb

        def update(j, s, vb):
            if softmax:
                m_prev = m_sc[...]
                m_next = jnp.maximum(m_prev, jnp.max(s, axis=-1, keepdims=True))
                alpha = jnp.exp(m_prev - m_next)
                p = jnp.exp(s - jnp.tile(m_next, (1, rep)))
                m_sc[...] = m_next
                acc_sc[...] = acc_sc[...] * jnp.tile(alpha, (1, vw // LANES)) + jnp.dot(
                    p.astype(BF16), vb, preferred_element_type=F32)
            else:
                fac = jnp.exp(lg_ref[0] * ((i - j) * T).astype(F32))
                p = s * (rel_ref[0] * jnp.tile(fac, (1, rep)))
                acc_sc[...] += jnp.dot(p.astype(BF16), vb, preferred_element_type=F32)

        def pair(jj, carry):
            first, second = scores(2 * jj), scores(2 * jj + 1)
            update(2 * jj, *first)
            update(2 * jj + 1, *second)
            return carry

        lax.fori_loop(0, i // 2, pair, 0)

        @pl.when(i % 2 == 1)
        def _():
            update(i - 1, *scores(i - 1))

        if softmax:
            l = acc_sc[:, dv:]
            o_ref[...] = acc_sc[:, :dv] / l
            lse_ref[...] = m_sc[...] + jnp.log(l)
        else:
            o_ref[...] = acc_sc[...]

    in_specs = [pl.BlockSpec((T, dk), lambda h, i: (i, h)), pl.BlockSpec((S, dk), lambda h, i: (0, h)),
                pl.BlockSpec((S, vw), lambda h, i: (0, h))]
    o_spec = pl.BlockSpec((T, dv), lambda h, i: (i, h))
    if softmax:
        return _call(body, name, [_sds((S, heads * dv), F32), _sds((S, heads * LANES), F32)], (heads, nq), in_specs,
                     [o_spec, pl.BlockSpec((T, LANES), lambda h, i: (i, h))],
                     scratch=[pltpu.VMEM((T, LANES), F32), pltpu.VMEM((T, vw), F32)],
                     sem=("parallel", "arbitrary"), side=side)(q, k, v)
    lgrow, rel, reld = tables
    in_specs += [pl.BlockSpec((1, 1, LANES), lambda h, i: (h, 0, 0)), pl.BlockSpec((1, T, T), lambda h, i: (h, 0, 0)),
                 pl.BlockSpec((1, T, T), lambda h, i: (h, 0, 0))]
    return _call(body, name, _sds((S, heads * dv), F32), (heads, nq), in_specs, o_spec,
                 scratch=[pltpu.VMEM((T, dv), F32)], sem=("parallel", "arbitrary"), side=side)(q, k, v, lgrow, rel, reld)


def _attn_bwd(q, k, v, do, heads, dk, dv, softmax, name, o=None, lse=None, tables=None, side=None):
    S = q.shape[0]
    T = ATT_BLOCK
    nq = S // T
    rep = T // LANES

    def body(*refs):
        if softmax:
            q_ref, k_ref, v_ref, do_ref, o_ref, lse_ref, dq_ref, dk_ref, dv_ref, dq_sc = refs
        else:
            q_ref, k_ref, v_ref, do_ref, lg_ref, rel_ref, reld_ref, dq_ref, dk_ref, dv_ref, dq_sc = refs
        i = pl.program_id(1)

        @pl.when(i == 0)
        def _():
            dk_ref[...] = jnp.zeros_like(dk_ref)
            dv_ref[...] = jnp.zeros_like(dv_ref)

        qv = q_ref[...]
        dof = do_ref[...].astype(F32)
        dov = dof.astype(BF16)
        if softmax:
            delta = jnp.sum(dof * o_ref[...], axis=-1, keepdims=True)
            lse_t = jnp.tile(lse_ref[...], (1, rep))
        dq_sc[...] = jnp.zeros_like(dq_sc)

        def products(j):
            rows = pl.ds(pl.multiple_of(j * T, T), T)
            kb = k_ref[rows, :]
            return rows, kb, _dot_nt(qv, kb), _dot_nt(dov, v_ref[rows, :])

        def block(j, diagonal, rows, kb, s, dp):
            if softmax:
                if diagonal:
                    s = jnp.where(_chunk_mask(T), s, NEG)
                p = jnp.exp(s - lse_t)
                ds = p * (dp - delta)
            else:
                if diagonal:
                    dec = reld_ref[0]
                else:
                    fac = jnp.exp(lg_ref[0] * ((i - j) * T).astype(F32))
                    dec = rel_ref[0] * jnp.tile(fac, (1, rep))
                p = s * dec
                ds = dp * dec
            dsb = ds.astype(BF16)
            dv_ref[rows, :] += _dot_tn(p.astype(BF16), dov)
            dk_ref[rows, :] += _dot_tn(dsb, qv)
            dq_sc[...] += jnp.dot(dsb, kb, preferred_element_type=F32)

        block(i, True, *products(i))

        def pair(jj, carry):
            first, second = products(2 * jj), products(2 * jj + 1)
            block(2 * jj, False, *first)
            block(2 * jj + 1, False, *second)
            return carry

        lax.fori_loop(0, i // 2, pair, 0)

        @pl.when(i % 2 == 1)
        def _():
            block(i - 1, False, *products(i - 1))

        dq_ref[...] = dq_sc[...]

    qspec = pl.BlockSpec((T, dk), lambda h, i: (i, h))
    kspec = pl.BlockSpec((S, dk), lambda h, i: (0, h))
    vspec = pl.BlockSpec((S, dv), lambda h, i: (0, h))
    dospec = pl.BlockSpec((T, dv), lambda h, i: (i, h))
    in_specs = [qspec, kspec, vspec, dospec]
    args = [q, k, v, do]
    if softmax:
        in_specs[2] = pl.BlockSpec((S, dv), lambda h, i: (0, 2 * h))
        in_specs += [dospec, pl.BlockSpec((T, LANES), lambda h, i: (i, h))]
        args += [o, lse]
    else:
        in_specs += [pl.BlockSpec((1, 1, LANES), lambda h, i: (h, 0, 0)),
                     pl.BlockSpec((1, T, T), lambda h, i: (h, 0, 0)), pl.BlockSpec((1, T, T), lambda h, i: (h, 0, 0))]
        args += list(tables)
    return _call(body, name, [_sds((S, heads * dk), F32), _sds((S, heads * dk), F32), _sds((S, heads * dv), F32)],
                 (heads, nq), in_specs, [qspec, kspec, vspec], scratch=[pltpu.VMEM((T, dk), F32)],
                 sem=("parallel", "arbitrary"), side=side)(*args)


def _rope_tables(pos):
    def tables(dim):
        inv_freq = ROPE_THETA ** (-jnp.arange(0, dim, 2, dtype=F32) / dim)
        ang = pos.astype(F32)[:, None] * inv_freq
        return jnp.cos(ang), jnp.sin(ang)

    cm, sm = tables(ROPE)
    S = pos.shape[0]
    z32, z64 = jnp.zeros((S, 32), F32), jnp.zeros((S, 64), F32)
    cr, sr = tables(RET_DK)
    return (jnp.concatenate([cm, cm, z64], 1), jnp.concatenate([z32, sm, z64], 1),
            jnp.concatenate([-sm, z32, z64], 1), cr, sr)


def _row(v):
    return v.reshape(1, -1).astype(F32)


def _local_step(x, pos, target, pipe, P):
    tabs = _rope_tables(pos)
    dtabs = _decay_tables(ATT_BLOCK)
    xf, xb = _ln_fwd([x], [1.0], _row(P["ln_in_g"]), _row(P["ln_in_b"]), "ln_in", False)
    pipe.gather_first()
    saved = []
    for l in range(DEPTH):
        w = functools.partial(pipe.weight, l)
        t = f"_l{l}"
        h = pipe.run(_matmul, "mm_h" + t, xb, w("w_in"))
        qn, kvn, kr, rq, rk, rv = _prep1(h, tabs, _row(P["q_norm_g"][l]), _row(P["kv_norm_g"][l]), "prep1" + t)
        q = _matmul(qn, w("w_uq"), "mm_q" + t)
        kv = _matmul(kvn, w("w_ukv"), "mm_kv" + t)
        qm, km, vm = _prep2(q, kv, kr, tabs, "prep2" + t)
        a, lse = pipe.run(_attn_fwd, "mla_fwd" + t, qm, km, vm, MLA_HEADS, HEAD_PAD, VDIM, True)
        o = pipe.run(_attn_fwd, "ret_fwd" + t, rq, rk, rv, RET_HEADS, RET_DK, RET_DV, False, tables=dtabs)
        mixin = _gn_gate(a, o, h, _row(P["ret_gn_g"][l]), _row(P["ret_gn_b"][l]), "gn_gate" + t)
        z1, x1f, x1b = _matmul_ln(mixin, w("w_out"), xf, _row(P["ln1_g"][l]), _row(P["ln1_b"][l]), "mm_mix_ln1" + t)
        gu, act = pipe.run(_matmul_swiglu, "mm_gu" + t, x1b, w("w_gu"))
        f = pipe.run(_matmul, "mm_down" + t, act, w("w_down"))
        z2, x2f, x2b = _ln_fwd([x1f, f], [ALPHA, 1.0], _row(P["ln2_g"][l]), _row(P["ln2_b"][l]), "ln2" + t, True)
        saved.append(dict(xb=xb, h=h, qn=qn, kvn=kvn, rq=rq, rk=rk, rv=rv, qm=qm, km=km, vm=vm, a=a, lse=lse, o=o,
                          mixin=mixin, z1=z1, x1b=x1b, gu=gu, act=act, z2=z2))
        xf, xb = x2f, x2b

    dy, sqerr = _loss_head(xf, target, "loss_head")
    dP = {}
    dys, coefs = [dy], [1.0]
    for l in reversed(range(DEPTH)):
        w, sv = functools.partial(pipe.weight, l), saved[l]
        t = f"_l{l}"
        dz2, dz2b, dg, db = _ln_bwd(dys, coefs, sv["z2"], _row(P["ln2_g"][l]), "ln2_bwd" + t)
        dP[("ln2_g", l)], dP[("ln2_b", l)] = dg, db
        pipe.reduce(l, w_down=pipe.run(_matmul, "mm_dw_down" + t, sv["act"], dz2b, ta=True, out_dtype=BF16))
        dgu = pipe.run(_matmul_swiglu_bwd, "mm_dact" + t, dz2b, w("w_down"), sv["gu"])
        pipe.reduce(l, w_gu=pipe.run(_matmul, "mm_dw_gu" + t, sv["x1b"], dgu, ta=True, out_dtype=BF16))
        dx1 = pipe.run(_matmul, "mm_dx1" + t, dgu, w("w_gu"), tb=True)
        dz1, dz1b, dg, db = _ln_bwd([dz2, dx1], [ALPHA, 1.0], sv["z1"], _row(P["ln1_g"][l]), "ln1_bwd" + t)
        dP[("ln1_g", l)], dP[("ln1_b", l)] = dg, db
        pipe.reduce(l, w_out=_matmul(sv["mixin"], dz1b, "mm_dw_out" + t, ta=True, out_dtype=BF16))
        dmixin = pipe.run(_matmul, "mm_dmixin" + t, dz1b, w("w_out"), tb=True)
        do, drg, dgg, dgb = _gn_gate_bwd(dmixin, sv["o"], sv["h"], _row(P["ret_gn_g"][l]), _row(P["ret_gn_b"][l]),
                                         "gn_gate_bwd" + t)
        dP[("ret_gn_g", l)], dP[("ret_gn_b", l)] = dgg, dgb
        drq, drk, drv = pipe.run(_attn_bwd, "ret_bwd" + t, sv["rq"], sv["rk"], sv["rv"], do, RET_HEADS, RET_DK, RET_DV,
                                 False, tables=dtabs)
        dqm, dkm, dvm = pipe.run(_attn_bwd, "mla_bwd" + t, sv["qm"], sv["km"], sv["vm"], dmixin, MLA_HEADS, HEAD_PAD,
                                 VDIM, True, o=sv["a"], lse=sv["lse"])
        dq, dkv, dkr = _prep2_bwd(dqm, dkm, dvm, tabs, "prep2_bwd" + t)
        g_uq = _matmul(sv["qn"], dq, "mm_dw_uq" + t, ta=True, out_dtype=BF16)
        dqn = _matmul(dq, w("w_uq"), "mm_dqn" + t, tb=True)
        g_ukv = _matmul(sv["kvn"], dkv, "mm_dw_ukv" + t, ta=True, out_dtype=BF16)
        dkvn = _matmul(dkv, w("w_ukv"), "mm_dkvn" + t, tb=True)
        dh, dqg, dkvg = _prep1_bwd(dqn, dkvn, dkr, drq, drk, drv, drg, sv["h"], tabs, _row(P["q_norm_g"][l]),
                                   _row(P["kv_norm_g"][l]), "prep1_bwd" + t)
        dP[("q_norm_g", l)], dP[("kv_norm_g", l)] = dqg, dkvg
        pipe.reduce(l, w_uq=g_uq, w_ukv=g_ukv,
                    w_in=pipe.run(_matmul, "mm_dw_in" + t, sv["xb"], dh, ta=True, out_dtype=BF16))
        dxl = pipe.run(_matmul, "mm_dxl" + t, dh, w("w_in"), tb=True)
        dys, coefs = [dz1, dxl], [ALPHA, 1.0]
    grad_x, _, dg, db = _ln_bwd(dys, coefs, x, _row(P["ln_in_g"]), "ln_in_bwd")
    dP[("ln_in_g", None)], dP[("ln_in_b", None)] = dg, db
    return sqerr, grad_x, dP


INTERNAL_OF = {"w_in": ("w_in",), "w_uq": ("w_uq",), "w_ukv": ("w_ukv",), "w_out": ("w_out",),
               "w_gu": ("w_gate", "w_up"), "w_down": ("w_down",)}
ROW_PIECES = {"w_up": 1024}


def _internal_weight(name, *blocks):
    cat = lambda parts: jnp.concatenate(parts, axis=1)
    cols = lambda b: cat([b[j] for j in range(N_CHIPS)])
    b = blocks[0]
    if name in ("w_out", "w_down"):
        return b.reshape(-1, b.shape[-1])
    if name == "w_gu":
        return cat([blk[j] for j in range(N_CHIPS) for blk in blocks])
    if name == "w_in":
        return cat([b[0][:, :MLA_IN_USED], jnp.zeros((D_MODEL, MLA_IN - MLA_IN_USED), BF16), b[0][:, MLA_IN_USED:]]
                   + [b[j] for j in range(1, N_CHIPS)])
    if name == "w_uq":
        uq, hw = cols(b), NOPE + ROPE
        pad = jnp.zeros((Q_LORA, HEAD_PAD - hw), BF16)
        return cat([p for h in range(MLA_HEADS) for p in (uq[:, h * hw:(h + 1) * hw], pad)])
    ukv = cols(b)
    return cat([ukv[:, 256 * h:256 * h + NOPE] for h in range(MLA_HEADS)]
               + [ukv[:, 256 * h + NOPE:256 * h + 256] for h in range(MLA_HEADS)])


def _grad_shards(name, g):
    cat = lambda parts: jnp.concatenate(parts, axis=1)
    if name in ("w_out", "w_down"):
        return {name: g.reshape(N_CHIPS, -1, g.shape[-1])}
    if name == "w_gu":
        return {"w_gate": _ColBlocks(g, 0), "w_up": _ColBlocks(g, 1)}
    if name == "w_in":
        ci, shift = BIG_SHARD["w_in"][1], MLA_IN - MLA_IN_USED
        return {name: [cat([g[:, :MLA_IN_USED], g[:, MLA_IN:ci + shift]])]
                + [g[:, ci * j + shift:ci * (j + 1) + shift] for j in range(1, N_CHIPS)]}
    if name == "w_uq":
        cq = NOPE + ROPE
        return {name: [cat([g[:, HEAD_PAD * h:HEAD_PAD * h + cq] for h in (2 * j, 2 * j + 1)]) for j in range(N_CHIPS)]}
    return {name: [cat([g[:, o + NOPE * h:o + NOPE * (h + 1)] for h in (2 * j, 2 * j + 1) for o in (0, MLA_HEADS * NOPE)])
                   for j in range(N_CHIPS)]}


def _small_layout(P):
    out, at = {}, 0
    for n in SMALL:
        out[n] = (at, P[n].size)
        at += P[n].size
    return out, at


def _flatten_small(P, last):
    v = jnp.concatenate([P[n].reshape(-1).astype(F32) for n in SMALL] + [last.reshape(-1).astype(F32)])
    return jnp.pad(v, (0, SMALL_ROWS * FLAT_W - v.size)).reshape(SMALL_ROWS, FLAT_W)


def _place():
    return lax.axis_index("x"), lax.axis_index("y"), lax.axis_index("c")


def _other_chips(x, y):
    return [(1 - x, y), (x, 1 - y), (1 - x, 1 - y)]


def _rcopy(src, dst, ssem, rsem, dev):
    return pltpu.make_async_remote_copy(src_ref=src, dst_ref=dst, send_sem=ssem, recv_sem=rsem, device_id=dev,
                                        device_id_type=MESH)


def _comm_call(body, name, out_shape, n_in, scratch):
    many = isinstance(out_shape, (list, tuple))
    return pl.pallas_call(body, name=name, out_shape=out_shape, in_specs=[HBM] * n_in,
                          out_specs=[HBM] * len(out_shape) if many else HBM, scratch_shapes=scratch)


def _half(ref, which):
    rows = ref.shape[0] // 2
    return ref.at[pl.ds(pl.multiple_of(which * rows, 16), rows)]


def _dma_sems(n):
    return pltpu.SemaphoreType.DMA((n,))


def _allgather_side(ws):
    k = len(ws)

    def peers():
        x, y, c = _place()
        return c, 2 * x + y, (x, y, 1 - c), [(n, t, cx, cy) for n in range(k) for t, (cx, cy) in enumerate(_other_chips(x, y))]

    def outgoing(w_refs, g_refs, sems):
        ssem, rsem, _, _, ossem, orsem = sems
        c, j, sib, nt = peers()
        owns = [_rcopy(w_refs[n], g_refs[n].at[j], ossem.at[n], orsem.at[n], sib) for n in range(k)]
        sends = [_rcopy(_half(w_refs[n], c), _half(g_refs[n].at[j], c), ssem.at[3 * n + t], rsem.at[3 * n + t],
                        (cx, cy, c)) for n, t, cx, cy in nt]
        return owns, sends

    def incoming(g_refs, sems):
        ssem, rsem, fssem, frsem, _, _ = sems
        c, _, sib, nt = peers()
        landed, passed, relayed = [], [], []
        for n, t, cx, cy in nt:
            mine, other = (_half(g_refs[n].at[2 * cx + cy], h) for h in (c, 1 - c))
            landed.append(_rcopy(mine, mine, ssem.at[3 * n + t], rsem.at[3 * n + t], (cx, cy, c)))
            passed.append(_rcopy(mine, mine, fssem.at[3 * n + t], frsem.at[3 * n + t], sib))
            relayed.append(_rcopy(other, other, fssem.at[3 * n + t], frsem.at[3 * n + t], sib))
        return landed, passed, relayed

    def start(w_refs, g_refs, sems):
        owns, sends = outgoing(w_refs, g_refs, sems)
        for cp in sends + owns:
            cp.start()

    def finish(w_refs, g_refs, sems):
        owns, sends = outgoing(w_refs, g_refs, sems)
        landed, passed, relayed = incoming(g_refs, sems)
        for got, on in zip(landed, passed):
            got.wait_recv()
            on.start()
        for cp in relayed:
            cp.wait_recv()
        for cp in owns:
            cp.wait()
        for cp in sends + passed:
            cp.wait_send()

    return _Side(list(ws), [_sds((N_CHIPS,) + w.shape, w.dtype) for w in ws],
                 [_dma_sems(3 * k)] * 4 + [_dma_sems(k)] * 2, start, finish)


def _exchange_side(parts):
    k = len(parts)

    def copies(p_refs, rcv_refs, sems):
        ssem, rsem = sems
        x, y, c = _place()
        return [_rcopy(p_refs[n].at[2 * cx + cy], rcv_refs[n].at[t], ssem.at[3 * n + t], rsem.at[3 * n + t], (cx, cy, c))
                for n in range(k) for t, (cx, cy) in enumerate(_other_chips(x, y))]

    def start(p_refs, rcv_refs, sems):
        for cp in copies(p_refs, rcv_refs, sems):
            cp.start()

    def finish(p_refs, rcv_refs, sems):
        for cp in copies(p_refs, rcv_refs, sems):
            cp.wait()

    return _Side(list(parts), [_sds((3,) + p.shape[1:], p.dtype) for p in parts], [_dma_sems(3 * k)] * 2, start, finish)


def _run_side(side, name):
    k_in, k_out = len(side.arrays), len(side.out_shape)

    def body(*refs):
        parts = refs[:k_in], refs[k_in:k_in + k_out], refs[k_in + k_out:]
        side.start(*parts)
        side.finish(*parts)

    return _comm_call(body, name, list(side.out_shape), k_in, list(side.scratch))(*side.arrays)


def _sibling_side(arrays, out_shape, n_copies, copies):
    def start(in_refs, out_refs, sems):
        for cp in copies(in_refs, out_refs, sems):
            cp.start()

    def finish(in_refs, out_refs, sems):
        for cp in copies(in_refs, out_refs, sems):
            cp.wait()

    return _Side(list(arrays), out_shape, [_dma_sems(n_copies)] * 2, start, finish)


class _ColBlocks:
    def __init__(self, array, off):
        self.array, self.off, self.dtype = array, off, array.dtype
        self.shape = (N_CHIPS, array.shape[0], GU_BLOCK)

    def block(self, ref, jj):
        return ref.at[:, pl.ds((2 * jj + self.off) * GU_BLOCK, GU_BLOCK)]


def _swap_side(gds):
    k = len(gds)

    def copies(gd_refs, out_refs, sems):
        ssem, rsem = sems
        x, y, c = _place()
        blocks = [[g.block(gd_refs[n], jj) if isinstance(g, _ColBlocks) else gd_refs[n].at[jj] for jj in range(N_CHIPS)]
                  for n, g in enumerate(gds)]
        return [_rcopy(_half(blocks[n][jj], 1 - c), out_refs[n].at[jj], ssem.at[N_CHIPS * n + jj],
                       rsem.at[N_CHIPS * n + jj], (x, y, 1 - c)) for n in range(k) for jj in range(N_CHIPS)]

    return _sibling_side([g.array if isinstance(g, _ColBlocks) else g for g in gds],
                         [_sds((N_CHIPS, g.shape[1] // 2, g.shape[2]), g.dtype) for g in gds], N_CHIPS * k, copies)


def _share_side(reds):
    k = len(reds)

    def copies(r_refs, out_refs, sems):
        ssem, rsem = sems
        x, y, c = _place()
        return [_rcopy(r_refs[n], out_refs[n], ssem.at[n], rsem.at[n], (x, y, 1 - c)) for n in range(k)]

    return _sibling_side(reds, [_sds(r.shape, r.dtype) for r in reds], k, copies)


def _join_sides(sides):
    if len(sides) == 1:
        return sides[0]
    cuts = [(len(s.arrays), len(s.out_shape), len(s.scratch)) for s in sides]

    def each(method, in_refs, out_refs, sems):
        a = o = m = 0
        for s, (ka, ko, km) in zip(sides, cuts):
            getattr(s, method)(in_refs[a:a + ka], out_refs[o:o + ko], sems[m:m + km])
            a, o, m = a + ka, o + ko, m + km

    return _Side([x for s in sides for x in s.arrays], [x for s in sides for x in s.out_shape],
                 [x for s in sides for x in s.scratch], functools.partial(each, "start"), functools.partial(each, "finish"))


def _allreduce_small(small):
    def body(s_ref, all_ref, sssem, srsem, lsem):
        x, y, c = _place()
        me = 4 * x + 2 * y + c
        own = pltpu.make_async_copy(s_ref, all_ref.at[me], lsem)
        own.start()
        cps = []
        for r in range(1, 8):
            fx, fy, fc = (r >> 2) & 1, (r >> 1) & 1, r & 1
            px, py, pc = (1 - x if fx else x, 1 - y if fy else y, 1 - c if fc else c)
            peer = 4 * px + 2 * py + pc
            send = _rcopy(s_ref, all_ref.at[me], sssem.at[r - 1], srsem.at[me], (px, py, pc))
            send.start()
            cps.append((send, _rcopy(s_ref, all_ref.at[peer], sssem.at[r - 1], srsem.at[peer], (px, py, pc))))
        for send, recv in cps:
            send.wait_send()
            recv.wait_recv()
        own.wait()

    return _comm_call(body, "allreduce_small", [_sds((8,) + small.shape, small.dtype)], 1,
                      [pltpu.SemaphoreType.DMA((7,)), pltpu.SemaphoreType.DMA((8,)), pltpu.SemaphoreType.DMA(())])(small)[0]


def _add_pair(gd, got, c, name):
    _, R, W = got.shape
    tm = _pick(R, (512, 256, 128, 64))
    nb = R // tm

    def body(c_ref, a_ref, b_ref, o_ref):
        o_ref[...] = (a_ref[...].astype(F32) + b_ref[...].astype(F32)).astype(o_ref.dtype)

    if isinstance(gd, _ColBlocks):
        off = gd.off
        own = pl.BlockSpec((tm, W), lambda j, i, c_ref: (c_ref[0] * nb + i, 2 * j + off))
        gd = gd.array
    else:
        own = pl.BlockSpec((None, tm, W), lambda j, i, c_ref: (j, c_ref[0] * nb + i, 0))
    grid_spec = pltpu.PrefetchScalarGridSpec(
        num_scalar_prefetch=1, grid=(N_CHIPS, nb),
        in_specs=[own, pl.BlockSpec((None, tm, W), lambda j, i, c_ref: (j, i, 0))],
        out_specs=pl.BlockSpec((None, tm, W), lambda j, i, c_ref: (j, i, 0)))
    return pl.pallas_call(body, name=name, grid_spec=grid_spec, out_shape=_sds((N_CHIPS, R, W), gd.dtype),
                          compiler_params=pltpu.CompilerParams(dimension_semantics=("parallel", "parallel"),
                                                               vmem_limit_bytes=VMEM_LIMIT))(c, gd, got)


def _add_chips(part, rcv, j, name):
    _, R, W = part.shape
    tm = _pick(R, (512, 256, 128, 64))

    def body(j_ref, p_ref, r0_ref, r1_ref, r2_ref, o_ref):
        o_ref[...] = ((p_ref[...].astype(F32) + r0_ref[...].astype(F32)) + r1_ref[...].astype(F32)) + r2_ref[...].astype(F32)

    def slot(t):
        return pl.BlockSpec((None, tm, W), lambda i, j_ref: (t, i, 0))

    grid_spec = pltpu.PrefetchScalarGridSpec(
        num_scalar_prefetch=1, grid=(R // tm,),
        in_specs=[pl.BlockSpec((None, tm, W), lambda i, j_ref: (j_ref[0], i, 0)), slot(0), slot(1), slot(2)],
        out_specs=pl.BlockSpec((tm, W), lambda i, j_ref: (i, 0)))
    return pl.pallas_call(body, name=name, grid_spec=grid_spec, out_shape=_sds((R, W), F32),
                          compiler_params=pltpu.CompilerParams(dimension_semantics=("parallel",),
                                                               vmem_limit_bytes=VMEM_LIMIT))(j, part, rcv, rcv, rcv)


def _sum_small(allsmall):
    _, R, W = allsmall.shape

    def body(a_ref, o_ref):
        acc = a_ref[0]
        for d in range(1, 8):
            acc = acc + a_ref[d]
        o_ref[...] = acc

    return _call(body, "sum_small", _sds((R, W), F32), (1,), [_whole((8, R, W))], _whole((R, W)),
                 sem=("arbitrary",))(allsmall)


def _adamw(w, g, m, v, name):
    R, C = w.shape
    tm = _pick(R, (256, 128, 64, 32, 8))

    def body(w_ref, g_ref, m_ref, v_ref, d_ref, mo_ref, vo_ref):
        gv = g_ref[...]
        mn = ADAM_B1 * m_ref[...] + (1.0 - ADAM_B1) * gv
        vn = ADAM_B2 * v_ref[...] + (1.0 - ADAM_B2) * (gv * gv)
        m_hat = mn / (1.0 - ADAM_B1 ** ADAM_STEP)
        v_hat = vn / (1.0 - ADAM_B2 ** ADAM_STEP)
        d_ref[...] = -ADAM_LR * (m_hat / (jnp.sqrt(v_hat) + ADAM_EPS) + ADAM_WD * w_ref[...])
        mo_ref[...] = mn
        vo_ref[...] = vn

    spec = _rows(tm, C)
    return _call(body, name, [_sds((R, C), F32)] * 3, (R // tm,), [spec] * 4, [spec] * 3, sem=("parallel",))(w, g, m, v)


def _adamw_layer(c, w, m, v, mine, other, l, prev, name):
    _, R, C = w.shape
    half = R // 2
    tm = _pick(half, (256, 128, 64))
    nbh = half // tm

    def body(c_ref, w_ref, m_ref, v_ref, a_ref, b_ref, *rest):
        g_ref, d_ref, mo_ref, vo_ref = rest[-4:]
        gv = jnp.where(pl.program_id(0) // nbh == c_ref[0], a_ref[...], b_ref[...])
        mn = ADAM_B1 * m_ref[...] + (1.0 - ADAM_B1) * gv
        vn = ADAM_B2 * v_ref[...] + (1.0 - ADAM_B2) * (gv * gv)
        m_hat = mn / (1.0 - ADAM_B1 ** ADAM_STEP)
        v_hat = vn / (1.0 - ADAM_B2 ** ADAM_STEP)
        g_ref[...] = gv
        d_ref[...] = -ADAM_LR * (m_hat / (jnp.sqrt(v_hat) + ADAM_EPS) + ADAM_WD * w_ref[...])
        mo_ref[...] = mn
        vo_ref[...] = vn

    layer = pl.BlockSpec((None, tm, C), lambda i, c_ref: (l, i, 0))
    halfspec = pl.BlockSpec((tm, C), lambda i, c_ref: (i % nbh, 0))
    n_prev = 0 if prev is None else 4
    grid_spec = pltpu.PrefetchScalarGridSpec(
        num_scalar_prefetch=1, grid=(R // tm,),
        in_specs=[layer] * 3 + [halfspec] * 2 + [pl.BlockSpec(memory_space=pl.ANY)] * n_prev,
        out_specs=[layer] * 4)
    return pl.pallas_call(body, name=name, grid_spec=grid_spec, out_shape=[_sds(w.shape, F32)] * 4,
                          input_output_aliases={6 + k: k for k in range(n_prev)},
                          compiler_params=pltpu.CompilerParams(dimension_semantics=("parallel",),
                                                               vmem_limit_bytes=VMEM_LIMIT))(
        c, w, m, v, mine, other, *(prev or ()))


FIRST_GATHER = ("w_in", "w_uq", "w_ukv")
G_DOWN, G_GU, G_OUT, G_IN = ("w_down",), ("w_gate", "w_up"), ("w_out",), ("w_uq", "w_ukv", "w_in")


def _backward_jobs(l):
    t = f"_l{l}"
    return {"mm_dact" + t: [("swap", l, G_DOWN)], "mm_dw_gu" + t: [("exchange", l, G_DOWN)],
            "mm_dx1" + t: [("swap", l, G_GU), ("share", l, G_DOWN)], "mm_dmixin" + t: [("swap", l, G_OUT)],
            "ret_bwd" + t: [("exchange", l, ("w_gate",))],
            "mla_bwd" + t: [("exchange", l, ("w_up", "w_out")), ("share", l, ("w_gate",))],
            "mm_dw_in" + t: [("share", l, ("w_up", "w_out"))]}


JOBS = {
    "mm_h_l0": [("gather", 0, ("w_up@a",))], "mla_fwd_l0": [("gather", 0, ("w_gate", "w_out"))],
    "ret_fwd_l0": [("gather", 0, ("w_up@b",))],
    "mm_gu_l0": [("gather", 0, ("w_down",)), ("gather", 1, ("w_uq", "w_ukv"))],
    "mm_down_l0": [("gather", 1, ("w_in",))], "mm_h_l1": [("gather", 1, ("w_up@a",))],
    "mla_fwd_l1": [("gather", 1, ("w_gate", "w_out"))], "ret_fwd_l1": [("gather", 1, ("w_up@b",))],
    "mm_gu_l1": [("gather", 1, ("w_down",))],
    **_backward_jobs(1), **_backward_jobs(0),
    "mm_dxl_l1": [("swap", 1, G_IN)],
    "mm_dx1_l0": [("swap", 0, G_GU), ("share", 0, G_DOWN), ("exchange", 1, G_IN)],
    "ret_bwd_l0": [("exchange", 0, ("w_gate",)), ("share", 1, G_IN)], "mm_dxl_l0": [("exchange", 0, G_IN)]}
PLANNED = {job for jobs in JOBS.values() for job in jobs}


class _Pipeline:
    def __init__(self, own, Wt, Mo, Vo, core, chip):
        self.own, self.Wt, self.Mo, self.Vo, self.core, self.chip = own, Wt, Mo, Vo, core, chip
        self.blocks, self.whole, self.gds, self.parts, self.reds = {}, {}, {}, {}, {}
        self.results = {n: None for n in BIG}

    def gather_first(self):
        job = ("gather", 0, FIRST_GATHER)
        self._done(*job, _run_side(self._side(*job), "allgather_first"))

    def _gathered(self, l, n):
        if n in ROW_PIECES:
            return jnp.concatenate([self.blocks[(l, n + "@a")], self.blocks[(l, n + "@b")]], axis=1)
        return self.blocks[(l, n)]

    def weight(self, l, name):
        if (l, name) not in self.whole:
            self.whole[(l, name)] = _internal_weight(name, *[self._gathered(l, n) for n in INTERNAL_OF[name]])
        return self.whole[(l, name)]

    def run(self, fn, name, *args, **kw):
        jobs = JOBS.get(name, ())
        if not jobs:
            return fn(*args, name=name, **kw)
        sides = [self._side(*job) for job in jobs]
        out, res = fn(*args, name=name, side=_join_sides(sides), **kw)
        for job, side in zip(jobs, sides):
            k = len(side.out_shape)
            self._done(*job, res[:k])
            res = res[k:]
        return out

    def reduce(self, l, **grads):
        shards = {}
        for name, g in grads.items():
            shards.update(_grad_shards(name, g))
        for n, sh in shards.items():
            self.gds[(l, n)] = sh if hasattr(sh, "shape") else jnp.stack(sh)
        self._alone("swap", l, tuple(shards))

    def _alone(self, kind, l, names):
        if (kind, l, names) not in PLANNED:
            self._done(kind, l, names, _run_side(self._side(kind, l, names), f"{kind}_{names[0]}_l{l}"))

    def _side(self, kind, l, names):
        if kind == "gather":
            return _allgather_side([self.own[l][n] for n in names])
        store = {"swap": self.gds, "exchange": self.parts, "share": self.reds}[kind]
        make = {"swap": _swap_side, "exchange": _exchange_side, "share": _share_side}[kind]
        return make([store[(l, n)] for n in names])

    def _done(self, kind, l, names, res):
        for n, r in zip(names, res):
            if kind == "gather":
                self.blocks[(l, n)] = r
            elif kind == "swap":
                self.parts[(l, n)] = _add_pair(self.gds[(l, n)], r, self.core, f"add_pair_{n}_l{l}")
            elif kind == "exchange":
                self.reds[(l, n)] = _add_chips(self.parts[(l, n)], r, self.chip, f"add_chips_{n}_l{l}")
            else:
                self.results[n] = _adamw_layer(self.core, self.Wt[n], self.Mo[n], self.Vo[n], self.reds[(l, n)], r, l,
                                               self.results[n], f"adamw_{n}_l{l}")
        if kind == "exchange":
            self._alone("share", l, names)


def kernel(x, positions, ln_in_g, ln_in_b, w_in, q_norm_g, kv_norm_g, w_uq, w_ukv, ret_gn_g, ret_gn_b, w_out, ln1_g, ln1_b, w_gate, w_up, w_down, ln2_g, ln2_b, loss_target, m_ln_in_g, m_ln_in_b, m_w_in, m_q_norm_g, m_kv_norm_g, m_w_uq, m_w_ukv, m_ret_gn_g, m_ret_gn_b, m_w_out, m_ln1_g, m_ln1_b, m_w_gate, m_w_up, m_w_down, m_ln2_g, m_ln2_b, v_ln_in_g, v_ln_in_b, v_w_in, v_q_norm_g, v_kv_norm_g, v_w_uq, v_w_ukv, v_ret_gn_g, v_ret_gn_b, v_w_out, v_ln1_g, v_ln1_b, v_w_gate, v_w_up, v_w_down, v_ln2_g, v_ln2_b):
    given = dict(locals())
    Wt = {n: given[n] for n in WEIGHTS}
    Mo = {n: given["m_" + n] for n in WEIGHTS}
    Vo = {n: given["v_" + n] for n in WEIGHTS}
    cx, cy, cc = _place()
    chip = (2 * cx + cy).astype(jnp.int32)
    core = cc.astype(jnp.int32)

    own = [{n: Wt[n][l].astype(BF16) for n in BIG} for l in range(DEPTH)]
    for shard in own:
        for n, at in ROW_PIECES.items():
            shard[n + "@a"], shard[n + "@b"] = shard[n][:at], shard[n][at:]
    pipe = _Pipeline(own, Wt, Mo, Vo, core.reshape(1), chip.reshape(1))
    sqerr, grad_x, dP = _local_step(x[0], positions[0], loss_target[0], pipe, Wt)
    results = pipe.results

    small_g = {n: (dP[(n, None)] if Wt[n].ndim == 1 else jnp.stack([dP[(n, l)] for l in range(DEPTH)])) for n in SMALL}
    local_loss = 0.5 * jnp.sum(sqerr) / D_MODEL
    small_sum = _sum_small(_allreduce_small(_flatten_small(small_g, local_loss))).reshape(-1)
    layout, n_small = _small_layout(Wt)
    loss = small_sum[n_small]

    grads, deltas, new_m, new_v = {}, {}, {}, {}
    for n in BIG:
        grads[n], deltas[n], new_m[n], new_v[n] = results[n]
    zero = jnp.zeros((), F32)
    d, mn, vn = _adamw(_flatten_small(Wt, zero), small_sum.reshape(SMALL_ROWS, FLAT_W), _flatten_small(Mo, zero),
                       _flatten_small(Vo, zero), "adamw_small")
    for n in SMALL:
        at, size = layout[n]
        pick = lambda a: a.reshape(-1)[at:at + size].reshape(Wt[n].shape)
        grads[n], deltas[n], new_m[n], new_v[n] = pick(small_sum), pick(d), pick(mn), pick(vn)

    return (loss, grad_x[None], *[grads[n] for n in WEIGHTS], *[deltas[n] for n in WEIGHTS],
            *[new_m[n] for n in WEIGHTS], *[new_v[n] for n in WEIGHTS])
```

```python
import functools

import jax
import jax.numpy as jnp
from jax import lax
from jax.experimental import pallas as pl
from jax.experimental.pallas import tpu as pltpu

F32 = jnp.float32
BF16 = jnp.bfloat16

D_MODEL = 2048
DEPTH = 2
CHUNK = 64
MLA_HEADS = 8
Q_LORA = 512
KV_LORA = 256
NOPE = 128
ROPE = 64
VDIM = 128
RET_HEADS = 4
RET_DK = 256
RET_DV = 256
D_FF = 5632
D_IN = 4928
ROPE_THETA = 10000.0
LN_EPS = 1e-5
RMS_EPS = 1e-6
GN_EPS = 1e-5
ALPHA = (2 * DEPTH) ** 0.25
MLA_SCALE = (NOPE + ROPE) ** -0.5
RET_SCALE = RET_DK ** -0.5
ADAM_LR = 0.001
ADAM_B1 = 0.9
ADAM_B2 = 0.999
ADAM_EPS = 1e-08
ADAM_WD = 0.01
ADAM_STEP = 10

LANES = 128
HEAD_PAD = 256
MLA_IN = 1024
MLA_IN_USED = Q_LORA + KV_LORA + ROPE
D_IN_PAD = MLA_IN + 4 * 1024
ATT_BLOCK = 512
NEG = -1e30
VMEM_LIMIT = 56 * 1024 * 1024

N_CHIPS = 4
FLAT_W = 1024
BIG = ("w_in", "w_uq", "w_ukv", "w_out", "w_gate", "w_up", "w_down")
BIG_SHARD = {"w_in": (2048, 1232), "w_uq": (512, 384), "w_ukv": (256, 512), "w_out": (512, 2048),
             "w_gate": (2048, 1408), "w_up": (2048, 1408), "w_down": (1408, 2048)}
SMALL = ("ln_in_g", "ln_in_b", "q_norm_g", "kv_norm_g", "ret_gn_g", "ret_gn_b", "ln1_g", "ln1_b", "ln2_g", "ln2_b")
WEIGHTS = ("ln_in_g", "ln_in_b", "w_in", "q_norm_g", "kv_norm_g", "w_uq", "w_ukv", "ret_gn_g", "ret_gn_b", "w_out",
           "ln1_g", "ln1_b", "w_gate", "w_up", "w_down", "ln2_g", "ln2_b")
SMALL_ROWS = 32

MESH = pl.DeviceIdType.MESH


def _pick(dim, cands):
    for c in cands:
        if dim % c == 0:
            return c
    return dim


HBM = pl.BlockSpec(memory_space=pltpu.HBM)


class _Side:
    def __init__(self, arrays, out_shape, scratch, start, finish):
        self.arrays, self.out_shape, self.scratch, self.start, self.finish = arrays, out_shape, scratch, start, finish


def _call(body, name, out_shape, grid, in_specs, out_specs, scratch=(), sem=None, side=None):
    params = pltpu.CompilerParams(dimension_semantics=sem if side is None else ("arbitrary",) * len(grid),
                                  vmem_limit_bytes=VMEM_LIMIT)
    if side is None:
        return pl.pallas_call(body, name=name, out_shape=out_shape, grid=grid, in_specs=in_specs, out_specs=out_specs,
                              scratch_shapes=list(scratch), compiler_params=params)
    single = not isinstance(out_shape, (list, tuple))
    outs = [out_shape] if single else list(out_shape)
    ospecs = [out_specs] if single else list(out_specs)
    cuts = [len(in_specs), len(side.arrays), len(outs), len(side.out_shape), len(scratch)]
    ends = [sum(cuts[:k + 1]) for k in range(len(cuts))]

    def hosted(*refs):
        ins, s_in, o, s_out, scr = (refs[a:b] for a, b in zip([0] + ends[:-1], ends))
        sems = refs[ends[-1]:]
        ids = [pl.program_id(a) for a in range(len(grid))]
        first = functools.reduce(jnp.logical_and, [i == 0 for i in ids])
        last = functools.reduce(jnp.logical_and, [i == g - 1 for i, g in zip(ids, grid)])

        @pl.when(first)
        def _():
            side.start(s_in, s_out, sems)

        body(*ins, *o, *scr)

        @pl.when(last)
        def _():
            side.finish(s_in, s_out, sems)

    call = pl.pallas_call(hosted, name=name, out_shape=outs + list(side.out_shape), grid=grid,
                          in_specs=list(in_specs) + [HBM] * len(side.arrays),
                          out_specs=ospecs + [HBM] * len(side.out_shape),
                          scratch_shapes=list(scratch) + list(side.scratch), compiler_params=params)

    def run(*args):
        res = call(*args, *side.arrays)
        return (res[0] if single else list(res[:len(outs)])), list(res[len(outs):])

    return run


def _rows(tm, w, col=0):
    return pl.BlockSpec((tm, w), lambda i: (i, col))


def _whole(shape):
    return pl.BlockSpec(shape, lambda i: (0,) * len(shape))


def _sds(shape, dtype):
    return jax.ShapeDtypeStruct(shape, dtype)


def _matmul(a, b, name, ta=False, tb=False, out_dtype=F32, side=None):
    (K, M) = a.shape if ta else a.shape[::-1]
    (N, Kb) = b.shape if tb else b.shape[::-1]
    assert K == Kb, (a.shape, b.shape, ta, tb)
    tm = _pick(M, (1024, 1408, 512, 256, 128))
    tn = _pick(N, (1024, 512, 256, 128))
    tk = _pick(K, (2816, 2560, 2048, 1024, 512, 256))
    nk = K // tk
    dn = (((0 if ta else 1,), (1 if tb else 0,)), ((), ()))

    def body(a_ref, b_ref, o_ref, acc_ref):
        k = pl.program_id(2)
        if nk == 1:
            o_ref[...] = lax.dot_general(a_ref[...].astype(BF16), b_ref[...].astype(BF16), dn,
                                         preferred_element_type=F32).astype(out_dtype)
        else:
            @pl.when(k == 0)
            def _():
                acc_ref[...] = jnp.zeros_like(acc_ref)

            acc_ref[...] += lax.dot_general(a_ref[...].astype(BF16), b_ref[...].astype(BF16), dn,
                                            preferred_element_type=F32)

            @pl.when(k == nk - 1)
            def _():
                o_ref[...] = acc_ref[...].astype(out_dtype)

    a_spec = pl.BlockSpec((tk, tm), lambda i, j, k: (k, i)) if ta else pl.BlockSpec((tm, tk), lambda i, j, k: (i, k))
    b_spec = pl.BlockSpec((tn, tk), lambda i, j, k: (j, k)) if tb else pl.BlockSpec((tk, tn), lambda i, j, k: (k, j))
    return _call(body, name, _sds((M, N), out_dtype), (M // tm, N // tn, nk), [a_spec, b_spec],
                 pl.BlockSpec((tm, tn), lambda i, j, k: (i, j)), scratch=[pltpu.VMEM((tm, tn), F32)],
                 sem=("parallel", "parallel", "arbitrary"), side=side)(a, b)


def _sigmoid(x):
    return 1.0 / (1.0 + jnp.exp(-x))


def _rope_group(r, c, sa, sb):
    return r * c + pltpu.roll(r, 32, 1) * sa + pltpu.roll(r, 96, 1) * sb


def _ln_fwd(xs, coefs, g, b, name, want_z):
    S, D = xs[0].shape
    tm = 512
    n = len(xs)

    def body(*refs):
        x_refs, g_ref, b_ref, outs = refs[:n], refs[n], refs[n + 1], refs[n + 2:]
        z = None
        for cf, r in zip(coefs, x_refs):
            t = r[...] if cf == 1.0 else cf * r[...]
            z = t if z is None else z + t
        mu = jnp.mean(z, axis=-1, keepdims=True)
        zc = z - mu
        var = jnp.mean(zc * zc, axis=-1, keepdims=True)
        y = zc * lax.rsqrt(var + LN_EPS) * g_ref[...] + b_ref[...]
        if want_z:
            outs[0][...] = z
        outs[-2][...] = y
        outs[-1][...] = y.astype(BF16)

    out_shape = [_sds((S, D), F32)] * (2 if want_z else 1) + [_sds((S, D), BF16)]
    return _call(body, name, out_shape, (S // tm,), [_rows(tm, D)] * n + [_whole((1, D))] * 2,
                 [_rows(tm, D)] * len(out_shape), sem=("parallel",))(*xs, g, b)


def _matmul_ln(a, w, x, g, b, name):
    S, K = a.shape
    D = w.shape[1]
    tm = _pick(S, (256, 128))

    def body(a_ref, w_ref, x_ref, g_ref, b_ref, z_ref, y_ref, yb_ref):
        z = ALPHA * x_ref[...] + jnp.dot(a_ref[...], w_ref[...], preferred_element_type=F32)
        mu = jnp.mean(z, axis=-1, keepdims=True)
        zc = z - mu
        var = jnp.mean(zc * zc, axis=-1, keepdims=True)
        y = zc * lax.rsqrt(var + LN_EPS) * g_ref[...] + b_ref[...]
        z_ref[...] = z
        y_ref[...] = y
        yb_ref[...] = y.astype(BF16)

    return _call(body, name, [_sds((S, D), F32), _sds((S, D), F32), _sds((S, D), BF16)], (S // tm,),
                 [_rows(tm, K), _whole((K, D)), _rows(tm, D), _whole((1, D)), _whole((1, D))], [_rows(tm, D)] * 3,
                 sem=("parallel",))(a, w, x, g, b)


def _ln_bwd(dys, coefs, z, g, name):
    S, D = z.shape
    tm = 512
    n = len(dys)

    def body(*refs):
        dy_refs, z_ref, g_ref = refs[:n], refs[n], refs[n + 1]
        dz_ref, dzb_ref, dg_ref, db_ref = refs[n + 2:]
        dy = None
        for cf, r in zip(coefs, dy_refs):
            t = r[...] if cf == 1.0 else cf * r[...]
            dy = t if dy is None else dy + t
        zv = z_ref[...]
        mu = jnp.mean(zv, axis=-1, keepdims=True)
        zc = zv - mu
        var = jnp.mean(zc * zc, axis=-1, keepdims=True)
        rstd = lax.rsqrt(var + LN_EPS)
        xh = zc * rstd
        dyg = dy * g_ref[...]
        dz = rstd * (dyg - jnp.mean(dyg, axis=-1, keepdims=True) - xh * jnp.mean(dyg * xh, axis=-1, keepdims=True))
        dz_ref[...] = dz
        dzb_ref[...] = dz.astype(BF16)

        @pl.when(pl.program_id(0) == 0)
        def _():
            dg_ref[...] = jnp.zeros_like(dg_ref)
            db_ref[...] = jnp.zeros_like(db_ref)

        dg_ref[...] += jnp.sum(dy * xh, axis=0, keepdims=True)
        db_ref[...] += jnp.sum(dy, axis=0, keepdims=True)

    return _call(body, name, [_sds((S, D), F32), _sds((S, D), BF16), _sds((1, D), F32), _sds((1, D), F32)],
                 (S // tm,), [_rows(tm, D)] * (n + 1) + [_whole((1, D))],
                 [_rows(tm, D), _rows(tm, D), _whole((1, D)), _whole((1, D))], sem=("arbitrary",))(*dys, z, g)


def _rms(x, g):
    return x * lax.rsqrt(jnp.mean(x * x, axis=-1, keepdims=True) + RMS_EPS) * g


def _prep1(h, tabs, qg, kvg, name):
    S = h.shape[0]
    tm = 512
    cm, sam, sbm, cr, sr = tabs

    def body(h_ref, cm_ref, sam_ref, sbm_ref, cr_ref, sr_ref, qg_ref, kvg_ref,
             qn_ref, kvn_ref, kr_ref, rq_ref, rk_ref, rv_ref):
        qn_ref[...] = _rms(h_ref[:, 0:Q_LORA], qg_ref[...]).astype(BF16)
        kvn_ref[...] = _rms(h_ref[:, Q_LORA:Q_LORA + KV_LORA], kvg_ref[...]).astype(BF16)
        kr_ref[...] = _rope_group(h_ref[:, 768:896], cm_ref[...], sam_ref[...], sbm_ref[...])
        c, s = cr_ref[...], sr_ref[...]
        for hd in range(RET_HEADS):
            for src, dst, scale in ((MLA_IN, rq_ref, RET_SCALE), (MLA_IN + 1024, rk_ref, None)):
                t1 = h_ref[:, src + hd * 256:src + hd * 256 + 128]
                t2 = h_ref[:, src + hd * 256 + 128:src + hd * 256 + 256]
                o1, o2 = t1 * c - t2 * s, t2 * c + t1 * s
                if scale is not None:
                    o1, o2 = o1 * scale, o2 * scale
                dst[:, hd * 256:hd * 256 + 128] = o1.astype(BF16)
                dst[:, hd * 256 + 128:hd * 256 + 256] = o2.astype(BF16)
        rv_ref[...] = h_ref[:, MLA_IN + 2048:MLA_IN + 3072].astype(BF16)

    t128 = _rows(tm, LANES)
    return _call(body, name,
                 [_sds((S, Q_LORA), BF16), _sds((S, KV_LORA), BF16), _sds((S, LANES), F32),
                  _sds((S, 1024), BF16), _sds((S, 1024), BF16), _sds((S, 1024), BF16)],
                 (S // tm,),
                 [_rows(tm, D_IN_PAD), t128, t128, t128, t128, t128, _whole((1, Q_LORA)), _whole((1, KV_LORA))],
                 [_rows(tm, Q_LORA), _rows(tm, KV_LORA), t128, _rows(tm, 1024), _rows(tm, 1024), _rows(tm, 1024)],
                 sem=("parallel",))(h, cm, sam, sbm, cr, sr, qg, kvg)


def _prep1_bwd(dqn, dkvn, dkr, drq, drk, drv, drg, h, tabs, qg, kvg, name):
    S = h.shape[0]
    tm = 512
    cm, sam, sbm, cr, sr = tabs

    def rms_bwd(x, g, dy):
        r = lax.rsqrt(jnp.mean(x * x, axis=-1, keepdims=True) + RMS_EPS)
        dyg = dy * g
        dx = r * dyg - x * (r * r * r) * jnp.mean(dyg * x, axis=-1, keepdims=True)
        return dx, jnp.sum(dy * x * r, axis=0, keepdims=True)

    def body(dqn_ref, dkvn_ref, dkr_ref, drq_ref, drk_ref, drv_ref, drg_ref, h_ref,
             cm_ref, sam_ref, sbm_ref, cr_ref, sr_ref, qg_ref, kvg_ref, dh_ref, dqg_ref, dkvg_ref):
        dcq, dqg = rms_bwd(h_ref[:, 0:Q_LORA], qg_ref[...], dqn_ref[...])
        dckv, dkvg = rms_bwd(h_ref[:, Q_LORA:Q_LORA + KV_LORA], kvg_ref[...], dkvn_ref[...])
        dh_ref[:, 0:Q_LORA] = dcq.astype(BF16)
        dh_ref[:, Q_LORA:Q_LORA + KV_LORA] = dckv.astype(BF16)
        dh_ref[:, 768:896] = _rope_group(dkr_ref[...], cm_ref[...], -sam_ref[...], -sbm_ref[...]).astype(BF16)
        dh_ref[:, 896:1024] = jnp.zeros((tm, LANES), BF16)
        c, s = cr_ref[...], sr_ref[...]
        for hd in range(RET_HEADS):
            for src, dst, scale in ((drq_ref, MLA_IN, RET_SCALE), (drk_ref, MLA_IN + 1024, None)):
                d1 = src[:, hd * 256:hd * 256 + 128]
                d2 = src[:, hd * 256 + 128:hd * 256 + 256]
                if scale is not None:
                    d1, d2 = d1 * scale, d2 * scale
                dh_ref[:, dst + hd * 256:dst + hd * 256 + 128] = (d1 * c + d2 * s).astype(BF16)
                dh_ref[:, dst + hd * 256 + 128:dst + hd * 256 + 256] = (d2 * c - d1 * s).astype(BF16)
        dh_ref[:, MLA_IN + 2048:MLA_IN + 3072] = drv_ref[...].astype(BF16)
        dh_ref[:, MLA_IN + 3072:MLA_IN + 4096] = drg_ref[...].astype(BF16)

        @pl.when(pl.program_id(0) == 0)
        def _():
            dqg_ref[...] = jnp.zeros_like(dqg_ref)
            dkvg_ref[...] = jnp.zeros_like(dkvg_ref)

        dqg_ref[...] += dqg
        dkvg_ref[...] += dkvg

    t128 = _rows(tm, LANES)
    return _call(body, name,
                 [_sds((S, D_IN_PAD), BF16), _sds((1, Q_LORA), F32), _sds((1, KV_LORA), F32)],
                 (S // tm,),
                 [_rows(tm, Q_LORA), _rows(tm, KV_LORA), t128, _rows(tm, 1024), _rows(tm, 1024), _rows(tm, 1024),
                  _rows(tm, 1024), _rows(tm, MLA_IN), t128, t128, t128, t128, t128,
                  _whole((1, Q_LORA)), _whole((1, KV_LORA))],
                 [_rows(tm, D_IN_PAD), _whole((1, Q_LORA)), _whole((1, KV_LORA))],
                 sem=("arbitrary",))(dqn, dkvn, dkr, drq, drk, drv, drg, h, cm, sam, sbm, cr, sr, qg, kvg)


def _prep2(q, kv, kr, tabs, name):
    S = q.shape[0]
    tm = 512
    cm, sam, sbm = tabs[:3]

    def body(q_ref, kv_ref, kr_ref, cm_ref, sam_ref, sbm_ref, qo_ref, ko_ref, vo_ref):
        c, sa, sb = cm_ref[...], sam_ref[...], sbm_ref[...]
        krb = kr_ref[...].astype(BF16)
        ones = jnp.ones((tm, LANES), BF16)
        for hd in range(MLA_HEADS):
            o = hd * HEAD_PAD
            qo_ref[:, o:o + 128] = (q_ref[:, o:o + 128] * MLA_SCALE).astype(BF16)
            qo_ref[:, o + 128:o + 256] = (_rope_group(q_ref[:, o + 128:o + 256], c, sa, sb) * MLA_SCALE).astype(BF16)
            ko_ref[:, o:o + 128] = kv_ref[:, hd * 128:hd * 128 + 128].astype(BF16)
            ko_ref[:, o + 128:o + 256] = krb
            vo_ref[:, o:o + 128] = kv_ref[:, 1024 + hd * 128:1024 + hd * 128 + 128].astype(BF16)
            vo_ref[:, o + 128:o + 256] = ones

    t128 = _rows(tm, LANES)
    return _call(body, name, [_sds((S, 2048), BF16)] * 3, (S // tm,),
                 [_rows(tm, 2048), _rows(tm, 2048), t128, t128, t128, t128],
                 [_rows(tm, 2048)] * 3, sem=("parallel",))(q, kv, kr, cm, sam, sbm)


def _prep2_bwd(dqm, dkm, dvm, tabs, name):
    S = dqm.shape[0]
    tm = 512
    cm, sam, sbm = tabs[:3]

    def body(dq_ref, dk_ref, dv_ref, cm_ref, sam_ref, sbm_ref, dqo_ref, dkvo_ref, dkr_ref):
        c, sa, sb = cm_ref[...], -sam_ref[...], -sbm_ref[...]
        dkr = None
        for hd in range(MLA_HEADS):
            o = hd * HEAD_PAD
            dqo_ref[:, o:o + 128] = (dq_ref[:, o:o + 128] * MLA_SCALE).astype(BF16)
            dqo_ref[:, o + 128:o + 256] = (_rope_group(dq_ref[:, o + 128:o + 256], c, sa, sb) * MLA_SCALE).astype(BF16)
            dkvo_ref[:, hd * 128:hd * 128 + 128] = dk_ref[:, o:o + 128].astype(BF16)
            t = dk_ref[:, o + 128:o + 256]
            dkr = t if dkr is None else dkr + t
        dkvo_ref[:, 1024:2048] = dv_ref[...].astype(BF16)
        dkr_ref[...] = dkr

    t128 = _rows(tm, LANES)
    return _call(body, name, [_sds((S, 2048), BF16), _sds((S, 2048), BF16), _sds((S, LANES), F32)], (S // tm,),
                 [_rows(tm, 2048), _rows(tm, 2048), _rows(tm, 1024), t128, t128, t128],
                 [_rows(tm, 2048), _rows(tm, 2048), t128], sem=("parallel",))(dqm, dkm, dvm, cm, sam, sbm)


def _gn_gate(a, o, h, gg, gb, name):
    S = a.shape[0]
    tm = 512

    def body(a_ref, o_ref, rg_ref, gg_ref, gb_ref, mix_ref):
        mix_ref[:, 0:1024] = a_ref[...].astype(BF16)
        for hd in range(RET_HEADS):
            sl = slice(hd * 256, hd * 256 + 256)
            ov = o_ref[:, sl]
            mu = jnp.mean(ov, axis=-1, keepdims=True)
            oc = ov - mu
            var = jnp.mean(oc * oc, axis=-1, keepdims=True)
            y = oc * lax.rsqrt(var + GN_EPS) * gg_ref[:, sl] + gb_ref[:, sl]
            rg = rg_ref[:, sl]
            mix_ref[:, 1024 + hd * 256:1024 + hd * 256 + 256] = (rg * _sigmoid(rg) * y).astype(BF16)

    return _call(body, name, _sds((S, 2048), BF16), (S // tm,),
                 [_rows(tm, 1024), _rows(tm, 1024), _rows(tm, 1024, 4), _whole((1, 1024)), _whole((1, 1024))],
                 _rows(tm, 2048), sem=("parallel",))(a, o, h, gg, gb)


def _gn_gate_bwd(dmixin, o, h, gg, gb, name):
    S = o.shape[0]
    tm = 512

    def body(dr_ref, o_ref, rg_ref, gg_ref, gb_ref, do_ref, drg_ref, dgg_ref, dgb_ref):
        @pl.when(pl.program_id(0) == 0)
        def _():
            dgg_ref[...] = jnp.zeros_like(dgg_ref)
            dgb_ref[...] = jnp.zeros_like(dgb_ref)

        for hd in range(RET_HEADS):
            sl = slice(hd * 256, hd * 256 + 256)
            ov = o_ref[:, sl]
            mu = jnp.mean(ov, axis=-1, keepdims=True)
            oc = ov - mu
            var = jnp.mean(oc * oc, axis=-1, keepdims=True)
            rstd = lax.rsqrt(var + GN_EPS)
            xh = oc * rstd
            g = gg_ref[:, sl]
            y = xh * g + gb_ref[:, sl]
            rg = rg_ref[:, sl]
            sg = _sigmoid(rg)
            dr = dr_ref[:, sl]
            dy = dr * (rg * sg)
            drg_ref[:, sl] = dr * y * (sg * (1.0 + rg * (1.0 - sg)))
            dgg_ref[:, sl] += jnp.sum(dy * xh, axis=0, keepdims=True)
            dgb_ref[:, sl] += jnp.sum(dy, axis=0, keepdims=True)
            dxh = dy * g
            do = rstd * (dxh - jnp.mean(dxh, axis=-1, keepdims=True) - xh * jnp.mean(dxh * xh, axis=-1, keepdims=True))
            do_ref[:, sl] = do.astype(BF16)

    return _call(body, name,
                 [_sds((S, 1024), BF16), _sds((S, 1024), F32), _sds((1, 1024), F32), _sds((1, 1024), F32)],
                 (S // tm,),
                 [_rows(tm, 1024, 1), _rows(tm, 1024), _rows(tm, 1024, 4), _whole((1, 1024)), _whole((1, 1024))],
                 [_rows(tm, 1024), _rows(tm, 1024), _whole((1, 1024)), _whole((1, 1024))],
                 sem=("arbitrary",))(dmixin, o, h, gg, gb)


GU_BLOCK = D_FF // N_CHIPS


def _matmul_swiglu(x, w_gu, name, side=None):
    S, K = x.shape
    tm = _pick(S, (512, 256, 128))
    tn = 2 * GU_BLOCK

    def body(x_ref, w_ref, gu_ref, act_ref):
        r = jnp.dot(x_ref[...], w_ref[...], preferred_element_type=F32)
        g, u = r[:, :GU_BLOCK], r[:, GU_BLOCK:]
        gu_ref[...] = r.astype(BF16)
        act_ref[...] = (g * _sigmoid(g) * u).astype(BF16)

    return _call(body, name, [_sds((S, 2 * D_FF), BF16), _sds((S, D_FF), BF16)], (S // tm, N_CHIPS),
                 [pl.BlockSpec((tm, K), lambda i, j: (i, 0)), pl.BlockSpec((K, tn), lambda i, j: (0, j))],
                 [pl.BlockSpec((tm, tn), lambda i, j: (i, j)), pl.BlockSpec((tm, GU_BLOCK), lambda i, j: (i, j))],
                 sem=("parallel", "parallel"), side=side)(x, w_gu)


def _matmul_swiglu_bwd(df, w_down, gu, name, side=None):
    S, K = df.shape
    tm = _pick(S, (512, 256, 128))

    def body(df_ref, w_ref, gu_ref, o_ref):
        d = _dot_nt(df_ref[...], w_ref[...])
        g = gu_ref[:, :GU_BLOCK].astype(F32)
        u = gu_ref[:, GU_BLOCK:].astype(F32)
        sg = _sigmoid(g)
        o_ref[:, :GU_BLOCK] = (d * u * (sg * (1.0 + g * (1.0 - sg)))).astype(BF16)
        o_ref[:, GU_BLOCK:] = (d * (g * sg)).astype(BF16)

    gu_spec = pl.BlockSpec((tm, 2 * GU_BLOCK), lambda i, j: (i, j))
    return _call(body, name, _sds((S, 2 * D_FF), BF16), (S // tm, N_CHIPS),
                 [pl.BlockSpec((tm, K), lambda i, j: (i, 0)), pl.BlockSpec((GU_BLOCK, K), lambda i, j: (j, 0)), gu_spec],
                 gu_spec, sem=("parallel", "parallel"), side=side)(df, w_down, gu)


def _ln_loss(x, f, g, b, target, name):
    S, D = x.shape
    tm = 256

    def body(x_ref, f_ref, g_ref, b_ref, t_ref, z_ref, dy_ref, acc_ref):
        z = ALPHA * x_ref[...] + f_ref[...]
        mu = jnp.mean(z, axis=-1, keepdims=True)
        zc = z - mu
        var = jnp.mean(zc * zc, axis=-1, keepdims=True)
        e = zc * lax.rsqrt(var + LN_EPS) * g_ref[...] + b_ref[...] - t_ref[...]
        z_ref[...] = z
        dy_ref[...] = e / D

        @pl.when(pl.program_id(0) == 0)
        def _():
            acc_ref[...] = jnp.zeros_like(acc_ref)

        acc_ref[...] += jnp.sum(e * e, axis=0, keepdims=True)

    return _call(body, name, [_sds((S, D), F32), _sds((S, D), F32), _sds((1, D), F32)], (S // tm,),
                 [_rows(tm, D), _rows(tm, D), _whole((1, D)), _whole((1, D)), _rows(tm, D)],
                 [_rows(tm, D), _rows(tm, D), _whole((1, D))], sem=("arbitrary",))(x, f, g, b, target)


def _chunk_mask(T):
    r = lax.shift_right_logical(lax.broadcasted_iota(jnp.int32, (T, T), 0), 6)
    c = lax.shift_right_logical(lax.broadcasted_iota(jnp.int32, (T, T), 1), 6)
    return r >= c


def _dot_nt(a, b):
    return lax.dot_general(a, b, (((1,), (1,)), ((), ())), preferred_element_type=F32)


def _dot_tn(a, b):
    return lax.dot_general(a, b, (((0,), (0,)), ((), ())), preferred_element_type=F32)


def _decay_tables(T):
    lg = jnp.log1p(-jnp.exp2(-5.0 - jnp.arange(RET_HEADS, dtype=F32)))
    idx = jnp.arange(T, dtype=F32)
    diff = idx[:, None] - idx[None, :]
    rel = jnp.exp(lg[:, None, None] * diff[None])
    cid = jnp.arange(T) // CHUNK
    mask = (cid[:, None] >= cid[None, :]).astype(F32)
    reld = jnp.exp(lg[:, None, None] * jnp.abs(diff)[None]) * mask[None]
    lgrow = jnp.broadcast_to(lg[:, None, None], (RET_HEADS, 1, LANES))
    return lgrow, rel, reld


def _attn_fwd(q, k, v, heads, dk, dv, softmax, name, tables=None, side=None):
    S = q.shape[0]
    T = ATT_BLOCK
    nq = S // T
    rep = T // LANES
    vw = 2 * dv if softmax else dv
    assert not softmax or dv == LANES

    def body(*refs):
        if softmax:
            q_ref, k_ref, v_ref, o_ref, lse_ref, m_sc, acc_sc = refs
        else:
            q_ref, k_ref, v_ref, lg_ref, rel_ref, reld_ref, o_ref, acc_sc = refs
        i = pl.program_id(1)
        qv = q_ref[...]

        def kv_block(j):
            rows = pl.ds(pl.multiple_of(j * T, T), T)
            return k_ref[rows, :], v_ref[rows, :]

        kb, vb = kv_block(i)
        s = _dot_nt(qv, kb)
        if softmax:
            s = jnp.where(_chunk_mask(T), s, NEG)
            m = jnp.max(s, axis=-1, keepdims=True)
            p = jnp.exp(s - m)
            m_sc[...] = jnp.broadcast_to(m, (T, LANES))
        else:
            p = s * reld_ref[0]
        acc_sc[...] = jnp.dot(p.astype(BF16), vb, preferred_element_type=F32)

        def scores(j):
            kb, vb = kv_block(j)
            return _dot_nt(qv, kb), vb

        def update(j, s, vb):
            if softmax:
                m_prev = m_sc[...]
                m_next = jnp.maximum(m_prev, jnp.max(s, axis=-1, keepdims=True))
                alpha = jnp.exp(m_prev - m_next)
                p = jnp.exp(s - jnp.tile(m_next, (1, rep)))
                m_sc[...] = m_next
                acc_sc[...] = acc_sc[...] * jnp.tile(alpha, (1, vw // LANES)) + jnp.dot(
                    p.astype(BF16), vb, preferred_element_type=F32)
            else:
                fac = jnp.exp(lg_ref[0] * ((i - j) * T).astype(F32))
                p = s * (rel_ref[0] * jnp.tile(fac, (1, rep)))
                acc_sc[...] += jnp.dot(p.astype(BF16), vb, preferred_element_type=F32)

        def pair(jj, carry):
            first, second = scores(2 * jj), scores(2 * jj + 1)
            update(2 * jj, *first)
            update(2 * jj + 1, *second)
            return carry

        lax.fori_loop(0, i // 2, pair, 0)

        @pl.when(i % 2 == 1)
        def _():
            update(i - 1, *scores(i - 1))

        if softmax:
            l = acc_sc[:, dv:]
            o_ref[...] = acc_sc[:, :dv] / l
            lse_ref[...] = m_sc[...] + jnp.log(l)
        else:
            o_ref[...] = acc_sc[...]

    in_specs = [pl.BlockSpec((T, dk), lambda h, i: (i, h)), pl.BlockSpec((S, dk), lambda h, i: (0, h)),
                pl.BlockSpec((S, vw), lambda h, i: (0, h))]
    o_spec = pl.BlockSpec((T, dv), lambda h, i: (i, h))
    if softmax:
        return _call(body, name, [_sds((S, heads * dv), F32), _sds((S, heads * LANES), F32)], (heads, nq), in_specs,
                     [o_spec, pl.BlockSpec((T, LANES), lambda h, i: (i, h))],
                     scratch=[pltpu.VMEM((T, LANES), F32), pltpu.VMEM((T, vw), F32)],
                     sem=("parallel", "arbitrary"), side=side)(q, k, v)
    lgrow, rel, reld = tables
    in_specs += [pl.BlockSpec((1, 1, LANES), lambda h, i: (h, 0, 0)), pl.BlockSpec((1, T, T), lambda h, i: (h, 0, 0)),
                 pl.BlockSpec((1, T, T), lambda h, i: (h, 0, 0))]
    return _call(body, name, _sds((S, heads * dv), F32), (heads, nq), in_specs, o_spec,
                 scratch=[pltpu.VMEM((T, dv), F32)], sem=("parallel", "arbitrary"), side=side)(q, k, v, lgrow, rel, reld)


def _attn_bwd(q, k, v, do, heads, dk, dv, softmax, name, o=None, lse=None, tables=None, side=None):
    S = q.shape[0]
    T = ATT_BLOCK
    nq = S // T
    rep = T // LANES

    def body(*refs):
        if softmax:
            q_ref, k_ref, v_ref, do_ref, o_ref, lse_ref, dq_ref, dk_ref, dv_ref, dq_sc = refs
        else:
            q_ref, k_ref, v_ref, do_ref, lg_ref, rel_ref, reld_ref, dq_ref, dk_ref, dv_ref, dq_sc = refs
        i = pl.program_id(1)

        @pl.when(i == 0)
        def _():
            dk_ref[...] = jnp.zeros_like(dk_ref)
            dv_ref[...] = jnp.zeros_like(dv_ref)

        qv = q_ref[...]
        dof = do_ref[...].astype(F32)
        dov = dof.astype(BF16)
        if softmax:
            delta = jnp.sum(dof * o_ref[...], axis=-1, keepdims=True)
            lse_t = jnp.tile(lse_ref[...], (1, rep))
        dq_sc[...] = jnp.zeros_like(dq_sc)

        def products(j):
            rows = pl.ds(pl.multiple_of(j * T, T), T)
            kb = k_ref[rows, :]
            return rows, kb, _dot_nt(qv, kb), _dot_nt(dov, v_ref[rows, :])

        def block(j, diagonal, rows, kb, s, dp):
            if softmax:
                if diagonal:
                    s = jnp.where(_chunk_mask(T), s, NEG)
                p = jnp.exp(s - lse_t)
                ds = p * (dp - delta)
            else:
                if diagonal:
                    dec = reld_ref[0]
                else:
                    fac = jnp.exp(lg_ref[0] * ((i - j) * T).astype(F32))
                    dec = rel_ref[0] * jnp.tile(fac, (1, rep))
                p = s * dec
                ds = dp * dec
            dsb = ds.astype(BF16)
            dv_ref[rows, :] += _dot_tn(p.astype(BF16), dov)
            dk_ref[rows, :] += _dot_tn(dsb, qv)
            dq_sc[...] += jnp.dot(dsb, kb, preferred_element_type=F32)

        block(i, True, *products(i))

        def pair(jj, carry):
            first, second = products(2 * jj), products(2 * jj + 1)
            block(2 * jj, False, *first)
            block(2 * jj + 1, False, *second)
            return carry

        lax.fori_loop(0, i // 2, pair, 0)

        @pl.when(i % 2 == 1)
        def _():
            block(i - 1, False, *products(i - 1))

        dq_ref[...] = dq_sc[...]

    qspec = pl.BlockSpec((T, dk), lambda h, i: (i, h))
    kspec = pl.BlockSpec((S, dk), lambda h, i: (0, h))
    vspec = pl.BlockSpec((S, dv), lambda h, i: (0, h))
    dospec = pl.BlockSpec((T, dv), lambda h, i: (i, h))
    in_specs = [qspec, kspec, vspec, dospec]
    args = [q, k, v, do]
    if softmax:
        in_specs[2] = pl.BlockSpec((S, dv), lambda h, i: (0, 2 * h))
        in_specs += [dospec, pl.BlockSpec((T, LANES), lambda h, i: (i, h))]
        args += [o, lse]
    else:
        in_specs += [pl.BlockSpec((1, 1, LANES), lambda h, i: (h, 0, 0)),
                     pl.BlockSpec((1, T, T), lambda h, i: (h, 0, 0)), pl.BlockSpec((1, T, T), lambda h, i: (h, 0, 0))]
        args += list(tables)
    return _call(body, name, [_sds((S, heads * dk), F32), _sds((S, heads * dk), F32), _sds((S, heads * dv), F32)],
                 (heads, nq), in_specs, [qspec, kspec, vspec], scratch=[pltpu.VMEM((T, dk), F32)],
                 sem=("parallel", "arbitrary"), side=side)(*args)


def _rope_tables(pos):
    def tables(dim):
        inv_freq = ROPE_THETA ** (-jnp.arange(0, dim, 2, dtype=F32) / dim)
        ang = pos.astype(F32)[:, None] * inv_freq
        return jnp.cos(ang), jnp.sin(ang)

    cm, sm = tables(ROPE)
    S = pos.shape[0]
    z32, z64 = jnp.zeros((S, 32), F32), jnp.zeros((S, 64), F32)
    cr, sr = tables(RET_DK)
    return (jnp.concatenate([cm, cm, z64], 1), jnp.concatenate([z32, sm, z64], 1),
            jnp.concatenate([-sm, z32, z64], 1), cr, sr)


def _row(v):
    return v.reshape(1, -1).astype(F32)


def _local_step(x, pos, target, pipe, P):
    tabs = _rope_tables(pos)
    dtabs = _decay_tables(ATT_BLOCK)
    xf, xb = _ln_fwd([x], [1.0], _row(P["ln_in_g"]), _row(P["ln_in_b"]), "ln_in", False)
    pipe.gather_first()
    saved = []
    for l in range(DEPTH):
        w = functools.partial(pipe.weight, l)
        t = f"_l{l}"
        h = pipe.run(_matmul, "mm_h" + t, xb, w("w_in"))
        qn, kvn, kr, rq, rk, rv = _prep1(h, tabs, _row(P["q_norm_g"][l]), _row(P["kv_norm_g"][l]), "prep1" + t)
        q = _matmul(qn, w("w_uq"), "mm_q" + t)
        kv = _matmul(kvn, w("w_ukv"), "mm_kv" + t)
        qm, km, vm = _prep2(q, kv, kr, tabs, "prep2" + t)
        a, lse = pipe.run(_attn_fwd, "mla_fwd" + t, qm, km, vm, MLA_HEADS, HEAD_PAD, VDIM, True)
        o = pipe.run(_attn_fwd, "ret_fwd" + t, rq, rk, rv, RET_HEADS, RET_DK, RET_DV, False, tables=dtabs)
        mixin = _gn_gate(a, o, h, _row(P["ret_gn_g"][l]), _row(P["ret_gn_b"][l]), "gn_gate" + t)
        z1, x1f, x1b = _matmul_ln(mixin, w("w_out"), xf, _row(P["ln1_g"][l]), _row(P["ln1_b"][l]), "mm_mix_ln1" + t)
        gu, act = pipe.run(_matmul_swiglu, "mm_gu" + t, x1b, w("w_gu"))
        f = pipe.run(_matmul, "mm_down" + t, act, w("w_down"))
        g2, b2 = _row(P["ln2_g"][l]), _row(P["ln2_b"][l])
        saved.append(dict(xb=xb, h=h, qn=qn, kvn=kvn, rq=rq, rk=rk, rv=rv, qm=qm, km=km, vm=vm, a=a, lse=lse, o=o,
                          mixin=mixin, z1=z1, x1b=x1b, gu=gu, act=act))
        if l == DEPTH - 1:
            saved[l]["z2"], dy, sqerr = _ln_loss(x1f, f, g2, b2, target, "ln2_loss" + t)
        else:
            saved[l]["z2"], xf, xb = _ln_fwd([x1f, f], [ALPHA, 1.0], g2, b2, "ln2" + t, True)

    dP = {}
    dys, coefs = [dy], [1.0]
    for l in reversed(range(DEPTH)):
        w, sv = functools.partial(pipe.weight, l), saved[l]
        t = f"_l{l}"
        dz2, dz2b, dg, db = _ln_bwd(dys, coefs, sv["z2"], _row(P["ln2_g"][l]), "ln2_bwd" + t)
        dP[("ln2_g", l)], dP[("ln2_b", l)] = dg, db
        pipe.reduce(l, w_down=pipe.run(_matmul, "mm_dw_down" + t, sv["act"], dz2b, ta=True, out_dtype=BF16))
        dgu = pipe.run(_matmul_swiglu_bwd, "mm_dact" + t, dz2b, w("w_down"), sv["gu"])
        pipe.reduce(l, w_gu=pipe.run(_matmul, "mm_dw_gu" + t, sv["x1b"], dgu, ta=True, out_dtype=BF16))
        dx1 = pipe.run(_matmul, "mm_dx1" + t, dgu, w("w_gu"), tb=True)
        dz1, dz1b, dg, db = _ln_bwd([dz2, dx1], [ALPHA, 1.0], sv["z1"], _row(P["ln1_g"][l]), "ln1_bwd" + t)
        dP[("ln1_g", l)], dP[("ln1_b", l)] = dg, db
        pipe.reduce(l, w_out=_matmul(sv["mixin"], dz1b, "mm_dw_out" + t, ta=True, out_dtype=BF16))
        dmixin = pipe.run(_matmul, "mm_dmixin" + t, dz1b, w("w_out"), tb=True)
        do, drg, dgg, dgb = _gn_gate_bwd(dmixin, sv["o"], sv["h"], _row(P["ret_gn_g"][l]), _row(P["ret_gn_b"][l]),
                                         "gn_gate_bwd" + t)
        dP[("ret_gn_g", l)], dP[("ret_gn_b", l)] = dgg, dgb
        drq, drk, drv = pipe.run(_attn_bwd, "ret_bwd" + t, sv["rq"], sv["rk"], sv["rv"], do, RET_HEADS, RET_DK, RET_DV,
                                 False, tables=dtabs)
        dqm, dkm, dvm = pipe.run(_attn_bwd, "mla_bwd" + t, sv["qm"], sv["km"], sv["vm"], dmixin, MLA_HEADS, HEAD_PAD,
                                 VDIM, True, o=sv["a"], lse=sv["lse"])
        dq, dkv, dkr = _prep2_bwd(dqm, dkm, dvm, tabs, "prep2_bwd" + t)
        g_uq = _matmul(sv["qn"], dq, "mm_dw_uq" + t, ta=True, out_dtype=BF16)
        dqn = _matmul(dq, w("w_uq"), "mm_dqn" + t, tb=True)
        g_ukv = _matmul(sv["kvn"], dkv, "mm_dw_ukv" + t, ta=True, out_dtype=BF16)
        dkvn = _matmul(dkv, w("w_ukv"), "mm_dkvn" + t, tb=True)
        dh, dqg, dkvg = _prep1_bwd(dqn, dkvn, dkr, drq, drk, drv, drg, sv["h"], tabs, _row(P["q_norm_g"][l]),
                                   _row(P["kv_norm_g"][l]), "prep1_bwd" + t)
        dP[("q_norm_g", l)], dP[("kv_norm_g", l)] = dqg, dkvg
        pipe.reduce(l, w_uq=g_uq, w_ukv=g_ukv,
                    w_in=pipe.run(_matmul, "mm_dw_in" + t, sv["xb"], dh, ta=True, out_dtype=BF16))
        dxl = pipe.run(_matmul, "mm_dxl" + t, dh, w("w_in"), tb=True)
        dys, coefs = [dz1, dxl], [ALPHA, 1.0]
    grad_x, _, dg, db = _ln_bwd(dys, coefs, x, _row(P["ln_in_g"]), "ln_in_bwd")
    dP[("ln_in_g", None)], dP[("ln_in_b", None)] = dg, db
    return sqerr, grad_x, dP


INTERNAL_OF = {"w_in": ("w_in",), "w_uq": ("w_uq",), "w_ukv": ("w_ukv",), "w_out": ("w_out",),
               "w_gu": ("w_gate", "w_up"), "w_down": ("w_down",)}
ROW_PIECES = {"w_up": 1024}


def _internal_weight(name, *blocks):
    cat = lambda parts: jnp.concatenate(parts, axis=1)
    cols = lambda b: cat([b[j] for j in range(N_CHIPS)])
    b = blocks[0]
    if name in ("w_out", "w_down"):
        return b.reshape(-1, b.shape[-1])
    if name == "w_gu":
        return cat([blk[j] for j in range(N_CHIPS) for blk in blocks])
    if name == "w_in":
        return cat([b[0][:, :MLA_IN_USED], jnp.zeros((D_MODEL, MLA_IN - MLA_IN_USED), BF16), b[0][:, MLA_IN_USED:]]
                   + [b[j] for j in range(1, N_CHIPS)])
    if name == "w_uq":
        uq, hw = cols(b), NOPE + ROPE
        pad = jnp.zeros((Q_LORA, HEAD_PAD - hw), BF16)
        return cat([p for h in range(MLA_HEADS) for p in (uq[:, h * hw:(h + 1) * hw], pad)])
    ukv = cols(b)
    return cat([ukv[:, 256 * h:256 * h + NOPE] for h in range(MLA_HEADS)]
               + [ukv[:, 256 * h + NOPE:256 * h + 256] for h in range(MLA_HEADS)])


def _grad_shards(name, g):
    cat = lambda parts: jnp.concatenate(parts, axis=1)
    if name in ("w_out", "w_down"):
        return {name: g.reshape(N_CHIPS, -1, g.shape[-1])}
    if name == "w_gu":
        return {"w_gate": _ColBlocks(g, 0), "w_up": _ColBlocks(g, 1)}
    if name == "w_in":
        ci, shift = BIG_SHARD["w_in"][1], MLA_IN - MLA_IN_USED
        return {name: [cat([g[:, :MLA_IN_USED], g[:, MLA_IN:ci + shift]])]
                + [g[:, ci * j + shift:ci * (j + 1) + shift] for j in range(1, N_CHIPS)]}
    if name == "w_uq":
        cq = NOPE + ROPE
        return {name: [cat([g[:, HEAD_PAD * h:HEAD_PAD * h + cq] for h in (2 * j, 2 * j + 1)]) for j in range(N_CHIPS)]}
    return {name: [cat([g[:, o + NOPE * h:o + NOPE * (h + 1)] for h in (2 * j, 2 * j + 1) for o in (0, MLA_HEADS * NOPE)])
                   for j in range(N_CHIPS)]}


def _small_layout(P):
    out, at = {}, 0
    for n in SMALL:
        out[n] = (at, P[n].size)
        at += P[n].size
    return out, at


def _flatten_small(P, last):
    v = jnp.concatenate([P[n].reshape(-1).astype(F32) for n in SMALL] + [last.reshape(-1).astype(F32)])
    return jnp.pad(v, (0, SMALL_ROWS * FLAT_W - v.size)).reshape(SMALL_ROWS, FLAT_W)


def _place():
    return lax.axis_index("x"), lax.axis_index("y"), lax.axis_index("c")


def _other_chips(x, y):
    return [(1 - x, y), (x, 1 - y), (1 - x, 1 - y)]


def _rcopy(src, dst, ssem, rsem, dev):
    return pltpu.make_async_remote_copy(src_ref=src, dst_ref=dst, send_sem=ssem, recv_sem=rsem, device_id=dev,
                                        device_id_type=MESH)


def _comm_call(body, name, out_shape, n_in, scratch):
    many = isinstance(out_shape, (list, tuple))
    return pl.pallas_call(body, name=name, out_shape=out_shape, in_specs=[HBM] * n_in,
                          out_specs=[HBM] * len(out_shape) if many else HBM, scratch_shapes=scratch)


def _half(ref, which):
    rows = ref.shape[0] // 2
    return ref.at[pl.ds(pl.multiple_of(which * rows, 16), rows)]


def _dma_sems(n):
    return pltpu.SemaphoreType.DMA((n,))


def _allgather_side(ws):
    k = len(ws)

    def peers():
        x, y, c = _place()
        return c, 2 * x + y, (x, y, 1 - c), [(n, t, cx, cy) for n in range(k) for t, (cx, cy) in enumerate(_other_chips(x, y))]

    def outgoing(w_refs, g_refs, sems):
        ssem, rsem, _, _, ossem, orsem = sems
        c, j, sib, nt = peers()
        owns = [_rcopy(w_refs[n], g_refs[n].at[j], ossem.at[n], orsem.at[n], sib) for n in range(k)]
        sends = [_rcopy(_half(w_refs[n], c), _half(g_refs[n].at[j], c), ssem.at[3 * n + t], rsem.at[3 * n + t],
                        (cx, cy, c)) for n, t, cx, cy in nt]
        return owns, sends

    def incoming(g_refs, sems):
        ssem, rsem, fssem, frsem, _, _ = sems
        c, _, sib, nt = peers()
        landed, passed, relayed = [], [], []
        for n, t, cx, cy in nt:
            mine, other = (_half(g_refs[n].at[2 * cx + cy], h) for h in (c, 1 - c))
            landed.append(_rcopy(mine, mine, ssem.at[3 * n + t], rsem.at[3 * n + t], (cx, cy, c)))
            passed.append(_rcopy(mine, mine, fssem.at[3 * n + t], frsem.at[3 * n + t], sib))
            relayed.append(_rcopy(other, other, fssem.at[3 * n + t], frsem.at[3 * n + t], sib))
        return landed, passed, relayed

    def start(w_refs, g_refs, sems):
        owns, sends = outgoing(w_refs, g_refs, sems)
        for cp in sends + owns:
            cp.start()

    def finish(w_refs, g_refs, sems):
        owns, sends = outgoing(w_refs, g_refs, sems)
        landed, passed, relayed = incoming(g_refs, sems)
        for got, on in zip(landed, passed):
            got.wait_recv()
            on.start()
        for cp in relayed:
            cp.wait_recv()
        for cp in owns:
            cp.wait()
        for cp in sends + passed:
            cp.wait_send()

    return _Side(list(ws), [_sds((N_CHIPS,) + w.shape, w.dtype) for w in ws],
                 [_dma_sems(3 * k)] * 4 + [_dma_sems(k)] * 2, start, finish)


def _exchange_side(parts):
    k = len(parts)

    def copies(p_refs, rcv_refs, sems):
        ssem, rsem = sems
        x, y, c = _place()
        return [_rcopy(p_refs[n].at[2 * cx + cy], rcv_refs[n].at[t], ssem.at[3 * n + t], rsem.at[3 * n + t], (cx, cy, c))
                for n in range(k) for t, (cx, cy) in enumerate(_other_chips(x, y))]

    def start(p_refs, rcv_refs, sems):
        for cp in copies(p_refs, rcv_refs, sems):
            cp.start()

    def finish(p_refs, rcv_refs, sems):
        for cp in copies(p_refs, rcv_refs, sems):
            cp.wait()

    return _Side(list(parts), [_sds((3,) + p.shape[1:], p.dtype) for p in parts], [_dma_sems(3 * k)] * 2, start, finish)


def _run_side(side, name):
    k_in, k_out = len(side.arrays), len(side.out_shape)

    def body(*refs):
        parts = refs[:k_in], refs[k_in:k_in + k_out], refs[k_in + k_out:]
        side.start(*parts)
        side.finish(*parts)

    return _comm_call(body, name, list(side.out_shape), k_in, list(side.scratch))(*side.arrays)


def _sibling_side(arrays, out_shape, n_copies, copies):
    def start(in_refs, out_refs, sems):
        for cp in copies(in_refs, out_refs, sems):
            cp.start()

    def finish(in_refs, out_refs, sems):
        for cp in copies(in_refs, out_refs, sems):
            cp.wait()

    return _Side(list(arrays), out_shape, [_dma_sems(n_copies)] * 2, start, finish)


class _ColBlocks:
    def __init__(self, array, off):
        self.array, self.off, self.dtype = array, off, array.dtype
        self.shape = (N_CHIPS, array.shape[0], GU_BLOCK)

    def block(self, ref, jj):
        return ref.at[:, pl.ds((2 * jj + self.off) * GU_BLOCK, GU_BLOCK)]


def _swap_side(gds):
    k = len(gds)

    def copies(gd_refs, out_refs, sems):
        ssem, rsem = sems
        x, y, c = _place()
        blocks = [[g.block(gd_refs[n], jj) if isinstance(g, _ColBlocks) else gd_refs[n].at[jj] for jj in range(N_CHIPS)]
                  for n, g in enumerate(gds)]
        return [_rcopy(_half(blocks[n][jj], 1 - c), out_refs[n].at[jj], ssem.at[N_CHIPS * n + jj],
                       rsem.at[N_CHIPS * n + jj], (x, y, 1 - c)) for n in range(k) for jj in range(N_CHIPS)]

    return _sibling_side([g.array if isinstance(g, _ColBlocks) else g for g in gds],
                         [_sds((N_CHIPS, g.shape[1] // 2, g.shape[2]), g.dtype) for g in gds], N_CHIPS * k, copies)


def _share_side(reds):
    k = len(reds)

    def copies(r_refs, out_refs, sems):
        ssem, rsem = sems
        x, y, c = _place()
        return [_rcopy(r_refs[n], out_refs[n], ssem.at[n], rsem.at[n], (x, y, 1 - c)) for n in range(k)]

    return _sibling_side(reds, [_sds(r.shape, r.dtype) for r in reds], k, copies)


def _join_sides(sides):
    if len(sides) == 1:
        return sides[0]
    cuts = [(len(s.arrays), len(s.out_shape), len(s.scratch)) for s in sides]

    def each(method, in_refs, out_refs, sems):
        a = o = m = 0
        for s, (ka, ko, km) in zip(sides, cuts):
            getattr(s, method)(in_refs[a:a + ka], out_refs[o:o + ko], sems[m:m + km])
            a, o, m = a + ka, o + ko, m + km

    return _Side([x for s in sides for x in s.arrays], [x for s in sides for x in s.out_shape],
                 [x for s in sides for x in s.scratch], functools.partial(each, "start"), functools.partial(each, "finish"))


def _allreduce_small(small):
    def body(s_ref, all_ref, sssem, srsem, lsem):
        x, y, c = _place()
        me = 4 * x + 2 * y + c
        own = pltpu.make_async_copy(s_ref, all_ref.at[me], lsem)
        own.start()
        cps = []
        for r in range(1, 8):
            fx, fy, fc = (r >> 2) & 1, (r >> 1) & 1, r & 1
            px, py, pc = (1 - x if fx else x, 1 - y if fy else y, 1 - c if fc else c)
            peer = 4 * px + 2 * py + pc
            send = _rcopy(s_ref, all_ref.at[me], sssem.at[r - 1], srsem.at[me], (px, py, pc))
            send.start()
            cps.append((send, _rcopy(s_ref, all_ref.at[peer], sssem.at[r - 1], srsem.at[peer], (px, py, pc))))
        for send, recv in cps:
            send.wait_send()
            recv.wait_recv()
        own.wait()

    return _comm_call(body, "allreduce_small", [_sds((8,) + small.shape, small.dtype)], 1,
                      [pltpu.SemaphoreType.DMA((7,)), pltpu.SemaphoreType.DMA((8,)), pltpu.SemaphoreType.DMA(())])(small)[0]


def _add_pair(gd, got, c, name):
    _, R, W = got.shape
    tm = _pick(R, (512, 256, 128, 64))
    nb = R // tm

    def body(c_ref, a_ref, b_ref, o_ref):
        o_ref[...] = (a_ref[...].astype(F32) + b_ref[...].astype(F32)).astype(o_ref.dtype)

    if isinstance(gd, _ColBlocks):
        off = gd.off
        own = pl.BlockSpec((tm, W), lambda j, i, c_ref: (c_ref[0] * nb + i, 2 * j + off))
        gd = gd.array
    else:
        own = pl.BlockSpec((None, tm, W), lambda j, i, c_ref: (j, c_ref[0] * nb + i, 0))
    grid_spec = pltpu.PrefetchScalarGridSpec(
        num_scalar_prefetch=1, grid=(N_CHIPS, nb),
        in_specs=[own, pl.BlockSpec((None, tm, W), lambda j, i, c_ref: (j, i, 0))],
        out_specs=pl.BlockSpec((None, tm, W), lambda j, i, c_ref: (j, i, 0)))
    return pl.pallas_call(body, name=name, grid_spec=grid_spec, out_shape=_sds((N_CHIPS, R, W), gd.dtype),
                          compiler_params=pltpu.CompilerParams(dimension_semantics=("parallel", "parallel"),
                                                               vmem_limit_bytes=VMEM_LIMIT))(c, gd, got)


def _add_chips(part, rcv, j, name):
    _, R, W = part.shape
    tm = _pick(R, (512, 256, 128, 64))

    def body(j_ref, p_ref, r0_ref, r1_ref, r2_ref, o_ref):
        o_ref[...] = ((p_ref[...].astype(F32) + r0_ref[...].astype(F32)) + r1_ref[...].astype(F32)) + r2_ref[...].astype(F32)

    def slot(t):
        return pl.BlockSpec((None, tm, W), lambda i, j_ref: (t, i, 0))

    grid_spec = pltpu.PrefetchScalarGridSpec(
        num_scalar_prefetch=1, grid=(R // tm,),
        in_specs=[pl.BlockSpec((None, tm, W), lambda i, j_ref: (j_ref[0], i, 0)), slot(0), slot(1), slot(2)],
        out_specs=pl.BlockSpec((tm, W), lambda i, j_ref: (i, 0)))
    return pl.pallas_call(body, name=name, grid_spec=grid_spec, out_shape=_sds((R, W), F32),
                          compiler_params=pltpu.CompilerParams(dimension_semantics=("parallel",),
                                                               vmem_limit_bytes=VMEM_LIMIT))(j, part, rcv, rcv, rcv)


def _sum_small(allsmall):
    _, R, W = allsmall.shape

    def body(a_ref, o_ref):
        acc = a_ref[0]
        for d in range(1, 8):
            acc = acc + a_ref[d]
        o_ref[...] = acc

    return _call(body, "sum_small", _sds((R, W), F32), (1,), [_whole((8, R, W))], _whole((R, W)),
                 sem=("arbitrary",))(allsmall)


def _adamw(w, g, m, v, name):
    R, C = w.shape
    tm = _pick(R, (256, 128, 64, 32, 8))

    def body(w_ref, g_ref, m_ref, v_ref, d_ref, mo_ref, vo_ref):
        gv = g_ref[...]
        mn = ADAM_B1 * m_ref[...] + (1.0 - ADAM_B1) * gv
        vn = ADAM_B2 * v_ref[...] + (1.0 - ADAM_B2) * (gv * gv)
        m_hat = mn / (1.0 - ADAM_B1 ** ADAM_STEP)
        v_hat = vn / (1.0 - ADAM_B2 ** ADAM_STEP)
        d_ref[...] = -ADAM_LR * (m_hat / (jnp.sqrt(v_hat) + ADAM_EPS) + ADAM_WD * w_ref[...])
        mo_ref[...] = mn
        vo_ref[...] = vn

    spec = _rows(tm, C)
    return _call(body, name, [_sds((R, C), F32)] * 3, (R // tm,), [spec] * 4, [spec] * 3, sem=("parallel",))(w, g, m, v)


def _adamw_layer(c, w, m, v, mine, other, l, prev, name):
    _, R, C = w.shape
    half = R // 2
    tm = _pick(half, (256, 128, 64))
    nbh = half // tm

    def body(c_ref, w_ref, m_ref, v_ref, a_ref, b_ref, *rest):
        g_ref, d_ref, mo_ref, vo_ref = rest[-4:]
        gv = jnp.where(pl.program_id(0) // nbh == c_ref[0], a_ref[...], b_ref[...])
        mn = ADAM_B1 * m_ref[...] + (1.0 - ADAM_B1) * gv
        vn = ADAM_B2 * v_ref[...] + (1.0 - ADAM_B2) * (gv * gv)
        m_hat = mn / (1.0 - ADAM_B1 ** ADAM_STEP)
        v_hat = vn / (1.0 - ADAM_B2 ** ADAM_STEP)
        g_ref[...] = gv
        d_ref[...] = -ADAM_LR * (m_hat / (jnp.sqrt(v_hat) + ADAM_EPS) + ADAM_WD * w_ref[...])
        mo_ref[...] = mn
        vo_ref[...] = vn

    layer = pl.BlockSpec((None, tm, C), lambda i, c_ref: (l, i, 0))
    halfspec = pl.BlockSpec((tm, C), lambda i, c_ref: (i % nbh, 0))
    n_prev = 0 if prev is None else 4
    grid_spec = pltpu.PrefetchScalarGridSpec(
        num_scalar_prefetch=1, grid=(R // tm,),
        in_specs=[layer] * 3 + [halfspec] * 2 + [pl.BlockSpec(memory_space=pl.ANY)] * n_prev,
        out_specs=[layer] * 4)
    return pl.pallas_call(body, name=name, grid_spec=grid_spec, out_shape=[_sds(w.shape, F32)] * 4,
                          input_output_aliases={6 + k: k for k in range(n_prev)},
                          compiler_params=pltpu.CompilerParams(dimension_semantics=("parallel",),
                                                               vmem_limit_bytes=VMEM_LIMIT))(
        c, w, m, v, mine, other, *(prev or ()))


FIRST_GATHER = ("w_in", "w_uq", "w_ukv")
G_DOWN, G_GU, G_OUT, G_IN = ("w_down",), ("w_gate", "w_up"), ("w_out",), ("w_uq", "w_ukv", "w_in")


def _backward_jobs(l):
    t = f"_l{l}"
    return {"mm_dact" + t: [("swap", l, G_DOWN)], "mm_dw_gu" + t: [("exchange", l, G_DOWN)],
            "mm_dx1" + t: [("swap", l, G_GU), ("share", l, G_DOWN)], "mm_dmixin" + t: [("swap", l, G_OUT)],
            "ret_bwd" + t: [("exchange", l, ("w_gate",))],
            "mla_bwd" + t: [("exchange", l, ("w_up", "w_out")), ("share", l, ("w_gate",))],
            "mm_dw_in" + t: [("share", l, ("w_up", "w_out"))]}


JOBS = {
    "mm_h_l0": [("gather", 0, ("w_up@a",))], "mla_fwd_l0": [("gather", 0, ("w_gate", "w_out"))],
    "ret_fwd_l0": [("gather", 0, ("w_up@b",))],
    "mm_gu_l0": [("gather", 0, ("w_down",)), ("gather", 1, ("w_uq", "w_ukv"))],
    "mm_down_l0": [("gather", 1, ("w_in",))], "mm_h_l1": [("gather", 1, ("w_up@a",))],
    "mla_fwd_l1": [("gather", 1, ("w_gate", "w_out"))], "ret_fwd_l1": [("gather", 1, ("w_up@b",))],
    "mm_gu_l1": [("gather", 1, ("w_down",))],
    **_backward_jobs(1), **_backward_jobs(0),
    "mm_dxl_l1": [("swap", 1, G_IN)],
    "mm_dx1_l0": [("swap", 0, G_GU), ("share", 0, G_DOWN), ("exchange", 1, G_IN)],
    "ret_bwd_l0": [("exchange", 0, ("w_gate",)), ("share", 1, G_IN)], "mm_dxl_l0": [("exchange", 0, G_IN)]}
PLANNED = {job for jobs in JOBS.values() for job in jobs}


class _Pipeline:
    def __init__(self, own, Wt, Mo, Vo, core, chip):
        self.own, self.Wt, self.Mo, self.Vo, self.core, self.chip = own, Wt, Mo, Vo, core, chip
        self.blocks, self.whole, self.gds, self.parts, self.reds = {}, {}, {}, {}, {}
        self.results = {n: None for n in BIG}

    def gather_first(self):
        job = ("gather", 0, FIRST_GATHER)
        self._done(*job, _run_side(self._side(*job), "allgather_first"))

    def _gathered(self, l, n):
        if n in ROW_PIECES:
            return jnp.concatenate([self.blocks[(l, n + "@a")], self.blocks[(l, n + "@b")]], axis=1)
        return self.blocks[(l, n)]

    def weight(self, l, name):
        if (l, name) not in self.whole:
            self.whole[(l, name)] = _internal_weight(name, *[self._gathered(l, n) for n in INTERNAL_OF[name]])
        return self.whole[(l, name)]

    def run(self, fn, name, *args, **kw):
        jobs = JOBS.get(name, ())
        if not jobs:
            return fn(*args, name=name, **kw)
        sides = [self._side(*job) for job in jobs]
        out, res = fn(*args, name=name, side=_join_sides(sides), **kw)
        for job, side in zip(jobs, sides):
            k = len(side.out_shape)
            self._done(*job, res[:k])
            res = res[k:]
        return out

    def reduce(self, l, **grads):
        shards = {}
        for name, g in grads.items():
            shards.update(_grad_shards(name, g))
        for n, sh in shards.items():
            self.gds[(l, n)] = sh if hasattr(sh, "shape") else jnp.stack(sh)
        self._alone("swap", l, tuple(shards))

    def _alone(self, kind, l, names):
        if (kind, l, names) not in PLANNED:
            self._done(kind, l, names, _run_side(self._side(kind, l, names), f"{kind}_{names[0]}_l{l}"))

    def _side(self, kind, l, names):
        if kind == "gather":
            return _allgather_side([self.own[l][n] for n in names])
        store = {"swap": self.gds, "exchange": self.parts, "share": self.reds}[kind]
        make = {"swap": _swap_side, "exchange": _exchange_side, "share": _share_side}[kind]
        return make([store[(l, n)] for n in names])

    def _done(self, kind, l, names, res):
        for n, r in zip(names, res):
            if kind == "gather":
                self.blocks[(l, n)] = r
            elif kind == "swap":
                self.parts[(l, n)] = _add_pair(self.gds[(l, n)], r, self.core, f"add_pair_{n}_l{l}")
            elif kind == "exchange":
                self.reds[(l, n)] = _add_chips(self.parts[(l, n)], r, self.chip, f"add_chips_{n}_l{l}")
            else:
                self.results[n] = _adamw_layer(self.core, self.Wt[n], self.Mo[n], self.Vo[n], self.reds[(l, n)], r, l,
                                               self.results[n], f"adamw_{n}_l{l}")
        if kind == "exchange":
            self._alone("share", l, names)


def kernel(x, positions, ln_in_g, ln_in_b, w_in, q_norm_g, kv_norm_g, w_uq, w_ukv, ret_gn_g, ret_gn_b, w_out, ln1_g, ln1_b, w_gate, w_up, w_down, ln2_g, ln2_b, loss_target, m_ln_in_g, m_ln_in_b, m_w_in, m_q_norm_g, m_kv_norm_g, m_w_uq, m_w_ukv, m_ret_gn_g, m_ret_gn_b, m_w_out, m_ln1_g, m_ln1_b, m_w_gate, m_w_up, m_w_down, m_ln2_g, m_ln2_b, v_ln_in_g, v_ln_in_b, v_w_in, v_q_norm_g, v_kv_norm_g, v_w_uq, v_w_ukv, v_ret_gn_g, v_ret_gn_b, v_w_out, v_ln1_g, v_ln1_b, v_w_gate, v_w_up, v_w_down, v_ln2_g, v_ln2_b):
    given = dict(locals())
    Wt = {n: given[n] for n in WEIGHTS}
    Mo = {n: given["m_" + n] for n in WEIGHTS}
    Vo = {n: given["v_" + n] for n in WEIGHTS}
    cx, cy, cc = _place()
    chip = (2 * cx + cy).astype(jnp.int32)
    core = cc.astype(jnp.int32)

    own = [{n: Wt[n][l].astype(BF16) for n in BIG} for l in range(DEPTH)]
    for shard in own:
        for n, at in ROW_PIECES.items():
            shard[n + "@a"], shard[n + "@b"] = shard[n][:at], shard[n][at:]
    pipe = _Pipeline(own, Wt, Mo, Vo, core.reshape(1), chip.reshape(1))
    sqerr, grad_x, dP = _local_step(x[0], positions[0], loss_target[0], pipe, Wt)
    results = pipe.results

    small_g = {n: (dP[(n, None)] if Wt[n].ndim == 1 else jnp.stack([dP[(n, l)] for l in range(DEPTH)])) for n in SMALL}
    local_loss = 0.5 * jnp.sum(sqerr) / D_MODEL
    small_sum = _sum_small(_allreduce_small(_flatten_small(small_g, local_loss))).reshape(-1)
    layout, n_small = _small_layout(Wt)
    loss = small_sum[n_small]

    grads, deltas, new_m, new_v = {}, {}, {}, {}
    for n in BIG:
        grads[n], deltas[n], new_m[n], new_v[n] = results[n]
    zero = jnp.zeros((), F32)
    d, mn, vn = _adamw(_flatten_small(Wt, zero), small_sum.reshape(SMALL_ROWS, FLAT_W), _flatten_small(Mo, zero),
                       _flatten_small(Vo, zero), "adamw_small")
    for n in SMALL:
        at, size = layout[n]
        pick = lambda a: a.reshape(-1)[at:at + size].reshape(Wt[n].shape)
        grads[n], deltas[n], new_m[n], new_v[n] = pick(small_sum), pick(d), pick(mn), pick(vn)

    return (loss, grad_x[None], *[grads[n] for n in WEIGHTS], *[deltas[n] for n in WEIGHTS],
            *[new_m[n] for n in WEIGHTS], *[new_v[n] for n in WEIGHTS])
```

```python
import functools

import jax
import jax.numpy as jnp
from jax import lax
from jax.experimental import pallas as pl
from jax.experimental.pallas import tpu as pltpu

F32 = jnp.float32
BF16 = jnp.bfloat16

D_MODEL = 2048
DEPTH = 2
CHUNK = 64
MLA_HEADS = 8
Q_LORA = 512
KV_LORA = 256
NOPE = 128
ROPE = 64
VDIM = 128
RET_HEADS = 4
RET_DK = 256
RET_DV = 256
D_FF = 5632
D_IN = 4928
ROPE_THETA = 10000.0
LN_EPS = 1e-5
RMS_EPS = 1e-6
GN_EPS = 1e-5
ALPHA = (2 * DEPTH) ** 0.25
MLA_SCALE = (NOPE + ROPE) ** -0.5
RET_SCALE = RET_DK ** -0.5
ADAM_LR = 0.001
ADAM_B1 = 0.9
ADAM_B2 = 0.999
ADAM_EPS = 1e-08
ADAM_WD = 0.01
ADAM_STEP = 10

LANES = 128
HEAD_PAD = 256
MLA_IN = 1024
MLA_IN_USED = Q_LORA + KV_LORA + ROPE
D_IN_PAD = MLA_IN + 4 * 1024
ATT_BLOCK = 512
NEG = -1e30
VMEM_LIMIT = 56 * 1024 * 1024

N_CHIPS = 4
FLAT_W = 1024
BIG = ("w_in", "w_uq", "w_ukv", "w_out", "w_gate", "w_up", "w_down")
BIG_SHARD = {"w_in": (2048, 1232), "w_uq": (512, 384), "w_ukv": (256, 512), "w_out": (512, 2048),
             "w_gate": (2048, 1408), "w_up": (2048, 1408), "w_down": (1408, 2048)}
SMALL = ("ln_in_g", "ln_in_b", "q_norm_g", "kv_norm_g", "ret_gn_g", "ret_gn_b", "ln1_g", "ln1_b", "ln2_g", "ln2_b")
WEIGHTS = ("ln_in_g", "ln_in_b", "w_in", "q_norm_g", "kv_norm_g", "w_uq", "w_ukv", "ret_gn_g", "ret_gn_b", "w_out",
           "ln1_g", "ln1_b", "w_gate", "w_up", "w_down", "ln2_g", "ln2_b")
SMALL_ROWS = 32

MESH = pl.DeviceIdType.MESH


def _pick(dim, cands):
    for c in cands:
        if dim % c == 0:
            return c
    return dim


HBM = pl.BlockSpec(memory_space=pltpu.HBM)


class _Side:
    def __init__(self, arrays, out_shape, scratch, start, finish):
        self.arrays, self.out_shape, self.scratch, self.start, self.finish = arrays, out_shape, scratch, start, finish


def _call(body, name, out_shape, grid, in_specs, out_specs, scratch=(), sem=None, side=None):
    params = pltpu.CompilerParams(dimension_semantics=sem if side is None else ("arbitrary",) * len(grid),
                                  vmem_limit_bytes=VMEM_LIMIT)
    if side is None:
        return pl.pallas_call(body, name=name, out_shape=out_shape, grid=grid, in_specs=in_specs, out_specs=out_specs,
                              scratch_shapes=list(scratch), compiler_params=params)
    single = not isinstance(out_shape, (list, tuple))
    outs = [out_shape] if single else list(out_shape)
    ospecs = [out_specs] if single else list(out_specs)
    cuts = [len(in_specs), len(side.arrays), len(outs), len(side.out_shape), len(scratch)]
    ends = [sum(cuts[:k + 1]) for k in range(len(cuts))]

    def hosted(*refs):
        ins, s_in, o, s_out, scr = (refs[a:b] for a, b in zip([0] + ends[:-1], ends))
        sems = refs[ends[-1]:]
        ids = [pl.program_id(a) for a in range(len(grid))]
        first = functools.reduce(jnp.logical_and, [i == 0 for i in ids])
        last = functools.reduce(jnp.logical_and, [i == g - 1 for i, g in zip(ids, grid)])

        @pl.when(first)
        def _():
            side.start(s_in, s_out, sems)

        body(*ins, *o, *scr)

        @pl.when(last)
        def _():
            side.finish(s_in, s_out, sems)

    call = pl.pallas_call(hosted, name=name, out_shape=outs + list(side.out_shape), grid=grid,
                          in_specs=list(in_specs) + [HBM] * len(side.arrays),
                          out_specs=ospecs + [HBM] * len(side.out_shape),
                          scratch_shapes=list(scratch) + list(side.scratch), compiler_params=params)

    def run(*args):
        res = call(*args, *side.arrays)
        return (res[0] if single else list(res[:len(outs)])), list(res[len(outs):])

    return run


def _rows(tm, w, col=0):
    return pl.BlockSpec((tm, w), lambda i: (i, col))


def _whole(shape):
    return pl.BlockSpec(shape, lambda i: (0,) * len(shape))


def _sds(shape, dtype):
    return jax.ShapeDtypeStruct(shape, dtype)


def _matmul(a, b, name, ta=False, tb=False, out_dtype=F32, side=None):
    (K, M) = a.shape if ta else a.shape[::-1]
    (N, Kb) = b.shape if tb else b.shape[::-1]
    assert K == Kb, (a.shape, b.shape, ta, tb)
    tm = _pick(M, (1024, 1408, 512, 256, 128))
    tn = _pick(N, (1024, 512, 256, 128))
    tk = _pick(K, (2816, 2560, 2048, 1024, 512, 256))
    nk = K // tk
    dn = (((0 if ta else 1,), (1 if tb else 0,)), ((), ()))

    def body(a_ref, b_ref, o_ref, acc_ref):
        k = pl.program_id(2)
        if nk == 1:
            o_ref[...] = lax.dot_general(a_ref[...].astype(BF16), b_ref[...].astype(BF16), dn,
                                         preferred_element_type=F32).astype(out_dtype)
        else:
            @pl.when(k == 0)
            def _():
                acc_ref[...] = jnp.zeros_like(acc_ref)

            acc_ref[...] += lax.dot_general(a_ref[...].astype(BF16), b_ref[...].astype(BF16), dn,
                                            preferred_element_type=F32)

            @pl.when(k == nk - 1)
            def _():
                o_ref[...] = acc_ref[...].astype(out_dtype)

    a_spec = pl.BlockSpec((tk, tm), lambda i, j, k: (k, i)) if ta else pl.BlockSpec((tm, tk), lambda i, j, k: (i, k))
    b_spec = pl.BlockSpec((tn, tk), lambda i, j, k: (j, k)) if tb else pl.BlockSpec((tk, tn), lambda i, j, k: (k, j))
    return _call(body, name, _sds((M, N), out_dtype), (M // tm, N // tn, nk), [a_spec, b_spec],
                 pl.BlockSpec((tm, tn), lambda i, j, k: (i, j)), scratch=[pltpu.VMEM((tm, tn), F32)],
                 sem=("parallel", "parallel", "arbitrary"), side=side)(a, b)


def _sigmoid(x):
    return 1.0 / (1.0 + jnp.exp(-x))


def _rope_group(r, c, sa, sb):
    return r * c + pltpu.roll(r, 32, 1) * sa + pltpu.roll(r, 96, 1) * sb


def _ln_fwd(xs, coefs, g, b, name, want_z):
    S, D = xs[0].shape
    tm = 512
    n = len(xs)

    def body(*refs):
        x_refs, g_ref, b_ref, outs = refs[:n], refs[n], refs[n + 1], refs[n + 2:]
        z = None
        for cf, r in zip(coefs, x_refs):
            t = r[...] if cf == 1.0 else cf * r[...]
            z = t if z is None else z + t
        mu = jnp.mean(z, axis=-1, keepdims=True)
        zc = z - mu
        var = jnp.mean(zc * zc, axis=-1, keepdims=True)
        y = zc * lax.rsqrt(var + LN_EPS) * g_ref[...] + b_ref[...]
        if want_z:
            outs[0][...] = z
        outs[-2][...] = y
        outs[-1][...] = y.astype(BF16)

    out_shape = [_sds((S, D), F32)] * (2 if want_z else 1) + [_sds((S, D), BF16)]
    return _call(body, name, out_shape, (S // tm,), [_rows(tm, D)] * n + [_whole((1, D))] * 2,
                 [_rows(tm, D)] * len(out_shape), sem=("parallel",))(*xs, g, b)


def _matmul_ln(a, w, x, g, b, name):
    S, K = a.shape
    D = w.shape[1]
    tm = _pick(S, (256, 128))

    def body(a_ref, w_ref, x_ref, g_ref, b_ref, z_ref, y_ref, yb_ref):
        z = ALPHA * x_ref[...] + jnp.dot(a_ref[...], w_ref[...], preferred_element_type=F32)
        mu = jnp.mean(z, axis=-1, keepdims=True)
        zc = z - mu
        var = jnp.mean(zc * zc, axis=-1, keepdims=True)
        y = zc * lax.rsqrt(var + LN_EPS) * g_ref[...] + b_ref[...]
        z_ref[...] = z
        y_ref[...] = y
        yb_ref[...] = y.astype(BF16)

    return _call(body, name, [_sds((S, D), F32), _sds((S, D), F32), _sds((S, D), BF16)], (S // tm,),
                 [_rows(tm, K), _whole((K, D)), _rows(tm, D), _whole((1, D)), _whole((1, D))], [_rows(tm, D)] * 3,
                 sem=("parallel",))(a, w, x, g, b)


def _ln_bwd(dys, coefs, z, g, name):
    S, D = z.shape
    tm = 512
    n = len(dys)

    def body(*refs):
        dy_refs, z_ref, g_ref = refs[:n], refs[n], refs[n + 1]
        dz_ref, dzb_ref, dg_ref, db_ref = refs[n + 2:]
        dy = None
        for cf, r in zip(coefs, dy_refs):
            t = r[...] if cf == 1.0 else cf * r[...]
            dy = t if dy is None else dy + t
        zv = z_ref[...]
        mu = jnp.mean(zv, axis=-1, keepdims=True)
        zc = zv - mu
        var = jnp.mean(zc * zc, axis=-1, keepdims=True)
        rstd = lax.rsqrt(var + LN_EPS)
        xh = zc * rstd
        dyg = dy * g_ref[...]
        dz = rstd * (dyg - jnp.mean(dyg, axis=-1, keepdims=True) - xh * jnp.mean(dyg * xh, axis=-1, keepdims=True))
        dz_ref[...] = dz
        dzb_ref[...] = dz.astype(BF16)

        @pl.when(pl.program_id(0) == 0)
        def _():
            dg_ref[...] = jnp.zeros_like(dg_ref)
            db_ref[...] = jnp.zeros_like(db_ref)

        dg_ref[...] += jnp.sum(dy * xh, axis=0, keepdims=True)
        db_ref[...] += jnp.sum(dy, axis=0, keepdims=True)

    return _call(body, name, [_sds((S, D), F32), _sds((S, D), BF16), _sds((1, D), F32), _sds((1, D), F32)],
                 (S // tm,), [_rows(tm, D)] * (n + 1) + [_whole((1, D))],
                 [_rows(tm, D), _rows(tm, D), _whole((1, D)), _whole((1, D))], sem=("arbitrary",))(*dys, z, g)


def _rms(x, g):
    return x * lax.rsqrt(jnp.mean(x * x, axis=-1, keepdims=True) + RMS_EPS) * g


def _prep1(h, tabs, qg, kvg, name):
    S = h.shape[0]
    tm = 512
    cm, sam, sbm, cr, sr = tabs

    def body(h_ref, cm_ref, sam_ref, sbm_ref, cr_ref, sr_ref, qg_ref, kvg_ref,
             qn_ref, kvn_ref, kr_ref, rq_ref, rk_ref, rv_ref):
        qn_ref[...] = _rms(h_ref[:, 0:Q_LORA], qg_ref[...]).astype(BF16)
        kvn_ref[...] = _rms(h_ref[:, Q_LORA:Q_LORA + KV_LORA], kvg_ref[...]).astype(BF16)
        kr_ref[...] = _rope_group(h_ref[:, 768:896], cm_ref[...], sam_ref[...], sbm_ref[...])
        c, s = cr_ref[...], sr_ref[...]
        for hd in range(RET_HEADS):
            for src, dst, scale in ((MLA_IN, rq_ref, RET_SCALE), (MLA_IN + 1024, rk_ref, None)):
                t1 = h_ref[:, src + hd * 256:src + hd * 256 + 128]
                t2 = h_ref[:, src + hd * 256 + 128:src + hd * 256 + 256]
                o1, o2 = t1 * c - t2 * s, t2 * c + t1 * s
                if scale is not None:
                    o1, o2 = o1 * scale, o2 * scale
                dst[:, hd * 256:hd * 256 + 128] = o1.astype(BF16)
                dst[:, hd * 256 + 128:hd * 256 + 256] = o2.astype(BF16)
        rv_ref[...] = h_ref[:, MLA_IN + 2048:MLA_IN + 3072].astype(BF16)

    t128 = _rows(tm, LANES)
    return _call(body, name,
                 [_sds((S, Q_LORA), BF16), _sds((S, KV_LORA), BF16), _sds((S, LANES), F32),
                  _sds((S, 1024), BF16), _sds((S, 1024), BF16), _sds((S, 1024), BF16)],
                 (S // tm,),
                 [_rows(tm, D_IN_PAD), t128, t128, t128, t128, t128, _whole((1, Q_LORA)), _whole((1, KV_LORA))],
                 [_rows(tm, Q_LORA), _rows(tm, KV_LORA), t128, _rows(tm, 1024), _rows(tm, 1024), _rows(tm, 1024)],
                 sem=("parallel",))(h, cm, sam, sbm, cr, sr, qg, kvg)


def _prep1_bwd(dqn, dkvn, dkr, drq, drk, drv, drg, h, tabs, qg, kvg, name):
    S = h.shape[0]
    tm = 512
    cm, sam, sbm, cr, sr = tabs

    def rms_bwd(x, g, dy):
        r = lax.rsqrt(jnp.mean(x * x, axis=-1, keepdims=True) + RMS_EPS)
        dyg = dy * g
        dx = r * dyg - x * (r * r * r) * jnp.mean(dyg * x, axis=-1, keepdims=True)
        return dx, jnp.sum(dy * x * r, axis=0, keepdims=True)

    def body(dqn_ref, dkvn_ref, dkr_ref, drq_ref, drk_ref, drv_ref, drg_ref, h_ref,
             cm_ref, sam_ref, sbm_ref, cr_ref, sr_ref, qg_ref, kvg_ref, dh_ref, dqg_ref, dkvg_ref):
        dcq, dqg = rms_bwd(h_ref[:, 0:Q_LORA], qg_ref[...], dqn_ref[...])
        dckv, dkvg = rms_bwd(h_ref[:, Q_LORA:Q_LORA + KV_LORA], kvg_ref[...], dkvn_ref[...])
        dh_ref[:, 0:Q_LORA] = dcq.astype(BF16)
        dh_ref[:, Q_LORA:Q_LORA + KV_LORA] = dckv.astype(BF16)
        dh_ref[:, 768:896] = _rope_group(dkr_ref[...], cm_ref[...], -sam_ref[...], -sbm_ref[...]).astype(BF16)
        dh_ref[:, 896:1024] = jnp.zeros((tm, LANES), BF16)
        c, s = cr_ref[...], sr_ref[...]
        for hd in range(RET_HEADS):
            for src, dst, scale in ((drq_ref, MLA_IN, RET_SCALE), (drk_ref, MLA_IN + 1024, None)):
                d1 = src[:, hd * 256:hd * 256 + 128]
                d2 = src[:, hd * 256 + 128:hd * 256 + 256]
                if scale is not None:
                    d1, d2 = d1 * scale, d2 * scale
                dh_ref[:, dst + hd * 256:dst + hd * 256 + 128] = (d1 * c + d2 * s).astype(BF16)
                dh_ref[:, dst + hd * 256 + 128:dst + hd * 256 + 256] = (d2 * c - d1 * s).astype(BF16)
        dh_ref[:, MLA_IN + 2048:MLA_IN + 3072] = drv_ref[...].astype(BF16)
        dh_ref[:, MLA_IN + 3072:MLA_IN + 4096] = drg_ref[...].astype(BF16)

        @pl.when(pl.program_id(0) == 0)
        def _():
            dqg_ref[...] = jnp.zeros_like(dqg_ref)
            dkvg_ref[...] = jnp.zeros_like(dkvg_ref)

        dqg_ref[...] += dqg
        dkvg_ref[...] += dkvg

    t128 = _rows(tm, LANES)
    return _call(body, name,
                 [_sds((S, D_IN_PAD), BF16), _sds((1, Q_LORA), F32), _sds((1, KV_LORA), F32)],
                 (S // tm,),
                 [_rows(tm, Q_LORA), _rows(tm, KV_LORA), t128, _rows(tm, 1024), _rows(tm, 1024), _rows(tm, 1024),
                  _rows(tm, 1024), _rows(tm, MLA_IN), t128, t128, t128, t128, t128,
                  _whole((1, Q_LORA)), _whole((1, KV_LORA))],
                 [_rows(tm, D_IN_PAD), _whole((1, Q_LORA)), _whole((1, KV_LORA))],
                 sem=("arbitrary",))(dqn, dkvn, dkr, drq, drk, drv, drg, h, cm, sam, sbm, cr, sr, qg, kvg)


def _prep2(q, kv, kr, tabs, name):
    S = q.shape[0]
    tm = 512
    cm, sam, sbm = tabs[:3]

    def body(q_ref, kv_ref, kr_ref, cm_ref, sam_ref, sbm_ref, qo_ref, ko_ref, vo_ref):
        c, sa, sb = cm_ref[...], sam_ref[...], sbm_ref[...]
        krb = kr_ref[...].astype(BF16)
        ones = jnp.ones((tm, LANES), BF16)
        for hd in range(MLA_HEADS):
            o = hd * HEAD_PAD
            qo_ref[:, o:o + 128] = (q_ref[:, o:o + 128] * MLA_SCALE).astype(BF16)
            qo_ref[:, o + 128:o + 256] = (_rope_group(q_ref[:, o + 128:o + 256], c, sa, sb) * MLA_SCALE).astype(BF16)
            ko_ref[:, o:o + 128] = kv_ref[:, hd * 128:hd * 128 + 128].astype(BF16)
            ko_ref[:, o + 128:o + 256] = krb
            vo_ref[:, o:o + 128] = kv_ref[:, 1024 + hd * 128:1024 + hd * 128 + 128].astype(BF16)
            vo_ref[:, o + 128:o + 256] = ones

    t128 = _rows(tm, LANES)
    return _call(body, name, [_sds((S, 2048), BF16)] * 3, (S // tm,),
                 [_rows(tm, 2048), _rows(tm, 2048), t128, t128, t128, t128],
                 [_rows(tm, 2048)] * 3, sem=("parallel",))(q, kv, kr, cm, sam, sbm)


def _prep2_bwd(dqm, dkm, dvm, tabs, name):
    S = dqm.shape[0]
    tm = 512
    cm, sam, sbm = tabs[:3]

    def body(dq_ref, dk_ref, dv_ref, cm_ref, sam_ref, sbm_ref, dqo_ref, dkvo_ref, dkr_ref):
        c, sa, sb = cm_ref[...], -sam_ref[...], -sbm_ref[...]
        dkr = None
        for hd in range(MLA_HEADS):
            o = hd * HEAD_PAD
            dqo_ref[:, o:o + 128] = (dq_ref[:, o:o + 128] * MLA_SCALE).astype(BF16)
            dqo_ref[:, o + 128:o + 256] = (_rope_group(dq_ref[:, o + 128:o + 256], c, sa, sb) * MLA_SCALE).astype(BF16)
            dkvo_ref[:, hd * 128:hd * 128 + 128] = dk_ref[:, o:o + 128].astype(BF16)
            t = dk_ref[:, o + 128:o + 256]
            dkr = t if dkr is None else dkr + t
        dkvo_ref[:, 1024:2048] = dv_ref[...].astype(BF16)
        dkr_ref[...] = dkr

    t128 = _rows(tm, LANES)
    return _call(body, name, [_sds((S, 2048), BF16), _sds((S, 2048), BF16), _sds((S, LANES), F32)], (S // tm,),
                 [_rows(tm, 2048), _rows(tm, 2048), _rows(tm, 1024), t128, t128, t128],
                 [_rows(tm, 2048), _rows(tm, 2048), t128], sem=("parallel",))(dqm, dkm, dvm, cm, sam, sbm)


def _gn_gate(a, o, h, gg, gb, name):
    S = a.shape[0]
    tm = 512

    def body(a_ref, o_ref, rg_ref, gg_ref, gb_ref, mix_ref):
        mix_ref[:, 0:1024] = a_ref[...].astype(BF16)
        for hd in range(RET_HEADS):
            sl = slice(hd * 256, hd * 256 + 256)
            ov = o_ref[:, sl]
            mu = jnp.mean(ov, axis=-1, keepdims=True)
            oc = ov - mu
            var = jnp.mean(oc * oc, axis=-1, keepdims=True)
            y = oc * lax.rsqrt(var + GN_EPS) * gg_ref[:, sl] + gb_ref[:, sl]
            rg = rg_ref[:, sl]
            mix_ref[:, 1024 + hd * 256:1024 + hd * 256 + 256] = (rg * _sigmoid(rg) * y).astype(BF16)

    return _call(body, name, _sds((S, 2048), BF16), (S // tm,),
                 [_rows(tm, 1024), _rows(tm, 1024), _rows(tm, 1024, 4), _whole((1, 1024)), _whole((1, 1024))],
                 _rows(tm, 2048), sem=("parallel",))(a, o, h, gg, gb)


def _gn_gate_bwd(dmixin, o, h, gg, gb, name):
    S = o.shape[0]
    tm = 512

    def body(dr_ref, o_ref, rg_ref, gg_ref, gb_ref, do_ref, drg_ref, dgg_ref, dgb_ref):
        @pl.when(pl.program_id(0) == 0)
        def _():
            dgg_ref[...] = jnp.zeros_like(dgg_ref)
            dgb_ref[...] = jnp.zeros_like(dgb_ref)

        for hd in range(RET_HEADS):
            sl = slice(hd * 256, hd * 256 + 256)
            ov = o_ref[:, sl]
            mu = jnp.mean(ov, axis=-1, keepdims=True)
            oc = ov - mu
            var = jnp.mean(oc * oc, axis=-1, keepdims=True)
            rstd = lax.rsqrt(var + GN_EPS)
            xh = oc * rstd
            g = gg_ref[:, sl]
            y = xh * g + gb_ref[:, sl]
            rg = rg_ref[:, sl]
            sg = _sigmoid(rg)
            dr = dr_ref[:, sl]
            dy = dr * (rg * sg)
            drg_ref[:, sl] = dr * y * (sg * (1.0 + rg * (1.0 - sg)))
            dgg_ref[:, sl] += jnp.sum(dy * xh, axis=0, keepdims=True)
            dgb_ref[:, sl] += jnp.sum(dy, axis=0, keepdims=True)
            dxh = dy * g
            do = rstd * (dxh - jnp.mean(dxh, axis=-1, keepdims=True) - xh * jnp.mean(dxh * xh, axis=-1, keepdims=True))
            do_ref[:, sl] = do.astype(BF16)

    return _call(body, name,
                 [_sds((S, 1024), BF16), _sds((S, 1024), F32), _sds((1, 1024), F32), _sds((1, 1024), F32)],
                 (S // tm,),
                 [_rows(tm, 1024, 1), _rows(tm, 1024), _rows(tm, 1024, 4), _whole((1, 1024)), _whole((1, 1024))],
                 [_rows(tm, 1024), _rows(tm, 1024), _whole((1, 1024)), _whole((1, 1024))],
                 sem=("arbitrary",))(dmixin, o, h, gg, gb)


GU_BLOCK = D_FF // N_CHIPS


def _matmul_swiglu(x, w_gu, name, side=None):
    S, K = x.shape
    tm = _pick(S, (512, 256, 128))
    tn = 2 * GU_BLOCK

    def body(x_ref, w_ref, gu_ref, act_ref):
        r = jnp.dot(x_ref[...], w_ref[...], preferred_element_type=F32)
        g, u = r[:, :GU_BLOCK], r[:, GU_BLOCK:]
        gu_ref[...] = r.astype(BF16)
        act_ref[...] = (g * _sigmoid(g) * u).astype(BF16)

    return _call(body, name, [_sds((S, 2 * D_FF), BF16), _sds((S, D_FF), BF16)], (N_CHIPS, S // tm),
                 [pl.BlockSpec((tm, K), lambda j, i: (i, 0)), pl.BlockSpec((K, tn), lambda j, i: (0, j))],
                 [pl.BlockSpec((tm, tn), lambda j, i: (i, j)), pl.BlockSpec((tm, GU_BLOCK), lambda j, i: (i, j))],
                 sem=("parallel", "parallel"), side=side)(x, w_gu)


def _matmul_swiglu_bwd(df, w_down, gu, name, side=None):
    S, K = df.shape
    tm = _pick(S, (512, 256, 128))

    def body(df_ref, w_ref, gu_ref, o_ref):
        d = _dot_nt(df_ref[...], w_ref[...])
        g = gu_ref[:, :GU_BLOCK].astype(F32)
        u = gu_ref[:, GU_BLOCK:].astype(F32)
        sg = _sigmoid(g)
        o_ref[:, :GU_BLOCK] = (d * u * (sg * (1.0 + g * (1.0 - sg)))).astype(BF16)
        o_ref[:, GU_BLOCK:] = (d * (g * sg)).astype(BF16)

    gu_spec = pl.BlockSpec((tm, 2 * GU_BLOCK), lambda j, i: (i, j))
    return _call(body, name, _sds((S, 2 * D_FF), BF16), (N_CHIPS, S // tm),
                 [pl.BlockSpec((tm, K), lambda j, i: (i, 0)), pl.BlockSpec((GU_BLOCK, K), lambda j, i: (j, 0)), gu_spec],
                 gu_spec, sem=("parallel", "parallel"), side=side)(df, w_down, gu)


def _ln_loss(x, f, g, b, target, name):
    S, D = x.shape
    tm = 256

    def body(x_ref, f_ref, g_ref, b_ref, t_ref, z_ref, dy_ref, acc_ref):
        z = ALPHA * x_ref[...] + f_ref[...]
        mu = jnp.mean(z, axis=-1, keepdims=True)
        zc = z - mu
        var = jnp.mean(zc * zc, axis=-1, keepdims=True)
        e = zc * lax.rsqrt(var + LN_EPS) * g_ref[...] + b_ref[...] - t_ref[...]
        z_ref[...] = z
        dy_ref[...] = e / D

        @pl.when(pl.program_id(0) == 0)
        def _():
            acc_ref[...] = jnp.zeros_like(acc_ref)

        acc_ref[...] += jnp.sum(e * e, axis=0, keepdims=True)

    return _call(body, name, [_sds((S, D), F32), _sds((S, D), F32), _sds((1, D), F32)], (S // tm,),
                 [_rows(tm, D), _rows(tm, D), _whole((1, D)), _whole((1, D)), _rows(tm, D)],
                 [_rows(tm, D), _rows(tm, D), _whole((1, D))], sem=("arbitrary",))(x, f, g, b, target)


def _chunk_mask(T):
    r = lax.shift_right_logical(lax.broadcasted_iota(jnp.int32, (T, T), 0), 6)
    c = lax.shift_right_logical(lax.broadcasted_iota(jnp.int32, (T, T), 1), 6)
    return r >= c


def _dot_nt(a, b):
    return lax.dot_general(a, b, (((1,), (1,)), ((), ())), preferred_element_type=F32)


def _dot_tn(a, b):
    return lax.dot_general(a, b, (((0,), (0,)), ((), ())), preferred_element_type=F32)


def _decay_tables(T):
    lg = jnp.log1p(-jnp.exp2(-5.0 - jnp.arange(RET_HEADS, dtype=F32)))
    idx = jnp.arange(T, dtype=F32)
    diff = idx[:, None] - idx[None, :]
    rel = jnp.exp(lg[:, None, None] * diff[None])
    cid = jnp.arange(T) // CHUNK
    mask = (cid[:, None] >= cid[None, :]).astype(F32)
    reld = jnp.exp(lg[:, None, None] * jnp.abs(diff)[None]) * mask[None]
    lgrow = jnp.broadcast_to(lg[:, None, None], (RET_HEADS, 1, LANES))
    return lgrow, rel, reld


def _attn_fwd(q, k, v, heads, dk, dv, softmax, name, tables=None, side=None):
    S = q.shape[0]
    T = ATT_BLOCK
    nq = S // T
    rep = T // LANES
    vw = 2 * dv if softmax else dv
    assert not softmax or dv == LANES

    def body(*refs):
        if softmax:
            q_ref, k_ref, v_ref, o_ref, lse_ref, m_sc, acc_sc = refs
        else:
            q_ref, k_ref, v_ref, lg_ref, rel_ref, reld_ref, o_ref, acc_sc = refs
        i = pl.program_id(1)
        qv = q_ref[...]

        def kv_block(j):
            rows = pl.ds(pl.multiple_of(j * T, T), T)
            return k_ref[rows, :], v_ref[rows, :]

        kb, vb = kv_block(i)
        s = _dot_nt(qv, kb)
        if softmax:
            s = jnp.where(_chunk_mask(T), s, NEG)
            m = jnp.max(s, axis=-1, keepdims=True)
            p = jnp.exp(s - m)
            m_sc[...] = jnp.broadcast_to(m, (T, LANES))
        else:
            p = s * reld_ref[0]
        acc_sc[...] = jnp.dot(p.astype(BF16), vb, preferred_element_type=F32)

        def scores(j):
            kb, vb = kv_block(j)
            return _dot_nt(qv, kb), vb

        def update(j, s, vb):
            if softmax:
                m_prev = m_sc[...]
                m_next = jnp.maximum(m_prev, jnp.max(s, axis=-1, keepdims=True))
                alpha = jnp.exp(m_prev - m_next)
                p = jnp.exp(s - jnp.tile(m_next, (1, rep)))
                m_sc[...] = m_next
                acc_sc[...] = acc_sc[...] * jnp.tile(alpha, (1, vw // LANES)) + jnp.dot(
                    p.astype(BF16), vb, preferred_element_type=F32)
            else:
                fac = jnp.exp(lg_ref[0] * ((i - j) * T).astype(F32))
                p = s * (rel_ref[0] * jnp.tile(fac, (1, rep)))
                acc_sc[...] += jnp.dot(p.astype(BF16), vb, preferred_element_type=F32)

        def pair(jj, carry):
            first, second = scores(2 * jj), scores(2 * jj + 1)
            update(2 * jj, *first)
            update(2 * jj + 1, *second)
            return carry

        lax.fori_loop(0, i // 2, pair, 0)

        @pl.when(i % 2 == 1)
        def _():
            update(i - 1, *scores(i - 1))

        if softmax:
            l = acc_sc[:, dv:]
            o_ref[...] = acc_sc[:, :dv] / l
            lse_ref[...] = m_sc[...] + jnp.log(l)
        else:
            o_ref[...] = acc_sc[...]

    in_specs = [pl.BlockSpec((T, dk), lambda h, i: (i, h)), pl.BlockSpec((S, dk), lambda h, i: (0, h)),
                pl.BlockSpec((S, vw), lambda h, i: (0, h))]
    o_spec = pl.BlockSpec((T, dv), lambda h, i: (i, h))
    if softmax:
        return _call(body, name, [_sds((S, heads * dv), F32), _sds((S, heads * LANES), F32)], (heads, nq), in_specs,
                     [o_spec, pl.BlockSpec((T, LANES), lambda h, i: (i, h))],
                     scratch=[pltpu.VMEM((T, LANES), F32), pltpu.VMEM((T, vw), F32)],
                     sem=("parallel", "arbitrary"), side=side)(q, k, v)
    lgrow, rel, reld = tables
    in_specs += [pl.BlockSpec((1, 1, LANES), lambda h, i: (h, 0, 0)), pl.BlockSpec((1, T, T), lambda h, i: (h, 0, 0)),
                 pl.BlockSpec((1, T, T), lambda h, i: (h, 0, 0))]
    return _call(body, name, _sds((S, heads * dv), F32), (heads, nq), in_specs, o_spec,
                 scratch=[pltpu.VMEM((T, dv), F32)], sem=("parallel", "arbitrary"), side=side)(q, k, v, lgrow, rel, reld)


def _attn_bwd(q, k, v, do, heads, dk, dv, softmax, name, o=None, lse=None, tables=None, side=None):
    S = q.shape[0]
    T = ATT_BLOCK
    nq = S // T
    rep = T // LANES

    def body(*refs):
        if softmax:
            q_ref, k_ref, v_ref, do_ref, o_ref, lse_ref, dq_ref, dk_ref, dv_ref, dq_sc = refs
        else:
            q_ref, k_ref, v_ref, do_ref, lg_ref, rel_ref, reld_ref, dq_ref, dk_ref, dv_ref, dq_sc = refs
        i = pl.program_id(1)

        @pl.when(i == 0)
        def _():
            dk_ref[...] = jnp.zeros_like(dk_ref)
            dv_ref[...] = jnp.zeros_like(dv_ref)

        qv = q_ref[...]
        dof = do_ref[...].astype(F32)
        dov = dof.astype(BF16)
        if softmax:
            delta = jnp.sum(dof * o_ref[...], axis=-1, keepdims=True)
            lse_t = jnp.tile(lse_ref[...], (1, rep))
        dq_sc[...] = jnp.zeros_like(dq_sc)

        def products(j):
            rows = pl.ds(pl.multiple_of(j * T, T), T)
            kb = k_ref[rows, :]
            return rows, kb, _dot_nt(qv, kb), _dot_nt(dov, v_ref[rows, :])

        def block(j, diagonal, rows, kb, s, dp):
            if softmax:
                if diagonal:
                    s = jnp.where(_chunk_mask(T), s, NEG)
                p = jnp.exp(s - lse_t)
                ds = p * (dp - delta)
            else:
                if diagonal:
                    dec = reld_ref[0]
                else:
                    fac = jnp.exp(lg_ref[0] * ((i - j) * T).astype(F32))
                    dec = rel_ref[0] * jnp.tile(fac, (1, rep))
                p = s * dec
                ds = dp * dec
            dsb = ds.astype(BF16)
            dv_ref[rows, :] += _dot_tn(p.astype(BF16), dov)
            dk_ref[rows, :] += _dot_tn(dsb, qv)
            dq_sc[...] += jnp.dot(dsb, kb, preferred_element_type=F32)

        block(i, True, *products(i))

        def pair(jj, carry):
            first, second = products(2 * jj), products(2 * jj + 1)
            block(2 * jj, False, *first)
            block(2 * jj + 1, False, *second)
            return carry

        lax.fori_loop(0, i // 2, pair, 0)

        @pl.when(i % 2 == 1)
        def _():
            block(i - 1, False, *products(i - 1))

        dq_ref[...] = dq_sc[...]

    qspec = pl.BlockSpec((T, dk), lambda h, i: (i, h))
    kspec = pl.BlockSpec((S, dk), lambda h, i: (0, h))
    vspec = pl.BlockSpec((S, dv), lambda h, i: (0, h))
    dospec = pl.BlockSpec((T, dv), lambda h, i: (i, h))
    in_specs = [qspec, kspec, vspec, dospec]
    args = [q, k, v, do]
    if softmax:
        in_specs[2] = pl.BlockSpec((S, dv), lambda h, i: (0, 2 * h))
        in_specs += [dospec, pl.BlockSpec((T, LANES), lambda h, i: (i, h))]
        args += [o, lse]
    else:
        in_specs += [pl.BlockSpec((1, 1, LANES), lambda h, i: (h, 0, 0)),
                     pl.BlockSpec((1, T, T), lambda h, i: (h, 0, 0)), pl.BlockSpec((1, T, T), lambda h, i: (h, 0, 0))]
        args += list(tables)
    return _call(body, name, [_sds((S, heads * dk), F32), _sds((S, heads * dk), F32), _sds((S, heads * dv), F32)],
                 (heads, nq), in_specs, [qspec, kspec, vspec], scratch=[pltpu.VMEM((T, dk), F32)],
                 sem=("parallel", "arbitrary"), side=side)(*args)


def _rope_tables(pos):
    def tables(dim):
        inv_freq = ROPE_THETA ** (-jnp.arange(0, dim, 2, dtype=F32) / dim)
        ang = pos.astype(F32)[:, None] * inv_freq
        return jnp.cos(ang), jnp.sin(ang)

    cm, sm = tables(ROPE)
    S = pos.shape[0]
    z32, z64 = jnp.zeros((S, 32), F32), jnp.zeros((S, 64), F32)
    cr, sr = tables(RET_DK)
    return (jnp.concatenate([cm, cm, z64], 1), jnp.concatenate([z32, sm, z64], 1),
            jnp.concatenate([-sm, z32, z64], 1), cr, sr)


def _row(v):
    return v.reshape(1, -1).astype(F32)


def _local_step(x, pos, target, pipe, P):
    tabs = _rope_tables(pos)
    dtabs = _decay_tables(ATT_BLOCK)
    xf, xb = _ln_fwd([x], [1.0], _row(P["ln_in_g"]), _row(P["ln_in_b"]), "ln_in", False)
    pipe.gather_first()
    saved = []
    for l in range(DEPTH):
        w = functools.partial(pipe.weight, l)
        t = f"_l{l}"
        h = pipe.run(_matmul, "mm_h" + t, xb, w("w_in"))
        qn, kvn, kr, rq, rk, rv = _prep1(h, tabs, _row(P["q_norm_g"][l]), _row(P["kv_norm_g"][l]), "prep1" + t)
        q = _matmul(qn, w("w_uq"), "mm_q" + t)
        kv = _matmul(kvn, w("w_ukv"), "mm_kv" + t)
        qm, km, vm = _prep2(q, kv, kr, tabs, "prep2" + t)
        a, lse = pipe.run(_attn_fwd, "mla_fwd" + t, qm, km, vm, MLA_HEADS, HEAD_PAD, VDIM, True)
        o = pipe.run(_attn_fwd, "ret_fwd" + t, rq, rk, rv, RET_HEADS, RET_DK, RET_DV, False, tables=dtabs)
        mixin = _gn_gate(a, o, h, _row(P["ret_gn_g"][l]), _row(P["ret_gn_b"][l]), "gn_gate" + t)
        z1, x1f, x1b = _matmul_ln(mixin, w("w_out"), xf, _row(P["ln1_g"][l]), _row(P["ln1_b"][l]), "mm_mix_ln1" + t)
        gu, act = pipe.run(_matmul_swiglu, "mm_gu" + t, x1b, w("w_gu"))
        f = pipe.run(_matmul, "mm_down" + t, act, w("w_down"))
        g2, b2 = _row(P["ln2_g"][l]), _row(P["ln2_b"][l])
        saved.append(dict(xb=xb, h=h, qn=qn, kvn=kvn, rq=rq, rk=rk, rv=rv, qm=qm, km=km, vm=vm, a=a, lse=lse, o=o,
                          mixin=mixin, z1=z1, x1b=x1b, gu=gu, act=act))
        if l == DEPTH - 1:
            saved[l]["z2"], dy, sqerr = _ln_loss(x1f, f, g2, b2, target, "ln2_loss" + t)
        else:
            saved[l]["z2"], xf, xb = _ln_fwd([x1f, f], [ALPHA, 1.0], g2, b2, "ln2" + t, True)

    dP = {}
    dys, coefs = [dy], [1.0]
    for l in reversed(range(DEPTH)):
        w, sv = functools.partial(pipe.weight, l), saved[l]
        t = f"_l{l}"
        dz2, dz2b, dg, db = _ln_bwd(dys, coefs, sv["z2"], _row(P["ln2_g"][l]), "ln2_bwd" + t)
        dP[("ln2_g", l)], dP[("ln2_b", l)] = dg, db
        pipe.reduce(l, w_down=pipe.run(_matmul, "mm_dw_down" + t, sv["act"], dz2b, ta=True, out_dtype=BF16))
        dgu = pipe.run(_matmul_swiglu_bwd, "mm_dact" + t, dz2b, w("w_down"), sv["gu"])
        pipe.reduce(l, w_gu=pipe.run(_matmul, "mm_dw_gu" + t, sv["x1b"], dgu, ta=True, out_dtype=BF16))
        dx1 = pipe.run(_matmul, "mm_dx1" + t, dgu, w("w_gu"), tb=True)
        dz1, dz1b, dg, db = _ln_bwd([dz2, dx1], [ALPHA, 1.0], sv["z1"], _row(P["ln1_g"][l]), "ln1_bwd" + t)
        dP[("ln1_g", l)], dP[("ln1_b", l)] = dg, db
        pipe.reduce(l, w_out=_matmul(sv["mixin"], dz1b, "mm_dw_out" + t, ta=True, out_dtype=BF16))
        dmixin = pipe.run(_matmul, "mm_dmixin" + t, dz1b, w("w_out"), tb=True)
        do, drg, dgg, dgb = _gn_gate_bwd(dmixin, sv["o"], sv["h"], _row(P["ret_gn_g"][l]), _row(P["ret_gn_b"][l]),
                                         "gn_gate_bwd" + t)
        dP[("ret_gn_g", l)], dP[("ret_gn_b", l)] = dgg, dgb
        drq, drk, drv = pipe.run(_attn_bwd, "ret_bwd" + t, sv["rq"], sv["rk"], sv["rv"], do, RET_HEADS, RET_DK, RET_DV,
                                 False, tables=dtabs)
        dqm, dkm, dvm = pipe.run(_attn_bwd, "mla_bwd" + t, sv["qm"], sv["km"], sv["vm"], dmixin, MLA_HEADS, HEAD_PAD,
                                 VDIM, True, o=sv["a"], lse=sv["lse"])
        dq, dkv, dkr = _prep2_bwd(dqm, dkm, dvm, tabs, "prep2_bwd" + t)
        g_uq = _matmul(sv["qn"], dq, "mm_dw_uq" + t, ta=True, out_dtype=BF16)
        dqn = _matmul(dq, w("w_uq"), "mm_dqn" + t, tb=True)
        g_ukv = _matmul(sv["kvn"], dkv, "mm_dw_ukv" + t, ta=True, out_dtype=BF16)
        dkvn = _matmul(dkv, w("w_ukv"), "mm_dkvn" + t, tb=True)
        dh, dqg, dkvg = _prep1_bwd(dqn, dkvn, dkr, drq, drk, drv, drg, sv["h"], tabs, _row(P["q_norm_g"][l]),
                                   _row(P["kv_norm_g"][l]), "prep1_bwd" + t)
        dP[("q_norm_g", l)], dP[("kv_norm_g", l)] = dqg, dkvg
        pipe.reduce(l, w_uq=g_uq, w_ukv=g_ukv,
                    w_in=pipe.run(_matmul, "mm_dw_in" + t, sv["xb"], dh, ta=True, out_dtype=BF16))
        dxl = pipe.run(_matmul, "mm_dxl" + t, dh, w("w_in"), tb=True)
        dys, coefs = [dz1, dxl], [ALPHA, 1.0]
    grad_x, _, dg, db = _ln_bwd(dys, coefs, x, _row(P["ln_in_g"]), "ln_in_bwd")
    dP[("ln_in_g", None)], dP[("ln_in_b", None)] = dg, db
    return sqerr, grad_x, dP


INTERNAL_OF = {"w_in": ("w_in",), "w_uq": ("w_uq",), "w_ukv": ("w_ukv",), "w_out": ("w_out",),
               "w_gu": ("w_gate", "w_up"), "w_down": ("w_down",)}
ROW_PIECES = {"w_up": 1024}


def _internal_weight(name, *blocks):
    cat = lambda parts: jnp.concatenate(parts, axis=1)
    cols = lambda b: cat([b[j] for j in range(N_CHIPS)])
    b = blocks[0]
    if name in ("w_out", "w_down"):
        return b.reshape(-1, b.shape[-1])
    if name == "w_gu":
        return cat([blk[j] for j in range(N_CHIPS) for blk in blocks])
    if name == "w_in":
        return cat([b[0][:, :MLA_IN_USED], jnp.zeros((D_MODEL, MLA_IN - MLA_IN_USED), BF16), b[0][:, MLA_IN_USED:]]
                   + [b[j] for j in range(1, N_CHIPS)])
    if name == "w_uq":
        uq, hw = cols(b), NOPE + ROPE
        pad = jnp.zeros((Q_LORA, HEAD_PAD - hw), BF16)
        return cat([p for h in range(MLA_HEADS) for p in (uq[:, h * hw:(h + 1) * hw], pad)])
    ukv = cols(b)
    return cat([ukv[:, 256 * h:256 * h + NOPE] for h in range(MLA_HEADS)]
               + [ukv[:, 256 * h + NOPE:256 * h + 256] for h in range(MLA_HEADS)])


def _grad_shards(name, g):
    cat = lambda parts: jnp.concatenate(parts, axis=1)
    if name in ("w_out", "w_down"):
        return {name: g.reshape(N_CHIPS, -1, g.shape[-1])}
    if name == "w_gu":
        return {"w_gate": _ColBlocks(g, 0), "w_up": _ColBlocks(g, 1)}
    if name == "w_in":
        ci, shift = BIG_SHARD["w_in"][1], MLA_IN - MLA_IN_USED
        return {name: [cat([g[:, :MLA_IN_USED], g[:, MLA_IN:ci + shift]])]
                + [g[:, ci * j + shift:ci * (j + 1) + shift] for j in range(1, N_CHIPS)]}
    if name == "w_uq":
        cq = NOPE + ROPE
        return {name: [cat([g[:, HEAD_PAD * h:HEAD_PAD * h + cq] for h in (2 * j, 2 * j + 1)]) for j in range(N_CHIPS)]}
    return {name: [cat([g[:, o + NOPE * h:o + NOPE * (h + 1)] for h in (2 * j, 2 * j + 1) for o in (0, MLA_HEADS * NOPE)])
                   for j in range(N_CHIPS)]}


def _small_layout(P):
    out, at = {}, 0
    for n in SMALL:
        out[n] = (at, P[n].size)
        at += P[n].size
    return out, at


def _flatten_small(P, last):
    v = jnp.concatenate([P[n].reshape(-1).astype(F32) for n in SMALL] + [last.reshape(-1).astype(F32)])
    return jnp.pad(v, (0, SMALL_ROWS * FLAT_W - v.size)).reshape(SMALL_ROWS, FLAT_W)


def _place():
    return lax.axis_index("x"), lax.axis_index("y"), lax.axis_index("c")


def _other_chips(x, y):
    return [(1 - x, y), (x, 1 - y), (1 - x, 1 - y)]


def _rcopy(src, dst, ssem, rsem, dev):
    return pltpu.make_async_remote_copy(src_ref=src, dst_ref=dst, send_sem=ssem, recv_sem=rsem, device_id=dev,
                                        device_id_type=MESH)


def _comm_call(body, name, out_shape, n_in, scratch):
    many = isinstance(out_shape, (list, tuple))
    return pl.pallas_call(body, name=name, out_shape=out_shape, in_specs=[HBM] * n_in,
                          out_specs=[HBM] * len(out_shape) if many else HBM, scratch_shapes=scratch)


def _half(ref, which):
    rows = ref.shape[0] // 2
    return ref.at[pl.ds(pl.multiple_of(which * rows, 16), rows)]


def _dma_sems(n):
    return pltpu.SemaphoreType.DMA((n,))


def _allgather_side(ws):
    k = len(ws)

    def peers():
        x, y, c = _place()
        return c, 2 * x + y, (x, y, 1 - c), [(n, t, cx, cy) for n in range(k) for t, (cx, cy) in enumerate(_other_chips(x, y))]

    def outgoing(w_refs, g_refs, sems):
        ssem, rsem, _, _, ossem, orsem = sems
        c, j, sib, nt = peers()
        owns = [_rcopy(w_refs[n], g_refs[n].at[j], ossem.at[n], orsem.at[n], sib) for n in range(k)]
        sends = [_rcopy(_half(w_refs[n], c), _half(g_refs[n].at[j], c), ssem.at[3 * n + t], rsem.at[3 * n + t],
                        (cx, cy, c)) for n, t, cx, cy in nt]
        return owns, sends

    def incoming(g_refs, sems):
        ssem, rsem, fssem, frsem, _, _ = sems
        c, _, sib, nt = peers()
        landed, passed, relayed = [], [], []
        for n, t, cx, cy in nt:
            mine, other = (_half(g_refs[n].at[2 * cx + cy], h) for h in (c, 1 - c))
            landed.append(_rcopy(mine, mine, ssem.at[3 * n + t], rsem.at[3 * n + t], (cx, cy, c)))
            passed.append(_rcopy(mine, mine, fssem.at[3 * n + t], frsem.at[3 * n + t], sib))
            relayed.append(_rcopy(other, other, fssem.at[3 * n + t], frsem.at[3 * n + t], sib))
        return landed, passed, relayed

    def start(w_refs, g_refs, sems):
        owns, sends = outgoing(w_refs, g_refs, sems)
        for cp in sends + owns:
            cp.start()

    def finish(w_refs, g_refs, sems):
        owns, sends = outgoing(w_refs, g_refs, sems)
        landed, passed, relayed = incoming(g_refs, sems)
        for got, on in zip(landed, passed):
            got.wait_recv()
            on.start()
        for cp in relayed:
            cp.wait_recv()
        for cp in owns:
            cp.wait()
        for cp in sends + passed:
            cp.wait_send()

    return _Side(list(ws), [_sds((N_CHIPS,) + w.shape, w.dtype) for w in ws],
                 [_dma_sems(3 * k)] * 4 + [_dma_sems(k)] * 2, start, finish)


def _exchange_side(parts):
    k = len(parts)

    def copies(p_refs, rcv_refs, sems):
        ssem, rsem = sems
        x, y, c = _place()
        return [_rcopy(p_refs[n].at[2 * cx + cy], rcv_refs[n].at[t], ssem.at[3 * n + t], rsem.at[3 * n + t], (cx, cy, c))
                for n in range(k) for t, (cx, cy) in enumerate(_other_chips(x, y))]

    def start(p_refs, rcv_refs, sems):
        for cp in copies(p_refs, rcv_refs, sems):
            cp.start()

    def finish(p_refs, rcv_refs, sems):
        for cp in copies(p_refs, rcv_refs, sems):
            cp.wait()

    return _Side(list(parts), [_sds((3,) + p.shape[1:], p.dtype) for p in parts], [_dma_sems(3 * k)] * 2, start, finish)


def _run_side(side, name):
    k_in, k_out = len(side.arrays), len(side.out_shape)

    def body(*refs):
        parts = refs[:k_in], refs[k_in:k_in + k_out], refs[k_in + k_out:]
        side.start(*parts)
        side.finish(*parts)

    return _comm_call(body, name, list(side.out_shape), k_in, list(side.scratch))(*side.arrays)


def _sibling_side(arrays, out_shape, n_copies, copies):
    def start(in_refs, out_refs, sems):
        for cp in copies(in_refs, out_refs, sems):
            cp.start()

    def finish(in_refs, out_refs, sems):
        for cp in copies(in_refs, out_refs, sems):
            cp.wait()

    return _Side(list(arrays), out_shape, [_dma_sems(n_copies)] * 2, start, finish)


class _ColBlocks:
    def __init__(self, array, off):
        self.array, self.off, self.dtype = array, off, array.dtype
        self.shape = (N_CHIPS, array.shape[0], GU_BLOCK)

    def block(self, ref, jj):
        return ref.at[:, pl.ds((2 * jj + self.off) * GU_BLOCK, GU_BLOCK)]


def _swap_side(gds):
    k = len(gds)

    def copies(gd_refs, out_refs, sems):
        ssem, rsem = sems
        x, y, c = _place()
        blocks = [[g.block(gd_refs[n], jj) if isinstance(g, _ColBlocks) else gd_refs[n].at[jj] for jj in range(N_CHIPS)]
                  for n, g in enumerate(gds)]
        return [_rcopy(_half(blocks[n][jj], 1 - c), out_refs[n].at[jj], ssem.at[N_CHIPS * n + jj],
                       rsem.at[N_CHIPS * n + jj], (x, y, 1 - c)) for n in range(k) for jj in range(N_CHIPS)]

    return _sibling_side([g.array if isinstance(g, _ColBlocks) else g for g in gds],
                         [_sds((N_CHIPS, g.shape[1] // 2, g.shape[2]), g.dtype) for g in gds], N_CHIPS * k, copies)


def _share_side(reds):
    k = len(reds)

    def copies(r_refs, out_refs, sems):
        ssem, rsem = sems
        x, y, c = _place()
        return [_rcopy(r_refs[n], out_refs[n], ssem.at[n], rsem.at[n], (x, y, 1 - c)) for n in range(k)]

    return _sibling_side(reds, [_sds(r.shape, r.dtype) for r in reds], k, copies)


def _join_sides(sides):
    if len(sides) == 1:
        return sides[0]
    cuts = [(len(s.arrays), len(s.out_shape), len(s.scratch)) for s in sides]

    def each(method, in_refs, out_refs, sems):
        a = o = m = 0
        for s, (ka, ko, km) in zip(sides, cuts):
            getattr(s, method)(in_refs[a:a + ka], out_refs[o:o + ko], sems[m:m + km])
            a, o, m = a + ka, o + ko, m + km

    return _Side([x for s in sides for x in s.arrays], [x for s in sides for x in s.out_shape],
                 [x for s in sides for x in s.scratch], functools.partial(each, "start"), functools.partial(each, "finish"))


def _allreduce_small(small):
    def body(s_ref, all_ref, sssem, srsem, lsem):
        x, y, c = _place()
        me = 4 * x + 2 * y + c
        own = pltpu.make_async_copy(s_ref, all_ref.at[me], lsem)
        own.start()
        cps = []
        for r in range(1, 8):
            fx, fy, fc = (r >> 2) & 1, (r >> 1) & 1, r & 1
            px, py, pc = (1 - x if fx else x, 1 - y if fy else y, 1 - c if fc else c)
            peer = 4 * px + 2 * py + pc
            send = _rcopy(s_ref, all_ref.at[me], sssem.at[r - 1], srsem.at[me], (px, py, pc))
            send.start()
            cps.append((send, _rcopy(s_ref, all_ref.at[peer], sssem.at[r - 1], srsem.at[peer], (px, py, pc))))
        for send, recv in cps:
            send.wait_send()
            recv.wait_recv()
        own.wait()

    return _comm_call(body, "allreduce_small", [_sds((8,) + small.shape, small.dtype)], 1,
                      [pltpu.SemaphoreType.DMA((7,)), pltpu.SemaphoreType.DMA((8,)), pltpu.SemaphoreType.DMA(())])(small)[0]


def _add_pair(gd, got, c, name):
    _, R, W = got.shape
    tm = _pick(R, (512, 256, 128, 64))
    nb = R // tm

    def body(c_ref, a_ref, b_ref, o_ref):
        o_ref[...] = (a_ref[...].astype(F32) + b_ref[...].astype(F32)).astype(o_ref.dtype)

    if isinstance(gd, _ColBlocks):
        off = gd.off
        own = pl.BlockSpec((tm, W), lambda j, i, c_ref: (c_ref[0] * nb + i, 2 * j + off))
        gd = gd.array
    else:
        own = pl.BlockSpec((None, tm, W), lambda j, i, c_ref: (j, c_ref[0] * nb + i, 0))
    grid_spec = pltpu.PrefetchScalarGridSpec(
        num_scalar_prefetch=1, grid=(N_CHIPS, nb),
        in_specs=[own, pl.BlockSpec((None, tm, W), lambda j, i, c_ref: (j, i, 0))],
        out_specs=pl.BlockSpec((None, tm, W), lambda j, i, c_ref: (j, i, 0)))
    return pl.pallas_call(body, name=name, grid_spec=grid_spec, out_shape=_sds((N_CHIPS, R, W), gd.dtype),
                          compiler_params=pltpu.CompilerParams(dimension_semantics=("parallel", "parallel"),
                                                               vmem_limit_bytes=VMEM_LIMIT))(c, gd, got)


def _add_chips(part, rcv, j, name):
    _, R, W = part.shape
    tm = _pick(R, (512, 256, 128, 64))

    def body(j_ref, p_ref, r0_ref, r1_ref, r2_ref, o_ref):
        o_ref[...] = ((p_ref[...].astype(F32) + r0_ref[...].astype(F32)) + r1_ref[...].astype(F32)) + r2_ref[...].astype(F32)

    def slot(t):
        return pl.BlockSpec((None, tm, W), lambda i, j_ref: (t, i, 0))

    grid_spec = pltpu.PrefetchScalarGridSpec(
        num_scalar_prefetch=1, grid=(R // tm,),
        in_specs=[pl.BlockSpec((None, tm, W), lambda i, j_ref: (j_ref[0], i, 0)), slot(0), slot(1), slot(2)],
        out_specs=pl.BlockSpec((tm, W), lambda i, j_ref: (i, 0)))
    return pl.pallas_call(body, name=name, grid_spec=grid_spec, out_shape=_sds((R, W), F32),
                          compiler_params=pltpu.CompilerParams(dimension_semantics=("parallel",),
                                                               vmem_limit_bytes=VMEM_LIMIT))(j, part, rcv, rcv, rcv)


def _sum_small(allsmall):
    _, R, W = allsmall.shape

    def body(a_ref, o_ref):
        acc = a_ref[0]
        for d in range(1, 8):
            acc = acc + a_ref[d]
        o_ref[...] = acc

    return _call(body, "sum_small", _sds((R, W), F32), (1,), [_whole((8, R, W))], _whole((R, W)),
                 sem=("arbitrary",))(allsmall)


def _adamw(w, g, m, v, name):
    R, C = w.shape
    tm = _pick(R, (256, 128, 64, 32, 8))

    def body(w_ref, g_ref, m_ref, v_ref, d_ref, mo_ref, vo_ref):
        gv = g_ref[...]
        mn = ADAM_B1 * m_ref[...] + (1.0 - ADAM_B1) * gv
        vn = ADAM_B2 * v_ref[...] + (1.0 - ADAM_B2) * (gv * gv)
        m_hat = mn / (1.0 - ADAM_B1 ** ADAM_STEP)
        v_hat = vn / (1.0 - ADAM_B2 ** ADAM_STEP)
        d_ref[...] = -ADAM_LR * (m_hat / (jnp.sqrt(v_hat) + ADAM_EPS) + ADAM_WD * w_ref[...])
        mo_ref[...] = mn
        vo_ref[...] = vn

    spec = _rows(tm, C)
    return _call(body, name, [_sds((R, C), F32)] * 3, (R // tm,), [spec] * 4, [spec] * 3, sem=("parallel",))(w, g, m, v)


def _adamw_layer(c, w, m, v, mine, other, l, prev, name):
    _, R, C = w.shape
    half = R // 2
    tm = _pick(half, (256, 128, 64))
    nbh = half // tm

    def body(c_ref, w_ref, m_ref, v_ref, a_ref, b_ref, *rest):
        g_ref, d_ref, mo_ref, vo_ref = rest[-4:]
        gv = jnp.where(pl.program_id(0) // nbh == c_ref[0], a_ref[...], b_ref[...])
        mn = ADAM_B1 * m_ref[...] + (1.0 - ADAM_B1) * gv
        vn = ADAM_B2 * v_ref[...] + (1.0 - ADAM_B2) * (gv * gv)
        m_hat = mn / (1.0 - ADAM_B1 ** ADAM_STEP)
        v_hat = vn / (1.0 - ADAM_B2 ** ADAM_STEP)
        g_ref[...] = gv
        d_ref[...] = -ADAM_LR * (m_hat / (jnp.sqrt(v_hat) + ADAM_EPS) + ADAM_WD * w_ref[...])
        mo_ref[...] = mn
        vo_ref[...] = vn

    layer = pl.BlockSpec((None, tm, C), lambda i, c_ref: (l, i, 0))
    halfspec = pl.BlockSpec((tm, C), lambda i, c_ref: (i % nbh, 0))
    n_prev = 0 if prev is None else 4
    grid_spec = pltpu.PrefetchScalarGridSpec(
        num_scalar_prefetch=1, grid=(R // tm,),
        in_specs=[layer] * 3 + [halfspec] * 2 + [pl.BlockSpec(memory_space=pl.ANY)] * n_prev,
        out_specs=[layer] * 4)
    return pl.pallas_call(body, name=name, grid_spec=grid_spec, out_shape=[_sds(w.shape, F32)] * 4,
                          input_output_aliases={6 + k: k for k in range(n_prev)},
                          compiler_params=pltpu.CompilerParams(dimension_semantics=("parallel",),
                                                               vmem_limit_bytes=VMEM_LIMIT))(
        c, w, m, v, mine, other, *(prev or ()))


FIRST_GATHER = ("w_in", "w_uq", "w_ukv")
G_DOWN, G_GU, G_OUT, G_IN = ("w_down",), ("w_gate", "w_up"), ("w_out",), ("w_uq", "w_ukv", "w_in")


def _backward_jobs(l):
    t = f"_l{l}"
    return {"mm_dact" + t: [("swap", l, G_DOWN)], "mm_dw_gu" + t: [("exchange", l, G_DOWN)],
            "mm_dx1" + t: [("swap", l, G_GU), ("share", l, G_DOWN)], "mm_dmixin" + t: [("swap", l, G_OUT)],
            "ret_bwd" + t: [("exchange", l, ("w_gate",))],
            "mla_bwd" + t: [("exchange", l, ("w_up", "w_out")), ("share", l, ("w_gate",))],
            "mm_dw_in" + t: [("share", l, ("w_up", "w_out"))]}


JOBS = {
    "mm_h_l0": [("gather", 0, ("w_up@a",))], "mla_fwd_l0": [("gather", 0, ("w_gate", "w_out"))],
    "ret_fwd_l0": [("gather", 0, ("w_up@b",))],
    "mm_gu_l0": [("gather", 0, ("w_down",)), ("gather", 1, ("w_uq", "w_ukv"))],
    "mm_down_l0": [("gather", 1, ("w_in",))], "mm_h_l1": [("gather", 1, ("w_up@a",))],
    "mla_fwd_l1": [("gather", 1, ("w_gate", "w_out"))], "ret_fwd_l1": [("gather", 1, ("w_up@b",))],
    "mm_gu_l1": [("gather", 1, ("w_down",))],
    **_backward_jobs(1), **_backward_jobs(0),
    "mm_dxl_l1": [("swap", 1, G_IN)],
    "mm_dx1_l0": [("swap", 0, G_GU), ("share", 0, G_DOWN), ("exchange", 1, G_IN)],
    "ret_bwd_l0": [("exchange", 0, ("w_gate",)), ("share", 1, G_IN)], "mm_dxl_l0": [("exchange", 0, G_IN)]}
PLANNED = {job for jobs in JOBS.values() for job in jobs}


class _Pipeline:
    def __init__(self, own, Wt, Mo, Vo, core, chip):
        self.own, self.Wt, self.Mo, self.Vo, self.core, self.chip = own, Wt, Mo, Vo, core, chip
        self.blocks, self.whole, self.gds, self.parts, self.reds = {}, {}, {}, {}, {}
        self.results = {n: None for n in BIG}

    def gather_first(self):
        job = ("gather", 0, FIRST_GATHER)
        self._done(*job, _run_side(self._side(*job), "allgather_first"))

    def _gathered(self, l, n):
        if n in ROW_PIECES:
            return jnp.concatenate([self.blocks[(l, n + "@a")], self.blocks[(l, n + "@b")]], axis=1)
        return self.blocks[(l, n)]

    def weight(self, l, name):
        if (l, name) not in self.whole:
            self.whole[(l, name)] = _internal_weight(name, *[self._gathered(l, n) for n in INTERNAL_OF[name]])
        return self.whole[(l, name)]

    def run(self, fn, name, *args, **kw):
        jobs = JOBS.get(name, ())
        if not jobs:
            return fn(*args, name=name, **kw)
        sides = [self._side(*job) for job in jobs]
        out, res = fn(*args, name=name, side=_join_sides(sides), **kw)
        for job, side in zip(jobs, sides):
            k = len(side.out_shape)
            self._done(*job, res[:k])
            res = res[k:]
        return out

    def reduce(self, l, **grads):
        shards = {}
        for name, g in grads.items():
            shards.update(_grad_shards(name, g))
        for n, sh in shards.items():
            self.gds[(l, n)] = sh if hasattr(sh, "shape") else jnp.stack(sh)
        self._alone("swap", l, tuple(shards))

    def _alone(self, kind, l, names):
        if (kind, l, names) not in PLANNED:
            self._done(kind, l, names, _run_side(self._side(kind, l, names), f"{kind}_{names[0]}_l{l}"))

    def _side(self, kind, l, names):
        if kind == "gather":
            return _allgather_side([self.own[l][n] for n in names])
        store = {"swap": self.gds, "exchange": self.parts, "share": self.reds}[kind]
        make = {"swap": _swap_side, "exchange": _exchange_side, "share": _share_side}[kind]
        return make([store[(l, n)] for n in names])

    def _done(self, kind, l, names, res):
        for n, r in zip(names, res):
            if kind == "gather":
                self.blocks[(l, n)] = r
            elif kind == "swap":
                self.parts[(l, n)] = _add_pair(self.gds[(l, n)], r, self.core, f"add_pair_{n}_l{l}")
            elif kind == "exchange":
                self.reds[(l, n)] = _add_chips(self.parts[(l, n)], r, self.chip, f"add_chips_{n}_l{l}")
            else:
                self.results[n] = _adamw_layer(self.core, self.Wt[n], self.Mo[n], self.Vo[n], self.reds[(l, n)], r, l,
                                               self.results[n], f"adamw_{n}_l{l}")
        if kind == "exchange":
            self._alone("share", l, names)


def kernel(x, positions, ln_in_g, ln_in_b, w_in, q_norm_g, kv_norm_g, w_uq, w_ukv, ret_gn_g, ret_gn_b, w_out, ln1_g, ln1_b, w_gate, w_up, w_down, ln2_g, ln2_b, loss_target, m_ln_in_g, m_ln_in_b, m_w_in, m_q_norm_g, m_kv_norm_g, m_w_uq, m_w_ukv, m_ret_gn_g, m_ret_gn_b, m_w_out, m_ln1_g, m_ln1_b, m_w_gate, m_w_up, m_w_down, m_ln2_g, m_ln2_b, v_ln_in_g, v_ln_in_b, v_w_in, v_q_norm_g, v_kv_norm_g, v_w_uq, v_w_ukv, v_ret_gn_g, v_ret_gn_b, v_w_out, v_ln1_g, v_ln1_b, v_w_gate, v_w_up, v_w_down, v_ln2_g, v_ln2_b):
    given = dict(locals())
    Wt = {n: given[n] for n in WEIGHTS}
    Mo = {n: given["m_" + n] for n in WEIGHTS}
    Vo = {n: given["v_" + n] for n in WEIGHTS}
    cx, cy, cc = _place()
    chip = (2 * cx + cy).astype(jnp.int32)
    core = cc.astype(jnp.int32)

    own = [{n: Wt[n][l].astype(BF16) for n in BIG} for l in range(DEPTH)]
    for shard in own:
        for n, at in ROW_PIECES.items():
            shard[n + "@a"], shard[n + "@b"] = shard[n][:at], shard[n][at:]
    pipe = _Pipeline(own, Wt, Mo, Vo, core.reshape(1), chip.reshape(1))
    sqerr, grad_x, dP = _local_step(x[0], positions[0], loss_target[0], pipe, Wt)
    results = pipe.results

    small_g = {n: (dP[(n, None)] if Wt[n].ndim == 1 else jnp.stack([dP[(n, l)] for l in range(DEPTH)])) for n in SMALL}
    local_loss = 0.5 * jnp.sum(sqerr) / D_MODEL
    small_sum = _sum_small(_allreduce_small(_flatten_small(small_g, local_loss))).reshape(-1)
    layout, n_small = _small_layout(Wt)
    loss = small_sum[n_small]

    grads, deltas, new_m, new_v = {}, {}, {}, {}
    for n in BIG:
        grads[n], deltas[n], new_m[n], new_v[n] = results[n]
    zero = jnp.zeros((), F32)
    d, mn, vn = _adamw(_flatten_small(Wt, zero), small_sum.reshape(SMALL_ROWS, FLAT_W), _flatten_small(Mo, zero),
                       _flatten_small(Vo, zero), "adamw_small")
    for n in SMALL:
        at, size = layout[n]
        pick = lambda a: a.reshape(-1)[at:at + size].reshape(Wt[n].shape)
        grads[n], deltas[n], new_m[n], new_v[n] = pick(small_sum), pick(d), pick(mn), pick(vn)

    return (loss, grad_x[None], *[grads[n] for n in WEIGHTS], *[deltas[n] for n in WEIGHTS],
            *[new_m[n] for n in WEIGHTS], *[new_v[n] for n in WEIGHTS])
```

```python
import functools

import jax
import jax.numpy as jnp
from jax import lax
from jax.experimental import pallas as pl
from jax.experimental.pallas import tpu as pltpu

F32 = jnp.float32
BF16 = jnp.bfloat16

D_MODEL = 2048
DEPTH = 2
CHUNK = 64
MLA_HEADS = 8
Q_LORA = 512
KV_LORA = 256
NOPE = 128
ROPE = 64
VDIM = 128
RET_HEADS = 4
RET_DK = 256
RET_DV = 256
D_FF = 5632
D_IN = 4928
ROPE_THETA = 10000.0
LN_EPS = 1e-5
RMS_EPS = 1e-6
GN_EPS = 1e-5
ALPHA = (2 * DEPTH) ** 0.25
MLA_SCALE = (NOPE + ROPE) ** -0.5
RET_SCALE = RET_DK ** -0.5
ADAM_LR = 0.001
ADAM_B1 = 0.9
ADAM_B2 = 0.999
ADAM_EPS = 1e-08
ADAM_WD = 0.01
ADAM_STEP = 10

LANES = 128
HEAD_PAD = 256
MLA_IN = 1024
MLA_IN_USED = Q_LORA + KV_LORA + ROPE
D_IN_PAD = MLA_IN + 4 * 1024
ATT_BLOCK = 512
NEG = -1e30
VMEM_LIMIT = 56 * 1024 * 1024

N_CHIPS = 4
FLAT_W = 1024
BIG = ("w_in", "w_uq", "w_ukv", "w_out", "w_gate", "w_up", "w_down")
BIG_SHARD = {"w_in": (2048, 1232), "w_uq": (512, 384), "w_ukv": (256, 512), "w_out": (512, 2048),
             "w_gate": (2048, 1408), "w_up": (2048, 1408), "w_down": (1408, 2048)}
SMALL = ("ln_in_g", "ln_in_b", "q_norm_g", "kv_norm_g", "ret_gn_g", "ret_gn_b", "ln1_g", "ln1_b", "ln2_g", "ln2_b")
WEIGHTS = ("ln_in_g", "ln_in_b", "w_in", "q_norm_g", "kv_norm_g", "w_uq", "w_ukv", "ret_gn_g", "ret_gn_b", "w_out",
           "ln1_g", "ln1_b", "w_gate", "w_up", "w_down", "ln2_g", "ln2_b")
SMALL_ROWS = 32

MESH = pl.DeviceIdType.MESH


def _pick(dim, cands):
    for c in cands:
        if dim % c == 0:
            return c
    return dim


HBM = pl.BlockSpec(memory_space=pltpu.HBM)


class _Side:
    def __init__(self, arrays, out_shape, scratch, start, finish):
        self.arrays, self.out_shape, self.scratch, self.start, self.finish = arrays, out_shape, scratch, start, finish


def _call(body, name, out_shape, grid, in_specs, out_specs, scratch=(), sem=None, side=None):
    params = pltpu.CompilerParams(dimension_semantics=sem if side is None else ("arbitrary",) * len(grid),
                                  vmem_limit_bytes=VMEM_LIMIT)
    if side is None:
        return pl.pallas_call(body, name=name, out_shape=out_shape, grid=grid, in_specs=in_specs, out_specs=out_specs,
                              scratch_shapes=list(scratch), compiler_params=params)
    single = not isinstance(out_shape, (list, tuple))
    outs = [out_shape] if single else list(out_shape)
    ospecs = [out_specs] if single else list(out_specs)
    cuts = [len(in_specs), len(side.arrays), len(outs), len(side.out_shape), len(scratch)]
    ends = [sum(cuts[:k + 1]) for k in range(len(cuts))]

    def hosted(*refs):
        ins, s_in, o, s_out, scr = (refs[a:b] for a, b in zip([0] + ends[:-1], ends))
        sems = refs[ends[-1]:]
        ids = [pl.program_id(a) for a in range(len(grid))]
        first = functools.reduce(jnp.logical_and, [i == 0 for i in ids])
        last = functools.reduce(jnp.logical_and, [i == g - 1 for i, g in zip(ids, grid)])

        @pl.when(first)
        def _():
            side.start(s_in, s_out, sems)

        body(*ins, *o, *scr)

        @pl.when(last)
        def _():
            side.finish(s_in, s_out, sems)

    call = pl.pallas_call(hosted, name=name, out_shape=outs + list(side.out_shape), grid=grid,
                          in_specs=list(in_specs) + [HBM] * len(side.arrays),
                          out_specs=ospecs + [HBM] * len(side.out_shape),
                          scratch_shapes=list(scratch) + list(side.scratch), compiler_params=params)

    def run(*args):
        res = call(*args, *side.arrays)
        return (res[0] if single else list(res[:len(outs)])), list(res[len(outs):])

    return run


def _rows(tm, w, col=0):
    return pl.BlockSpec((tm, w), lambda i: (i, col))


def _whole(shape):
    return pl.BlockSpec(shape, lambda i: (0,) * len(shape))


def _sds(shape, dtype):
    return jax.ShapeDtypeStruct(shape, dtype)


def _matmul(a, b, name, ta=False, tb=False, out_dtype=F32, side=None):
    (K, M) = a.shape if ta else a.shape[::-1]
    (N, Kb) = b.shape if tb else b.shape[::-1]
    assert K == Kb, (a.shape, b.shape, ta, tb)
    tm = _pick(M, (1024, 1408, 512, 256, 128))
    tn = _pick(N, (1024, 512, 256, 128))
    tk = _pick(K, (2816, 2560, 2048, 1024, 512, 256))
    nk = K // tk
    dn = (((0 if ta else 1,), (1 if tb else 0,)), ((), ()))

    def body(a_ref, b_ref, o_ref, acc_ref):
        k = pl.program_id(2)
        if nk == 1:
            o_ref[...] = lax.dot_general(a_ref[...].astype(BF16), b_ref[...].astype(BF16), dn,
                                         preferred_element_type=F32).astype(out_dtype)
        else:
            @pl.when(k == 0)
            def _():
                acc_ref[...] = jnp.zeros_like(acc_ref)

            acc_ref[...] += lax.dot_general(a_ref[...].astype(BF16), b_ref[...].astype(BF16), dn,
                                            preferred_element_type=F32)

            @pl.when(k == nk - 1)
            def _():
                o_ref[...] = acc_ref[...].astype(out_dtype)

    a_spec = pl.BlockSpec((tk, tm), lambda i, j, k: (k, i)) if ta else pl.BlockSpec((tm, tk), lambda i, j, k: (i, k))
    b_spec = pl.BlockSpec((tn, tk), lambda i, j, k: (j, k)) if tb else pl.BlockSpec((tk, tn), lambda i, j, k: (k, j))
    return _call(body, name, _sds((M, N), out_dtype), (M // tm, N // tn, nk), [a_spec, b_spec],
                 pl.BlockSpec((tm, tn), lambda i, j, k: (i, j)), scratch=[pltpu.VMEM((tm, tn), F32)],
                 sem=("parallel", "parallel", "arbitrary"), side=side)(a, b)


def _sigmoid(x):
    return 1.0 / (1.0 + jnp.exp(-x))


def _rope_group(r, c, sa, sb):
    return r * c + pltpu.roll(r, 32, 1) * sa + pltpu.roll(r, 96, 1) * sb


def _ln_fwd(xs, coefs, g, b, name, want_z, side=None):
    S, D = xs[0].shape
    tm = 512
    n = len(xs)

    def body(*refs):
        x_refs, g_ref, b_ref, outs = refs[:n], refs[n], refs[n + 1], refs[n + 2:]
        z = None
        for cf, r in zip(coefs, x_refs):
            t = r[...] if cf == 1.0 else cf * r[...]
            z = t if z is None else z + t
        mu = jnp.mean(z, axis=-1, keepdims=True)
        zc = z - mu
        var = jnp.mean(zc * zc, axis=-1, keepdims=True)
        y = zc * lax.rsqrt(var + LN_EPS) * g_ref[...] + b_ref[...]
        if want_z:
            outs[0][...] = z
        outs[-2][...] = y
        outs[-1][...] = y.astype(BF16)

    out_shape = [_sds((S, D), F32)] * (2 if want_z else 1) + [_sds((S, D), BF16)]
    return _call(body, name, out_shape, (S // tm,), [_rows(tm, D)] * n + [_whole((1, D))] * 2,
                 [_rows(tm, D)] * len(out_shape), sem=("parallel",), side=side)(*xs, g, b)


def _matmul_ln(a, w, x, g, b, name):
    S, K = a.shape
    D = w.shape[1]
    tm = _pick(S, (256, 128))

    def body(a_ref, w_ref, x_ref, g_ref, b_ref, z_ref, y_ref, yb_ref):
        z = ALPHA * x_ref[...] + jnp.dot(a_ref[...], w_ref[...], preferred_element_type=F32)
        mu = jnp.mean(z, axis=-1, keepdims=True)
        zc = z - mu
        var = jnp.mean(zc * zc, axis=-1, keepdims=True)
        y = zc * lax.rsqrt(var + LN_EPS) * g_ref[...] + b_ref[...]
        z_ref[...] = z
        y_ref[...] = y
        yb_ref[...] = y.astype(BF16)

    return _call(body, name, [_sds((S, D), F32), _sds((S, D), F32), _sds((S, D), BF16)], (S // tm,),
                 [_rows(tm, K), _whole((K, D)), _rows(tm, D), _whole((1, D)), _whole((1, D))], [_rows(tm, D)] * 3,
                 sem=("parallel",))(a, w, x, g, b)


def _ln_bwd(dys, coefs, z, g, name):
    S, D = z.shape
    tm = 512
    n = len(dys)

    def body(*refs):
        dy_refs, z_ref, g_ref = refs[:n], refs[n], refs[n + 1]
        dz_ref, dzb_ref, dg_ref, db_ref = refs[n + 2:]
        dy = None
        for cf, r in zip(coefs, dy_refs):
            t = r[...] if cf == 1.0 else cf * r[...]
            dy = t if dy is None else dy + t
        zv = z_ref[...]
        mu = jnp.mean(zv, axis=-1, keepdims=True)
        zc = zv - mu
        var = jnp.mean(zc * zc, axis=-1, keepdims=True)
        rstd = lax.rsqrt(var + LN_EPS)
        xh = zc * rstd
        dyg = dy * g_ref[...]
        dz = rstd * (dyg - jnp.mean(dyg, axis=-1, keepdims=True) - xh * jnp.mean(dyg * xh, axis=-1, keepdims=True))
        dz_ref[...] = dz
        dzb_ref[...] = dz.astype(BF16)

        @pl.when(pl.program_id(0) == 0)
        def _():
            dg_ref[...] = jnp.zeros_like(dg_ref)
            db_ref[...] = jnp.zeros_like(db_ref)

        dg_ref[...] += jnp.sum(dy * xh, axis=0, keepdims=True)
        db_ref[...] += jnp.sum(dy, axis=0, keepdims=True)

    return _call(body, name, [_sds((S, D), F32), _sds((S, D), BF16), _sds((1, D), F32), _sds((1, D), F32)],
                 (S // tm,), [_rows(tm, D)] * (n + 1) + [_whole((1, D))],
                 [_rows(tm, D), _rows(tm, D), _whole((1, D)), _whole((1, D))], sem=("arbitrary",))(*dys, z, g)


def _rms(x, g):
    return x * lax.rsqrt(jnp.mean(x * x, axis=-1, keepdims=True) + RMS_EPS) * g


def _prep1(h, tabs, qg, kvg, name):
    S = h.shape[0]
    tm = 512
    cm, sam, sbm, cr, sr = tabs

    def body(h_ref, cm_ref, sam_ref, sbm_ref, cr_ref, sr_ref, qg_ref, kvg_ref,
             qn_ref, kvn_ref, kr_ref, rq_ref, rk_ref, rv_ref):
        qn_ref[...] = _rms(h_ref[:, 0:Q_LORA], qg_ref[...]).astype(BF16)
        kvn_ref[...] = _rms(h_ref[:, Q_LORA:Q_LORA + KV_LORA], kvg_ref[...]).astype(BF16)
        kr_ref[...] = _rope_group(h_ref[:, 768:896], cm_ref[...], sam_ref[...], sbm_ref[...])
        c, s = cr_ref[...], sr_ref[...]
        for hd in range(RET_HEADS):
            for src, dst, scale in ((MLA_IN, rq_ref, RET_SCALE), (MLA_IN + 1024, rk_ref, None)):
                t1 = h_ref[:, src + hd * 256:src + hd * 256 + 128]
                t2 = h_ref[:, src + hd * 256 + 128:src + hd * 256 + 256]
                o1, o2 = t1 * c - t2 * s, t2 * c + t1 * s
                if scale is not None:
                    o1, o2 = o1 * scale, o2 * scale
                dst[:, hd * 256:hd * 256 + 128] = o1.astype(BF16)
                dst[:, hd * 256 + 128:hd * 256 + 256] = o2.astype(BF16)
        rv_ref[...] = h_ref[:, MLA_IN + 2048:MLA_IN + 3072].astype(BF16)

    t128 = _rows(tm, LANES)
    return _call(body, name,
                 [_sds((S, Q_LORA), BF16), _sds((S, KV_LORA), BF16), _sds((S, LANES), F32),
                  _sds((S, 1024), BF16), _sds((S, 1024), BF16), _sds((S, 1024), BF16)],
                 (S // tm,),
                 [_rows(tm, D_IN_PAD), t128, t128, t128, t128, t128, _whole((1, Q_LORA)), _whole((1, KV_LORA))],
                 [_rows(tm, Q_LORA), _rows(tm, KV_LORA), t128, _rows(tm, 1024), _rows(tm, 1024), _rows(tm, 1024)],
                 sem=("parallel",))(h, cm, sam, sbm, cr, sr, qg, kvg)


def _prep1_bwd(dqn, dkvn, dkr, drq, drk, drv, drg, h, tabs, qg, kvg, name):
    S = h.shape[0]
    tm = 512
    cm, sam, sbm, cr, sr = tabs

    def rms_bwd(x, g, dy):
        r = lax.rsqrt(jnp.mean(x * x, axis=-1, keepdims=True) + RMS_EPS)
        dyg = dy * g
        dx = r * dyg - x * (r * r * r) * jnp.mean(dyg * x, axis=-1, keepdims=True)
        return dx, jnp.sum(dy * x * r, axis=0, keepdims=True)

    def body(dqn_ref, dkvn_ref, dkr_ref, drq_ref, drk_ref, drv_ref, drg_ref, h_ref,
             cm_ref, sam_ref, sbm_ref, cr_ref, sr_ref, qg_ref, kvg_ref, dh_ref, dqg_ref, dkvg_ref):
        dcq, dqg = rms_bwd(h_ref[:, 0:Q_LORA], qg_ref[...], dqn_ref[...])
        dckv, dkvg = rms_bwd(h_ref[:, Q_LORA:Q_LORA + KV_LORA], kvg_ref[...], dkvn_ref[...])
        dh_ref[:, 0:Q_LORA] = dcq.astype(BF16)
        dh_ref[:, Q_LORA:Q_LORA + KV_LORA] = dckv.astype(BF16)
        dh_ref[:, 768:896] = _rope_group(dkr_ref[...], cm_ref[...], -sam_ref[...], -sbm_ref[...]).astype(BF16)
        dh_ref[:, 896:1024] = jnp.zeros((tm, LANES), BF16)
        c, s = cr_ref[...], sr_ref[...]
        for hd in range(RET_HEADS):
            for src, dst, scale in ((drq_ref, MLA_IN, RET_SCALE), (drk_ref, MLA_IN + 1024, None)):
                d1 = src[:, hd * 256:hd * 256 + 128]
                d2 = src[:, hd * 256 + 128:hd * 256 + 256]
                if scale is not None:
                    d1, d2 = d1 * scale, d2 * scale
                dh_ref[:, dst + hd * 256:dst + hd * 256 + 128] = (d1 * c + d2 * s).astype(BF16)
                dh_ref[:, dst + hd * 256 + 128:dst + hd * 256 + 256] = (d2 * c - d1 * s).astype(BF16)
        dh_ref[:, MLA_IN + 2048:MLA_IN + 3072] = drv_ref[...].astype(BF16)
        dh_ref[:, MLA_IN + 3072:MLA_IN + 4096] = drg_ref[...].astype(BF16)

        @pl.when(pl.program_id(0) == 0)
        def _():
            dqg_ref[...] = jnp.zeros_like(dqg_ref)
            dkvg_ref[...] = jnp.zeros_like(dkvg_ref)

        dqg_ref[...] += dqg
        dkvg_ref[...] += dkvg

    t128 = _rows(tm, LANES)
    return _call(body, name,
                 [_sds((S, D_IN_PAD), BF16), _sds((1, Q_LORA), F32), _sds((1, KV_LORA), F32)],
                 (S // tm,),
                 [_rows(tm, Q_LORA), _rows(tm, KV_LORA), t128, _rows(tm, 1024), _rows(tm, 1024), _rows(tm, 1024),
                  _rows(tm, 1024), _rows(tm, MLA_IN), t128, t128, t128, t128, t128,
                  _whole((1, Q_LORA)), _whole((1, KV_LORA))],
                 [_rows(tm, D_IN_PAD), _whole((1, Q_LORA)), _whole((1, KV_LORA))],
                 sem=("arbitrary",))(dqn, dkvn, dkr, drq, drk, drv, drg, h, cm, sam, sbm, cr, sr, qg, kvg)


def _prep2(q, kv, kr, tabs, name):
    S = q.shape[0]
    tm = 512
    cm, sam, sbm = tabs[:3]

    def body(q_ref, kv_ref, kr_ref, cm_ref, sam_ref, sbm_ref, qo_ref, ko_ref, vo_ref):
        c, sa, sb = cm_ref[...], sam_ref[...], sbm_ref[...]
        krb = kr_ref[...].astype(BF16)
        ones = jnp.ones((tm, LANES), BF16)
        for hd in range(MLA_HEADS):
            o = hd * HEAD_PAD
            qo_ref[:, o:o + 128] = (q_ref[:, o:o + 128] * MLA_SCALE).astype(BF16)
            qo_ref[:, o + 128:o + 256] = (_rope_group(q_ref[:, o + 128:o + 256], c, sa, sb) * MLA_SCALE).astype(BF16)
            ko_ref[:, o:o + 128] = kv_ref[:, hd * 128:hd * 128 + 128].astype(BF16)
            ko_ref[:, o + 128:o + 256] = krb
            vo_ref[:, o:o + 128] = kv_ref[:, 1024 + hd * 128:1024 + hd * 128 + 128].astype(BF16)
            vo_ref[:, o + 128:o + 256] = ones

    t128 = _rows(tm, LANES)
    return _call(body, name, [_sds((S, 2048), BF16)] * 3, (S // tm,),
                 [_rows(tm, 2048), _rows(tm, 2048), t128, t128, t128, t128],
                 [_rows(tm, 2048)] * 3, sem=("parallel",))(q, kv, kr, cm, sam, sbm)


def _prep2_bwd(dqm, dkm, dvm, tabs, name):
    S = dqm.shape[0]
    tm = 512
    cm, sam, sbm = tabs[:3]

    def body(dq_ref, dk_ref, dv_ref, cm_ref, sam_ref, sbm_ref, dqo_ref, dkvo_ref, dkr_ref):
        c, sa, sb = cm_ref[...], -sam_ref[...], -sbm_ref[...]
        dkr = None
        for hd in range(MLA_HEADS):
            o = hd * HEAD_PAD
            dqo_ref[:, o:o + 128] = (dq_ref[:, o:o + 128] * MLA_SCALE).astype(BF16)
            dqo_ref[:, o + 128:o + 256] = (_rope_group(dq_ref[:, o + 128:o + 256], c, sa, sb) * MLA_SCALE).astype(BF16)
            dkvo_ref[:, hd * 128:hd * 128 + 128] = dk_ref[:, o:o + 128].astype(BF16)
            t = dk_ref[:, o + 128:o + 256]
            dkr = t if dkr is None else dkr + t
        dkvo_ref[:, 1024:2048] = dv_ref[...].astype(BF16)
        dkr_ref[...] = dkr

    t128 = _rows(tm, LANES)
    return _call(body, name, [_sds((S, 2048), BF16), _sds((S, 2048), BF16), _sds((S, LANES), F32)], (S // tm,),
                 [_rows(tm, 2048), _rows(tm, 2048), _rows(tm, 1024), t128, t128, t128],
                 [_rows(tm, 2048), _rows(tm, 2048), t128], sem=("parallel",))(dqm, dkm, dvm, cm, sam, sbm)


def _gn_gate(a, o, h, gg, gb, name):
    S = a.shape[0]
    tm = 512

    def body(a_ref, o_ref, rg_ref, gg_ref, gb_ref, mix_ref):
        mix_ref[:, 0:1024] = a_ref[...].astype(BF16)
        for hd in range(RET_HEADS):
            sl = slice(hd * 256, hd * 256 + 256)
            ov = o_ref[:, sl]
            mu = jnp.mean(ov, axis=-1, keepdims=True)
            oc = ov - mu
            var = jnp.mean(oc * oc, axis=-1, keepdims=True)
            y = oc * lax.rsqrt(var + GN_EPS) * gg_ref[:, sl] + gb_ref[:, sl]
            rg = rg_ref[:, sl]
            mix_ref[:, 1024 + hd * 256:1024 + hd * 256 + 256] = (rg * _sigmoid(rg) * y).astype(BF16)

    return _call(body, name, _sds((S, 2048), BF16), (S // tm,),
                 [_rows(tm, 1024), _rows(tm, 1024), _rows(tm, 1024, 4), _whole((1, 1024)), _whole((1, 1024))],
                 _rows(tm, 2048), sem=("parallel",))(a, o, h, gg, gb)


def _gn_gate_bwd(dmixin, o, h, gg, gb, name):
    S = o.shape[0]
    tm = 512

    def body(dr_ref, o_ref, rg_ref, gg_ref, gb_ref, do_ref, drg_ref, dgg_ref, dgb_ref):
        @pl.when(pl.program_id(0) == 0)
        def _():
            dgg_ref[...] = jnp.zeros_like(dgg_ref)
            dgb_ref[...] = jnp.zeros_like(dgb_ref)

        for hd in range(RET_HEADS):
            sl = slice(hd * 256, hd * 256 + 256)
            ov = o_ref[:, sl]
            mu = jnp.mean(ov, axis=-1, keepdims=True)
            oc = ov - mu
            var = jnp.mean(oc * oc, axis=-1, keepdims=True)
            rstd = lax.rsqrt(var + GN_EPS)
            xh = oc * rstd
            g = gg_ref[:, sl]
            y = xh * g + gb_ref[:, sl]
            rg = rg_ref[:, sl]
            sg = _sigmoid(rg)
            dr = dr_ref[:, sl]
            dy = dr * (rg * sg)
            drg_ref[:, sl] = dr * y * (sg * (1.0 + rg * (1.0 - sg)))
            dgg_ref[:, sl] += jnp.sum(dy * xh, axis=0, keepdims=True)
            dgb_ref[:, sl] += jnp.sum(dy, axis=0, keepdims=True)
            dxh = dy * g
            do = rstd * (dxh - jnp.mean(dxh, axis=-1, keepdims=True) - xh * jnp.mean(dxh * xh, axis=-1, keepdims=True))
            do_ref[:, sl] = do.astype(BF16)

    return _call(body, name,
                 [_sds((S, 1024), BF16), _sds((S, 1024), F32), _sds((1, 1024), F32), _sds((1, 1024), F32)],
                 (S // tm,),
                 [_rows(tm, 1024, 1), _rows(tm, 1024), _rows(tm, 1024, 4), _whole((1, 1024)), _whole((1, 1024))],
                 [_rows(tm, 1024), _rows(tm, 1024), _whole((1, 1024)), _whole((1, 1024))],
                 sem=("arbitrary",))(dmixin, o, h, gg, gb)


GU_BLOCK = D_FF // N_CHIPS


def _matmul_swiglu(x, w_gu, name, side=None):
    S, K = x.shape
    tm = _pick(S, (512, 256, 128))
    tn = 2 * GU_BLOCK

    def body(x_ref, w_ref, gu_ref, act_ref):
        r = jnp.dot(x_ref[...], w_ref[...], preferred_element_type=F32)
        g, u = r[:, :GU_BLOCK], r[:, GU_BLOCK:]
        gu_ref[...] = r.astype(BF16)
        act_ref[...] = (g * _sigmoid(g) * u).astype(BF16)

    return _call(body, name, [_sds((S, 2 * D_FF), BF16), _sds((S, D_FF), BF16)], (N_CHIPS, S // tm),
                 [pl.BlockSpec((tm, K), lambda j, i: (i, 0)), pl.BlockSpec((K, tn), lambda j, i: (0, j))],
                 [pl.BlockSpec((tm, tn), lambda j, i: (i, j)), pl.BlockSpec((tm, GU_BLOCK), lambda j, i: (i, j))],
                 sem=("parallel", "parallel"), side=side)(x, w_gu)


def _matmul_swiglu_bwd(df, w_down, gu, name, side=None):
    S, K = df.shape
    tm = _pick(S, (512, 256, 128))

    def body(df_ref, w_ref, gu_ref, o_ref):
        d = _dot_nt(df_ref[...], w_ref[...])
        g = gu_ref[:, :GU_BLOCK].astype(F32)
        u = gu_ref[:, GU_BLOCK:].astype(F32)
        sg = _sigmoid(g)
        o_ref[:, :GU_BLOCK] = (d * u * (sg * (1.0 + g * (1.0 - sg)))).astype(BF16)
        o_ref[:, GU_BLOCK:] = (d * (g * sg)).astype(BF16)

    gu_spec = pl.BlockSpec((tm, 2 * GU_BLOCK), lambda j, i: (i, j))
    return _call(body, name, _sds((S, 2 * D_FF), BF16), (N_CHIPS, S // tm),
                 [pl.BlockSpec((tm, K), lambda j, i: (i, 0)), pl.BlockSpec((GU_BLOCK, K), lambda j, i: (j, 0)), gu_spec],
                 gu_spec, sem=("parallel", "parallel"), side=side)(df, w_down, gu)


def _ln_loss(x, f, g, b, target, name):
    S, D = x.shape
    tm = 256

    def body(x_ref, f_ref, g_ref, b_ref, t_ref, z_ref, dy_ref, acc_ref):
        z = ALPHA * x_ref[...] + f_ref[...]
        mu = jnp.mean(z, axis=-1, keepdims=True)
        zc = z - mu
        var = jnp.mean(zc * zc, axis=-1, keepdims=True)
        e = zc * lax.rsqrt(var + LN_EPS) * g_ref[...] + b_ref[...] - t_ref[...]
        z_ref[...] = z
        dy_ref[...] = e / D

        @pl.when(pl.program_id(0) == 0)
        def _():
            acc_ref[...] = jnp.zeros_like(acc_ref)

        acc_ref[...] += jnp.sum(e * e, axis=0, keepdims=True)

    return _call(body, name, [_sds((S, D), F32), _sds((S, D), F32), _sds((1, D), F32)], (S // tm,),
                 [_rows(tm, D), _rows(tm, D), _whole((1, D)), _whole((1, D)), _rows(tm, D)],
                 [_rows(tm, D), _rows(tm, D), _whole((1, D))], sem=("arbitrary",))(x, f, g, b, target)


def _chunk_mask(T):
    r = lax.shift_right_logical(lax.broadcasted_iota(jnp.int32, (T, T), 0), 6)
    c = lax.shift_right_logical(lax.broadcasted_iota(jnp.int32, (T, T), 1), 6)
    return r >= c


def _dot_nt(a, b):
    return lax.dot_general(a, b, (((1,), (1,)), ((), ())), preferred_element_type=F32)


def _dot_tn(a, b):
    return lax.dot_general(a, b, (((0,), (0,)), ((), ())), preferred_element_type=F32)


def _decay_tables(T):
    lg = jnp.log1p(-jnp.exp2(-5.0 - jnp.arange(RET_HEADS, dtype=F32)))
    idx = jnp.arange(T, dtype=F32)
    diff = idx[:, None] - idx[None, :]
    rel = jnp.exp(lg[:, None, None] * diff[None])
    cid = jnp.arange(T) // CHUNK
    mask = (cid[:, None] >= cid[None, :]).astype(F32)
    reld = jnp.exp(lg[:, None, None] * jnp.abs(diff)[None]) * mask[None]
    lgrow = jnp.broadcast_to(lg[:, None, None], (RET_HEADS, 1, LANES))
    return lgrow, rel, reld


def _attn_fwd(q, k, v, heads, dk, dv, softmax, name, tables=None, side=None):
    S = q.shape[0]
    T = ATT_BLOCK
    nq = S // T
    rep = T // LANES
    vw = 2 * dv if softmax else dv
    assert not softmax or dv == LANES

    def body(*refs):
        if softmax:
            q_ref, k_ref, v_ref, o_ref, lse_ref, m_sc, acc_sc = refs
        else:
            q_ref, k_ref, v_ref, lg_ref, rel_ref, reld_ref, o_ref, acc_sc = refs
        i = pl.program_id(1)
        qv = q_ref[...]

        def kv_block(j):
            rows = pl.ds(pl.multiple_of(j * T, T), T)
            return k_ref[rows, :], v_ref[rows, :]

        kb, vb = kv_block(i)
        s = _dot_nt(qv, kb)
        if softmax:
            s = jnp.where(_chunk_mask(T), s, NEG)
            m = jnp.max(s, axis=-1, keepdims=True)
            p = jnp.exp(s - m)
            m_sc[...] = jnp.broadcast_to(m, (T, LANES))
        else:
            p = s * reld_ref[0]
        acc_sc[...] = jnp.dot(p.astype(BF16), vb, preferred_element_type=F32)

        def scores(j):
            kb, vb = kv_block(j)
            return _dot_nt(qv, kb), vb

        def update(j, s, vb):
            if softmax:
                m_prev = m_sc[...]
                m_next = jnp.maximum(m_prev, jnp.max(s, axis=-1, keepdims=True))
                alpha = jnp.exp(m_prev - m_next)
                p = jnp.exp(s - jnp.tile(m_next, (1, rep)))
                m_sc[...] = m_next
                acc_sc[...] = acc_sc[...] * jnp.tile(alpha, (1, vw // LANES)) + jnp.dot(
                    p.astype(BF16), vb, preferred_element_type=F32)
            else:
                fac = jnp.exp(lg_ref[0] * ((i - j) * T).astype(F32))
                p = s * (rel_ref[0] * jnp.tile(fac, (1, rep)))
                acc_sc[...] += jnp.dot(p.astype(BF16), vb, preferred_element_type=F32)

        def pair(jj, carry):
            first, second = scores(2 * jj), scores(2 * jj + 1)
            update(2 * jj, *first)
            update(2 * jj + 1, *second)
            return carry

        lax.fori_loop(0, i // 2, pair, 0)

        @pl.when(i % 2 == 1)
        def _():
            update(i - 1, *scores(i - 1))

        if softmax:
            l = acc_sc[:, dv:]
            o_ref[...] = acc_sc[:, :dv] / l
            lse_ref[...] = m_sc[...] + jnp.log(l)
        else:
            o_ref[...] = acc_sc[...]

    in_specs = [pl.BlockSpec((T, dk), lambda h, i: (i, h)), pl.BlockSpec((S, dk), lambda h, i: (0, h)),
                pl.BlockSpec((S, vw), lambda h, i: (0, h))]
    o_spec = pl.BlockSpec((T, dv), lambda h, i: (i, h))
    if softmax:
        return _call(body, name, [_sds((S, heads * dv), F32), _sds((S, heads * LANES), F32)], (heads, nq), in_specs,
                     [o_spec, pl.BlockSpec((T, LANES), lambda h, i: (i, h))],
                     scratch=[pltpu.VMEM((T, LANES), F32), pltpu.VMEM((T, vw), F32)],
                     sem=("parallel", "arbitrary"), side=side)(q, k, v)
    lgrow, rel, reld = tables
    in_specs += [pl.BlockSpec((1, 1, LANES), lambda h, i: (h, 0, 0)), pl.BlockSpec((1, T, T), lambda h, i: (h, 0, 0)),
                 pl.BlockSpec((1, T, T), lambda h, i: (h, 0, 0))]
    return _call(body, name, _sds((S, heads * dv), F32), (heads, nq), in_specs, o_spec,
                 scratch=[pltpu.VMEM((T, dv), F32)], sem=("parallel", "arbitrary"), side=side)(q, k, v, lgrow, rel, reld)


def _attn_bwd(q, k, v, do, heads, dk, dv, softmax, name, o=None, lse=None, tables=None, side=None):
    S = q.shape[0]
    T = ATT_BLOCK
    nq = S // T
    rep = T // LANES

    def body(*refs):
        if softmax:
            q_ref, k_ref, v_ref, do_ref, o_ref, lse_ref, dq_ref, dk_ref, dv_ref, dq_sc = refs
        else:
            q_ref, k_ref, v_ref, do_ref, lg_ref, rel_ref, reld_ref, dq_ref, dk_ref, dv_ref, dq_sc = refs
        i = pl.program_id(1)

        @pl.when(i == 0)
        def _():
            dk_ref[...] = jnp.zeros_like(dk_ref)
            dv_ref[...] = jnp.zeros_like(dv_ref)

        qv = q_ref[...]
        dof = do_ref[...].astype(F32)
        dov = dof.astype(BF16)
        if softmax:
            delta = jnp.sum(dof * o_ref[...], axis=-1, keepdims=True)
            lse_t = jnp.tile(lse_ref[...], (1, rep))
        dq_sc[...] = jnp.zeros_like(dq_sc)

        def products(j):
            rows = pl.ds(pl.multiple_of(j * T, T), T)
            kb = k_ref[rows, :]
            return rows, kb, _dot_nt(qv, kb), _dot_nt(dov, v_ref[rows, :])

        def block(j, diagonal, rows, kb, s, dp):
            if softmax:
                if diagonal:
                    s = jnp.where(_chunk_mask(T), s, NEG)
                p = jnp.exp(s - lse_t)
                ds = p * (dp - delta)
            else:
                if diagonal:
                    dec = reld_ref[0]
                else:
                    fac = jnp.exp(lg_ref[0] * ((i - j) * T).astype(F32))
                    dec = rel_ref[0] * jnp.tile(fac, (1, rep))
                p = s * dec
                ds = dp * dec
            dsb = ds.astype(BF16)
            dv_ref[rows, :] += _dot_tn(p.astype(BF16), dov)
            dk_ref[rows, :] += _dot_tn(dsb, qv)
            dq_sc[...] += jnp.dot(dsb, kb, preferred_element_type=F32)

        block(i, True, *products(i))

        def pair(jj, carry):
            first, second = products(2 * jj), products(2 * jj + 1)
            block(2 * jj, False, *first)
            block(2 * jj + 1, False, *second)
            return carry

        lax.fori_loop(0, i // 2, pair, 0)

        @pl.when(i % 2 == 1)
        def _():
            block(i - 1, False, *products(i - 1))

        dq_ref[...] = dq_sc[...]

    qspec = pl.BlockSpec((T, dk), lambda h, i: (i, h))
    kspec = pl.BlockSpec((S, dk), lambda h, i: (0, h))
    vspec = pl.BlockSpec((S, dv), lambda h, i: (0, h))
    dospec = pl.BlockSpec((T, dv), lambda h, i: (i, h))
    in_specs = [qspec, kspec, vspec, dospec]
    args = [q, k, v, do]
    if softmax:
        in_specs[2] = pl.BlockSpec((S, dv), lambda h, i: (0, 2 * h))
        in_specs += [dospec, pl.BlockSpec((T, LANES), lambda h, i: (i, h))]
        args += [o, lse]
    else:
        in_specs += [pl.BlockSpec((1, 1, LANES), lambda h, i: (h, 0, 0)),
                     pl.BlockSpec((1, T, T), lambda h, i: (h, 0, 0)), pl.BlockSpec((1, T, T), lambda h, i: (h, 0, 0))]
        args += list(tables)
    return _call(body, name, [_sds((S, heads * dk), F32), _sds((S, heads * dk), F32), _sds((S, heads * dv), F32)],
                 (heads, nq), in_specs, [qspec, kspec, vspec], scratch=[pltpu.VMEM((T, dk), F32)],
                 sem=("parallel", "arbitrary"), side=side)(*args)


def _rope_tables(pos):
    def tables(dim):
        inv_freq = ROPE_THETA ** (-jnp.arange(0, dim, 2, dtype=F32) / dim)
        ang = pos.astype(F32)[:, None] * inv_freq
        return jnp.cos(ang), jnp.sin(ang)

    cm, sm = tables(ROPE)
    S = pos.shape[0]
    z32, z64 = jnp.zeros((S, 32), F32), jnp.zeros((S, 64), F32)
    cr, sr = tables(RET_DK)
    return (jnp.concatenate([cm, cm, z64], 1), jnp.concatenate([z32, sm, z64], 1),
            jnp.concatenate([-sm, z32, z64], 1), cr, sr)


def _row(v):
    return v.reshape(1, -1).astype(F32)


def _local_step(x, pos, target, pipe, P):
    tabs = _rope_tables(pos)
    dtabs = _decay_tables(ATT_BLOCK)
    xf, xb = pipe.run(_ln_fwd, "ln_in", [x], [1.0], _row(P["ln_in_g"]), _row(P["ln_in_b"]), want_z=False)
    saved = []
    for l in range(DEPTH):
        w = functools.partial(pipe.weight, l)
        t = f"_l{l}"
        h = pipe.run(_matmul, "mm_h" + t, xb, w("w_in"))
        qn, kvn, kr, rq, rk, rv = _prep1(h, tabs, _row(P["q_norm_g"][l]), _row(P["kv_norm_g"][l]), "prep1" + t)
        q = _matmul(qn, w("w_uq"), "mm_q" + t)
        kv = _matmul(kvn, w("w_ukv"), "mm_kv" + t)
        qm, km, vm = _prep2(q, kv, kr, tabs, "prep2" + t)
        a, lse = pipe.run(_attn_fwd, "mla_fwd" + t, qm, km, vm, MLA_HEADS, HEAD_PAD, VDIM, True)
        o = pipe.run(_attn_fwd, "ret_fwd" + t, rq, rk, rv, RET_HEADS, RET_DK, RET_DV, False, tables=dtabs)
        mixin = _gn_gate(a, o, h, _row(P["ret_gn_g"][l]), _row(P["ret_gn_b"][l]), "gn_gate" + t)
        z1, x1f, x1b = _matmul_ln(mixin, w("w_out"), xf, _row(P["ln1_g"][l]), _row(P["ln1_b"][l]), "mm_mix_ln1" + t)
        gu, act = pipe.run(_matmul_swiglu, "mm_gu" + t, x1b, w("w_gu"))
        f = pipe.run(_matmul, "mm_down" + t, act, w("w_down"))
        g2, b2 = _row(P["ln2_g"][l]), _row(P["ln2_b"][l])
        saved.append(dict(xb=xb, h=h, qn=qn, kvn=kvn, rq=rq, rk=rk, rv=rv, qm=qm, km=km, vm=vm, a=a, lse=lse, o=o,
                          mixin=mixin, z1=z1, x1b=x1b, gu=gu, act=act))
        if l == DEPTH - 1:
            saved[l]["z2"], dy, sqerr = _ln_loss(x1f, f, g2, b2, target, "ln2_loss" + t)
        else:
            saved[l]["z2"], xf, xb = _ln_fwd([x1f, f], [ALPHA, 1.0], g2, b2, "ln2" + t, True)

    dP = {}
    dys, coefs = [dy], [1.0]
    for l in reversed(range(DEPTH)):
        w, sv = functools.partial(pipe.weight, l), saved[l]
        t = f"_l{l}"
        dz2, dz2b, dg, db = _ln_bwd(dys, coefs, sv["z2"], _row(P["ln2_g"][l]), "ln2_bwd" + t)
        dP[("ln2_g", l)], dP[("ln2_b", l)] = dg, db
        pipe.reduce(l, w_down=pipe.run(_matmul, "mm_dw_down" + t, sv["act"], dz2b, ta=True, out_dtype=BF16))
        dgu = pipe.run(_matmul_swiglu_bwd, "mm_dact" + t, dz2b, w("w_down"), sv["gu"])
        pipe.reduce(l, w_gu=pipe.run(_matmul, "mm_dw_gu" + t, sv["x1b"], dgu, ta=True, out_dtype=BF16))
        dx1 = pipe.run(_matmul, "mm_dx1" + t, dgu, w("w_gu"), tb=True)
        dz1, dz1b, dg, db = _ln_bwd([dz2, dx1], [ALPHA, 1.0], sv["z1"], _row(P["ln1_g"][l]), "ln1_bwd" + t)
        dP[("ln1_g", l)], dP[("ln1_b", l)] = dg, db
        pipe.reduce(l, w_out=_matmul(sv["mixin"], dz1b, "mm_dw_out" + t, ta=True, out_dtype=BF16))
        dmixin = pipe.run(_matmul, "mm_dmixin" + t, dz1b, w("w_out"), tb=True)
        do, drg, dgg, dgb = _gn_gate_bwd(dmixin, sv["o"], sv["h"], _row(P["ret_gn_g"][l]), _row(P["ret_gn_b"][l]),
                                         "gn_gate_bwd" + t)
        dP[("ret_gn_g", l)], dP[("ret_gn_b", l)] = dgg, dgb
        drq, drk, drv = pipe.run(_attn_bwd, "ret_bwd" + t, sv["rq"], sv["rk"], sv["rv"], do, RET_HEADS, RET_DK, RET_DV,
                                 False, tables=dtabs)
        dqm, dkm, dvm = pipe.run(_attn_bwd, "mla_bwd" + t, sv["qm"], sv["km"], sv["vm"], dmixin, MLA_HEADS, HEAD_PAD,
                                 VDIM, True, o=sv["a"], lse=sv["lse"])
        dq, dkv, dkr = _prep2_bwd(dqm, dkm, dvm, tabs, "prep2_bwd" + t)
        g_uq = _matmul(sv["qn"], dq, "mm_dw_uq" + t, ta=True, out_dtype=BF16)
        dqn = _matmul(dq, w("w_uq"), "mm_dqn" + t, tb=True)
        g_ukv = _matmul(sv["kvn"], dkv, "mm_dw_ukv" + t, ta=True, out_dtype=BF16)
        dkvn = _matmul(dkv, w("w_ukv"), "mm_dkvn" + t, tb=True)
        dh, dqg, dkvg = _prep1_bwd(dqn, dkvn, dkr, drq, drk, drv, drg, sv["h"], tabs, _row(P["q_norm_g"][l]),
                                   _row(P["kv_norm_g"][l]), "prep1_bwd" + t)
        dP[("q_norm_g", l)], dP[("kv_norm_g", l)] = dqg, dkvg
        pipe.reduce(l, w_uq=g_uq, w_ukv=g_ukv,
                    w_in=pipe.run(_matmul, "mm_dw_in" + t, sv["xb"], dh, ta=True, out_dtype=BF16))
        dxl = pipe.run(_matmul, "mm_dxl" + t, dh, w("w_in"), tb=True)
        dys, coefs = [dz1, dxl], [ALPHA, 1.0]
    grad_x, _, dg, db = _ln_bwd(dys, coefs, x, _row(P["ln_in_g"]), "ln_in_bwd")
    dP[("ln_in_g", None)], dP[("ln_in_b", None)] = dg, db
    return sqerr, grad_x, dP


INTERNAL_OF = {"w_in": ("w_in",), "w_uq": ("w_uq",), "w_ukv": ("w_ukv",), "w_out": ("w_out",),
               "w_gu": ("w_gate", "w_up"), "w_down": ("w_down",)}
ROW_PIECES = {"w_up": 1024}


def _internal_weight(name, *blocks):
    cat = lambda parts: jnp.concatenate(parts, axis=1)
    cols = lambda b: cat([b[j] for j in range(N_CHIPS)])
    b = blocks[0]
    if name in ("w_out", "w_down"):
        return b.reshape(-1, b.shape[-1])
    if name == "w_gu":
        return cat([blk[j] for j in range(N_CHIPS) for blk in blocks])
    if name == "w_in":
        return cat([b[0][:, :MLA_IN_USED], jnp.zeros((D_MODEL, MLA_IN - MLA_IN_USED), BF16), b[0][:, MLA_IN_USED:]]
                   + [b[j] for j in range(1, N_CHIPS)])
    if name == "w_uq":
        uq, hw = cols(b), NOPE + ROPE
        pad = jnp.zeros((Q_LORA, HEAD_PAD - hw), BF16)
        return cat([p for h in range(MLA_HEADS) for p in (uq[:, h * hw:(h + 1) * hw], pad)])
    ukv = cols(b)
    return cat([ukv[:, 256 * h:256 * h + NOPE] for h in range(MLA_HEADS)]
               + [ukv[:, 256 * h + NOPE:256 * h + 256] for h in range(MLA_HEADS)])


def _grad_shards(name, g):
    cat = lambda parts: jnp.concatenate(parts, axis=1)
    if name in ("w_out", "w_down"):
        return {name: g.reshape(N_CHIPS, -1, g.shape[-1])}
    if name == "w_gu":
        return {"w_gate": _ColBlocks(g, 0), "w_up": _ColBlocks(g, 1)}
    if name == "w_in":
        ci, shift = BIG_SHARD["w_in"][1], MLA_IN - MLA_IN_USED
        return {name: [cat([g[:, :MLA_IN_USED], g[:, MLA_IN:ci + shift]])]
                + [g[:, ci * j + shift:ci * (j + 1) + shift] for j in range(1, N_CHIPS)]}
    if name == "w_uq":
        cq = NOPE + ROPE
        return {name: [cat([g[:, HEAD_PAD * h:HEAD_PAD * h + cq] for h in (2 * j, 2 * j + 1)]) for j in range(N_CHIPS)]}
    return {name: [cat([g[:, o + NOPE * h:o + NOPE * (h + 1)] for h in (2 * j, 2 * j + 1) for o in (0, MLA_HEADS * NOPE)])
                   for j in range(N_CHIPS)]}


def _small_layout(P):
    out, at = {}, 0
    for n in SMALL:
        out[n] = (at, P[n].size)
        at += P[n].size
    return out, at


def _flatten_small(P, last):
    v = jnp.concatenate([P[n].reshape(-1).astype(F32) for n in SMALL] + [last.reshape(-1).astype(F32)])
    return jnp.pad(v, (0, SMALL_ROWS * FLAT_W - v.size)).reshape(SMALL_ROWS, FLAT_W)


def _place():
    return lax.axis_index("x"), lax.axis_index("y"), lax.axis_index("c")


def _other_chips(x, y):
    return [(1 - x, y), (x, 1 - y), (1 - x, 1 - y)]


def _rcopy(src, dst, ssem, rsem, dev):
    return pltpu.make_async_remote_copy(src_ref=src, dst_ref=dst, send_sem=ssem, recv_sem=rsem, device_id=dev,
                                        device_id_type=MESH)


def _comm_call(body, name, out_shape, n_in, scratch):
    many = isinstance(out_shape, (list, tuple))
    return pl.pallas_call(body, name=name, out_shape=out_shape, in_specs=[HBM] * n_in,
                          out_specs=[HBM] * len(out_shape) if many else HBM, scratch_shapes=scratch)


def _half(ref, which):
    rows = ref.shape[0] // 2
    return ref.at[pl.ds(pl.multiple_of(which * rows, 16), rows)]


def _dma_sems(n):
    return pltpu.SemaphoreType.DMA((n,))


def _allgather_side(ws):
    k = len(ws)

    def peers():
        x, y, c = _place()
        return c, 2 * x + y, (x, y, 1 - c), [(n, t, cx, cy) for n in range(k) for t, (cx, cy) in enumerate(_other_chips(x, y))]

    def outgoing(w_refs, g_refs, sems):
        ssem, rsem, _, _, ossem, orsem = sems
        c, j, sib, nt = peers()
        owns = [_rcopy(w_refs[n], g_refs[n].at[j], ossem.at[n], orsem.at[n], sib) for n in range(k)]
        sends = [_rcopy(_half(w_refs[n], c), _half(g_refs[n].at[j], c), ssem.at[3 * n + t], rsem.at[3 * n + t],
                        (cx, cy, c)) for n, t, cx, cy in nt]
        return owns, sends

    def incoming(g_refs, sems):
        ssem, rsem, fssem, frsem, _, _ = sems
        c, _, sib, nt = peers()
        landed, passed, relayed = [], [], []
        for n, t, cx, cy in nt:
            mine, other = (_half(g_refs[n].at[2 * cx + cy], h) for h in (c, 1 - c))
            landed.append(_rcopy(mine, mine, ssem.at[3 * n + t], rsem.at[3 * n + t], (cx, cy, c)))
            passed.append(_rcopy(mine, mine, fssem.at[3 * n + t], frsem.at[3 * n + t], sib))
            relayed.append(_rcopy(other, other, fssem.at[3 * n + t], frsem.at[3 * n + t], sib))
        return landed, passed, relayed

    def start(w_refs, g_refs, sems):
        owns, sends = outgoing(w_refs, g_refs, sems)
        for cp in sends + owns:
            cp.start()

    def finish(w_refs, g_refs, sems):
        owns, sends = outgoing(w_refs, g_refs, sems)
        landed, passed, relayed = incoming(g_refs, sems)
        for got, on in zip(landed, passed):
            got.wait_recv()
            on.start()
        for cp in relayed:
            cp.wait_recv()
        for cp in owns:
            cp.wait()
        for cp in sends + passed:
            cp.wait_send()

    return _Side(list(ws), [_sds((N_CHIPS,) + w.shape, w.dtype) for w in ws],
                 [_dma_sems(3 * k)] * 4 + [_dma_sems(k)] * 2, start, finish)


def _exchange_side(parts):
    k = len(parts)

    def copies(p_refs, rcv_refs, sems):
        ssem, rsem = sems
        x, y, c = _place()
        return [_rcopy(p_refs[n].at[2 * cx + cy], rcv_refs[n].at[t], ssem.at[3 * n + t], rsem.at[3 * n + t], (cx, cy, c))
                for n in range(k) for t, (cx, cy) in enumerate(_other_chips(x, y))]

    def start(p_refs, rcv_refs, sems):
        for cp in copies(p_refs, rcv_refs, sems):
            cp.start()

    def finish(p_refs, rcv_refs, sems):
        for cp in copies(p_refs, rcv_refs, sems):
            cp.wait()

    return _Side(list(parts), [_sds((3,) + p.shape[1:], p.dtype) for p in parts], [_dma_sems(3 * k)] * 2, start, finish)


def _run_side(side, name):
    k_in, k_out = len(side.arrays), len(side.out_shape)

    def body(*refs):
        parts = refs[:k_in], refs[k_in:k_in + k_out], refs[k_in + k_out:]
        side.start(*parts)
        side.finish(*parts)

    return _comm_call(body, name, list(side.out_shape), k_in, list(side.scratch))(*side.arrays)


def _sibling_side(arrays, out_shape, n_copies, copies):
    def start(in_refs, out_refs, sems):
        for cp in copies(in_refs, out_refs, sems):
            cp.start()

    def finish(in_refs, out_refs, sems):
        for cp in copies(in_refs, out_refs, sems):
            cp.wait()

    return _Side(list(arrays), out_shape, [_dma_sems(n_copies)] * 2, start, finish)


class _ColBlocks:
    def __init__(self, array, off):
        self.array, self.off, self.dtype = array, off, array.dtype
        self.shape = (N_CHIPS, array.shape[0], GU_BLOCK)

    def block(self, ref, jj):
        return ref.at[:, pl.ds((2 * jj + self.off) * GU_BLOCK, GU_BLOCK)]


def _swap_side(gds):
    k = len(gds)

    def copies(gd_refs, out_refs, sems):
        ssem, rsem = sems
        x, y, c = _place()
        blocks = [[g.block(gd_refs[n], jj) if isinstance(g, _ColBlocks) else gd_refs[n].at[jj] for jj in range(N_CHIPS)]
                  for n, g in enumerate(gds)]
        return [_rcopy(_half(blocks[n][jj], 1 - c), out_refs[n].at[jj], ssem.at[N_CHIPS * n + jj],
                       rsem.at[N_CHIPS * n + jj], (x, y, 1 - c)) for n in range(k) for jj in range(N_CHIPS)]

    return _sibling_side([g.array if isinstance(g, _ColBlocks) else g for g in gds],
                         [_sds((N_CHIPS, g.shape[1] // 2, g.shape[2]), g.dtype) for g in gds], N_CHIPS * k, copies)


def _share_side(reds):
    k = len(reds)

    def copies(r_refs, out_refs, sems):
        ssem, rsem = sems
        x, y, c = _place()
        return [_rcopy(r_refs[n], out_refs[n], ssem.at[n], rsem.at[n], (x, y, 1 - c)) for n in range(k)]

    return _sibling_side(reds, [_sds(r.shape, r.dtype) for r in reds], k, copies)


def _join_sides(sides):
    if len(sides) == 1:
        return sides[0]
    cuts = [(len(s.arrays), len(s.out_shape), len(s.scratch)) for s in sides]

    def each(method, in_refs, out_refs, sems):
        a = o = m = 0
        for s, (ka, ko, km) in zip(sides, cuts):
            getattr(s, method)(in_refs[a:a + ka], out_refs[o:o + ko], sems[m:m + km])
            a, o, m = a + ka, o + ko, m + km

    return _Side([x for s in sides for x in s.arrays], [x for s in sides for x in s.out_shape],
                 [x for s in sides for x in s.scratch], functools.partial(each, "start"), functools.partial(each, "finish"))


def _allreduce_small(small):
    def body(s_ref, all_ref, sssem, srsem, lsem):
        x, y, c = _place()
        me = 4 * x + 2 * y + c
        own = pltpu.make_async_copy(s_ref, all_ref.at[me], lsem)
        own.start()
        cps = []
        for r in range(1, 8):
            fx, fy, fc = (r >> 2) & 1, (r >> 1) & 1, r & 1
            px, py, pc = (1 - x if fx else x, 1 - y if fy else y, 1 - c if fc else c)
            peer = 4 * px + 2 * py + pc
            send = _rcopy(s_ref, all_ref.at[me], sssem.at[r - 1], srsem.at[me], (px, py, pc))
            send.start()
            cps.append((send, _rcopy(s_ref, all_ref.at[peer], sssem.at[r - 1], srsem.at[peer], (px, py, pc))))
        for send, recv in cps:
            send.wait_send()
            recv.wait_recv()
        own.wait()

    return _comm_call(body, "allreduce_small", [_sds((8,) + small.shape, small.dtype)], 1,
                      [pltpu.SemaphoreType.DMA((7,)), pltpu.SemaphoreType.DMA((8,)), pltpu.SemaphoreType.DMA(())])(small)[0]


def _add_pair(gd, got, c, name):
    _, R, W = got.shape
    tm = _pick(R, (512, 256, 128, 64))
    nb = R // tm

    def body(c_ref, a_ref, b_ref, o_ref):
        o_ref[...] = (a_ref[...].astype(F32) + b_ref[...].astype(F32)).astype(o_ref.dtype)

    if isinstance(gd, _ColBlocks):
        off = gd.off
        own = pl.BlockSpec((tm, W), lambda j, i, c_ref: (c_ref[0] * nb + i, 2 * j + off))
        gd = gd.array
    else:
        own = pl.BlockSpec((None, tm, W), lambda j, i, c_ref: (j, c_ref[0] * nb + i, 0))
    grid_spec = pltpu.PrefetchScalarGridSpec(
        num_scalar_prefetch=1, grid=(N_CHIPS, nb),
        in_specs=[own, pl.BlockSpec((None, tm, W), lambda j, i, c_ref: (j, i, 0))],
        out_specs=pl.BlockSpec((None, tm, W), lambda j, i, c_ref: (j, i, 0)))
    return pl.pallas_call(body, name=name, grid_spec=grid_spec, out_shape=_sds((N_CHIPS, R, W), gd.dtype),
                          compiler_params=pltpu.CompilerParams(dimension_semantics=("parallel", "parallel"),
                                                               vmem_limit_bytes=VMEM_LIMIT))(c, gd, got)


def _add_chips(part, rcv, j, name):
    _, R, W = part.shape
    tm = _pick(R, (512, 256, 128, 64))

    def body(j_ref, p_ref, r0_ref, r1_ref, r2_ref, o_ref):
        o_ref[...] = ((p_ref[...].astype(F32) + r0_ref[...].astype(F32)) + r1_ref[...].astype(F32)) + r2_ref[...].astype(F32)

    def slot(t):
        return pl.BlockSpec((None, tm, W), lambda i, j_ref: (t, i, 0))

    grid_spec = pltpu.PrefetchScalarGridSpec(
        num_scalar_prefetch=1, grid=(R // tm,),
        in_specs=[pl.BlockSpec((None, tm, W), lambda i, j_ref: (j_ref[0], i, 0)), slot(0), slot(1), slot(2)],
        out_specs=pl.BlockSpec((tm, W), lambda i, j_ref: (i, 0)))
    return pl.pallas_call(body, name=name, grid_spec=grid_spec, out_shape=_sds((R, W), F32),
                          compiler_params=pltpu.CompilerParams(dimension_semantics=("parallel",),
                                                               vmem_limit_bytes=VMEM_LIMIT))(j, part, rcv, rcv, rcv)


def _sum_small(allsmall):
    _, R, W = allsmall.shape

    def body(a_ref, o_ref):
        acc = a_ref[0]
        for d in range(1, 8):
            acc = acc + a_ref[d]
        o_ref[...] = acc

    return _call(body, "sum_small", _sds((R, W), F32), (1,), [_whole((8, R, W))], _whole((R, W)),
                 sem=("arbitrary",))(allsmall)


def _adamw(w, g, m, v, name):
    R, C = w.shape
    tm = _pick(R, (256, 128, 64, 32, 8))

    def body(w_ref, g_ref, m_ref, v_ref, d_ref, mo_ref, vo_ref):
        gv = g_ref[...]
        mn = ADAM_B1 * m_ref[...] + (1.0 - ADAM_B1) * gv
        vn = ADAM_B2 * v_ref[...] + (1.0 - ADAM_B2) * (gv * gv)
        m_hat = mn / (1.0 - ADAM_B1 ** ADAM_STEP)
        v_hat = vn / (1.0 - ADAM_B2 ** ADAM_STEP)
        d_ref[...] = -ADAM_LR * (m_hat / (jnp.sqrt(v_hat) + ADAM_EPS) + ADAM_WD * w_ref[...])
        mo_ref[...] = mn
        vo_ref[...] = vn

    spec = _rows(tm, C)
    return _call(body, name, [_sds((R, C), F32)] * 3, (R // tm,), [spec] * 4, [spec] * 3, sem=("parallel",))(w, g, m, v)


def _adamw_layer(c, w, m, v, mine, other, l, prev, name):
    _, R, C = w.shape
    half = R // 2
    tm = _pick(half, (256, 128, 64))
    nbh = half // tm

    def body(c_ref, w_ref, m_ref, v_ref, a_ref, b_ref, *rest):
        g_ref, d_ref, mo_ref, vo_ref = rest[-4:]
        gv = jnp.where(pl.program_id(0) // nbh == c_ref[0], a_ref[...], b_ref[...])
        mn = ADAM_B1 * m_ref[...] + (1.0 - ADAM_B1) * gv
        vn = ADAM_B2 * v_ref[...] + (1.0 - ADAM_B2) * (gv * gv)
        m_hat = mn / (1.0 - ADAM_B1 ** ADAM_STEP)
        v_hat = vn / (1.0 - ADAM_B2 ** ADAM_STEP)
        g_ref[...] = gv
        d_ref[...] = -ADAM_LR * (m_hat / (jnp.sqrt(v_hat) + ADAM_EPS) + ADAM_WD * w_ref[...])
        mo_ref[...] = mn
        vo_ref[...] = vn

    layer = pl.BlockSpec((None, tm, C), lambda i, c_ref: (l, i, 0))
    halfspec = pl.BlockSpec((tm, C), lambda i, c_ref: (i % nbh, 0))
    n_prev = 0 if prev is None else 4
    grid_spec = pltpu.PrefetchScalarGridSpec(
        num_scalar_prefetch=1, grid=(R // tm,),
        in_specs=[layer] * 3 + [halfspec] * 2 + [pl.BlockSpec(memory_space=pl.ANY)] * n_prev,
        out_specs=[layer] * 4)
    return pl.pallas_call(body, name=name, grid_spec=grid_spec, out_shape=[_sds(w.shape, F32)] * 4,
                          input_output_aliases={6 + k: k for k in range(n_prev)},
                          compiler_params=pltpu.CompilerParams(dimension_semantics=("parallel",),
                                                               vmem_limit_bytes=VMEM_LIMIT))(
        c, w, m, v, mine, other, *(prev or ()))


FIRST_GATHER = ("w_in", "w_uq", "w_ukv")
G_DOWN, G_GU, G_OUT, G_IN = ("w_down",), ("w_gate", "w_up"), ("w_out",), ("w_uq", "w_ukv", "w_in")


def _backward_jobs(l):
    t = f"_l{l}"
    return {"mm_dact" + t: [("swap", l, G_DOWN)], "mm_dw_gu" + t: [("exchange", l, G_DOWN)],
            "mm_dx1" + t: [("swap", l, G_GU), ("share", l, G_DOWN)], "mm_dmixin" + t: [("swap", l, G_OUT)],
            "ret_bwd" + t: [("exchange", l, ("w_gate",))],
            "mla_bwd" + t: [("exchange", l, ("w_up", "w_out")), ("share", l, ("w_gate",))],
            "mm_dw_in" + t: [("share", l, ("w_up", "w_out"))]}


JOBS = {
    "ln_in": [("gather", 0, FIRST_GATHER)],
    "mm_h_l0": [("gather", 0, ("w_up@a",))], "mla_fwd_l0": [("gather", 0, ("w_gate", "w_out"))],
    "ret_fwd_l0": [("gather", 0, ("w_up@b",))],
    "mm_gu_l0": [("gather", 0, ("w_down",)), ("gather", 1, ("w_uq", "w_ukv"))],
    "mm_down_l0": [("gather", 1, ("w_in",))], "mm_h_l1": [("gather", 1, ("w_up@a",))],
    "mla_fwd_l1": [("gather", 1, ("w_gate", "w_out"))], "ret_fwd_l1": [("gather", 1, ("w_up@b",))],
    "mm_gu_l1": [("gather", 1, ("w_down",))],
    **_backward_jobs(1), **_backward_jobs(0),
    "mm_dxl_l1": [("swap", 1, G_IN)],
    "mm_dx1_l0": [("swap", 0, G_GU), ("share", 0, G_DOWN), ("exchange", 1, G_IN)],
    "ret_bwd_l0": [("exchange", 0, ("w_gate",)), ("share", 1, G_IN)], "mm_dxl_l0": [("exchange", 0, G_IN)]}
PLANNED = {job for jobs in JOBS.values() for job in jobs}


class _Pipeline:
    def __init__(self, own, Wt, Mo, Vo, core, chip):
        self.own, self.Wt, self.Mo, self.Vo, self.core, self.chip = own, Wt, Mo, Vo, core, chip
        self.blocks, self.whole, self.gds, self.parts, self.reds = {}, {}, {}, {}, {}
        self.results = {n: None for n in BIG}

    def _gathered(self, l, n):
        if n in ROW_PIECES:
            return jnp.concatenate([self.blocks[(l, n + "@a")], self.blocks[(l, n + "@b")]], axis=1)
        return self.blocks[(l, n)]

    def weight(self, l, name):
        if (l, name) not in self.whole:
            self.whole[(l, name)] = _internal_weight(name, *[self._gathered(l, n) for n in INTERNAL_OF[name]])
        return self.whole[(l, name)]

    def run(self, fn, name, *args, **kw):
        jobs = JOBS.get(name, ())
        if not jobs:
            return fn(*args, name=name, **kw)
        sides = [self._side(*job) for job in jobs]
        out, res = fn(*args, name=name, side=_join_sides(sides), **kw)
        for job, side in zip(jobs, sides):
            k = len(side.out_shape)
            self._done(*job, res[:k])
            res = res[k:]
        return out

    def reduce(self, l, **grads):
        shards = {}
        for name, g in grads.items():
            shards.update(_grad_shards(name, g))
        for n, sh in shards.items():
            self.gds[(l, n)] = sh if hasattr(sh, "shape") else jnp.stack(sh)
        self._alone("swap", l, tuple(shards))

    def _alone(self, kind, l, names):
        if (kind, l, names) not in PLANNED:
            self._done(kind, l, names, _run_side(self._side(kind, l, names), f"{kind}_{names[0]}_l{l}"))

    def _side(self, kind, l, names):
        if kind == "gather":
            return _allgather_side([self.own[l][n] for n in names])
        store = {"swap": self.gds, "exchange": self.parts, "share": self.reds}[kind]
        make = {"swap": _swap_side, "exchange": _exchange_side, "share": _share_side}[kind]
        return make([store[(l, n)] for n in names])

    def _done(self, kind, l, names, res):
        for n, r in zip(names, res):
            if kind == "gather":
                self.blocks[(l, n)] = r
            elif kind == "swap":
                self.parts[(l, n)] = _add_pair(self.gds[(l, n)], r, self.core, f"add_pair_{n}_l{l}")
            elif kind == "exchange":
                self.reds[(l, n)] = _add_chips(self.parts[(l, n)], r, self.chip, f"add_chips_{n}_l{l}")
            else:
                self.results[n] = _adamw_layer(self.core, self.Wt[n], self.Mo[n], self.Vo[n], self.reds[(l, n)], r, l,
                                               self.results[n], f"adamw_{n}_l{l}")
        if kind == "exchange":
            self._alone("share", l, names)


def kernel(x, positions, ln_in_g, ln_in_b, w_in, q_norm_g, kv_norm_g, w_uq, w_ukv, ret_gn_g, ret_gn_b, w_out, ln1_g, ln1_b, w_gate, w_up, w_down, ln2_g, ln2_b, loss_target, m_ln_in_g, m_ln_in_b, m_w_in, m_q_norm_g, m_kv_norm_g, m_w_uq, m_w_ukv, m_ret_gn_g, m_ret_gn_b, m_w_out, m_ln1_g, m_ln1_b, m_w_gate, m_w_up, m_w_down, m_ln2_g, m_ln2_b, v_ln_in_g, v_ln_in_b, v_w_in, v_q_norm_g, v_kv_norm_g, v_w_uq, v_w_ukv, v_ret_gn_g, v_ret_gn_b, v_w_out, v_ln1_g, v_ln1_b, v_w_gate, v_w_up, v_w_down, v_ln2_g, v_ln2_b):
    given = dict(locals())
    Wt = {n: given[n] for n in WEIGHTS}
    Mo = {n: given["m_" + n] for n in WEIGHTS}
    Vo = {n: given["v_" + n] for n in WEIGHTS}
    cx, cy, cc = _place()
    chip = (2 * cx + cy).astype(jnp.int32)
    core = cc.astype(jnp.int32)

    own = [{n: Wt[n][l].astype(BF16) for n in BIG} for l in range(DEPTH)]
    for shard in own:
        for n, at in ROW_PIECES.items():
            shard[n + "@a"], shard[n + "@b"] = shard[n][:at], shard[n][at:]
    pipe = _Pipeline(own, Wt, Mo, Vo, core.reshape(1), chip.reshape(1))
    sqerr, grad_x, dP = _local_step(x[0], positions[0], loss_target[0], pipe, Wt)
    results = pipe.results

    small_g = {n: (dP[(n, None)] if Wt[n].ndim == 1 else jnp.stack([dP[(n, l)] for l in range(DEPTH)])) for n in SMALL}
    local_loss = 0.5 * jnp.sum(sqerr) / D_MODEL
    small_sum = _sum_small(_allreduce_small(_flatten_small(small_g, local_loss))).reshape(-1)
    layout, n_small = _small_layout(Wt)
    loss = small_sum[n_small]

    grads, deltas, new_m, new_v = {}, {}, {}, {}
    for n in BIG:
        grads[n], deltas[n], new_m[n], new_v[n] = results[n]
    zero = jnp.zeros((), F32)
    d, mn, vn = _adamw(_flatten_small(Wt, zero), small_sum.reshape(SMALL_ROWS, FLAT_W), _flatten_small(Mo, zero),
                       _flatten_small(Vo, zero), "adamw_small")
    for n in SMALL:
        at, size = layout[n]
        pick = lambda a: a.reshape(-1)[at:at + size].reshape(Wt[n].shape)
        grads[n], deltas[n], new_m[n], new_v[n] = pick(small_sum), pick(d), pick(mn), pick(vn)

    return (loss, grad_x[None], *[grads[n] for n in WEIGHTS], *[deltas[n] for n in WEIGHTS],
            *[new_m[n] for n in WEIGHTS], *[new_v[n] for n in WEIGHTS])
```
